```python
import math
import jax, jax.numpy as jnp
from jax import lax
import numpy as np

D_MODEL = 2048
BATCH = 8
SEQ = 2048
DEPTH = 1
DEC_BATCH = 128
DEC_SEQ = 8
PAST_LEN = 2048
PAGE_SIZE = 128

N_HEADS = 8
N_KV_HEADS = 2
HEAD_DIM = 128
GROUP = N_HEADS // N_KV_HEADS
D_ATTN = N_HEADS * HEAD_DIM
D_KV = N_KV_HEADS * HEAD_DIM
IDX_HEADS = 8
IDX_DIM = 64
TOPK_MAX = 256
Q_BLOCK = 128
N_BUCKETS = 32
MAX_DISTANCE = 128
D_CONV = D_MODEL // 2
CONV_WIDTH = 31
EPS = 1e-6
NEG = -1e30

SPLITS = (('q', D_ATTN), ('k', D_KV), ('v', D_KV), ('z_attn', D_ATTN),
          ('q_idx', IDX_HEADS * IDX_DIM), ('k_idx', IDX_DIM), ('w_idx', IDX_HEADS),
          ('glu_val', D_CONV), ('glu_gate', D_CONV), ('z_conv', D_CONV),
          ('gate_attn', D_MODEL), ('gate_conv', D_MODEL))
D_IN = sum(w for _, w in SPLITS)

kernel_name = 'dsa_conformer_gated_hybrid_step'


def rms_norm(x, g):
    xf = x.astype(jnp.float32)
    y = xf * lax.rsqrt(jnp.mean(xf * xf, axis=-1, keepdims=True) + EPS)
    return y.astype(x.dtype) * g


def layer_norm(x, g, b):
    xf = x.astype(jnp.float32)
    mu = jnp.mean(xf, axis=-1, keepdims=True)
    var = jnp.mean(jnp.square(xf - mu), axis=-1, keepdims=True)
    return ((xf - mu) * lax.rsqrt(var + EPS)).astype(x.dtype) * g + b


def t5_bucket(dist):
    n = jnp.maximum(dist, 0)
    max_exact = N_BUCKETS // 2
    ratio = jnp.log(jnp.maximum(n, 1).astype(jnp.float32) / max_exact) / math.log(MAX_DISTANCE / max_exact)
    large = jnp.minimum(max_exact + (ratio * (N_BUCKETS - max_exact)).astype(jnp.int32), N_BUCKETS - 1)
    return jnp.where(n < max_exact, n, large)


def split_columns(h):
    out = {}
    off = 0
    for name, w in SPLITS:
        out[name] = h[..., off:off + w]
        off += w
    return out


def dsa_block(q, q_idx, w_idx, q_pos, k, v, k_idx, rel_bias, top_k):
    f32 = jnp.float32
    B, Tq = q.shape[0], q.shape[1]
    L = k.shape[1]
    dots = jnp.einsum('bthd,bsd->bths', q_idx.astype(f32), k_idx.astype(f32)) * (IDX_DIM ** -0.5)
    score = jnp.einsum('bths,bth->bts', jax.nn.relu(dots), w_idx.astype(f32)) * (IDX_HEADS ** -0.5)
    causal = jnp.arange(L, dtype=jnp.int32)[None, :] <= q_pos[:, None]
    score = jnp.where(causal[None], score, -jnp.inf)
    _, sel = lax.top_k(score, top_k)
    valid = sel <= q_pos[None, :, None]
    gather = jax.vmap(lambda arr, idx: arr[idx])
    k_sel = gather(k, sel)
    v_sel = gather(v, sel)
    qg = q.reshape(B, Tq, N_KV_HEADS, GROUP, HEAD_DIM)
    logits = jnp.einsum('btkgd,btjkd->btkgj', qg.astype(f32), k_sel.astype(f32)) * (HEAD_DIM ** -0.5)
    bias = rel_bias.astype(f32)[t5_bucket(q_pos[None, :, None] - sel)]
    bias = bias.reshape(B, Tq, top_k, N_KV_HEADS, GROUP).transpose(0, 1, 3, 4, 2)
    logits = jnp.where(valid[:, :, None, None, :], logits + bias, NEG)
    probs = jax.nn.softmax(logits, axis=-1)
    out = jnp.einsum('btkgj,btjkd->btkgd', probs.astype(v.dtype), v_sel)
    return out.reshape(B, Tq, D_ATTN).astype(q.dtype)


def decoder_layer(x, k_past, v_past, kidx_past, conv_past, ln_g, w_in, conv_w, conv_b,
                  conv_norm_g, conv_norm_b, w_pw, b_pw, w_up_attn, w_up_conv, w_out, rel_bias):
    B, T, _ = x.shape
    P = k_past.shape[1]
    L = P + T
    top_k = min(TOPK_MAX, L // 4)
    c = split_columns(rms_norm(x, ln_g) @ w_in)

    q = c['q'].reshape(B, T, N_HEADS, HEAD_DIM)
    k_new = c['k'].reshape(B, T, N_KV_HEADS, HEAD_DIM)
    v_new = c['v'].reshape(B, T, N_KV_HEADS, HEAD_DIM)
    kidx_new = c['k_idx']
    q_idx = c['q_idx'].reshape(B, T, IDX_HEADS, IDX_DIM)
    w_idx = c['w_idx']
    k_all = jnp.concatenate([k_past, k_new], axis=1)
    v_all = jnp.concatenate([v_past, v_new], axis=1)
    kidx_all = jnp.concatenate([kidx_past, kidx_new], axis=1)
    qb = min(Q_BLOCK, T)
    if T % qb:
        qb = T
    nb = T // qb

    def to_blocks(a):
        return a.reshape(B, nb, qb, *a.shape[2:]).swapaxes(0, 1)

    q_pos = (P + jnp.arange(T, dtype=jnp.int32)).reshape(nb, qb)

    def attend(blk):
        qb_, qi_, wi_, pos_ = blk
        return dsa_block(qb_, qi_, wi_, pos_, k_all, v_all, kidx_all, rel_bias, top_k)

    attn = lax.map(attend, (to_blocks(q), to_blocks(q_idx), to_blocks(w_idx), q_pos))
    attn = attn.swapaxes(0, 1).reshape(B, T, D_ATTN)
    branch_attn = (attn * jax.nn.silu(c['z_attn'])) @ w_up_attn

    u = c['glu_val'] * jax.nn.sigmoid(c['glu_gate'])
    u_pad = jnp.concatenate([conv_past, u], axis=1)
    dw = lax.conv_general_dilated(u_pad, conv_w[:, None, :], window_strides=(1,), padding='VALID',
                                  dimension_numbers=('NWC', 'WIO', 'NWC'),
                                  feature_group_count=D_CONV) + conv_b
    conv_out = jax.nn.silu(layer_norm(dw, conv_norm_g, conv_norm_b)) @ w_pw + b_pw
    branch_conv = (conv_out * jax.nn.silu(c['z_conv'])) @ w_up_conv

    merged = jax.nn.sigmoid(c['gate_attn']) * branch_attn + jax.nn.sigmoid(c['gate_conv']) * branch_conv
    y = x + merged @ w_out
    return y, k_new, v_new, kidx_new, u_pad[:, -(CONV_WIDTH - 1):]


def setup_inputs(seed: int = 0) -> dict:
    key = jax.random.key(seed)
    ks = jax.random.split(key, 24)
    n_pages = PAST_LEN // PAGE_SIZE
    n_used = DEC_BATCH * n_pages
    n_phys = n_used + max(1, n_used // 4)
    nrm = jax.random.normal
    f32 = jnp.float32
    page_table = jax.random.permutation(ks[0], n_phys)[:n_used].astype(jnp.int32).reshape(DEC_BATCH, n_pages)
    return {
        'x_prompt': nrm(ks[1], (BATCH, SEQ, D_MODEL), f32),
        'x_sample': nrm(ks[2], (DEC_BATCH, DEC_SEQ, D_MODEL), f32),
        'cache_k': nrm(ks[3], (DEPTH, n_phys, PAGE_SIZE, N_KV_HEADS, HEAD_DIM), f32),
        'cache_v': nrm(ks[4], (DEPTH, n_phys, PAGE_SIZE, N_KV_HEADS, HEAD_DIM), f32),
        'cache_kidx': nrm(ks[5], (DEPTH, n_phys, PAGE_SIZE, IDX_DIM), f32),
        'state_conv': 0.5 * nrm(ks[6], (DEPTH, DEC_BATCH, CONV_WIDTH - 1, D_CONV), f32),
        'page_table': page_table,
        'ln_g': 1.0 + 0.02 * nrm(ks[7], (DEPTH, D_MODEL), f32),
        'w_in': nrm(ks[8], (DEPTH, D_MODEL, D_IN), f32) * D_MODEL ** -0.5,
        'conv_w': nrm(ks[9], (DEPTH, CONV_WIDTH, D_CONV), f32) * CONV_WIDTH ** -0.5,
        'conv_b': 0.02 * nrm(ks[10], (DEPTH, D_CONV), f32),
        'conv_norm_g': 1.0 + 0.02 * nrm(ks[11], (DEPTH, D_CONV), f32),
        'conv_norm_b': 0.02 * nrm(ks[12], (DEPTH, D_CONV), f32),
        'w_pw': nrm(ks[13], (DEPTH, D_CONV, D_CONV), f32) * D_CONV ** -0.5,
        'b_pw': 0.02 * nrm(ks[14], (DEPTH, D_CONV), f32),
        'w_up_attn': nrm(ks[15], (DEPTH, D_ATTN, D_MODEL), f32) * D_ATTN ** -0.5,
        'w_up_conv': nrm(ks[16], (DEPTH, D_CONV, D_MODEL), f32) * D_CONV ** -0.5,
        'w_out': nrm(ks[17], (DEPTH, D_MODEL, D_MODEL), f32) * D_MODEL ** -0.5,
        'rel_bias': 0.5 * nrm(ks[18], (N_BUCKETS, N_HEADS), f32),
        'final_g': 1.0 + 0.02 * nrm(ks[19], (D_MODEL,), f32),
    }


def reference(x_prompt, x_sample, cache_k, cache_v, cache_kidx, state_conv, page_table,
              ln_g, w_in, conv_w, conv_b, conv_norm_g, conv_norm_b, w_pw, b_pw,
              w_up_attn, w_up_conv, w_out, rel_bias, final_g):
    def gather_pages(pool):
        g = pool[page_table]
        return g.reshape(g.shape[0], g.shape[1] * g.shape[2], *g.shape[3:])

    B = x_prompt.shape[0]
    dt = x_prompt.dtype
    xp, xs = x_prompt, x_sample
    kp_l, vp_l, ip_l, cp_l = [], [], [], []
    ks_l, vs_l, is_l, cs_l = [], [], [], []
    for layer in range(DEPTH):
        params = (ln_g[layer], w_in[layer], conv_w[layer], conv_b[layer], conv_norm_g[layer],
                  conv_norm_b[layer], w_pw[layer], b_pw[layer], w_up_attn[layer],
                  w_up_conv[layer], w_out[layer], rel_bias)
        xp, kp, vp, ip, cp = decoder_layer(
            xp,
            jnp.zeros((B, 0, N_KV_HEADS, HEAD_DIM), dt),
            jnp.zeros((B, 0, N_KV_HEADS, HEAD_DIM), dt),
            jnp.zeros((B, 0, IDX_DIM), dt),
            jnp.zeros((B, CONV_WIDTH - 1, D_CONV), dt),
            *params)
        xs, ks_, vs_, is_, cs_ = decoder_layer(
            xs,
            gather_pages(cache_k[layer]),
            gather_pages(cache_v[layer]),
            gather_pages(cache_kidx[layer]),
            state_conv[layer],
            *params)
        kp_l.append(kp); vp_l.append(vp); ip_l.append(ip); cp_l.append(cp)
        ks_l.append(ks_); vs_l.append(vs_); is_l.append(is_); cs_l.append(cs_)
    y_prompt = rms_norm(xp, final_g)
    y_sample = rms_norm(xs, final_g)
    return (y_prompt, y_sample,
            jnp.stack(kp_l), jnp.stack(vp_l), jnp.stack(ip_l), jnp.stack(cp_l),
            jnp.stack(ks_l), jnp.stack(vs_l), jnp.stack(is_l), jnp.stack(cs_l))
```

```python
import functools
import math

import numpy as np
import jax
import jax.numpy as jnp
from jax import lax
from jax.experimental import pallas as pl
from jax.experimental.pallas import tpu as pltpu

F32 = jnp.float32
BF16 = jnp.bfloat16

D_MODEL = 2048
BATCH = 8
SEQ = 2048
DEC_BATCH = 128
DEC_SEQ = 8
PAST_LEN = 2048
PAGE_SIZE = 128
N_PAGES = PAST_LEN // PAGE_SIZE
N_HEADS = 8
N_KV_HEADS = 2
HEAD_DIM = 128
GROUP = N_HEADS // N_KV_HEADS
D_ATTN = N_HEADS * HEAD_DIM
D_KV = N_KV_HEADS * HEAD_DIM
IDX_HEADS = 8
IDX_DIM = 64
TOPK = 256
Q_BLOCK = 128
N_BUCKETS = 32
MAX_DISTANCE = 128
D_CONV = D_MODEL // 2
CONV_WIDTH = 31
EPS = 1e-6
NEG = -1e30
NEG_INF = float("-inf")

LANE = 128
SUBLANE = 8
VMEM_LIMIT = 56 * 1024 * 1024

C_GA = 0
C_GC = C_GA + D_MODEL
C_Q = C_GC + D_MODEL
C_ZA = C_Q + D_ATTN
C_GV = C_ZA + D_ATTN
C_GG = C_GV + D_CONV
C_ZC = C_GG + D_CONV
C_K = C_ZC + D_CONV
C_V = C_K + D_KV
C_QI = C_V + D_KV
C_KI = C_QI + IDX_HEADS * IDX_DIM
D_H = C_KI + LANE
W_OFF = IDX_DIM

_SRC = {}
_off = 0
for _name, _w in (('q', D_ATTN), ('k', D_KV), ('v', D_KV), ('z_attn', D_ATTN),
                  ('q_idx', IDX_HEADS * IDX_DIM), ('k_idx', IDX_DIM), ('w_idx', IDX_HEADS),
                  ('glu_val', D_CONV), ('glu_gate', D_CONV), ('z_conv', D_CONV),
                  ('gate_attn', D_MODEL), ('gate_conv', D_MODEL)):
    _SRC[_name] = (_off, _w)
    _off += _w

N_BISECT = 26
PAD_ROWS = 32
CONV_CHUNK = 64
SCORE_SCALE = (IDX_DIM ** -0.5) * (IDX_HEADS ** -0.5)
QK_SCALE = HEAD_DIM ** -0.5


def _t5_bucket_static(dist):
    n = np.maximum(dist, 0)
    max_exact = N_BUCKETS // 2
    ratio = (np.log(np.maximum(n, 1).astype(np.float32) / np.float32(max_exact))
             / np.float32(math.log(MAX_DISTANCE / max_exact)))
    large = np.minimum(max_exact + (ratio * np.float32(N_BUCKETS - max_exact)).astype(np.int32),
                       N_BUCKETS - 1)
    return np.where(n < max_exact, n, large).astype(np.int32)


FAR_BUCKET = int(_t5_bucket_static(np.array([2 * MAX_DISTANCE]))[0])


def _params(sem):
    return pltpu.CompilerParams(dimension_semantics=sem, vmem_limit_bytes=VMEM_LIMIT)


def _proj_kernel(x_ref, g_ref, w_ref, o_ref, xn_ref):
    @pl.when(pl.program_id(1) == 0)
    def _():
        x = x_ref[...]
        ms = jnp.mean(x * x, axis=-1, keepdims=True)
        xn_ref[...] = (x * lax.rsqrt(ms + EPS) * g_ref[...]).astype(BF16)

    o_ref[...] = jnp.dot(xn_ref[...], w_ref[...], preferred_element_type=F32)


def _proj(x2d, g, w, tm, tn):
    n = x2d.shape[0]
    return pl.pallas_call(
        _proj_kernel,
        grid=(n // tm, D_H // tn),
        in_specs=[pl.BlockSpec((tm, D_MODEL), lambda i, j: (i, 0)),
                  pl.BlockSpec((1, D_MODEL), lambda i, j: (0, 0)),
                  pl.BlockSpec((D_MODEL, tn), lambda i, j: (0, j))],
        out_specs=pl.BlockSpec((tm, tn), lambda i, j: (i, j)),
        out_shape=jax.ShapeDtypeStruct((n, D_H), F32),
        scratch_shapes=[pltpu.VMEM((tm, D_MODEL), BF16)],
        compiler_params=_params(("parallel", "arbitrary")),
        name="proj",
    )(x2d, g, w)


def _bias_kernel(rb_ref, bp_ref, bs_ref, op_ref, os_ref):
    bp = bp_ref[...]
    bs = bs_ref[...]
    for h in range(N_HEADS):
        far = rb_ref[FAR_BUCKET, h]
        tp = jnp.zeros(bp.shape, F32)
        ts = jnp.zeros(bs.shape, F32)
        for b in range(N_BUCKETS):
            val = rb_ref[b, h] - far
            tp = jnp.where(bp == b, val, tp)
            ts = jnp.where(bs == b, val, ts)
        op_ref[h] = tp
        os_ref[h] = ts


def _bias_tables(rel_bias):
    i = np.arange(Q_BLOCK)[:, None]
    j = np.arange(Q_BLOCK)[None, :]
    bucket_p = np.stack([_t5_bucket_static(d * Q_BLOCK + i - j) for d in range(2)])
    qi = np.arange(DEC_SEQ)[:, None]
    col = np.arange(2 * LANE)[None, :]
    dist_s = np.where(col < LANE, LANE + qi - col, qi - (col - LANE))
    bucket_s = _t5_bucket_static(dist_s)
    return pl.pallas_call(
        _bias_kernel,
        in_specs=[pl.BlockSpec(memory_space=pltpu.SMEM),
                  pl.BlockSpec(memory_space=pltpu.VMEM),
                  pl.BlockSpec(memory_space=pltpu.VMEM)],
        out_specs=[pl.BlockSpec(memory_space=pltpu.VMEM),
                   pl.BlockSpec(memory_space=pltpu.VMEM)],
        out_shape=[jax.ShapeDtypeStruct((N_HEADS, 2, Q_BLOCK, Q_BLOCK), F32),
                   jax.ShapeDtypeStruct((N_HEADS, DEC_SEQ, 2 * LANE), F32)],
        name="bias_tables",
    )(rel_bias, jnp.asarray(bucket_p), jnp.asarray(bucket_s))


def _row_any(x):
    return jnp.max(jnp.where(x, 1.0, 0.0)) > 0.5


def _select_mask(tile_fn, ntiles, rows, static):
    def reduce_tiles(fn, init, comb):
        if static:
            acc = init
            for j in range(ntiles):
                acc = comb(acc, fn(*tile_fn(j)))
            return acc
        return lax.fori_loop(0, ntiles, lambda j, acc: comb(acc, fn(*tile_fn(j))), init)

    zeros = jnp.zeros((rows, LANE), F32)

    def wide(x):
        return jnp.broadcast_to(x, (rows, LANE))

    def lane_sum(x):
        return jnp.sum(x, axis=1, keepdims=True)

    def lane_max(x):
        return jnp.max(x, axis=1, keepdims=True)

    def count(pred_fn):
        return lane_sum(reduce_tiles(lambda s, c: jnp.where(pred_fn(s, c), 1.0, 0.0),
                                     zeros, lambda a, b: a + b))

    def masked_max(pred_fn):
        return lane_max(reduce_tiles(lambda s, c: jnp.where(pred_fn(s, c), s, NEG_INF),
                                     jnp.full((rows, LANE), NEG_INF, F32), jnp.maximum))

    bound = lane_max(reduce_tiles(lambda s, c: jnp.where(s > NEG_INF, jnp.abs(s), 0.0),
                                  zeros, jnp.maximum))

    def bisect(_, carry):
        lo, hi = carry
        mid = 0.5 * lo + 0.5 * hi
        midw = wide(mid)
        few = count(lambda s, c: s > midw) < TOPK
        return jnp.where(few, lo, mid), jnp.where(few, mid, hi)

    _, hi = lax.fori_loop(0, N_BISECT, bisect, (-bound, bound))

    hiw = wide(hi)
    thr = masked_max(lambda s, c: s <= hiw)
    thrw = wide(thr)
    n_ge = count(lambda s, c: s >= thrw)

    def fix_cond(carry):
        return carry[2]

    def fix_body(carry):
        thr, n_ge, _ = carry
        short = n_ge < TOPK
        oldw = wide(thr)
        lower = masked_max(lambda s, c: s < oldw)
        thr = jnp.where(short, lower, thr)
        neww = wide(thr)
        n_ge = count(lambda s, c: s >= neww)
        return thr, n_ge, _row_any(n_ge < TOPK)

    thr, n_ge, _ = lax.while_loop(fix_cond, fix_body, (thr, n_ge, _row_any(n_ge < TOPK)))
    thrw = wide(thr)
    n_gt = count(lambda s, c: s > thrw)
    need = TOPK - n_gt
    use_ties = _row_any(n_ge - n_gt > need)
    return thr, need, use_ties, count


def _tie_cut(count, thr, need, ncols):
    steps = int(math.ceil(math.log2(ncols))) + 1
    thrw = jnp.broadcast_to(thr, (thr.shape[0], LANE))

    def body(_, carry):
        lo, hi = carry
        mid = jnp.floor((lo + hi) * 0.5)
        midw = jnp.broadcast_to(mid, thrw.shape)
        enough = count(lambda s, c: jnp.where(s == thrw, c, ncols + 1.0) <= midw) >= need
        return jnp.where(enough, lo, mid), jnp.where(enough, mid, hi)

    lo0 = jnp.full(thr.shape, -1.0, F32)
    hi0 = jnp.full(thr.shape, ncols - 1.0, F32)
    _, hi = lax.fori_loop(0, steps, body, (lo0, hi0))
    return hi


def _attn_p_kernel(q_ref, qi_ref, wi_ref, k_ref, v_ref, ki_ref, bias_ref, o_ref,
                   kb_ref, vb_ref, kib_ref, qh_ref, qih_ref, wb_ref,
                   score_ref, mask_ref, m_ref, l_ref, acc_ref):
    i = pl.program_id(1)
    nkb = i + 1
    T = Q_BLOCK

    @pl.when(i == 0)
    def _():
        kb_ref[...] = k_ref[0].astype(BF16)
        vb_ref[...] = v_ref[0].astype(BF16)
        kib_ref[...] = ki_ref[0][:, :IDX_DIM].astype(BF16)

    q = q_ref[0] * QK_SCALE
    for h in range(N_HEADS):
        qh_ref[h] = q[:, h * HEAD_DIM:(h + 1) * HEAD_DIM].astype(BF16)
    qi = qi_ref[0]
    w = wi_ref[0][:, W_OFF:W_OFF + IDX_HEADS] * SCORE_SCALE
    for h in range(IDX_HEADS):
        qih_ref[h] = qi[:, h * IDX_DIM:(h + 1) * IDX_DIM].astype(BF16)
        wb_ref[h] = jnp.broadcast_to(w[:, h:h + 1], (T, LANE))

    row = i * T + lax.broadcasted_iota(jnp.int32, (T, LANE), 0)
    lane = lax.broadcasted_iota(jnp.int32, (T, LANE), 1)

    def cols(j):
        return pl.ds(pl.multiple_of(j * LANE, LANE), LANE)

    def score_body(j, _):
        kj = kib_ref[cols(j), :]
        acc = jnp.zeros((T, LANE), F32)
        for h in range(IDX_HEADS):
            d = lax.dot_general(qih_ref[h], kj, (((1,), (1,)), ((), ())),
                                preferred_element_type=F32)
            acc = acc + jnp.maximum(d, 0.0) * wb_ref[h]
        acc = jnp.where(j * LANE + lane <= row, acc, NEG_INF)
        score_ref[:, cols(j)] = acc
        return 0

    lax.fori_loop(0, nkb, score_body, 0)

    def keep(pred):
        return jnp.where(pred, 0.0, NEG)

    def write_mask(mask_fn):
        def body(j, _):
            s = score_ref[:, cols(j)]
            c = (j * LANE + lane).astype(F32)
            mask_ref[:, cols(j)] = mask_fn(s, c)
            return 0
        lax.fori_loop(0, nkb, body, 0)

    @pl.when(i * T + T <= TOPK)
    def _():
        write_mask(lambda s, c: keep(s > NEG_INF))

    @pl.when(i * T + T > TOPK)
    def _():
        def tile_fn(j):
            return score_ref[:, cols(j)], (j * LANE + lane).astype(F32)

        thr, need, use_ties, count = _select_mask(tile_fn, nkb, T, static=False)

        thrw = jnp.broadcast_to(thr, (T, LANE))

        @pl.when(jnp.logical_not(use_ties))
        def _():
            write_mask(lambda s, c: keep(s >= thrw))

        @pl.when(use_ties)
        def _():
            jcutw = jnp.broadcast_to(_tie_cut(count, thr, need, SEQ), (T, LANE))
            write_mask(lambda s, c: jnp.where(s == thrw, keep(c <= jcutw), keep(s > thrw)))

    m_ref[...] = jnp.full(m_ref.shape, NEG, F32)
    l_ref[...] = jnp.zeros(l_ref.shape, F32)
    acc_ref[...] = jnp.zeros(acc_ref.shape, F32)

    def attend(j, near):
        mb = mask_ref[:, cols(j)]
        for g in range(N_KV_HEADS):
            kj = kb_ref[cols(j), g * HEAD_DIM:(g + 1) * HEAD_DIM]
            vj = vb_ref[cols(j), g * HEAD_DIM:(g + 1) * HEAD_DIM]
            qg = qh_ref[g * GROUP:(g + 1) * GROUP].reshape(GROUP * T, HEAD_DIM)
            s = lax.dot_general(qg, kj, (((1,), (1,)), ((), ())), preferred_element_type=F32)
            s = s.reshape(GROUP, T, LANE) + mb[None]
            if near is not None:
                s = s + bias_ref[g * GROUP:(g + 1) * GROUP, near]
            s = s.reshape(GROUP * T, LANE)
            m_old = m_ref[g]
            m_new = jnp.maximum(m_old, jnp.max(s, axis=1, keepdims=True))
            alpha = jnp.exp(m_old - m_new)
            p = jnp.exp(s - m_new)
            l_ref[g] = alpha * l_ref[g] + jnp.sum(p, axis=1, keepdims=True)
            acc_ref[g] = alpha * acc_ref[g] + jnp.dot(p.astype(BF16), vj,
                                                      preferred_element_type=F32)
            m_ref[g] = m_new

    def far_body(j, _):
        attend(j, None)
        return 0

    lax.fori_loop(0, jnp.maximum(i - 1, 0), far_body, 0)

    @pl.when(i >= 1)
    def _():
        attend(i - 1, 1)

    attend(i, 0)

    for g in range(N_KV_HEADS):
        o = acc_ref[g] / l_ref[g]
        for hq in range(GROUP):
            h = g * GROUP + hq
            o_ref[0, :, h * HEAD_DIM:(h + 1) * HEAD_DIM] = o[hq * T:(hq + 1) * T]


def _attn_prompt(h_p, bias_p):
    h3 = h_p.reshape(BATCH, SEQ, D_H)
    nqb = SEQ // Q_BLOCK
    T = Q_BLOCK
    return pl.pallas_call(
        _attn_p_kernel,
        grid=(BATCH, nqb),
        in_specs=[
            pl.BlockSpec((1, T, D_ATTN), lambda b, i: (b, i, C_Q // D_ATTN)),
            pl.BlockSpec((1, T, IDX_HEADS * IDX_DIM), lambda b, i: (b, i, C_QI // (IDX_HEADS * IDX_DIM))),
            pl.BlockSpec((1, T, LANE), lambda b, i: (b, i, C_KI // LANE)),
            pl.BlockSpec((1, SEQ, D_KV), lambda b, i: (b, 0, C_K // D_KV)),
            pl.BlockSpec((1, SEQ, D_KV), lambda b, i: (b, 0, C_V // D_KV)),
            pl.BlockSpec((1, SEQ, LANE), lambda b, i: (b, 0, C_KI // LANE)),
            pl.BlockSpec((N_HEADS, 2, T, T), lambda b, i: (0, 0, 0, 0)),
        ],
        out_specs=pl.BlockSpec((1, T, D_ATTN), lambda b, i: (b, i, 0)),
        out_shape=jax.ShapeDtypeStruct((BATCH, SEQ, D_ATTN), F32),
        scratch_shapes=[
            pltpu.VMEM((SEQ, D_KV), BF16),
            pltpu.VMEM((SEQ, D_KV), BF16),
            pltpu.VMEM((SEQ, IDX_DIM), BF16),
            pltpu.VMEM((N_HEADS, T, HEAD_DIM), BF16),
            pltpu.VMEM((IDX_HEADS, T, IDX_DIM), BF16),
            pltpu.VMEM((IDX_HEADS, T, LANE), F32),
            pltpu.VMEM((T, SEQ), F32),
            pltpu.VMEM((T, SEQ), F32),
            pltpu.VMEM((N_KV_HEADS, GROUP * T, 1), F32),
            pltpu.VMEM((N_KV_HEADS, GROUP * T, 1), F32),
            pltpu.VMEM((N_KV_HEADS, GROUP * T, HEAD_DIM), F32),
        ],
        compiler_params=_params(("parallel", "arbitrary")),
        name="attn_prompt",
    )(h3, h3, h3, h3, h3, h3, bias_p)


def _attn_s_kernel(pt_ref, q_ref, qi_ref, w_ref, kn_ref, vn_ref, kin_ref, bias_ref, *rest):
    kp = rest[0:N_PAGES]
    vp = rest[N_PAGES:2 * N_PAGES]
    kip = rest[2 * N_PAGES:3 * N_PAGES]
    o_ref = rest[3 * N_PAGES]
    knp_ref, vnp_ref, kinp_ref, logit_ref = rest[3 * N_PAGES + 1:]
    del pt_ref
    R = DEC_SEQ
    NT = N_PAGES + 1
    L = NT * LANE

    knp_ref[...] = jnp.zeros(knp_ref.shape, BF16)
    vnp_ref[...] = jnp.zeros(vnp_ref.shape, BF16)
    kinp_ref[...] = jnp.zeros(kinp_ref.shape, BF16)
    knp_ref[0:2 * R] = jnp.concatenate([kn_ref[0], jnp.zeros((R, D_KV), F32)], 0).astype(BF16)
    vnp_ref[0:2 * R] = jnp.concatenate([vn_ref[0], jnp.zeros((R, D_KV), F32)], 0).astype(BF16)
    kinp_ref[0:2 * R] = jnp.concatenate([kin_ref[0][:, :IDX_DIM], jnp.zeros((R, IDX_DIM), F32)],
                                        0).astype(BF16)

    qi = qi_ref[0].astype(BF16)
    wb = jnp.broadcast_to(w_ref[0] * SCORE_SCALE, (IDX_HEADS * R, LANE))
    qrow = lax.broadcasted_iota(jnp.int32, (R, LANE), 0)
    lane = lax.broadcasted_iota(jnp.int32, (R, LANE), 1)

    def idx_keys(t):
        if t < N_PAGES:
            return kip[t][0].astype(BF16)
        return kinp_ref[...]

    tiles = []
    for t in range(NT):
        d = lax.dot_general(qi, idx_keys(t), (((1,), (1,)), ((), ())), preferred_element_type=F32)
        e = (jnp.maximum(d, 0.0) * wb).reshape(IDX_HEADS, R, LANE)
        s = e[0]
        for h in range(1, IDX_HEADS):
            s = s + e[h]
        if t == N_PAGES:
            s = jnp.where(lane <= qrow, s, NEG_INF)
        tiles.append(s)

    def tile_fn(j):
        return tiles[j], (j * LANE + lane).astype(F32)

    thr, need, use_ties, count = _select_mask(tile_fn, NT, R, static=True)
    jcut = lax.cond(use_ties,
                    lambda: _tie_cut(count, thr, need, L),
                    lambda: jnp.full(thr.shape, float(L), F32))

    q = (q_ref[0] * QK_SCALE).astype(BF16)
    for t in range(NT):
        s, c = tile_fn(t)
        mb = jnp.where(s == thr, jnp.where(c <= jcut, 0.0, NEG), jnp.where(s > thr, 0.0, NEG))
        if t < N_PAGES:
            kt = kp[t][0].astype(BF16)
        else:
            kt = knp_ref[...]
        for g in range(N_KV_HEADS):
            lg = lax.dot_general(q[g * GROUP * R:(g + 1) * GROUP * R],
                                 kt[:, g * HEAD_DIM:(g + 1) * HEAD_DIM],
                                 (((1,), (1,)), ((), ())), preferred_element_type=F32)
            lg = lg.reshape(GROUP, R, LANE) + mb[None]
            if t >= N_PAGES - 1:
                off = (t - (N_PAGES - 1)) * LANE
                lg = lg + bias_ref[g * GROUP:(g + 1) * GROUP, :, off:off + LANE]
            logit_ref[g * GROUP * R:(g + 1) * GROUP * R, t * LANE:(t + 1) * LANE] = (
                lg.reshape(GROUP * R, LANE))

    logits = logit_ref[...]
    m = jnp.max(logits, axis=1, keepdims=True)
    p = jnp.exp(logits - m)
    inv = 1.0 / jnp.sum(p, axis=1, keepdims=True)
    pb = p.astype(BF16)
    outs = [jnp.zeros((GROUP * R, HEAD_DIM), F32) for _ in range(N_KV_HEADS)]
    for t in range(NT):
        if t < N_PAGES:
            vt = vp[t][0].astype(BF16)
        else:
            vt = vnp_ref[...]
        for g in range(N_KV_HEADS):
            outs[g] = outs[g] + jnp.dot(pb[g * GROUP * R:(g + 1) * GROUP * R, t * LANE:(t + 1) * LANE],
                                        vt[:, g * HEAD_DIM:(g + 1) * HEAD_DIM],
                                        preferred_element_type=F32)
    for g in range(N_KV_HEADS):
        o = outs[g] * inv[g * GROUP * R:(g + 1) * GROUP * R]
        for hq in range(GROUP):
            h = g * GROUP + hq
            o_ref[0, :, h * HEAD_DIM:(h + 1) * HEAD_DIM] = o[hq * R:(hq + 1) * R]


def _attn_sample(h_s, page_table, cache_k, cache_v, cache_kidx, bias_s):
    R = DEC_SEQ
    h3 = h_s.reshape(DEC_BATCH, R, D_H)
    q_hq = h3[:, :, C_Q:C_Q + D_ATTN].reshape(DEC_BATCH, R, N_HEADS, HEAD_DIM)
    q_hq = q_hq.transpose(0, 2, 1, 3).reshape(DEC_BATCH, N_HEADS * R, HEAD_DIM)
    qi_hq = h3[:, :, C_QI:C_QI + IDX_HEADS * IDX_DIM].reshape(DEC_BATCH, R, IDX_HEADS, IDX_DIM)
    qi_hq = qi_hq.transpose(0, 2, 1, 3).reshape(DEC_BATCH, IDX_HEADS * R, IDX_DIM)
    w_hq = h3[:, :, C_KI + W_OFF:C_KI + W_OFF + IDX_HEADS].transpose(0, 2, 1)
    w_hq = w_hq.reshape(DEC_BATCH, IDX_HEADS * R, 1)
    n_phys = cache_k.shape[0]
    ck = cache_k.reshape(n_phys, PAGE_SIZE, D_KV)
    cv = cache_v.reshape(n_phys, PAGE_SIZE, D_KV)
    cki = cache_kidx.reshape(n_phys, PAGE_SIZE, IDX_DIM)

    def page_spec(width, p):
        return pl.BlockSpec((1, PAGE_SIZE, width), lambda b, pt: (pt[b, p], 0, 0))

    in_specs = [
        pl.BlockSpec((1, N_HEADS * R, HEAD_DIM), lambda b, pt: (b, 0, 0)),
        pl.BlockSpec((1, IDX_HEADS * R, IDX_DIM), lambda b, pt: (b, 0, 0)),
        pl.BlockSpec((1, IDX_HEADS * R, 1), lambda b, pt: (b, 0, 0)),
        pl.BlockSpec((1, R, D_KV), lambda b, pt: (b, 0, C_K // D_KV)),
        pl.BlockSpec((1, R, D_KV), lambda b, pt: (b, 0, C_V // D_KV)),
        pl.BlockSpec((1, R, LANE), lambda b, pt: (b, 0, C_KI // LANE)),
        pl.BlockSpec((N_HEADS, R, 2 * LANE), lambda b, pt: (0, 0, 0)),
    ]
    in_specs += [page_spec(D_KV, p) for p in range(N_PAGES)]
    in_specs += [page_spec(D_KV, p) for p in range(N_PAGES)]
    in_specs += [page_spec(IDX_DIM, p) for p in range(N_PAGES)]
    grid_spec = pltpu.PrefetchScalarGridSpec(
        num_scalar_prefetch=1,
        grid=(DEC_BATCH,),
        in_specs=in_specs,
        out_specs=pl.BlockSpec((1, R, D_ATTN), lambda b, pt: (b, 0, 0)),
        scratch_shapes=[
            pltpu.VMEM((PAGE_SIZE, D_KV), BF16),
            pltpu.VMEM((PAGE_SIZE, D_KV), BF16),
            pltpu.VMEM((PAGE_SIZE, IDX_DIM), BF16),
            pltpu.VMEM((N_HEADS * R, (N_PAGES + 1) * LANE), F32),
        ],
    )
    return pl.pallas_call(
        _attn_s_kernel,
        grid_spec=grid_spec,
        out_shape=jax.ShapeDtypeStruct((DEC_BATCH, R, D_ATTN), F32),
        compiler_params=_params(("arbitrary",)),
        name="attn_sample",
    )(page_table, q_hq, qi_hq, w_hq, h3, h3, h3, bias_s,
      *([ck] * N_PAGES), *([cv] * N_PAGES), *([cki] * N_PAGES))


def _conv_p_kernel(val_ref, gate_ref, cw_ref, cb_ref, dw_ref, ut_ref, pad_ref):
    pad_ref[0:PAD_ROWS] = jnp.zeros((PAD_ROWS, LANE), F32)
    pad_ref[PAD_ROWS:] = val_ref[0] * jax.nn.sigmoid(gate_ref[0])
    ut_ref[0] = pad_ref[SEQ:SEQ + PAD_ROWS]
    cw = cw_ref[...]
    cb = cb_ref[...]
    first = PAD_ROWS - (CONV_WIDTH - 1)
    for c in range(SEQ // CONV_CHUNK):
        base = c * CONV_CHUNK
        acc = jnp.broadcast_to(cb, (CONV_CHUNK, LANE))
        for r in range(SUBLANE):
            taps = [w for w in range(CONV_WIDTH) if (first + w) % SUBLANE == r]
            span = max(first + w - r for w in taps) + CONV_CHUNK
            win = pad_ref[base + r:base + r + span]
            for w in taps:
                a = first + w - r
                acc = acc + win[a:a + CONV_CHUNK] * cw[w:w + 1]
        dw_ref[0, base:base + CONV_CHUNK] = acc


def _conv_prompt(h_p, cw_pad, cb):
    h3 = h_p.reshape(BATCH, SEQ, D_H)
    nc = D_CONV // LANE
    return pl.pallas_call(
        _conv_p_kernel,
        grid=(BATCH, nc),
        in_specs=[pl.BlockSpec((1, SEQ, LANE), lambda b, c: (b, 0, C_GV // LANE + c)),
                  pl.BlockSpec((1, SEQ, LANE), lambda b, c: (b, 0, C_GG // LANE + c)),
                  pl.BlockSpec((PAD_ROWS, LANE), lambda b, c: (0, c)),
                  pl.BlockSpec((1, LANE), lambda b, c: (0, c))],
        out_specs=[pl.BlockSpec((1, SEQ, LANE), lambda b, c: (b, 0, c)),
                   pl.BlockSpec((1, PAD_ROWS, LANE), lambda b, c: (b, 0, c))],
        out_shape=[jax.ShapeDtypeStruct((BATCH, SEQ, D_CONV), F32),
                   jax.ShapeDtypeStruct((BATCH, PAD_ROWS, D_CONV), F32)],
        scratch_shapes=[pltpu.VMEM((PAD_ROWS + SEQ, LANE), F32)],
        compiler_params=_params(("parallel", "parallel")),
        name="conv_prompt",
    )(h3, h3, cw_pad, cb)


def _conv_s_kernel(val_ref, gate_ref, st_ref, cw_ref, cb_ref, dw_ref, u_ref, pad_ref):
    R = DEC_SEQ
    u = val_ref[...] * jax.nn.sigmoid(gate_ref[...])
    u_ref[...] = u
    first = PAD_ROWS - (CONV_WIDTH - 1)
    pad_ref[:, 0:SUBLANE] = jnp.zeros((DEC_BATCH, SUBLANE, LANE), F32)
    pad_ref[:, first:PAD_ROWS] = st_ref[...]
    pad_ref[:, PAD_ROWS:PAD_ROWS + R] = u
    cw = cw_ref[...]
    acc = jnp.broadcast_to(cb_ref[...][None], (DEC_BATCH, R, LANE))
    for w in range(CONV_WIDTH):
        acc = acc + pad_ref[:, first + w:first + w + R] * cw[w:w + 1][None]
    dw_ref[...] = acc


def _conv_sample(h_s, state, cw_pad, cb):
    R = DEC_SEQ
    h3 = h_s.reshape(DEC_BATCH, R, D_H)
    nc = D_CONV // LANE
    blk = pl.BlockSpec((DEC_BATCH, R, LANE), lambda c: (0, 0, c))
    return pl.pallas_call(
        _conv_s_kernel,
        grid=(nc,),
        in_specs=[pl.BlockSpec((DEC_BATCH, R, LANE), lambda c: (0, 0, C_GV // LANE + c)),
                  pl.BlockSpec((DEC_BATCH, R, LANE), lambda c: (0, 0, C_GG // LANE + c)),
                  pl.BlockSpec((DEC_BATCH, CONV_WIDTH - 1, LANE), lambda c: (0, 0, c)),
                  pl.BlockSpec((PAD_ROWS, LANE), lambda c: (0, c)),
                  pl.BlockSpec((1, LANE), lambda c: (0, c))],
        out_specs=[blk, blk],
        out_shape=[jax.ShapeDtypeStruct((DEC_BATCH, R, D_CONV), F32),
                   jax.ShapeDtypeStruct((DEC_BATCH, R, D_CONV), F32)],
        scratch_shapes=[pltpu.VMEM((DEC_BATCH, PAD_ROWS + R, LANE), F32)],
        compiler_params=_params(("parallel",)),
        name="conv_sample",
    )(h3, h3, state, cw_pad, cb)


def _tail_kernel(attn_ref, za_ref, dw_ref, zc_ref, ga_ref, gc_ref, x_ref,
                 wua_ref, wpw_ref, wuc_ref, wo_ref, ng_ref, nb_ref, bpw_ref, fg_ref, y_ref):
    a = attn_ref[...] * jax.nn.silu(za_ref[...])
    branch_attn = jnp.dot(a.astype(BF16), wua_ref[...], preferred_element_type=F32)

    dw = dw_ref[...]
    mu = jnp.mean(dw, axis=-1, keepdims=True)
    var = jnp.mean(jnp.square(dw - mu), axis=-1, keepdims=True)
    ln = (dw - mu) * lax.rsqrt(var + EPS) * ng_ref[...] + nb_ref[...]
    conv_out = jnp.dot(jax.nn.silu(ln).astype(BF16), wpw_ref[...],
                       preferred_element_type=F32) + bpw_ref[...]
    c = conv_out * jax.nn.silu(zc_ref[...])
    branch_conv = jnp.dot(c.astype(BF16), wuc_ref[...], preferred_element_type=F32)

    merged = jax.nn.sigmoid(ga_ref[...]) * branch_attn + jax.nn.sigmoid(gc_ref[...]) * branch_conv
    y = x_ref[...] + jnp.dot(merged.astype(BF16), wo_ref[...], preferred_element_type=F32)
    ms = jnp.mean(y * y, axis=-1, keepdims=True)
    y_ref[...] = y * lax.rsqrt(ms + EPS) * fg_ref[...]


def _tail(h, attn, dw, x2d, wua, wpw, wuc, wo, ng, nb, bpw, fg, tm):
    n = x2d.shape[0]

    def const(shape):
        return pl.BlockSpec(shape, lambda i: (0, 0), pipeline_mode=pl.Buffered(1))

    return pl.pallas_call(
        _tail_kernel,
        grid=(n // tm,),
        in_specs=[pl.BlockSpec((tm, D_ATTN), lambda i: (i, 0)),
                  pl.BlockSpec((tm, D_ATTN), lambda i: (i, C_ZA // D_ATTN)),
                  pl.BlockSpec((tm, D_CONV), lambda i: (i, 0)),
                  pl.BlockSpec((tm, D_CONV), lambda i: (i, C_ZC // D_CONV)),
                  pl.BlockSpec((tm, D_MODEL), lambda i: (i, C_GA // D_MODEL)),
                  pl.BlockSpec((tm, D_MODEL), lambda i: (i, C_GC // D_MODEL)),
                  pl.BlockSpec((tm, D_MODEL), lambda i: (i, 0)),
                  const((D_ATTN, D_MODEL)), const((D_CONV, D_CONV)),
                  const((D_CONV, D_MODEL)), const((D_MODEL, D_MODEL)),
                  const((1, D_CONV)), const((1, D_CONV)), const((1, D_CONV)),
                  const((1, D_MODEL))],
        out_specs=pl.BlockSpec((tm, D_MODEL), lambda i: (i, 0)),
        out_shape=jax.ShapeDtypeStruct((n, D_MODEL), F32),
        compiler_params=_params(("parallel",)),
        name="tail",
    )(attn, h, dw, h, h, h, x2d, wua, wpw, wuc, wo, ng, nb, bpw, fg)


def _reorder_w_in(w_in):
    def col(name):
        o, w = _SRC[name]
        return w_in[:, o:o + w]

    pad = jnp.zeros((D_MODEL, LANE - IDX_DIM - IDX_HEADS), w_in.dtype)
    w = jnp.concatenate([col('gate_attn'), col('gate_conv'), col('q'), col('z_attn'),
                         col('glu_val'), col('glu_gate'), col('z_conv'), col('k'), col('v'),
                         col('q_idx'), col('k_idx'), col('w_idx'), pad], axis=1)
    return w.astype(BF16)


def kernel(x_prompt, x_sample, cache_k, cache_v, cache_kidx, state_conv, page_table,
           ln_g, w_in, conv_w, conv_b, conv_norm_g, conv_norm_b, w_pw, b_pw,
           w_up_attn, w_up_conv, w_out, rel_bias, final_g):
    w_all = _reorder_w_in(w_in[0])
    g_in = ln_g[0].reshape(1, D_MODEL)
    xp = x_prompt.reshape(BATCH * SEQ, D_MODEL)
    xs = x_sample.reshape(DEC_BATCH * DEC_SEQ, D_MODEL)
    h_p = _proj(xp, g_in, w_all, 512, 1152)
    h_s = _proj(xs, g_in, w_all, 512, 1152)

    bias_p, bias_s = _bias_tables(rel_bias)

    attn_p = _attn_prompt(h_p, bias_p).reshape(BATCH * SEQ, D_ATTN)
    attn_s = _attn_sample(h_s, page_table, cache_k[0], cache_v[0], cache_kidx[0], bias_s)
    attn_s = attn_s.reshape(DEC_BATCH * DEC_SEQ, D_ATTN)

    cw_pad = jnp.concatenate([conv_w[0], jnp.zeros((PAD_ROWS - CONV_WIDTH, D_CONV), F32)], 0)
    cb = conv_b[0].reshape(1, D_CONV)
    dw_p, u_tail = _conv_prompt(h_p, cw_pad, cb)
    dw_s, u_s = _conv_sample(h_s, state_conv[0], cw_pad, cb)

    wua = w_up_attn[0].astype(BF16)
    wpw = w_pw[0].astype(BF16)
    wuc = w_up_conv[0].astype(BF16)
    wo = w_out[0].astype(BF16)
    ng = conv_norm_g[0].reshape(1, D_CONV)
    nb = conv_norm_b[0].reshape(1, D_CONV)
    bpw = b_pw[0].reshape(1, D_CONV)
    fg = final_g.reshape(1, D_MODEL)
    y_p = _tail(h_p, attn_p, dw_p.reshape(BATCH * SEQ, D_CONV), xp,
                wua, wpw, wuc, wo, ng, nb, bpw, fg, 256)
    y_s = _tail(h_s, attn_s, dw_s.reshape(DEC_BATCH * DEC_SEQ, D_CONV), xs,
                wua, wpw, wuc, wo, ng, nb, bpw, fg, 256)

    hp3 = h_p.reshape(BATCH, SEQ, D_H)
    hs3 = h_s.reshape(DEC_BATCH, DEC_SEQ, D_H)
    tail_rows = CONV_WIDTH - 1
    return (
        y_p.reshape(BATCH, SEQ, D_MODEL),
        y_s.reshape(DEC_BATCH, DEC_SEQ, D_MODEL),
        hp3[:, :, C_K:C_K + D_KV].reshape(1, BATCH, SEQ, N_KV_HEADS, HEAD_DIM),
        hp3[:, :, C_V:C_V + D_KV].reshape(1, BATCH, SEQ, N_KV_HEADS, HEAD_DIM),
        hp3[:, :, C_KI:C_KI + IDX_DIM].reshape(1, BATCH, SEQ, IDX_DIM),
        u_tail[:, PAD_ROWS - tail_rows:].reshape(1, BATCH, tail_rows, D_CONV),
        hs3[:, :, C_K:C_K + D_KV].reshape(1, DEC_BATCH, DEC_SEQ, N_KV_HEADS, HEAD_DIM),
        hs3[:, :, C_V:C_V + D_KV].reshape(1, DEC_BATCH, DEC_SEQ, N_KV_HEADS, HEAD_DIM),
        hs3[:, :, C_KI:C_KI + IDX_DIM].reshape(1, DEC_BATCH, DEC_SEQ, IDX_DIM),
        jnp.concatenate([state_conv[0][:, DEC_SEQ:], u_s], axis=1).reshape(
            1, DEC_BATCH, tail_rows, D_CONV),
    )
```

```python
import math

import numpy as np
import jax
import jax.numpy as jnp
from jax import lax
from jax.experimental import pallas as pl
from jax.experimental.pallas import tpu as pltpu

F32 = jnp.float32
BF16 = jnp.bfloat16

D_MODEL = 2048
BATCH = 8
SEQ = 2048
DEC_BATCH = 128
DEC_SEQ = 8
PAST_LEN = 2048
PAGE_SIZE = 128
N_PAGES = PAST_LEN // PAGE_SIZE
N_HEADS = 8
N_KV_HEADS = 2
HEAD_DIM = 128
GROUP = N_HEADS // N_KV_HEADS
D_ATTN = N_HEADS * HEAD_DIM
D_KV = N_KV_HEADS * HEAD_DIM
IDX_HEADS = 8
IDX_DIM = 64
TOPK = 256
Q_BLOCK = 128
N_BUCKETS = 32
MAX_DISTANCE = 128
D_CONV = D_MODEL // 2
CONV_WIDTH = 31
EPS = 1e-6
NEG = -1e30
NEG_INF = float("-inf")

LANE = 128
SUBLANE = 8
VMEM_LIMIT = 56 * 1024 * 1024

C_GA = 0
C_GC = C_GA + D_MODEL
C_Q = C_GC + D_MODEL
C_ZA = C_Q + D_ATTN
C_GV = C_ZA + D_ATTN
C_GG = C_GV + D_CONV
C_ZC = C_GG + D_CONV
C_K = C_ZC + D_CONV
C_V = C_K + D_KV
C_QI = C_V + D_KV
C_KI = C_QI + IDX_HEADS * IDX_DIM
D_H = C_KI + LANE
W_OFF = IDX_DIM

_SRC = {}
_off = 0
for _name, _w in (('q', D_ATTN), ('k', D_KV), ('v', D_KV), ('z_attn', D_ATTN),
                  ('q_idx', IDX_HEADS * IDX_DIM), ('k_idx', IDX_DIM), ('w_idx', IDX_HEADS),
                  ('glu_val', D_CONV), ('glu_gate', D_CONV), ('z_conv', D_CONV),
                  ('gate_attn', D_MODEL), ('gate_conv', D_MODEL)):
    _SRC[_name] = (_off, _w)
    _off += _w

N_BISECT = 22
PAD_ROWS = 32
CONV_CHUNK = 64
SCORE_SCALE = (IDX_DIM ** -0.5) * (IDX_HEADS ** -0.5)
QK_SCALE = HEAD_DIM ** -0.5
L_SAMPLE = (N_PAGES + 1) * LANE
SUBTILES = LANE // SUBLANE
CHUNK_BLOCKS = 4
CHUNK = CHUNK_BLOCKS * Q_BLOCK
PAD_KEYS = CHUNK - Q_BLOCK


def _t5_bucket_static(dist):
    n = np.maximum(dist, 0)
    max_exact = N_BUCKETS // 2
    ratio = (np.log(np.maximum(n, 1).astype(np.float32) / np.float32(max_exact))
             / np.float32(math.log(MAX_DISTANCE / max_exact)))
    large = np.minimum(max_exact + (ratio * np.float32(N_BUCKETS - max_exact)).astype(np.int32),
                       N_BUCKETS - 1)
    return np.where(n < max_exact, n, large).astype(np.int32)


FAR_BUCKET = int(_t5_bucket_static(np.array([2 * MAX_DISTANCE]))[0])


def _params(sem):
    return pltpu.CompilerParams(dimension_semantics=sem, vmem_limit_bytes=VMEM_LIMIT)


def _proj_kernel(x_ref, g_ref, w_ref, o_ref, xn_ref):
    @pl.when(pl.program_id(1) == 0)
    def _():
        x = x_ref[...]
        ms = jnp.mean(x * x, axis=-1, keepdims=True)
        xn_ref[...] = (x * lax.rsqrt(ms + EPS) * g_ref[...]).astype(BF16)

    o_ref[...] = jnp.dot(xn_ref[...], w_ref[...], preferred_element_type=F32)


def _proj(x2d, g, w, tm, tn):
    n = x2d.shape[0]
    return pl.pallas_call(
        _proj_kernel,
        grid=(n // tm, D_H // tn),
        in_specs=[pl.BlockSpec((tm, D_MODEL), lambda i, j: (i, 0)),
                  pl.BlockSpec((1, D_MODEL), lambda i, j: (0, 0)),
                  pl.BlockSpec((D_MODEL, tn), lambda i, j: (0, j))],
        out_specs=pl.BlockSpec((tm, tn), lambda i, j: (i, j)),
        out_shape=jax.ShapeDtypeStruct((n, D_H), F32),
        scratch_shapes=[pltpu.VMEM((tm, D_MODEL), BF16)],
        compiler_params=_params(("parallel", "arbitrary")),
        name="proj",
    )(x2d, g, w)


def _bias_kernel(rb_ref, bp_ref, bs_ref, op_ref, os_ref):
    bp = bp_ref[...]
    bs = bs_ref[...]
    for h in range(N_HEADS):
        far = rb_ref[FAR_BUCKET, h]
        tp = jnp.zeros(bp.shape, F32)
        ts = jnp.zeros(bs.shape, F32)
        for b in range(N_BUCKETS):
            val = rb_ref[b, h] - far
            tp = jnp.where(bp == b, val, tp)
            ts = jnp.where(bs == b, val, ts)
        op_ref[h] = tp
        os_ref[h] = ts


def _bias_tables(rel_bias):
    key = np.arange(CHUNK)[:, None]
    qry = np.arange(Q_BLOCK)[None, :]
    bucket_p = _t5_bucket_static(CHUNK - Q_BLOCK + qry - key)
    qi = np.arange(DEC_SEQ)[:, None]
    col = np.arange(2 * LANE)[None, :]
    dist_s = np.where(col < LANE, LANE + qi - col, qi - (col - LANE))
    bucket_s = _t5_bucket_static(dist_s)
    return pl.pallas_call(
        _bias_kernel,
        in_specs=[pl.BlockSpec(memory_space=pltpu.SMEM),
                  pl.BlockSpec(memory_space=pltpu.VMEM),
                  pl.BlockSpec(memory_space=pltpu.VMEM)],
        out_specs=[pl.BlockSpec(memory_space=pltpu.VMEM),
                   pl.BlockSpec(memory_space=pltpu.VMEM)],
        out_shape=[jax.ShapeDtypeStruct((N_HEADS, CHUNK, Q_BLOCK), F32),
                   jax.ShapeDtypeStruct((N_HEADS, DEC_SEQ, 2 * LANE), F32)],
        name="bias_tables",
    )(rel_bias, jnp.asarray(bucket_p), jnp.asarray(bucket_s))


def _any(x):
    return jnp.max(jnp.where(x, 1.0, 0.0)) > 0.5


def _rep(x):
    return jnp.broadcast_to(x, (SUBLANE, LANE))


def _fold_rows(x, comb):
    parts = [x[k:k + SUBLANE] for k in range(0, x.shape[0], SUBLANE)]
    while len(parts) > 1:
        parts = [comb(parts[k], parts[k + 1]) for k in range(0, len(parts), 2)]
    return parts[0]


def _key_iota(nsub):
    sub = lax.broadcasted_iota(jnp.int32, (nsub, SUBLANE, LANE), 0) * SUBLANE
    return sub + lax.broadcasted_iota(jnp.int32, (nsub, SUBLANE, LANE), 1)


def _select_threshold(tile_fn, ntiles, static):
    def reduce_tiles(fn, init, comb):
        def step(j, acc):
            x = fn(tile_fn(j), j)
            parts = [x[k] for k in range(x.shape[0])]
            while len(parts) > 1:
                parts = [comb(parts[k], parts[k + 1]) for k in range(0, len(parts), 2)]
            return comb(acc, parts[0])
        if static:
            acc = init
            for j in range(ntiles):
                acc = step(j, acc)
            return acc
        return lax.fori_loop(0, ntiles, step, init)

    zeros = jnp.zeros((SUBLANE, LANE), F32)

    def count(pred_fn):
        acc = reduce_tiles(lambda s, j: jnp.where(pred_fn(s, j), 1.0, 0.0), zeros,
                           lambda a, b: a + b)
        return _rep(jnp.sum(acc, axis=0, keepdims=True))

    def masked_max(pred_fn):
        acc = reduce_tiles(lambda s, j: jnp.where(pred_fn(s, j), s, NEG_INF),
                           jnp.full((SUBLANE, LANE), NEG_INF, F32), jnp.maximum)
        return _rep(jnp.max(acc, axis=0, keepdims=True))

    bound = reduce_tiles(lambda s, j: jnp.where(s > NEG_INF, jnp.abs(s), 0.0), zeros,
                         jnp.maximum)
    bound = _rep(jnp.max(bound, axis=0, keepdims=True))

    def bisect(_, carry):
        lo, hi = carry
        mid = 0.5 * lo + 0.5 * hi
        few = count(lambda s, j: s > mid[None]) < TOPK
        return jnp.where(few, lo, mid), jnp.where(few, mid, hi)

    _, hi = lax.fori_loop(0, N_BISECT, bisect, (-bound, bound))

    thr = masked_max(lambda s, j: s <= hi[None])
    n_ge = count(lambda s, j: s >= thr[None])

    def fix_body(carry):
        thr, n_ge, _ = carry
        lower = masked_max(lambda s, j: s < thr[None])
        thr = jnp.where(n_ge < TOPK, lower, thr)
        n_ge = count(lambda s, j: s >= thr[None])
        return thr, n_ge, _any(n_ge < TOPK)

    thr, n_ge, _ = lax.while_loop(lambda c: c[2], fix_body, (thr, n_ge, _any(n_ge < TOPK)))
    n_gt = count(lambda s, j: s > thr[None])
    need = TOPK - n_gt
    use_ties = _any(n_ge - n_gt > need)
    return thr, need, use_ties, count


def _tie_cut(count, key_fn, thr, need, nkeys):
    steps = int(math.ceil(math.log2(nkeys))) + 1

    def body(_, carry):
        lo, hi = carry
        mid = jnp.floor((lo + hi) * 0.5)
        enough = count(lambda s, j: jnp.where(s == thr[None], key_fn(j), nkeys + 1.0)
                       <= mid[None]) >= need
        return jnp.where(enough, lo, mid), jnp.where(enough, mid, hi)

    lo0 = jnp.full((SUBLANE, LANE), -1.0, F32)
    hi0 = jnp.full((SUBLANE, LANE), nkeys - 1.0, F32)
    _, hi = lax.fori_loop(0, steps, body, (lo0, hi0))
    return hi


def _keep(pred):
    return jnp.where(pred, 0.0, NEG)


def _mask_plain(thr):
    return lambda s, j: _keep(s >= thr[None])


def _mask_ties(key_fn, thr, cut):
    return lambda s, j: jnp.where(s == thr[None], _keep(key_fn(j) <= cut[None]),
                                  _keep(s > thr[None]))


def _attn_p_kernel(q_ref, qi_ref, wi_ref, k_ref, v_ref, ki_ref, bias_ref, o_ref,
                   kb_ref, vt_ref, kib_ref, qh_ref, qih_ref,
                   score_ref, mask_ref, m_ref, l_ref, acc_ref):
    i = pl.program_id(1)
    T = Q_BLOCK
    W = GROUP * T
    nch = i // CHUNK_BLOCKS + 1

    @pl.when(i == 0)
    def _():
        kb_ref[0:PAD_KEYS] = jnp.zeros((PAD_KEYS, D_KV), BF16)
        kb_ref[PAD_KEYS:] = k_ref[0].astype(BF16)
        kib_ref[0:PAD_KEYS] = jnp.zeros((PAD_KEYS, IDX_DIM), BF16)
        kib_ref[PAD_KEYS:] = ki_ref[0][:, :IDX_DIM].astype(BF16)
        vt_ref[:, 0:PAD_KEYS] = jnp.zeros((D_KV, PAD_KEYS), BF16)
        for g in range(N_KV_HEADS):
            for c in range(SEQ // LANE):
                blk = v_ref[0, c * LANE:(c + 1) * LANE, g * HEAD_DIM:(g + 1) * HEAD_DIM]
                vt_ref[g * HEAD_DIM:(g + 1) * HEAD_DIM,
                       PAD_KEYS + c * LANE:PAD_KEYS + (c + 1) * LANE] = blk.T.astype(BF16)
        score_ref[0:PAD_KEYS] = jnp.full((PAD_KEYS, T), NEG_INF, F32)
        mask_ref[0:PAD_KEYS] = jnp.full((PAD_KEYS, T), NEG, F32)

    q = q_ref[0] * QK_SCALE
    for h in range(N_HEADS):
        qh_ref[h] = q[:, h * HEAD_DIM:(h + 1) * HEAD_DIM].astype(BF16)
    qi = qi_ref[0]
    for h in range(IDX_HEADS):
        qih_ref[h] = qi[:, h * IDX_DIM:(h + 1) * IDX_DIM].astype(BF16)
    w_rows = wi_ref[0].T[W_OFF:W_OFF + IDX_HEADS] * SCORE_SCALE

    def span(c):
        return pl.ds(pl.multiple_of((i - CHUNK_BLOCKS * c) * LANE, LANE), CHUNK)

    def first_key(c):
        return (i - CHUNK_BLOCKS * c) * LANE - PAD_KEYS

    key_l = lax.broadcasted_iota(jnp.int32, (CHUNK, T), 0)
    qry = i * T + lax.broadcasted_iota(jnp.int32, (CHUNK, T), 1)

    def score_body(c, _):
        kc = kib_ref[span(c), :]
        d = lax.dot_general(kc, qih_ref[...].reshape(IDX_HEADS * T, IDX_DIM),
                            (((1,), (1,)), ((), ())), preferred_element_type=F32)
        acc = jnp.zeros((CHUNK, T), F32)
        for h in range(IDX_HEADS):
            acc = acc + jnp.maximum(d[:, h * T:(h + 1) * T], 0.0) * w_rows[h:h + 1]
        key = first_key(c) + key_l
        acc = jnp.where(key <= qry, jnp.where(key >= 0, acc, NEG_INF), NEG_INF)
        score_ref[span(c), :] = acc
        return 0

    lax.fori_loop(0, nch, score_body, 0)

    nsub = CHUNK // SUBLANE

    def tile_fn(c):
        return score_ref[span(c), :].reshape(nsub, SUBLANE, T)

    def key_fn(c):
        return (first_key(c) + _key_iota(nsub)).astype(F32)

    def write_mask(mask_fn):
        def body(c, _):
            mask_ref[span(c), :] = mask_fn(tile_fn(c), c).reshape(CHUNK, T)
            return 0
        lax.fori_loop(0, nch, body, 0)

    @pl.when(i * T + T <= TOPK)
    def _():
        write_mask(lambda s, c: _keep(s > NEG_INF))

    @pl.when(i * T + T > TOPK)
    def _():
        thr, need, use_ties, count = _select_threshold(tile_fn, nch, static=False)

        @pl.when(jnp.logical_not(use_ties))
        def _():
            write_mask(_mask_plain(thr))

        @pl.when(use_ties)
        def _():
            write_mask(_mask_ties(key_fn, thr, _tie_cut(count, key_fn, thr, need, SEQ)))

    m_ref[...] = jnp.full(m_ref.shape, NEG, F32)
    l_ref[...] = jnp.zeros(l_ref.shape, F32)
    acc_ref[...] = jnp.zeros(acc_ref.shape, F32)

    def attend(c, with_bias):
        mb = mask_ref[span(c), :]
        for g in range(N_KV_HEADS):
            kc = kb_ref[span(c), g * HEAD_DIM:(g + 1) * HEAD_DIM]
            qg = qh_ref[g * GROUP:(g + 1) * GROUP].reshape(W, HEAD_DIM)
            s = lax.dot_general(kc, qg, (((1,), (1,)), ((), ())), preferred_element_type=F32)
            if with_bias:
                add = jnp.concatenate([mb + bias_ref[g * GROUP + hq] for hq in range(GROUP)],
                                      axis=1)
            else:
                add = jnp.concatenate([mb] * GROUP, axis=1)
            s = s + add
            m_old = m_ref[g]
            m_new = jnp.maximum(m_old, jnp.max(_fold_rows(s, jnp.maximum), axis=0, keepdims=True))
            alpha = jnp.exp(m_old - m_new)
            p = jnp.exp(s - m_new)
            l_ref[g] = alpha * l_ref[g] + jnp.sum(_fold_rows(p, jnp.add), axis=0, keepdims=True)
            vtc = vt_ref[g * HEAD_DIM:(g + 1) * HEAD_DIM, span(c)]
            acc_ref[g] = alpha * acc_ref[g] + jnp.dot(vtc, p.astype(BF16),
                                                      preferred_element_type=F32)
            m_ref[g] = m_new

    attend(0, True)

    def far_body(c, _):
        attend(c, False)
        return 0

    lax.fori_loop(1, nch, far_body, 0)

    for g in range(N_KV_HEADS):
        o = acc_ref[g] / l_ref[g]
        for hq in range(GROUP):
            h = g * GROUP + hq
            o_ref[0, :, h * HEAD_DIM:(h + 1) * HEAD_DIM] = o[:, hq * T:(hq + 1) * T].T


def _attn_prompt(h_p, bias_p):
    h3 = h_p.reshape(BATCH, SEQ, D_H)
    nqb = SEQ // Q_BLOCK
    T = Q_BLOCK
    return pl.pallas_call(
        _attn_p_kernel,
        grid=(BATCH, nqb),
        in_specs=[
            pl.BlockSpec((1, T, D_ATTN), lambda b, i: (b, i, C_Q // D_ATTN)),
            pl.BlockSpec((1, T, IDX_HEADS * IDX_DIM), lambda b, i: (b, i, C_QI // (IDX_HEADS * IDX_DIM))),
            pl.BlockSpec((1, T, LANE), lambda b, i: (b, i, C_KI // LANE)),
            pl.BlockSpec((1, SEQ, D_KV), lambda b, i: (b, 0, C_K // D_KV)),
            pl.BlockSpec((1, SEQ, D_KV), lambda b, i: (b, 0, C_V // D_KV)),
            pl.BlockSpec((1, SEQ, LANE), lambda b, i: (b, 0, C_KI // LANE)),
            pl.BlockSpec((N_HEADS, CHUNK, T), lambda b, i: (0, 0, 0)),
        ],
        out_specs=pl.BlockSpec((1, T, D_ATTN), lambda b, i: (b, i, 0)),
        out_shape=jax.ShapeDtypeStruct((BATCH, SEQ, D_ATTN), F32),
        scratch_shapes=[
            pltpu.VMEM((PAD_KEYS + SEQ, D_KV), BF16),
            pltpu.VMEM((D_KV, PAD_KEYS + SEQ), BF16),
            pltpu.VMEM((PAD_KEYS + SEQ, IDX_DIM), BF16),
            pltpu.VMEM((N_HEADS, T, HEAD_DIM), BF16),
            pltpu.VMEM((IDX_HEADS, T, IDX_DIM), BF16),
            pltpu.VMEM((PAD_KEYS + SEQ, T), F32),
            pltpu.VMEM((PAD_KEYS + SEQ, T), F32),
            pltpu.VMEM((N_KV_HEADS, 1, GROUP * T), F32),
            pltpu.VMEM((N_KV_HEADS, 1, GROUP * T), F32),
            pltpu.VMEM((N_KV_HEADS, HEAD_DIM, GROUP * T), F32),
        ],
        compiler_params=_params(("parallel", "arbitrary")),
        name="attn_prompt",
    )(h3, h3, h3, h3, h3, h3, bias_p)


def _score_s_kernel(pt_ref, qi_ref, w_ref, kin_ref, *rest):
    kip = rest[0:N_PAGES]
    o_ref = rest[N_PAGES]
    kinp_ref = rest[N_PAGES + 1]
    del pt_ref
    R = DEC_SEQ
    kinp_ref[...] = jnp.zeros(kinp_ref.shape, BF16)
    kinp_ref[0:2 * R] = jnp.concatenate([kin_ref[0][:, :IDX_DIM], jnp.zeros((R, IDX_DIM), F32)],
                                        0).astype(BF16)
    qi = qi_ref[0].astype(BF16)
    wb = jnp.broadcast_to(w_ref[0] * SCORE_SCALE, (IDX_HEADS * R, LANE))
    qrow = lax.broadcasted_iota(jnp.int32, (R, LANE), 0)
    lane = lax.broadcasted_iota(jnp.int32, (R, LANE), 1)
    for t in range(N_PAGES + 1):
        keys = kip[t][0].astype(BF16) if t < N_PAGES else kinp_ref[...]
        d = lax.dot_general(qi, keys, (((1,), (1,)), ((), ())), preferred_element_type=F32)
        e = (jnp.maximum(d, 0.0) * wb).reshape(IDX_HEADS, R, LANE)
        s = e[0]
        for h in range(1, IDX_HEADS):
            s = s + e[h]
        if t == N_PAGES:
            s = jnp.where(lane <= qrow, s, NEG_INF)
        o_ref[0, :, t * LANE:(t + 1) * LANE] = s


def _score_sample(h_s, page_table, cache_kidx):
    R = DEC_SEQ
    h3 = h_s.reshape(DEC_BATCH, R, D_H)
    qi_hq = h3[:, :, C_QI:C_QI + IDX_HEADS * IDX_DIM].reshape(DEC_BATCH, R, IDX_HEADS, IDX_DIM)
    qi_hq = qi_hq.transpose(0, 2, 1, 3).reshape(DEC_BATCH, IDX_HEADS * R, IDX_DIM)
    w_hq = h3[:, :, C_KI + W_OFF:C_KI + W_OFF + IDX_HEADS].transpose(0, 2, 1)
    w_hq = w_hq.reshape(DEC_BATCH, IDX_HEADS * R, 1)
    in_specs = [
        pl.BlockSpec((1, IDX_HEADS * R, IDX_DIM), lambda b, pt: (b, 0, 0)),
        pl.BlockSpec((1, IDX_HEADS * R, 1), lambda b, pt: (b, 0, 0)),
        pl.BlockSpec((1, R, LANE), lambda b, pt: (b, 0, C_KI // LANE)),
    ]
    in_specs += [pl.BlockSpec((1, PAGE_SIZE, IDX_DIM), lambda b, pt, p=p: (pt[b, p], 0, 0))
                 for p in range(N_PAGES)]
    grid_spec = pltpu.PrefetchScalarGridSpec(
        num_scalar_prefetch=1,
        grid=(DEC_BATCH,),
        in_specs=in_specs,
        out_specs=pl.BlockSpec((1, R, L_SAMPLE), lambda b, pt: (b, 0, 0)),
        scratch_shapes=[pltpu.VMEM((PAGE_SIZE, IDX_DIM), BF16)],
    )
    return pl.pallas_call(
        _score_s_kernel,
        grid_spec=grid_spec,
        out_shape=jax.ShapeDtypeStruct((DEC_BATCH, R, L_SAMPLE), F32),
        compiler_params=_params(("arbitrary",)),
        name="score_sample",
    )(page_table, qi_hq, w_hq, h3, *([cache_kidx] * N_PAGES))


def _select_s_kernel(s_ref, o_ref):
    nt = N_PAGES + 1

    def tile_fn(j):
        return s_ref[j * LANE:(j + 1) * LANE, :].reshape(SUBTILES, SUBLANE, LANE)

    def key_fn(j):
        return (j * LANE + _key_iota(SUBTILES)).astype(F32)

    def write_mask(mask_fn):
        for j in range(nt):
            o_ref[j * LANE:(j + 1) * LANE, :] = mask_fn(tile_fn(j), j).reshape(LANE, LANE)

    thr, need, use_ties, count = _select_threshold(tile_fn, nt, static=True)

    @pl.when(jnp.logical_not(use_ties))
    def _():
        write_mask(_mask_plain(thr))

    @pl.when(use_ties)
    def _():
        write_mask(_mask_ties(key_fn, thr, _tie_cut(count, key_fn, thr, need, L_SAMPLE)))


def _select_sample(scores_t):
    n = scores_t.shape[1]
    return pl.pallas_call(
        _select_s_kernel,
        grid=(n // LANE,),
        in_specs=[pl.BlockSpec((L_SAMPLE, LANE), lambda c: (0, c))],
        out_specs=pl.BlockSpec((L_SAMPLE, LANE), lambda c: (0, c)),
        out_shape=jax.ShapeDtypeStruct((L_SAMPLE, n), F32),
        compiler_params=_params(("parallel",)),
        name="select_sample",
    )(scores_t)


def _attn_s_kernel(pt_ref, q_ref, kn_ref, vn_ref, mask_ref, bias_ref, *rest):
    kp = rest[0:N_PAGES]
    vp = rest[N_PAGES:2 * N_PAGES]
    o_ref = rest[2 * N_PAGES]
    knp_ref, vnp_ref, logit_ref = rest[2 * N_PAGES + 1:]
    del pt_ref
    R = DEC_SEQ
    NT = N_PAGES + 1
    GR = GROUP * R

    knp_ref[...] = jnp.zeros(knp_ref.shape, BF16)
    vnp_ref[...] = jnp.zeros(vnp_ref.shape, BF16)
    knp_ref[0:2 * R] = jnp.concatenate([kn_ref[0], jnp.zeros((R, D_KV), F32)], 0).astype(BF16)
    vnp_ref[0:2 * R] = jnp.concatenate([vn_ref[0], jnp.zeros((R, D_KV), F32)], 0).astype(BF16)

    def page_head(refs, t, g):
        if t < N_PAGES:
            return refs[t][pl.ds(g, PAGE_SIZE, stride=N_KV_HEADS), :].astype(BF16)
        pad = knp_ref if refs is kp else vnp_ref
        return pad[:, g * HEAD_DIM:(g + 1) * HEAD_DIM]

    q = (q_ref[0] * QK_SCALE).astype(BF16)
    for t in range(NT):
        mb = mask_ref[0, :, t * LANE:(t + 1) * LANE]
        for g in range(N_KV_HEADS):
            lg = lax.dot_general(q[g * GR:(g + 1) * GR], page_head(kp, t, g),
                                 (((1,), (1,)), ((), ())), preferred_element_type=F32)
            lg = lg.reshape(GROUP, R, LANE) + mb[None]
            if t >= N_PAGES - 1:
                off = (t - (N_PAGES - 1)) * LANE
                lg = lg + bias_ref[g * GROUP:(g + 1) * GROUP, :, off:off + LANE]
            logit_ref[g * GR:(g + 1) * GR, t * LANE:(t + 1) * LANE] = lg.reshape(GR, LANE)

    logits = logit_ref[...]
    m = jnp.max(logits, axis=1, keepdims=True)
    p = jnp.exp(logits - m)
    inv = 1.0 / jnp.sum(p, axis=1, keepdims=True)
    pb = p.astype(BF16)
    outs = [jnp.zeros((GR, HEAD_DIM), F32) for _ in range(N_KV_HEADS)]
    for t in range(NT):
        for g in range(N_KV_HEADS):
            outs[g] = outs[g] + jnp.dot(pb[g * GR:(g + 1) * GR, t * LANE:(t + 1) * LANE],
                                        page_head(vp, t, g), preferred_element_type=F32)
    for g in range(N_KV_HEADS):
        o = outs[g] * inv[g * GR:(g + 1) * GR]
        for hq in range(GROUP):
            h = g * GROUP + hq
            o_ref[0, :, h * HEAD_DIM:(h + 1) * HEAD_DIM] = o[hq * R:(hq + 1) * R]


def _attn_sample(h_s, page_table, cache_k, cache_v, mask, bias_s):
    R = DEC_SEQ
    h3 = h_s.reshape(DEC_BATCH, R, D_H)
    q_hq = h3[:, :, C_Q:C_Q + D_ATTN].reshape(DEC_BATCH, R, N_HEADS, HEAD_DIM)
    q_hq = q_hq.transpose(0, 2, 1, 3).reshape(DEC_BATCH, N_HEADS * R, HEAD_DIM)
    rows_per_page = PAGE_SIZE * N_KV_HEADS
    ck = cache_k.reshape(-1, HEAD_DIM)
    cv = cache_v.reshape(-1, HEAD_DIM)

    in_specs = [
        pl.BlockSpec((1, N_HEADS * R, HEAD_DIM), lambda b, pt: (b, 0, 0)),
        pl.BlockSpec((1, R, D_KV), lambda b, pt: (b, 0, C_K // D_KV)),
        pl.BlockSpec((1, R, D_KV), lambda b, pt: (b, 0, C_V // D_KV)),
        pl.BlockSpec((1, R, L_SAMPLE), lambda b, pt: (b, 0, 0)),
        pl.BlockSpec((N_HEADS, R, 2 * LANE), lambda b, pt: (0, 0, 0)),
    ]
    page = [pl.BlockSpec((rows_per_page, HEAD_DIM), lambda b, pt, p=p: (pt[b, p], 0))
            for p in range(N_PAGES)]
    in_specs += page + page
    grid_spec = pltpu.PrefetchScalarGridSpec(
        num_scalar_prefetch=1,
        grid=(DEC_BATCH,),
        in_specs=in_specs,
        out_specs=pl.BlockSpec((1, R, D_ATTN), lambda b, pt: (b, 0, 0)),
        scratch_shapes=[
            pltpu.VMEM((PAGE_SIZE, D_KV), BF16),
            pltpu.VMEM((PAGE_SIZE, D_KV), BF16),
            pltpu.VMEM((N_HEADS * R, L_SAMPLE), F32),
        ],
    )
    return pl.pallas_call(
        _attn_s_kernel,
        grid_spec=grid_spec,
        out_shape=jax.ShapeDtypeStruct((DEC_BATCH, R, D_ATTN), F32),
        compiler_params=_params(("arbitrary",)),
        name="attn_sample",
    )(page_table, q_hq, h3, h3, mask, bias_s, *([ck] * N_PAGES), *([cv] * N_PAGES))


def _conv_p_kernel(val_ref, gate_ref, cw_ref, cb_ref, dw_ref, ut_ref, pad_ref):
    pad_ref[0:PAD_ROWS] = jnp.zeros((PAD_ROWS, LANE), F32)
    pad_ref[PAD_ROWS:] = val_ref[0] * jax.nn.sigmoid(gate_ref[0])
    ut_ref[0] = pad_ref[SEQ:SEQ + PAD_ROWS]
    cw = cw_ref[...]
    cb = cb_ref[...]
    first = PAD_ROWS - (CONV_WIDTH - 1)
    for c in range(SEQ // CONV_CHUNK):
        base = c * CONV_CHUNK
        acc = jnp.broadcast_to(cb, (CONV_CHUNK, LANE))
        for r in range(SUBLANE):
            taps = [w for w in range(CONV_WIDTH) if (first + w) % SUBLANE == r]
            span = max(first + w - r for w in taps) + CONV_CHUNK
            win = pad_ref[base + r:base + r + span]
            for w in taps:
                a = first + w - r
                acc = acc + win[a:a + CONV_CHUNK] * cw[w:w + 1]
        dw_ref[0, base:base + CONV_CHUNK] = acc


def _conv_prompt(h_p, cw_pad, cb):
    h3 = h_p.reshape(BATCH, SEQ, D_H)
    nc = D_CONV // LANE
    return pl.pallas_call(
        _conv_p_kernel,
        grid=(BATCH, nc),
        in_specs=[pl.BlockSpec((1, SEQ, LANE), lambda b, c: (b, 0, C_GV // LANE + c)),
                  pl.BlockSpec((1, SEQ, LANE), lambda b, c: (b, 0, C_GG // LANE + c)),
                  pl.BlockSpec((PAD_ROWS, LANE), lambda b, c: (0, c)),
                  pl.BlockSpec((1, LANE), lambda b, c: (0, c))],
        out_specs=[pl.BlockSpec((1, SEQ, LANE), lambda b, c: (b, 0, c)),
                   pl.BlockSpec((1, PAD_ROWS, LANE), lambda b, c: (b, 0, c))],
        out_shape=[jax.ShapeDtypeStruct((BATCH, SEQ, D_CONV), F32),
                   jax.ShapeDtypeStruct((BATCH, PAD_ROWS, D_CONV), F32)],
        scratch_shapes=[pltpu.VMEM((PAD_ROWS + SEQ, LANE), F32)],
        compiler_params=_params(("parallel", "parallel")),
        name="conv_prompt",
    )(h3, h3, cw_pad, cb)


def _conv_s_kernel(val_ref, gate_ref, st_ref, cw_ref, cb_ref, dw_ref, u_ref, pad_ref):
    R = DEC_SEQ
    u = val_ref[...] * jax.nn.sigmoid(gate_ref[...])
    u_ref[...] = u
    first = PAD_ROWS - (CONV_WIDTH - 1)
    pad_ref[:, 0:SUBLANE] = jnp.zeros((DEC_BATCH, SUBLANE, LANE), F32)
    pad_ref[:, first:PAD_ROWS] = st_ref[...]
    pad_ref[:, PAD_ROWS:PAD_ROWS + R] = u
    cw = cw_ref[...]
    acc = jnp.broadcast_to(cb_ref[...][None], (DEC_BATCH, R, LANE))
    for w in range(CONV_WIDTH):
        acc = acc + pad_ref[:, first + w:first + w + R] * cw[w:w + 1][None]
    dw_ref[...] = acc


def _conv_sample(h_s, state, cw_pad, cb):
    R = DEC_SEQ
    h3 = h_s.reshape(DEC_BATCH, R, D_H)
    nc = D_CONV // LANE
    blk = pl.BlockSpec((DEC_BATCH, R, LANE), lambda c: (0, 0, c))
    return pl.pallas_call(
        _conv_s_kernel,
        grid=(nc,),
        in_specs=[pl.BlockSpec((DEC_BATCH, R, LANE), lambda c: (0, 0, C_GV // LANE + c)),
                  pl.BlockSpec((DEC_BATCH, R, LANE), lambda c: (0, 0, C_GG // LANE + c)),
                  pl.BlockSpec((DEC_BATCH, CONV_WIDTH - 1, LANE), lambda c: (0, 0, c)),
                  pl.BlockSpec((PAD_ROWS, LANE), lambda c: (0, c)),
                  pl.BlockSpec((1, LANE), lambda c: (0, c))],
        out_specs=[blk, blk],
        out_shape=[jax.ShapeDtypeStruct((DEC_BATCH, R, D_CONV), F32),
                   jax.ShapeDtypeStruct((DEC_BATCH, R, D_CONV), F32)],
        scratch_shapes=[pltpu.VMEM((DEC_BATCH, PAD_ROWS + R, LANE), F32)],
        compiler_params=_params(("parallel",)),
        name="conv_sample",
    )(h3, h3, state, cw_pad, cb)


def _tail_kernel(attn_ref, za_ref, dw_ref, zc_ref, ga_ref, gc_ref, x_ref,
                 wua_ref, wpw_ref, wuc_ref, wo_ref, ng_ref, nb_ref, bpw_ref, fg_ref, y_ref):
    a = attn_ref[...] * jax.nn.silu(za_ref[...])
    branch_attn = jnp.dot(a.astype(BF16), wua_ref[...], preferred_element_type=F32)

    dw = dw_ref[...]
    mu = jnp.mean(dw, axis=-1, keepdims=True)
    var = jnp.mean(jnp.square(dw - mu), axis=-1, keepdims=True)
    ln = (dw - mu) * lax.rsqrt(var + EPS) * ng_ref[...] + nb_ref[...]
    conv_out = jnp.dot(jax.nn.silu(ln).astype(BF16), wpw_ref[...],
                       preferred_element_type=F32) + bpw_ref[...]
    c = conv_out * jax.nn.silu(zc_ref[...])
    branch_conv = jnp.dot(c.astype(BF16), wuc_ref[...], preferred_element_type=F32)

    merged = jax.nn.sigmoid(ga_ref[...]) * branch_attn + jax.nn.sigmoid(gc_ref[...]) * branch_conv
    y = x_ref[...] + jnp.dot(merged.astype(BF16), wo_ref[...], preferred_element_type=F32)
    ms = jnp.mean(y * y, axis=-1, keepdims=True)
    y_ref[...] = y * lax.rsqrt(ms + EPS) * fg_ref[...]


def _tail(h, attn, dw, x2d, wua, wpw, wuc, wo, ng, nb, bpw, fg, tm):
    n = x2d.shape[0]

    def const(shape):
        return pl.BlockSpec(shape, lambda i: (0, 0), pipeline_mode=pl.Buffered(1))

    return pl.pallas_call(
        _tail_kernel,
        grid=(n // tm,),
        in_specs=[pl.BlockSpec((tm, D_ATTN), lambda i: (i, 0)),
                  pl.BlockSpec((tm, D_ATTN), lambda i: (i, C_ZA // D_ATTN)),
                  pl.BlockSpec((tm, D_CONV), lambda i: (i, 0)),
                  pl.BlockSpec((tm, D_CONV), lambda i: (i, C_ZC // D_CONV)),
                  pl.BlockSpec((tm, D_MODEL), lambda i: (i, C_GA // D_MODEL)),
                  pl.BlockSpec((tm, D_MODEL), lambda i: (i, C_GC // D_MODEL)),
                  pl.BlockSpec((tm, D_MODEL), lambda i: (i, 0)),
                  const((D_ATTN, D_MODEL)), const((D_CONV, D_CONV)),
                  const((D_CONV, D_MODEL)), const((D_MODEL, D_MODEL)),
                  const((1, D_CONV)), const((1, D_CONV)), const((1, D_CONV)),
                  const((1, D_MODEL))],
        out_specs=pl.BlockSpec((tm, D_MODEL), lambda i: (i, 0)),
        out_shape=jax.ShapeDtypeStruct((n, D_MODEL), F32),
        compiler_params=_params(("parallel",)),
        name="tail",
    )(attn, h, dw, h, h, h, x2d, wua, wpw, wuc, wo, ng, nb, bpw, fg)


def _reorder_w_in(w_in):
    def col(name):
        o, w = _SRC[name]
        return w_in[:, o:o + w]

    pad = jnp.zeros((D_MODEL, LANE - IDX_DIM - IDX_HEADS), w_in.dtype)
    w = jnp.concatenate([col('gate_attn'), col('gate_conv'), col('q'), col('z_attn'),
                         col('glu_val'), col('glu_gate'), col('z_conv'), col('k'), col('v'),
                         col('q_idx'), col('k_idx'), col('w_idx'), pad], axis=1)
    return w.astype(BF16)


def kernel(x_prompt, x_sample, cache_k, cache_v, cache_kidx, state_conv, page_table,
           ln_g, w_in, conv_w, conv_b, conv_norm_g, conv_norm_b, w_pw, b_pw,
           w_up_attn, w_up_conv, w_out, rel_bias, final_g):
    w_all = _reorder_w_in(w_in[0])
    g_in = ln_g[0].reshape(1, D_MODEL)
    xp = x_prompt.reshape(BATCH * SEQ, D_MODEL)
    xs = x_sample.reshape(DEC_BATCH * DEC_SEQ, D_MODEL)
    h_p = _proj(xp, g_in, w_all, 512, 1152)
    h_s = _proj(xs, g_in, w_all, 512, 1152)

    bias_p, bias_s = _bias_tables(rel_bias)

    attn_p = _attn_prompt(h_p, bias_p).reshape(BATCH * SEQ, D_ATTN)

    n_s = DEC_BATCH * DEC_SEQ
    scores = _score_sample(h_s, page_table, cache_kidx[0])
    mask_t = _select_sample(scores.reshape(n_s, L_SAMPLE).T)
    mask = mask_t.T.reshape(DEC_BATCH, DEC_SEQ, L_SAMPLE)
    attn_s = _attn_sample(h_s, page_table, cache_k[0], cache_v[0], mask, bias_s)
    attn_s = attn_s.reshape(n_s, D_ATTN)

    cw_pad = jnp.concatenate([conv_w[0], jnp.zeros((PAD_ROWS - CONV_WIDTH, D_CONV), F32)], 0)
    cb = conv_b[0].reshape(1, D_CONV)
    dw_p, u_tail = _conv_prompt(h_p, cw_pad, cb)
    dw_s, u_s = _conv_sample(h_s, state_conv[0], cw_pad, cb)

    wua = w_up_attn[0].astype(BF16)
    wpw = w_pw[0].astype(BF16)
    wuc = w_up_conv[0].astype(BF16)
    wo = w_out[0].astype(BF16)
    ng = conv_norm_g[0].reshape(1, D_CONV)
    nb = conv_norm_b[0].reshape(1, D_CONV)
    bpw = b_pw[0].reshape(1, D_CONV)
    fg = final_g.reshape(1, D_MODEL)
    y_p = _tail(h_p, attn_p, dw_p.reshape(BATCH * SEQ, D_CONV), xp,
                wua, wpw, wuc, wo, ng, nb, bpw, fg, 256)
    y_s = _tail(h_s, attn_s, dw_s.reshape(n_s, D_CONV), xs,
                wua, wpw, wuc, wo, ng, nb, bpw, fg, 256)

    hp3 = h_p.reshape(BATCH, SEQ, D_H)
    hs3 = h_s.reshape(DEC_BATCH, DEC_SEQ, D_H)
    tail_rows = CONV_WIDTH - 1
    return (
        y_p.reshape(BATCH, SEQ, D_MODEL),
        y_s.reshape(DEC_BATCH, DEC_SEQ, D_MODEL),
        hp3[:, :, C_K:C_K + D_KV].reshape(1, BATCH, SEQ, N_KV_HEADS, HEAD_DIM),
        hp3[:, :, C_V:C_V + D_KV].reshape(1, BATCH, SEQ, N_KV_HEADS, HEAD_DIM),
        hp3[:, :, C_KI:C_KI + IDX_DIM].reshape(1, BATCH, SEQ, IDX_DIM),
        u_tail[:, PAD_ROWS - tail_rows:].reshape(1, BATCH, tail_rows, D_CONV),
        hs3[:, :, C_K:C_K + D_KV].reshape(1, DEC_BATCH, DEC_SEQ, N_KV_HEADS, HEAD_DIM),
        hs3[:, :, C_V:C_V + D_KV].reshape(1, DEC_BATCH, DEC_SEQ, N_KV_HEADS, HEAD_DIM),
        hs3[:, :, C_KI:C_KI + IDX_DIM].reshape(1, DEC_BATCH, DEC_SEQ, IDX_DIM),
        jnp.concatenate([state_conv[0][:, DEC_SEQ:], u_s], axis=1).reshape(
            1, DEC_BATCH, tail_rows, D_CONV),
    )
```

```python
import math

import numpy as np
import jax
import jax.numpy as jnp
from jax import lax
from jax.experimental import pallas as pl
from jax.experimental.pallas import tpu as pltpu

F32 = jnp.float32
BF16 = jnp.bfloat16

D_MODEL = 2048
BATCH = 8
SEQ = 2048
DEC_BATCH = 128
DEC_SEQ = 8
PAST_LEN = 2048
PAGE_SIZE = 128
N_PAGES = PAST_LEN // PAGE_SIZE
N_HEADS = 8
N_KV_HEADS = 2
HEAD_DIM = 128
GROUP = N_HEADS // N_KV_HEADS
D_ATTN = N_HEADS * HEAD_DIM
D_KV = N_KV_HEADS * HEAD_DIM
IDX_HEADS = 8
IDX_DIM = 64
TOPK = 256
Q_BLOCK = 128
N_BUCKETS = 32
MAX_DISTANCE = 128
D_CONV = D_MODEL // 2
CONV_WIDTH = 31
EPS = 1e-6
NEG = -1e30
NEG_INF = float("-inf")

LANE = 128
SUBLANE = 8
VMEM_LIMIT = 56 * 1024 * 1024

C_GA = 0
C_GC = C_GA + D_MODEL
C_Q = C_GC + D_MODEL
C_ZA = C_Q + D_ATTN
C_GV = C_ZA + D_ATTN
C_GG = C_GV + D_CONV
C_ZC = C_GG + D_CONV
C_K = C_ZC + D_CONV
C_V = C_K + D_KV
C_QI = C_V + D_KV
C_KI = C_QI + IDX_HEADS * IDX_DIM
D_H = C_KI + LANE
W_OFF = IDX_DIM

_SRC = {}
_off = 0
for _name, _w in (('q', D_ATTN), ('k', D_KV), ('v', D_KV), ('z_attn', D_ATTN),
                  ('q_idx', IDX_HEADS * IDX_DIM), ('k_idx', IDX_DIM), ('w_idx', IDX_HEADS),
                  ('glu_val', D_CONV), ('glu_gate', D_CONV), ('z_conv', D_CONV),
                  ('gate_attn', D_MODEL), ('gate_conv', D_MODEL)):
    _SRC[_name] = (_off, _w)
    _off += _w

N_BISECT = 20
PAD_ROWS = 32
CONV_CHUNK = 64
SCORE_SCALE = (IDX_DIM ** -0.5) * (IDX_HEADS ** -0.5)
QK_SCALE = HEAD_DIM ** -0.5
LOG2E = math.log2(math.e)
L_SAMPLE = (N_PAGES + 1) * LANE
SUBTILES = LANE // SUBLANE
CHUNK_BLOCKS = 4
CHUNK = CHUNK_BLOCKS * Q_BLOCK
PAD_KEYS = CHUNK - Q_BLOCK
SCORE_ROWS = 4
PROJ_TM = 1024
PROJ_TN = 1152
TAIL_TM = 256


def _t5_bucket_static(dist):
    n = np.maximum(dist, 0)
    max_exact = N_BUCKETS // 2
    ratio = (np.log(np.maximum(n, 1).astype(np.float32) / np.float32(max_exact))
             / np.float32(math.log(MAX_DISTANCE / max_exact)))
    large = np.minimum(max_exact + (ratio * np.float32(N_BUCKETS - max_exact)).astype(np.int32),
                       N_BUCKETS - 1)
    return np.where(n < max_exact, n, large).astype(np.int32)


FAR_BUCKET = int(_t5_bucket_static(np.array([2 * MAX_DISTANCE]))[0])


def _params(sem):
    return pltpu.CompilerParams(dimension_semantics=sem, vmem_limit_bytes=VMEM_LIMIT)


def _proj_kernel(x_ref, g_ref, w_ref, o_ref, xn_ref):
    @pl.when(pl.program_id(1) == 0)
    def _():
        x = x_ref[...]
        ms = jnp.mean(x * x, axis=-1, keepdims=True)
        xn_ref[...] = (x * lax.rsqrt(ms + EPS) * g_ref[...]).astype(BF16)

    o_ref[...] = lax.dot_general(xn_ref[...], w_ref[...], (((1,), (1,)), ((), ())),
                                 preferred_element_type=F32)


def _proj(x2d, g, w, tm, tn):
    n = x2d.shape[0]
    return pl.pallas_call(
        _proj_kernel,
        grid=(n // tm, D_H // tn),
        in_specs=[pl.BlockSpec((tm, D_MODEL), lambda i, j: (i, 0)),
                  pl.BlockSpec((1, D_MODEL), lambda i, j: (0, 0)),
                  pl.BlockSpec((tn, D_MODEL), lambda i, j: (j, 0))],
        out_specs=pl.BlockSpec((tm, tn), lambda i, j: (i, j)),
        out_shape=jax.ShapeDtypeStruct((n, D_H), F32),
        scratch_shapes=[pltpu.VMEM((tm, D_MODEL), BF16)],
        compiler_params=_params(("parallel", "arbitrary")),
        name="proj",
    )(x2d, g, w)


def _bias_kernel(rb_ref, bp_ref, bs_ref, op_ref, os_ref):
    bp = bp_ref[...]
    bs = bs_ref[...]
    for h in range(N_HEADS):
        far = rb_ref[FAR_BUCKET, h]
        tp = jnp.zeros(bp.shape, F32)
        ts = jnp.zeros(bs.shape, F32)
        for b in range(N_BUCKETS):
            val = rb_ref[b, h] - far
            tp = jnp.where(bp == b, val * LOG2E, tp)
            ts = jnp.where(bs == b, val, ts)
        op_ref[h] = tp
        os_ref[h] = ts


def _bias_tables(rel_bias):
    key = np.arange(CHUNK)[:, None]
    qry = np.arange(Q_BLOCK)[None, :]
    bucket_p = _t5_bucket_static(CHUNK - Q_BLOCK + qry - key)
    qi = np.arange(DEC_SEQ)[:, None]
    col = np.arange(2 * LANE)[None, :]
    dist_s = np.where(col < LANE, LANE + qi - col, qi - (col - LANE))
    bucket_s = _t5_bucket_static(dist_s)
    return pl.pallas_call(
        _bias_kernel,
        in_specs=[pl.BlockSpec(memory_space=pltpu.SMEM),
                  pl.BlockSpec(memory_space=pltpu.VMEM),
                  pl.BlockSpec(memory_space=pltpu.VMEM)],
        out_specs=[pl.BlockSpec(memory_space=pltpu.VMEM),
                   pl.BlockSpec(memory_space=pltpu.VMEM)],
        out_shape=[jax.ShapeDtypeStruct((N_HEADS, CHUNK, Q_BLOCK), F32),
                   jax.ShapeDtypeStruct((N_HEADS, DEC_SEQ, 2 * LANE), F32)],
        name="bias_tables",
    )(rel_bias, jnp.asarray(bucket_p), jnp.asarray(bucket_s))


def _any(x):
    return jnp.max(jnp.where(x, 1.0, 0.0)) > 0.5


def _rep(x):
    return jnp.broadcast_to(x, (SUBLANE, LANE))


def _fold_rows(x, comb):
    parts = [x[k:k + SUBLANE] for k in range(0, x.shape[0], SUBLANE)]
    while len(parts) > 1:
        parts = [comb(parts[k], parts[k + 1]) for k in range(0, len(parts), 2)]
    return parts[0]


def _key_iota(nsub):
    sub = lax.broadcasted_iota(jnp.int32, (nsub, SUBLANE, LANE), 0) * SUBLANE
    return sub + lax.broadcasted_iota(jnp.int32, (nsub, SUBLANE, LANE), 1)


def _select_threshold(tile_fn, ntiles, static):
    def reduce_tiles(fn, init, comb):
        def step(j, acc):
            x = fn(tile_fn(j), j)
            parts = [x[k] for k in range(x.shape[0])]
            while len(parts) > 1:
                parts = [comb(parts[k], parts[k + 1]) for k in range(0, len(parts), 2)]
            return comb(acc, parts[0])
        if static:
            acc = init
            for j in range(ntiles):
                acc = step(j, acc)
            return acc
        return lax.fori_loop(0, ntiles, step, init)

    zeros = jnp.zeros((SUBLANE, LANE), F32)

    def count(pred_fn):
        acc = reduce_tiles(lambda s, j: jnp.where(pred_fn(s, j), 1.0, 0.0), zeros,
                           lambda a, b: a + b)
        return _rep(jnp.sum(acc, axis=0, keepdims=True))

    def masked_max(pred_fn):
        acc = reduce_tiles(lambda s, j: jnp.where(pred_fn(s, j), s, NEG_INF),
                           jnp.full((SUBLANE, LANE), NEG_INF, F32), jnp.maximum)
        return _rep(jnp.max(acc, axis=0, keepdims=True))

    bound = reduce_tiles(lambda s, j: jnp.where(s > NEG_INF, jnp.abs(s), 0.0), zeros,
                         jnp.maximum)
    bound = _rep(jnp.max(bound, axis=0, keepdims=True))

    def bisect(_, carry):
        lo, hi = carry
        mid = 0.5 * lo + 0.5 * hi
        few = count(lambda s, j: s > mid[None]) < TOPK
        return jnp.where(few, lo, mid), jnp.where(few, mid, hi)

    _, hi = lax.fori_loop(0, N_BISECT, bisect, (-bound, bound))

    thr = masked_max(lambda s, j: s <= hi[None])
    n_ge = count(lambda s, j: s >= thr[None])

    def fix_body(carry):
        thr, n_ge, _ = carry
        lower = masked_max(lambda s, j: s < thr[None])
        thr = jnp.where(n_ge < TOPK, lower, thr)
        n_ge = count(lambda s, j: s >= thr[None])
        return thr, n_ge, _any(n_ge < TOPK)

    thr, n_ge, _ = lax.while_loop(lambda c: c[2], fix_body, (thr, n_ge, _any(n_ge < TOPK)))
    n_gt = count(lambda s, j: s > thr[None])
    need = TOPK - n_gt
    use_ties = _any(n_ge - n_gt > need)
    return thr, need, use_ties, count


def _tie_cut(count, key_fn, thr, need, nkeys):
    steps = int(math.ceil(math.log2(nkeys))) + 1

    def body(_, carry):
        lo, hi = carry
        mid = jnp.floor((lo + hi) * 0.5)
        enough = count(lambda s, j: jnp.where(s == thr[None], key_fn(j), nkeys + 1.0)
                       <= mid[None]) >= need
        return jnp.where(enough, lo, mid), jnp.where(enough, mid, hi)

    lo0 = jnp.full((SUBLANE, LANE), -1.0, F32)
    hi0 = jnp.full((SUBLANE, LANE), nkeys - 1.0, F32)
    _, hi = lax.fori_loop(0, steps, body, (lo0, hi0))
    return hi


def _keep(pred):
    return jnp.where(pred, 0.0, NEG)


def _mask_plain(thr):
    return lambda s, j: _keep(s >= thr[None])


def _mask_ties(key_fn, thr, cut):
    return lambda s, j: jnp.where(s == thr[None], _keep(key_fn(j) <= cut[None]),
                                  _keep(s > thr[None]))


def _attn_p_kernel(q_ref, qi_ref, wi_ref, k_ref, v_ref, ki_ref, bias_ref, o_ref,
                   kb_ref, vt_ref, kib_ref, qh_ref, qih_ref,
                   score_ref, mask_ref, s_ref, m_ref, l_ref, acc_ref):
    i = pl.program_id(1)
    T = Q_BLOCK
    W = GROUP * T
    nch = i // CHUNK_BLOCKS + 1

    @pl.when(i == 0)
    def _():
        kb_ref[0:PAD_KEYS] = jnp.zeros((PAD_KEYS, D_KV), BF16)
        kb_ref[PAD_KEYS:] = k_ref[0].astype(BF16)
        kib_ref[0:PAD_KEYS] = jnp.zeros((PAD_KEYS, IDX_DIM), BF16)
        kib_ref[PAD_KEYS:] = ki_ref[0][:, :IDX_DIM].astype(BF16)
        vt_ref[:, 0:PAD_KEYS] = jnp.zeros((D_KV, PAD_KEYS), BF16)
        for g in range(N_KV_HEADS):
            for c in range(SEQ // LANE):
                blk = v_ref[0, c * LANE:(c + 1) * LANE, g * HEAD_DIM:(g + 1) * HEAD_DIM]
                vt_ref[g * HEAD_DIM:(g + 1) * HEAD_DIM,
                       PAD_KEYS + c * LANE:PAD_KEYS + (c + 1) * LANE] = blk.T.astype(BF16)
        score_ref[0:PAD_KEYS] = jnp.full((PAD_KEYS, T), NEG_INF, F32)
        mask_ref[0:PAD_KEYS] = jnp.full((PAD_KEYS, T), NEG, F32)

    q = q_ref[0] * (QK_SCALE * LOG2E)
    for h in range(N_HEADS):
        qh_ref[h] = q[:, h * HEAD_DIM:(h + 1) * HEAD_DIM].astype(BF16)
    qi = qi_ref[0]
    for h in range(IDX_HEADS):
        qih_ref[h] = qi[:, h * IDX_DIM:(h + 1) * IDX_DIM].astype(BF16)
    w_rows = wi_ref[0].T[W_OFF:W_OFF + IDX_HEADS] * SCORE_SCALE

    def span(c):
        return pl.ds(pl.multiple_of((i - CHUNK_BLOCKS * c) * LANE, LANE), CHUNK)

    def first_key(c):
        return (i - CHUNK_BLOCKS * c) * LANE - PAD_KEYS

    key_l = lax.broadcasted_iota(jnp.int32, (CHUNK, T), 0)
    qry = i * T + lax.broadcasted_iota(jnp.int32, (CHUNK, T), 1)

    def score_body(c, _):
        kc = kib_ref[span(c), :]
        d = lax.dot_general(kc, qih_ref[...].reshape(IDX_HEADS * T, IDX_DIM),
                            (((1,), (1,)), ((), ())), preferred_element_type=F32)
        acc = jnp.zeros((CHUNK, T), F32)
        for h in range(IDX_HEADS):
            acc = acc + jnp.maximum(d[:, h * T:(h + 1) * T], 0.0) * w_rows[h:h + 1]
        key = first_key(c) + key_l
        acc = jnp.where(key <= qry, jnp.where(key >= 0, acc, NEG_INF), NEG_INF)
        score_ref[span(c), :] = acc
        return 0

    lax.fori_loop(0, nch, score_body, 0)

    nsub = CHUNK // SUBLANE

    def tile_fn(c):
        return score_ref[span(c), :].reshape(nsub, SUBLANE, T)

    def key_fn(c):
        return (first_key(c) + _key_iota(nsub)).astype(F32)

    def write_mask(mask_fn):
        def body(c, _):
            mask_ref[span(c), :] = mask_fn(tile_fn(c), c).reshape(CHUNK, T)
            return 0
        lax.fori_loop(0, nch, body, 0)

    @pl.when(i * T + T <= TOPK)
    def _():
        write_mask(lambda s, c: _keep(s > NEG_INF))

    @pl.when(i * T + T > TOPK)
    def _():
        thr, need, use_ties, count = _select_threshold(tile_fn, nch, static=False)

        @pl.when(jnp.logical_not(use_ties))
        def _():
            write_mask(_mask_plain(thr))

        @pl.when(use_ties)
        def _():
            write_mask(_mask_ties(key_fn, thr, _tie_cut(count, key_fn, thr, need, SEQ)))

    m_ref[...] = jnp.full(m_ref.shape, NEG, F32)
    l_ref[...] = jnp.zeros(l_ref.shape, F32)
    acc_ref[...] = jnp.zeros(acc_ref.shape, F32)

    def logits(c, with_bias):
        mb = mask_ref[span(c), :]
        for g in range(N_KV_HEADS):
            kc = kb_ref[span(c), g * HEAD_DIM:(g + 1) * HEAD_DIM]
            qg = qh_ref[g * GROUP:(g + 1) * GROUP].reshape(W, HEAD_DIM)
            s = lax.dot_general(kc, qg, (((1,), (1,)), ((), ())), preferred_element_type=F32)
            if with_bias:
                add = jnp.concatenate([mb + bias_ref[g * GROUP + hq] for hq in range(GROUP)],
                                      axis=1)
            else:
                add = jnp.concatenate([mb] * GROUP, axis=1)
            s = s + add
            s_ref[span(c), g * W:(g + 1) * W] = s
            top = jnp.max(_fold_rows(s, jnp.maximum), axis=0, keepdims=True)
            m_ref[g] = jnp.maximum(m_ref[g], top)

    logits(0, True)

    def logits_body(c, _):
        logits(c, False)
        return 0

    lax.fori_loop(1, nch, logits_body, 0)

    def weigh(c, _):
        for g in range(N_KV_HEADS):
            p = jnp.exp2(s_ref[span(c), g * W:(g + 1) * W] - m_ref[g])
            l_ref[g] += jnp.sum(_fold_rows(p, jnp.add), axis=0, keepdims=True)
            vtc = vt_ref[g * HEAD_DIM:(g + 1) * HEAD_DIM, span(c)]
            acc_ref[g] += jnp.dot(vtc, p.astype(BF16), preferred_element_type=F32)
        return 0

    lax.fori_loop(0, nch, weigh, 0)

    for g in range(N_KV_HEADS):
        o = acc_ref[g] / l_ref[g]
        for hq in range(GROUP):
            h = g * GROUP + hq
            o_ref[0, :, h * HEAD_DIM:(h + 1) * HEAD_DIM] = o[:, hq * T:(hq + 1) * T].T


def _attn_prompt(h_p, bias_p):
    h3 = h_p.reshape(BATCH, SEQ, D_H)
    nqb = SEQ // Q_BLOCK
    T = Q_BLOCK
    return pl.pallas_call(
        _attn_p_kernel,
        grid=(BATCH, nqb),
        in_specs=[
            pl.BlockSpec((1, T, D_ATTN), lambda b, i: (b, i, C_Q // D_ATTN)),
            pl.BlockSpec((1, T, IDX_HEADS * IDX_DIM), lambda b, i: (b, i, C_QI // (IDX_HEADS * IDX_DIM))),
            pl.BlockSpec((1, T, LANE), lambda b, i: (b, i, C_KI // LANE)),
            pl.BlockSpec((1, SEQ, D_KV), lambda b, i: (b, 0, C_K // D_KV)),
            pl.BlockSpec((1, SEQ, D_KV), lambda b, i: (b, 0, C_V // D_KV)),
            pl.BlockSpec((1, SEQ, LANE), lambda b, i: (b, 0, C_KI // LANE)),
            pl.BlockSpec((N_HEADS, CHUNK, T), lambda b, i: (0, 0, 0)),
        ],
        out_specs=pl.BlockSpec((1, T, D_ATTN), lambda b, i: (b, i, 0)),
        out_shape=jax.ShapeDtypeStruct((BATCH, SEQ, D_ATTN), F32),
        scratch_shapes=[
            pltpu.VMEM((PAD_KEYS + SEQ, D_KV), BF16),
            pltpu.VMEM((D_KV, PAD_KEYS + SEQ), BF16),
            pltpu.VMEM((PAD_KEYS + SEQ, IDX_DIM), BF16),
            pltpu.VMEM((N_HEADS, T, HEAD_DIM), BF16),
            pltpu.VMEM((IDX_HEADS, T, IDX_DIM), BF16),
            pltpu.VMEM((PAD_KEYS + SEQ, T), F32),
            pltpu.VMEM((PAD_KEYS + SEQ, T), F32),
            pltpu.VMEM((PAD_KEYS + SEQ, N_HEADS * T), F32),
            pltpu.VMEM((N_KV_HEADS, 1, GROUP * T), F32),
            pltpu.VMEM((N_KV_HEADS, 1, GROUP * T), F32),
            pltpu.VMEM((N_KV_HEADS, HEAD_DIM, GROUP * T), F32),
        ],
        compiler_params=_params(("parallel", "arbitrary")),
        name="attn_prompt",
    )(h3, h3, h3, h3, h3, h3, bias_p)


def _score_s_kernel(pt_ref, qi_ref, w_ref, kin_ref, *rest):
    npg = SCORE_ROWS * N_PAGES
    kip = rest[0:npg]
    o_ref = rest[npg]
    kinp_ref = rest[npg + 1]
    del pt_ref
    R = DEC_SEQ
    qrow = lax.broadcasted_iota(jnp.int32, (R, LANE), 0)
    lane = lax.broadcasted_iota(jnp.int32, (R, LANE), 1)
    kinp_ref[...] = jnp.zeros(kinp_ref.shape, BF16)
    for r in range(SCORE_ROWS):
        kinp_ref[r, 0:2 * R] = jnp.concatenate(
            [kin_ref[r][:, :IDX_DIM], jnp.zeros((R, IDX_DIM), F32)], 0).astype(BF16)
    for r in range(SCORE_ROWS):
        qi = qi_ref[r].astype(BF16)
        wb = jnp.broadcast_to(w_ref[r] * SCORE_SCALE, (IDX_HEADS * R, LANE))
        for t in range(N_PAGES + 1):
            if t < N_PAGES:
                d = jnp.dot(qi, kip[r * N_PAGES + t][0].astype(BF16), preferred_element_type=F32)
            else:
                d = lax.dot_general(qi, kinp_ref[r], (((1,), (1,)), ((), ())),
                                    preferred_element_type=F32)
            e = (jnp.maximum(d, 0.0) * wb).reshape(IDX_HEADS, R, LANE)
            s = e[0]
            for h in range(1, IDX_HEADS):
                s = s + e[h]
            if t == N_PAGES:
                s = jnp.where(lane <= qrow, s, NEG_INF)
            o_ref[r, :, t * LANE:(t + 1) * LANE] = s


def _score_sample(h_s, page_table, cache_kidx_t):
    R = DEC_SEQ
    G = SCORE_ROWS
    h3 = h_s.reshape(DEC_BATCH, R, D_H)
    qi_hq = h3[:, :, C_QI:C_QI + IDX_HEADS * IDX_DIM].reshape(DEC_BATCH, R, IDX_HEADS, IDX_DIM)
    qi_hq = qi_hq.transpose(0, 2, 1, 3).reshape(DEC_BATCH, IDX_HEADS * R, IDX_DIM)
    w_hq = h3[:, :, C_KI + W_OFF:C_KI + W_OFF + IDX_HEADS].transpose(0, 2, 1)
    w_hq = w_hq.reshape(DEC_BATCH, IDX_HEADS * R, 1)
    in_specs = [
        pl.BlockSpec((G, IDX_HEADS * R, IDX_DIM), lambda b, pt: (b, 0, 0)),
        pl.BlockSpec((G, IDX_HEADS * R, 1), lambda b, pt: (b, 0, 0)),
        pl.BlockSpec((G, R, LANE), lambda b, pt: (b, 0, C_KI // LANE)),
    ]
    in_specs += [pl.BlockSpec((1, IDX_DIM, PAGE_SIZE),
                              lambda b, pt, r=r, p=p: (pt[b * G + r, p], 0, 0))
                 for r in range(G) for p in range(N_PAGES)]
    grid_spec = pltpu.PrefetchScalarGridSpec(
        num_scalar_prefetch=1,
        grid=(DEC_BATCH // G,),
        in_specs=in_specs,
        out_specs=pl.BlockSpec((G, R, L_SAMPLE), lambda b, pt: (b, 0, 0)),
        scratch_shapes=[pltpu.VMEM((G, PAGE_SIZE, IDX_DIM), BF16)],
    )
    return pl.pallas_call(
        _score_s_kernel,
        grid_spec=grid_spec,
        out_shape=jax.ShapeDtypeStruct((DEC_BATCH, R, L_SAMPLE), F32),
        compiler_params=_params(("arbitrary",)),
        name="score_sample",
    )(page_table, qi_hq, w_hq, h3, *([cache_kidx_t] * (G * N_PAGES)))


def _select_s_kernel(s_ref, o_ref):
    nt = N_PAGES + 1

    def tile_fn(j):
        return s_ref[j * LANE:(j + 1) * LANE, :].reshape(SUBTILES, SUBLANE, LANE)

    def key_fn(j):
        return (j * LANE + _key_iota(SUBTILES)).astype(F32)

    def write_mask(mask_fn):
        for j in range(nt):
            o_ref[j * LANE:(j + 1) * LANE, :] = mask_fn(tile_fn(j), j).reshape(LANE, LANE)

    thr, need, use_ties, count = _select_threshold(tile_fn, nt, static=True)

    @pl.when(jnp.logical_not(use_ties))
    def _():
        write_mask(_mask_plain(thr))

    @pl.when(use_ties)
    def _():
        write_mask(_mask_ties(key_fn, thr, _tie_cut(count, key_fn, thr, need, L_SAMPLE)))


def _select_sample(scores_t):
    n = scores_t.shape[1]
    return pl.pallas_call(
        _select_s_kernel,
        grid=(n // LANE,),
        in_specs=[pl.BlockSpec((L_SAMPLE, LANE), lambda c: (0, c))],
        out_specs=pl.BlockSpec((L_SAMPLE, LANE), lambda c: (0, c)),
        out_shape=jax.ShapeDtypeStruct((L_SAMPLE, n), F32),
        compiler_params=_params(("parallel",)),
        name="select_sample",
    )(scores_t)


def _attn_s_kernel(pt_ref, q_ref, kn_ref, vn_ref, mask_ref, bias_ref, *rest):
    kp = rest[0:N_PAGES]
    vp = rest[N_PAGES:2 * N_PAGES]
    o_ref = rest[2 * N_PAGES]
    knp_ref, vnp_ref, logit_ref = rest[2 * N_PAGES + 1:]
    del pt_ref
    R = DEC_SEQ
    NT = N_PAGES + 1
    GR = GROUP * R

    knp_ref[...] = jnp.zeros(knp_ref.shape, BF16)
    vnp_ref[...] = jnp.zeros(vnp_ref.shape, BF16)
    knp_ref[0:2 * R] = jnp.concatenate([kn_ref[0], jnp.zeros((R, D_KV), F32)], 0).astype(BF16)
    vnp_ref[0:2 * R] = jnp.concatenate([vn_ref[0], jnp.zeros((R, D_KV), F32)], 0).astype(BF16)

    def page_head(refs, t, g):
        if t < N_PAGES:
            return refs[t][pl.ds(g, PAGE_SIZE, stride=N_KV_HEADS), :].astype(BF16)
        pad = knp_ref if refs is kp else vnp_ref
        return pad[:, g * HEAD_DIM:(g + 1) * HEAD_DIM]

    q = (q_ref[0] * QK_SCALE).astype(BF16)
    for t in range(NT):
        mb = mask_ref[0, :, t * LANE:(t + 1) * LANE]
        for g in range(N_KV_HEADS):
            lg = lax.dot_general(q[g * GR:(g + 1) * GR], page_head(kp, t, g),
                                 (((1,), (1,)), ((), ())), preferred_element_type=F32)
            lg = lg.reshape(GROUP, R, LANE) + mb[None]
            if t >= N_PAGES - 1:
                off = (t - (N_PAGES - 1)) * LANE
                lg = lg + bias_ref[g * GROUP:(g + 1) * GROUP, :, off:off + LANE]
            logit_ref[g * GR:(g + 1) * GR, t * LANE:(t + 1) * LANE] = lg.reshape(GR, LANE)

    logits = logit_ref[...]
    m = jnp.max(logits, axis=1, keepdims=True)
    p = jnp.exp(logits - m)
    inv = 1.0 / jnp.sum(p, axis=1, keepdims=True)
    pb = p.astype(BF16)
    outs = [jnp.zeros((GR, HEAD_DIM), F32) for _ in range(N_KV_HEADS)]
    for t in range(NT):
        for g in range(N_KV_HEADS):
            outs[g] = outs[g] + jnp.dot(pb[g * GR:(g + 1) * GR, t * LANE:(t + 1) * LANE],
                                        page_head(vp, t, g), preferred_element_type=F32)
    for g in range(N_KV_HEADS):
        o = outs[g] * inv[g * GR:(g + 1) * GR]
        for hq in range(GROUP):
            h = g * GROUP + hq
            o_ref[0, :, h * HEAD_DIM:(h + 1) * HEAD_DIM] = o[hq * R:(hq + 1) * R]


def _attn_sample(h_s, page_table, cache_k, cache_v, mask, bias_s):
    R = DEC_SEQ
    h3 = h_s.reshape(DEC_BATCH, R, D_H)
    q_hq = h3[:, :, C_Q:C_Q + D_ATTN].reshape(DEC_BATCH, R, N_HEADS, HEAD_DIM)
    q_hq = q_hq.transpose(0, 2, 1, 3).reshape(DEC_BATCH, N_HEADS * R, HEAD_DIM)
    rows_per_page = PAGE_SIZE * N_KV_HEADS
    ck = cache_k.reshape(-1, HEAD_DIM)
    cv = cache_v.reshape(-1, HEAD_DIM)

    in_specs = [
        pl.BlockSpec((1, N_HEADS * R, HEAD_DIM), lambda b, pt: (b, 0, 0)),
        pl.BlockSpec((1, R, D_KV), lambda b, pt: (b, 0, C_K // D_KV)),
        pl.BlockSpec((1, R, D_KV), lambda b, pt: (b, 0, C_V // D_KV)),
        pl.BlockSpec((1, R, L_SAMPLE), lambda b, pt: (b, 0, 0)),
        pl.BlockSpec((N_HEADS, R, 2 * LANE), lambda b, pt: (0, 0, 0)),
    ]
    page = [pl.BlockSpec((rows_per_page, HEAD_DIM), lambda b, pt, p=p: (pt[b, p], 0))
            for p in range(N_PAGES)]
    in_specs += page + page
    grid_spec = pltpu.PrefetchScalarGridSpec(
        num_scalar_prefetch=1,
        grid=(DEC_BATCH,),
        in_specs=in_specs,
        out_specs=pl.BlockSpec((1, R, D_ATTN), lambda b, pt: (b, 0, 0)),
        scratch_shapes=[
            pltpu.VMEM((PAGE_SIZE, D_KV), BF16),
            pltpu.VMEM((PAGE_SIZE, D_KV), BF16),
            pltpu.VMEM((N_HEADS * R, L_SAMPLE), F32),
        ],
    )
    return pl.pallas_call(
        _attn_s_kernel,
        grid_spec=grid_spec,
        out_shape=jax.ShapeDtypeStruct((DEC_BATCH, R, D_ATTN), F32),
        compiler_params=_params(("arbitrary",)),
        name="attn_sample",
    )(page_table, q_hq, h3, h3, mask, bias_s, *([ck] * N_PAGES), *([cv] * N_PAGES))


def _conv_p_kernel(val_ref, gate_ref, cw_ref, cb_ref, dw_ref, ut_ref, pad_ref):
    pad_ref[0:PAD_ROWS] = jnp.zeros((PAD_ROWS, LANE), F32)
    pad_ref[PAD_ROWS:] = val_ref[0] * jax.nn.sigmoid(gate_ref[0])
    ut_ref[0] = pad_ref[SEQ:SEQ + PAD_ROWS]
    cw = cw_ref[...]
    cb = cb_ref[...]
    first = PAD_ROWS - (CONV_WIDTH - 1)
    for c in range(SEQ // CONV_CHUNK):
        base = c * CONV_CHUNK
        acc = jnp.broadcast_to(cb, (CONV_CHUNK, LANE))
        for r in range(SUBLANE):
            taps = [w for w in range(CONV_WIDTH) if (first + w) % SUBLANE == r]
            span = max(first + w - r for w in taps) + CONV_CHUNK
            win = pad_ref[base + r:base + r + span]
            for w in taps:
                a = first + w - r
                acc = acc + win[a:a + CONV_CHUNK] * cw[w:w + 1]
        dw_ref[0, base:base + CONV_CHUNK] = acc


def _conv_prompt(h_p, cw_pad, cb):
    h3 = h_p.reshape(BATCH, SEQ, D_H)
    nc = D_CONV // LANE
    return pl.pallas_call(
        _conv_p_kernel,
        grid=(BATCH, nc),
        in_specs=[pl.BlockSpec((1, SEQ, LANE), lambda b, c: (b, 0, C_GV // LANE + c)),
                  pl.BlockSpec((1, SEQ, LANE), lambda b, c: (b, 0, C_GG // LANE + c)),
                  pl.BlockSpec((PAD_ROWS, LANE), lambda b, c: (0, c)),
                  pl.BlockSpec((1, LANE), lambda b, c: (0, c))],
        out_specs=[pl.BlockSpec((1, SEQ, LANE), lambda b, c: (b, 0, c)),
                   pl.BlockSpec((1, PAD_ROWS, LANE), lambda b, c: (b, 0, c))],
        out_shape=[jax.ShapeDtypeStruct((BATCH, SEQ, D_CONV), F32),
                   jax.ShapeDtypeStruct((BATCH, PAD_ROWS, D_CONV), F32)],
        scratch_shapes=[pltpu.VMEM((PAD_ROWS + SEQ, LANE), F32)],
        compiler_params=_params(("parallel", "parallel")),
        name="conv_prompt",
    )(h3, h3, cw_pad, cb)


def _conv_s_kernel(val_ref, gate_ref, st_ref, cw_ref, cb_ref, dw_ref, u_ref, pad_ref):
    R = DEC_SEQ
    u = val_ref[...] * jax.nn.sigmoid(gate_ref[...])
    u_ref[...] = u
    first = PAD_ROWS - (CONV_WIDTH - 1)
    pad_ref[:, 0:SUBLANE] = jnp.zeros((DEC_BATCH, SUBLANE, LANE), F32)
    pad_ref[:, first:PAD_ROWS] = st_ref[...]
    pad_ref[:, PAD_ROWS:PAD_ROWS + R] = u
    cw = cw_ref[...]
    acc = jnp.broadcast_to(cb_ref[...][None], (DEC_BATCH, R, LANE))
    for w in range(CONV_WIDTH):
        acc = acc + pad_ref[:, first + w:first + w + R] * cw[w:w + 1][None]
    dw_ref[...] = acc


def _conv_sample(h_s, state, cw_pad, cb):
    R = DEC_SEQ
    h3 = h_s.reshape(DEC_BATCH, R, D_H)
    nc = D_CONV // LANE
    blk = pl.BlockSpec((DEC_BATCH, R, LANE), lambda c: (0, 0, c))
    return pl.pallas_call(
        _conv_s_kernel,
        grid=(nc,),
        in_specs=[pl.BlockSpec((DEC_BATCH, R, LANE), lambda c: (0, 0, C_GV // LANE + c)),
                  pl.BlockSpec((DEC_BATCH, R, LANE), lambda c: (0, 0, C_GG // LANE + c)),
                  pl.BlockSpec((DEC_BATCH, CONV_WIDTH - 1, LANE), lambda c: (0, 0, c)),
                  pl.BlockSpec((PAD_ROWS, LANE), lambda c: (0, c)),
                  pl.BlockSpec((1, LANE), lambda c: (0, c))],
        out_specs=[blk, blk],
        out_shape=[jax.ShapeDtypeStruct((DEC_BATCH, R, D_CONV), F32),
                   jax.ShapeDtypeStruct((DEC_BATCH, R, D_CONV), F32)],
        scratch_shapes=[pltpu.VMEM((DEC_BATCH, PAD_ROWS + R, LANE), F32)],
        compiler_params=_params(("parallel",)),
        name="conv_sample",
    )(h3, h3, state, cw_pad, cb)


def _tail_kernel(attn_ref, za_ref, dw_ref, zc_ref, ga_ref, gc_ref, x_ref,
                 wua_ref, wpw_ref, wuc_ref, wo_ref, ng_ref, nb_ref, bpw_ref, fg_ref, y_ref):
    a = attn_ref[...] * jax.nn.silu(za_ref[...])
    branch_attn = jnp.dot(a.astype(BF16), wua_ref[...], preferred_element_type=F32)

    dw = dw_ref[...]
    mu = jnp.mean(dw, axis=-1, keepdims=True)
    var = jnp.mean(jnp.square(dw - mu), axis=-1, keepdims=True)
    ln = (dw - mu) * lax.rsqrt(var + EPS) * ng_ref[...] + nb_ref[...]
    conv_out = jnp.dot(jax.nn.silu(ln).astype(BF16), wpw_ref[...],
                       preferred_element_type=F32) + bpw_ref[...]
    c = conv_out * jax.nn.silu(zc_ref[...])
    branch_conv = jnp.dot(c.astype(BF16), wuc_ref[...], preferred_element_type=F32)

    merged = jax.nn.sigmoid(ga_ref[...]) * branch_attn + jax.nn.sigmoid(gc_ref[...]) * branch_conv
    y = x_ref[...] + jnp.dot(merged.astype(BF16), wo_ref[...], preferred_element_type=F32)
    ms = jnp.mean(y * y, axis=-1, keepdims=True)
    y_ref[...] = y * lax.rsqrt(ms + EPS) * fg_ref[...]


def _tail(h, attn, dw, x2d, wua, wpw, wuc, wo, ng, nb, bpw, fg, tm):
    n = x2d.shape[0]

    def const(shape):
        return pl.BlockSpec(shape, lambda i: (0, 0), pipeline_mode=pl.Buffered(1))

    return pl.pallas_call(
        _tail_kernel,
        grid=(n // tm,),
        in_specs=[pl.BlockSpec((tm, D_ATTN), lambda i: (i, 0)),
                  pl.BlockSpec((tm, D_ATTN), lambda i: (i, C_ZA // D_ATTN)),
                  pl.BlockSpec((tm, D_CONV), lambda i: (i, 0)),
                  pl.BlockSpec((tm, D_CONV), lambda i: (i, C_ZC // D_CONV)),
                  pl.BlockSpec((tm, D_MODEL), lambda i: (i, C_GA // D_MODEL)),
                  pl.BlockSpec((tm, D_MODEL), lambda i: (i, C_GC // D_MODEL)),
                  pl.BlockSpec((tm, D_MODEL), lambda i: (i, 0)),
                  const((D_ATTN, D_MODEL)), const((D_CONV, D_CONV)),
                  const((D_CONV, D_MODEL)), const((D_MODEL, D_MODEL)),
                  const((1, D_CONV)), const((1, D_CONV)), const((1, D_CONV)),
                  const((1, D_MODEL))],
        out_specs=pl.BlockSpec((tm, D_MODEL), lambda i: (i, 0)),
        out_shape=jax.ShapeDtypeStruct((n, D_MODEL), F32),
        compiler_params=_params(("parallel",)),
        name="tail",
    )(attn, h, dw, h, h, h, x2d, wua, wpw, wuc, wo, ng, nb, bpw, fg)


def _reorder_w_in(w_in):
    w_t = w_in.T

    def rows(name):
        o, w = _SRC[name]
        return w_t[o:o + w]

    pad = jnp.zeros((LANE - IDX_DIM - IDX_HEADS, D_MODEL), w_in.dtype)
    w = jnp.concatenate([rows('gate_attn'), rows('gate_conv'), rows('q'), rows('z_attn'),
                         rows('glu_val'), rows('glu_gate'), rows('z_conv'), rows('k'), rows('v'),
                         rows('q_idx'), rows('k_idx'), rows('w_idx'), pad], axis=0)
    return w.astype(BF16)


def kernel(x_prompt, x_sample, cache_k, cache_v, cache_kidx, state_conv, page_table,
           ln_g, w_in, conv_w, conv_b, conv_norm_g, conv_norm_b, w_pw, b_pw,
           w_up_attn, w_up_conv, w_out, rel_bias, final_g):
    w_all = _reorder_w_in(w_in[0])
    g_in = ln_g[0].reshape(1, D_MODEL)
    xp = x_prompt.reshape(BATCH * SEQ, D_MODEL)
    xs = x_sample.reshape(DEC_BATCH * DEC_SEQ, D_MODEL)
    h_p = _proj(xp, g_in, w_all, PROJ_TM, PROJ_TN)
    h_s = _proj(xs, g_in, w_all, PROJ_TM, PROJ_TN)

    bias_p, bias_s = _bias_tables(rel_bias)

    attn_p = _attn_prompt(h_p, bias_p).reshape(BATCH * SEQ, D_ATTN)

    n_s = DEC_BATCH * DEC_SEQ
    scores = _score_sample(h_s, page_table, cache_kidx[0].transpose(0, 2, 1))
    mask_t = _select_sample(scores.reshape(n_s, L_SAMPLE).T)
    mask = mask_t.T.reshape(DEC_BATCH, DEC_SEQ, L_SAMPLE)
    attn_s = _attn_sample(h_s, page_table, cache_k[0], cache_v[0], mask, bias_s)
    attn_s = attn_s.reshape(n_s, D_ATTN)

    cw_pad = jnp.concatenate([conv_w[0], jnp.zeros((PAD_ROWS - CONV_WIDTH, D_CONV), F32)], 0)
    cb = conv_b[0].reshape(1, D_CONV)
    dw_p, u_tail = _conv_prompt(h_p, cw_pad, cb)
    dw_s, u_s = _conv_sample(h_s, state_conv[0], cw_pad, cb)

    wua = w_up_attn[0].astype(BF16)
    wpw = w_pw[0].astype(BF16)
    wuc = w_up_conv[0].astype(BF16)
    wo = w_out[0].astype(BF16)
    ng = conv_norm_g[0].reshape(1, D_CONV)
    nb = conv_norm_b[0].reshape(1, D_CONV)
    bpw = b_pw[0].reshape(1, D_CONV)
    fg = final_g.reshape(1, D_MODEL)
    y_p = _tail(h_p, attn_p, dw_p.reshape(BATCH * SEQ, D_CONV), xp,
                wua, wpw, wuc, wo, ng, nb, bpw, fg, TAIL_TM)
    y_s = _tail(h_s, attn_s, dw_s.reshape(n_s, D_CONV), xs,
                wua, wpw, wuc, wo, ng, nb, bpw, fg, TAIL_TM)

    hp3 = h_p.reshape(BATCH, SEQ, D_H)
    hs3 = h_s.reshape(DEC_BATCH, DEC_SEQ, D_H)
    tail_rows = CONV_WIDTH - 1
    return (
        y_p.reshape(BATCH, SEQ, D_MODEL),
        y_s.reshape(DEC_BATCH, DEC_SEQ, D_MODEL),
        hp3[:, :, C_K:C_K + D_KV].reshape(1, BATCH, SEQ, N_KV_HEADS, HEAD_DIM),
        hp3[:, :, C_V:C_V + D_KV].reshape(1, BATCH, SEQ, N_KV_HEADS, HEAD_DIM),
        hp3[:, :, C_KI:C_KI + IDX_DIM].reshape(1, BATCH, SEQ, IDX_DIM),
        u_tail[:, PAD_ROWS - tail_rows:].reshape(1, BATCH, tail_rows, D_CONV),
        hs3[:, :, C_K:C_K + D_KV].reshape(1, DEC_BATCH, DEC_SEQ, N_KV_HEADS, HEAD_DIM),
        hs3[:, :, C_V:C_V + D_KV].reshape(1, DEC_BATCH, DEC_SEQ, N_KV_HEADS, HEAD_DIM),
        hs3[:, :, C_KI:C_KI + IDX_DIM].reshape(1, DEC_BATCH, DEC_SEQ, IDX_DIM),
        jnp.concatenate([state_conv[0][:, DEC_SEQ:], u_s], axis=1).reshape(
            1, DEC_BATCH, tail_rows, D_CONV),
    )
```

```python
import math

import numpy as np
import jax
import jax.numpy as jnp
from jax import lax
from jax.experimental import pallas as pl
from jax.experimental.pallas import tpu as pltpu

F32 = jnp.float32
BF16 = jnp.bfloat16

D_MODEL = 2048
BATCH = 8
SEQ = 2048
DEC_BATCH = 128
DEC_SEQ = 8
PAST_LEN = 2048
PAGE_SIZE = 128
N_PAGES = PAST_LEN // PAGE_SIZE
N_HEADS = 8
N_KV_HEADS = 2
HEAD_DIM = 128
GROUP = N_HEADS // N_KV_HEADS
D_ATTN = N_HEADS * HEAD_DIM
D_KV = N_KV_HEADS * HEAD_DIM
IDX_HEADS = 8
IDX_DIM = 64
TOPK = 256
Q_BLOCK = 128
N_BUCKETS = 32
MAX_DISTANCE = 128
D_CONV = D_MODEL // 2
CONV_WIDTH = 31
EPS = 1e-6
NEG = -1e30
NEG_INF = float("-inf")

LANE = 128
SUBLANE = 8
VMEM_LIMIT = 56 * 1024 * 1024

C_GA = 0
C_GC = C_GA + D_MODEL
C_Q = C_GC + D_MODEL
C_ZA = C_Q + D_ATTN
C_GV = C_ZA + D_ATTN
C_GG = C_GV + D_CONV
C_ZC = C_GG + D_CONV
C_K = C_ZC + D_CONV
C_V = C_K + D_KV
C_QI = C_V + D_KV
C_KI = C_QI + IDX_HEADS * IDX_DIM
W_OFF = IDX_DIM
MXU_WIDTH = 256
PROJ_TM = 1024
PROJ_TN = 6 * MXU_WIDTH
D_H = -(-(C_KI + LANE) // PROJ_TN) * PROJ_TN
KV_TILE = C_K // PROJ_TN
assert C_K % PROJ_TN == 0 and C_V == C_K + D_KV

_SRC = {}
_off = 0
for _name, _w in (('q', D_ATTN), ('k', D_KV), ('v', D_KV), ('z_attn', D_ATTN),
                  ('q_idx', IDX_HEADS * IDX_DIM), ('k_idx', IDX_DIM), ('w_idx', IDX_HEADS),
                  ('glu_val', D_CONV), ('glu_gate', D_CONV), ('z_conv', D_CONV),
                  ('gate_attn', D_MODEL), ('gate_conv', D_MODEL)):
    _SRC[_name] = (_off, _w)
    _off += _w

N_BISECT = 20
PAD_ROWS = 32
CONV_CHUNK = 64
SCORE_SCALE = (IDX_DIM ** -0.5) * (IDX_HEADS ** -0.5)
QK_SCALE = HEAD_DIM ** -0.5
LOG2E = math.log2(math.e)
L_SAMPLE = (N_PAGES + 1) * LANE
SUBTILES = LANE // SUBLANE
CHUNK_BLOCKS = 4
CHUNK = CHUNK_BLOCKS * Q_BLOCK
PAD_KEYS = CHUNK - Q_BLOCK
SCORE_ROWS = 4
TAIL_TM = 256


def _t5_bucket_static(dist):
    n = np.maximum(dist, 0)
    max_exact = N_BUCKETS // 2
    ratio = (np.log(np.maximum(n, 1).astype(np.float32) / np.float32(max_exact))
             / np.float32(math.log(MAX_DISTANCE / max_exact)))
    large = np.minimum(max_exact + (ratio * np.float32(N_BUCKETS - max_exact)).astype(np.int32),
                       N_BUCKETS - 1)
    return np.where(n < max_exact, n, large).astype(np.int32)


FAR_BUCKET = int(_t5_bucket_static(np.array([2 * MAX_DISTANCE]))[0])


def _params(sem):
    return pltpu.CompilerParams(dimension_semantics=sem, vmem_limit_bytes=VMEM_LIMIT)


def _proj_kernel(x_ref, g_ref, w_ref, o_ref, k_ref, v_ref, xn_ref):
    tm = x_ref.shape[0]

    @pl.when(pl.program_id(1) == 0)
    def _():
        x = x_ref[...]
        ms = jnp.mean(x * x, axis=-1, keepdims=True)
        xn_ref[...] = (x * lax.rsqrt(ms + EPS) * g_ref[...]).astype(BF16)

    o_ref[...] = lax.dot_general(xn_ref[...], w_ref[...], (((1,), (1,)), ((), ())),
                                 preferred_element_type=F32)

    @pl.when(pl.program_id(1) == KV_TILE)
    def _():
        for g in range(N_KV_HEADS):
            rows = pl.ds(g, tm, stride=N_KV_HEADS)
            k_ref[rows, :] = o_ref[:, g * HEAD_DIM:(g + 1) * HEAD_DIM]
            v_ref[rows, :] = o_ref[:, D_KV + g * HEAD_DIM:D_KV + (g + 1) * HEAD_DIM]


def _proj(x2d, g, w):
    n = x2d.shape[0]
    tm, tn = PROJ_TM, PROJ_TN
    kv_spec = pl.BlockSpec((N_KV_HEADS * tm, HEAD_DIM), lambda i, j: (i, 0))
    kv_shape = jax.ShapeDtypeStruct((N_KV_HEADS * n, HEAD_DIM), F32)
    return pl.pallas_call(
        _proj_kernel,
        grid=(n // tm, D_H // tn),
        in_specs=[pl.BlockSpec((tm, D_MODEL), lambda i, j: (i, 0)),
                  pl.BlockSpec((1, D_MODEL), lambda i, j: (0, 0)),
                  pl.BlockSpec((tn, D_MODEL), lambda i, j: (j, 0))],
        out_specs=[pl.BlockSpec((tm, tn), lambda i, j: (i, j)), kv_spec, kv_spec],
        out_shape=[jax.ShapeDtypeStruct((n, D_H), F32), kv_shape, kv_shape],
        scratch_shapes=[pltpu.VMEM((tm, D_MODEL), BF16)],
        compiler_params=_params(("parallel", "arbitrary")),
        name="proj",
    )(x2d, g, w)


def _bias_kernel(rb_ref, bp_ref, bs_ref, op_ref, os_ref):
    bp = bp_ref[...]
    bs = bs_ref[...]
    for h in range(N_HEADS):
        far = rb_ref[FAR_BUCKET, h]
        tp = jnp.zeros(bp.shape, F32)
        ts = jnp.zeros(bs.shape, F32)
        for b in range(N_BUCKETS):
            val = rb_ref[b, h] - far
            tp = jnp.where(bp == b, val * LOG2E, tp)
            ts = jnp.where(bs == b, val, ts)
        op_ref[h] = tp
        os_ref[h] = ts


def _bias_tables(rel_bias):
    key = np.arange(CHUNK)[:, None]
    qry = np.arange(Q_BLOCK)[None, :]
    bucket_p = _t5_bucket_static(CHUNK - Q_BLOCK + qry - key)
    qi = np.arange(DEC_SEQ)[:, None]
    col = np.arange(2 * LANE)[None, :]
    dist_s = np.where(col < LANE, LANE + qi - col, qi - (col - LANE))
    bucket_s = _t5_bucket_static(dist_s)
    return pl.pallas_call(
        _bias_kernel,
        in_specs=[pl.BlockSpec(memory_space=pltpu.SMEM),
                  pl.BlockSpec(memory_space=pltpu.VMEM),
                  pl.BlockSpec(memory_space=pltpu.VMEM)],
        out_specs=[pl.BlockSpec(memory_space=pltpu.VMEM),
                   pl.BlockSpec(memory_space=pltpu.VMEM)],
        out_shape=[jax.ShapeDtypeStruct((N_HEADS, CHUNK, Q_BLOCK), F32),
                   jax.ShapeDtypeStruct((N_HEADS, DEC_SEQ, 2 * LANE), F32)],
        name="bias_tables",
    )(rel_bias, jnp.asarray(bucket_p), jnp.asarray(bucket_s))


def _any(x):
    return jnp.max(jnp.where(x, 1.0, 0.0)) > 0.5


def _rep(x):
    return jnp.broadcast_to(x, (SUBLANE, LANE))


def _fold_rows(x, comb):
    parts = [x[k:k + SUBLANE] for k in range(0, x.shape[0], SUBLANE)]
    while len(parts) > 1:
        parts = [comb(parts[k], parts[k + 1]) for k in range(0, len(parts), 2)]
    return parts[0]


def _key_iota(nsub):
    sub = lax.broadcasted_iota(jnp.int32, (nsub, SUBLANE, LANE), 0) * SUBLANE
    return sub + lax.broadcasted_iota(jnp.int32, (nsub, SUBLANE, LANE), 1)


def _select_threshold(tile_fn, ntiles, static):
    def reduce_tiles(fn, init, comb):
        def step(j, acc):
            x = fn(tile_fn(j), j)
            parts = [x[k] for k in range(x.shape[0])]
            while len(parts) > 1:
                parts = [comb(parts[k], parts[k + 1]) for k in range(0, len(parts), 2)]
            return comb(acc, parts[0])
        if static:
            acc = init
            for j in range(ntiles):
                acc = step(j, acc)
            return acc
        return lax.fori_loop(0, ntiles, step, init)

    zeros = jnp.zeros((SUBLANE, LANE), F32)

    def count(pred_fn):
        acc = reduce_tiles(lambda s, j: jnp.where(pred_fn(s, j), 1.0, 0.0), zeros,
                           lambda a, b: a + b)
        return _rep(jnp.sum(acc, axis=0, keepdims=True))

    def masked_max(pred_fn):
        acc = reduce_tiles(lambda s, j: jnp.where(pred_fn(s, j), s, NEG_INF),
                           jnp.full((SUBLANE, LANE), NEG_INF, F32), jnp.maximum)
        return _rep(jnp.max(acc, axis=0, keepdims=True))

    bound = reduce_tiles(lambda s, j: jnp.where(s > NEG_INF, jnp.abs(s), 0.0), zeros,
                         jnp.maximum)
    bound = _rep(jnp.max(bound, axis=0, keepdims=True))

    def bisect(_, carry):
        lo, hi = carry
        mid = 0.5 * lo + 0.5 * hi
        few = count(lambda s, j: s > mid[None]) < TOPK
        return jnp.where(few, lo, mid), jnp.where(few, mid, hi)

    _, hi = lax.fori_loop(0, N_BISECT, bisect, (-bound, bound))

    thr = masked_max(lambda s, j: s <= hi[None])
    n_ge = count(lambda s, j: s >= thr[None])

    def fix_body(carry):
        thr, n_ge, _ = carry
        lower = masked_max(lambda s, j: s < thr[None])
        thr = jnp.where(n_ge < TOPK, lower, thr)
        n_ge = count(lambda s, j: s >= thr[None])
        return thr, n_ge, _any(n_ge < TOPK)

    thr, n_ge, _ = lax.while_loop(lambda c: c[2], fix_body, (thr, n_ge, _any(n_ge < TOPK)))
    n_gt = count(lambda s, j: s > thr[None])
    need = TOPK - n_gt
    use_ties = _any(n_ge - n_gt > need)
    return thr, need, use_ties, count


def _tie_cut(count, key_fn, thr, need, nkeys):
    steps = int(math.ceil(math.log2(nkeys))) + 1

    def body(_, carry):
        lo, hi = carry
        mid = jnp.floor((lo + hi) * 0.5)
        enough = count(lambda s, j: jnp.where(s == thr[None], key_fn(j), nkeys + 1.0)
                       <= mid[None]) >= need
        return jnp.where(enough, lo, mid), jnp.where(enough, mid, hi)

    lo0 = jnp.full((SUBLANE, LANE), -1.0, F32)
    hi0 = jnp.full((SUBLANE, LANE), nkeys - 1.0, F32)
    _, hi = lax.fori_loop(0, steps, body, (lo0, hi0))
    return hi


def _keep(pred):
    return jnp.where(pred, 0.0, NEG)


def _mask_plain(thr):
    return lambda s, j: _keep(s >= thr[None])


def _mask_ties(key_fn, thr, cut):
    return lambda s, j: jnp.where(s == thr[None], _keep(key_fn(j) <= cut[None]),
                                  _keep(s > thr[None]))


def _attn_p_kernel(q_ref, qi_ref, wi_ref, k_ref, v_ref, ki_ref, bias_ref, o_ref,
                   kb_ref, vt_ref, kib_ref, qh_ref, qih_ref,
                   score_ref, mask_ref, s_ref, m_ref, l_ref, acc_ref):
    i = pl.program_id(1)
    T = Q_BLOCK
    W = GROUP * T
    nch = i // CHUNK_BLOCKS + 1

    @pl.when(i == 0)
    def _():
        kb_ref[0:PAD_KEYS] = jnp.zeros((PAD_KEYS, D_KV), BF16)
        kb_ref[PAD_KEYS:] = k_ref[0].astype(BF16)
        kib_ref[0:PAD_KEYS] = jnp.zeros((PAD_KEYS, IDX_DIM), BF16)
        kib_ref[PAD_KEYS:] = ki_ref[0][:, :IDX_DIM].astype(BF16)
        vt_ref[:, 0:PAD_KEYS] = jnp.zeros((D_KV, PAD_KEYS), BF16)
        for g in range(N_KV_HEADS):
            for c in range(SEQ // LANE):
                blk = v_ref[0, c * LANE:(c + 1) * LANE, g * HEAD_DIM:(g + 1) * HEAD_DIM]
                vt_ref[g * HEAD_DIM:(g + 1) * HEAD_DIM,
                       PAD_KEYS + c * LANE:PAD_KEYS + (c + 1) * LANE] = blk.T.astype(BF16)
        score_ref[0:PAD_KEYS] = jnp.full((PAD_KEYS, T), NEG_INF, F32)
        mask_ref[0:PAD_KEYS] = jnp.full((PAD_KEYS, T), NEG, F32)

    q = q_ref[0] * (QK_SCALE * LOG2E)
    for h in range(N_HEADS):
        qh_ref[h] = q[:, h * HEAD_DIM:(h + 1) * HEAD_DIM].astype(BF16)
    qi = qi_ref[0]
    for h in range(IDX_HEADS):
        qih_ref[h] = qi[:, h * IDX_DIM:(h + 1) * IDX_DIM].astype(BF16)
    w_rows = wi_ref[0].T[W_OFF:W_OFF + IDX_HEADS] * SCORE_SCALE

    def span(c):
        return pl.ds(pl.multiple_of((i - CHUNK_BLOCKS * c) * LANE, LANE), CHUNK)

    def first_key(c):
        return (i - CHUNK_BLOCKS * c) * LANE - PAD_KEYS

    key_l = lax.broadcasted_iota(jnp.int32, (CHUNK, T), 0)
    qry = i * T + lax.broadcasted_iota(jnp.int32, (CHUNK, T), 1)

    def score_body(c, _):
        kc = kib_ref[span(c), :]
        d = lax.dot_general(kc, qih_ref[...].reshape(IDX_HEADS * T, IDX_DIM),
                            (((1,), (1,)), ((), ())), preferred_element_type=F32)
        acc = jnp.zeros((CHUNK, T), F32)
        for h in range(IDX_HEADS):
            acc = acc + jnp.maximum(d[:, h * T:(h + 1) * T], 0.0) * w_rows[h:h + 1]
        key = first_key(c) + key_l
        acc = jnp.where(key <= qry, jnp.where(key >= 0, acc, NEG_INF), NEG_INF)
        score_ref[span(c), :] = acc
        return 0

    lax.fori_loop(0, nch, score_body, 0)

    nsub = CHUNK // SUBLANE

    def tile_fn(c):
        return score_ref[span(c), :].reshape(nsub, SUBLANE, T)

    def key_fn(c):
        return (first_key(c) + _key_iota(nsub)).astype(F32)

    def write_mask(mask_fn):
        def body(c, _):
            mask_ref[span(c), :] = mask_fn(tile_fn(c), c).reshape(CHUNK, T)
            return 0
        lax.fori_loop(0, nch, body, 0)

    @pl.when(i * T + T <= TOPK)
    def _():
        write_mask(lambda s, c: _keep(s > NEG_INF))

    @pl.when(i * T + T > TOPK)
    def _():
        thr, need, use_ties, count = _select_threshold(tile_fn, nch, static=False)

        @pl.when(jnp.logical_not(use_ties))
        def _():
            write_mask(_mask_plain(thr))

        @pl.when(use_ties)
        def _():
            write_mask(_mask_ties(key_fn, thr, _tie_cut(count, key_fn, thr, need, SEQ)))

    m_ref[...] = jnp.full(m_ref.shape, NEG, F32)
    l_ref[...] = jnp.zeros(l_ref.shape, F32)
    acc_ref[...] = jnp.zeros(acc_ref.shape, F32)

    def logits(c, with_bias):
        mb = mask_ref[span(c), :]
        for g in range(N_KV_HEADS):
            kc = kb_ref[span(c), g * HEAD_DIM:(g + 1) * HEAD_DIM]
            qg = qh_ref[g * GROUP:(g + 1) * GROUP].reshape(W, HEAD_DIM)
            s = lax.dot_general(kc, qg, (((1,), (1,)), ((), ())), preferred_element_type=F32)
            if with_bias:
                add = jnp.concatenate([mb + bias_ref[g * GROUP + hq] for hq in range(GROUP)],
                                      axis=1)
            else:
                add = jnp.concatenate([mb] * GROUP, axis=1)
            s = s + add
            s_ref[span(c), g * W:(g + 1) * W] = s
            top = jnp.max(_fold_rows(s, jnp.maximum), axis=0, keepdims=True)
            m_ref[g] = jnp.maximum(m_ref[g], top)

    logits(0, True)

    def logits_body(c, _):
        logits(c, False)
        return 0

    lax.fori_loop(1, nch, logits_body, 0)

    def weigh(c, _):
        for g in range(N_KV_HEADS):
            p = jnp.exp2(s_ref[span(c), g * W:(g + 1) * W] - m_ref[g])
            l_ref[g] += jnp.sum(_fold_rows(p, jnp.add), axis=0, keepdims=True)
            vtc = vt_ref[g * HEAD_DIM:(g + 1) * HEAD_DIM, span(c)]
            acc_ref[g] += jnp.dot(vtc, p.astype(BF16), preferred_element_type=F32)
        return 0

    lax.fori_loop(0, nch, weigh, 0)

    for g in range(N_KV_HEADS):
        o = acc_ref[g] / l_ref[g]
        for hq in range(GROUP):
            h = g * GROUP + hq
            o_ref[0, :, h * HEAD_DIM:(h + 1) * HEAD_DIM] = o[:, hq * T:(hq + 1) * T].T


def _attn_prompt(h_p, bias_p):
    h3 = h_p.reshape(BATCH, SEQ, D_H)
    nqb = SEQ // Q_BLOCK
    T = Q_BLOCK
    return pl.pallas_call(
        _attn_p_kernel,
        grid=(BATCH, nqb),
        in_specs=[
            pl.BlockSpec((1, T, D_ATTN), lambda b, i: (b, i, C_Q // D_ATTN)),
            pl.BlockSpec((1, T, IDX_HEADS * IDX_DIM), lambda b, i: (b, i, C_QI // (IDX_HEADS * IDX_DIM))),
            pl.BlockSpec((1, T, LANE), lambda b, i: (b, i, C_KI // LANE)),
            pl.BlockSpec((1, SEQ, D_KV), lambda b, i: (b, 0, C_K // D_KV)),
            pl.BlockSpec((1, SEQ, D_KV), lambda b, i: (b, 0, C_V // D_KV)),
            pl.BlockSpec((1, SEQ, LANE), lambda b, i: (b, 0, C_KI // LANE)),
            pl.BlockSpec((N_HEADS, CHUNK, T), lambda b, i: (0, 0, 0)),
        ],
        out_specs=pl.BlockSpec((1, T, D_ATTN), lambda b, i: (b, i, 0)),
        out_shape=jax.ShapeDtypeStruct((BATCH, SEQ, D_ATTN), F32),
        scratch_shapes=[
            pltpu.VMEM((PAD_KEYS + SEQ, D_KV), BF16),
            pltpu.VMEM((D_KV, PAD_KEYS + SEQ), BF16),
            pltpu.VMEM((PAD_KEYS + SEQ, IDX_DIM), BF16),
            pltpu.VMEM((N_HEADS, T, HEAD_DIM), BF16),
            pltpu.VMEM((IDX_HEADS, T, IDX_DIM), BF16),
            pltpu.VMEM((PAD_KEYS + SEQ, T), F32),
            pltpu.VMEM((PAD_KEYS + SEQ, T), F32),
            pltpu.VMEM((PAD_KEYS + SEQ, N_HEADS * T), F32),
            pltpu.VMEM((N_KV_HEADS, 1, GROUP * T), F32),
            pltpu.VMEM((N_KV_HEADS, 1, GROUP * T), F32),
            pltpu.VMEM((N_KV_HEADS, HEAD_DIM, GROUP * T), F32),
        ],
        compiler_params=_params(("parallel", "arbitrary")),
        name="attn_prompt",
    )(h3, h3, h3, h3, h3, h3, bias_p)


def _score_s_kernel(pt_ref, qi_ref, w_ref, kin_ref, *rest):
    npg = SCORE_ROWS * N_PAGES
    kip = rest[0:npg]
    o_ref = rest[npg]
    kinp_ref = rest[npg + 1]
    del pt_ref
    R = DEC_SEQ
    qrow = lax.broadcasted_iota(jnp.int32, (R, LANE), 0)
    lane = lax.broadcasted_iota(jnp.int32, (R, LANE), 1)
    kinp_ref[...] = jnp.zeros(kinp_ref.shape, BF16)
    for r in range(SCORE_ROWS):
        kinp_ref[r, 0:2 * R] = jnp.concatenate(
            [kin_ref[r][:, :IDX_DIM], jnp.zeros((R, IDX_DIM), F32)], 0).astype(BF16)
    for r in range(SCORE_ROWS):
        qi = qi_ref[r].astype(BF16)
        wb = jnp.broadcast_to(w_ref[r] * SCORE_SCALE, (IDX_HEADS * R, LANE))
        for t in range(N_PAGES + 1):
            if t < N_PAGES:
                d = jnp.dot(qi, kip[r * N_PAGES + t][0].astype(BF16), preferred_element_type=F32)
            else:
                d = lax.dot_general(qi, kinp_ref[r], (((1,), (1,)), ((), ())),
                                    preferred_element_type=F32)
            e = (jnp.maximum(d, 0.0) * wb).reshape(IDX_HEADS, R, LANE)
            s = e[0]
            for h in range(1, IDX_HEADS):
                s = s + e[h]
            if t == N_PAGES:
                s = jnp.where(lane <= qrow, s, NEG_INF)
            o_ref[r, :, t * LANE:(t + 1) * LANE] = s


def _score_sample(h_s, page_table, cache_kidx_t):
    R = DEC_SEQ
    G = SCORE_ROWS
    h3 = h_s.reshape(DEC_BATCH, R, D_H)
    qi_hq = h3[:, :, C_QI:C_QI + IDX_HEADS * IDX_DIM].reshape(DEC_BATCH, R, IDX_HEADS, IDX_DIM)
    qi_hq = qi_hq.transpose(0, 2, 1, 3).reshape(DEC_BATCH, IDX_HEADS * R, IDX_DIM)
    w_hq = h3[:, :, C_KI + W_OFF:C_KI + W_OFF + IDX_HEADS].transpose(0, 2, 1)
    w_hq = w_hq.reshape(DEC_BATCH, IDX_HEADS * R, 1)
    in_specs = [
        pl.BlockSpec((G, IDX_HEADS * R, IDX_DIM), lambda b, pt: (b, 0, 0)),
        pl.BlockSpec((G, IDX_HEADS * R, 1), lambda b, pt: (b, 0, 0)),
        pl.BlockSpec((G, R, LANE), lambda b, pt: (b, 0, C_KI // LANE)),
    ]
    in_specs += [pl.BlockSpec((1, IDX_DIM, PAGE_SIZE),
                              lambda b, pt, r=r, p=p: (pt[b * G + r, p], 0, 0))
                 for r in range(G) for p in range(N_PAGES)]
    grid_spec = pltpu.PrefetchScalarGridSpec(
        num_scalar_prefetch=1,
        grid=(DEC_BATCH // G,),
        in_specs=in_specs,
        out_specs=pl.BlockSpec((G, R, L_SAMPLE), lambda b, pt: (b, 0, 0)),
        scratch_shapes=[pltpu.VMEM((G, PAGE_SIZE, IDX_DIM), BF16)],
    )
    return pl.pallas_call(
        _score_s_kernel,
        grid_spec=grid_spec,
        out_shape=jax.ShapeDtypeStruct((DEC_BATCH, R, L_SAMPLE), F32),
        compiler_params=_params(("arbitrary",)),
        name="score_sample",
    )(page_table, qi_hq, w_hq, h3, *([cache_kidx_t] * (G * N_PAGES)))


def _select_s_kernel(s_ref, o_ref):
    nt = N_PAGES + 1

    def tile_fn(j):
        return s_ref[j * LANE:(j + 1) * LANE, :].reshape(SUBTILES, SUBLANE, LANE)

    def key_fn(j):
        return (j * LANE + _key_iota(SUBTILES)).astype(F32)

    def write_mask(mask_fn):
        for j in range(nt):
            o_ref[j * LANE:(j + 1) * LANE, :] = mask_fn(tile_fn(j), j).reshape(LANE, LANE)

    thr, need, use_ties, count = _select_threshold(tile_fn, nt, static=True)

    @pl.when(jnp.logical_not(use_ties))
    def _():
        write_mask(_mask_plain(thr))

    @pl.when(use_ties)
    def _():
        write_mask(_mask_ties(key_fn, thr, _tie_cut(count, key_fn, thr, need, L_SAMPLE)))


def _select_sample(scores_t):
    n = scores_t.shape[1]
    return pl.pallas_call(
        _select_s_kernel,
        grid=(n // LANE,),
        in_specs=[pl.BlockSpec((L_SAMPLE, LANE), lambda c: (0, c))],
        out_specs=pl.BlockSpec((L_SAMPLE, LANE), lambda c: (0, c)),
        out_shape=jax.ShapeDtypeStruct((L_SAMPLE, n), F32),
        compiler_params=_params(("parallel",)),
        name="select_sample",
    )(scores_t)


def _attn_s_kernel(pt_ref, q_ref, kn_ref, vn_ref, mask_ref, bias_ref, *rest):
    kp = rest[0:N_PAGES]
    vp = rest[N_PAGES:2 * N_PAGES]
    o_ref = rest[2 * N_PAGES]
    knp_ref, vnp_ref, logit_ref = rest[2 * N_PAGES + 1:]
    del pt_ref
    R = DEC_SEQ
    NT = N_PAGES + 1
    GR = GROUP * R

    knp_ref[...] = jnp.zeros(knp_ref.shape, BF16)
    vnp_ref[...] = jnp.zeros(vnp_ref.shape, BF16)
    knp_ref[0:2 * R] = jnp.concatenate([kn_ref[0], jnp.zeros((R, D_KV), F32)], 0).astype(BF16)
    vnp_ref[0:2 * R] = jnp.concatenate([vn_ref[0], jnp.zeros((R, D_KV), F32)], 0).astype(BF16)

    def page_head(refs, t, g):
        if t < N_PAGES:
            return refs[t][pl.ds(g, PAGE_SIZE, stride=N_KV_HEADS), :].astype(BF16)
        pad = knp_ref if refs is kp else vnp_ref
        return pad[:, g * HEAD_DIM:(g + 1) * HEAD_DIM]

    q = (q_ref[0] * QK_SCALE).astype(BF16)
    for t in range(NT):
        mb = mask_ref[0, :, t * LANE:(t + 1) * LANE]
        for g in range(N_KV_HEADS):
            lg = lax.dot_general(q[g * GR:(g + 1) * GR], page_head(kp, t, g),
                                 (((1,), (1,)), ((), ())), preferred_element_type=F32)
            lg = lg.reshape(GROUP, R, LANE) + mb[None]
            if t >= N_PAGES - 1:
                off = (t - (N_PAGES - 1)) * LANE
                lg = lg + bias_ref[g * GROUP:(g + 1) * GROUP, :, off:off + LANE]
            logit_ref[g * GR:(g + 1) * GR, t * LANE:(t + 1) * LANE] = lg.reshape(GR, LANE)

    logits = logit_ref[...]
    m = jnp.max(logits, axis=1, keepdims=True)
    p = jnp.exp(logits - m)
    inv = 1.0 / jnp.sum(p, axis=1, keepdims=True)
    pb = p.astype(BF16)
    outs = [jnp.zeros((GR, HEAD_DIM), F32) for _ in range(N_KV_HEADS)]
    for t in range(NT):
        for g in range(N_KV_HEADS):
            outs[g] = outs[g] + jnp.dot(pb[g * GR:(g + 1) * GR, t * LANE:(t + 1) * LANE],
                                        page_head(vp, t, g), preferred_element_type=F32)
    for g in range(N_KV_HEADS):
        o = outs[g] * inv[g * GR:(g + 1) * GR]
        for hq in range(GROUP):
            h = g * GROUP + hq
            o_ref[0, :, h * HEAD_DIM:(h + 1) * HEAD_DIM] = o[hq * R:(hq + 1) * R]


def _attn_sample(h_s, page_table, cache_k, cache_v, mask, bias_s):
    R = DEC_SEQ
    h3 = h_s.reshape(DEC_BATCH, R, D_H)
    q_hq = h3[:, :, C_Q:C_Q + D_ATTN].reshape(DEC_BATCH, R, N_HEADS, HEAD_DIM)
    q_hq = q_hq.transpose(0, 2, 1, 3).reshape(DEC_BATCH, N_HEADS * R, HEAD_DIM)
    rows_per_page = PAGE_SIZE * N_KV_HEADS
    ck = cache_k.reshape(-1, HEAD_DIM)
    cv = cache_v.reshape(-1, HEAD_DIM)

    in_specs = [
        pl.BlockSpec((1, N_HEADS * R, HEAD_DIM), lambda b, pt: (b, 0, 0)),
        pl.BlockSpec((1, R, D_KV), lambda b, pt: (b, 0, C_K // D_KV)),
        pl.BlockSpec((1, R, D_KV), lambda b, pt: (b, 0, C_V // D_KV)),
        pl.BlockSpec((1, R, L_SAMPLE), lambda b, pt: (b, 0, 0)),
        pl.BlockSpec((N_HEADS, R, 2 * LANE), lambda b, pt: (0, 0, 0)),
    ]
    page = [pl.BlockSpec((rows_per_page, HEAD_DIM), lambda b, pt, p=p: (pt[b, p], 0))
            for p in range(N_PAGES)]
    in_specs += page + page
    grid_spec = pltpu.PrefetchScalarGridSpec(
        num_scalar_prefetch=1,
        grid=(DEC_BATCH,),
        in_specs=in_specs,
        out_specs=pl.BlockSpec((1, R, D_ATTN), lambda b, pt: (b, 0, 0)),
        scratch_shapes=[
            pltpu.VMEM((PAGE_SIZE, D_KV), BF16),
            pltpu.VMEM((PAGE_SIZE, D_KV), BF16),
            pltpu.VMEM((N_HEADS * R, L_SAMPLE), F32),
        ],
    )
    return pl.pallas_call(
        _attn_s_kernel,
        grid_spec=grid_spec,
        out_shape=jax.ShapeDtypeStruct((DEC_BATCH, R, D_ATTN), F32),
        compiler_params=_params(("arbitrary",)),
        name="attn_sample",
    )(page_table, q_hq, h3, h3, mask, bias_s, *([ck] * N_PAGES), *([cv] * N_PAGES))


def _conv_p_kernel(val_ref, gate_ref, cw_ref, cb_ref, dw_ref, ut_ref, pad_ref):
    pad_ref[0:PAD_ROWS] = jnp.zeros((PAD_ROWS, LANE), F32)
    pad_ref[PAD_ROWS:] = val_ref[0] * jax.nn.sigmoid(gate_ref[0])
    ut_ref[0] = pad_ref[SEQ:SEQ + PAD_ROWS]
    cw = cw_ref[...]
    cb = cb_ref[...]
    first = PAD_ROWS - (CONV_WIDTH - 1)
    for c in range(SEQ // CONV_CHUNK):
        base = c * CONV_CHUNK
        acc = jnp.broadcast_to(cb, (CONV_CHUNK, LANE))
        for r in range(SUBLANE):
            taps = [w for w in range(CONV_WIDTH) if (first + w) % SUBLANE == r]
            span = max(first + w - r for w in taps) + CONV_CHUNK
            win = pad_ref[base + r:base + r + span]
            for w in taps:
                a = first + w - r
                acc = acc + win[a:a + CONV_CHUNK] * cw[w:w + 1]
        dw_ref[0, base:base + CONV_CHUNK] = acc


def _conv_prompt(h_p, cw_pad, cb):
    h3 = h_p.reshape(BATCH, SEQ, D_H)
    nc = D_CONV // LANE
    return pl.pallas_call(
        _conv_p_kernel,
        grid=(BATCH, nc),
        in_specs=[pl.BlockSpec((1, SEQ, LANE), lambda b, c: (b, 0, C_GV // LANE + c)),
                  pl.BlockSpec((1, SEQ, LANE), lambda b, c: (b, 0, C_GG // LANE + c)),
                  pl.BlockSpec((PAD_ROWS, LANE), lambda b, c: (0, c)),
                  pl.BlockSpec((1, LANE), lambda b, c: (0, c))],
        out_specs=[pl.BlockSpec((1, SEQ, LANE), lambda b, c: (b, 0, c)),
                   pl.BlockSpec((1, PAD_ROWS, LANE), lambda b, c: (b, 0, c))],
        out_shape=[jax.ShapeDtypeStruct((BATCH, SEQ, D_CONV), F32),
                   jax.ShapeDtypeStruct((BATCH, PAD_ROWS, D_CONV), F32)],
        scratch_shapes=[pltpu.VMEM((PAD_ROWS + SEQ, LANE), F32)],
        compiler_params=_params(("parallel", "parallel")),
        name="conv_prompt",
    )(h3, h3, cw_pad, cb)


def _conv_s_kernel(val_ref, gate_ref, st_ref, cw_ref, cb_ref, dw_ref, ns_ref):
    R = DEC_SEQ
    H = CONV_WIDTH - 1
    cw = cw_ref[...]
    cb = jnp.broadcast_to(cb_ref[...], (DEC_BATCH, LANE))
    u = []
    for q in range(R):
        rows = pl.ds(q, DEC_BATCH, stride=R)
        u.append(val_ref[rows, :] * jax.nn.sigmoid(gate_ref[rows, :]))

    def row(r):
        return st_ref[r] if r < H else u[r - H]

    for q in range(R):
        acc = cb
        for w in range(CONV_WIDTH):
            acc = acc + row(q + w) * cw[w:w + 1]
        dw_ref[pl.ds(q, DEC_BATCH, stride=R), :] = acc
    for r in range(H):
        ns_ref[r] = row(r + R)


def _conv_sample(h_s, state_t, cw_pad, cb):
    n_s = DEC_BATCH * DEC_SEQ
    H = CONV_WIDTH - 1
    nc = D_CONV // LANE
    return pl.pallas_call(
        _conv_s_kernel,
        grid=(nc,),
        in_specs=[pl.BlockSpec((n_s, LANE), lambda c: (0, C_GV // LANE + c)),
                  pl.BlockSpec((n_s, LANE), lambda c: (0, C_GG // LANE + c)),
                  pl.BlockSpec((H, DEC_BATCH, LANE), lambda c: (0, 0, c)),
                  pl.BlockSpec((PAD_ROWS, LANE), lambda c: (0, c)),
                  pl.BlockSpec((1, LANE), lambda c: (0, c))],
        out_specs=[pl.BlockSpec((n_s, LANE), lambda c: (0, c)),
                   pl.BlockSpec((H, DEC_BATCH, LANE), lambda c: (0, 0, c))],
        out_shape=[jax.ShapeDtypeStruct((n_s, D_CONV), F32),
                   jax.ShapeDtypeStruct((H, DEC_BATCH, D_CONV), F32)],
        compiler_params=_params(("parallel",)),
        name="conv_sample",
    )(h_s, h_s, state_t, cw_pad, cb)


def _tail_kernel(attn_ref, za_ref, dw_ref, zc_ref, ga_ref, gc_ref, x_ref,
                 wua_ref, wpw_ref, wuc_ref, wo_ref, ng_ref, nb_ref, bpw_ref, fg_ref, y_ref):
    a = attn_ref[...] * jax.nn.silu(za_ref[...])
    branch_attn = jnp.dot(a.astype(BF16), wua_ref[...], preferred_element_type=F32)

    dw = dw_ref[...]
    mu = jnp.mean(dw, axis=-1, keepdims=True)
    var = jnp.mean(jnp.square(dw - mu), axis=-1, keepdims=True)
    ln = (dw - mu) * lax.rsqrt(var + EPS) * ng_ref[...] + nb_ref[...]
    conv_out = jnp.dot(jax.nn.silu(ln).astype(BF16), wpw_ref[...],
                       preferred_element_type=F32) + bpw_ref[...]
    c = conv_out * jax.nn.silu(zc_ref[...])
    branch_conv = jnp.dot(c.astype(BF16), wuc_ref[...], preferred_element_type=F32)

    merged = jax.nn.sigmoid(ga_ref[...]) * branch_attn + jax.nn.sigmoid(gc_ref[...]) * branch_conv
    y = x_ref[...] + jnp.dot(merged.astype(BF16), wo_ref[...], preferred_element_type=F32)
    ms = jnp.mean(y * y, axis=-1, keepdims=True)
    y_ref[...] = y * lax.rsqrt(ms + EPS) * fg_ref[...]


def _tail(h, attn, dw, x2d, wua, wpw, wuc, wo, ng, nb, bpw, fg, tm):
    n = x2d.shape[0]

    def const(shape):
        return pl.BlockSpec(shape, lambda i: (0, 0), pipeline_mode=pl.Buffered(1))

    return pl.pallas_call(
        _tail_kernel,
        grid=(n // tm,),
        in_specs=[pl.BlockSpec((tm, D_ATTN), lambda i: (i, 0)),
                  pl.BlockSpec((tm, D_ATTN), lambda i: (i, C_ZA // D_ATTN)),
                  pl.BlockSpec((tm, D_CONV), lambda i: (i, 0)),
                  pl.BlockSpec((tm, D_CONV), lambda i: (i, C_ZC // D_CONV)),
                  pl.BlockSpec((tm, D_MODEL), lambda i: (i, C_GA // D_MODEL)),
                  pl.BlockSpec((tm, D_MODEL), lambda i: (i, C_GC // D_MODEL)),
                  pl.BlockSpec((tm, D_MODEL), lambda i: (i, 0)),
                  const((D_ATTN, D_MODEL)), const((D_CONV, D_CONV)),
                  const((D_CONV, D_MODEL)), const((D_MODEL, D_MODEL)),
                  const((1, D_CONV)), const((1, D_CONV)), const((1, D_CONV)),
                  const((1, D_MODEL))],
        out_specs=pl.BlockSpec((tm, D_MODEL), lambda i: (i, 0)),
        out_shape=jax.ShapeDtypeStruct((n, D_MODEL), F32),
        compiler_params=_params(("parallel",)),
        name="tail",
    )(attn, h, dw, h, h, h, x2d, wua, wpw, wuc, wo, ng, nb, bpw, fg)


def _reorder_w_in(w_in):
    w_t = w_in.T

    def rows(name):
        o, w = _SRC[name]
        return w_t[o:o + w]

    pad = jnp.zeros((D_H - C_KI - IDX_DIM - IDX_HEADS, D_MODEL), w_in.dtype)
    w = jnp.concatenate([rows('gate_attn'), rows('gate_conv'), rows('q'), rows('z_attn'),
                         rows('glu_val'), rows('glu_gate'), rows('z_conv'), rows('k'), rows('v'),
                         rows('q_idx'), rows('k_idx'), rows('w_idx'), pad], axis=0)
    return w.astype(BF16)


def kernel(x_prompt, x_sample, cache_k, cache_v, cache_kidx, state_conv, page_table,
           ln_g, w_in, conv_w, conv_b, conv_norm_g, conv_norm_b, w_pw, b_pw,
           w_up_attn, w_up_conv, w_out, rel_bias, final_g):
    w_all = _reorder_w_in(w_in[0])
    g_in = ln_g[0].reshape(1, D_MODEL)
    xp = x_prompt.reshape(BATCH * SEQ, D_MODEL)
    xs = x_sample.reshape(DEC_BATCH * DEC_SEQ, D_MODEL)
    h_p, k_p, v_p = _proj(xp, g_in, w_all)
    h_s, k_s, v_s = _proj(xs, g_in, w_all)

    bias_p, bias_s = _bias_tables(rel_bias)

    attn_p = _attn_prompt(h_p, bias_p).reshape(BATCH * SEQ, D_ATTN)

    n_s = DEC_BATCH * DEC_SEQ
    scores = _score_sample(h_s, page_table, cache_kidx[0].transpose(0, 2, 1))
    mask_t = _select_sample(scores.reshape(n_s, L_SAMPLE).T)
    mask = mask_t.T.reshape(DEC_BATCH, DEC_SEQ, L_SAMPLE)
    attn_s = _attn_sample(h_s, page_table, cache_k[0], cache_v[0], mask, bias_s)
    attn_s = attn_s.reshape(n_s, D_ATTN)

    cw_pad = jnp.concatenate([conv_w[0], jnp.zeros((PAD_ROWS - CONV_WIDTH, D_CONV), F32)], 0)
    cb = conv_b[0].reshape(1, D_CONV)
    dw_p, u_tail = _conv_prompt(h_p, cw_pad, cb)
    dw_s, state_new = _conv_sample(h_s, state_conv[0].transpose(1, 0, 2), cw_pad, cb)

    wua = w_up_attn[0].astype(BF16)
    wpw = w_pw[0].astype(BF16)
    wuc = w_up_conv[0].astype(BF16)
    wo = w_out[0].astype(BF16)
    ng = conv_norm_g[0].reshape(1, D_CONV)
    nb = conv_norm_b[0].reshape(1, D_CONV)
    bpw = b_pw[0].reshape(1, D_CONV)
    fg = final_g.reshape(1, D_MODEL)
    y_p = _tail(h_p, attn_p, dw_p.reshape(BATCH * SEQ, D_CONV), xp,
                wua, wpw, wuc, wo, ng, nb, bpw, fg, TAIL_TM)
    y_s = _tail(h_s, attn_s, dw_s, xs,
                wua, wpw, wuc, wo, ng, nb, bpw, fg, TAIL_TM)

    hp3 = h_p.reshape(BATCH, SEQ, D_H)
    hs3 = h_s.reshape(DEC_BATCH, DEC_SEQ, D_H)
    tail_rows = CONV_WIDTH - 1
    return (
        y_p.reshape(BATCH, SEQ, D_MODEL),
        y_s.reshape(DEC_BATCH, DEC_SEQ, D_MODEL),
        k_p.reshape(1, BATCH, SEQ, N_KV_HEADS, HEAD_DIM),
        v_p.reshape(1, BATCH, SEQ, N_KV_HEADS, HEAD_DIM),
        hp3[:, :, C_KI:C_KI + IDX_DIM].reshape(1, BATCH, SEQ, IDX_DIM),
        u_tail[:, PAD_ROWS - tail_rows:].reshape(1, BATCH, tail_rows, D_CONV),
        k_s.reshape(1, DEC_BATCH, DEC_SEQ, N_KV_HEADS, HEAD_DIM),
        v_s.reshape(1, DEC_BATCH, DEC_SEQ, N_KV_HEADS, HEAD_DIM),
        hs3[:, :, C_KI:C_KI + IDX_DIM].reshape(1, DEC_BATCH, DEC_SEQ, IDX_DIM),
        state_new.transpose(1, 0, 2).reshape(1, DEC_BATCH, tail_rows, D_CONV),
    )
```

```python
import math

import numpy as np
import jax
import jax.numpy as jnp
from jax import lax
from jax.experimental import pallas as pl
from jax.experimental.pallas import tpu as pltpu

F32 = jnp.float32
BF16 = jnp.bfloat16

D_MODEL = 2048
BATCH = 8
SEQ = 2048
DEC_BATCH = 128
DEC_SEQ = 8
PAST_LEN = 2048
PAGE_SIZE = 128
N_PAGES = PAST_LEN // PAGE_SIZE
N_HEADS = 8
N_KV_HEADS = 2
HEAD_DIM = 128
GROUP = N_HEADS // N_KV_HEADS
D_ATTN = N_HEADS * HEAD_DIM
D_KV = N_KV_HEADS * HEAD_DIM
IDX_HEADS = 8
IDX_DIM = 64
TOPK = 256
Q_BLOCK = 128
N_BUCKETS = 32
MAX_DISTANCE = 128
D_CONV = D_MODEL // 2
CONV_WIDTH = 31
EPS = 1e-6
NEG = -1e30
NEG_INF = float("-inf")

LANE = 128
SUBLANE = 8
VMEM_LIMIT = 56 * 1024 * 1024

C_GA = 0
C_GC = C_GA + D_MODEL
C_Q = C_GC + D_MODEL
C_ZA = C_Q + D_ATTN
C_GV = C_ZA + D_ATTN
C_GG = C_GV + D_CONV
C_ZC = C_GG + D_CONV
C_K = C_ZC + D_CONV
C_V = C_K + D_KV
C_QI = C_V + D_KV
C_KI = C_QI + IDX_HEADS * IDX_DIM
W_OFF = IDX_DIM
MXU_WIDTH = 256
PROJ_TM = 1024
PROJ_TN = 6 * MXU_WIDTH
D_H = -(-(C_KI + LANE) // PROJ_TN) * PROJ_TN
KV_TILE = C_K // PROJ_TN
assert C_K % PROJ_TN == 0 and D_H == C_K + PROJ_TN
D_HB = C_K
D_HF = C_KI + LANE - C_K
F_K = C_K - D_HB
F_V = C_V - D_HB
F_QI = C_QI - D_HB
F_KI = C_KI - D_HB

_SRC = {}
_off = 0
for _name, _w in (('q', D_ATTN), ('k', D_KV), ('v', D_KV), ('z_attn', D_ATTN),
                  ('q_idx', IDX_HEADS * IDX_DIM), ('k_idx', IDX_DIM), ('w_idx', IDX_HEADS),
                  ('glu_val', D_CONV), ('glu_gate', D_CONV), ('z_conv', D_CONV),
                  ('gate_attn', D_MODEL), ('gate_conv', D_MODEL)):
    _SRC[_name] = (_off, _w)
    _off += _w

N_BISECT = 20
PAD_ROWS = 32
CONV_CHUNK = 64
SCORE_SCALE = (IDX_DIM ** -0.5) * (IDX_HEADS ** -0.5)
QK_SCALE = HEAD_DIM ** -0.5
LOG2E = math.log2(math.e)
L_SAMPLE = (N_PAGES + 1) * LANE
SUBTILES = LANE // SUBLANE
CHUNK_BLOCKS = 4
CHUNK = CHUNK_BLOCKS * Q_BLOCK
PAD_KEYS = CHUNK - Q_BLOCK
VT_ROWS = HEAD_DIM + 2 * SUBLANE
SCORE_ROWS = 4
TAIL_TM = 256


def _t5_bucket_static(dist):
    n = np.maximum(dist, 0)
    max_exact = N_BUCKETS // 2
    ratio = (np.log(np.maximum(n, 1).astype(np.float32) / np.float32(max_exact))
             / np.float32(math.log(MAX_DISTANCE / max_exact)))
    large = np.minimum(max_exact + (ratio * np.float32(N_BUCKETS - max_exact)).astype(np.int32),
                       N_BUCKETS - 1)
    return np.where(n < max_exact, n, large).astype(np.int32)


FAR_BUCKET = int(_t5_bucket_static(np.array([2 * MAX_DISTANCE]))[0])


def _params(sem):
    return pltpu.CompilerParams(dimension_semantics=sem, vmem_limit_bytes=VMEM_LIMIT)


def _proj_kernel(x_ref, g_ref, w_ref, ob_ref, of_ref, k_ref, v_ref, xn_ref):
    tm = x_ref.shape[0]
    j = pl.program_id(1)

    @pl.when(j == 0)
    def _():
        x = x_ref[...]
        ms = jnp.mean(x * x, axis=-1, keepdims=True)
        xn_ref[...] = (x * lax.rsqrt(ms + EPS) * g_ref[...]).astype(BF16)

    def project():
        return lax.dot_general(xn_ref[...], w_ref[...], (((1,), (1,)), ((), ())),
                               preferred_element_type=F32)

    @pl.when(j < KV_TILE)
    def _():
        ob_ref[...] = project().astype(BF16)

    @pl.when(j == KV_TILE)
    def _():
        of_ref[...] = project()[:, :D_HF]
        for g in range(N_KV_HEADS):
            rows = pl.ds(g, tm, stride=N_KV_HEADS)
            k_ref[rows, :] = of_ref[:, F_K + g * HEAD_DIM:F_K + (g + 1) * HEAD_DIM]
            v_ref[rows, :] = of_ref[:, F_V + g * HEAD_DIM:F_V + (g + 1) * HEAD_DIM]


def _proj(x2d, g, w):
    n = x2d.shape[0]
    tm, tn = PROJ_TM, PROJ_TN
    kv_spec = pl.BlockSpec((N_KV_HEADS * tm, HEAD_DIM), lambda i, j: (i, 0))
    kv_shape = jax.ShapeDtypeStruct((N_KV_HEADS * n, HEAD_DIM), F32)
    return pl.pallas_call(
        _proj_kernel,
        grid=(n // tm, D_H // tn),
        in_specs=[pl.BlockSpec((tm, D_MODEL), lambda i, j: (i, 0)),
                  pl.BlockSpec((1, D_MODEL), lambda i, j: (0, 0)),
                  pl.BlockSpec((tn, D_MODEL), lambda i, j: (j, 0))],
        out_specs=[pl.BlockSpec((tm, tn), lambda i, j: (i, jnp.minimum(j, KV_TILE - 1))),
                   pl.BlockSpec((tm, D_HF), lambda i, j: (i, 0)), kv_spec, kv_spec],
        out_shape=[jax.ShapeDtypeStruct((n, D_HB), BF16),
                   jax.ShapeDtypeStruct((n, D_HF), F32), kv_shape, kv_shape],
        scratch_shapes=[pltpu.VMEM((tm, D_MODEL), BF16)],
        compiler_params=_params(("parallel", "arbitrary")),
        name="proj",
    )(x2d, g, w)


def _bias_kernel(rb_ref, bp_ref, bs_ref, op_ref, os_ref):
    bp = bp_ref[...]
    bs = bs_ref[...]
    for h in range(N_HEADS):
        far = rb_ref[FAR_BUCKET, h]
        tp = jnp.zeros(bp.shape, F32)
        ts = jnp.zeros(bs.shape, F32)
        for b in range(N_BUCKETS):
            val = rb_ref[b, h] - far
            tp = jnp.where(bp == b, val * LOG2E, tp)
            ts = jnp.where(bs == b, val, ts)
        op_ref[h] = tp
        os_ref[h] = ts


def _bias_tables(rel_bias):
    key = np.arange(CHUNK)[:, None]
    qry = np.arange(Q_BLOCK)[None, :]
    bucket_p = _t5_bucket_static(CHUNK - Q_BLOCK + qry - key)
    qi = np.arange(DEC_SEQ)[:, None]
    col = np.arange(2 * LANE)[None, :]
    dist_s = np.where(col < LANE, LANE + qi - col, qi - (col - LANE))
    bucket_s = _t5_bucket_static(dist_s)
    return pl.pallas_call(
        _bias_kernel,
        in_specs=[pl.BlockSpec(memory_space=pltpu.SMEM),
                  pl.BlockSpec(memory_space=pltpu.VMEM),
                  pl.BlockSpec(memory_space=pltpu.VMEM)],
        out_specs=[pl.BlockSpec(memory_space=pltpu.VMEM),
                   pl.BlockSpec(memory_space=pltpu.VMEM)],
        out_shape=[jax.ShapeDtypeStruct((N_HEADS, CHUNK, Q_BLOCK), F32),
                   jax.ShapeDtypeStruct((N_HEADS, DEC_SEQ, 2 * LANE), F32)],
        name="bias_tables",
    )(rel_bias, jnp.asarray(bucket_p), jnp.asarray(bucket_s))


def _any(x):
    return jnp.max(jnp.where(x, 1.0, 0.0)) > 0.5


def _rep(x):
    return jnp.broadcast_to(x, (SUBLANE, LANE))


def _fold_rows(x, comb):
    parts = [x[k:k + SUBLANE] for k in range(0, x.shape[0], SUBLANE)]
    while len(parts) > 1:
        parts = [comb(parts[k], parts[k + 1]) for k in range(0, len(parts), 2)]
    return parts[0]


def _select_threshold(tile_fn, ntiles, static):
    def reduce_tiles(fn, init, comb):
        def step(j, acc):
            x = fn(tile_fn(j), j)
            parts = [x[k] for k in range(x.shape[0])]
            while len(parts) > 1:
                parts = [comb(parts[k], parts[k + 1]) for k in range(0, len(parts), 2)]
            return comb(acc, parts[0])
        if static:
            acc = init
            for j in range(ntiles):
                acc = step(j, acc)
            return acc
        return lax.fori_loop(0, ntiles, step, init)

    zeros = jnp.zeros((SUBLANE, LANE), F32)

    def count(pred_fn):
        acc = reduce_tiles(lambda s, j: jnp.where(pred_fn(s, j), 1.0, 0.0), zeros,
                           lambda a, b: a + b)
        return _rep(jnp.sum(acc, axis=0, keepdims=True))

    def masked_max(pred_fn):
        acc = reduce_tiles(lambda s, j: jnp.where(pred_fn(s, j), s, NEG_INF),
                           jnp.full((SUBLANE, LANE), NEG_INF, F32), jnp.maximum)
        return _rep(jnp.max(acc, axis=0, keepdims=True))

    bound = reduce_tiles(lambda s, j: jnp.where(s > NEG_INF, jnp.abs(s), 0.0), zeros,
                         jnp.maximum)
    bound = _rep(jnp.max(bound, axis=0, keepdims=True))

    def bisect(_, carry):
        lo, hi = carry
        mid = 0.5 * lo + 0.5 * hi
        few = count(lambda s, j: s > mid[None]) < TOPK
        return jnp.where(few, lo, mid), jnp.where(few, mid, hi)

    _, hi = lax.fori_loop(0, N_BISECT, bisect, (-bound, bound))

    thr = masked_max(lambda s, j: s <= hi[None])
    n_ge = count(lambda s, j: s >= thr[None])

    def fix_body(carry):
        thr, n_ge, _ = carry
        lower = masked_max(lambda s, j: s < thr[None])
        thr = jnp.where(n_ge < TOPK, lower, thr)
        n_ge = count(lambda s, j: s >= thr[None])
        return thr, n_ge, _any(n_ge < TOPK)

    thr, n_ge, _ = lax.while_loop(lambda c: c[2], fix_body, (thr, n_ge, _any(n_ge < TOPK)))
    n_gt = count(lambda s, j: s > thr[None])
    need = TOPK - n_gt
    return thr[0:1], need[0:1]


def _keep(pred):
    return jnp.where(pred, 0.0, NEG)


def _selection_mask(s, thr, need, seen, tri):
    tie = s == thr
    rank = jnp.dot(tri, jnp.where(tie, 1.0, 0.0).astype(BF16),
                   preferred_element_type=F32) + seen
    mask = jnp.where(tie, _keep(rank <= need), _keep(s > thr))
    return mask, rank[s.shape[0] - 1:]


def _attn_p_kernel(q_ref, qi_ref, wi_ref, k_ref, v_ref, ki_ref, bias_ref, tri_ref, o_ref,
                   kb_ref, vt_ref, kib_ref, qh_ref, qih_ref,
                   score_ref, mask_ref, s_ref, m_ref, acc_ref):
    i = pl.program_id(1)
    T = Q_BLOCK
    W = GROUP * T
    nch = i // CHUNK_BLOCKS + 1

    @pl.when(i == 0)
    def _():
        kb_ref[0:PAD_KEYS] = jnp.zeros((PAD_KEYS, D_KV), BF16)
        kb_ref[PAD_KEYS:] = k_ref[0].astype(BF16)
        kib_ref[0:PAD_KEYS] = jnp.zeros((PAD_KEYS, IDX_DIM), BF16)
        kib_ref[PAD_KEYS:] = ki_ref[0][:, :IDX_DIM].astype(BF16)
        ones_row = lax.broadcasted_iota(jnp.int32, (VT_ROWS - HEAD_DIM, PAD_KEYS + SEQ), 0) == 0
        for g in range(N_KV_HEADS):
            vt_ref[g, 0:HEAD_DIM, 0:PAD_KEYS] = jnp.zeros((HEAD_DIM, PAD_KEYS), BF16)
            vt_ref[g, HEAD_DIM:VT_ROWS, :] = jnp.where(ones_row, 1.0, 0.0).astype(BF16)
            for c in range(SEQ // LANE):
                blk = v_ref[0, c * LANE:(c + 1) * LANE, g * HEAD_DIM:(g + 1) * HEAD_DIM]
                vt_ref[g, 0:HEAD_DIM,
                       PAD_KEYS + c * LANE:PAD_KEYS + (c + 1) * LANE] = blk.T.astype(BF16)
        score_ref[0:PAD_KEYS] = jnp.full((PAD_KEYS, T), NEG_INF, F32)
        mask_ref[0:PAD_KEYS] = jnp.full((PAD_KEYS, T), NEG, F32)

    q = q_ref[0].astype(F32) * (QK_SCALE * LOG2E)
    for h in range(N_HEADS):
        qh_ref[h] = q[:, h * HEAD_DIM:(h + 1) * HEAD_DIM].astype(BF16)
    qi = qi_ref[0]
    for h in range(IDX_HEADS):
        qih_ref[h] = qi[:, h * IDX_DIM:(h + 1) * IDX_DIM].astype(BF16)
    w_rows = wi_ref[0].T[W_OFF:W_OFF + IDX_HEADS] * SCORE_SCALE

    def span(c):
        return pl.ds(pl.multiple_of((i - CHUNK_BLOCKS * c) * LANE, LANE), CHUNK)

    def first_key(c):
        return (i - CHUNK_BLOCKS * c) * LANE - PAD_KEYS

    key_l = lax.broadcasted_iota(jnp.int32, (CHUNK, T), 0)
    qry = i * T + lax.broadcasted_iota(jnp.int32, (CHUNK, T), 1)

    def score_body(c, _):
        kc = kib_ref[span(c), :]
        d = lax.dot_general(kc, qih_ref[...].reshape(IDX_HEADS * T, IDX_DIM),
                            (((1,), (1,)), ((), ())), preferred_element_type=F32)
        acc = jnp.zeros((CHUNK, T), F32)
        for h in range(IDX_HEADS):
            acc = acc + jnp.maximum(d[:, h * T:(h + 1) * T], 0.0) * w_rows[h:h + 1]
        key = first_key(c) + key_l
        acc = jnp.where(key <= qry, jnp.where(key >= 0, acc, NEG_INF), NEG_INF)
        score_ref[span(c), :] = acc
        return 0

    lax.fori_loop(0, nch, score_body, 0)

    nsub = CHUNK // SUBLANE

    def tile_fn(c):
        return score_ref[span(c), :].reshape(nsub, SUBLANE, T)

    @pl.when(i * T + T <= TOPK)
    def _():
        def body(c, _):
            mask_ref[span(c), :] = _keep(score_ref[span(c), :] > NEG_INF)
            return 0
        lax.fori_loop(0, nch, body, 0)

    @pl.when(i * T + T > TOPK)
    def _():
        thr, need = _select_threshold(tile_fn, nch, static=False)

        def body(k, seen):
            c = nch - 1 - k
            start = pl.multiple_of((i - CHUNK_BLOCKS * c) * LANE, LANE)
            for part in range(CHUNK // MXU_WIDTH):
                rows = pl.ds(start + part * MXU_WIDTH, MXU_WIDTH)
                mask, seen = _selection_mask(score_ref[rows, :], thr, need, seen, tri_ref[...])
                mask_ref[rows, :] = mask
            return seen

        lax.fori_loop(0, nch, body, jnp.zeros((1, T), F32))

    m_ref[...] = jnp.full(m_ref.shape, NEG, F32)
    acc_ref[...] = jnp.zeros(acc_ref.shape, F32)

    def logits(c, with_bias):
        mb = mask_ref[span(c), :]
        for g in range(N_KV_HEADS):
            kc = kb_ref[span(c), g * HEAD_DIM:(g + 1) * HEAD_DIM]
            qg = qh_ref[g * GROUP:(g + 1) * GROUP].reshape(W, HEAD_DIM)
            s = lax.dot_general(kc, qg, (((1,), (1,)), ((), ())), preferred_element_type=F32)
            if with_bias:
                add = jnp.concatenate([mb + bias_ref[g * GROUP + hq] for hq in range(GROUP)],
                                      axis=1)
            else:
                add = jnp.concatenate([mb] * GROUP, axis=1)
            s = s + add
            s_ref[span(c), g * W:(g + 1) * W] = s
            top = jnp.max(_fold_rows(s, jnp.maximum), axis=0, keepdims=True)
            m_ref[g] = jnp.maximum(m_ref[g], top)

    logits(0, True)

    def logits_body(c, _):
        logits(c, False)
        return 0

    lax.fori_loop(1, nch, logits_body, 0)

    def weigh(c, _):
        for g in range(N_KV_HEADS):
            p = jnp.exp2(s_ref[span(c), g * W:(g + 1) * W] - m_ref[g])
            acc_ref[g] += jnp.dot(vt_ref[g, :, span(c)], p.astype(BF16),
                                  preferred_element_type=F32)
        return 0

    lax.fori_loop(0, nch, weigh, 0)

    for g in range(N_KV_HEADS):
        o = acc_ref[g, 0:HEAD_DIM] / acc_ref[g, HEAD_DIM:HEAD_DIM + 1]
        for hq in range(GROUP):
            h = g * GROUP + hq
            o_ref[0, :, h * HEAD_DIM:(h + 1) * HEAD_DIM] = o[:, hq * T:(hq + 1) * T].T


def _attn_prompt(hb_p, hf_p, bias_p):
    hb3 = hb_p.reshape(BATCH, SEQ, D_HB)
    hf3 = hf_p.reshape(BATCH, SEQ, D_HF)
    nqb = SEQ // Q_BLOCK
    T = Q_BLOCK
    qi_w = IDX_HEADS * IDX_DIM
    return pl.pallas_call(
        _attn_p_kernel,
        grid=(BATCH, nqb),
        in_specs=[
            pl.BlockSpec((1, T, D_ATTN), lambda b, i: (b, i, C_Q // D_ATTN)),
            pl.BlockSpec((1, T, qi_w), lambda b, i: (b, i, F_QI // qi_w)),
            pl.BlockSpec((1, T, LANE), lambda b, i: (b, i, F_KI // LANE)),
            pl.BlockSpec((1, SEQ, D_KV), lambda b, i: (b, 0, F_K // D_KV)),
            pl.BlockSpec((1, SEQ, D_KV), lambda b, i: (b, 0, F_V // D_KV)),
            pl.BlockSpec((1, SEQ, LANE), lambda b, i: (b, 0, F_KI // LANE)),
            pl.BlockSpec((N_HEADS, CHUNK, T), lambda b, i: (0, 0, 0)),
            pl.BlockSpec((MXU_WIDTH, MXU_WIDTH), lambda b, i: (0, 0)),
        ],
        out_specs=pl.BlockSpec((1, T, D_ATTN), lambda b, i: (b, i, 0)),
        out_shape=jax.ShapeDtypeStruct((BATCH, SEQ, D_ATTN), F32),
        scratch_shapes=[
            pltpu.VMEM((PAD_KEYS + SEQ, D_KV), BF16),
            pltpu.VMEM((N_KV_HEADS, VT_ROWS, PAD_KEYS + SEQ), BF16),
            pltpu.VMEM((PAD_KEYS + SEQ, IDX_DIM), BF16),
            pltpu.VMEM((N_HEADS, T, HEAD_DIM), BF16),
            pltpu.VMEM((IDX_HEADS, T, IDX_DIM), BF16),
            pltpu.VMEM((PAD_KEYS + SEQ, T), F32),
            pltpu.VMEM((PAD_KEYS + SEQ, T), F32),
            pltpu.VMEM((PAD_KEYS + SEQ, N_HEADS * T), F32),
            pltpu.VMEM((N_KV_HEADS, 1, GROUP * T), F32),
            pltpu.VMEM((N_KV_HEADS, VT_ROWS, GROUP * T), F32),
        ],
        compiler_params=_params(("parallel", "arbitrary")),
        name="attn_prompt",
    )(hb3, hf3, hf3, hf3, hf3, hf3, bias_p, jnp.tri(MXU_WIDTH, dtype=BF16))


def _score_s_kernel(pt_ref, qi_ref, w_ref, kin_ref, *rest):
    npg = SCORE_ROWS * N_PAGES
    kip = rest[0:npg]
    o_ref = rest[npg]
    kinp_ref = rest[npg + 1]
    del pt_ref
    R = DEC_SEQ
    qrow = lax.broadcasted_iota(jnp.int32, (R, LANE), 0)
    lane = lax.broadcasted_iota(jnp.int32, (R, LANE), 1)
    kinp_ref[...] = jnp.zeros(kinp_ref.shape, BF16)
    for r in range(SCORE_ROWS):
        kinp_ref[r, 0:2 * R] = jnp.concatenate(
            [kin_ref[r][:, :IDX_DIM], jnp.zeros((R, IDX_DIM), F32)], 0).astype(BF16)
    for r in range(SCORE_ROWS):
        qi = qi_ref[r].astype(BF16)
        wb = jnp.broadcast_to(w_ref[r] * SCORE_SCALE, (IDX_HEADS * R, LANE))
        for t in range(N_PAGES + 1):
            if t < N_PAGES:
                d = jnp.dot(qi, kip[r * N_PAGES + t][0].astype(BF16), preferred_element_type=F32)
            else:
                d = lax.dot_general(qi, kinp_ref[r], (((1,), (1,)), ((), ())),
                                    preferred_element_type=F32)
            e = (jnp.maximum(d, 0.0) * wb).reshape(IDX_HEADS, R, LANE)
            s = e[0]
            for h in range(1, IDX_HEADS):
                s = s + e[h]
            if t == N_PAGES:
                s = jnp.where(lane <= qrow, s, NEG_INF)
            o_ref[r, :, t * LANE:(t + 1) * LANE] = s


def _score_sample(hf_s, page_table, cache_kidx_t):
    R = DEC_SEQ
    G = SCORE_ROWS
    h3 = hf_s.reshape(DEC_BATCH, R, D_HF)
    qi_hq = h3[:, :, F_QI:F_QI + IDX_HEADS * IDX_DIM].reshape(DEC_BATCH, R, IDX_HEADS, IDX_DIM)
    qi_hq = qi_hq.transpose(0, 2, 1, 3).reshape(DEC_BATCH, IDX_HEADS * R, IDX_DIM)
    w_hq = h3[:, :, F_KI + W_OFF:F_KI + W_OFF + IDX_HEADS].transpose(0, 2, 1)
    w_hq = w_hq.reshape(DEC_BATCH, IDX_HEADS * R, 1)
    in_specs = [
        pl.BlockSpec((G, IDX_HEADS * R, IDX_DIM), lambda b, pt: (b, 0, 0)),
        pl.BlockSpec((G, IDX_HEADS * R, 1), lambda b, pt: (b, 0, 0)),
        pl.BlockSpec((G, R, LANE), lambda b, pt: (b, 0, F_KI // LANE)),
    ]
    in_specs += [pl.BlockSpec((1, IDX_DIM, PAGE_SIZE),
                              lambda b, pt, r=r, p=p: (pt[b * G + r, p], 0, 0))
                 for r in range(G) for p in range(N_PAGES)]
    grid_spec = pltpu.PrefetchScalarGridSpec(
        num_scalar_prefetch=1,
        grid=(DEC_BATCH // G,),
        in_specs=in_specs,
        out_specs=pl.BlockSpec((G, R, L_SAMPLE), lambda b, pt: (b, 0, 0)),
        scratch_shapes=[pltpu.VMEM((G, PAGE_SIZE, IDX_DIM), BF16)],
    )
    return pl.pallas_call(
        _score_s_kernel,
        grid_spec=grid_spec,
        out_shape=jax.ShapeDtypeStruct((DEC_BATCH, R, L_SAMPLE), F32),
        compiler_params=_params(("arbitrary",)),
        name="score_sample",
    )(page_table, qi_hq, w_hq, h3, *([cache_kidx_t] * (G * N_PAGES)))


def _select_s_kernel(s_ref, tri_ref, o_ref):
    nt = N_PAGES + 1

    def tile_fn(j):
        return s_ref[j * LANE:(j + 1) * LANE, :].reshape(SUBTILES, SUBLANE, LANE)

    thr, need = _select_threshold(tile_fn, nt, static=True)
    seen = jnp.zeros((1, LANE), F32)
    for j in range(nt):
        rows = slice(j * LANE, (j + 1) * LANE)
        o_ref[rows, :], seen = _selection_mask(s_ref[rows, :], thr, need, seen, tri_ref[...])


def _select_sample(scores_t):
    n = scores_t.shape[1]
    return pl.pallas_call(
        _select_s_kernel,
        grid=(n // LANE,),
        in_specs=[pl.BlockSpec((L_SAMPLE, LANE), lambda c: (0, c)),
                  pl.BlockSpec((LANE, LANE), lambda c: (0, 0))],
        out_specs=pl.BlockSpec((L_SAMPLE, LANE), lambda c: (0, c)),
        out_shape=jax.ShapeDtypeStruct((L_SAMPLE, n), F32),
        compiler_params=_params(("parallel",)),
        name="select_sample",
    )(scores_t, jnp.tri(LANE, dtype=BF16))


def _attn_s_kernel(pt_ref, q_ref, kn_ref, vn_ref, mask_ref, bias_ref, *rest):
    kp = rest[0:N_PAGES]
    vp = rest[N_PAGES:2 * N_PAGES]
    o_ref = rest[2 * N_PAGES]
    knp_ref, vnp_ref, logit_ref = rest[2 * N_PAGES + 1:]
    del pt_ref
    R = DEC_SEQ
    NT = N_PAGES + 1
    GR = GROUP * R

    knp_ref[...] = jnp.zeros(knp_ref.shape, BF16)
    vnp_ref[...] = jnp.zeros(vnp_ref.shape, BF16)
    knp_ref[0:2 * R] = jnp.concatenate([kn_ref[0], jnp.zeros((R, D_KV), F32)], 0).astype(BF16)
    vnp_ref[0:2 * R] = jnp.concatenate([vn_ref[0], jnp.zeros((R, D_KV), F32)], 0).astype(BF16)

    def page_head(refs, t, g):
        if t < N_PAGES:
            return refs[t][pl.ds(g, PAGE_SIZE, stride=N_KV_HEADS), :].astype(BF16)
        pad = knp_ref if refs is kp else vnp_ref
        return pad[:, g * HEAD_DIM:(g + 1) * HEAD_DIM]

    q = (q_ref[0].astype(F32) * QK_SCALE).astype(BF16)
    for t in range(NT):
        mb = mask_ref[0, :, t * LANE:(t + 1) * LANE]
        for g in range(N_KV_HEADS):
            lg = lax.dot_general(q[g * GR:(g + 1) * GR], page_head(kp, t, g),
                                 (((1,), (1,)), ((), ())), preferred_element_type=F32)
            lg = lg.reshape(GROUP, R, LANE) + mb[None]
            if t >= N_PAGES - 1:
                off = (t - (N_PAGES - 1)) * LANE
                lg = lg + bias_ref[g * GROUP:(g + 1) * GROUP, :, off:off + LANE]
            logit_ref[g * GR:(g + 1) * GR, t * LANE:(t + 1) * LANE] = lg.reshape(GR, LANE)

    logits = logit_ref[...]
    m = jnp.max(logits, axis=1, keepdims=True)
    p = jnp.exp(logits - m)
    inv = 1.0 / jnp.sum(p, axis=1, keepdims=True)
    pb = p.astype(BF16)
    outs = [jnp.zeros((GR, HEAD_DIM), F32) for _ in range(N_KV_HEADS)]
    for t in range(NT):
        for g in range(N_KV_HEADS):
            outs[g] = outs[g] + jnp.dot(pb[g * GR:(g + 1) * GR, t * LANE:(t + 1) * LANE],
                                        page_head(vp, t, g), preferred_element_type=F32)
    for g in range(N_KV_HEADS):
        o = outs[g] * inv[g * GR:(g + 1) * GR]
        for hq in range(GROUP):
            h = g * GROUP + hq
            o_ref[0, :, h * HEAD_DIM:(h + 1) * HEAD_DIM] = o[hq * R:(hq + 1) * R]


def _attn_sample(hb_s, hf_s, page_table, cache_k, cache_v, mask, bias_s):
    R = DEC_SEQ
    h3 = hf_s.reshape(DEC_BATCH, R, D_HF)
    q_hq = hb_s[:, C_Q:C_Q + D_ATTN].reshape(DEC_BATCH, R, N_HEADS, HEAD_DIM)
    q_hq = q_hq.transpose(0, 2, 1, 3).reshape(DEC_BATCH, N_HEADS * R, HEAD_DIM)
    rows_per_page = PAGE_SIZE * N_KV_HEADS
    ck = cache_k.reshape(-1, HEAD_DIM)
    cv = cache_v.reshape(-1, HEAD_DIM)

    in_specs = [
        pl.BlockSpec((1, N_HEADS * R, HEAD_DIM), lambda b, pt: (b, 0, 0)),
        pl.BlockSpec((1, R, D_KV), lambda b, pt: (b, 0, F_K // D_KV)),
        pl.BlockSpec((1, R, D_KV), lambda b, pt: (b, 0, F_V // D_KV)),
        pl.BlockSpec((1, R, L_SAMPLE), lambda b, pt: (b, 0, 0)),
        pl.BlockSpec((N_HEADS, R, 2 * LANE), lambda b, pt: (0, 0, 0)),
    ]
    page = [pl.BlockSpec((rows_per_page, HEAD_DIM), lambda b, pt, p=p: (pt[b, p], 0))
            for p in range(N_PAGES)]
    in_specs += page + page
    grid_spec = pltpu.PrefetchScalarGridSpec(
        num_scalar_prefetch=1,
        grid=(DEC_BATCH,),
        in_specs=in_specs,
        out_specs=pl.BlockSpec((1, R, D_ATTN), lambda b, pt: (b, 0, 0)),
        scratch_shapes=[
            pltpu.VMEM((PAGE_SIZE, D_KV), BF16),
            pltpu.VMEM((PAGE_SIZE, D_KV), BF16),
            pltpu.VMEM((N_HEADS * R, L_SAMPLE), F32),
        ],
    )
    return pl.pallas_call(
        _attn_s_kernel,
        grid_spec=grid_spec,
        out_shape=jax.ShapeDtypeStruct((DEC_BATCH, R, D_ATTN), F32),
        compiler_params=_params(("arbitrary",)),
        name="attn_sample",
    )(page_table, q_hq, h3, h3, mask, bias_s, *([ck] * N_PAGES), *([cv] * N_PAGES))


def _conv_p_kernel(val_ref, gate_ref, cw_ref, cb_ref, dw_ref, ut_ref, pad_ref):
    pad_ref[0:PAD_ROWS] = jnp.zeros((PAD_ROWS, LANE), F32)
    pad_ref[PAD_ROWS:] = val_ref[0].astype(F32) * jax.nn.sigmoid(gate_ref[0].astype(F32))
    ut_ref[0] = pad_ref[SEQ:SEQ + PAD_ROWS]
    cw = cw_ref[...]
    cb = cb_ref[...]
    first = PAD_ROWS - (CONV_WIDTH - 1)
    for c in range(SEQ // CONV_CHUNK):
        base = c * CONV_CHUNK
        acc = jnp.broadcast_to(cb, (CONV_CHUNK, LANE))
        for r in range(SUBLANE):
            taps = [w for w in range(CONV_WIDTH) if (first + w) % SUBLANE == r]
            span = max(first + w - r for w in taps) + CONV_CHUNK
            win = pad_ref[base + r:base + r + span]
            for w in taps:
                a = first + w - r
                acc = acc + win[a:a + CONV_CHUNK] * cw[w:w + 1]
        dw_ref[0, base:base + CONV_CHUNK] = acc


def _conv_prompt(hb_p, cw_pad, cb):
    h3 = hb_p.reshape(BATCH, SEQ, D_HB)
    nc = D_CONV // LANE
    return pl.pallas_call(
        _conv_p_kernel,
        grid=(BATCH, nc),
        in_specs=[pl.BlockSpec((1, SEQ, LANE), lambda b, c: (b, 0, C_GV // LANE + c)),
                  pl.BlockSpec((1, SEQ, LANE), lambda b, c: (b, 0, C_GG // LANE + c)),
                  pl.BlockSpec((PAD_ROWS, LANE), lambda b, c: (0, c)),
                  pl.BlockSpec((1, LANE), lambda b, c: (0, c))],
        out_specs=[pl.BlockSpec((1, SEQ, LANE), lambda b, c: (b, 0, c)),
                   pl.BlockSpec((1, PAD_ROWS, LANE), lambda b, c: (b, 0, c))],
        out_shape=[jax.ShapeDtypeStruct((BATCH, SEQ, D_CONV), F32),
                   jax.ShapeDtypeStruct((BATCH, PAD_ROWS, D_CONV), F32)],
        scratch_shapes=[pltpu.VMEM((PAD_ROWS + SEQ, LANE), F32)],
        compiler_params=_params(("parallel", "parallel")),
        name="conv_prompt",
    )(h3, h3, cw_pad, cb)


def _conv_s_kernel(val_ref, gate_ref, st_ref, cw_ref, cb_ref, dw_ref, ns_ref):
    R = DEC_SEQ
    H = CONV_WIDTH - 1
    cw = cw_ref[...]
    cb = jnp.broadcast_to(cb_ref[...], (DEC_BATCH, LANE))
    u = []
    for q in range(R):
        rows = pl.ds(q, DEC_BATCH, stride=R)
        u.append(val_ref[rows, :] * jax.nn.sigmoid(gate_ref[rows, :]))

    def row(r):
        return st_ref[r] if r < H else u[r - H]

    for q in range(R):
        acc = cb
        for w in range(CONV_WIDTH):
            acc = acc + row(q + w) * cw[w:w + 1]
        dw_ref[pl.ds(q, DEC_BATCH, stride=R), :] = acc
    for r in range(H):
        ns_ref[r] = row(r + R)


def _conv_sample(val, gate, state_t, cw_pad, cb):
    n_s = DEC_BATCH * DEC_SEQ
    H = CONV_WIDTH - 1
    nc = D_CONV // LANE
    return pl.pallas_call(
        _conv_s_kernel,
        grid=(nc,),
        in_specs=[pl.BlockSpec((n_s, LANE), lambda c: (0, c)),
                  pl.BlockSpec((n_s, LANE), lambda c: (0, c)),
                  pl.BlockSpec((H, DEC_BATCH, LANE), lambda c: (0, 0, c)),
                  pl.BlockSpec((PAD_ROWS, LANE), lambda c: (0, c)),
                  pl.BlockSpec((1, LANE), lambda c: (0, c))],
        out_specs=[pl.BlockSpec((n_s, LANE), lambda c: (0, c)),
                   pl.BlockSpec((H, DEC_BATCH, LANE), lambda c: (0, 0, c))],
        out_shape=[jax.ShapeDtypeStruct((n_s, D_CONV), F32),
                   jax.ShapeDtypeStruct((H, DEC_BATCH, D_CONV), F32)],
        compiler_params=_params(("parallel",)),
        name="conv_sample",
    )(val, gate, state_t, cw_pad, cb)


def _tail_kernel(attn_ref, za_ref, dw_ref, zc_ref, ga_ref, gc_ref, x_ref,
                 wua_ref, wpw_ref, wuc_ref, wo_ref, ng_ref, nb_ref, bpw_ref, fg_ref, y_ref):
    a = attn_ref[...] * jax.nn.silu(za_ref[...].astype(F32))
    branch_attn = jnp.dot(a.astype(BF16), wua_ref[...], preferred_element_type=F32)

    dw = dw_ref[...]
    mu = jnp.mean(dw, axis=-1, keepdims=True)
    var = jnp.mean(jnp.square(dw - mu), axis=-1, keepdims=True)
    ln = (dw - mu) * lax.rsqrt(var + EPS) * ng_ref[...] + nb_ref[...]
    conv_out = jnp.dot(jax.nn.silu(ln).astype(BF16), wpw_ref[...],
                       preferred_element_type=F32) + bpw_ref[...]
    c = conv_out * jax.nn.silu(zc_ref[...].astype(F32))
    branch_conv = jnp.dot(c.astype(BF16), wuc_ref[...], preferred_element_type=F32)

    merged = (jax.nn.sigmoid(ga_ref[...].astype(F32)) * branch_attn
              + jax.nn.sigmoid(gc_ref[...].astype(F32)) * branch_conv)
    y = x_ref[...] + jnp.dot(merged.astype(BF16), wo_ref[...], preferred_element_type=F32)
    ms = jnp.mean(y * y, axis=-1, keepdims=True)
    y_ref[...] = y * lax.rsqrt(ms + EPS) * fg_ref[...]


def _tail(h, attn, dw, x2d, wua, wpw, wuc, wo, ng, nb, bpw, fg, tm):
    n = x2d.shape[0]

    def const(shape):
        return pl.BlockSpec(shape, lambda i: (0, 0), pipeline_mode=pl.Buffered(1))

    return pl.pallas_call(
        _tail_kernel,
        grid=(n // tm,),
        in_specs=[pl.BlockSpec((tm, D_ATTN), lambda i: (i, 0)),
                  pl.BlockSpec((tm, D_ATTN), lambda i: (i, C_ZA // D_ATTN)),
                  pl.BlockSpec((tm, D_CONV), lambda i: (i, 0)),
                  pl.BlockSpec((tm, D_CONV), lambda i: (i, C_ZC // D_CONV)),
                  pl.BlockSpec((tm, D_MODEL), lambda i: (i, C_GA // D_MODEL)),
                  pl.BlockSpec((tm, D_MODEL), lambda i: (i, C_GC // D_MODEL)),
                  pl.BlockSpec((tm, D_MODEL), lambda i: (i, 0)),
                  const((D_ATTN, D_MODEL)), const((D_CONV, D_CONV)),
                  const((D_CONV, D_MODEL)), const((D_MODEL, D_MODEL)),
                  const((1, D_CONV)), const((1, D_CONV)), const((1, D_CONV)),
                  const((1, D_MODEL))],
        out_specs=pl.BlockSpec((tm, D_MODEL), lambda i: (i, 0)),
        out_shape=jax.ShapeDtypeStruct((n, D_MODEL), F32),
        compiler_params=_params(("parallel",)),
        name="tail",
    )(attn, h, dw, h, h, h, x2d, wua, wpw, wuc, wo, ng, nb, bpw, fg)


def _reorder_w_in(w_in):
    w_t = w_in.T

    def rows(name):
        o, w = _SRC[name]
        return w_t[o:o + w]

    pad = jnp.zeros((D_H - C_KI - IDX_DIM - IDX_HEADS, D_MODEL), w_in.dtype)
    w = jnp.concatenate([rows('gate_attn'), rows('gate_conv'), rows('q'), rows('z_attn'),
                         rows('glu_val'), rows('glu_gate'), rows('z_conv'), rows('k'), rows('v'),
                         rows('q_idx'), rows('k_idx'), rows('w_idx'), pad], axis=0)
    return w.astype(BF16)


def kernel(x_prompt, x_sample, cache_k, cache_v, cache_kidx, state_conv, page_table,
           ln_g, w_in, conv_w, conv_b, conv_norm_g, conv_norm_b, w_pw, b_pw,
           w_up_attn, w_up_conv, w_out, rel_bias, final_g):
    w_all = _reorder_w_in(w_in[0])
    g_in = ln_g[0].reshape(1, D_MODEL)
    xp = x_prompt.reshape(BATCH * SEQ, D_MODEL)
    xs = x_sample.reshape(DEC_BATCH * DEC_SEQ, D_MODEL)
    hb_p, hf_p, k_p, v_p = _proj(xp, g_in, w_all)
    hb_s, hf_s, k_s, v_s = _proj(xs, g_in, w_all)

    bias_p, bias_s = _bias_tables(rel_bias)

    attn_p = _attn_prompt(hb_p, hf_p, bias_p).reshape(BATCH * SEQ, D_ATTN)

    n_s = DEC_BATCH * DEC_SEQ
    scores = _score_sample(hf_s, page_table, cache_kidx[0].transpose(0, 2, 1))
    mask_t = _select_sample(scores.reshape(n_s, L_SAMPLE).T)
    mask = mask_t.T.reshape(DEC_BATCH, DEC_SEQ, L_SAMPLE)
    attn_s = _attn_sample(hb_s, hf_s, page_table, cache_k[0], cache_v[0], mask, bias_s)
    attn_s = attn_s.reshape(n_s, D_ATTN)

    cw_pad = jnp.concatenate([conv_w[0], jnp.zeros((PAD_ROWS - CONV_WIDTH, D_CONV), F32)], 0)
    cb = conv_b[0].reshape(1, D_CONV)
    dw_p, u_tail = _conv_prompt(hb_p, cw_pad, cb)
    dw_s, state_new = _conv_sample(hb_s[:, C_GV:C_GV + D_CONV].astype(F32),
                                   hb_s[:, C_GG:C_GG + D_CONV].astype(F32),
                                   state_conv[0].transpose(1, 0, 2), cw_pad, cb)

    wua = w_up_attn[0].astype(BF16)
    wpw = w_pw[0].astype(BF16)
    wuc = w_up_conv[0].astype(BF16)
    wo = w_out[0].astype(BF16)
    ng = conv_norm_g[0].reshape(1, D_CONV)
    nb = conv_norm_b[0].reshape(1, D_CONV)
    bpw = b_pw[0].reshape(1, D_CONV)
    fg = final_g.reshape(1, D_MODEL)
    y_p = _tail(hb_p, attn_p, dw_p.reshape(BATCH * SEQ, D_CONV), xp,
                wua, wpw, wuc, wo, ng, nb, bpw, fg, TAIL_TM)
    y_s = _tail(hb_s, attn_s, dw_s, xs,
                wua, wpw, wuc, wo, ng, nb, bpw, fg, TAIL_TM)

    tail_rows = CONV_WIDTH - 1
    return (
        y_p.reshape(BATCH, SEQ, D_MODEL),
        y_s.reshape(DEC_BATCH, DEC_SEQ, D_MODEL),
        k_p.reshape(1, BATCH, SEQ, N_KV_HEADS, HEAD_DIM),
        v_p.reshape(1, BATCH, SEQ, N_KV_HEADS, HEAD_DIM),
        hf_p[:, F_KI:F_KI + IDX_DIM].reshape(1, BATCH, SEQ, IDX_DIM),
        u_tail[:, PAD_ROWS - tail_rows:].reshape(1, BATCH, tail_rows, D_CONV),
        k_s.reshape(1, DEC_BATCH, DEC_SEQ, N_KV_HEADS, HEAD_DIM),
        v_s.reshape(1, DEC_BATCH, DEC_SEQ, N_KV_HEADS, HEAD_DIM),
        hf_s[:, F_KI:F_KI + IDX_DIM].reshape(1, DEC_BATCH, DEC_SEQ, IDX_DIM),
        state_new.transpose(1, 0, 2).reshape(1, DEC_BATCH, tail_rows, D_CONV),
    )
```

```python
import math

import numpy as np
import jax
import jax.numpy as jnp
from jax import lax
from jax.experimental import pallas as pl
from jax.experimental.pallas import tpu as pltpu

F32 = jnp.float32
BF16 = jnp.bfloat16

D_MODEL = 2048
BATCH = 8
SEQ = 2048
DEC_BATCH = 128
DEC_SEQ = 8
PAST_LEN = 2048
PAGE_SIZE = 128
N_PAGES = PAST_LEN // PAGE_SIZE
N_HEADS = 8
N_KV_HEADS = 2
HEAD_DIM = 128
GROUP = N_HEADS // N_KV_HEADS
D_ATTN = N_HEADS * HEAD_DIM
D_KV = N_KV_HEADS * HEAD_DIM
IDX_HEADS = 8
IDX_DIM = 64
TOPK = 256
Q_BLOCK = 128
N_BUCKETS = 32
MAX_DISTANCE = 128
D_CONV = D_MODEL // 2
CONV_WIDTH = 31
EPS = 1e-6
NEG = -1e30
NEG_INF = float("-inf")

LANE = 128
SUBLANE = 8
VMEM_LIMIT = 56 * 1024 * 1024

C_GA = 0
C_GC = C_GA + D_MODEL
C_Q = C_GC + D_MODEL
C_ZA = C_Q + D_ATTN
C_GV = C_ZA + D_ATTN
C_GG = C_GV + D_CONV
C_ZC = C_GG + D_CONV
C_K = C_ZC + D_CONV
C_V = C_K + D_KV
C_QI = C_V + D_KV
C_KI = C_QI + IDX_HEADS * IDX_DIM
W_OFF = IDX_DIM
MXU_WIDTH = 256
PROJ_TM = 1024
PROJ_TN = 6 * MXU_WIDTH
D_H = -(-(C_KI + LANE) // PROJ_TN) * PROJ_TN
KV_TILE = C_K // PROJ_TN
assert C_K % PROJ_TN == 0 and D_H == C_K + PROJ_TN
D_HB = C_K
D_HF = C_KI + LANE - C_K
F_K = C_K - D_HB
F_V = C_V - D_HB
F_QI = C_QI - D_HB
F_KI = C_KI - D_HB

_SRC = {}
_off = 0
for _name, _w in (('q', D_ATTN), ('k', D_KV), ('v', D_KV), ('z_attn', D_ATTN),
                  ('q_idx', IDX_HEADS * IDX_DIM), ('k_idx', IDX_DIM), ('w_idx', IDX_HEADS),
                  ('glu_val', D_CONV), ('glu_gate', D_CONV), ('z_conv', D_CONV),
                  ('gate_attn', D_MODEL), ('gate_conv', D_MODEL)):
    _SRC[_name] = (_off, _w)
    _off += _w

N_BISECT = 20
PAD_ROWS = 32
CONV_CHUNK = 64
SCORE_SCALE = (IDX_DIM ** -0.5) * (IDX_HEADS ** -0.5)
QK_SCALE = HEAD_DIM ** -0.5
LOG2E = math.log2(math.e)
L_SAMPLE = (N_PAGES + 1) * LANE
SUBTILES = LANE // SUBLANE
CHUNK_BLOCKS = 4
CHUNK = CHUNK_BLOCKS * Q_BLOCK
PAD_KEYS = CHUNK - Q_BLOCK
VT_ROWS = HEAD_DIM + 2 * SUBLANE
SCORE_ROWS = 4
TAIL_TM = 256


def _t5_bucket_static(dist):
    n = np.maximum(dist, 0)
    max_exact = N_BUCKETS // 2
    ratio = (np.log(np.maximum(n, 1).astype(np.float32) / np.float32(max_exact))
             / np.float32(math.log(MAX_DISTANCE / max_exact)))
    large = np.minimum(max_exact + (ratio * np.float32(N_BUCKETS - max_exact)).astype(np.int32),
                       N_BUCKETS - 1)
    return np.where(n < max_exact, n, large).astype(np.int32)


FAR_BUCKET = int(_t5_bucket_static(np.array([2 * MAX_DISTANCE]))[0])


def _params(sem):
    return pltpu.CompilerParams(dimension_semantics=sem, vmem_limit_bytes=VMEM_LIMIT)


def _proj_kernel(x_ref, g_ref, w_ref, ob_ref, of_ref, k_ref, v_ref, xn_ref):
    tm = x_ref.shape[0]
    j = pl.program_id(1)

    @pl.when(j == 0)
    def _():
        x = x_ref[...]
        ms = jnp.mean(x * x, axis=-1, keepdims=True)
        xn_ref[...] = (x * lax.rsqrt(ms + EPS) * g_ref[...]).astype(BF16)

    def project():
        return lax.dot_general(xn_ref[...], w_ref[...], (((1,), (1,)), ((), ())),
                               preferred_element_type=F32)

    @pl.when(j < KV_TILE)
    def _():
        ob_ref[...] = project().astype(BF16)

    @pl.when(j == KV_TILE)
    def _():
        of_ref[...] = project()[:, :D_HF]
        for g in range(N_KV_HEADS):
            rows = pl.ds(g, tm, stride=N_KV_HEADS)
            k_ref[rows, :] = of_ref[:, F_K + g * HEAD_DIM:F_K + (g + 1) * HEAD_DIM]
            v_ref[rows, :] = of_ref[:, F_V + g * HEAD_DIM:F_V + (g + 1) * HEAD_DIM]


def _proj(x2d, g, w):
    n = x2d.shape[0]
    tm, tn = PROJ_TM, PROJ_TN
    kv_spec = pl.BlockSpec((N_KV_HEADS * tm, HEAD_DIM), lambda i, j: (i, 0))
    kv_shape = jax.ShapeDtypeStruct((N_KV_HEADS * n, HEAD_DIM), F32)
    return pl.pallas_call(
        _proj_kernel,
        grid=(n // tm, D_H // tn),
        in_specs=[pl.BlockSpec((tm, D_MODEL), lambda i, j: (i, 0)),
                  pl.BlockSpec((1, D_MODEL), lambda i, j: (0, 0)),
                  pl.BlockSpec((tn, D_MODEL), lambda i, j: (j, 0))],
        out_specs=[pl.BlockSpec((tm, tn), lambda i, j: (i, jnp.minimum(j, KV_TILE - 1))),
                   pl.BlockSpec((tm, D_HF), lambda i, j: (i, 0)), kv_spec, kv_spec],
        out_shape=[jax.ShapeDtypeStruct((n, D_HB), BF16),
                   jax.ShapeDtypeStruct((n, D_HF), F32), kv_shape, kv_shape],
        scratch_shapes=[pltpu.VMEM((tm, D_MODEL), BF16)],
        compiler_params=_params(("parallel", "arbitrary")),
        name="proj",
    )(x2d, g, w)


def _bias_kernel(rb_ref, bp_ref, bs_ref, op_ref, os_ref):
    bp = bp_ref[...]
    bs = bs_ref[...]
    for h in range(N_HEADS):
        far = rb_ref[FAR_BUCKET, h]
        tp = jnp.zeros(bp.shape, F32)
        ts = jnp.zeros(bs.shape, F32)
        for b in range(N_BUCKETS):
            val = rb_ref[b, h] - far
            tp = jnp.where(bp == b, val * LOG2E, tp)
            ts = jnp.where(bs == b, val, ts)
        op_ref[h] = tp
        os_ref[h] = ts


def _bias_tables(rel_bias):
    key = np.arange(CHUNK)[:, None]
    qry = np.arange(Q_BLOCK)[None, :]
    bucket_p = _t5_bucket_static(CHUNK - Q_BLOCK + qry - key)
    qi = np.arange(DEC_SEQ)[:, None]
    col = np.arange(2 * LANE)[None, :]
    dist_s = np.where(col < LANE, LANE + qi - col, qi - (col - LANE))
    bucket_s = _t5_bucket_static(dist_s)
    return pl.pallas_call(
        _bias_kernel,
        in_specs=[pl.BlockSpec(memory_space=pltpu.SMEM),
                  pl.BlockSpec(memory_space=pltpu.VMEM),
                  pl.BlockSpec(memory_space=pltpu.VMEM)],
        out_specs=[pl.BlockSpec(memory_space=pltpu.VMEM),
                   pl.BlockSpec(memory_space=pltpu.VMEM)],
        out_shape=[jax.ShapeDtypeStruct((N_HEADS, CHUNK, Q_BLOCK), F32),
                   jax.ShapeDtypeStruct((N_HEADS, DEC_SEQ, 2 * LANE), F32)],
        name="bias_tables",
    )(rel_bias, jnp.asarray(bucket_p), jnp.asarray(bucket_s))


def _any(x):
    return jnp.max(jnp.where(x, 1.0, 0.0)) > 0.5


def _rep(x):
    return jnp.broadcast_to(x, (SUBLANE, LANE))


def _fold_rows(x, comb):
    parts = [x[k:k + SUBLANE] for k in range(0, x.shape[0], SUBLANE)]
    while len(parts) > 1:
        parts = [comb(parts[k], parts[k + 1]) for k in range(0, len(parts), 2)]
    return parts[0]


def _select_threshold(tile_fn, ntiles, static):
    def reduce_tiles(fn, init, comb):
        def step(j, acc):
            x = fn(tile_fn(j), j)
            parts = [x[k] for k in range(x.shape[0])]
            while len(parts) > 1:
                parts = [comb(parts[k], parts[k + 1]) for k in range(0, len(parts), 2)]
            return comb(acc, parts[0])
        if static:
            acc = init
            for j in range(ntiles):
                acc = step(j, acc)
            return acc
        return lax.fori_loop(0, ntiles, step, init)

    zeros = jnp.zeros((SUBLANE, LANE), F32)

    def count(pred_fn):
        acc = reduce_tiles(lambda s, j: jnp.where(pred_fn(s, j), 1.0, 0.0), zeros,
                           lambda a, b: a + b)
        return _rep(jnp.sum(acc, axis=0, keepdims=True))

    def masked_max(pred_fn):
        acc = reduce_tiles(lambda s, j: jnp.where(pred_fn(s, j), s, NEG_INF),
                           jnp.full((SUBLANE, LANE), NEG_INF, F32), jnp.maximum)
        return _rep(jnp.max(acc, axis=0, keepdims=True))

    bound = reduce_tiles(lambda s, j: jnp.where(s > NEG_INF, jnp.abs(s), 0.0), zeros,
                         jnp.maximum)
    bound = _rep(jnp.max(bound, axis=0, keepdims=True))

    def bisect(_, carry):
        lo, hi = carry
        mid = 0.5 * lo + 0.5 * hi
        few = count(lambda s, j: s > mid[None]) < TOPK
        return jnp.where(few, lo, mid), jnp.where(few, mid, hi)

    _, hi = lax.fori_loop(0, N_BISECT, bisect, (-bound, bound))

    thr = masked_max(lambda s, j: s <= hi[None])
    n_ge = count(lambda s, j: s >= thr[None])

    def fix_body(carry):
        thr, n_ge, _ = carry
        lower = masked_max(lambda s, j: s < thr[None])
        thr = jnp.where(n_ge < TOPK, lower, thr)
        n_ge = count(lambda s, j: s >= thr[None])
        return thr, n_ge, _any(n_ge < TOPK)

    thr, n_ge, _ = lax.while_loop(lambda c: c[2], fix_body, (thr, n_ge, _any(n_ge < TOPK)))
    n_gt = count(lambda s, j: s > thr[None])
    need = TOPK - n_gt
    return thr[0:1], need[0:1]


def _keep(pred):
    return jnp.where(pred, 0.0, NEG)


def _selection_masks(tiles, thr, need, seen, tri):
    ties = [s == thr for s in tiles]
    ranks = []
    for k in range(0, len(tiles), 2):
        pair = jnp.concatenate([jnp.where(t, 1.0, 0.0).astype(BF16) for t in ties[k:k + 2]], axis=1)
        rank = jnp.dot(tri, pair, preferred_element_type=F32)
        ranks += [rank[:, p * LANE:(p + 1) * LANE] for p in range(len(ties[k:k + 2]))]
    masks = []
    for s, tie, rank in zip(tiles, ties, ranks):
        masks.append(jnp.where(tie, _keep(rank + seen <= need), _keep(s > thr)))
        seen = seen + rank[s.shape[0] - 1:]
    return masks, seen


def _attn_p_kernel(q_ref, qi_ref, wi_ref, k_ref, v_ref, ki_ref, bias_ref, tri_ref, o_ref,
                   kb_ref, vt_ref, kib_ref, qh_ref, qih_ref,
                   score_ref, mask_ref, s_ref, m_ref, acc_ref):
    i = pl.program_id(1)
    T = Q_BLOCK
    W = GROUP * T
    nch = i // CHUNK_BLOCKS + 1

    @pl.when(i == 0)
    def _():
        kb_ref[0:PAD_KEYS] = jnp.zeros((PAD_KEYS, D_KV), BF16)
        kb_ref[PAD_KEYS:] = k_ref[0].astype(BF16)
        kib_ref[0:PAD_KEYS] = jnp.zeros((PAD_KEYS, IDX_DIM), BF16)
        kib_ref[PAD_KEYS:] = ki_ref[0][:, :IDX_DIM].astype(BF16)
        ones_row = lax.broadcasted_iota(jnp.int32, (VT_ROWS - HEAD_DIM, PAD_KEYS + SEQ), 0) == 0
        for g in range(N_KV_HEADS):
            vt_ref[g, 0:HEAD_DIM, 0:PAD_KEYS] = jnp.zeros((HEAD_DIM, PAD_KEYS), BF16)
            vt_ref[g, HEAD_DIM:VT_ROWS, :] = jnp.where(ones_row, 1.0, 0.0).astype(BF16)
            for c in range(SEQ // LANE):
                blk = v_ref[0, c * LANE:(c + 1) * LANE, g * HEAD_DIM:(g + 1) * HEAD_DIM]
                vt_ref[g, 0:HEAD_DIM,
                       PAD_KEYS + c * LANE:PAD_KEYS + (c + 1) * LANE] = blk.T.astype(BF16)
        score_ref[0:PAD_KEYS] = jnp.full((PAD_KEYS, T), NEG_INF, F32)
        mask_ref[0:PAD_KEYS] = jnp.full((PAD_KEYS, T), NEG, F32)

    q = q_ref[0].astype(F32) * (QK_SCALE * LOG2E)
    for h in range(N_HEADS):
        qh_ref[h] = q[:, h * HEAD_DIM:(h + 1) * HEAD_DIM].astype(BF16)
    qi = qi_ref[0]
    for h in range(IDX_HEADS):
        qih_ref[h] = qi[:, h * IDX_DIM:(h + 1) * IDX_DIM].astype(BF16)
    w_rows = wi_ref[0].T[W_OFF:W_OFF + IDX_HEADS] * SCORE_SCALE

    def span(c):
        return pl.ds(pl.multiple_of((i - CHUNK_BLOCKS * c) * LANE, LANE), CHUNK)

    def first_key(c):
        return (i - CHUNK_BLOCKS * c) * LANE - PAD_KEYS

    key_l = lax.broadcasted_iota(jnp.int32, (CHUNK, T), 0)
    qry = i * T + lax.broadcasted_iota(jnp.int32, (CHUNK, T), 1)

    def score_body(c, _):
        kc = kib_ref[span(c), :]
        d = lax.dot_general(kc, qih_ref[...].reshape(IDX_HEADS * T, IDX_DIM),
                            (((1,), (1,)), ((), ())), preferred_element_type=F32)
        acc = jnp.zeros((CHUNK, T), F32)
        for h in range(IDX_HEADS):
            acc = acc + jnp.maximum(d[:, h * T:(h + 1) * T], 0.0) * w_rows[h:h + 1]
        key = first_key(c) + key_l
        acc = jnp.where(key <= qry, jnp.where(key >= 0, acc, NEG_INF), NEG_INF)
        score_ref[span(c), :] = acc
        return 0

    lax.fori_loop(0, nch, score_body, 0)

    nsub = CHUNK // SUBLANE

    def tile_fn(c):
        return score_ref[span(c), :].reshape(nsub, SUBLANE, T)

    @pl.when(i * T + T <= TOPK)
    def _():
        def body(c, _):
            mask_ref[span(c), :] = _keep(score_ref[span(c), :] > NEG_INF)
            return 0
        lax.fori_loop(0, nch, body, 0)

    @pl.when(i * T + T > TOPK)
    def _():
        thr, need = _select_threshold(tile_fn, nch, static=False)

        def body(k, seen):
            c = nch - 1 - k
            start = pl.multiple_of((i - CHUNK_BLOCKS * c) * LANE, LANE)
            parts = [pl.ds(start + p * MXU_WIDTH, MXU_WIDTH) for p in range(CHUNK // MXU_WIDTH)]
            masks, seen = _selection_masks([score_ref[rows, :] for rows in parts],
                                           thr, need, seen, tri_ref[...])
            for rows, mask in zip(parts, masks):
                mask_ref[rows, :] = mask
            return seen

        lax.fori_loop(0, nch, body, jnp.zeros((1, T), F32))

    m_ref[...] = jnp.full(m_ref.shape, NEG, F32)
    acc_ref[...] = jnp.zeros(acc_ref.shape, F32)

    def logits(c, with_bias):
        mb = mask_ref[span(c), :]
        for g in range(N_KV_HEADS):
            kc = kb_ref[span(c), g * HEAD_DIM:(g + 1) * HEAD_DIM]
            qg = qh_ref[g * GROUP:(g + 1) * GROUP].reshape(W, HEAD_DIM)
            s = lax.dot_general(kc, qg, (((1,), (1,)), ((), ())), preferred_element_type=F32)
            if with_bias:
                add = jnp.concatenate([mb + bias_ref[g * GROUP + hq] for hq in range(GROUP)],
                                      axis=1)
            else:
                add = jnp.concatenate([mb] * GROUP, axis=1)
            s = s + add
            s_ref[span(c), g * W:(g + 1) * W] = s
            top = jnp.max(_fold_rows(s, jnp.maximum), axis=0, keepdims=True)
            m_ref[g] = jnp.maximum(m_ref[g], top)

    logits(0, True)

    def logits_body(c, _):
        logits(c, False)
        return 0

    lax.fori_loop(1, nch, logits_body, 0)

    def weigh(c, _):
        ps = [jnp.exp2((s_ref[span(c), g * W:(g + 1) * W] - m_ref[g]).astype(BF16))
              for g in range(N_KV_HEADS)]
        pv = [jnp.dot(vt_ref[g, :, span(c)], ps[g], preferred_element_type=F32)
              for g in range(N_KV_HEADS)]
        for g in range(N_KV_HEADS):
            acc_ref[g] += pv[g]
        return 0

    lax.fori_loop(0, nch, weigh, 0)

    for g in range(N_KV_HEADS):
        o = acc_ref[g, 0:HEAD_DIM] / acc_ref[g, HEAD_DIM:HEAD_DIM + 1]
        for hq in range(GROUP):
            h = g * GROUP + hq
            o_ref[0, :, h * HEAD_DIM:(h + 1) * HEAD_DIM] = o[:, hq * T:(hq + 1) * T].T


def _attn_prompt(hb_p, hf_p, bias_p):
    hb3 = hb_p.reshape(BATCH, SEQ, D_HB)
    hf3 = hf_p.reshape(BATCH, SEQ, D_HF)
    nqb = SEQ // Q_BLOCK
    T = Q_BLOCK
    qi_w = IDX_HEADS * IDX_DIM
    return pl.pallas_call(
        _attn_p_kernel,
        grid=(BATCH, nqb),
        in_specs=[
            pl.BlockSpec((1, T, D_ATTN), lambda b, i: (b, i, C_Q // D_ATTN)),
            pl.BlockSpec((1, T, qi_w), lambda b, i: (b, i, F_QI // qi_w)),
            pl.BlockSpec((1, T, LANE), lambda b, i: (b, i, F_KI // LANE)),
            pl.BlockSpec((1, SEQ, D_KV), lambda b, i: (b, 0, F_K // D_KV)),
            pl.BlockSpec((1, SEQ, D_KV), lambda b, i: (b, 0, F_V // D_KV)),
            pl.BlockSpec((1, SEQ, LANE), lambda b, i: (b, 0, F_KI // LANE)),
            pl.BlockSpec((N_HEADS, CHUNK, T), lambda b, i: (0, 0, 0)),
            pl.BlockSpec((MXU_WIDTH, MXU_WIDTH), lambda b, i: (0, 0)),
        ],
        out_specs=pl.BlockSpec((1, T, D_ATTN), lambda b, i: (b, i, 0)),
        out_shape=jax.ShapeDtypeStruct((BATCH, SEQ, D_ATTN), F32),
        scratch_shapes=[
            pltpu.VMEM((PAD_KEYS + SEQ, D_KV), BF16),
            pltpu.VMEM((N_KV_HEADS, VT_ROWS, PAD_KEYS + SEQ), BF16),
            pltpu.VMEM((PAD_KEYS + SEQ, IDX_DIM), BF16),
            pltpu.VMEM((N_HEADS, T, HEAD_DIM), BF16),
            pltpu.VMEM((IDX_HEADS, T, IDX_DIM), BF16),
            pltpu.VMEM((PAD_KEYS + SEQ, T), F32),
            pltpu.VMEM((PAD_KEYS + SEQ, T), F32),
            pltpu.VMEM((PAD_KEYS + SEQ, N_HEADS * T), F32),
            pltpu.VMEM((N_KV_HEADS, 1, GROUP * T), F32),
            pltpu.VMEM((N_KV_HEADS, VT_ROWS, GROUP * T), F32),
        ],
        compiler_params=_params(("parallel", "arbitrary")),
        name="attn_prompt",
    )(hb3, hf3, hf3, hf3, hf3, hf3, bias_p, jnp.tri(MXU_WIDTH, dtype=BF16))


def _score_s_kernel(pt_ref, qi_ref, w_ref, kin_ref, *rest):
    npg = SCORE_ROWS * N_PAGES
    kip = rest[0:npg]
    o_ref = rest[npg]
    kinp_ref = rest[npg + 1]
    del pt_ref
    R = DEC_SEQ
    qrow = lax.broadcasted_iota(jnp.int32, (R, LANE), 0)
    lane = lax.broadcasted_iota(jnp.int32, (R, LANE), 1)
    kinp_ref[...] = jnp.zeros(kinp_ref.shape, BF16)
    for r in range(SCORE_ROWS):
        kinp_ref[r, 0:2 * R] = jnp.concatenate(
            [kin_ref[r][:, :IDX_DIM], jnp.zeros((R, IDX_DIM), F32)], 0).astype(BF16)
    for r in range(SCORE_ROWS):
        qi = qi_ref[r].astype(BF16)
        wb = jnp.broadcast_to(w_ref[r] * SCORE_SCALE, (IDX_HEADS * R, LANE))
        for t in range(N_PAGES + 1):
            if t < N_PAGES:
                d = jnp.dot(qi, kip[r * N_PAGES + t][0].astype(BF16), preferred_element_type=F32)
            else:
                d = lax.dot_general(qi, kinp_ref[r], (((1,), (1,)), ((), ())),
                                    preferred_element_type=F32)
            e = (jnp.maximum(d, 0.0) * wb).reshape(IDX_HEADS, R, LANE)
            s = e[0]
            for h in range(1, IDX_HEADS):
                s = s + e[h]
            if t == N_PAGES:
                s = jnp.where(lane <= qrow, s, NEG_INF)
            o_ref[r, :, t * LANE:(t + 1) * LANE] = s


def _score_sample(hf_s, page_table, cache_kidx_t):
    R = DEC_SEQ
    G = SCORE_ROWS
    h3 = hf_s.reshape(DEC_BATCH, R, D_HF)
    qi_hq = h3[:, :, F_QI:F_QI + IDX_HEADS * IDX_DIM].reshape(DEC_BATCH, R, IDX_HEADS, IDX_DIM)
    qi_hq = qi_hq.transpose(0, 2, 1, 3).reshape(DEC_BATCH, IDX_HEADS * R, IDX_DIM)
    w_hq = h3[:, :, F_KI + W_OFF:F_KI + W_OFF + IDX_HEADS].transpose(0, 2, 1)
    w_hq = w_hq.reshape(DEC_BATCH, IDX_HEADS * R, 1)
    in_specs = [
        pl.BlockSpec((G, IDX_HEADS * R, IDX_DIM), lambda b, pt: (b, 0, 0)),
        pl.BlockSpec((G, IDX_HEADS * R, 1), lambda b, pt: (b, 0, 0)),
        pl.BlockSpec((G, R, LANE), lambda b, pt: (b, 0, F_KI // LANE)),
    ]
    in_specs += [pl.BlockSpec((1, IDX_DIM, PAGE_SIZE),
                              lambda b, pt, r=r, p=p: (pt[b * G + r, p], 0, 0))
                 for r in range(G) for p in range(N_PAGES)]
    grid_spec = pltpu.PrefetchScalarGridSpec(
        num_scalar_prefetch=1,
        grid=(DEC_BATCH // G,),
        in_specs=in_specs,
        out_specs=pl.BlockSpec((G, R, L_SAMPLE), lambda b, pt: (b, 0, 0)),
        scratch_shapes=[pltpu.VMEM((G, PAGE_SIZE, IDX_DIM), BF16)],
    )
    return pl.pallas_call(
        _score_s_kernel,
        grid_spec=grid_spec,
        out_shape=jax.ShapeDtypeStruct((DEC_BATCH, R, L_SAMPLE), F32),
        compiler_params=_params(("arbitrary",)),
        name="score_sample",
    )(page_table, qi_hq, w_hq, h3, *([cache_kidx_t] * (G * N_PAGES)))


def _select_s_kernel(s_ref, tri_ref, o_ref):
    nt = N_PAGES + 1

    def tile_fn(j):
        return s_ref[j * LANE:(j + 1) * LANE, :].reshape(SUBTILES, SUBLANE, LANE)

    thr, need = _select_threshold(tile_fn, nt, static=True)
    parts = [slice(j * LANE, (j + 1) * LANE) for j in range(nt)]
    masks, _ = _selection_masks([s_ref[rows, :] for rows in parts], thr, need,
                                jnp.zeros((1, LANE), F32), tri_ref[...])
    for rows, mask in zip(parts, masks):
        o_ref[rows, :] = mask


def _select_sample(scores_t):
    n = scores_t.shape[1]
    return pl.pallas_call(
        _select_s_kernel,
        grid=(n // LANE,),
        in_specs=[pl.BlockSpec((L_SAMPLE, LANE), lambda c: (0, c)),
                  pl.BlockSpec((LANE, LANE), lambda c: (0, 0))],
        out_specs=pl.BlockSpec((L_SAMPLE, LANE), lambda c: (0, c)),
        out_shape=jax.ShapeDtypeStruct((L_SAMPLE, n), F32),
        compiler_params=_params(("parallel",)),
        name="select_sample",
    )(scores_t, jnp.tri(LANE, dtype=BF16))


def _attn_s_kernel(pt_ref, q_ref, kn_ref, vn_ref, mask_ref, bias_ref, *rest):
    kp = rest[0:N_PAGES]
    vp = rest[N_PAGES:2 * N_PAGES]
    o_ref = rest[2 * N_PAGES]
    knp_ref, vnp_ref, logit_ref = rest[2 * N_PAGES + 1:]
    del pt_ref
    R = DEC_SEQ
    NT = N_PAGES + 1
    GR = GROUP * R

    knp_ref[...] = jnp.zeros(knp_ref.shape, BF16)
    vnp_ref[...] = jnp.zeros(vnp_ref.shape, BF16)
    knp_ref[0:2 * R] = jnp.concatenate([kn_ref[0], jnp.zeros((R, D_KV), F32)], 0).astype(BF16)
    vnp_ref[0:2 * R] = jnp.concatenate([vn_ref[0], jnp.zeros((R, D_KV), F32)], 0).astype(BF16)

    def page_head(refs, t, g):
        if t < N_PAGES:
            return refs[t][pl.ds(g, PAGE_SIZE, stride=N_KV_HEADS), :].astype(BF16)
        pad = knp_ref if refs is kp else vnp_ref
        return pad[:, g * HEAD_DIM:(g + 1) * HEAD_DIM]

    q = (q_ref[0].astype(F32) * QK_SCALE).astype(BF16)
    for t in range(NT):
        mb = mask_ref[0, :, t * LANE:(t + 1) * LANE]
        for g in range(N_KV_HEADS):
            lg = lax.dot_general(q[g * GR:(g + 1) * GR], page_head(kp, t, g),
                                 (((1,), (1,)), ((), ())), preferred_element_type=F32)
            lg = lg.reshape(GROUP, R, LANE) + mb[None]
            if t >= N_PAGES - 1:
                off = (t - (N_PAGES - 1)) * LANE
                lg = lg + bias_ref[g * GROUP:(g + 1) * GROUP, :, off:off + LANE]
            logit_ref[g * GR:(g + 1) * GR, t * LANE:(t + 1) * LANE] = lg.reshape(GR, LANE)

    logits = logit_ref[...]
    m = jnp.max(logits, axis=1, keepdims=True)
    p = jnp.exp(logits - m)
    inv = 1.0 / jnp.sum(p, axis=1, keepdims=True)
    pb = p.astype(BF16)
    outs = [jnp.zeros((GR, HEAD_DIM), F32) for _ in range(N_KV_HEADS)]
    for t in range(NT):
        for g in range(N_KV_HEADS):
            outs[g] = outs[g] + jnp.dot(pb[g * GR:(g + 1) * GR, t * LANE:(t + 1) * LANE],
                                        page_head(vp, t, g), preferred_element_type=F32)
    for g in range(N_KV_HEADS):
        o = outs[g] * inv[g * GR:(g + 1) * GR]
        for hq in range(GROUP):
            h = g * GROUP + hq
            o_ref[0, :, h * HEAD_DIM:(h + 1) * HEAD_DIM] = o[hq * R:(hq + 1) * R]


def _attn_sample(hb_s, hf_s, page_table, cache_k, cache_v, mask, bias_s):
    R = DEC_SEQ
    h3 = hf_s.reshape(DEC_BATCH, R, D_HF)
    q_hq = hb_s[:, C_Q:C_Q + D_ATTN].reshape(DEC_BATCH, R, N_HEADS, HEAD_DIM)
    q_hq = q_hq.transpose(0, 2, 1, 3).reshape(DEC_BATCH, N_HEADS * R, HEAD_DIM)
    rows_per_page = PAGE_SIZE * N_KV_HEADS
    ck = cache_k.reshape(-1, HEAD_DIM)
    cv = cache_v.reshape(-1, HEAD_DIM)

    in_specs = [
        pl.BlockSpec((1, N_HEADS * R, HEAD_DIM), lambda b, pt: (b, 0, 0)),
        pl.BlockSpec((1, R, D_KV), lambda b, pt: (b, 0, F_K // D_KV)),
        pl.BlockSpec((1, R, D_KV), lambda b, pt: (b, 0, F_V // D_KV)),
        pl.BlockSpec((1, R, L_SAMPLE), lambda b, pt: (b, 0, 0)),
        pl.BlockSpec((N_HEADS, R, 2 * LANE), lambda b, pt: (0, 0, 0)),
    ]
    page = [pl.BlockSpec((rows_per_page, HEAD_DIM), lambda b, pt, p=p: (pt[b, p], 0))
            for p in range(N_PAGES)]
    in_specs += page + page
    grid_spec = pltpu.PrefetchScalarGridSpec(
        num_scalar_prefetch=1,
        grid=(DEC_BATCH,),
        in_specs=in_specs,
        out_specs=pl.BlockSpec((1, R, D_ATTN), lambda b, pt: (b, 0, 0)),
        scratch_shapes=[
            pltpu.VMEM((PAGE_SIZE, D_KV), BF16),
            pltpu.VMEM((PAGE_SIZE, D_KV), BF16),
            pltpu.VMEM((N_HEADS * R, L_SAMPLE), F32),
        ],
    )
    return pl.pallas_call(
        _attn_s_kernel,
        grid_spec=grid_spec,
        out_shape=jax.ShapeDtypeStruct((DEC_BATCH, R, D_ATTN), F32),
        compiler_params=_params(("arbitrary",)),
        name="attn_sample",
    )(page_table, q_hq, h3, h3, mask, bias_s, *([ck] * N_PAGES), *([cv] * N_PAGES))


def _conv_p_kernel(val_ref, gate_ref, cw_ref, cb_ref, dw_ref, ut_ref, pad_ref):
    pad_ref[0:PAD_ROWS] = jnp.zeros((PAD_ROWS, LANE), F32)
    pad_ref[PAD_ROWS:] = val_ref[0].astype(F32) * jax.nn.sigmoid(gate_ref[0].astype(F32))
    ut_ref[0] = pad_ref[SEQ:SEQ + PAD_ROWS]
    cw = cw_ref[...]
    cb = cb_ref[...]
    first = PAD_ROWS - (CONV_WIDTH - 1)
    for c in range(SEQ // CONV_CHUNK):
        base = c * CONV_CHUNK
        acc = jnp.broadcast_to(cb, (CONV_CHUNK, LANE))
        for r in range(SUBLANE):
            taps = [w for w in range(CONV_WIDTH) if (first + w) % SUBLANE == r]
            span = max(first + w - r for w in taps) + CONV_CHUNK
            win = pad_ref[base + r:base + r + span]
            for w in taps:
                a = first + w - r
                acc = acc + win[a:a + CONV_CHUNK] * cw[w:w + 1]
        dw_ref[0, base:base + CONV_CHUNK] = acc


def _conv_prompt(hb_p, cw_pad, cb):
    h3 = hb_p.reshape(BATCH, SEQ, D_HB)
    nc = D_CONV // LANE
    return pl.pallas_call(
        _conv_p_kernel,
        grid=(BATCH, nc),
        in_specs=[pl.BlockSpec((1, SEQ, LANE), lambda b, c: (b, 0, C_GV // LANE + c)),
                  pl.BlockSpec((1, SEQ, LANE), lambda b, c: (b, 0, C_GG // LANE + c)),
                  pl.BlockSpec((PAD_ROWS, LANE), lambda b, c: (0, c)),
                  pl.BlockSpec((1, LANE), lambda b, c: (0, c))],
        out_specs=[pl.BlockSpec((1, SEQ, LANE), lambda b, c: (b, 0, c)),
                   pl.BlockSpec((1, PAD_ROWS, LANE), lambda b, c: (b, 0, c))],
        out_shape=[jax.ShapeDtypeStruct((BATCH, SEQ, D_CONV), F32),
                   jax.ShapeDtypeStruct((BATCH, PAD_ROWS, D_CONV), F32)],
        scratch_shapes=[pltpu.VMEM((PAD_ROWS + SEQ, LANE), F32)],
        compiler_params=_params(("parallel", "parallel")),
        name="conv_prompt",
    )(h3, h3, cw_pad, cb)


def _conv_s_kernel(val_ref, gate_ref, st_ref, cw_ref, cb_ref, dw_ref, ns_ref):
    R = DEC_SEQ
    H = CONV_WIDTH - 1
    cw = cw_ref[...]
    cb = jnp.broadcast_to(cb_ref[...], (DEC_BATCH, LANE))
    u = []
    for q in range(R):
        rows = pl.ds(q, DEC_BATCH, stride=R)
        u.append(val_ref[rows, :] * jax.nn.sigmoid(gate_ref[rows, :]))

    def row(r):
        return st_ref[r] if r < H else u[r - H]

    for q in range(R):
        acc = cb
        for w in range(CONV_WIDTH):
            acc = acc + row(q + w) * cw[w:w + 1]
        dw_ref[pl.ds(q, DEC_BATCH, stride=R), :] = acc
    for r in range(H):
        ns_ref[r] = row(r + R)


def _conv_sample(val, gate, state_t, cw_pad, cb):
    n_s = DEC_BATCH * DEC_SEQ
    H = CONV_WIDTH - 1
    nc = D_CONV // LANE
    return pl.pallas_call(
        _conv_s_kernel,
        grid=(nc,),
        in_specs=[pl.BlockSpec((n_s, LANE), lambda c: (0, c)),
                  pl.BlockSpec((n_s, LANE), lambda c: (0, c)),
                  pl.BlockSpec((H, DEC_BATCH, LANE), lambda c: (0, 0, c)),
                  pl.BlockSpec((PAD_ROWS, LANE), lambda c: (0, c)),
                  pl.BlockSpec((1, LANE), lambda c: (0, c))],
        out_specs=[pl.BlockSpec((n_s, LANE), lambda c: (0, c)),
                   pl.BlockSpec((H, DEC_BATCH, LANE), lambda c: (0, 0, c))],
        out_shape=[jax.ShapeDtypeStruct((n_s, D_CONV), F32),
                   jax.ShapeDtypeStruct((H, DEC_BATCH, D_CONV), F32)],
        compiler_params=_params(("parallel",)),
        name="conv_sample",
    )(val, gate, state_t, cw_pad, cb)


def _tail_kernel(attn_ref, za_ref, dw_ref, zc_ref, ga_ref, gc_ref, x_ref,
                 wua_ref, wpw_ref, wuc_ref, wo_ref, ng_ref, nb_ref, bpw_ref, fg_ref, y_ref):
    a = attn_ref[...] * jax.nn.silu(za_ref[...].astype(F32))
    branch_attn = jnp.dot(a.astype(BF16), wua_ref[...], preferred_element_type=F32)

    dw = dw_ref[...]
    mu = jnp.mean(dw, axis=-1, keepdims=True)
    var = jnp.mean(jnp.square(dw - mu), axis=-1, keepdims=True)
    ln = (dw - mu) * lax.rsqrt(var + EPS) * ng_ref[...] + nb_ref[...]
    conv_out = jnp.dot(jax.nn.silu(ln).astype(BF16), wpw_ref[...],
                       preferred_element_type=F32) + bpw_ref[...]
    c = conv_out * jax.nn.silu(zc_ref[...].astype(F32))
    branch_conv = jnp.dot(c.astype(BF16), wuc_ref[...], preferred_element_type=F32)

    merged = (jax.nn.sigmoid(ga_ref[...].astype(F32)) * branch_attn
              + jax.nn.sigmoid(gc_ref[...].astype(F32)) * branch_conv)
    y = x_ref[...] + jnp.dot(merged.astype(BF16), wo_ref[...], preferred_element_type=F32)
    ms = jnp.mean(y * y, axis=-1, keepdims=True)
    y_ref[...] = y * lax.rsqrt(ms + EPS) * fg_ref[...]


def _tail(h, attn, dw, x2d, weights, tm):
    n = x2d.shape[0]

    def const(shape):
        return pl.BlockSpec(shape, lambda i: (0, 0), pipeline_mode=pl.Buffered(1))

    def cols(width, offset):
        return pl.BlockSpec((tm, width), lambda i: (i, offset // width))

    return pl.pallas_call(
        _tail_kernel,
        grid=(n // tm,),
        in_specs=[cols(D_ATTN, 0), cols(D_ATTN, C_ZA), cols(D_CONV, 0), cols(D_CONV, C_ZC),
                  cols(D_MODEL, C_GA), cols(D_MODEL, C_GC), cols(D_MODEL, 0),
                  const((D_ATTN, D_MODEL)), const((D_CONV, D_CONV)),
                  const((D_CONV, D_MODEL)), const((D_MODEL, D_MODEL)),
                  const((1, D_CONV)), const((1, D_CONV)), const((1, D_CONV)),
                  const((1, D_MODEL))],
        out_specs=pl.BlockSpec((tm, D_MODEL), lambda i: (i, 0)),
        out_shape=jax.ShapeDtypeStruct((n, D_MODEL), F32),
        compiler_params=_params(("parallel",)),
        name="tail",
    )(attn, h, dw, h, h, h, x2d, *weights)


def _reorder_w_in(w_in):
    w_t = w_in.T

    def rows(name):
        o, w = _SRC[name]
        return w_t[o:o + w]

    pad = jnp.zeros((D_H - C_KI - IDX_DIM - IDX_HEADS, D_MODEL), w_in.dtype)
    w = jnp.concatenate([rows('gate_attn'), rows('gate_conv'), rows('q'), rows('z_attn'),
                         rows('glu_val'), rows('glu_gate'), rows('z_conv'), rows('k'), rows('v'),
                         rows('q_idx'), rows('k_idx'), rows('w_idx'), pad], axis=0)
    return w.astype(BF16)


def kernel(x_prompt, x_sample, cache_k, cache_v, cache_kidx, state_conv, page_table,
           ln_g, w_in, conv_w, conv_b, conv_norm_g, conv_norm_b, w_pw, b_pw,
           w_up_attn, w_up_conv, w_out, rel_bias, final_g):
    w_all = _reorder_w_in(w_in[0])
    g_in = ln_g[0].reshape(1, D_MODEL)
    xp = x_prompt.reshape(BATCH * SEQ, D_MODEL)
    xs = x_sample.reshape(DEC_BATCH * DEC_SEQ, D_MODEL)
    hb_p, hf_p, k_p, v_p = _proj(xp, g_in, w_all)
    hb_s, hf_s, k_s, v_s = _proj(xs, g_in, w_all)

    bias_p, bias_s = _bias_tables(rel_bias)

    attn_p = _attn_prompt(hb_p, hf_p, bias_p).reshape(BATCH * SEQ, D_ATTN)

    n_s = DEC_BATCH * DEC_SEQ
    scores = _score_sample(hf_s, page_table, cache_kidx[0].transpose(0, 2, 1))
    mask_t = _select_sample(scores.reshape(n_s, L_SAMPLE).T)
    mask = mask_t.T.reshape(DEC_BATCH, DEC_SEQ, L_SAMPLE)
    attn_s = _attn_sample(hb_s, hf_s, page_table, cache_k[0], cache_v[0], mask, bias_s)
    attn_s = attn_s.reshape(n_s, D_ATTN)

    cw_pad = jnp.concatenate([conv_w[0], jnp.zeros((PAD_ROWS - CONV_WIDTH, D_CONV), F32)], 0)
    cb = conv_b[0].reshape(1, D_CONV)
    dw_p, u_tail = _conv_prompt(hb_p, cw_pad, cb)
    dw_s, state_new = _conv_sample(hb_s[:, C_GV:C_GV + D_CONV].astype(F32),
                                   hb_s[:, C_GG:C_GG + D_CONV].astype(F32),
                                   state_conv[0].transpose(1, 0, 2), cw_pad, cb)

    wua = w_up_attn[0].astype(BF16)
    wpw = w_pw[0].astype(BF16)
    wuc = w_up_conv[0].astype(BF16)
    wo = w_out[0].astype(BF16)
    ng = conv_norm_g[0].reshape(1, D_CONV)
    nb = conv_norm_b[0].reshape(1, D_CONV)
    bpw = b_pw[0].reshape(1, D_CONV)
    fg = final_g.reshape(1, D_MODEL)
    weights = (wua, wpw, wuc, wo, ng, nb, bpw, fg)
    y_p = _tail(hb_p, attn_p, dw_p.reshape(BATCH * SEQ, D_CONV), xp, weights, TAIL_TM)
    y_s = _tail(hb_s, attn_s, dw_s, xs, weights, TAIL_TM)

    tail_rows = CONV_WIDTH - 1
    return (
        y_p.reshape(BATCH, SEQ, D_MODEL),
        y_s.reshape(DEC_BATCH, DEC_SEQ, D_MODEL),
        k_p.reshape(1, BATCH, SEQ, N_KV_HEADS, HEAD_DIM),
        v_p.reshape(1, BATCH, SEQ, N_KV_HEADS, HEAD_DIM),
        hf_p[:, F_KI:F_KI + IDX_DIM].reshape(1, BATCH, SEQ, IDX_DIM),
        u_tail[:, PAD_ROWS - tail_rows:].reshape(1, BATCH, tail_rows, D_CONV),
        k_s.reshape(1, DEC_BATCH, DEC_SEQ, N_KV_HEADS, HEAD_DIM),
        v_s.reshape(1, DEC_BATCH, DEC_SEQ, N_KV_HEADS, HEAD_DIM),
        hf_s[:, F_KI:F_KI + IDX_DIM].reshape(1, DEC_BATCH, DEC_SEQ, IDX_DIM),
        state_new.transpose(1, 0, 2).reshape(1, DEC_BATCH, tail_rows, D_CONV),
    )
```

```python
import math

import numpy as np
import jax
import jax.numpy as jnp
from jax import lax
from jax.experimental import pallas as pl
from jax.experimental.pallas import tpu as pltpu

F32 = jnp.float32
BF16 = jnp.bfloat16

D_MODEL = 2048
BATCH = 8
SEQ = 2048
DEC_BATCH = 128
DEC_SEQ = 8
PAST_LEN = 2048
PAGE_SIZE = 128
N_PAGES = PAST_LEN // PAGE_SIZE
N_HEADS = 8
N_KV_HEADS = 2
HEAD_DIM = 128
GROUP = N_HEADS // N_KV_HEADS
D_ATTN = N_HEADS * HEAD_DIM
D_KV = N_KV_HEADS * HEAD_DIM
IDX_HEADS = 8
IDX_DIM = 64
TOPK = 256
Q_BLOCK = 128
N_BUCKETS = 32
MAX_DISTANCE = 128
D_CONV = D_MODEL // 2
CONV_WIDTH = 31
EPS = 1e-6
NEG = -1e30
NEG_INF = float("-inf")

LANE = 128
SUBLANE = 8
VMEM_LIMIT = 56 * 1024 * 1024

C_GA = 0
C_GC = C_GA + D_MODEL
C_Q = C_GC + D_MODEL
C_ZA = C_Q + D_ATTN
C_GV = C_ZA + D_ATTN
C_GG = C_GV + D_CONV
C_ZC = C_GG + D_CONV
C_K = C_ZC + D_CONV
C_V = C_K + D_KV
C_QI = C_V + D_KV
C_KI = C_QI + IDX_HEADS * IDX_DIM
W_OFF = IDX_DIM
MXU_WIDTH = 256
PROJ_TM = 1024
PROJ_TN = 6 * MXU_WIDTH
D_H = -(-(C_KI + LANE) // PROJ_TN) * PROJ_TN
KV_TILE = C_K // PROJ_TN
assert C_K % PROJ_TN == 0 and D_H == C_K + PROJ_TN
D_HB = C_K
D_HF = C_KI + LANE - C_K
F_K = C_K - D_HB
F_V = C_V - D_HB
F_QI = C_QI - D_HB
F_KI = C_KI - D_HB

_SRC = {}
_off = 0
for _name, _w in (('q', D_ATTN), ('k', D_KV), ('v', D_KV), ('z_attn', D_ATTN),
                  ('q_idx', IDX_HEADS * IDX_DIM), ('k_idx', IDX_DIM), ('w_idx', IDX_HEADS),
                  ('glu_val', D_CONV), ('glu_gate', D_CONV), ('z_conv', D_CONV),
                  ('gate_attn', D_MODEL), ('gate_conv', D_MODEL)):
    _SRC[_name] = (_off, _w)
    _off += _w

N_BISECT = 20
PAD_ROWS = 32
CONV_CHUNK = 64
SCORE_SCALE = (IDX_DIM ** -0.5) * (IDX_HEADS ** -0.5)
QK_SCALE = HEAD_DIM ** -0.5
LOG2E = math.log2(math.e)
L_SAMPLE = (N_PAGES + 1) * LANE
SUBTILES = LANE // SUBLANE
CHUNK_BLOCKS = 4
CHUNK = CHUNK_BLOCKS * Q_BLOCK
PAD_KEYS = CHUNK - Q_BLOCK
VT_ROWS = HEAD_DIM + 2 * SUBLANE
SCORE_ROWS = 4
ATTN_ROWS = 4
TAIL_TM = 256


def _t5_bucket_static(dist):
    n = np.maximum(dist, 0)
    max_exact = N_BUCKETS // 2
    ratio = (np.log(np.maximum(n, 1).astype(np.float32) / np.float32(max_exact))
             / np.float32(math.log(MAX_DISTANCE / max_exact)))
    large = np.minimum(max_exact + (ratio * np.float32(N_BUCKETS - max_exact)).astype(np.int32),
                       N_BUCKETS - 1)
    return np.where(n < max_exact, n, large).astype(np.int32)


FAR_BUCKET = int(_t5_bucket_static(np.array([2 * MAX_DISTANCE]))[0])


def _params(sem):
    return pltpu.CompilerParams(dimension_semantics=sem, vmem_limit_bytes=VMEM_LIMIT)


def _proj_kernel(x_ref, g_ref, w_ref, ob_ref, of_ref, k_ref, v_ref, xn_ref):
    tm = x_ref.shape[0]
    j = pl.program_id(1)

    @pl.when(j == 0)
    def _():
        x = x_ref[...]
        ms = jnp.mean(x * x, axis=-1, keepdims=True)
        xn_ref[...] = (x * lax.rsqrt(ms + EPS) * g_ref[...]).astype(BF16)

    def project():
        return lax.dot_general(xn_ref[...], w_ref[...], (((1,), (1,)), ((), ())),
                               preferred_element_type=F32)

    @pl.when(j < KV_TILE)
    def _():
        ob_ref[...] = project().astype(BF16)

    @pl.when(j == KV_TILE)
    def _():
        of_ref[...] = project()[:, :D_HF]
        for g in range(N_KV_HEADS):
            rows = pl.ds(g, tm, stride=N_KV_HEADS)
            k_ref[rows, :] = of_ref[:, F_K + g * HEAD_DIM:F_K + (g + 1) * HEAD_DIM]
            v_ref[rows, :] = of_ref[:, F_V + g * HEAD_DIM:F_V + (g + 1) * HEAD_DIM]


def _proj(x2d, g, w):
    n = x2d.shape[0]
    tm, tn = PROJ_TM, PROJ_TN
    kv_spec = pl.BlockSpec((N_KV_HEADS * tm, HEAD_DIM), lambda i, j: (i, 0))
    kv_shape = jax.ShapeDtypeStruct((N_KV_HEADS * n, HEAD_DIM), F32)
    return pl.pallas_call(
        _proj_kernel,
        grid=(n // tm, D_H // tn),
        in_specs=[pl.BlockSpec((tm, D_MODEL), lambda i, j: (i, 0)),
                  pl.BlockSpec((1, D_MODEL), lambda i, j: (0, 0)),
                  pl.BlockSpec((tn, D_MODEL), lambda i, j: (j, 0))],
        out_specs=[pl.BlockSpec((tm, tn), lambda i, j: (i, jnp.minimum(j, KV_TILE - 1))),
                   pl.BlockSpec((tm, D_HF), lambda i, j: (i, 0)), kv_spec, kv_spec],
        out_shape=[jax.ShapeDtypeStruct((n, D_HB), BF16),
                   jax.ShapeDtypeStruct((n, D_HF), F32), kv_shape, kv_shape],
        scratch_shapes=[pltpu.VMEM((tm, D_MODEL), BF16)],
        compiler_params=_params(("parallel", "arbitrary")),
        name="proj",
    )(x2d, g, w)


def _bias_kernel(rb_ref, bp_ref, bs_ref, op_ref, os_ref):
    bp = bp_ref[...]
    bs = bs_ref[...]
    for h in range(N_HEADS):
        far = rb_ref[FAR_BUCKET, h]
        tp = jnp.zeros(bp.shape, F32)
        ts = jnp.zeros(bs.shape, F32)
        for b in range(N_BUCKETS):
            val = rb_ref[b, h] - far
            tp = jnp.where(bp == b, val * LOG2E, tp)
            ts = jnp.where(bs == b, val, ts)
        op_ref[h] = tp
        os_ref[h] = ts


def _bias_tables(rel_bias):
    key = np.arange(CHUNK)[:, None]
    qry = np.arange(Q_BLOCK)[None, :]
    bucket_p = _t5_bucket_static(CHUNK - Q_BLOCK + qry - key)
    qi = np.arange(DEC_SEQ)[:, None]
    col = np.arange(2 * LANE)[None, :]
    dist_s = np.where(col < LANE, LANE + qi - col, qi - (col - LANE))
    bucket_s = _t5_bucket_static(dist_s)
    return pl.pallas_call(
        _bias_kernel,
        in_specs=[pl.BlockSpec(memory_space=pltpu.SMEM),
                  pl.BlockSpec(memory_space=pltpu.VMEM),
                  pl.BlockSpec(memory_space=pltpu.VMEM)],
        out_specs=[pl.BlockSpec(memory_space=pltpu.VMEM),
                   pl.BlockSpec(memory_space=pltpu.VMEM)],
        out_shape=[jax.ShapeDtypeStruct((N_HEADS, CHUNK, Q_BLOCK), F32),
                   jax.ShapeDtypeStruct((N_HEADS, DEC_SEQ, 2 * LANE), F32)],
        name="bias_tables",
    )(rel_bias, jnp.asarray(bucket_p), jnp.asarray(bucket_s))


def _any(x):
    return jnp.max(jnp.where(x, 1.0, 0.0)) > 0.5


def _rep(x):
    return jnp.broadcast_to(x, (SUBLANE, LANE))


def _fold_rows(x, comb):
    parts = [x[k:k + SUBLANE] for k in range(0, x.shape[0], SUBLANE)]
    while len(parts) > 1:
        parts = [comb(parts[k], parts[k + 1]) for k in range(0, len(parts), 2)]
    return parts[0]


def _select_threshold(tile_fn, ntiles, static):
    def reduce_tiles(fn, init, comb):
        def step(j, acc):
            x = fn(tile_fn(j), j)
            parts = [x[k] for k in range(x.shape[0])]
            while len(parts) > 1:
                parts = [comb(parts[k], parts[k + 1]) for k in range(0, len(parts), 2)]
            return comb(acc, parts[0])
        if static:
            acc = init
            for j in range(ntiles):
                acc = step(j, acc)
            return acc
        return lax.fori_loop(0, ntiles, step, init)

    zeros = jnp.zeros((SUBLANE, LANE), F32)

    def count(pred_fn):
        acc = reduce_tiles(lambda s, j: jnp.where(pred_fn(s, j), 1.0, 0.0), zeros,
                           lambda a, b: a + b)
        return _rep(jnp.sum(acc, axis=0, keepdims=True))

    def masked_max(pred_fn):
        acc = reduce_tiles(lambda s, j: jnp.where(pred_fn(s, j), s, NEG_INF),
                           jnp.full((SUBLANE, LANE), NEG_INF, F32), jnp.maximum)
        return _rep(jnp.max(acc, axis=0, keepdims=True))

    bound = reduce_tiles(lambda s, j: jnp.where(s > NEG_INF, jnp.abs(s), 0.0), zeros,
                         jnp.maximum)
    bound = _rep(jnp.max(bound, axis=0, keepdims=True))

    def bisect(_, carry):
        lo, hi = carry
        mid = 0.5 * lo + 0.5 * hi
        few = count(lambda s, j: s > mid[None]) < TOPK
        return jnp.where(few, lo, mid), jnp.where(few, mid, hi)

    _, hi = lax.fori_loop(0, N_BISECT, bisect, (-bound, bound))

    thr = masked_max(lambda s, j: s <= hi[None])
    n_ge = count(lambda s, j: s >= thr[None])

    def fix_body(carry):
        thr, n_ge, _ = carry
        lower = masked_max(lambda s, j: s < thr[None])
        thr = jnp.where(n_ge < TOPK, lower, thr)
        n_ge = count(lambda s, j: s >= thr[None])
        return thr, n_ge, _any(n_ge < TOPK)

    thr, n_ge, _ = lax.while_loop(lambda c: c[2], fix_body, (thr, n_ge, _any(n_ge < TOPK)))
    n_gt = count(lambda s, j: s > thr[None])
    need = TOPK - n_gt
    return thr[0:1], need[0:1]


def _keep(pred):
    return jnp.where(pred, 0.0, NEG)


def _selection_masks(tiles, thr, need, seen, tri):
    ties = [s == thr for s in tiles]
    ranks = []
    for k in range(0, len(tiles), 2):
        pair = jnp.concatenate([jnp.where(t, 1.0, 0.0).astype(BF16) for t in ties[k:k + 2]], axis=1)
        rank = jnp.dot(tri, pair, preferred_element_type=F32)
        ranks += [rank[:, p * LANE:(p + 1) * LANE] for p in range(len(ties[k:k + 2]))]
    masks = []
    for s, tie, rank in zip(tiles, ties, ranks):
        masks.append(jnp.where(tie, _keep(rank + seen <= need), _keep(s > thr)))
        seen = seen + rank[s.shape[0] - 1:]
    return masks, seen


def _attn_p_kernel(q_ref, qi_ref, wi_ref, k_ref, v_ref, ki_ref, bias_ref, tri_ref, o_ref,
                   kb_ref, vt_ref, kib_ref, qh_ref, qih_ref,
                   score_ref, mask_ref, s_ref, m_ref, acc_ref):
    i = pl.program_id(1)
    T = Q_BLOCK
    W = GROUP * T
    nch = i // CHUNK_BLOCKS + 1

    @pl.when(i == 0)
    def _():
        kb_ref[0:PAD_KEYS] = jnp.zeros((PAD_KEYS, D_KV), BF16)
        kb_ref[PAD_KEYS:] = k_ref[0].astype(BF16)
        kib_ref[0:PAD_KEYS] = jnp.zeros((PAD_KEYS, IDX_DIM), BF16)
        kib_ref[PAD_KEYS:] = ki_ref[0][:, :IDX_DIM].astype(BF16)
        ones_row = lax.broadcasted_iota(jnp.int32, (VT_ROWS - HEAD_DIM, PAD_KEYS + SEQ), 0) == 0
        for g in range(N_KV_HEADS):
            vt_ref[g, 0:HEAD_DIM, 0:PAD_KEYS] = jnp.zeros((HEAD_DIM, PAD_KEYS), BF16)
            vt_ref[g, HEAD_DIM:VT_ROWS, :] = jnp.where(ones_row, 1.0, 0.0).astype(BF16)
            for c in range(SEQ // LANE):
                blk = v_ref[0, c * LANE:(c + 1) * LANE, g * HEAD_DIM:(g + 1) * HEAD_DIM]
                vt_ref[g, 0:HEAD_DIM,
                       PAD_KEYS + c * LANE:PAD_KEYS + (c + 1) * LANE] = blk.T.astype(BF16)
        score_ref[0:PAD_KEYS] = jnp.full((PAD_KEYS, T), NEG_INF, F32)
        mask_ref[0:PAD_KEYS] = jnp.full((PAD_KEYS, T), NEG, F32)

    q = q_ref[0].astype(F32) * (QK_SCALE * LOG2E)
    for h in range(N_HEADS):
        qh_ref[h] = q[:, h * HEAD_DIM:(h + 1) * HEAD_DIM].astype(BF16)
    qi = qi_ref[0]
    for h in range(IDX_HEADS):
        qih_ref[h] = qi[:, h * IDX_DIM:(h + 1) * IDX_DIM].astype(BF16)
    w_rows = wi_ref[0].T[W_OFF:W_OFF + IDX_HEADS] * SCORE_SCALE

    def span(c):
        return pl.ds(pl.multiple_of((i - CHUNK_BLOCKS * c) * LANE, LANE), CHUNK)

    def first_key(c):
        return (i - CHUNK_BLOCKS * c) * LANE - PAD_KEYS

    key_l = lax.broadcasted_iota(jnp.int32, (CHUNK, T), 0)
    qry = i * T + lax.broadcasted_iota(jnp.int32, (CHUNK, T), 1)

    def score_body(c, _):
        kc = kib_ref[span(c), :]
        d = lax.dot_general(kc, qih_ref[...].reshape(IDX_HEADS * T, IDX_DIM),
                            (((1,), (1,)), ((), ())), preferred_element_type=F32)
        acc = jnp.zeros((CHUNK, T), F32)
        for h in range(IDX_HEADS):
            acc = acc + jnp.maximum(d[:, h * T:(h + 1) * T], 0.0) * w_rows[h:h + 1]
        key = first_key(c) + key_l
        acc = jnp.where(key <= qry, jnp.where(key >= 0, acc, NEG_INF), NEG_INF)
        score_ref[span(c), :] = acc
        return 0

    lax.fori_loop(0, nch, score_body, 0)

    nsub = CHUNK // SUBLANE

    def tile_fn(c):
        return score_ref[span(c), :].reshape(nsub, SUBLANE, T)

    @pl.when(i * T + T <= TOPK)
    def _():
        def body(c, _):
            mask_ref[span(c), :] = _keep(score_ref[span(c), :] > NEG_INF)
            return 0
        lax.fori_loop(0, nch, body, 0)

    @pl.when(i * T + T > TOPK)
    def _():
        thr, need = _select_threshold(tile_fn, nch, static=False)

        def body(k, seen):
            c = nch - 1 - k
            start = pl.multiple_of((i - CHUNK_BLOCKS * c) * LANE, LANE)
            parts = [pl.ds(start + p * MXU_WIDTH, MXU_WIDTH) for p in range(CHUNK // MXU_WIDTH)]
            masks, seen = _selection_masks([score_ref[rows, :] for rows in parts],
                                           thr, need, seen, tri_ref[...])
            for rows, mask in zip(parts, masks):
                mask_ref[rows, :] = mask
            return seen

        lax.fori_loop(0, nch, body, jnp.zeros((1, T), F32))

    m_ref[...] = jnp.full(m_ref.shape, NEG, F32)
    acc_ref[...] = jnp.zeros(acc_ref.shape, F32)

    def logits(c, with_bias):
        mb = mask_ref[span(c), :]
        for g in range(N_KV_HEADS):
            kc = kb_ref[span(c), g * HEAD_DIM:(g + 1) * HEAD_DIM]
            qg = qh_ref[g * GROUP:(g + 1) * GROUP].reshape(W, HEAD_DIM)
            s = lax.dot_general(kc, qg, (((1,), (1,)), ((), ())), preferred_element_type=F32)
            if with_bias:
                add = jnp.concatenate([mb + bias_ref[g * GROUP + hq] for hq in range(GROUP)],
                                      axis=1)
            else:
                add = jnp.concatenate([mb] * GROUP, axis=1)
            s = s + add
            s_ref[span(c), g * W:(g + 1) * W] = s
            top = jnp.max(_fold_rows(s, jnp.maximum), axis=0, keepdims=True)
            m_ref[g] = jnp.maximum(m_ref[g], top)

    logits(0, True)

    def logits_body(c, _):
        logits(c, False)
        return 0

    lax.fori_loop(1, nch, logits_body, 0)

    def weigh(c, _):
        ps = [jnp.exp2((s_ref[span(c), g * W:(g + 1) * W] - m_ref[g]).astype(BF16))
              for g in range(N_KV_HEADS)]
        pv = [jnp.dot(vt_ref[g, :, span(c)], ps[g], preferred_element_type=F32)
              for g in range(N_KV_HEADS)]
        for g in range(N_KV_HEADS):
            acc_ref[g] += pv[g]
        return 0

    lax.fori_loop(0, nch, weigh, 0)

    for g in range(N_KV_HEADS):
        o = acc_ref[g, 0:HEAD_DIM] / acc_ref[g, HEAD_DIM:HEAD_DIM + 1]
        for hq in range(GROUP):
            h = g * GROUP + hq
            o_ref[0, :, h * HEAD_DIM:(h + 1) * HEAD_DIM] = o[:, hq * T:(hq + 1) * T].T


def _attn_prompt(hb_p, hf_p, bias_p):
    hb3 = hb_p.reshape(BATCH, SEQ, D_HB)
    hf3 = hf_p.reshape(BATCH, SEQ, D_HF)
    nqb = SEQ // Q_BLOCK
    T = Q_BLOCK
    qi_w = IDX_HEADS * IDX_DIM
    return pl.pallas_call(
        _attn_p_kernel,
        grid=(BATCH, nqb),
        in_specs=[
            pl.BlockSpec((1, T, D_ATTN), lambda b, i: (b, i, C_Q // D_ATTN)),
            pl.BlockSpec((1, T, qi_w), lambda b, i: (b, i, F_QI // qi_w)),
            pl.BlockSpec((1, T, LANE), lambda b, i: (b, i, F_KI // LANE)),
            pl.BlockSpec((1, SEQ, D_KV), lambda b, i: (b, 0, F_K // D_KV)),
            pl.BlockSpec((1, SEQ, D_KV), lambda b, i: (b, 0, F_V // D_KV)),
            pl.BlockSpec((1, SEQ, LANE), lambda b, i: (b, 0, F_KI // LANE)),
            pl.BlockSpec((N_HEADS, CHUNK, T), lambda b, i: (0, 0, 0)),
            pl.BlockSpec((MXU_WIDTH, MXU_WIDTH), lambda b, i: (0, 0)),
        ],
        out_specs=pl.BlockSpec((1, T, D_ATTN), lambda b, i: (b, i, 0)),
        out_shape=jax.ShapeDtypeStruct((BATCH, SEQ, D_ATTN), F32),
        scratch_shapes=[
            pltpu.VMEM((PAD_KEYS + SEQ, D_KV), BF16),
            pltpu.VMEM((N_KV_HEADS, VT_ROWS, PAD_KEYS + SEQ), BF16),
            pltpu.VMEM((PAD_KEYS + SEQ, IDX_DIM), BF16),
            pltpu.VMEM((N_HEADS, T, HEAD_DIM), BF16),
            pltpu.VMEM((IDX_HEADS, T, IDX_DIM), BF16),
            pltpu.VMEM((PAD_KEYS + SEQ, T), F32),
            pltpu.VMEM((PAD_KEYS + SEQ, T), F32),
            pltpu.VMEM((PAD_KEYS + SEQ, N_HEADS * T), F32),
            pltpu.VMEM((N_KV_HEADS, 1, GROUP * T), F32),
            pltpu.VMEM((N_KV_HEADS, VT_ROWS, GROUP * T), F32),
        ],
        compiler_params=_params(("parallel", "arbitrary")),
        name="attn_prompt",
    )(hb3, hf3, hf3, hf3, hf3, hf3, bias_p, jnp.tri(MXU_WIDTH, dtype=BF16))


def _score_s_kernel(pt_ref, qi_ref, w_ref, kin_ref, *rest):
    npg = SCORE_ROWS * N_PAGES
    kip = rest[0:npg]
    o_ref = rest[npg]
    kinp_ref = rest[npg + 1]
    del pt_ref
    R = DEC_SEQ
    qrow = lax.broadcasted_iota(jnp.int32, (R, LANE), 0)
    lane = lax.broadcasted_iota(jnp.int32, (R, LANE), 1)
    kinp_ref[...] = jnp.zeros(kinp_ref.shape, BF16)
    for r in range(SCORE_ROWS):
        kinp_ref[r, 0:2 * R] = jnp.concatenate(
            [kin_ref[r][:, :IDX_DIM], jnp.zeros((R, IDX_DIM), F32)], 0).astype(BF16)
    for r in range(SCORE_ROWS):
        qi = qi_ref[r].astype(BF16)
        wb = jnp.broadcast_to(w_ref[r] * SCORE_SCALE, (IDX_HEADS * R, LANE))
        for t in range(N_PAGES + 1):
            if t < N_PAGES:
                d = jnp.dot(qi, kip[r * N_PAGES + t][0].astype(BF16), preferred_element_type=F32)
            else:
                d = lax.dot_general(qi, kinp_ref[r], (((1,), (1,)), ((), ())),
                                    preferred_element_type=F32)
            e = (jnp.maximum(d, 0.0) * wb).reshape(IDX_HEADS, R, LANE)
            s = e[0]
            for h in range(1, IDX_HEADS):
                s = s + e[h]
            if t == N_PAGES:
                s = jnp.where(lane <= qrow, s, NEG_INF)
            o_ref[r, :, t * LANE:(t + 1) * LANE] = s


def _score_sample(hf_s, page_table, cache_kidx_t):
    R = DEC_SEQ
    G = SCORE_ROWS
    h3 = hf_s.reshape(DEC_BATCH, R, D_HF)
    qi_hq = h3[:, :, F_QI:F_QI + IDX_HEADS * IDX_DIM].reshape(DEC_BATCH, R, IDX_HEADS, IDX_DIM)
    qi_hq = qi_hq.transpose(0, 2, 1, 3).reshape(DEC_BATCH, IDX_HEADS * R, IDX_DIM)
    w_hq = h3[:, :, F_KI + W_OFF:F_KI + W_OFF + IDX_HEADS].transpose(0, 2, 1)
    w_hq = w_hq.reshape(DEC_BATCH, IDX_HEADS * R, 1)
    in_specs = [
        pl.BlockSpec((G, IDX_HEADS * R, IDX_DIM), lambda b, pt: (b, 0, 0)),
        pl.BlockSpec((G, IDX_HEADS * R, 1), lambda b, pt: (b, 0, 0)),
        pl.BlockSpec((G, R, LANE), lambda b, pt: (b, 0, F_KI // LANE)),
    ]
    in_specs += [pl.BlockSpec((1, IDX_DIM, PAGE_SIZE),
                              lambda b, pt, r=r, p=p: (pt[b * G + r, p], 0, 0))
                 for r in range(G) for p in range(N_PAGES)]
    grid_spec = pltpu.PrefetchScalarGridSpec(
        num_scalar_prefetch=1,
        grid=(DEC_BATCH // G,),
        in_specs=in_specs,
        out_specs=pl.BlockSpec((G, R, L_SAMPLE), lambda b, pt: (b, 0, 0)),
        scratch_shapes=[pltpu.VMEM((G, PAGE_SIZE, IDX_DIM), BF16)],
    )
    return pl.pallas_call(
        _score_s_kernel,
        grid_spec=grid_spec,
        out_shape=jax.ShapeDtypeStruct((DEC_BATCH, R, L_SAMPLE), F32),
        compiler_params=_params(("arbitrary",)),
        name="score_sample",
    )(page_table, qi_hq, w_hq, h3, *([cache_kidx_t] * (G * N_PAGES)))


def _select_s_kernel(s_ref, tri_ref, o_ref):
    nt = N_PAGES + 1

    def tile_fn(j):
        return s_ref[j * LANE:(j + 1) * LANE, :].reshape(SUBTILES, SUBLANE, LANE)

    thr, need = _select_threshold(tile_fn, nt, static=True)
    parts = [slice(j * LANE, (j + 1) * LANE) for j in range(nt)]
    masks, _ = _selection_masks([s_ref[rows, :] for rows in parts], thr, need,
                                jnp.zeros((1, LANE), F32), tri_ref[...])
    for rows, mask in zip(parts, masks):
        o_ref[rows, :] = mask


def _select_sample(scores_t):
    n = scores_t.shape[1]
    return pl.pallas_call(
        _select_s_kernel,
        grid=(n // LANE,),
        in_specs=[pl.BlockSpec((L_SAMPLE, LANE), lambda c: (0, c)),
                  pl.BlockSpec((LANE, LANE), lambda c: (0, 0))],
        out_specs=pl.BlockSpec((L_SAMPLE, LANE), lambda c: (0, c)),
        out_shape=jax.ShapeDtypeStruct((L_SAMPLE, n), F32),
        compiler_params=_params(("parallel",)),
        name="select_sample",
    )(scores_t, jnp.tri(LANE, dtype=BF16))


def _attn_s_kernel(pt_ref, q_ref, kn_ref, vn_ref, mask_ref, bias_ref, *rest):
    npg = ATTN_ROWS * N_PAGES
    kp = rest[0:npg]
    vp = rest[npg:2 * npg]
    o_ref = rest[2 * npg]
    knp_ref, vnp_ref, logit_ref = rest[2 * npg + 1:]
    del pt_ref
    R = DEC_SEQ
    NT = N_PAGES + 1
    GR = GROUP * R

    knp_ref[...] = jnp.zeros(knp_ref.shape, BF16)
    vnp_ref[...] = jnp.zeros(vnp_ref.shape, BF16)
    for r in range(ATTN_ROWS):
        knp_ref[r, 0:2 * R] = jnp.concatenate([kn_ref[r], jnp.zeros((R, D_KV), F32)], 0).astype(BF16)
        vnp_ref[r, 0:2 * R] = jnp.concatenate([vn_ref[r], jnp.zeros((R, D_KV), F32)], 0).astype(BF16)

    def page_head(refs, pad_ref, r, t, g):
        if t < N_PAGES:
            return refs[r * N_PAGES + t][pl.ds(g, PAGE_SIZE, stride=N_KV_HEADS), :].astype(BF16)
        return pad_ref[r, :, g * HEAD_DIM:(g + 1) * HEAD_DIM]

    for r in range(ATTN_ROWS):
        q = (q_ref[r].astype(F32) * QK_SCALE).astype(BF16)
        for t in range(NT):
            mb = mask_ref[r, :, t * LANE:(t + 1) * LANE]
            for g in range(N_KV_HEADS):
                lg = lax.dot_general(q[g * GR:(g + 1) * GR], page_head(kp, knp_ref, r, t, g),
                                     (((1,), (1,)), ((), ())), preferred_element_type=F32)
                lg = lg.reshape(GROUP, R, LANE) + mb[None]
                if t >= N_PAGES - 1:
                    off = (t - (N_PAGES - 1)) * LANE
                    lg = lg + bias_ref[g * GROUP:(g + 1) * GROUP, :, off:off + LANE]
                logit_ref[r, g * GR:(g + 1) * GR, t * LANE:(t + 1) * LANE] = lg.reshape(GR, LANE)

    for r in range(ATTN_ROWS):
        logits = logit_ref[r]
        m = jnp.max(logits, axis=1, keepdims=True)
        p = jnp.exp(logits - m)
        inv = 1.0 / jnp.sum(p, axis=1, keepdims=True)
        pb = p.astype(BF16)
        outs = [jnp.zeros((GR, HEAD_DIM), F32) for _ in range(N_KV_HEADS)]
        for t in range(NT):
            for g in range(N_KV_HEADS):
                outs[g] = outs[g] + jnp.dot(pb[g * GR:(g + 1) * GR, t * LANE:(t + 1) * LANE],
                                            page_head(vp, vnp_ref, r, t, g),
                                            preferred_element_type=F32)
        for g in range(N_KV_HEADS):
            o = outs[g] * inv[g * GR:(g + 1) * GR]
            for hq in range(GROUP):
                h = g * GROUP + hq
                o_ref[r, :, h * HEAD_DIM:(h + 1) * HEAD_DIM] = o[hq * R:(hq + 1) * R]


def _attn_sample(hb_s, hf_s, page_table, cache_k, cache_v, mask, bias_s):
    R = DEC_SEQ
    G = ATTN_ROWS
    h3 = hf_s.reshape(DEC_BATCH, R, D_HF)
    q_hq = hb_s[:, C_Q:C_Q + D_ATTN].reshape(DEC_BATCH, R, N_HEADS, HEAD_DIM)
    q_hq = q_hq.transpose(0, 2, 1, 3).reshape(DEC_BATCH, N_HEADS * R, HEAD_DIM)
    rows_per_page = PAGE_SIZE * N_KV_HEADS
    ck = cache_k.reshape(-1, HEAD_DIM)
    cv = cache_v.reshape(-1, HEAD_DIM)

    in_specs = [
        pl.BlockSpec((G, N_HEADS * R, HEAD_DIM), lambda b, pt: (b, 0, 0)),
        pl.BlockSpec((G, R, D_KV), lambda b, pt: (b, 0, F_K // D_KV)),
        pl.BlockSpec((G, R, D_KV), lambda b, pt: (b, 0, F_V // D_KV)),
        pl.BlockSpec((G, R, L_SAMPLE), lambda b, pt: (b, 0, 0)),
        pl.BlockSpec((N_HEADS, R, 2 * LANE), lambda b, pt: (0, 0, 0)),
    ]
    pages = [pl.BlockSpec((rows_per_page, HEAD_DIM), lambda b, pt, r=r, p=p: (pt[b * G + r, p], 0))
             for r in range(G) for p in range(N_PAGES)]
    in_specs += pages + pages
    grid_spec = pltpu.PrefetchScalarGridSpec(
        num_scalar_prefetch=1,
        grid=(DEC_BATCH // G,),
        in_specs=in_specs,
        out_specs=pl.BlockSpec((G, R, D_ATTN), lambda b, pt: (b, 0, 0)),
        scratch_shapes=[
            pltpu.VMEM((G, PAGE_SIZE, D_KV), BF16),
            pltpu.VMEM((G, PAGE_SIZE, D_KV), BF16),
            pltpu.VMEM((G, N_HEADS * R, L_SAMPLE), F32),
        ],
    )
    return pl.pallas_call(
        _attn_s_kernel,
        grid_spec=grid_spec,
        out_shape=jax.ShapeDtypeStruct((DEC_BATCH, R, D_ATTN), F32),
        compiler_params=_params(("arbitrary",)),
        name="attn_sample",
    )(page_table, q_hq, h3, h3, mask, bias_s, *([ck] * (G * N_PAGES)), *([cv] * (G * N_PAGES)))


def _conv_p_kernel(val_ref, gate_ref, cw_ref, cb_ref, dw_ref, ut_ref, pad_ref):
    pad_ref[0:PAD_ROWS] = jnp.zeros((PAD_ROWS, LANE), F32)
    pad_ref[PAD_ROWS:] = val_ref[0].astype(F32) * jax.nn.sigmoid(gate_ref[0].astype(F32))
    ut_ref[0] = pad_ref[SEQ:SEQ + PAD_ROWS]
    cw = cw_ref[...]
    cb = cb_ref[...]
    first = PAD_ROWS - (CONV_WIDTH - 1)
    for c in range(SEQ // CONV_CHUNK):
        base = c * CONV_CHUNK
        acc = jnp.broadcast_to(cb, (CONV_CHUNK, LANE))
        for r in range(SUBLANE):
            taps = [w for w in range(CONV_WIDTH) if (first + w) % SUBLANE == r]
            span = max(first + w - r for w in taps) + CONV_CHUNK
            win = pad_ref[base + r:base + r + span]
            for w in taps:
                a = first + w - r
                acc = acc + win[a:a + CONV_CHUNK] * cw[w:w + 1]
        dw_ref[0, base:base + CONV_CHUNK] = acc


def _conv_prompt(hb_p, cw_pad, cb):
    h3 = hb_p.reshape(BATCH, SEQ, D_HB)
    nc = D_CONV // LANE
    return pl.pallas_call(
        _conv_p_kernel,
        grid=(BATCH, nc),
        in_specs=[pl.BlockSpec((1, SEQ, LANE), lambda b, c: (b, 0, C_GV // LANE + c)),
                  pl.BlockSpec((1, SEQ, LANE), lambda b, c: (b, 0, C_GG // LANE + c)),
                  pl.BlockSpec((PAD_ROWS, LANE), lambda b, c: (0, c)),
                  pl.BlockSpec((1, LANE), lambda b, c: (0, c))],
        out_specs=[pl.BlockSpec((1, SEQ, LANE), lambda b, c: (b, 0, c)),
                   pl.BlockSpec((1, PAD_ROWS, LANE), lambda b, c: (b, 0, c))],
        out_shape=[jax.ShapeDtypeStruct((BATCH, SEQ, D_CONV), F32),
                   jax.ShapeDtypeStruct((BATCH, PAD_ROWS, D_CONV), F32)],
        scratch_shapes=[pltpu.VMEM((PAD_ROWS + SEQ, LANE), F32)],
        compiler_params=_params(("parallel", "parallel")),
        name="conv_prompt",
    )(h3, h3, cw_pad, cb)


def _conv_s_kernel(val_ref, gate_ref, st_ref, cw_ref, cb_ref, dw_ref, ns_ref):
    R = DEC_SEQ
    H = CONV_WIDTH - 1
    cw = cw_ref[...]
    cb = jnp.broadcast_to(cb_ref[...], (DEC_BATCH, LANE))
    u = []
    for q in range(R):
        rows = pl.ds(q, DEC_BATCH, stride=R)
        u.append(val_ref[rows, :] * jax.nn.sigmoid(gate_ref[rows, :]))

    def row(r):
        return st_ref[r] if r < H else u[r - H]

    for q in range(R):
        acc = cb
        for w in range(CONV_WIDTH):
            acc = acc + row(q + w) * cw[w:w + 1]
        dw_ref[pl.ds(q, DEC_BATCH, stride=R), :] = acc
    for r in range(H):
        ns_ref[r] = row(r + R)


def _conv_sample(val, gate, state_t, cw_pad, cb):
    n_s = DEC_BATCH * DEC_SEQ
    H = CONV_WIDTH - 1
    nc = D_CONV // LANE
    return pl.pallas_call(
        _conv_s_kernel,
        grid=(nc,),
        in_specs=[pl.BlockSpec((n_s, LANE), lambda c: (0, c)),
                  pl.BlockSpec((n_s, LANE), lambda c: (0, c)),
                  pl.BlockSpec((H, DEC_BATCH, LANE), lambda c: (0, 0, c)),
                  pl.BlockSpec((PAD_ROWS, LANE), lambda c: (0, c)),
                  pl.BlockSpec((1, LANE), lambda c: (0, c))],
        out_specs=[pl.BlockSpec((n_s, LANE), lambda c: (0, c)),
                   pl.BlockSpec((H, DEC_BATCH, LANE), lambda c: (0, 0, c))],
        out_shape=[jax.ShapeDtypeStruct((n_s, D_CONV), F32),
                   jax.ShapeDtypeStruct((H, DEC_BATCH, D_CONV), F32)],
        compiler_params=_params(("parallel",)),
        name="conv_sample",
    )(val, gate, state_t, cw_pad, cb)


def _tail_kernel(attn_ref, za_ref, dw_ref, zc_ref, ga_ref, gc_ref, x_ref,
                 wua_ref, wpw_ref, wuc_ref, wo_ref, ng_ref, nb_ref, bpw_ref, fg_ref, y_ref):
    a = attn_ref[...] * jax.nn.silu(za_ref[...].astype(F32))
    branch_attn = jnp.dot(a.astype(BF16), wua_ref[...], preferred_element_type=F32)

    dw = dw_ref[...]
    mu = jnp.mean(dw, axis=-1, keepdims=True)
    var = jnp.mean(jnp.square(dw - mu), axis=-1, keepdims=True)
    ln = (dw - mu) * lax.rsqrt(var + EPS) * ng_ref[...] + nb_ref[...]
    conv_out = jnp.dot(jax.nn.silu(ln).astype(BF16), wpw_ref[...],
                       preferred_element_type=F32) + bpw_ref[...]
    c = conv_out * jax.nn.silu(zc_ref[...].astype(F32))
    branch_conv = jnp.dot(c.astype(BF16), wuc_ref[...], preferred_element_type=F32)

    merged = (jax.nn.sigmoid(ga_ref[...].astype(F32)) * branch_attn
              + jax.nn.sigmoid(gc_ref[...].astype(F32)) * branch_conv)
    y = x_ref[...] + jnp.dot(merged.astype(BF16), wo_ref[...], preferred_element_type=F32)
    ms = jnp.mean(y * y, axis=-1, keepdims=True)
    y_ref[...] = y * lax.rsqrt(ms + EPS) * fg_ref[...]


def _tail(h, attn, dw, x2d, weights, tm):
    n = x2d.shape[0]

    def const(shape):
        return pl.BlockSpec(shape, lambda i: (0, 0), pipeline_mode=pl.Buffered(1))

    def cols(width, offset):
        return pl.BlockSpec((tm, width), lambda i: (i, offset // width))

    return pl.pallas_call(
        _tail_kernel,
        grid=(n // tm,),
        in_specs=[cols(D_ATTN, 0), cols(D_ATTN, C_ZA), cols(D_CONV, 0), cols(D_CONV, C_ZC),
                  cols(D_MODEL, C_GA), cols(D_MODEL, C_GC), cols(D_MODEL, 0),
                  const((D_ATTN, D_MODEL)), const((D_CONV, D_CONV)),
                  const((D_CONV, D_MODEL)), const((D_MODEL, D_MODEL)),
                  const((1, D_CONV)), const((1, D_CONV)), const((1, D_CONV)),
                  const((1, D_MODEL))],
        out_specs=pl.BlockSpec((tm, D_MODEL), lambda i: (i, 0)),
        out_shape=jax.ShapeDtypeStruct((n, D_MODEL), F32),
        compiler_params=_params(("parallel",)),
        name="tail",
    )(attn, h, dw, h, h, h, x2d, *weights)


def _reorder_w_in(w_in):
    w_t = w_in.T

    def rows(name):
        o, w = _SRC[name]
        return w_t[o:o + w]

    pad = jnp.zeros((D_H - C_KI - IDX_DIM - IDX_HEADS, D_MODEL), w_in.dtype)
    w = jnp.concatenate([rows('gate_attn'), rows('gate_conv'), rows('q'), rows('z_attn'),
                         rows('glu_val'), rows('glu_gate'), rows('z_conv'), rows('k'), rows('v'),
                         rows('q_idx'), rows('k_idx'), rows('w_idx'), pad], axis=0)
    return w.astype(BF16)


def kernel(x_prompt, x_sample, cache_k, cache_v, cache_kidx, state_conv, page_table,
           ln_g, w_in, conv_w, conv_b, conv_norm_g, conv_norm_b, w_pw, b_pw,
           w_up_attn, w_up_conv, w_out, rel_bias, final_g):
    w_all = _reorder_w_in(w_in[0])
    g_in = ln_g[0].reshape(1, D_MODEL)
    xp = x_prompt.reshape(BATCH * SEQ, D_MODEL)
    xs = x_sample.reshape(DEC_BATCH * DEC_SEQ, D_MODEL)
    hb_p, hf_p, k_p, v_p = _proj(xp, g_in, w_all)
    hb_s, hf_s, k_s, v_s = _proj(xs, g_in, w_all)

    bias_p, bias_s = _bias_tables(rel_bias)

    attn_p = _attn_prompt(hb_p, hf_p, bias_p).reshape(BATCH * SEQ, D_ATTN)

    n_s = DEC_BATCH * DEC_SEQ
    scores = _score_sample(hf_s, page_table, cache_kidx[0].transpose(0, 2, 1))
    mask_t = _select_sample(scores.reshape(n_s, L_SAMPLE).T)
    mask = mask_t.T.reshape(DEC_BATCH, DEC_SEQ, L_SAMPLE)
    attn_s = _attn_sample(hb_s, hf_s, page_table, cache_k[0], cache_v[0], mask, bias_s)
    attn_s = attn_s.reshape(n_s, D_ATTN)

    cw_pad = jnp.concatenate([conv_w[0], jnp.zeros((PAD_ROWS - CONV_WIDTH, D_CONV), F32)], 0)
    cb = conv_b[0].reshape(1, D_CONV)
    dw_p, u_tail = _conv_prompt(hb_p, cw_pad, cb)
    dw_s, state_new = _conv_sample(hb_s[:, C_GV:C_GV + D_CONV].astype(F32),
                                   hb_s[:, C_GG:C_GG + D_CONV].astype(F32),
                                   state_conv[0].transpose(1, 0, 2), cw_pad, cb)

    wua = w_up_attn[0].astype(BF16)
    wpw = w_pw[0].astype(BF16)
    wuc = w_up_conv[0].astype(BF16)
    wo = w_out[0].astype(BF16)
    ng = conv_norm_g[0].reshape(1, D_CONV)
    nb = conv_norm_b[0].reshape(1, D_CONV)
    bpw = b_pw[0].reshape(1, D_CONV)
    fg = final_g.reshape(1, D_MODEL)
    weights = (wua, wpw, wuc, wo, ng, nb, bpw, fg)
    y_p = _tail(hb_p, attn_p, dw_p.reshape(BATCH * SEQ, D_CONV), xp, weights, TAIL_TM)
    y_s = _tail(hb_s, attn_s, dw_s, xs, weights, TAIL_TM)

    tail_rows = CONV_WIDTH - 1
    return (
        y_p.reshape(BATCH, SEQ, D_MODEL),
        y_s.reshape(DEC_BATCH, DEC_SEQ, D_MODEL),
        k_p.reshape(1, BATCH, SEQ, N_KV_HEADS, HEAD_DIM),
        v_p.reshape(1, BATCH, SEQ, N_KV_HEADS, HEAD_DIM),
        hf_p[:, F_KI:F_KI + IDX_DIM].reshape(1, BATCH, SEQ, IDX_DIM),
        u_tail[:, PAD_ROWS - tail_rows:].reshape(1, BATCH, tail_rows, D_CONV),
        k_s.reshape(1, DEC_BATCH, DEC_SEQ, N_KV_HEADS, HEAD_DIM),
        v_s.reshape(1, DEC_BATCH, DEC_SEQ, N_KV_HEADS, HEAD_DIM),
        hf_s[:, F_KI:F_KI + IDX_DIM].reshape(1, DEC_BATCH, DEC_SEQ, IDX_DIM),
        state_new.transpose(1, 0, 2).reshape(1, DEC_BATCH, tail_rows, D_CONV),
    )
```

```python
import math

import numpy as np
import jax
import jax.numpy as jnp
from jax import lax
from jax.experimental import pallas as pl
from jax.experimental.pallas import tpu as pltpu

F32 = jnp.float32
BF16 = jnp.bfloat16

D_MODEL = 2048
BATCH = 8
SEQ = 2048
DEC_BATCH = 128
DEC_SEQ = 8
PAST_LEN = 2048
PAGE_SIZE = 128
N_PAGES = PAST_LEN // PAGE_SIZE
N_HEADS = 8
N_KV_HEADS = 2
HEAD_DIM = 128
GROUP = N_HEADS // N_KV_HEADS
D_ATTN = N_HEADS * HEAD_DIM
D_KV = N_KV_HEADS * HEAD_DIM
IDX_HEADS = 8
IDX_DIM = 64
TOPK = 256
Q_BLOCK = 128
N_BUCKETS = 32
MAX_DISTANCE = 128
D_CONV = D_MODEL // 2
CONV_WIDTH = 31
EPS = 1e-6
NEG = -1e30
NEG_INF = float("-inf")

LANE = 128
SUBLANE = 8
VMEM_LIMIT = 56 * 1024 * 1024

C_GA = 0
C_GC = C_GA + D_MODEL
C_Q = C_GC + D_MODEL
C_ZA = C_Q + D_ATTN
C_GV = C_ZA + D_ATTN
C_GG = C_GV + D_CONV
C_ZC = C_GG + D_CONV
C_K = C_ZC + D_CONV
C_V = C_K + D_KV
C_QI = C_V + D_KV
C_KI = C_QI + IDX_HEADS * IDX_DIM
W_OFF = IDX_DIM
MXU_WIDTH = 256
PROJ_TM = 1024
PROJ_TN = 6 * MXU_WIDTH
D_H = -(-(C_KI + LANE) // PROJ_TN) * PROJ_TN
KV_TILE = C_K // PROJ_TN
assert C_K % PROJ_TN == 0 and D_H == C_K + PROJ_TN
D_HB = C_K
D_HF = C_KI + LANE - C_K
F_K = C_K - D_HB
F_V = C_V - D_HB
F_QI = C_QI - D_HB
F_KI = C_KI - D_HB

_SRC = {}
_off = 0
for _name, _w in (('q', D_ATTN), ('k', D_KV), ('v', D_KV), ('z_attn', D_ATTN),
                  ('q_idx', IDX_HEADS * IDX_DIM), ('k_idx', IDX_DIM), ('w_idx', IDX_HEADS),
                  ('glu_val', D_CONV), ('glu_gate', D_CONV), ('z_conv', D_CONV),
                  ('gate_attn', D_MODEL), ('gate_conv', D_MODEL)):
    _SRC[_name] = (_off, _w)
    _off += _w
D_IN = _off
REORDER_ROWS = 512

N_BISECT = 20
PAD_ROWS = 32
CONV_CHUNK = 64
SCORE_SCALE = (IDX_DIM ** -0.5) * (IDX_HEADS ** -0.5)
QK_SCALE = HEAD_DIM ** -0.5
LOG2E = math.log2(math.e)
L_SAMPLE = (N_PAGES + 1) * LANE
SUBTILES = LANE // SUBLANE
CHUNK_BLOCKS = 4
CHUNK = CHUNK_BLOCKS * Q_BLOCK
PAD_KEYS = CHUNK - Q_BLOCK
VT_ROWS = HEAD_DIM + 2 * SUBLANE
SCORE_ROWS = 4
ATTN_ROWS = 4
TAIL_TM = 256


def _t5_bucket_static(dist):
    n = np.maximum(dist, 0)
    max_exact = N_BUCKETS // 2
    ratio = (np.log(np.maximum(n, 1).astype(np.float32) / np.float32(max_exact))
             / np.float32(math.log(MAX_DISTANCE / max_exact)))
    large = np.minimum(max_exact + (ratio * np.float32(N_BUCKETS - max_exact)).astype(np.int32),
                       N_BUCKETS - 1)
    return np.where(n < max_exact, n, large).astype(np.int32)


FAR_BUCKET = int(_t5_bucket_static(np.array([2 * MAX_DISTANCE]))[0])


def _params(sem):
    return pltpu.CompilerParams(dimension_semantics=sem, vmem_limit_bytes=VMEM_LIMIT)


def _proj_kernel(x_ref, g_ref, w_ref, ob_ref, of_ref, k_ref, v_ref, xn_ref):
    tm = x_ref.shape[0]
    j = pl.program_id(1)

    @pl.when(j == 0)
    def _():
        x = x_ref[...]
        ms = jnp.mean(x * x, axis=-1, keepdims=True)
        xn_ref[...] = (x * lax.rsqrt(ms + EPS) * g_ref[...]).astype(BF16)

    def project():
        return lax.dot_general(xn_ref[...], w_ref[...], (((1,), (1,)), ((), ())),
                               preferred_element_type=F32)

    @pl.when(j < KV_TILE)
    def _():
        ob_ref[...] = project().astype(BF16)

    @pl.when(j == KV_TILE)
    def _():
        of_ref[...] = project()[:, :D_HF]
        for g in range(N_KV_HEADS):
            rows = pl.ds(g, tm, stride=N_KV_HEADS)
            k_ref[rows, :] = of_ref[:, F_K + g * HEAD_DIM:F_K + (g + 1) * HEAD_DIM]
            v_ref[rows, :] = of_ref[:, F_V + g * HEAD_DIM:F_V + (g + 1) * HEAD_DIM]


def _proj(x2d, g, w):
    n = x2d.shape[0]
    tm, tn = PROJ_TM, PROJ_TN
    kv_spec = pl.BlockSpec((N_KV_HEADS * tm, HEAD_DIM), lambda i, j: (i, 0))
    kv_shape = jax.ShapeDtypeStruct((N_KV_HEADS * n, HEAD_DIM), F32)
    return pl.pallas_call(
        _proj_kernel,
        grid=(n // tm, D_H // tn),
        in_specs=[pl.BlockSpec((tm, D_MODEL), lambda i, j: (i, 0)),
                  pl.BlockSpec((1, D_MODEL), lambda i, j: (0, 0)),
                  pl.BlockSpec((tn, D_MODEL), lambda i, j: (j, 0))],
        out_specs=[pl.BlockSpec((tm, tn), lambda i, j: (i, jnp.minimum(j, KV_TILE - 1))),
                   pl.BlockSpec((tm, D_HF), lambda i, j: (i, 0)), kv_spec, kv_spec],
        out_shape=[jax.ShapeDtypeStruct((n, D_HB), BF16),
                   jax.ShapeDtypeStruct((n, D_HF), F32), kv_shape, kv_shape],
        scratch_shapes=[pltpu.VMEM((tm, D_MODEL), BF16)],
        compiler_params=_params(("parallel", "arbitrary")),
        name="proj",
    )(x2d, g, w)


def _bias_kernel(rb_ref, bp_ref, bs_ref, op_ref, os_ref):
    bp = bp_ref[...]
    bs = bs_ref[...]
    for h in range(N_HEADS):
        far = rb_ref[FAR_BUCKET, h]
        tp = jnp.zeros(bp.shape, F32)
        ts = jnp.zeros(bs.shape, F32)
        for b in range(N_BUCKETS):
            val = rb_ref[b, h] - far
            tp = jnp.where(bp == b, val * LOG2E, tp)
            ts = jnp.where(bs == b, val, ts)
        op_ref[h] = tp
        os_ref[h] = ts


def _bias_tables(rel_bias):
    key = np.arange(CHUNK)[:, None]
    qry = np.arange(Q_BLOCK)[None, :]
    bucket_p = _t5_bucket_static(CHUNK - Q_BLOCK + qry - key)
    qi = np.arange(DEC_SEQ)[:, None]
    col = np.arange(2 * LANE)[None, :]
    dist_s = np.where(col < LANE, LANE + qi - col, qi - (col - LANE))
    bucket_s = _t5_bucket_static(dist_s)
    return pl.pallas_call(
        _bias_kernel,
        in_specs=[pl.BlockSpec(memory_space=pltpu.SMEM),
                  pl.BlockSpec(memory_space=pltpu.VMEM),
                  pl.BlockSpec(memory_space=pltpu.VMEM)],
        out_specs=[pl.BlockSpec(memory_space=pltpu.VMEM),
                   pl.BlockSpec(memory_space=pltpu.VMEM)],
        out_shape=[jax.ShapeDtypeStruct((N_HEADS, CHUNK, Q_BLOCK), F32),
                   jax.ShapeDtypeStruct((N_HEADS, DEC_SEQ, 2 * LANE), F32)],
        name="bias_tables",
    )(rel_bias, jnp.asarray(bucket_p), jnp.asarray(bucket_s))


def _any(x):
    return jnp.max(jnp.where(x, 1.0, 0.0)) > 0.5


def _rep(x):
    return jnp.broadcast_to(x, (SUBLANE, LANE))


def _fold_rows(x, comb):
    parts = [x[k:k + SUBLANE] for k in range(0, x.shape[0], SUBLANE)]
    while len(parts) > 1:
        parts = [comb(parts[k], parts[k + 1]) for k in range(0, len(parts), 2)]
    return parts[0]


def _select_threshold(tile_fn, ntiles, static):
    def reduce_tiles(fn, init, comb):
        def step(j, acc):
            x = fn(tile_fn(j), j)
            parts = [x[k] for k in range(x.shape[0])]
            while len(parts) > 1:
                parts = [comb(parts[k], parts[k + 1]) for k in range(0, len(parts), 2)]
            return comb(acc, parts[0])
        if static:
            acc = init
            for j in range(ntiles):
                acc = step(j, acc)
            return acc
        return lax.fori_loop(0, ntiles, step, init)

    zeros = jnp.zeros((SUBLANE, LANE), F32)

    def count(pred_fn):
        acc = reduce_tiles(lambda s, j: jnp.where(pred_fn(s, j), 1.0, 0.0), zeros,
                           lambda a, b: a + b)
        return _rep(jnp.sum(acc, axis=0, keepdims=True))

    def masked_max(pred_fn):
        acc = reduce_tiles(lambda s, j: jnp.where(pred_fn(s, j), s, NEG_INF),
                           jnp.full((SUBLANE, LANE), NEG_INF, F32), jnp.maximum)
        return _rep(jnp.max(acc, axis=0, keepdims=True))

    bound = reduce_tiles(lambda s, j: jnp.where(s > NEG_INF, jnp.abs(s), 0.0), zeros,
                         jnp.maximum)
    bound = _rep(jnp.max(bound, axis=0, keepdims=True))

    def bisect(_, carry):
        lo, hi = carry
        mid = 0.5 * lo + 0.5 * hi
        few = count(lambda s, j: s > mid[None]) < TOPK
        return jnp.where(few, lo, mid), jnp.where(few, mid, hi)

    _, hi = lax.fori_loop(0, N_BISECT, bisect, (-bound, bound))

    thr = masked_max(lambda s, j: s <= hi[None])
    n_ge = count(lambda s, j: s >= thr[None])

    def fix_body(carry):
        thr, n_ge, _ = carry
        lower = masked_max(lambda s, j: s < thr[None])
        thr = jnp.where(n_ge < TOPK, lower, thr)
        n_ge = count(lambda s, j: s >= thr[None])
        return thr, n_ge, _any(n_ge < TOPK)

    thr, n_ge, _ = lax.while_loop(lambda c: c[2], fix_body, (thr, n_ge, _any(n_ge < TOPK)))
    n_gt = count(lambda s, j: s > thr[None])
    need = TOPK - n_gt
    return thr[0:1], need[0:1]


def _keep(pred):
    return jnp.where(pred, 0.0, NEG)


def _selection_masks(tiles, thr, need, seen, tri):
    ties = [s == thr for s in tiles]
    ranks = []
    for k in range(0, len(tiles), 2):
        pair = jnp.concatenate([jnp.where(t, 1.0, 0.0).astype(BF16) for t in ties[k:k + 2]], axis=1)
        rank = jnp.dot(tri, pair, preferred_element_type=F32)
        ranks += [rank[:, p * LANE:(p + 1) * LANE] for p in range(len(ties[k:k + 2]))]
    masks = []
    for s, tie, rank in zip(tiles, ties, ranks):
        masks.append(jnp.where(tie, _keep(rank + seen <= need), _keep(s > thr)))
        seen = seen + rank[s.shape[0] - 1:]
    return masks, seen


def _attn_p_kernel(q_ref, qi_ref, wi_ref, k_ref, v_ref, ki_ref, bias_ref, tri_ref, o_ref,
                   kb_ref, vt_ref, kib_ref, qh_ref, qih_ref,
                   score_ref, mask_ref, s_ref, m_ref, acc_ref):
    i = pl.program_id(1)
    T = Q_BLOCK
    W = GROUP * T
    nch = i // CHUNK_BLOCKS + 1

    @pl.when(i == 0)
    def _():
        kb_ref[0:PAD_KEYS] = jnp.zeros((PAD_KEYS, D_KV), BF16)
        kb_ref[PAD_KEYS:] = k_ref[0].astype(BF16)
        kib_ref[0:PAD_KEYS] = jnp.zeros((PAD_KEYS, IDX_DIM), BF16)
        kib_ref[PAD_KEYS:] = ki_ref[0][:, :IDX_DIM].astype(BF16)
        ones_row = lax.broadcasted_iota(jnp.int32, (VT_ROWS - HEAD_DIM, PAD_KEYS + SEQ), 0) == 0
        for g in range(N_KV_HEADS):
            vt_ref[g, 0:HEAD_DIM, 0:PAD_KEYS] = jnp.zeros((HEAD_DIM, PAD_KEYS), BF16)
            vt_ref[g, HEAD_DIM:VT_ROWS, :] = jnp.where(ones_row, 1.0, 0.0).astype(BF16)
            for c in range(SEQ // LANE):
                blk = v_ref[0, c * LANE:(c + 1) * LANE, g * HEAD_DIM:(g + 1) * HEAD_DIM]
                vt_ref[g, 0:HEAD_DIM,
                       PAD_KEYS + c * LANE:PAD_KEYS + (c + 1) * LANE] = blk.T.astype(BF16)
        score_ref[0:PAD_KEYS] = jnp.full((PAD_KEYS, T), NEG_INF, F32)
        mask_ref[0:PAD_KEYS] = jnp.full((PAD_KEYS, T), NEG, F32)

    q = q_ref[0].astype(F32) * (QK_SCALE * LOG2E)
    for h in range(N_HEADS):
        qh_ref[h] = q[:, h * HEAD_DIM:(h + 1) * HEAD_DIM].astype(BF16)
    qi = qi_ref[0]
    for h in range(IDX_HEADS):
        qih_ref[h] = qi[:, h * IDX_DIM:(h + 1) * IDX_DIM].astype(BF16)
    w_rows = wi_ref[0].T[W_OFF:W_OFF + IDX_HEADS] * SCORE_SCALE

    def span(c):
        return pl.ds(pl.multiple_of((i - CHUNK_BLOCKS * c) * LANE, LANE), CHUNK)

    def first_key(c):
        return (i - CHUNK_BLOCKS * c) * LANE - PAD_KEYS

    key_l = lax.broadcasted_iota(jnp.int32, (CHUNK, T), 0)
    qry = i * T + lax.broadcasted_iota(jnp.int32, (CHUNK, T), 1)

    def score_body(c, _):
        kc = kib_ref[span(c), :]
        d = lax.dot_general(kc, qih_ref[...].reshape(IDX_HEADS * T, IDX_DIM),
                            (((1,), (1,)), ((), ())), preferred_element_type=F32)
        acc = jnp.zeros((CHUNK, T), F32)
        for h in range(IDX_HEADS):
            acc = acc + jnp.maximum(d[:, h * T:(h + 1) * T], 0.0) * w_rows[h:h + 1]
        key = first_key(c) + key_l
        acc = jnp.where(key <= qry, jnp.where(key >= 0, acc, NEG_INF), NEG_INF)
        score_ref[span(c), :] = acc
        return 0

    lax.fori_loop(0, nch, score_body, 0)

    nsub = CHUNK // SUBLANE

    def tile_fn(c):
        return score_ref[span(c), :].reshape(nsub, SUBLANE, T)

    @pl.when(i * T + T <= TOPK)
    def _():
        def body(c, _):
            mask_ref[span(c), :] = _keep(score_ref[span(c), :] > NEG_INF)
            return 0
        lax.fori_loop(0, nch, body, 0)

    @pl.when(i * T + T > TOPK)
    def _():
        thr, need = _select_threshold(tile_fn, nch, static=False)

        def body(k, seen):
            c = nch - 1 - k
            start = pl.multiple_of((i - CHUNK_BLOCKS * c) * LANE, LANE)
            parts = [pl.ds(start + p * MXU_WIDTH, MXU_WIDTH) for p in range(CHUNK // MXU_WIDTH)]
            masks, seen = _selection_masks([score_ref[rows, :] for rows in parts],
                                           thr, need, seen, tri_ref[...])
            for rows, mask in zip(parts, masks):
                mask_ref[rows, :] = mask
            return seen

        lax.fori_loop(0, nch, body, jnp.zeros((1, T), F32))

    m_ref[...] = jnp.full(m_ref.shape, NEG, F32)
    acc_ref[...] = jnp.zeros(acc_ref.shape, F32)

    def logits(c, with_bias):
        mb = mask_ref[span(c), :]
        for g in range(N_KV_HEADS):
            kc = kb_ref[span(c), g * HEAD_DIM:(g + 1) * HEAD_DIM]
            qg = qh_ref[g * GROUP:(g + 1) * GROUP].reshape(W, HEAD_DIM)
            s = lax.dot_general(kc, qg, (((1,), (1,)), ((), ())), preferred_element_type=F32)
            if with_bias:
                add = jnp.concatenate([mb + bias_ref[g * GROUP + hq] for hq in range(GROUP)],
                                      axis=1)
            else:
                add = jnp.concatenate([mb] * GROUP, axis=1)
            s = s + add
            s_ref[span(c), g * W:(g + 1) * W] = s
            top = jnp.max(_fold_rows(s, jnp.maximum), axis=0, keepdims=True)
            m_ref[g] = jnp.maximum(m_ref[g], top)

    logits(0, True)

    def logits_body(c, _):
        logits(c, False)
        return 0

    lax.fori_loop(1, nch, logits_body, 0)

    def weigh(c, _):
        ps = [jnp.exp2((s_ref[span(c), g * W:(g + 1) * W] - m_ref[g]).astype(BF16))
              for g in range(N_KV_HEADS)]
        pv = [jnp.dot(vt_ref[g, :, span(c)], ps[g], preferred_element_type=F32)
              for g in range(N_KV_HEADS)]
        for g in range(N_KV_HEADS):
            acc_ref[g] += pv[g]
        return 0

    lax.fori_loop(0, nch, weigh, 0)

    for g in range(N_KV_HEADS):
        o = acc_ref[g, 0:HEAD_DIM] / acc_ref[g, HEAD_DIM:HEAD_DIM + 1]
        for hq in range(GROUP):
            h = g * GROUP + hq
            o_ref[0, :, h * HEAD_DIM:(h + 1) * HEAD_DIM] = o[:, hq * T:(hq + 1) * T].T


def _attn_prompt(hb_p, hf_p, bias_p):
    hb3 = hb_p.reshape(BATCH, SEQ, D_HB)
    hf3 = hf_p.reshape(BATCH, SEQ, D_HF)
    nqb = SEQ // Q_BLOCK
    T = Q_BLOCK
    qi_w = IDX_HEADS * IDX_DIM
    return pl.pallas_call(
        _attn_p_kernel,
        grid=(BATCH, nqb),
        in_specs=[
            pl.BlockSpec((1, T, D_ATTN), lambda b, i: (b, i, C_Q // D_ATTN)),
            pl.BlockSpec((1, T, qi_w), lambda b, i: (b, i, F_QI // qi_w)),
            pl.BlockSpec((1, T, LANE), lambda b, i: (b, i, F_KI // LANE)),
            pl.BlockSpec((1, SEQ, D_KV), lambda b, i: (b, 0, F_K // D_KV)),
            pl.BlockSpec((1, SEQ, D_KV), lambda b, i: (b, 0, F_V // D_KV)),
            pl.BlockSpec((1, SEQ, LANE), lambda b, i: (b, 0, F_KI // LANE)),
            pl.BlockSpec((N_HEADS, CHUNK, T), lambda b, i: (0, 0, 0)),
            pl.BlockSpec((MXU_WIDTH, MXU_WIDTH), lambda b, i: (0, 0)),
        ],
        out_specs=pl.BlockSpec((1, T, D_ATTN), lambda b, i: (b, i, 0)),
        out_shape=jax.ShapeDtypeStruct((BATCH, SEQ, D_ATTN), F32),
        scratch_shapes=[
            pltpu.VMEM((PAD_KEYS + SEQ, D_KV), BF16),
            pltpu.VMEM((N_KV_HEADS, VT_ROWS, PAD_KEYS + SEQ), BF16),
            pltpu.VMEM((PAD_KEYS + SEQ, IDX_DIM), BF16),
            pltpu.VMEM((N_HEADS, T, HEAD_DIM), BF16),
            pltpu.VMEM((IDX_HEADS, T, IDX_DIM), BF16),
            pltpu.VMEM((PAD_KEYS + SEQ, T), F32),
            pltpu.VMEM((PAD_KEYS + SEQ, T), F32),
            pltpu.VMEM((PAD_KEYS + SEQ, N_HEADS * T), F32),
            pltpu.VMEM((N_KV_HEADS, 1, GROUP * T), F32),
            pltpu.VMEM((N_KV_HEADS, VT_ROWS, GROUP * T), F32),
        ],
        compiler_params=_params(("parallel", "arbitrary")),
        name="attn_prompt",
    )(hb3, hf3, hf3, hf3, hf3, hf3, bias_p, jnp.tri(MXU_WIDTH, dtype=BF16))


def _score_s_kernel(pt_ref, qi_ref, w_ref, kin_ref, *rest):
    npg = SCORE_ROWS * N_PAGES
    kip = rest[0:npg]
    o_ref = rest[npg]
    kinp_ref = rest[npg + 1]
    del pt_ref
    R = DEC_SEQ
    qrow = lax.broadcasted_iota(jnp.int32, (R, LANE), 0)
    lane = lax.broadcasted_iota(jnp.int32, (R, LANE), 1)
    kinp_ref[...] = jnp.zeros(kinp_ref.shape, BF16)
    for r in range(SCORE_ROWS):
        kinp_ref[r, 0:2 * R] = jnp.concatenate(
            [kin_ref[r][:, :IDX_DIM], jnp.zeros((R, IDX_DIM), F32)], 0).astype(BF16)
    for r in range(SCORE_ROWS):
        qi = qi_ref[r].astype(BF16)
        wb = jnp.broadcast_to(w_ref[r] * SCORE_SCALE, (IDX_HEADS * R, LANE))
        for t in range(N_PAGES + 1):
            if t < N_PAGES:
                d = jnp.dot(qi, kip[r * N_PAGES + t][0].astype(BF16), preferred_element_type=F32)
            else:
                d = lax.dot_general(qi, kinp_ref[r], (((1,), (1,)), ((), ())),
                                    preferred_element_type=F32)
            e = (jnp.maximum(d, 0.0) * wb).reshape(IDX_HEADS, R, LANE)
            s = e[0]
            for h in range(1, IDX_HEADS):
                s = s + e[h]
            if t == N_PAGES:
                s = jnp.where(lane <= qrow, s, NEG_INF)
            o_ref[r, :, t * LANE:(t + 1) * LANE] = s


def _score_sample(hf_s, page_table, cache_kidx_t):
    R = DEC_SEQ
    G = SCORE_ROWS
    h3 = hf_s.reshape(DEC_BATCH, R, D_HF)
    qi_hq = h3[:, :, F_QI:F_QI + IDX_HEADS * IDX_DIM].reshape(DEC_BATCH, R, IDX_HEADS, IDX_DIM)
    qi_hq = qi_hq.transpose(0, 2, 1, 3).reshape(DEC_BATCH, IDX_HEADS * R, IDX_DIM)
    w_hq = h3[:, :, F_KI + W_OFF:F_KI + W_OFF + IDX_HEADS].transpose(0, 2, 1)
    w_hq = w_hq.reshape(DEC_BATCH, IDX_HEADS * R, 1)
    in_specs = [
        pl.BlockSpec((G, IDX_HEADS * R, IDX_DIM), lambda b, pt: (b, 0, 0)),
        pl.BlockSpec((G, IDX_HEADS * R, 1), lambda b, pt: (b, 0, 0)),
        pl.BlockSpec((G, R, LANE), lambda b, pt: (b, 0, F_KI // LANE)),
    ]
    in_specs += [pl.BlockSpec((1, IDX_DIM, PAGE_SIZE),
                              lambda b, pt, r=r, p=p: (pt[b * G + r, p], 0, 0))
                 for r in range(G) for p in range(N_PAGES)]
    grid_spec = pltpu.PrefetchScalarGridSpec(
        num_scalar_prefetch=1,
        grid=(DEC_BATCH // G,),
        in_specs=in_specs,
        out_specs=pl.BlockSpec((G, R, L_SAMPLE), lambda b, pt: (b, 0, 0)),
        scratch_shapes=[pltpu.VMEM((G, PAGE_SIZE, IDX_DIM), BF16)],
    )
    return pl.pallas_call(
        _score_s_kernel,
        grid_spec=grid_spec,
        out_shape=jax.ShapeDtypeStruct((DEC_BATCH, R, L_SAMPLE), F32),
        compiler_params=_params(("arbitrary",)),
        name="score_sample",
    )(page_table, qi_hq, w_hq, h3, *([cache_kidx_t] * (G * N_PAGES)))


def _select_s_kernel(s_ref, tri_ref, o_ref):
    nt = N_PAGES + 1

    def tile_fn(j):
        return s_ref[j * LANE:(j + 1) * LANE, :].reshape(SUBTILES, SUBLANE, LANE)

    thr, need = _select_threshold(tile_fn, nt, static=True)
    parts = [slice(j * LANE, (j + 1) * LANE) for j in range(nt)]
    masks, _ = _selection_masks([s_ref[rows, :] for rows in parts], thr, need,
                                jnp.zeros((1, LANE), F32), tri_ref[...])
    for rows, mask in zip(parts, masks):
        o_ref[rows, :] = mask


def _select_sample(scores_t):
    n = scores_t.shape[1]
    return pl.pallas_call(
        _select_s_kernel,
        grid=(n // LANE,),
        in_specs=[pl.BlockSpec((L_SAMPLE, LANE), lambda c: (0, c)),
                  pl.BlockSpec((LANE, LANE), lambda c: (0, 0))],
        out_specs=pl.BlockSpec((L_SAMPLE, LANE), lambda c: (0, c)),
        out_shape=jax.ShapeDtypeStruct((L_SAMPLE, n), F32),
        compiler_params=_params(("parallel",)),
        name="select_sample",
    )(scores_t, jnp.tri(LANE, dtype=BF16))


def _attn_s_kernel(pt_ref, q_ref, kn_ref, vn_ref, mask_ref, bias_ref, *rest):
    npg = ATTN_ROWS * N_PAGES
    kp = rest[0:npg]
    vp = rest[npg:2 * npg]
    o_ref = rest[2 * npg]
    knp_ref, vnp_ref, logit_ref = rest[2 * npg + 1:]
    del pt_ref
    R = DEC_SEQ
    NT = N_PAGES + 1
    GR = GROUP * R

    knp_ref[...] = jnp.zeros(knp_ref.shape, BF16)
    vnp_ref[...] = jnp.zeros(vnp_ref.shape, BF16)
    for r in range(ATTN_ROWS):
        knp_ref[r, 0:2 * R] = jnp.concatenate([kn_ref[r], jnp.zeros((R, D_KV), F32)], 0).astype(BF16)
        vnp_ref[r, 0:2 * R] = jnp.concatenate([vn_ref[r], jnp.zeros((R, D_KV), F32)], 0).astype(BF16)

    def page_head(refs, pad_ref, r, t, g):
        if t < N_PAGES:
            return refs[r * N_PAGES + t][pl.ds(g, PAGE_SIZE, stride=N_KV_HEADS), :].astype(BF16)
        return pad_ref[r, :, g * HEAD_DIM:(g + 1) * HEAD_DIM]

    for r in range(ATTN_ROWS):
        q = (q_ref[r].astype(F32) * QK_SCALE).astype(BF16)
        for t in range(NT):
            mb = mask_ref[r, :, t * LANE:(t + 1) * LANE]
            for g in range(N_KV_HEADS):
                lg = lax.dot_general(q[g * GR:(g + 1) * GR], page_head(kp, knp_ref, r, t, g),
                                     (((1,), (1,)), ((), ())), preferred_element_type=F32)
                lg = lg.reshape(GROUP, R, LANE) + mb[None]
                if t >= N_PAGES - 1:
                    off = (t - (N_PAGES - 1)) * LANE
                    lg = lg + bias_ref[g * GROUP:(g + 1) * GROUP, :, off:off + LANE]
                logit_ref[r, g * GR:(g + 1) * GR, t * LANE:(t + 1) * LANE] = lg.reshape(GR, LANE)

    for r in range(ATTN_ROWS):
        logits = logit_ref[r]
        m = jnp.max(logits, axis=1, keepdims=True)
        p = jnp.exp(logits - m)
        inv = 1.0 / jnp.sum(p, axis=1, keepdims=True)
        pb = p.astype(BF16)
        outs = [jnp.zeros((GR, HEAD_DIM), F32) for _ in range(N_KV_HEADS)]
        for t in range(NT):
            for g in range(N_KV_HEADS):
                outs[g] = outs[g] + jnp.dot(pb[g * GR:(g + 1) * GR, t * LANE:(t + 1) * LANE],
                                            page_head(vp, vnp_ref, r, t, g),
                                            preferred_element_type=F32)
        for g in range(N_KV_HEADS):
            o = outs[g] * inv[g * GR:(g + 1) * GR]
            for hq in range(GROUP):
                h = g * GROUP + hq
                o_ref[r, :, h * HEAD_DIM:(h + 1) * HEAD_DIM] = o[hq * R:(hq + 1) * R]


def _attn_sample(hb_s, hf_s, page_table, cache_k, cache_v, mask, bias_s):
    R = DEC_SEQ
    G = ATTN_ROWS
    h3 = hf_s.reshape(DEC_BATCH, R, D_HF)
    q_hq = hb_s[:, C_Q:C_Q + D_ATTN].reshape(DEC_BATCH, R, N_HEADS, HEAD_DIM)
    q_hq = q_hq.transpose(0, 2, 1, 3).reshape(DEC_BATCH, N_HEADS * R, HEAD_DIM)
    rows_per_page = PAGE_SIZE * N_KV_HEADS
    ck = cache_k.reshape(-1, HEAD_DIM)
    cv = cache_v.reshape(-1, HEAD_DIM)

    in_specs = [
        pl.BlockSpec((G, N_HEADS * R, HEAD_DIM), lambda b, pt: (b, 0, 0)),
        pl.BlockSpec((G, R, D_KV), lambda b, pt: (b, 0, F_K // D_KV)),
        pl.BlockSpec((G, R, D_KV), lambda b, pt: (b, 0, F_V // D_KV)),
        pl.BlockSpec((G, R, L_SAMPLE), lambda b, pt: (b, 0, 0)),
        pl.BlockSpec((N_HEADS, R, 2 * LANE), lambda b, pt: (0, 0, 0)),
    ]
    pages = [pl.BlockSpec((rows_per_page, HEAD_DIM), lambda b, pt, r=r, p=p: (pt[b * G + r, p], 0))
             for r in range(G) for p in range(N_PAGES)]
    in_specs += pages + pages
    grid_spec = pltpu.PrefetchScalarGridSpec(
        num_scalar_prefetch=1,
        grid=(DEC_BATCH // G,),
        in_specs=in_specs,
        out_specs=pl.BlockSpec((G, R, D_ATTN), lambda b, pt: (b, 0, 0)),
        scratch_shapes=[
            pltpu.VMEM((G, PAGE_SIZE, D_KV), BF16),
            pltpu.VMEM((G, PAGE_SIZE, D_KV), BF16),
            pltpu.VMEM((G, N_HEADS * R, L_SAMPLE), F32),
        ],
    )
    return pl.pallas_call(
        _attn_s_kernel,
        grid_spec=grid_spec,
        out_shape=jax.ShapeDtypeStruct((DEC_BATCH, R, D_ATTN), F32),
        compiler_params=_params(("arbitrary",)),
        name="attn_sample",
    )(page_table, q_hq, h3, h3, mask, bias_s, *([ck] * (G * N_PAGES)), *([cv] * (G * N_PAGES)))


def _conv_p_kernel(val_ref, gate_ref, cw_ref, cb_ref, dw_ref, ut_ref, pad_ref):
    pad_ref[0:PAD_ROWS] = jnp.zeros((PAD_ROWS, LANE), F32)
    pad_ref[PAD_ROWS:] = val_ref[0].astype(F32) * jax.nn.sigmoid(gate_ref[0].astype(F32))
    ut_ref[0] = pad_ref[SEQ:SEQ + PAD_ROWS]
    cw = cw_ref[...]
    cb = cb_ref[...]
    first = PAD_ROWS - (CONV_WIDTH - 1)
    for c in range(SEQ // CONV_CHUNK):
        base = c * CONV_CHUNK
        acc = jnp.broadcast_to(cb, (CONV_CHUNK, LANE))
        for r in range(SUBLANE):
            taps = [w for w in range(CONV_WIDTH) if (first + w) % SUBLANE == r]
            span = max(first + w - r for w in taps) + CONV_CHUNK
            win = pad_ref[base + r:base + r + span]
            for w in taps:
                a = first + w - r
                acc = acc + win[a:a + CONV_CHUNK] * cw[w:w + 1]
        dw_ref[0, base:base + CONV_CHUNK] = acc


def _conv_prompt(hb_p, cw_pad, cb):
    h3 = hb_p.reshape(BATCH, SEQ, D_HB)
    nc = D_CONV // LANE
    return pl.pallas_call(
        _conv_p_kernel,
        grid=(BATCH, nc),
        in_specs=[pl.BlockSpec((1, SEQ, LANE), lambda b, c: (b, 0, C_GV // LANE + c)),
                  pl.BlockSpec((1, SEQ, LANE), lambda b, c: (b, 0, C_GG // LANE + c)),
                  pl.BlockSpec((PAD_ROWS, LANE), lambda b, c: (0, c)),
                  pl.BlockSpec((1, LANE), lambda b, c: (0, c))],
        out_specs=[pl.BlockSpec((1, SEQ, LANE), lambda b, c: (b, 0, c)),
                   pl.BlockSpec((1, PAD_ROWS, LANE), lambda b, c: (b, 0, c))],
        out_shape=[jax.ShapeDtypeStruct((BATCH, SEQ, D_CONV), F32),
                   jax.ShapeDtypeStruct((BATCH, PAD_ROWS, D_CONV), F32)],
        scratch_shapes=[pltpu.VMEM((PAD_ROWS + SEQ, LANE), F32)],
        compiler_params=_params(("parallel", "parallel")),
        name="conv_prompt",
    )(h3, h3, cw_pad, cb)


def _conv_s_kernel(val_ref, gate_ref, st_ref, cw_ref, cb_ref, dw_ref, ns_ref):
    R = DEC_SEQ
    H = CONV_WIDTH - 1
    cw = cw_ref[...]
    cb = jnp.broadcast_to(cb_ref[...], (DEC_BATCH, LANE))
    u = []
    for q in range(R):
        rows = pl.ds(q, DEC_BATCH, stride=R)
        u.append(val_ref[rows, :] * jax.nn.sigmoid(gate_ref[rows, :]))

    def row(r):
        return st_ref[r] if r < H else u[r - H]

    for q in range(R):
        acc = cb
        for w in range(CONV_WIDTH):
            acc = acc + row(q + w) * cw[w:w + 1]
        dw_ref[pl.ds(q, DEC_BATCH, stride=R), :] = acc
    for r in range(H):
        ns_ref[r] = row(r + R)


def _conv_sample(val, gate, state_t, cw_pad, cb):
    n_s = DEC_BATCH * DEC_SEQ
    H = CONV_WIDTH - 1
    nc = D_CONV // LANE
    return pl.pallas_call(
        _conv_s_kernel,
        grid=(nc,),
        in_specs=[pl.BlockSpec((n_s, LANE), lambda c: (0, c)),
                  pl.BlockSpec((n_s, LANE), lambda c: (0, c)),
                  pl.BlockSpec((H, DEC_BATCH, LANE), lambda c: (0, 0, c)),
                  pl.BlockSpec((PAD_ROWS, LANE), lambda c: (0, c)),
                  pl.BlockSpec((1, LANE), lambda c: (0, c))],
        out_specs=[pl.BlockSpec((n_s, LANE), lambda c: (0, c)),
                   pl.BlockSpec((H, DEC_BATCH, LANE), lambda c: (0, 0, c))],
        out_shape=[jax.ShapeDtypeStruct((n_s, D_CONV), F32),
                   jax.ShapeDtypeStruct((H, DEC_BATCH, D_CONV), F32)],
        compiler_params=_params(("parallel",)),
        name="conv_sample",
    )(val, gate, state_t, cw_pad, cb)


def _tail_kernel(attn_ref, za_ref, dw_ref, zc_ref, ga_ref, gc_ref, x_ref,
                 wua_ref, wpw_ref, wuc_ref, wo_ref, ng_ref, nb_ref, bpw_ref, fg_ref, y_ref):
    a = attn_ref[...] * jax.nn.silu(za_ref[...].astype(F32))
    branch_attn = jnp.dot(a.astype(BF16), wua_ref[...], preferred_element_type=F32)

    dw = dw_ref[...]
    mu = jnp.mean(dw, axis=-1, keepdims=True)
    var = jnp.mean(jnp.square(dw - mu), axis=-1, keepdims=True)
    ln = (dw - mu) * lax.rsqrt(var + EPS) * ng_ref[...] + nb_ref[...]
    conv_out = jnp.dot(jax.nn.silu(ln).astype(BF16), wpw_ref[...],
                       preferred_element_type=F32) + bpw_ref[...]
    c = conv_out * jax.nn.silu(zc_ref[...].astype(F32))
    branch_conv = jnp.dot(c.astype(BF16), wuc_ref[...], preferred_element_type=F32)

    merged = (jax.nn.sigmoid(ga_ref[...].astype(F32)) * branch_attn
              + jax.nn.sigmoid(gc_ref[...].astype(F32)) * branch_conv)
    y = x_ref[...] + jnp.dot(merged.astype(BF16), wo_ref[...], preferred_element_type=F32)
    ms = jnp.mean(y * y, axis=-1, keepdims=True)
    y_ref[...] = y * lax.rsqrt(ms + EPS) * fg_ref[...]


def _tail(h, attn, dw, x2d, weights, tm):
    n = x2d.shape[0]

    def const(shape):
        return pl.BlockSpec(shape, lambda i: (0, 0), pipeline_mode=pl.Buffered(1))

    def cols(width, offset):
        return pl.BlockSpec((tm, width), lambda i: (i, offset // width))

    return pl.pallas_call(
        _tail_kernel,
        grid=(n // tm,),
        in_specs=[cols(D_ATTN, 0), cols(D_ATTN, C_ZA), cols(D_CONV, 0), cols(D_CONV, C_ZC),
                  cols(D_MODEL, C_GA), cols(D_MODEL, C_GC), cols(D_MODEL, 0),
                  const((D_ATTN, D_MODEL)), const((D_CONV, D_CONV)),
                  const((D_CONV, D_MODEL)), const((D_MODEL, D_MODEL)),
                  const((1, D_CONV)), const((1, D_CONV)), const((1, D_CONV)),
                  const((1, D_MODEL))],
        out_specs=pl.BlockSpec((tm, D_MODEL), lambda i: (i, 0)),
        out_shape=jax.ShapeDtypeStruct((n, D_MODEL), F32),
        compiler_params=_params(("parallel",)),
        name="tail",
    )(attn, h, dw, h, h, h, x2d, *weights)


def _reorder_kernel(off_ref, w_ref, o_ref):
    del off_ref
    o_ref[...] = w_ref[...].astype(BF16)


def _reorder_w_in(w_in):
    w_t = w_in.T
    src = []
    for first, dest, width in (('gate_attn', C_GA, 2 * D_MODEL), ('q', C_Q, D_ATTN),
                               ('z_attn', C_ZA, D_ATTN), ('glu_val', C_GV, 3 * D_CONV),
                               ('k', C_K, 2 * D_KV), ('q_idx', C_QI, D_H - C_QI)):
        assert dest == len(src) * REORDER_ROWS
        for r in range(0, width, REORDER_ROWS):
            src.append(min(_SRC[first][0] + r, D_IN - REORDER_ROWS) // SUBLANE)
    grid_spec = pltpu.PrefetchScalarGridSpec(
        num_scalar_prefetch=1,
        grid=(len(src),),
        in_specs=[pl.BlockSpec((pl.Element(REORDER_ROWS), pl.Element(D_MODEL)),
                               lambda d, off: (off[d] * SUBLANE, 0))],
        out_specs=pl.BlockSpec((REORDER_ROWS, D_MODEL), lambda d, off: (d, 0)),
    )
    return pl.pallas_call(
        _reorder_kernel,
        grid_spec=grid_spec,
        out_shape=jax.ShapeDtypeStruct((D_H, D_MODEL), BF16),
        compiler_params=_params(("arbitrary",)),
        name="reorder_w_in",
    )(jnp.asarray(src, jnp.int32), w_t)


def kernel(x_prompt, x_sample, cache_k, cache_v, cache_kidx, state_conv, page_table,
           ln_g, w_in, conv_w, conv_b, conv_norm_g, conv_norm_b, w_pw, b_pw,
           w_up_attn, w_up_conv, w_out, rel_bias, final_g):
    w_all = _reorder_w_in(w_in[0])
    g_in = ln_g[0].reshape(1, D_MODEL)
    xp = x_prompt.reshape(BATCH * SEQ, D_MODEL)
    xs = x_sample.reshape(DEC_BATCH * DEC_SEQ, D_MODEL)
    hb_p, hf_p, k_p, v_p = _proj(xp, g_in, w_all)
    hb_s, hf_s, k_s, v_s = _proj(xs, g_in, w_all)

    bias_p, bias_s = _bias_tables(rel_bias)

    attn_p = _attn_prompt(hb_p, hf_p, bias_p).reshape(BATCH * SEQ, D_ATTN)

    n_s = DEC_BATCH * DEC_SEQ
    scores = _score_sample(hf_s, page_table, cache_kidx[0].transpose(0, 2, 1))
    mask_t = _select_sample(scores.reshape(n_s, L_SAMPLE).T)
    mask = mask_t.T.reshape(DEC_BATCH, DEC_SEQ, L_SAMPLE)
    attn_s = _attn_sample(hb_s, hf_s, page_table, cache_k[0], cache_v[0], mask, bias_s)
    attn_s = attn_s.reshape(n_s, D_ATTN)

    cw_pad = jnp.concatenate([conv_w[0], jnp.zeros((PAD_ROWS - CONV_WIDTH, D_CONV), F32)], 0)
    cb = conv_b[0].reshape(1, D_CONV)
    dw_p, u_tail = _conv_prompt(hb_p, cw_pad, cb)
    dw_s, state_new = _conv_sample(hb_s[:, C_GV:C_GV + D_CONV].astype(F32),
                                   hb_s[:, C_GG:C_GG + D_CONV].astype(F32),
                                   state_conv[0].transpose(1, 0, 2), cw_pad, cb)

    wua = w_up_attn[0].astype(BF16)
    wpw = w_pw[0].astype(BF16)
    wuc = w_up_conv[0].astype(BF16)
    wo = w_out[0].astype(BF16)
    ng = conv_norm_g[0].reshape(1, D_CONV)
    nb = conv_norm_b[0].reshape(1, D_CONV)
    bpw = b_pw[0].reshape(1, D_CONV)
    fg = final_g.reshape(1, D_MODEL)
    weights = (wua, wpw, wuc, wo, ng, nb, bpw, fg)
    y_p = _tail(hb_p, attn_p, dw_p.reshape(BATCH * SEQ, D_CONV), xp, weights, TAIL_TM)
    y_s = _tail(hb_s, attn_s, dw_s, xs, weights, TAIL_TM)

    tail_rows = CONV_WIDTH - 1
    return (
        y_p.reshape(BATCH, SEQ, D_MODEL),
        y_s.reshape(DEC_BATCH, DEC_SEQ, D_MODEL),
        k_p.reshape(1, BATCH, SEQ, N_KV_HEADS, HEAD_DIM),
        v_p.reshape(1, BATCH, SEQ, N_KV_HEADS, HEAD_DIM),
        hf_p[:, F_KI:F_KI + IDX_DIM].reshape(1, BATCH, SEQ, IDX_DIM),
        u_tail[:, PAD_ROWS - tail_rows:].reshape(1, BATCH, tail_rows, D_CONV),
        k_s.reshape(1, DEC_BATCH, DEC_SEQ, N_KV_HEADS, HEAD_DIM),
        v_s.reshape(1, DEC_BATCH, DEC_SEQ, N_KV_HEADS, HEAD_DIM),
        hf_s[:, F_KI:F_KI + IDX_DIM].reshape(1, DEC_BATCH, DEC_SEQ, IDX_DIM),
        state_new.transpose(1, 0, 2).reshape(1, DEC_BATCH, tail_rows, D_CONV),
    )
```

```python
import functools
import math

import numpy as np
import jax
import jax.numpy as jnp
from jax import lax
from jax.experimental import pallas as pl
from jax.experimental.pallas import tpu as pltpu

F32 = jnp.float32
BF16 = jnp.bfloat16

D_MODEL = 2048
BATCH = 8
SEQ = 2048
DEC_BATCH = 128
DEC_SEQ = 8
PAST_LEN = 2048
PAGE_SIZE = 128
N_PAGES = PAST_LEN // PAGE_SIZE
N_HEADS = 8
N_KV_HEADS = 2
HEAD_DIM = 128
GROUP = N_HEADS // N_KV_HEADS
D_ATTN = N_HEADS * HEAD_DIM
D_KV = N_KV_HEADS * HEAD_DIM
IDX_HEADS = 8
IDX_DIM = 64
TOPK = 256
Q_BLOCK = 128
N_BUCKETS = 32
MAX_DISTANCE = 128
D_CONV = D_MODEL // 2
CONV_WIDTH = 31
EPS = 1e-6
NEG = -1e30
NEG_INF = float("-inf")

LANE = 128
SUBLANE = 8
VMEM_LIMIT = 56 * 1024 * 1024

C_GA = 0
C_GC = C_GA + D_MODEL
C_Q = C_GC + D_MODEL
C_ZA = C_Q + D_ATTN
C_GV = C_ZA + D_ATTN
C_GG = C_GV + D_CONV
C_ZC = C_GG + D_CONV
C_K = C_ZC + D_CONV
C_V = C_K + D_KV
C_QI = C_V + D_KV
C_KI = C_QI + IDX_HEADS * IDX_DIM
W_OFF = IDX_DIM
MXU_WIDTH = 256
PROJ_TM = 1024
PROJ_TN = 6 * MXU_WIDTH
D_H = -(-(C_KI + LANE) // PROJ_TN) * PROJ_TN
KV_TILE = C_K // PROJ_TN
assert C_K % PROJ_TN == 0 and D_H == C_K + PROJ_TN
D_HB = C_K
D_HF = C_KI + LANE - C_K
F_K = C_K - D_HB
F_V = C_V - D_HB
F_QI = C_QI - D_HB
F_KI = C_KI - D_HB

_SRC = {}
_off = 0
for _name, _w in (('q', D_ATTN), ('k', D_KV), ('v', D_KV), ('z_attn', D_ATTN),
                  ('q_idx', IDX_HEADS * IDX_DIM), ('k_idx', IDX_DIM), ('w_idx', IDX_HEADS),
                  ('glu_val', D_CONV), ('glu_gate', D_CONV), ('z_conv', D_CONV),
                  ('gate_attn', D_MODEL), ('gate_conv', D_MODEL)):
    _SRC[_name] = (_off, _w)
    _off += _w
D_IN = _off
REORDER_ROWS = 512

N_BISECT = 20
PAD_ROWS = 32
CONV_CHUNK = 64
SCORE_SCALE = (IDX_DIM ** -0.5) * (IDX_HEADS ** -0.5)
QK_SCALE = HEAD_DIM ** -0.5
LOG2E = math.log2(math.e)
L_SAMPLE = (N_PAGES + 1) * LANE
SUBTILES = LANE // SUBLANE
CHUNK_BLOCKS = 4
CHUNK = CHUNK_BLOCKS * Q_BLOCK
PAD_KEYS = CHUNK - Q_BLOCK
VT_ROWS = HEAD_DIM + 2 * SUBLANE
SCORE_ROWS = 4
ATTN_ROWS = 4
TAIL_TM = 256


def _t5_bucket_static(dist):
    n = np.maximum(dist, 0)
    max_exact = N_BUCKETS // 2
    ratio = (np.log(np.maximum(n, 1).astype(np.float32) / np.float32(max_exact))
             / np.float32(math.log(MAX_DISTANCE / max_exact)))
    large = np.minimum(max_exact + (ratio * np.float32(N_BUCKETS - max_exact)).astype(np.int32),
                       N_BUCKETS - 1)
    return np.where(n < max_exact, n, large).astype(np.int32)


FAR_BUCKET = int(_t5_bucket_static(np.array([2 * MAX_DISTANCE]))[0])


def _params(sem):
    return pltpu.CompilerParams(dimension_semantics=sem, vmem_limit_bytes=VMEM_LIMIT)


def _proj_kernel(x_ref, g_ref, w_ref, ob_ref, of_ref, k_ref, v_ref, xn_ref):
    tm = x_ref.shape[0]
    j = pl.program_id(1)

    @pl.when(j == 0)
    def _():
        x = x_ref[...]
        ms = jnp.mean(x * x, axis=-1, keepdims=True)
        xn_ref[...] = (x * lax.rsqrt(ms + EPS) * g_ref[...]).astype(BF16)

    def project():
        return lax.dot_general(xn_ref[...], w_ref[...], (((1,), (1,)), ((), ())),
                               preferred_element_type=F32)

    @pl.when(j < KV_TILE)
    def _():
        ob_ref[...] = project().astype(BF16)

    @pl.when(j == KV_TILE)
    def _():
        of_ref[...] = project()[:, :D_HF]
        for g in range(N_KV_HEADS):
            rows = pl.ds(g, tm, stride=N_KV_HEADS)
            k_ref[rows, :] = of_ref[:, F_K + g * HEAD_DIM:F_K + (g + 1) * HEAD_DIM]
            v_ref[rows, :] = of_ref[:, F_V + g * HEAD_DIM:F_V + (g + 1) * HEAD_DIM]


def _proj(x2d, g, w):
    n = x2d.shape[0]
    tm, tn = PROJ_TM, PROJ_TN
    kv_spec = pl.BlockSpec((N_KV_HEADS * tm, HEAD_DIM), lambda i, j: (i, 0))
    kv_shape = jax.ShapeDtypeStruct((N_KV_HEADS * n, HEAD_DIM), F32)
    return pl.pallas_call(
        _proj_kernel,
        grid=(n // tm, D_H // tn),
        in_specs=[pl.BlockSpec((tm, D_MODEL), lambda i, j: (i, 0)),
                  pl.BlockSpec((1, D_MODEL), lambda i, j: (0, 0)),
                  pl.BlockSpec((tn, D_MODEL), lambda i, j: (j, 0))],
        out_specs=[pl.BlockSpec((tm, tn), lambda i, j: (i, jnp.minimum(j, KV_TILE - 1))),
                   pl.BlockSpec((tm, D_HF), lambda i, j: (i, 0)), kv_spec, kv_spec],
        out_shape=[jax.ShapeDtypeStruct((n, D_HB), BF16),
                   jax.ShapeDtypeStruct((n, D_HF), F32), kv_shape, kv_shape],
        scratch_shapes=[pltpu.VMEM((tm, D_MODEL), BF16)],
        compiler_params=_params(("parallel", "arbitrary")),
        name="proj",
    )(x2d, g, w)


def _bias_kernel(rb_ref, bp_ref, bs_ref, op_ref, os_ref):
    bp = bp_ref[...]
    bs = bs_ref[...]
    for h in range(N_HEADS):
        far = rb_ref[FAR_BUCKET, h]
        tp = jnp.zeros(bp.shape, F32)
        ts = jnp.zeros(bs.shape, F32)
        for b in range(N_BUCKETS):
            val = rb_ref[b, h] - far
            tp = jnp.where(bp == b, val * LOG2E, tp)
            ts = jnp.where(bs == b, val, ts)
        op_ref[h] = tp
        os_ref[h] = ts


def _bias_tables(rel_bias):
    key = np.arange(CHUNK)[:, None]
    qry = np.arange(Q_BLOCK)[None, :]
    bucket_p = _t5_bucket_static(CHUNK - Q_BLOCK + qry - key)
    qi = np.arange(DEC_SEQ)[:, None]
    col = np.arange(2 * LANE)[None, :]
    dist_s = np.where(col < LANE, LANE + qi - col, qi - (col - LANE))
    bucket_s = _t5_bucket_static(dist_s)
    return pl.pallas_call(
        _bias_kernel,
        in_specs=[pl.BlockSpec(memory_space=pltpu.SMEM),
                  pl.BlockSpec(memory_space=pltpu.VMEM),
                  pl.BlockSpec(memory_space=pltpu.VMEM)],
        out_specs=[pl.BlockSpec(memory_space=pltpu.VMEM),
                   pl.BlockSpec(memory_space=pltpu.VMEM)],
        out_shape=[jax.ShapeDtypeStruct((N_HEADS, CHUNK, Q_BLOCK), F32),
                   jax.ShapeDtypeStruct((N_HEADS, DEC_SEQ, 2 * LANE), F32)],
        name="bias_tables",
    )(rel_bias, jnp.asarray(bucket_p), jnp.asarray(bucket_s))


def _any(x):
    return jnp.max(jnp.where(x, 1.0, 0.0)) > 0.5


def _rep(x):
    return jnp.broadcast_to(x, (SUBLANE, LANE))


def _fold_rows(x, comb):
    parts = [x[k:k + SUBLANE] for k in range(0, x.shape[0], SUBLANE)]
    while len(parts) > 1:
        parts = [comb(parts[k], parts[k + 1]) for k in range(0, len(parts), 2)]
    return parts[0]


def _select_threshold(tile_fn, ntiles, static):
    def reduce_tiles(fn, init, comb):
        def step(j, acc):
            x = fn(tile_fn(j), j)
            parts = [x[k] for k in range(x.shape[0])]
            while len(parts) > 1:
                parts = [comb(parts[k], parts[k + 1]) for k in range(0, len(parts), 2)]
            return comb(acc, parts[0])
        if static:
            acc = init
            for j in range(ntiles):
                acc = step(j, acc)
            return acc
        return lax.fori_loop(0, ntiles, step, init)

    zeros = jnp.zeros((SUBLANE, LANE), F32)

    def count(pred_fn):
        acc = reduce_tiles(lambda s, j: jnp.where(pred_fn(s, j), 1.0, 0.0), zeros,
                           lambda a, b: a + b)
        return _rep(jnp.sum(acc, axis=0, keepdims=True))

    def masked_max(pred_fn):
        acc = reduce_tiles(lambda s, j: jnp.where(pred_fn(s, j), s, NEG_INF),
                           jnp.full((SUBLANE, LANE), NEG_INF, F32), jnp.maximum)
        return _rep(jnp.max(acc, axis=0, keepdims=True))

    bound = reduce_tiles(lambda s, j: jnp.where(s > NEG_INF, jnp.abs(s), 0.0), zeros,
                         jnp.maximum)
    bound = _rep(jnp.max(bound, axis=0, keepdims=True))

    def bisect(_, carry):
        lo, hi = carry
        mid = 0.5 * lo + 0.5 * hi
        few = count(lambda s, j: s > mid[None]) < TOPK
        return jnp.where(few, lo, mid), jnp.where(few, mid, hi)

    _, hi = lax.fori_loop(0, N_BISECT, bisect, (-bound, bound))

    thr = masked_max(lambda s, j: s <= hi[None])
    n_ge = count(lambda s, j: s >= thr[None])

    def fix_body(carry):
        thr, n_ge, _ = carry
        lower = masked_max(lambda s, j: s < thr[None])
        thr = jnp.where(n_ge < TOPK, lower, thr)
        n_ge = count(lambda s, j: s >= thr[None])
        return thr, n_ge, _any(n_ge < TOPK)

    thr, n_ge, _ = lax.while_loop(lambda c: c[2], fix_body, (thr, n_ge, _any(n_ge < TOPK)))
    n_gt = count(lambda s, j: s > thr[None])
    need = TOPK - n_gt
    return thr[0:1], need[0:1]


def _keep(pred):
    return jnp.where(pred, 0.0, NEG)


def _selection_masks(tiles, thr, need, seen, tri):
    ties = [s == thr for s in tiles]
    ranks = []
    for k in range(0, len(tiles), 2):
        pair = jnp.concatenate([jnp.where(t, 1.0, 0.0).astype(BF16) for t in ties[k:k + 2]], axis=1)
        rank = jnp.dot(tri, pair, preferred_element_type=F32)
        ranks += [rank[:, p * LANE:(p + 1) * LANE] for p in range(len(ties[k:k + 2]))]
    masks = []
    for s, tie, rank in zip(tiles, ties, ranks):
        masks.append(jnp.where(tie, _keep(rank + seen <= need), _keep(s > thr)))
        seen = seen + rank[s.shape[0] - 1:]
    return masks, seen


def _attn_p_kernel(q_ref, qi_ref, wi_ref, k_ref, v_ref, ki_ref, bias_ref, tri_ref, o_ref,
                   kb_ref, vt_ref, kib_ref, qh_ref, qih_ref,
                   score_ref, mask_ref, s_ref, m_ref, acc_ref):
    i = pl.program_id(1)
    T = Q_BLOCK
    W = GROUP * T
    nch = i // CHUNK_BLOCKS + 1

    @pl.when(i == 0)
    def _():
        kb_ref[0:PAD_KEYS] = jnp.zeros((PAD_KEYS, D_KV), BF16)
        kb_ref[PAD_KEYS:] = k_ref[0].astype(BF16)
        kib_ref[0:PAD_KEYS] = jnp.zeros((PAD_KEYS, IDX_DIM), BF16)
        kib_ref[PAD_KEYS:] = ki_ref[0][:, :IDX_DIM].astype(BF16)
        ones_row = lax.broadcasted_iota(jnp.int32, (VT_ROWS - HEAD_DIM, PAD_KEYS + SEQ), 0) == 0
        for g in range(N_KV_HEADS):
            vt_ref[g, 0:HEAD_DIM, 0:PAD_KEYS] = jnp.zeros((HEAD_DIM, PAD_KEYS), BF16)
            vt_ref[g, HEAD_DIM:VT_ROWS, :] = jnp.where(ones_row, 1.0, 0.0).astype(BF16)
            for c in range(SEQ // LANE):
                blk = v_ref[0, c * LANE:(c + 1) * LANE, g * HEAD_DIM:(g + 1) * HEAD_DIM]
                vt_ref[g, 0:HEAD_DIM,
                       PAD_KEYS + c * LANE:PAD_KEYS + (c + 1) * LANE] = blk.T.astype(BF16)
        score_ref[0:PAD_KEYS] = jnp.full((PAD_KEYS, T), NEG_INF, F32)
        mask_ref[0:PAD_KEYS] = jnp.full((PAD_KEYS, T), NEG, F32)

    q = q_ref[0].astype(F32) * (QK_SCALE * LOG2E)
    for h in range(N_HEADS):
        qh_ref[h] = q[:, h * HEAD_DIM:(h + 1) * HEAD_DIM].astype(BF16)
    qi = qi_ref[0]
    for h in range(IDX_HEADS):
        qih_ref[h] = qi[:, h * IDX_DIM:(h + 1) * IDX_DIM].astype(BF16)
    w_rows = wi_ref[0].T[W_OFF:W_OFF + IDX_HEADS] * SCORE_SCALE

    def span(c):
        return pl.ds(pl.multiple_of((i - CHUNK_BLOCKS * c) * LANE, LANE), CHUNK)

    def first_key(c):
        return (i - CHUNK_BLOCKS * c) * LANE - PAD_KEYS

    key_l = lax.broadcasted_iota(jnp.int32, (CHUNK, T), 0)
    qry = i * T + lax.broadcasted_iota(jnp.int32, (CHUNK, T), 1)

    def score_body(c, _):
        kc = kib_ref[span(c), :]
        d = lax.dot_general(kc, qih_ref[...].reshape(IDX_HEADS * T, IDX_DIM),
                            (((1,), (1,)), ((), ())), preferred_element_type=F32)
        acc = jnp.zeros((CHUNK, T), F32)
        for h in range(IDX_HEADS):
            acc = acc + jnp.maximum(d[:, h * T:(h + 1) * T], 0.0) * w_rows[h:h + 1]
        key = first_key(c) + key_l
        acc = jnp.where(key <= qry, jnp.where(key >= 0, acc, NEG_INF), NEG_INF)
        score_ref[span(c), :] = acc
        return 0

    lax.fori_loop(0, nch, score_body, 0)

    nsub = CHUNK // SUBLANE

    def tile_fn(c):
        return score_ref[span(c), :].reshape(nsub, SUBLANE, T)

    @pl.when(i * T + T <= TOPK)
    def _():
        def body(c, _):
            mask_ref[span(c), :] = _keep(score_ref[span(c), :] > NEG_INF)
            return 0
        lax.fori_loop(0, nch, body, 0)

    def select(n):
        thr, need = _select_threshold(tile_fn, n, static=True)
        seen = jnp.zeros((1, T), F32)
        for c in reversed(range(n)):
            start = pl.multiple_of((i - CHUNK_BLOCKS * c) * LANE, LANE)
            parts = [pl.ds(start + p * MXU_WIDTH, MXU_WIDTH) for p in range(CHUNK // MXU_WIDTH)]
            masks, seen = _selection_masks([score_ref[rows, :] for rows in parts],
                                           thr, need, seen, tri_ref[...])
            for rows, mask in zip(parts, masks):
                mask_ref[rows, :] = mask

    for n in range(1, SEQ // CHUNK + 1):
        pl.when(jnp.logical_and(i * T + T > TOPK, nch == n))(functools.partial(select, n))

    m_ref[...] = jnp.full(m_ref.shape, NEG, F32)
    acc_ref[...] = jnp.zeros(acc_ref.shape, F32)

    def logits(c, with_bias):
        mb = mask_ref[span(c), :]
        for g in range(N_KV_HEADS):
            kc = kb_ref[span(c), g * HEAD_DIM:(g + 1) * HEAD_DIM]
            qg = qh_ref[g * GROUP:(g + 1) * GROUP].reshape(W, HEAD_DIM)
            s = lax.dot_general(kc, qg, (((1,), (1,)), ((), ())), preferred_element_type=F32)
            if with_bias:
                add = jnp.concatenate([mb + bias_ref[g * GROUP + hq] for hq in range(GROUP)],
                                      axis=1)
            else:
                add = jnp.concatenate([mb] * GROUP, axis=1)
            s = s + add
            s_ref[span(c), g * W:(g + 1) * W] = s
            top = jnp.max(_fold_rows(s, jnp.maximum), axis=0, keepdims=True)
            m_ref[g] = jnp.maximum(m_ref[g], top)

    logits(0, True)

    def logits_body(c, _):
        logits(c, False)
        return 0

    lax.fori_loop(1, nch, logits_body, 0)

    def weigh(c, _):
        ps = [jnp.exp2((s_ref[span(c), g * W:(g + 1) * W] - m_ref[g]).astype(BF16))
              for g in range(N_KV_HEADS)]
        pv = [jnp.dot(vt_ref[g, :, span(c)], ps[g], preferred_element_type=F32)
              for g in range(N_KV_HEADS)]
        for g in range(N_KV_HEADS):
            acc_ref[g] += pv[g]
        return 0

    lax.fori_loop(0, nch, weigh, 0)

    for g in range(N_KV_HEADS):
        o = acc_ref[g, 0:HEAD_DIM] / acc_ref[g, HEAD_DIM:HEAD_DIM + 1]
        for hq in range(GROUP):
            h = g * GROUP + hq
            o_ref[0, :, h * HEAD_DIM:(h + 1) * HEAD_DIM] = o[:, hq * T:(hq + 1) * T].T


def _attn_prompt(hb_p, hf_p, bias_p):
    hb3 = hb_p.reshape(BATCH, SEQ, D_HB)
    hf3 = hf_p.reshape(BATCH, SEQ, D_HF)
    nqb = SEQ // Q_BLOCK
    T = Q_BLOCK
    qi_w = IDX_HEADS * IDX_DIM
    return pl.pallas_call(
        _attn_p_kernel,
        grid=(BATCH, nqb),
        in_specs=[
            pl.BlockSpec((1, T, D_ATTN), lambda b, i: (b, i, C_Q // D_ATTN)),
            pl.BlockSpec((1, T, qi_w), lambda b, i: (b, i, F_QI // qi_w)),
            pl.BlockSpec((1, T, LANE), lambda b, i: (b, i, F_KI // LANE)),
            pl.BlockSpec((1, SEQ, D_KV), lambda b, i: (b, 0, F_K // D_KV)),
            pl.BlockSpec((1, SEQ, D_KV), lambda b, i: (b, 0, F_V // D_KV)),
            pl.BlockSpec((1, SEQ, LANE), lambda b, i: (b, 0, F_KI // LANE)),
            pl.BlockSpec((N_HEADS, CHUNK, T), lambda b, i: (0, 0, 0)),
            pl.BlockSpec((MXU_WIDTH, MXU_WIDTH), lambda b, i: (0, 0)),
        ],
        out_specs=pl.BlockSpec((1, T, D_ATTN), lambda b, i: (b, i, 0)),
        out_shape=jax.ShapeDtypeStruct((BATCH, SEQ, D_ATTN), F32),
        scratch_shapes=[
            pltpu.VMEM((PAD_KEYS + SEQ, D_KV), BF16),
            pltpu.VMEM((N_KV_HEADS, VT_ROWS, PAD_KEYS + SEQ), BF16),
            pltpu.VMEM((PAD_KEYS + SEQ, IDX_DIM), BF16),
            pltpu.VMEM((N_HEADS, T, HEAD_DIM), BF16),
            pltpu.VMEM((IDX_HEADS, T, IDX_DIM), BF16),
            pltpu.VMEM((PAD_KEYS + SEQ, T), F32),
            pltpu.VMEM((PAD_KEYS + SEQ, T), F32),
            pltpu.VMEM((PAD_KEYS + SEQ, N_HEADS * T), F32),
            pltpu.VMEM((N_KV_HEADS, 1, GROUP * T), F32),
            pltpu.VMEM((N_KV_HEADS, VT_ROWS, GROUP * T), F32),
        ],
        compiler_params=_params(("parallel", "arbitrary")),
        name="attn_prompt",
    )(hb3, hf3, hf3, hf3, hf3, hf3, bias_p, jnp.tri(MXU_WIDTH, dtype=BF16))


def _score_s_kernel(pt_ref, qi_ref, w_ref, kin_ref, *rest):
    npg = SCORE_ROWS * N_PAGES
    kip = rest[0:npg]
    o_ref = rest[npg]
    kinp_ref = rest[npg + 1]
    del pt_ref
    R = DEC_SEQ
    qrow = lax.broadcasted_iota(jnp.int32, (R, LANE), 0)
    lane = lax.broadcasted_iota(jnp.int32, (R, LANE), 1)
    kinp_ref[...] = jnp.zeros(kinp_ref.shape, BF16)
    for r in range(SCORE_ROWS):
        kinp_ref[r, 0:2 * R] = jnp.concatenate(
            [kin_ref[r][:, :IDX_DIM], jnp.zeros((R, IDX_DIM), F32)], 0).astype(BF16)
    for r in range(SCORE_ROWS):
        qi = qi_ref[r].astype(BF16)
        wb = jnp.broadcast_to(w_ref[r] * SCORE_SCALE, (IDX_HEADS * R, LANE))
        for t in range(N_PAGES + 1):
            if t < N_PAGES:
                d = jnp.dot(qi, kip[r * N_PAGES + t][0].astype(BF16), preferred_element_type=F32)
            else:
                d = lax.dot_general(qi, kinp_ref[r], (((1,), (1,)), ((), ())),
                                    preferred_element_type=F32)
            e = (jnp.maximum(d, 0.0) * wb).reshape(IDX_HEADS, R, LANE)
            s = e[0]
            for h in range(1, IDX_HEADS):
                s = s + e[h]
            if t == N_PAGES:
                s = jnp.where(lane <= qrow, s, NEG_INF)
            o_ref[r, :, t * LANE:(t + 1) * LANE] = s


def _score_sample(hf_s, page_table, cache_kidx_t):
    R = DEC_SEQ
    G = SCORE_ROWS
    h3 = hf_s.reshape(DEC_BATCH, R, D_HF)
    qi_hq = h3[:, :, F_QI:F_QI + IDX_HEADS * IDX_DIM].reshape(DEC_BATCH, R, IDX_HEADS, IDX_DIM)
    qi_hq = qi_hq.transpose(0, 2, 1, 3).reshape(DEC_BATCH, IDX_HEADS * R, IDX_DIM)
    w_hq = h3[:, :, F_KI + W_OFF:F_KI + W_OFF + IDX_HEADS].transpose(0, 2, 1)
    w_hq = w_hq.reshape(DEC_BATCH, IDX_HEADS * R, 1)
    in_specs = [
        pl.BlockSpec((G, IDX_HEADS * R, IDX_DIM), lambda b, pt: (b, 0, 0)),
        pl.BlockSpec((G, IDX_HEADS * R, 1), lambda b, pt: (b, 0, 0)),
        pl.BlockSpec((G, R, LANE), lambda b, pt: (b, 0, F_KI // LANE)),
    ]
    in_specs += [pl.BlockSpec((1, IDX_DIM, PAGE_SIZE),
                              lambda b, pt, r=r, p=p: (pt[b * G + r, p], 0, 0))
                 for r in range(G) for p in range(N_PAGES)]
    grid_spec = pltpu.PrefetchScalarGridSpec(
        num_scalar_prefetch=1,
        grid=(DEC_BATCH // G,),
        in_specs=in_specs,
        out_specs=pl.BlockSpec((G, R, L_SAMPLE), lambda b, pt: (b, 0, 0)),
        scratch_shapes=[pltpu.VMEM((G, PAGE_SIZE, IDX_DIM), BF16)],
    )
    return pl.pallas_call(
        _score_s_kernel,
        grid_spec=grid_spec,
        out_shape=jax.ShapeDtypeStruct((DEC_BATCH, R, L_SAMPLE), F32),
        compiler_params=_params(("arbitrary",)),
        name="score_sample",
    )(page_table, qi_hq, w_hq, h3, *([cache_kidx_t] * (G * N_PAGES)))


def _select_s_kernel(s_ref, tri_ref, o_ref):
    nt = N_PAGES + 1

    def tile_fn(j):
        return s_ref[j * LANE:(j + 1) * LANE, :].reshape(SUBTILES, SUBLANE, LANE)

    thr, need = _select_threshold(tile_fn, nt, static=True)
    parts = [slice(j * LANE, (j + 1) * LANE) for j in range(nt)]
    masks, _ = _selection_masks([s_ref[rows, :] for rows in parts], thr, need,
                                jnp.zeros((1, LANE), F32), tri_ref[...])
    for rows, mask in zip(parts, masks):
        o_ref[rows, :] = mask


def _select_sample(scores_t):
    n = scores_t.shape[1]
    return pl.pallas_call(
        _select_s_kernel,
        grid=(n // LANE,),
        in_specs=[pl.BlockSpec((L_SAMPLE, LANE), lambda c: (0, c)),
                  pl.BlockSpec((LANE, LANE), lambda c: (0, 0))],
        out_specs=pl.BlockSpec((L_SAMPLE, LANE), lambda c: (0, c)),
        out_shape=jax.ShapeDtypeStruct((L_SAMPLE, n), F32),
        compiler_params=_params(("parallel",)),
        name="select_sample",
    )(scores_t, jnp.tri(LANE, dtype=BF16))


def _attn_s_kernel(pt_ref, q_ref, kn_ref, vn_ref, mask_ref, bias_ref, *rest):
    npg = ATTN_ROWS * N_PAGES
    kp = rest[0:npg]
    vp = rest[npg:2 * npg]
    o_ref = rest[2 * npg]
    knp_ref, vnp_ref, logit_ref = rest[2 * npg + 1:]
    del pt_ref
    R = DEC_SEQ
    NT = N_PAGES + 1
    GR = GROUP * R

    knp_ref[...] = jnp.zeros(knp_ref.shape, BF16)
    vnp_ref[...] = jnp.zeros(vnp_ref.shape, BF16)
    for r in range(ATTN_ROWS):
        knp_ref[r, 0:2 * R] = jnp.concatenate([kn_ref[r], jnp.zeros((R, D_KV), F32)], 0).astype(BF16)
        vnp_ref[r, 0:2 * R] = jnp.concatenate([vn_ref[r], jnp.zeros((R, D_KV), F32)], 0).astype(BF16)

    def page_head(refs, pad_ref, r, t, g):
        if t < N_PAGES:
            return refs[r * N_PAGES + t][pl.ds(g, PAGE_SIZE, stride=N_KV_HEADS), :].astype(BF16)
        return pad_ref[r, :, g * HEAD_DIM:(g + 1) * HEAD_DIM]

    for r in range(ATTN_ROWS):
        q = (q_ref[r].astype(F32) * QK_SCALE).astype(BF16)
        for t in range(NT):
            mb = mask_ref[r, :, t * LANE:(t + 1) * LANE]
            for g in range(N_KV_HEADS):
                lg = lax.dot_general(q[g * GR:(g + 1) * GR], page_head(kp, knp_ref, r, t, g),
                                     (((1,), (1,)), ((), ())), preferred_element_type=F32)
                lg = lg.reshape(GROUP, R, LANE) + mb[None]
                if t >= N_PAGES - 1:
                    off = (t - (N_PAGES - 1)) * LANE
                    lg = lg + bias_ref[g * GROUP:(g + 1) * GROUP, :, off:off + LANE]
                logit_ref[r, g * GR:(g + 1) * GR, t * LANE:(t + 1) * LANE] = lg.reshape(GR, LANE)

    for r in range(ATTN_ROWS):
        logits = logit_ref[r]
        m = jnp.max(logits, axis=1, keepdims=True)
        p = jnp.exp(logits - m)
        inv = 1.0 / jnp.sum(p, axis=1, keepdims=True)
        pb = p.astype(BF16)
        outs = [jnp.zeros((GR, HEAD_DIM), F32) for _ in range(N_KV_HEADS)]
        for t in range(NT):
            for g in range(N_KV_HEADS):
                outs[g] = outs[g] + jnp.dot(pb[g * GR:(g + 1) * GR, t * LANE:(t + 1) * LANE],
                                            page_head(vp, vnp_ref, r, t, g),
                                            preferred_element_type=F32)
        for g in range(N_KV_HEADS):
            o = outs[g] * inv[g * GR:(g + 1) * GR]
            for hq in range(GROUP):
                h = g * GROUP + hq
                o_ref[r, :, h * HEAD_DIM:(h + 1) * HEAD_DIM] = o[hq * R:(hq + 1) * R]


def _attn_sample(hb_s, hf_s, page_table, cache_k, cache_v, mask, bias_s):
    R = DEC_SEQ
    G = ATTN_ROWS
    h3 = hf_s.reshape(DEC_BATCH, R, D_HF)
    q_hq = hb_s[:, C_Q:C_Q + D_ATTN].reshape(DEC_BATCH, R, N_HEADS, HEAD_DIM)
    q_hq = q_hq.transpose(0, 2, 1, 3).reshape(DEC_BATCH, N_HEADS * R, HEAD_DIM)
    rows_per_page = PAGE_SIZE * N_KV_HEADS
    ck = cache_k.reshape(-1, HEAD_DIM)
    cv = cache_v.reshape(-1, HEAD_DIM)

    in_specs = [
        pl.BlockSpec((G, N_HEADS * R, HEAD_DIM), lambda b, pt: (b, 0, 0)),
        pl.BlockSpec((G, R, D_KV), lambda b, pt: (b, 0, F_K // D_KV)),
        pl.BlockSpec((G, R, D_KV), lambda b, pt: (b, 0, F_V // D_KV)),
        pl.BlockSpec((G, R, L_SAMPLE), lambda b, pt: (b, 0, 0)),
        pl.BlockSpec((N_HEADS, R, 2 * LANE), lambda b, pt: (0, 0, 0)),
    ]
    pages = [pl.BlockSpec((rows_per_page, HEAD_DIM), lambda b, pt, r=r, p=p: (pt[b * G + r, p], 0))
             for r in range(G) for p in range(N_PAGES)]
    in_specs += pages + pages
    grid_spec = pltpu.PrefetchScalarGridSpec(
        num_scalar_prefetch=1,
        grid=(DEC_BATCH // G,),
        in_specs=in_specs,
        out_specs=pl.BlockSpec((G, R, D_ATTN), lambda b, pt: (b, 0, 0)),
        scratch_shapes=[
            pltpu.VMEM((G, PAGE_SIZE, D_KV), BF16),
            pltpu.VMEM((G, PAGE_SIZE, D_KV), BF16),
            pltpu.VMEM((G, N_HEADS * R, L_SAMPLE), F32),
        ],
    )
    return pl.pallas_call(
        _attn_s_kernel,
        grid_spec=grid_spec,
        out_shape=jax.ShapeDtypeStruct((DEC_BATCH, R, D_ATTN), F32),
        compiler_params=_params(("arbitrary",)),
        name="attn_sample",
    )(page_table, q_hq, h3, h3, mask, bias_s, *([ck] * (G * N_PAGES)), *([cv] * (G * N_PAGES)))


def _conv_p_kernel(val_ref, gate_ref, cw_ref, cb_ref, dw_ref, ut_ref, pad_ref):
    pad_ref[0:PAD_ROWS] = jnp.zeros((PAD_ROWS, LANE), F32)
    pad_ref[PAD_ROWS:] = val_ref[0].astype(F32) * jax.nn.sigmoid(gate_ref[0].astype(F32))
    ut_ref[0] = pad_ref[SEQ:SEQ + PAD_ROWS]
    cw = cw_ref[...]
    cb = cb_ref[...]
    first = PAD_ROWS - (CONV_WIDTH - 1)
    for c in range(SEQ // CONV_CHUNK):
        base = c * CONV_CHUNK
        acc = jnp.broadcast_to(cb, (CONV_CHUNK, LANE))
        for r in range(SUBLANE):
            taps = [w for w in range(CONV_WIDTH) if (first + w) % SUBLANE == r]
            span = max(first + w - r for w in taps) + CONV_CHUNK
            win = pad_ref[base + r:base + r + span]
            for w in taps:
                a = first + w - r
                acc = acc + win[a:a + CONV_CHUNK] * cw[w:w + 1]
        dw_ref[0, base:base + CONV_CHUNK] = acc


def _conv_prompt(hb_p, cw_pad, cb):
    h3 = hb_p.reshape(BATCH, SEQ, D_HB)
    nc = D_CONV // LANE
    return pl.pallas_call(
        _conv_p_kernel,
        grid=(BATCH, nc),
        in_specs=[pl.BlockSpec((1, SEQ, LANE), lambda b, c: (b, 0, C_GV // LANE + c)),
                  pl.BlockSpec((1, SEQ, LANE), lambda b, c: (b, 0, C_GG // LANE + c)),
                  pl.BlockSpec((PAD_ROWS, LANE), lambda b, c: (0, c)),
                  pl.BlockSpec((1, LANE), lambda b, c: (0, c))],
        out_specs=[pl.BlockSpec((1, SEQ, LANE), lambda b, c: (b, 0, c)),
                   pl.BlockSpec((1, PAD_ROWS, LANE), lambda b, c: (b, 0, c))],
        out_shape=[jax.ShapeDtypeStruct((BATCH, SEQ, D_CONV), F32),
                   jax.ShapeDtypeStruct((BATCH, PAD_ROWS, D_CONV), F32)],
        scratch_shapes=[pltpu.VMEM((PAD_ROWS + SEQ, LANE), F32)],
        compiler_params=_params(("parallel", "parallel")),
        name="conv_prompt",
    )(h3, h3, cw_pad, cb)


def _conv_s_kernel(val_ref, gate_ref, st_ref, cw_ref, cb_ref, dw_ref, ns_ref):
    R = DEC_SEQ
    H = CONV_WIDTH - 1
    cw = cw_ref[...]
    cb = jnp.broadcast_to(cb_ref[...], (DEC_BATCH, LANE))
    u = []
    for q in range(R):
        rows = pl.ds(q, DEC_BATCH, stride=R)
        u.append(val_ref[rows, :] * jax.nn.sigmoid(gate_ref[rows, :]))

    def row(r):
        return st_ref[r] if r < H else u[r - H]

    for q in range(R):
        acc = cb
        for w in range(CONV_WIDTH):
            acc = acc + row(q + w) * cw[w:w + 1]
        dw_ref[pl.ds(q, DEC_BATCH, stride=R), :] = acc
    for r in range(H):
        ns_ref[r] = row(r + R)


def _conv_sample(val, gate, state_t, cw_pad, cb):
    n_s = DEC_BATCH * DEC_SEQ
    H = CONV_WIDTH - 1
    nc = D_CONV // LANE
    return pl.pallas_call(
        _conv_s_kernel,
        grid=(nc,),
        in_specs=[pl.BlockSpec((n_s, LANE), lambda c: (0, c)),
                  pl.BlockSpec((n_s, LANE), lambda c: (0, c)),
                  pl.BlockSpec((H, DEC_BATCH, LANE), lambda c: (0, 0, c)),
                  pl.BlockSpec((PAD_ROWS, LANE), lambda c: (0, c)),
                  pl.BlockSpec((1, LANE), lambda c: (0, c))],
        out_specs=[pl.BlockSpec((n_s, LANE), lambda c: (0, c)),
                   pl.BlockSpec((H, DEC_BATCH, LANE), lambda c: (0, 0, c))],
        out_shape=[jax.ShapeDtypeStruct((n_s, D_CONV), F32),
                   jax.ShapeDtypeStruct((H, DEC_BATCH, D_CONV), F32)],
        compiler_params=_params(("parallel",)),
        name="conv_sample",
    )(val, gate, state_t, cw_pad, cb)


def _tail_kernel(attn_ref, za_ref, dw_ref, zc_ref, ga_ref, gc_ref, x_ref,
                 wua_ref, wpw_ref, wuc_ref, wo_ref, ng_ref, nb_ref, bpw_ref, fg_ref, y_ref):
    a = attn_ref[...] * jax.nn.silu(za_ref[...].astype(F32))
    branch_attn = jnp.dot(a.astype(BF16), wua_ref[...], preferred_element_type=F32)

    dw = dw_ref[...]
    mu = jnp.mean(dw, axis=-1, keepdims=True)
    var = jnp.mean(jnp.square(dw - mu), axis=-1, keepdims=True)
    ln = (dw - mu) * lax.rsqrt(var + EPS) * ng_ref[...] + nb_ref[...]
    conv_out = jnp.dot(jax.nn.silu(ln).astype(BF16), wpw_ref[...],
                       preferred_element_type=F32) + bpw_ref[...]
    c = conv_out * jax.nn.silu(zc_ref[...].astype(F32))
    branch_conv = jnp.dot(c.astype(BF16), wuc_ref[...], preferred_element_type=F32)

    merged = (jax.nn.sigmoid(ga_ref[...].astype(F32)) * branch_attn
              + jax.nn.sigmoid(gc_ref[...].astype(F32)) * branch_conv)
    y = x_ref[...] + jnp.dot(merged.astype(BF16), wo_ref[...], preferred_element_type=F32)
    ms = jnp.mean(y * y, axis=-1, keepdims=True)
    y_ref[...] = y * lax.rsqrt(ms + EPS) * fg_ref[...]


def _tail(h, attn, dw, x2d, weights, tm):
    n = x2d.shape[0]

    def const(shape):
        return pl.BlockSpec(shape, lambda i: (0, 0), pipeline_mode=pl.Buffered(1))

    def cols(width, offset):
        return pl.BlockSpec((tm, width), lambda i: (i, offset // width))

    return pl.pallas_call(
        _tail_kernel,
        grid=(n // tm,),
        in_specs=[cols(D_ATTN, 0), cols(D_ATTN, C_ZA), cols(D_CONV, 0), cols(D_CONV, C_ZC),
                  cols(D_MODEL, C_GA), cols(D_MODEL, C_GC), cols(D_MODEL, 0),
                  const((D_ATTN, D_MODEL)), const((D_CONV, D_CONV)),
                  const((D_CONV, D_MODEL)), const((D_MODEL, D_MODEL)),
                  const((1, D_CONV)), const((1, D_CONV)), const((1, D_CONV)),
                  const((1, D_MODEL))],
        out_specs=pl.BlockSpec((tm, D_MODEL), lambda i: (i, 0)),
        out_shape=jax.ShapeDtypeStruct((n, D_MODEL), F32),
        compiler_params=_params(("parallel",)),
        name="tail",
    )(attn, h, dw, h, h, h, x2d, *weights)


def _reorder_kernel(off_ref, w_ref, o_ref):
    del off_ref
    o_ref[...] = w_ref[...].astype(BF16)


def _reorder_w_in(w_in):
    w_t = w_in.T
    src = []
    for first, dest, width in (('gate_attn', C_GA, 2 * D_MODEL), ('q', C_Q, D_ATTN),
                               ('z_attn', C_ZA, D_ATTN), ('glu_val', C_GV, 3 * D_CONV),
                               ('k', C_K, 2 * D_KV), ('q_idx', C_QI, D_H - C_QI)):
        assert dest == len(src) * REORDER_ROWS
        for r in range(0, width, REORDER_ROWS):
            src.append(min(_SRC[first][0] + r, D_IN - REORDER_ROWS) // SUBLANE)
    grid_spec = pltpu.PrefetchScalarGridSpec(
        num_scalar_prefetch=1,
        grid=(len(src),),
        in_specs=[pl.BlockSpec((pl.Element(REORDER_ROWS), pl.Element(D_MODEL)),
                               lambda d, off: (off[d] * SUBLANE, 0))],
        out_specs=pl.BlockSpec((REORDER_ROWS, D_MODEL), lambda d, off: (d, 0)),
    )
    return pl.pallas_call(
        _reorder_kernel,
        grid_spec=grid_spec,
        out_shape=jax.ShapeDtypeStruct((D_H, D_MODEL), BF16),
        compiler_params=_params(("arbitrary",)),
        name="reorder_w_in",
    )(jnp.asarray(src, jnp.int32), w_t)


def kernel(x_prompt, x_sample, cache_k, cache_v, cache_kidx, state_conv, page_table,
           ln_g, w_in, conv_w, conv_b, conv_norm_g, conv_norm_b, w_pw, b_pw,
           w_up_attn, w_up_conv, w_out, rel_bias, final_g):
    w_all = _reorder_w_in(w_in[0])
    g_in = ln_g[0].reshape(1, D_MODEL)
    xp = x_prompt.reshape(BATCH * SEQ, D_MODEL)
    xs = x_sample.reshape(DEC_BATCH * DEC_SEQ, D_MODEL)
    hb_p, hf_p, k_p, v_p = _proj(xp, g_in, w_all)
    hb_s, hf_s, k_s, v_s = _proj(xs, g_in, w_all)

    bias_p, bias_s = _bias_tables(rel_bias)

    attn_p = _attn_prompt(hb_p, hf_p, bias_p).reshape(BATCH * SEQ, D_ATTN)

    n_s = DEC_BATCH * DEC_SEQ
    scores = _score_sample(hf_s, page_table, cache_kidx[0].transpose(0, 2, 1))
    mask_t = _select_sample(scores.reshape(n_s, L_SAMPLE).T)
    mask = mask_t.T.reshape(DEC_BATCH, DEC_SEQ, L_SAMPLE)
    attn_s = _attn_sample(hb_s, hf_s, page_table, cache_k[0], cache_v[0], mask, bias_s)
    attn_s = attn_s.reshape(n_s, D_ATTN)

    cw_pad = jnp.concatenate([conv_w[0], jnp.zeros((PAD_ROWS - CONV_WIDTH, D_CONV), F32)], 0)
    cb = conv_b[0].reshape(1, D_CONV)
    dw_p, u_tail = _conv_prompt(hb_p, cw_pad, cb)
    dw_s, state_new = _conv_sample(hb_s[:, C_GV:C_GV + D_CONV].astype(F32),
                                   hb_s[:, C_GG:C_GG + D_CONV].astype(F32),
                                   state_conv[0].transpose(1, 0, 2), cw_pad, cb)

    wua = w_up_attn[0].astype(BF16)
    wpw = w_pw[0].astype(BF16)
    wuc = w_up_conv[0].astype(BF16)
    wo = w_out[0].astype(BF16)
    ng = conv_norm_g[0].reshape(1, D_CONV)
    nb = conv_norm_b[0].reshape(1, D_CONV)
    bpw = b_pw[0].reshape(1, D_CONV)
    fg = final_g.reshape(1, D_MODEL)
    weights = (wua, wpw, wuc, wo, ng, nb, bpw, fg)
    y_p = _tail(hb_p, attn_p, dw_p.reshape(BATCH * SEQ, D_CONV), xp, weights, TAIL_TM)
    y_s = _tail(hb_s, attn_s, dw_s, xs, weights, TAIL_TM)

    tail_rows = CONV_WIDTH - 1
    return (
        y_p.reshape(BATCH, SEQ, D_MODEL),
        y_s.reshape(DEC_BATCH, DEC_SEQ, D_MODEL),
        k_p.reshape(1, BATCH, SEQ, N_KV_HEADS, HEAD_DIM),
        v_p.reshape(1, BATCH, SEQ, N_KV_HEADS, HEAD_DIM),
        hf_p[:, F_KI:F_KI + IDX_DIM].reshape(1, BATCH, SEQ, IDX_DIM),
        u_tail[:, PAD_ROWS - tail_rows:].reshape(1, BATCH, tail_rows, D_CONV),
        k_s.reshape(1, DEC_BATCH, DEC_SEQ, N_KV_HEADS, HEAD_DIM),
        v_s.reshape(1, DEC_BATCH, DEC_SEQ, N_KV_HEADS, HEAD_DIM),
        hf_s[:, F_KI:F_KI + IDX_DIM].reshape(1, DEC_BATCH, DEC_SEQ, IDX_DIM),
        state_new.transpose(1, 0, 2).reshape(1, DEC_BATCH, tail_rows, D_CONV),
    )
```

```python
import functools
import math

import numpy as np
import jax
import jax.numpy as jnp
from jax import lax
from jax.experimental import pallas as pl
from jax.experimental.pallas import tpu as pltpu

F32 = jnp.float32
BF16 = jnp.bfloat16

D_MODEL = 2048
BATCH = 8
SEQ = 2048
DEC_BATCH = 128
DEC_SEQ = 8
PAST_LEN = 2048
PAGE_SIZE = 128
N_PAGES = PAST_LEN // PAGE_SIZE
N_HEADS = 8
N_KV_HEADS = 2
HEAD_DIM = 128
GROUP = N_HEADS // N_KV_HEADS
D_ATTN = N_HEADS * HEAD_DIM
D_KV = N_KV_HEADS * HEAD_DIM
IDX_HEADS = 8
IDX_DIM = 64
TOPK = 256
Q_BLOCK = 128
N_BUCKETS = 32
MAX_DISTANCE = 128
D_CONV = D_MODEL // 2
CONV_WIDTH = 31
EPS = 1e-6
NEG = -1e30
NEG_INF = float("-inf")

LANE = 128
SUBLANE = 8
VMEM_LIMIT = 56 * 1024 * 1024

C_GA = 0
C_GC = C_GA + D_MODEL
C_Q = C_GC + D_MODEL
C_ZA = C_Q + D_ATTN
C_GV = C_ZA + D_ATTN
C_GG = C_GV + D_CONV
C_ZC = C_GG + D_CONV
C_K = C_ZC + D_CONV
C_V = C_K + D_KV
C_QI = C_V + D_KV
C_KI = C_QI + IDX_HEADS * IDX_DIM
W_OFF = IDX_DIM
MXU_WIDTH = 256
PROJ_TM = 1024
PROJ_TN = 6 * MXU_WIDTH
D_H = -(-(C_KI + LANE) // PROJ_TN) * PROJ_TN
KV_TILE = C_K // PROJ_TN
assert C_K % PROJ_TN == 0 and D_H == C_K + PROJ_TN
D_HB = C_K
D_HF = C_KI + LANE - C_K
F_K = C_K - D_HB
F_V = C_V - D_HB
F_QI = C_QI - D_HB
F_KI = C_KI - D_HB

_SRC = {}
_off = 0
for _name, _w in (('q', D_ATTN), ('k', D_KV), ('v', D_KV), ('z_attn', D_ATTN),
                  ('q_idx', IDX_HEADS * IDX_DIM), ('k_idx', IDX_DIM), ('w_idx', IDX_HEADS),
                  ('glu_val', D_CONV), ('glu_gate', D_CONV), ('z_conv', D_CONV),
                  ('gate_attn', D_MODEL), ('gate_conv', D_MODEL)):
    _SRC[_name] = (_off, _w)
    _off += _w
D_IN = _off
REORDER_ROWS = 512

N_BISECT = 20
PAD_ROWS = 32
CONV_CHUNK = 64
SCORE_SCALE = (IDX_DIM ** -0.5) * (IDX_HEADS ** -0.5)
QK_SCALE = HEAD_DIM ** -0.5
LOG2E = math.log2(math.e)
L_SAMPLE = (N_PAGES + 1) * LANE
SUBTILES = LANE // SUBLANE
CHUNK_BLOCKS = 4
CHUNK = CHUNK_BLOCKS * Q_BLOCK
PAD_KEYS = CHUNK - Q_BLOCK
VT_ROWS = HEAD_DIM + 2 * SUBLANE
SCORE_ROWS = 4
ATTN_ROWS = 4
TAIL_TM = 256


def _t5_bucket_static(dist):
    n = np.maximum(dist, 0)
    max_exact = N_BUCKETS // 2
    ratio = (np.log(np.maximum(n, 1).astype(np.float32) / np.float32(max_exact))
             / np.float32(math.log(MAX_DISTANCE / max_exact)))
    large = np.minimum(max_exact + (ratio * np.float32(N_BUCKETS - max_exact)).astype(np.int32),
                       N_BUCKETS - 1)
    return np.where(n < max_exact, n, large).astype(np.int32)


FAR_BUCKET = int(_t5_bucket_static(np.array([2 * MAX_DISTANCE]))[0])


def _params(sem):
    return pltpu.CompilerParams(dimension_semantics=sem, vmem_limit_bytes=VMEM_LIMIT)


def _proj_kernel(x_ref, g_ref, w_ref, ob_ref, of_ref, k_ref, v_ref, xn_ref):
    tm = x_ref.shape[0]
    j = pl.program_id(1)

    @pl.when(j == 0)
    def _():
        x = x_ref[...]
        ms = jnp.mean(x * x, axis=-1, keepdims=True)
        xn_ref[...] = (x * lax.rsqrt(ms + EPS) * g_ref[...]).astype(BF16)

    def project():
        return lax.dot_general(xn_ref[...], w_ref[...], (((1,), (1,)), ((), ())),
                               preferred_element_type=F32)

    @pl.when(j < KV_TILE)
    def _():
        ob_ref[...] = project().astype(BF16)

    @pl.when(j == KV_TILE)
    def _():
        of_ref[...] = project()[:, :D_HF]
        for g in range(N_KV_HEADS):
            rows = pl.ds(g, tm, stride=N_KV_HEADS)
            k_ref[rows, :] = of_ref[:, F_K + g * HEAD_DIM:F_K + (g + 1) * HEAD_DIM]
            v_ref[rows, :] = of_ref[:, F_V + g * HEAD_DIM:F_V + (g + 1) * HEAD_DIM]


def _proj(x2d, g, w):
    n = x2d.shape[0]
    tm, tn = PROJ_TM, PROJ_TN
    kv_spec = pl.BlockSpec((N_KV_HEADS * tm, HEAD_DIM), lambda i, j: (i, 0))
    kv_shape = jax.ShapeDtypeStruct((N_KV_HEADS * n, HEAD_DIM), F32)
    return pl.pallas_call(
        _proj_kernel,
        grid=(n // tm, D_H // tn),
        in_specs=[pl.BlockSpec((tm, D_MODEL), lambda i, j: (i, 0)),
                  pl.BlockSpec((1, D_MODEL), lambda i, j: (0, 0)),
                  pl.BlockSpec((tn, D_MODEL), lambda i, j: (j, 0))],
        out_specs=[pl.BlockSpec((tm, tn), lambda i, j: (i, jnp.minimum(j, KV_TILE - 1))),
                   pl.BlockSpec((tm, D_HF), lambda i, j: (i, 0)), kv_spec, kv_spec],
        out_shape=[jax.ShapeDtypeStruct((n, D_HB), BF16),
                   jax.ShapeDtypeStruct((n, D_HF), F32), kv_shape, kv_shape],
        scratch_shapes=[pltpu.VMEM((tm, D_MODEL), BF16)],
        compiler_params=_params(("parallel", "arbitrary")),
        name="proj",
    )(x2d, g, w)


def _bias_kernel(rb_ref, bp_ref, bs_ref, op_ref, os_ref):
    bp = bp_ref[...]
    bs = bs_ref[...]
    for h in range(N_HEADS):
        far = rb_ref[FAR_BUCKET, h]
        tp = jnp.zeros(bp.shape, F32)
        ts = jnp.zeros(bs.shape, F32)
        for b in range(N_BUCKETS):
            val = rb_ref[b, h] - far
            tp = jnp.where(bp == b, val * LOG2E, tp)
            ts = jnp.where(bs == b, val, ts)
        op_ref[h] = tp
        os_ref[h] = ts


def _bias_tables(rel_bias):
    key = np.arange(CHUNK)[:, None]
    qry = np.arange(Q_BLOCK)[None, :]
    bucket_p = _t5_bucket_static(CHUNK - Q_BLOCK + qry - key)
    qi = np.arange(DEC_SEQ)[:, None]
    col = np.arange(2 * LANE)[None, :]
    dist_s = np.where(col < LANE, LANE + qi - col, qi - (col - LANE))
    bucket_s = _t5_bucket_static(dist_s)
    return pl.pallas_call(
        _bias_kernel,
        in_specs=[pl.BlockSpec(memory_space=pltpu.SMEM),
                  pl.BlockSpec(memory_space=pltpu.VMEM),
                  pl.BlockSpec(memory_space=pltpu.VMEM)],
        out_specs=[pl.BlockSpec(memory_space=pltpu.VMEM),
                   pl.BlockSpec(memory_space=pltpu.VMEM)],
        out_shape=[jax.ShapeDtypeStruct((N_HEADS, CHUNK, Q_BLOCK), F32),
                   jax.ShapeDtypeStruct((N_HEADS, DEC_SEQ, 2 * LANE), F32)],
        name="bias_tables",
    )(rel_bias, jnp.asarray(bucket_p), jnp.asarray(bucket_s))


def _any(x):
    return jnp.max(jnp.where(x, 1.0, 0.0)) > 0.5


def _rep(x):
    return jnp.broadcast_to(x, (SUBLANE, LANE))


def _fold_rows(x, comb):
    parts = [x[k:k + SUBLANE] for k in range(0, x.shape[0], SUBLANE)]
    while len(parts) > 1:
        parts = [comb(parts[k], parts[k + 1]) for k in range(0, len(parts), 2)]
    return parts[0]


def _select_threshold(tile_fn, ntiles, static):
    def reduce_tiles(fn, init, comb):
        def step(j, acc):
            x = fn(tile_fn(j), j)
            parts = [x[k] for k in range(x.shape[0])]
            while len(parts) > 1:
                parts = [comb(parts[k], parts[k + 1]) for k in range(0, len(parts), 2)]
            return comb(acc, parts[0])
        if static:
            acc = init
            for j in range(ntiles):
                acc = step(j, acc)
            return acc
        return lax.fori_loop(0, ntiles, step, init)

    zeros = jnp.zeros((SUBLANE, LANE), F32)

    def count(pred_fn):
        acc = reduce_tiles(lambda s, j: jnp.where(pred_fn(s, j), 1.0, 0.0), zeros,
                           lambda a, b: a + b)
        return _rep(jnp.sum(acc, axis=0, keepdims=True))

    def masked_max(pred_fn):
        acc = reduce_tiles(lambda s, j: jnp.where(pred_fn(s, j), s, NEG_INF),
                           jnp.full((SUBLANE, LANE), NEG_INF, F32), jnp.maximum)
        return _rep(jnp.max(acc, axis=0, keepdims=True))

    bound = reduce_tiles(lambda s, j: jnp.where(s > NEG_INF, jnp.abs(s), 0.0), zeros,
                         jnp.maximum)
    bound = _rep(jnp.max(bound, axis=0, keepdims=True))

    def bisect(_, carry):
        lo, hi = carry
        mid = 0.5 * lo + 0.5 * hi
        few = count(lambda s, j: s > mid[None]) < TOPK
        return jnp.where(few, lo, mid), jnp.where(few, mid, hi)

    _, hi = lax.fori_loop(0, N_BISECT, bisect, (-bound, bound))

    thr = masked_max(lambda s, j: s <= hi[None])
    n_ge = count(lambda s, j: s >= thr[None])

    def fix_body(carry):
        thr, n_ge, _ = carry
        lower = masked_max(lambda s, j: s < thr[None])
        thr = jnp.where(n_ge < TOPK, lower, thr)
        n_ge = count(lambda s, j: s >= thr[None])
        return thr, n_ge, _any(n_ge < TOPK)

    thr, n_ge, _ = lax.while_loop(lambda c: c[2], fix_body, (thr, n_ge, _any(n_ge < TOPK)))
    n_gt = count(lambda s, j: s > thr[None])
    need = TOPK - n_gt
    return thr[0:1], need[0:1]


def _keep(pred):
    return jnp.where(pred, 0.0, NEG)


def _selection_masks(tiles, thr, need, seen, tri):
    ties = [s == thr for s in tiles]
    ranks = []
    for k in range(0, len(tiles), 2):
        pair = jnp.concatenate([jnp.where(t, 1.0, 0.0).astype(BF16) for t in ties[k:k + 2]], axis=1)
        rank = jnp.dot(tri, pair, preferred_element_type=F32)
        ranks += [rank[:, p * LANE:(p + 1) * LANE] for p in range(len(ties[k:k + 2]))]
    masks = []
    for s, tie, rank in zip(tiles, ties, ranks):
        masks.append(jnp.where(tie, _keep(rank + seen <= need), _keep(s > thr)))
        seen = seen + rank[s.shape[0] - 1:]
    return masks, seen


def _attn_p_kernel(q_ref, qi_ref, wi_ref, k_ref, v_ref, ki_ref, bias_ref, tri_ref, o_ref,
                   kb_ref, vt_ref, kib_ref, qh_ref, qih_ref,
                   score_ref, mask_ref, s_ref):
    i = pl.program_id(1)
    T = Q_BLOCK
    W = GROUP * T
    nch = i // CHUNK_BLOCKS + 1

    @pl.when(i == 0)
    def _():
        kb_ref[0:PAD_KEYS] = jnp.zeros((PAD_KEYS, D_KV), BF16)
        kb_ref[PAD_KEYS:] = k_ref[0].astype(BF16)
        kib_ref[0:PAD_KEYS] = jnp.zeros((PAD_KEYS, IDX_DIM), BF16)
        kib_ref[PAD_KEYS:] = ki_ref[0][:, :IDX_DIM].astype(BF16)
        ones_row = lax.broadcasted_iota(jnp.int32, (VT_ROWS - HEAD_DIM, PAD_KEYS + SEQ), 0) == 0
        for g in range(N_KV_HEADS):
            vt_ref[g, 0:HEAD_DIM, 0:PAD_KEYS] = jnp.zeros((HEAD_DIM, PAD_KEYS), BF16)
            vt_ref[g, HEAD_DIM:VT_ROWS, :] = jnp.where(ones_row, 1.0, 0.0).astype(BF16)
            for c in range(SEQ // LANE):
                blk = v_ref[0, c * LANE:(c + 1) * LANE, g * HEAD_DIM:(g + 1) * HEAD_DIM]
                vt_ref[g, 0:HEAD_DIM,
                       PAD_KEYS + c * LANE:PAD_KEYS + (c + 1) * LANE] = blk.T.astype(BF16)
        score_ref[0:PAD_KEYS] = jnp.full((PAD_KEYS, T), NEG_INF, F32)
        mask_ref[0:PAD_KEYS] = jnp.full((PAD_KEYS, T), NEG, F32)

    q = q_ref[0].astype(F32) * (QK_SCALE * LOG2E)
    for h in range(N_HEADS):
        qh_ref[h] = q[:, h * HEAD_DIM:(h + 1) * HEAD_DIM].astype(BF16)
    qi = qi_ref[0]
    for h in range(IDX_HEADS):
        qih_ref[h] = qi[:, h * IDX_DIM:(h + 1) * IDX_DIM].astype(BF16)
    w_rows = wi_ref[0].T[W_OFF:W_OFF + IDX_HEADS] * SCORE_SCALE

    def span(c):
        return pl.ds(pl.multiple_of((i - CHUNK_BLOCKS * c) * LANE, LANE), CHUNK)

    def first_key(c):
        return (i - CHUNK_BLOCKS * c) * LANE - PAD_KEYS

    key_l = lax.broadcasted_iota(jnp.int32, (CHUNK, T), 0)
    qry = i * T + lax.broadcasted_iota(jnp.int32, (CHUNK, T), 1)

    def score_chunk(c):
        kc = kib_ref[span(c), :]
        d = lax.dot_general(kc, qih_ref[...].reshape(IDX_HEADS * T, IDX_DIM),
                            (((1,), (1,)), ((), ())), preferred_element_type=F32)
        acc = jnp.zeros((CHUNK, T), F32)
        for h in range(IDX_HEADS):
            acc = acc + jnp.maximum(d[:, h * T:(h + 1) * T], 0.0) * w_rows[h:h + 1]
        key = first_key(c) + key_l
        acc = jnp.where(key <= qry, jnp.where(key >= 0, acc, NEG_INF), NEG_INF)
        score_ref[span(c), :] = acc

    nsub = CHUNK // SUBLANE

    def tile_fn(c):
        return score_ref[span(c), :].reshape(nsub, SUBLANE, T)

    def select_all():
        mask_ref[span(0), :] = _keep(score_ref[span(0), :] > NEG_INF)

    def select(n):
        thr, need = _select_threshold(tile_fn, n, static=True)
        seen = jnp.zeros((1, T), F32)
        for c in reversed(range(n)):
            start = pl.multiple_of((i - CHUNK_BLOCKS * c) * LANE, LANE)
            parts = [pl.ds(start + p * MXU_WIDTH, MXU_WIDTH) for p in range(CHUNK // MXU_WIDTH)]
            masks, seen = _selection_masks([score_ref[rows, :] for rows in parts],
                                           thr, need, seen, tri_ref[...])
            for rows, mask in zip(parts, masks):
                mask_ref[rows, :] = mask

    def logits(c, g):
        kc = kb_ref[span(c), g * HEAD_DIM:(g + 1) * HEAD_DIM]
        qg = qh_ref[g * GROUP:(g + 1) * GROUP].reshape(W, HEAD_DIM)
        s = lax.dot_general(kc, qg, (((1,), (1,)), ((), ())), preferred_element_type=F32)
        mb = mask_ref[span(c), :]
        if c == 0:
            add = jnp.concatenate([mb + bias_ref[g * GROUP + hq] for hq in range(GROUP)], axis=1)
        else:
            add = jnp.concatenate([mb] * GROUP, axis=1)
        s = s + add
        s_ref[span(c), g * W:(g + 1) * W] = s
        return jnp.max(_fold_rows(s, jnp.maximum), axis=0, keepdims=True)

    def run(n):
        for c in range(n):
            score_chunk(c)
        if n == 1:
            pl.when(i * T + T <= TOPK)(select_all)
            pl.when(i * T + T > TOPK)(functools.partial(select, n))
        else:
            select(n)
        for g in range(N_KV_HEADS):
            m = functools.reduce(jnp.maximum, [logits(c, g) for c in range(n)])
            acc = None
            for c in range(n):
                p = jnp.exp2((s_ref[span(c), g * W:(g + 1) * W] - m).astype(BF16))
                pv = jnp.dot(vt_ref[g, :, span(c)], p, preferred_element_type=F32)
                acc = pv if acc is None else acc + pv
            o = acc[0:HEAD_DIM] / acc[HEAD_DIM:HEAD_DIM + 1]
            for hq in range(GROUP):
                h = g * GROUP + hq
                o_ref[0, :, h * HEAD_DIM:(h + 1) * HEAD_DIM] = o[:, hq * T:(hq + 1) * T].T

    for n in range(1, SEQ // CHUNK + 1):
        pl.when(nch == n)(functools.partial(run, n))


def _attn_prompt(hb_p, hf_p, bias_p):
    hb3 = hb_p.reshape(BATCH, SEQ, D_HB)
    hf3 = hf_p.reshape(BATCH, SEQ, D_HF)
    nqb = SEQ // Q_BLOCK
    T = Q_BLOCK
    qi_w = IDX_HEADS * IDX_DIM
    return pl.pallas_call(
        _attn_p_kernel,
        grid=(BATCH, nqb),
        in_specs=[
            pl.BlockSpec((1, T, D_ATTN), lambda b, i: (b, i, C_Q // D_ATTN)),
            pl.BlockSpec((1, T, qi_w), lambda b, i: (b, i, F_QI // qi_w)),
            pl.BlockSpec((1, T, LANE), lambda b, i: (b, i, F_KI // LANE)),
            pl.BlockSpec((1, SEQ, D_KV), lambda b, i: (b, 0, F_K // D_KV)),
            pl.BlockSpec((1, SEQ, D_KV), lambda b, i: (b, 0, F_V // D_KV)),
            pl.BlockSpec((1, SEQ, LANE), lambda b, i: (b, 0, F_KI // LANE)),
            pl.BlockSpec((N_HEADS, CHUNK, T), lambda b, i: (0, 0, 0)),
            pl.BlockSpec((MXU_WIDTH, MXU_WIDTH), lambda b, i: (0, 0)),
        ],
        out_specs=pl.BlockSpec((1, T, D_ATTN), lambda b, i: (b, i, 0)),
        out_shape=jax.ShapeDtypeStruct((BATCH, SEQ, D_ATTN), F32),
        scratch_shapes=[
            pltpu.VMEM((PAD_KEYS + SEQ, D_KV), BF16),
            pltpu.VMEM((N_KV_HEADS, VT_ROWS, PAD_KEYS + SEQ), BF16),
            pltpu.VMEM((PAD_KEYS + SEQ, IDX_DIM), BF16),
            pltpu.VMEM((N_HEADS, T, HEAD_DIM), BF16),
            pltpu.VMEM((IDX_HEADS, T, IDX_DIM), BF16),
            pltpu.VMEM((PAD_KEYS + SEQ, T), F32),
            pltpu.VMEM((PAD_KEYS + SEQ, T), F32),
            pltpu.VMEM((PAD_KEYS + SEQ, N_HEADS * T), F32),
        ],
        compiler_params=_params(("parallel", "arbitrary")),
        name="attn_prompt",
    )(hb3, hf3, hf3, hf3, hf3, hf3, bias_p, jnp.tri(MXU_WIDTH, dtype=BF16))


def _score_s_kernel(pt_ref, qi_ref, w_ref, kin_ref, *rest):
    npg = SCORE_ROWS * N_PAGES
    kip = rest[0:npg]
    o_ref = rest[npg]
    kinp_ref = rest[npg + 1]
    del pt_ref
    R = DEC_SEQ
    qrow = lax.broadcasted_iota(jnp.int32, (R, LANE), 0)
    lane = lax.broadcasted_iota(jnp.int32, (R, LANE), 1)
    kinp_ref[...] = jnp.zeros(kinp_ref.shape, BF16)
    for r in range(SCORE_ROWS):
        kinp_ref[r, 0:2 * R] = jnp.concatenate(
            [kin_ref[r][:, :IDX_DIM], jnp.zeros((R, IDX_DIM), F32)], 0).astype(BF16)
    for r in range(SCORE_ROWS):
        qi = qi_ref[r].astype(BF16)
        wb = jnp.broadcast_to(w_ref[r] * SCORE_SCALE, (IDX_HEADS * R, LANE))
        for t in range(N_PAGES + 1):
            if t < N_PAGES:
                d = jnp.dot(qi, kip[r * N_PAGES + t][0].astype(BF16), preferred_element_type=F32)
            else:
                d = lax.dot_general(qi, kinp_ref[r], (((1,), (1,)), ((), ())),
                                    preferred_element_type=F32)
            e = (jnp.maximum(d, 0.0) * wb).reshape(IDX_HEADS, R, LANE)
            s = e[0]
            for h in range(1, IDX_HEADS):
                s = s + e[h]
            if t == N_PAGES:
                s = jnp.where(lane <= qrow, s, NEG_INF)
            o_ref[r, :, t * LANE:(t + 1) * LANE] = s


def _score_sample(hf_s, page_table, cache_kidx_t):
    R = DEC_SEQ
    G = SCORE_ROWS
    h3 = hf_s.reshape(DEC_BATCH, R, D_HF)
    qi_hq = h3[:, :, F_QI:F_QI + IDX_HEADS * IDX_DIM].reshape(DEC_BATCH, R, IDX_HEADS, IDX_DIM)
    qi_hq = qi_hq.transpose(0, 2, 1, 3).reshape(DEC_BATCH, IDX_HEADS * R, IDX_DIM)
    w_hq = h3[:, :, F_KI + W_OFF:F_KI + W_OFF + IDX_HEADS].transpose(0, 2, 1)
    w_hq = w_hq.reshape(DEC_BATCH, IDX_HEADS * R, 1)
    in_specs = [
        pl.BlockSpec((G, IDX_HEADS * R, IDX_DIM), lambda b, pt: (b, 0, 0)),
        pl.BlockSpec((G, IDX_HEADS * R, 1), lambda b, pt: (b, 0, 0)),
        pl.BlockSpec((G, R, LANE), lambda b, pt: (b, 0, F_KI // LANE)),
    ]
    in_specs += [pl.BlockSpec((1, IDX_DIM, PAGE_SIZE),
                              lambda b, pt, r=r, p=p: (pt[b * G + r, p], 0, 0))
                 for r in range(G) for p in range(N_PAGES)]
    grid_spec = pltpu.PrefetchScalarGridSpec(
        num_scalar_prefetch=1,
        grid=(DEC_BATCH // G,),
        in_specs=in_specs,
        out_specs=pl.BlockSpec((G, R, L_SAMPLE), lambda b, pt: (b, 0, 0)),
        scratch_shapes=[pltpu.VMEM((G, PAGE_SIZE, IDX_DIM), BF16)],
    )
    return pl.pallas_call(
        _score_s_kernel,
        grid_spec=grid_spec,
        out_shape=jax.ShapeDtypeStruct((DEC_BATCH, R, L_SAMPLE), F32),
        compiler_params=_params(("arbitrary",)),
        name="score_sample",
    )(page_table, qi_hq, w_hq, h3, *([cache_kidx_t] * (G * N_PAGES)))


def _select_s_kernel(s_ref, tri_ref, o_ref):
    nt = N_PAGES + 1

    def tile_fn(j):
        return s_ref[j * LANE:(j + 1) * LANE, :].reshape(SUBTILES, SUBLANE, LANE)

    thr, need = _select_threshold(tile_fn, nt, static=True)
    parts = [slice(j * LANE, (j + 1) * LANE) for j in range(nt)]
    masks, _ = _selection_masks([s_ref[rows, :] for rows in parts], thr, need,
                                jnp.zeros((1, LANE), F32), tri_ref[...])
    for rows, mask in zip(parts, masks):
        o_ref[rows, :] = mask


def _select_sample(scores_t):
    n = scores_t.shape[1]
    return pl.pallas_call(
        _select_s_kernel,
        grid=(n // LANE,),
        in_specs=[pl.BlockSpec((L_SAMPLE, LANE), lambda c: (0, c)),
                  pl.BlockSpec((LANE, LANE), lambda c: (0, 0))],
        out_specs=pl.BlockSpec((L_SAMPLE, LANE), lambda c: (0, c)),
        out_shape=jax.ShapeDtypeStruct((L_SAMPLE, n), F32),
        compiler_params=_params(("parallel",)),
        name="select_sample",
    )(scores_t, jnp.tri(LANE, dtype=BF16))


def _attn_s_kernel(pt_ref, q_ref, kn_ref, vn_ref, mask_ref, bias_ref, *rest):
    npg = ATTN_ROWS * N_PAGES
    kp = rest[0:npg]
    vp = rest[npg:2 * npg]
    o_ref = rest[2 * npg]
    knp_ref, vnp_ref, logit_ref = rest[2 * npg + 1:]
    del pt_ref
    R = DEC_SEQ
    NT = N_PAGES + 1
    GR = GROUP * R

    knp_ref[...] = jnp.zeros(knp_ref.shape, BF16)
    vnp_ref[...] = jnp.zeros(vnp_ref.shape, BF16)
    for r in range(ATTN_ROWS):
        knp_ref[r, 0:2 * R] = jnp.concatenate([kn_ref[r], jnp.zeros((R, D_KV), F32)], 0).astype(BF16)
        vnp_ref[r, 0:2 * R] = jnp.concatenate([vn_ref[r], jnp.zeros((R, D_KV), F32)], 0).astype(BF16)

    def page_head(refs, pad_ref, r, t, g):
        if t < N_PAGES:
            return refs[r * N_PAGES + t][pl.ds(g, PAGE_SIZE, stride=N_KV_HEADS), :].astype(BF16)
        return pad_ref[r, :, g * HEAD_DIM:(g + 1) * HEAD_DIM]

    for r in range(ATTN_ROWS):
        q = (q_ref[r].astype(F32) * QK_SCALE).astype(BF16)
        for t in range(NT):
            mb = mask_ref[r, :, t * LANE:(t + 1) * LANE]
            for g in range(N_KV_HEADS):
                lg = lax.dot_general(q[g * GR:(g + 1) * GR], page_head(kp, knp_ref, r, t, g),
                                     (((1,), (1,)), ((), ())), preferred_element_type=F32)
                lg = lg.reshape(GROUP, R, LANE) + mb[None]
                if t >= N_PAGES - 1:
                    off = (t - (N_PAGES - 1)) * LANE
                    lg = lg + bias_ref[g * GROUP:(g + 1) * GROUP, :, off:off + LANE]
                logit_ref[r, g * GR:(g + 1) * GR, t * LANE:(t + 1) * LANE] = lg.reshape(GR, LANE)

    for r in range(ATTN_ROWS):
        logits = logit_ref[r]
        m = jnp.max(logits, axis=1, keepdims=True)
        p = jnp.exp(logits - m)
        inv = 1.0 / jnp.sum(p, axis=1, keepdims=True)
        pb = p.astype(BF16)
        outs = [jnp.zeros((GR, HEAD_DIM), F32) for _ in range(N_KV_HEADS)]
        for t in range(NT):
            for g in range(N_KV_HEADS):
                outs[g] = outs[g] + jnp.dot(pb[g * GR:(g + 1) * GR, t * LANE:(t + 1) * LANE],
                                            page_head(vp, vnp_ref, r, t, g),
                                            preferred_element_type=F32)
        for g in range(N_KV_HEADS):
            o = outs[g] * inv[g * GR:(g + 1) * GR]
            for hq in range(GROUP):
                h = g * GROUP + hq
                o_ref[r, :, h * HEAD_DIM:(h + 1) * HEAD_DIM] = o[hq * R:(hq + 1) * R]


def _attn_sample(hb_s, hf_s, page_table, cache_k, cache_v, mask, bias_s):
    R = DEC_SEQ
    G = ATTN_ROWS
    h3 = hf_s.reshape(DEC_BATCH, R, D_HF)
    q_hq = hb_s[:, C_Q:C_Q + D_ATTN].reshape(DEC_BATCH, R, N_HEADS, HEAD_DIM)
    q_hq = q_hq.transpose(0, 2, 1, 3).reshape(DEC_BATCH, N_HEADS * R, HEAD_DIM)
    rows_per_page = PAGE_SIZE * N_KV_HEADS
    ck = cache_k.reshape(-1, HEAD_DIM)
    cv = cache_v.reshape(-1, HEAD_DIM)

    in_specs = [
        pl.BlockSpec((G, N_HEADS * R, HEAD_DIM), lambda b, pt: (b, 0, 0)),
        pl.BlockSpec((G, R, D_KV), lambda b, pt: (b, 0, F_K // D_KV)),
        pl.BlockSpec((G, R, D_KV), lambda b, pt: (b, 0, F_V // D_KV)),
        pl.BlockSpec((G, R, L_SAMPLE), lambda b, pt: (b, 0, 0)),
        pl.BlockSpec((N_HEADS, R, 2 * LANE), lambda b, pt: (0, 0, 0)),
    ]
    pages = [pl.BlockSpec((rows_per_page, HEAD_DIM), lambda b, pt, r=r, p=p: (pt[b * G + r, p], 0))
             for r in range(G) for p in range(N_PAGES)]
    in_specs += pages + pages
    grid_spec = pltpu.PrefetchScalarGridSpec(
        num_scalar_prefetch=1,
        grid=(DEC_BATCH // G,),
        in_specs=in_specs,
        out_specs=pl.BlockSpec((G, R, D_ATTN), lambda b, pt: (b, 0, 0)),
        scratch_shapes=[
            pltpu.VMEM((G, PAGE_SIZE, D_KV), BF16),
            pltpu.VMEM((G, PAGE_SIZE, D_KV), BF16),
            pltpu.VMEM((G, N_HEADS * R, L_SAMPLE), F32),
        ],
    )
    return pl.pallas_call(
        _attn_s_kernel,
        grid_spec=grid_spec,
        out_shape=jax.ShapeDtypeStruct((DEC_BATCH, R, D_ATTN), F32),
        compiler_params=_params(("arbitrary",)),
        name="attn_sample",
    )(page_table, q_hq, h3, h3, mask, bias_s, *([ck] * (G * N_PAGES)), *([cv] * (G * N_PAGES)))


def _conv_p_kernel(val_ref, gate_ref, cw_ref, cb_ref, dw_ref, ut_ref, pad_ref):
    pad_ref[0:PAD_ROWS] = jnp.zeros((PAD_ROWS, LANE), F32)
    pad_ref[PAD_ROWS:] = val_ref[0].astype(F32) * jax.nn.sigmoid(gate_ref[0].astype(F32))
    ut_ref[0] = pad_ref[SEQ:SEQ + PAD_ROWS]
    cw = cw_ref[...]
    cb = cb_ref[...]
    first = PAD_ROWS - (CONV_WIDTH - 1)
    for c in range(SEQ // CONV_CHUNK):
        base = c * CONV_CHUNK
        acc = jnp.broadcast_to(cb, (CONV_CHUNK, LANE))
        for r in range(SUBLANE):
            taps = [w for w in range(CONV_WIDTH) if (first + w) % SUBLANE == r]
            span = max(first + w - r for w in taps) + CONV_CHUNK
            win = pad_ref[base + r:base + r + span]
            for w in taps:
                a = first + w - r
                acc = acc + win[a:a + CONV_CHUNK] * cw[w:w + 1]
        dw_ref[0, base:base + CONV_CHUNK] = acc


def _conv_prompt(hb_p, cw_pad, cb):
    h3 = hb_p.reshape(BATCH, SEQ, D_HB)
    nc = D_CONV // LANE
    return pl.pallas_call(
        _conv_p_kernel,
        grid=(BATCH, nc),
        in_specs=[pl.BlockSpec((1, SEQ, LANE), lambda b, c: (b, 0, C_GV // LANE + c)),
                  pl.BlockSpec((1, SEQ, LANE), lambda b, c: (b, 0, C_GG // LANE + c)),
                  pl.BlockSpec((PAD_ROWS, LANE), lambda b, c: (0, c)),
                  pl.BlockSpec((1, LANE), lambda b, c: (0, c))],
        out_specs=[pl.BlockSpec((1, SEQ, LANE), lambda b, c: (b, 0, c)),
                   pl.BlockSpec((1, PAD_ROWS, LANE), lambda b, c: (b, 0, c))],
        out_shape=[jax.ShapeDtypeStruct((BATCH, SEQ, D_CONV), F32),
                   jax.ShapeDtypeStruct((BATCH, PAD_ROWS, D_CONV), F32)],
        scratch_shapes=[pltpu.VMEM((PAD_ROWS + SEQ, LANE), F32)],
        compiler_params=_params(("parallel", "parallel")),
        name="conv_prompt",
    )(h3, h3, cw_pad, cb)


def _conv_s_kernel(val_ref, gate_ref, st_ref, cw_ref, cb_ref, dw_ref, ns_ref):
    R = DEC_SEQ
    H = CONV_WIDTH - 1
    cw = cw_ref[...]
    cb = jnp.broadcast_to(cb_ref[...], (DEC_BATCH, LANE))
    u = []
    for q in range(R):
        rows = pl.ds(q, DEC_BATCH, stride=R)
        u.append(val_ref[rows, :] * jax.nn.sigmoid(gate_ref[rows, :]))

    def row(r):
        return st_ref[r] if r < H else u[r - H]

    for q in range(R):
        acc = cb
        for w in range(CONV_WIDTH):
            acc = acc + row(q + w) * cw[w:w + 1]
        dw_ref[pl.ds(q, DEC_BATCH, stride=R), :] = acc
    for r in range(H):
        ns_ref[r] = row(r + R)


def _conv_sample(val, gate, state_t, cw_pad, cb):
    n_s = DEC_BATCH * DEC_SEQ
    H = CONV_WIDTH - 1
    nc = D_CONV // LANE
    return pl.pallas_call(
        _conv_s_kernel,
        grid=(nc,),
        in_specs=[pl.BlockSpec((n_s, LANE), lambda c: (0, c)),
                  pl.BlockSpec((n_s, LANE), lambda c: (0, c)),
                  pl.BlockSpec((H, DEC_BATCH, LANE), lambda c: (0, 0, c)),
                  pl.BlockSpec((PAD_ROWS, LANE), lambda c: (0, c)),
                  pl.BlockSpec((1, LANE), lambda c: (0, c))],
        out_specs=[pl.BlockSpec((n_s, LANE), lambda c: (0, c)),
                   pl.BlockSpec((H, DEC_BATCH, LANE), lambda c: (0, 0, c))],
        out_shape=[jax.ShapeDtypeStruct((n_s, D_CONV), F32),
                   jax.ShapeDtypeStruct((H, DEC_BATCH, D_CONV), F32)],
        compiler_params=_params(("parallel",)),
        name="conv_sample",
    )(val, gate, state_t, cw_pad, cb)


def _tail_kernel(attn_ref, za_ref, dw_ref, zc_ref, ga_ref, gc_ref, x_ref,
                 wua_ref, wpw_ref, wuc_ref, wo_ref, ng_ref, nb_ref, bpw_ref, fg_ref, y_ref):
    a = attn_ref[...] * jax.nn.silu(za_ref[...].astype(F32))
    branch_attn = jnp.dot(a.astype(BF16), wua_ref[...], preferred_element_type=F32)

    dw = dw_ref[...]
    mu = jnp.mean(dw, axis=-1, keepdims=True)
    var = jnp.mean(jnp.square(dw - mu), axis=-1, keepdims=True)
    ln = (dw - mu) * lax.rsqrt(var + EPS) * ng_ref[...] + nb_ref[...]
    conv_out = jnp.dot(jax.nn.silu(ln).astype(BF16), wpw_ref[...],
                       preferred_element_type=F32) + bpw_ref[...]
    c = conv_out * jax.nn.silu(zc_ref[...].astype(F32))
    branch_conv = jnp.dot(c.astype(BF16), wuc_ref[...], preferred_element_type=F32)

    merged = (jax.nn.sigmoid(ga_ref[...].astype(F32)) * branch_attn
              + jax.nn.sigmoid(gc_ref[...].astype(F32)) * branch_conv)
    y = x_ref[...] + jnp.dot(merged.astype(BF16), wo_ref[...], preferred_element_type=F32)
    ms = jnp.mean(y * y, axis=-1, keepdims=True)
    y_ref[...] = y * lax.rsqrt(ms + EPS) * fg_ref[...]


def _tail(h, attn, dw, x2d, weights, tm):
    n = x2d.shape[0]

    def const(shape):
        return pl.BlockSpec(shape, lambda i: (0, 0), pipeline_mode=pl.Buffered(1))

    def cols(width, offset):
        return pl.BlockSpec((tm, width), lambda i: (i, offset // width))

    return pl.pallas_call(
        _tail_kernel,
        grid=(n // tm,),
        in_specs=[cols(D_ATTN, 0), cols(D_ATTN, C_ZA), cols(D_CONV, 0), cols(D_CONV, C_ZC),
                  cols(D_MODEL, C_GA), cols(D_MODEL, C_GC), cols(D_MODEL, 0),
                  const((D_ATTN, D_MODEL)), const((D_CONV, D_CONV)),
                  const((D_CONV, D_MODEL)), const((D_MODEL, D_MODEL)),
                  const((1, D_CONV)), const((1, D_CONV)), const((1, D_CONV)),
                  const((1, D_MODEL))],
        out_specs=pl.BlockSpec((tm, D_MODEL), lambda i: (i, 0)),
        out_shape=jax.ShapeDtypeStruct((n, D_MODEL), F32),
        compiler_params=_params(("parallel",)),
        name="tail",
    )(attn, h, dw, h, h, h, x2d, *weights)


def _reorder_kernel(off_ref, w_ref, o_ref):
    del off_ref
    o_ref[...] = w_ref[...].astype(BF16)


def _reorder_w_in(w_in):
    w_t = w_in.T
    src = []
    for first, dest, width in (('gate_attn', C_GA, 2 * D_MODEL), ('q', C_Q, D_ATTN),
                               ('z_attn', C_ZA, D_ATTN), ('glu_val', C_GV, 3 * D_CONV),
                               ('k', C_K, 2 * D_KV), ('q_idx', C_QI, D_H - C_QI)):
        assert dest == len(src) * REORDER_ROWS
        for r in range(0, width, REORDER_ROWS):
            src.append(min(_SRC[first][0] + r, D_IN - REORDER_ROWS) // SUBLANE)
    grid_spec = pltpu.PrefetchScalarGridSpec(
        num_scalar_prefetch=1,
        grid=(len(src),),
        in_specs=[pl.BlockSpec((pl.Element(REORDER_ROWS), pl.Element(D_MODEL)),
                               lambda d, off: (off[d] * SUBLANE, 0))],
        out_specs=pl.BlockSpec((REORDER_ROWS, D_MODEL), lambda d, off: (d, 0)),
    )
    return pl.pallas_call(
        _reorder_kernel,
        grid_spec=grid_spec,
        out_shape=jax.ShapeDtypeStruct((D_H, D_MODEL), BF16),
        compiler_params=_params(("arbitrary",)),
        name="reorder_w_in",
    )(jnp.asarray(src, jnp.int32), w_t)


def kernel(x_prompt, x_sample, cache_k, cache_v, cache_kidx, state_conv, page_table,
           ln_g, w_in, conv_w, conv_b, conv_norm_g, conv_norm_b, w_pw, b_pw,
           w_up_attn, w_up_conv, w_out, rel_bias, final_g):
    w_all = _reorder_w_in(w_in[0])
    g_in = ln_g[0].reshape(1, D_MODEL)
    xp = x_prompt.reshape(BATCH * SEQ, D_MODEL)
    xs = x_sample.reshape(DEC_BATCH * DEC_SEQ, D_MODEL)
    hb_p, hf_p, k_p, v_p = _proj(xp, g_in, w_all)
    hb_s, hf_s, k_s, v_s = _proj(xs, g_in, w_all)

    bias_p, bias_s = _bias_tables(rel_bias)

    attn_p = _attn_prompt(hb_p, hf_p, bias_p).reshape(BATCH * SEQ, D_ATTN)

    n_s = DEC_BATCH * DEC_SEQ
    scores = _score_sample(hf_s, page_table, cache_kidx[0].transpose(0, 2, 1))
    mask_t = _select_sample(scores.reshape(n_s, L_SAMPLE).T)
    mask = mask_t.T.reshape(DEC_BATCH, DEC_SEQ, L_SAMPLE)
    attn_s = _attn_sample(hb_s, hf_s, page_table, cache_k[0], cache_v[0], mask, bias_s)
    attn_s = attn_s.reshape(n_s, D_ATTN)

    cw_pad = jnp.concatenate([conv_w[0], jnp.zeros((PAD_ROWS - CONV_WIDTH, D_CONV), F32)], 0)
    cb = conv_b[0].reshape(1, D_CONV)
    dw_p, u_tail = _conv_prompt(hb_p, cw_pad, cb)
    dw_s, state_new = _conv_sample(hb_s[:, C_GV:C_GV + D_CONV].astype(F32),
                                   hb_s[:, C_GG:C_GG + D_CONV].astype(F32),
                                   state_conv[0].transpose(1, 0, 2), cw_pad, cb)

    wua = w_up_attn[0].astype(BF16)
    wpw = w_pw[0].astype(BF16)
    wuc = w_up_conv[0].astype(BF16)
    wo = w_out[0].astype(BF16)
    ng = conv_norm_g[0].reshape(1, D_CONV)
    nb = conv_norm_b[0].reshape(1, D_CONV)
    bpw = b_pw[0].reshape(1, D_CONV)
    fg = final_g.reshape(1, D_MODEL)
    weights = (wua, wpw, wuc, wo, ng, nb, bpw, fg)
    y_p = _tail(hb_p, attn_p, dw_p.reshape(BATCH * SEQ, D_CONV), xp, weights, TAIL_TM)
    y_s = _tail(hb_s, attn_s, dw_s, xs, weights, TAIL_TM)

    tail_rows = CONV_WIDTH - 1
    return (
        y_p.reshape(BATCH, SEQ, D_MODEL),
        y_s.reshape(DEC_BATCH, DEC_SEQ, D_MODEL),
        k_p.reshape(1, BATCH, SEQ, N_KV_HEADS, HEAD_DIM),
        v_p.reshape(1, BATCH, SEQ, N_KV_HEADS, HEAD_DIM),
        hf_p[:, F_KI:F_KI + IDX_DIM].reshape(1, BATCH, SEQ, IDX_DIM),
        u_tail[:, PAD_ROWS - tail_rows:].reshape(1, BATCH, tail_rows, D_CONV),
        k_s.reshape(1, DEC_BATCH, DEC_SEQ, N_KV_HEADS, HEAD_DIM),
        v_s.reshape(1, DEC_BATCH, DEC_SEQ, N_KV_HEADS, HEAD_DIM),
        hf_s[:, F_KI:F_KI + IDX_DIM].reshape(1, DEC_BATCH, DEC_SEQ, IDX_DIM),
        state_new.transpose(1, 0, 2).reshape(1, DEC_BATCH, tail_rows, D_CONV),
    )
```

```python
import functools
import math

import numpy as np
import jax
import jax.numpy as jnp
from jax import lax
from jax.experimental import pallas as pl
from jax.experimental.pallas import tpu as pltpu

F32 = jnp.float32
BF16 = jnp.bfloat16

D_MODEL = 2048
BATCH = 8
SEQ = 2048
DEC_BATCH = 128
DEC_SEQ = 8
PAST_LEN = 2048
PAGE_SIZE = 128
N_PAGES = PAST_LEN // PAGE_SIZE
N_HEADS = 8
N_KV_HEADS = 2
HEAD_DIM = 128
GROUP = N_HEADS // N_KV_HEADS
D_ATTN = N_HEADS * HEAD_DIM
D_KV = N_KV_HEADS * HEAD_DIM
IDX_HEADS = 8
IDX_DIM = 64
TOPK = 256
Q_BLOCK = 128
N_BUCKETS = 32
MAX_DISTANCE = 128
D_CONV = D_MODEL // 2
CONV_WIDTH = 31
EPS = 1e-6
NEG = -1e30
NEG_INF = float("-inf")

LANE = 128
SUBLANE = 8
VMEM_LIMIT = 56 * 1024 * 1024

C_GA = 0
C_GC = C_GA + D_MODEL
C_Q = C_GC + D_MODEL
C_ZA = C_Q + D_ATTN
C_GV = C_ZA + D_ATTN
C_GG = C_GV + D_CONV
C_ZC = C_GG + D_CONV
C_K = C_ZC + D_CONV
C_V = C_K + D_KV
C_QI = C_V + D_KV
C_KI = C_QI + IDX_HEADS * IDX_DIM
W_OFF = IDX_DIM
MXU_WIDTH = 256
PROJ_TM = 1024
PROJ_TN = 6 * MXU_WIDTH
D_H = -(-(C_KI + LANE) // PROJ_TN) * PROJ_TN
KV_TILE = C_K // PROJ_TN
assert C_K % PROJ_TN == 0 and D_H == C_K + PROJ_TN
D_HB = C_K
D_HF = C_KI + LANE - C_K
F_K = C_K - D_HB
F_V = C_V - D_HB
F_QI = C_QI - D_HB
F_KI = C_KI - D_HB

_SRC = {}
_off = 0
for _name, _w in (('q', D_ATTN), ('k', D_KV), ('v', D_KV), ('z_attn', D_ATTN),
                  ('q_idx', IDX_HEADS * IDX_DIM), ('k_idx', IDX_DIM), ('w_idx', IDX_HEADS),
                  ('glu_val', D_CONV), ('glu_gate', D_CONV), ('z_conv', D_CONV),
                  ('gate_attn', D_MODEL), ('gate_conv', D_MODEL)):
    _SRC[_name] = (_off, _w)
    _off += _w
D_IN = _off
REORDER_ROWS = 512

N_BISECT = 20
PAD_ROWS = 32
CONV_CHUNK = 64
SCORE_SCALE = (IDX_DIM ** -0.5) * (IDX_HEADS ** -0.5)
QK_SCALE = HEAD_DIM ** -0.5
LOG2E = math.log2(math.e)
L_SAMPLE = (N_PAGES + 1) * LANE
SUBTILES = LANE // SUBLANE
CHUNK_BLOCKS = 4
CHUNK = CHUNK_BLOCKS * Q_BLOCK
PAD_KEYS = CHUNK - Q_BLOCK
VT_ROWS = HEAD_DIM + 2 * SUBLANE
SCORE_ROWS = 4
ATTN_ROWS = 4
TAIL_TM = 256


def _t5_bucket_static(dist):
    n = np.maximum(dist, 0)
    max_exact = N_BUCKETS // 2
    ratio = (np.log(np.maximum(n, 1).astype(np.float32) / np.float32(max_exact))
             / np.float32(math.log(MAX_DISTANCE / max_exact)))
    large = np.minimum(max_exact + (ratio * np.float32(N_BUCKETS - max_exact)).astype(np.int32),
                       N_BUCKETS - 1)
    return np.where(n < max_exact, n, large).astype(np.int32)


FAR_BUCKET = int(_t5_bucket_static(np.array([2 * MAX_DISTANCE]))[0])


def _params(sem):
    return pltpu.CompilerParams(dimension_semantics=sem, vmem_limit_bytes=VMEM_LIMIT)


def _proj_kernel(x_ref, g_ref, w_ref, ob_ref, of_ref, k_ref, v_ref, xn_ref):
    tm = x_ref.shape[0]
    j = pl.program_id(1)

    @pl.when(j == 0)
    def _():
        x = x_ref[...]
        ms = jnp.mean(x * x, axis=-1, keepdims=True)
        xn_ref[...] = (x * lax.rsqrt(ms + EPS) * g_ref[...]).astype(BF16)

    def project():
        return lax.dot_general(xn_ref[...], w_ref[...], (((1,), (1,)), ((), ())),
                               preferred_element_type=F32)

    @pl.when(j < KV_TILE)
    def _():
        ob_ref[...] = project().astype(BF16)

    @pl.when(j == KV_TILE)
    def _():
        of_ref[...] = project()[:, :D_HF]
        for g in range(N_KV_HEADS):
            rows = pl.ds(g, tm, stride=N_KV_HEADS)
            k_ref[rows, :] = of_ref[:, F_K + g * HEAD_DIM:F_K + (g + 1) * HEAD_DIM]
            v_ref[rows, :] = of_ref[:, F_V + g * HEAD_DIM:F_V + (g + 1) * HEAD_DIM]


def _proj(x2d, g, w):
    n = x2d.shape[0]
    tm, tn = PROJ_TM, PROJ_TN
    kv_spec = pl.BlockSpec((N_KV_HEADS * tm, HEAD_DIM), lambda i, j: (i, 0))
    kv_shape = jax.ShapeDtypeStruct((N_KV_HEADS * n, HEAD_DIM), F32)
    return pl.pallas_call(
        _proj_kernel,
        grid=(n // tm, D_H // tn),
        in_specs=[pl.BlockSpec((tm, D_MODEL), lambda i, j: (i, 0)),
                  pl.BlockSpec((1, D_MODEL), lambda i, j: (0, 0)),
                  pl.BlockSpec((tn, D_MODEL), lambda i, j: (j, 0))],
        out_specs=[pl.BlockSpec((tm, tn), lambda i, j: (i, jnp.minimum(j, KV_TILE - 1))),
                   pl.BlockSpec((tm, D_HF), lambda i, j: (i, 0)), kv_spec, kv_spec],
        out_shape=[jax.ShapeDtypeStruct((n, D_HB), BF16),
                   jax.ShapeDtypeStruct((n, D_HF), F32), kv_shape, kv_shape],
        scratch_shapes=[pltpu.VMEM((tm, D_MODEL), BF16)],
        compiler_params=_params(("parallel", "arbitrary")),
        name="proj",
    )(x2d, g, w)


def _bias_kernel(rb_ref, bp_ref, bs_ref, op_ref, os_ref):
    bp = bp_ref[...]
    bs = bs_ref[...]
    for h in range(N_HEADS):
        far = rb_ref[FAR_BUCKET, h]
        tp = jnp.zeros(bp.shape, F32)
        ts = jnp.zeros(bs.shape, F32)
        for b in range(N_BUCKETS):
            val = rb_ref[b, h] - far
            tp = jnp.where(bp == b, val * LOG2E, tp)
            ts = jnp.where(bs == b, val, ts)
        op_ref[h] = tp
        os_ref[h] = ts


def _bias_tables(rel_bias):
    key = np.arange(CHUNK)[:, None]
    qry = np.arange(Q_BLOCK)[None, :]
    bucket_p = _t5_bucket_static(CHUNK - Q_BLOCK + qry - key)
    qi = np.arange(DEC_SEQ)[:, None]
    col = np.arange(2 * LANE)[None, :]
    dist_s = np.where(col < LANE, LANE + qi - col, qi - (col - LANE))
    bucket_s = _t5_bucket_static(dist_s)
    return pl.pallas_call(
        _bias_kernel,
        in_specs=[pl.BlockSpec(memory_space=pltpu.SMEM),
                  pl.BlockSpec(memory_space=pltpu.VMEM),
                  pl.BlockSpec(memory_space=pltpu.VMEM)],
        out_specs=[pl.BlockSpec(memory_space=pltpu.VMEM),
                   pl.BlockSpec(memory_space=pltpu.VMEM)],
        out_shape=[jax.ShapeDtypeStruct((N_HEADS, CHUNK, Q_BLOCK), F32),
                   jax.ShapeDtypeStruct((N_HEADS, DEC_SEQ, 2 * LANE), F32)],
        name="bias_tables",
    )(rel_bias, jnp.asarray(bucket_p), jnp.asarray(bucket_s))


def _any(x):
    return jnp.max(jnp.where(x, 1.0, 0.0)) > 0.5


def _rep(x):
    return jnp.broadcast_to(x, (SUBLANE, LANE))


def _tree(parts, comb):
    while len(parts) > 1:
        pairs = [comb(parts[k], parts[k + 1]) for k in range(0, len(parts) - 1, 2)]
        parts = pairs + parts[len(parts) - len(parts) % 2:]
    return parts[0]


def _fold_rows(x, comb):
    return _tree([x[k:k + SUBLANE] for k in range(0, x.shape[0], SUBLANE)], comb)


def _select_threshold(tile_fn, ntiles, static):
    def reduce_tiles(fn, init, comb):
        def step(j, acc):
            x = fn(tile_fn(j), j)
            return comb(acc, _tree([x[k] for k in range(x.shape[0])], comb))
        if static:
            acc = init
            for j in range(ntiles):
                acc = step(j, acc)
            return acc
        return lax.fori_loop(0, ntiles, step, init)

    zeros = jnp.zeros((SUBLANE, LANE), F32)

    def count(pred_fn):
        acc = reduce_tiles(lambda s, j: jnp.where(pred_fn(s, j), 1.0, 0.0), zeros,
                           lambda a, b: a + b)
        return _rep(jnp.sum(acc, axis=0, keepdims=True))

    def masked_max(pred_fn):
        acc = reduce_tiles(lambda s, j: jnp.where(pred_fn(s, j), s, NEG_INF),
                           jnp.full((SUBLANE, LANE), NEG_INF, F32), jnp.maximum)
        return _rep(jnp.max(acc, axis=0, keepdims=True))

    bound = reduce_tiles(lambda s, j: jnp.where(s > NEG_INF, jnp.abs(s), 0.0), zeros,
                         jnp.maximum)
    bound = _rep(jnp.max(bound, axis=0, keepdims=True))

    def bisect(_, carry):
        lo, hi = carry
        mid = 0.5 * lo + 0.5 * hi
        few = count(lambda s, j: s > mid[None]) < TOPK
        return jnp.where(few, lo, mid), jnp.where(few, mid, hi)

    _, hi = lax.fori_loop(0, N_BISECT, bisect, (-bound, bound))

    thr = masked_max(lambda s, j: s <= hi[None])
    n_ge = count(lambda s, j: s >= thr[None])

    def fix_body(carry):
        thr, n_ge, _ = carry
        lower = masked_max(lambda s, j: s < thr[None])
        thr = jnp.where(n_ge < TOPK, lower, thr)
        n_ge = count(lambda s, j: s >= thr[None])
        return thr, n_ge, _any(n_ge < TOPK)

    thr, n_ge, _ = lax.while_loop(lambda c: c[2], fix_body, (thr, n_ge, _any(n_ge < TOPK)))
    n_gt = count(lambda s, j: s > thr[None])
    need = TOPK - n_gt
    return thr[0:1], need[0:1]


def _keep(pred):
    return jnp.where(pred, 0.0, NEG)


def _selection_masks(tiles, thr, need, seen, tri):
    ties = [s == thr for s in tiles]
    ranks = []
    for k in range(0, len(tiles), 2):
        pair = jnp.concatenate([jnp.where(t, 1.0, 0.0).astype(BF16) for t in ties[k:k + 2]], axis=1)
        rank = jnp.dot(tri, pair, preferred_element_type=F32)
        ranks += [rank[:, p * LANE:(p + 1) * LANE] for p in range(len(ties[k:k + 2]))]
    masks = []
    for s, tie, rank in zip(tiles, ties, ranks):
        masks.append(jnp.where(tie, _keep(rank + seen <= need), _keep(s > thr)))
        seen = seen + rank[s.shape[0] - 1:]
    return masks, seen


def _attn_p_kernel(q_ref, qi_ref, wi_ref, k_ref, v_ref, ki_ref, bias_ref, tri_ref, o_ref,
                   kb_ref, vt_ref, kib_ref, qh_ref, qih_ref,
                   score_ref, mask_ref, s_ref):
    i = pl.program_id(1)
    T = Q_BLOCK
    W = GROUP * T
    nch = i // CHUNK_BLOCKS + 1

    @pl.when(i == 0)
    def _():
        kb_ref[0:PAD_KEYS] = jnp.zeros((PAD_KEYS, D_KV), BF16)
        kb_ref[PAD_KEYS:] = k_ref[0].astype(BF16)
        kib_ref[0:PAD_KEYS] = jnp.zeros((PAD_KEYS, IDX_DIM), BF16)
        kib_ref[PAD_KEYS:] = ki_ref[0][:, :IDX_DIM].astype(BF16)
        ones_row = lax.broadcasted_iota(jnp.int32, (VT_ROWS - HEAD_DIM, PAD_KEYS + SEQ), 0) == 0
        for g in range(N_KV_HEADS):
            vt_ref[g, 0:HEAD_DIM, 0:PAD_KEYS] = jnp.zeros((HEAD_DIM, PAD_KEYS), BF16)
            vt_ref[g, HEAD_DIM:VT_ROWS, :] = jnp.where(ones_row, 1.0, 0.0).astype(BF16)
            for c in range(SEQ // LANE):
                blk = v_ref[0, c * LANE:(c + 1) * LANE, g * HEAD_DIM:(g + 1) * HEAD_DIM]
                vt_ref[g, 0:HEAD_DIM,
                       PAD_KEYS + c * LANE:PAD_KEYS + (c + 1) * LANE] = blk.T.astype(BF16)
        score_ref[0:PAD_KEYS] = jnp.full((PAD_KEYS, T), NEG_INF, F32)
        mask_ref[0:PAD_KEYS] = jnp.full((PAD_KEYS, T), NEG, F32)

    q = q_ref[0].astype(F32) * (QK_SCALE * LOG2E)
    for h in range(N_HEADS):
        qh_ref[h] = q[:, h * HEAD_DIM:(h + 1) * HEAD_DIM].astype(BF16)
    qi = qi_ref[0]
    for h in range(IDX_HEADS):
        qih_ref[h] = qi[:, h * IDX_DIM:(h + 1) * IDX_DIM].astype(BF16)
    w_rows = wi_ref[0].T[W_OFF:W_OFF + IDX_HEADS] * SCORE_SCALE

    def span(c):
        return pl.ds(pl.multiple_of((i - CHUNK_BLOCKS * c) * LANE, LANE), CHUNK)

    def first_key(c):
        return (i - CHUNK_BLOCKS * c) * LANE - PAD_KEYS

    key_l = lax.broadcasted_iota(jnp.int32, (CHUNK, T), 0)
    qry = i * T + lax.broadcasted_iota(jnp.int32, (CHUNK, T), 1)

    def score_chunk(c):
        kc = kib_ref[span(c), :]
        d = lax.dot_general(kc, qih_ref[...].reshape(IDX_HEADS * T, IDX_DIM),
                            (((1,), (1,)), ((), ())), preferred_element_type=F32)
        acc = jnp.zeros((CHUNK, T), F32)
        for h in range(IDX_HEADS):
            acc = acc + jnp.maximum(d[:, h * T:(h + 1) * T], 0.0) * w_rows[h:h + 1]
        key = first_key(c) + key_l
        acc = jnp.where(key <= qry, jnp.where(key >= 0, acc, NEG_INF), NEG_INF)
        score_ref[span(c), :] = acc

    nsub = CHUNK // SUBLANE

    def tile_fn(c):
        return score_ref[span(c), :].reshape(nsub, SUBLANE, T)

    def select_all():
        mask_ref[span(0), :] = _keep(score_ref[span(0), :] > NEG_INF)

    def select(n):
        thr, need = _select_threshold(tile_fn, n, static=True)
        seen = jnp.zeros((1, T), F32)
        for c in reversed(range(n)):
            start = pl.multiple_of((i - CHUNK_BLOCKS * c) * LANE, LANE)
            parts = [pl.ds(start + p * MXU_WIDTH, MXU_WIDTH) for p in range(CHUNK // MXU_WIDTH)]
            masks, seen = _selection_masks([score_ref[rows, :] for rows in parts],
                                           thr, need, seen, tri_ref[...])
            for rows, mask in zip(parts, masks):
                mask_ref[rows, :] = mask

    def logits(c, g):
        kc = kb_ref[span(c), g * HEAD_DIM:(g + 1) * HEAD_DIM]
        qg = qh_ref[g * GROUP:(g + 1) * GROUP].reshape(W, HEAD_DIM)
        s = lax.dot_general(kc, qg, (((1,), (1,)), ((), ())), preferred_element_type=F32)
        mb = mask_ref[span(c), :]
        if c == 0:
            add = jnp.concatenate([mb + bias_ref[g * GROUP + hq] for hq in range(GROUP)], axis=1)
        else:
            add = jnp.concatenate([mb] * GROUP, axis=1)
        s = s + add
        s_ref[span(c), g * W:(g + 1) * W] = s
        return jnp.max(_fold_rows(s, jnp.maximum), axis=0, keepdims=True)

    def run(n):
        for c in range(n):
            score_chunk(c)
        if n == 1:
            pl.when(i * T + T <= TOPK)(select_all)
            pl.when(i * T + T > TOPK)(functools.partial(select, n))
        else:
            select(n)
        for g in range(N_KV_HEADS):
            m = functools.reduce(jnp.maximum, [logits(c, g) for c in range(n)])
            acc = None
            for c in range(n):
                p = jnp.exp2((s_ref[span(c), g * W:(g + 1) * W] - m).astype(BF16))
                pv = jnp.dot(vt_ref[g, :, span(c)], p, preferred_element_type=F32)
                acc = pv if acc is None else acc + pv
            o = acc[0:HEAD_DIM] / acc[HEAD_DIM:HEAD_DIM + 1]
            for hq in range(GROUP):
                h = g * GROUP + hq
                o_ref[0, :, h * HEAD_DIM:(h + 1) * HEAD_DIM] = o[:, hq * T:(hq + 1) * T].T

    for n in range(1, SEQ // CHUNK + 1):
        pl.when(nch == n)(functools.partial(run, n))


def _attn_q_kernel(q_ref, qi_ref, wi_ref, k_ref, v_ref, ki_ref, bias_ref, tri_ref, o_ref,
                   kb_ref, vt_ref, kib_ref, qh_ref, qih_ref, score_ref, mask_ref, s_ref):
    i = pl.program_id(1)
    T = Q_BLOCK
    W = GROUP * T

    @pl.when(i == 0)
    def _():
        kb_ref[...] = k_ref[0].astype(BF16)
        kib_ref[...] = ki_ref[0][:, :IDX_DIM].astype(BF16)
        ones_row = lax.broadcasted_iota(jnp.int32, (VT_ROWS - HEAD_DIM, SEQ), 0) == 0
        for g in range(N_KV_HEADS):
            vt_ref[g, HEAD_DIM:VT_ROWS, :] = jnp.where(ones_row, 1.0, 0.0).astype(BF16)
            for c in range(SEQ // LANE):
                blk = v_ref[0, c * LANE:(c + 1) * LANE, g * HEAD_DIM:(g + 1) * HEAD_DIM]
                vt_ref[g, 0:HEAD_DIM, c * LANE:(c + 1) * LANE] = blk.T.astype(BF16)

    q = q_ref[0].astype(F32) * (QK_SCALE * LOG2E)
    for h in range(N_HEADS):
        qh_ref[h] = q[:, h * HEAD_DIM:(h + 1) * HEAD_DIM].astype(BF16)
    qi = qi_ref[0]
    for h in range(IDX_HEADS):
        qih_ref[h] = qi[:, h * IDX_DIM:(h + 1) * IDX_DIM].astype(BF16)
    w_rows = wi_ref[0].T[W_OFF:W_OFF + IDX_HEADS] * SCORE_SCALE

    def run(blk):
        chunks = [(max(e - CHUNK_BLOCKS, 0), e) for e in range(blk + 1, 0, -CHUNK_BLOCKS)]

        def rows(chunk):
            return slice(chunk[0] * LANE, chunk[1] * LANE)

        for chunk in chunks:
            nk = (chunk[1] - chunk[0]) * LANE
            d = lax.dot_general(kib_ref[rows(chunk), :],
                                qih_ref[...].reshape(IDX_HEADS * T, IDX_DIM),
                                (((1,), (1,)), ((), ())), preferred_element_type=F32)
            acc = jnp.zeros((nk, T), F32)
            for h in range(IDX_HEADS):
                acc = acc + jnp.maximum(d[:, h * T:(h + 1) * T], 0.0) * w_rows[h:h + 1]
            if chunk[1] == blk + 1:
                key = chunk[0] * LANE + lax.broadcasted_iota(jnp.int32, (nk, T), 0)
                qry = blk * T + lax.broadcasted_iota(jnp.int32, (nk, T), 1)
                acc = jnp.where(key <= qry, acc, NEG_INF)
            score_ref[rows(chunk), :] = acc

        if (blk + 1) * T <= TOPK:
            for chunk in chunks:
                mask_ref[rows(chunk), :] = _keep(score_ref[rows(chunk), :] > NEG_INF)
        else:
            def tile_fn(c):
                nk = (chunks[c][1] - chunks[c][0]) * LANE
                return score_ref[rows(chunks[c]), :].reshape(nk // SUBLANE, SUBLANE, T)

            thr, need = _select_threshold(tile_fn, len(chunks), static=True)
            blocks = [slice(b * LANE, (b + 1) * LANE) for b in range(blk + 1)]
            masks, _ = _selection_masks([score_ref[r, :] for r in blocks], thr, need,
                                        jnp.zeros((1, T), F32), tri_ref[...])
            for r, mask in zip(blocks, masks):
                mask_ref[r, :] = mask

        for g in range(N_KV_HEADS):
            qg = qh_ref[g * GROUP:(g + 1) * GROUP].reshape(W, HEAD_DIM)
            tops = []
            for chunk in chunks:
                nk = (chunk[1] - chunk[0]) * LANE
                s = lax.dot_general(kb_ref[rows(chunk), g * HEAD_DIM:(g + 1) * HEAD_DIM], qg,
                                    (((1,), (1,)), ((), ())), preferred_element_type=F32)
                mb = mask_ref[rows(chunk), :]
                if chunk[1] == blk + 1:
                    add = jnp.concatenate(
                        [mb + bias_ref[g * GROUP + hq, CHUNK - nk:CHUNK] for hq in range(GROUP)],
                        axis=1)
                else:
                    add = jnp.concatenate([mb] * GROUP, axis=1)
                s = s + add
                s_ref[rows(chunk), g * W:(g + 1) * W] = s
                tops.append(jnp.max(_fold_rows(s, jnp.maximum), axis=0, keepdims=True))
            m = functools.reduce(jnp.maximum, tops)
            acc = None
            for chunk in chunks:
                p = jnp.exp2((s_ref[rows(chunk), g * W:(g + 1) * W] - m).astype(BF16))
                pv = jnp.dot(vt_ref[g, :, rows(chunk)], p, preferred_element_type=F32)
                acc = pv if acc is None else acc + pv
            o = acc[0:HEAD_DIM] / acc[HEAD_DIM:HEAD_DIM + 1]
            for hq in range(GROUP):
                h = g * GROUP + hq
                o_ref[0, :, h * HEAD_DIM:(h + 1) * HEAD_DIM] = o[:, hq * T:(hq + 1) * T].T

    for blk in range(SEQ // T):
        pl.when(i == blk)(functools.partial(run, blk))


def _attn_prompt(hb_p, hf_p, bias_p):
    hb3 = hb_p.reshape(BATCH, SEQ, D_HB)
    hf3 = hf_p.reshape(BATCH, SEQ, D_HF)
    nqb = SEQ // Q_BLOCK
    T = Q_BLOCK
    qi_w = IDX_HEADS * IDX_DIM
    return pl.pallas_call(
        _attn_q_kernel,
        grid=(BATCH, nqb),
        in_specs=[
            pl.BlockSpec((1, T, D_ATTN), lambda b, i: (b, i, C_Q // D_ATTN)),
            pl.BlockSpec((1, T, qi_w), lambda b, i: (b, i, F_QI // qi_w)),
            pl.BlockSpec((1, T, LANE), lambda b, i: (b, i, F_KI // LANE)),
            pl.BlockSpec((1, SEQ, D_KV), lambda b, i: (b, 0, F_K // D_KV)),
            pl.BlockSpec((1, SEQ, D_KV), lambda b, i: (b, 0, F_V // D_KV)),
            pl.BlockSpec((1, SEQ, LANE), lambda b, i: (b, 0, F_KI // LANE)),
            pl.BlockSpec((N_HEADS, CHUNK, T), lambda b, i: (0, 0, 0)),
            pl.BlockSpec((LANE, LANE), lambda b, i: (0, 0)),
        ],
        out_specs=pl.BlockSpec((1, T, D_ATTN), lambda b, i: (b, i, 0)),
        out_shape=jax.ShapeDtypeStruct((BATCH, SEQ, D_ATTN), F32),
        scratch_shapes=[
            pltpu.VMEM((SEQ, D_KV), BF16),
            pltpu.VMEM((N_KV_HEADS, VT_ROWS, SEQ), BF16),
            pltpu.VMEM((SEQ, IDX_DIM), BF16),
            pltpu.VMEM((N_HEADS, T, HEAD_DIM), BF16),
            pltpu.VMEM((IDX_HEADS, T, IDX_DIM), BF16),
            pltpu.VMEM((SEQ, T), F32),
            pltpu.VMEM((SEQ, T), F32),
            pltpu.VMEM((SEQ, N_HEADS * T), F32),
        ],
        compiler_params=_params(("parallel", "arbitrary")),
        name="attn_prompt",
    )(hb3, hf3, hf3, hf3, hf3, hf3, bias_p, jnp.tri(LANE, dtype=BF16))


def _score_s_kernel(pt_ref, qi_ref, w_ref, kin_ref, *rest):
    npg = SCORE_ROWS * N_PAGES
    kip = rest[0:npg]
    o_ref = rest[npg]
    kinp_ref = rest[npg + 1]
    del pt_ref
    R = DEC_SEQ
    qrow = lax.broadcasted_iota(jnp.int32, (R, LANE), 0)
    lane = lax.broadcasted_iota(jnp.int32, (R, LANE), 1)
    kinp_ref[...] = jnp.zeros(kinp_ref.shape, BF16)
    for r in range(SCORE_ROWS):
        kinp_ref[r, 0:2 * R] = jnp.concatenate(
            [kin_ref[r][:, :IDX_DIM], jnp.zeros((R, IDX_DIM), F32)], 0).astype(BF16)
    for r in range(SCORE_ROWS):
        qi = qi_ref[r].astype(BF16)
        wb = jnp.broadcast_to(w_ref[r] * SCORE_SCALE, (IDX_HEADS * R, LANE))
        for t in range(N_PAGES + 1):
            if t < N_PAGES:
                d = jnp.dot(qi, kip[r * N_PAGES + t][0].astype(BF16), preferred_element_type=F32)
            else:
                d = lax.dot_general(qi, kinp_ref[r], (((1,), (1,)), ((), ())),
                                    preferred_element_type=F32)
            e = (jnp.maximum(d, 0.0) * wb).reshape(IDX_HEADS, R, LANE)
            s = e[0]
            for h in range(1, IDX_HEADS):
                s = s + e[h]
            if t == N_PAGES:
                s = jnp.where(lane <= qrow, s, NEG_INF)
            o_ref[r, :, t * LANE:(t + 1) * LANE] = s


def _score_sample(hf_s, page_table, cache_kidx_t):
    R = DEC_SEQ
    G = SCORE_ROWS
    h3 = hf_s.reshape(DEC_BATCH, R, D_HF)
    qi_hq = h3[:, :, F_QI:F_QI + IDX_HEADS * IDX_DIM].reshape(DEC_BATCH, R, IDX_HEADS, IDX_DIM)
    qi_hq = qi_hq.transpose(0, 2, 1, 3).reshape(DEC_BATCH, IDX_HEADS * R, IDX_DIM)
    w_hq = h3[:, :, F_KI + W_OFF:F_KI + W_OFF + IDX_HEADS].transpose(0, 2, 1)
    w_hq = w_hq.reshape(DEC_BATCH, IDX_HEADS * R, 1)
    in_specs = [
        pl.BlockSpec((G, IDX_HEADS * R, IDX_DIM), lambda b, pt: (b, 0, 0)),
        pl.BlockSpec((G, IDX_HEADS * R, 1), lambda b, pt: (b, 0, 0)),
        pl.BlockSpec((G, R, LANE), lambda b, pt: (b, 0, F_KI // LANE)),
    ]
    in_specs += [pl.BlockSpec((1, IDX_DIM, PAGE_SIZE),
                              lambda b, pt, r=r, p=p: (pt[b * G + r, p], 0, 0))
                 for r in range(G) for p in range(N_PAGES)]
    grid_spec = pltpu.PrefetchScalarGridSpec(
        num_scalar_prefetch=1,
        grid=(DEC_BATCH // G,),
        in_specs=in_specs,
        out_specs=pl.BlockSpec((G, R, L_SAMPLE), lambda b, pt: (b, 0, 0)),
        scratch_shapes=[pltpu.VMEM((G, PAGE_SIZE, IDX_DIM), BF16)],
    )
    return pl.pallas_call(
        _score_s_kernel,
        grid_spec=grid_spec,
        out_shape=jax.ShapeDtypeStruct((DEC_BATCH, R, L_SAMPLE), F32),
        compiler_params=_params(("arbitrary",)),
        name="score_sample",
    )(page_table, qi_hq, w_hq, h3, *([cache_kidx_t] * (G * N_PAGES)))


def _select_s_kernel(s_ref, tri_ref, o_ref):
    nt = N_PAGES + 1

    def tile_fn(j):
        return s_ref[j * LANE:(j + 1) * LANE, :].reshape(SUBTILES, SUBLANE, LANE)

    thr, need = _select_threshold(tile_fn, nt, static=True)
    parts = [slice(j * LANE, (j + 1) * LANE) for j in range(nt)]
    masks, _ = _selection_masks([s_ref[rows, :] for rows in parts], thr, need,
                                jnp.zeros((1, LANE), F32), tri_ref[...])
    for rows, mask in zip(parts, masks):
        o_ref[rows, :] = mask


def _select_sample(scores_t):
    n = scores_t.shape[1]
    return pl.pallas_call(
        _select_s_kernel,
        grid=(n // LANE,),
        in_specs=[pl.BlockSpec((L_SAMPLE, LANE), lambda c: (0, c)),
                  pl.BlockSpec((LANE, LANE), lambda c: (0, 0))],
        out_specs=pl.BlockSpec((L_SAMPLE, LANE), lambda c: (0, c)),
        out_shape=jax.ShapeDtypeStruct((L_SAMPLE, n), F32),
        compiler_params=_params(("parallel",)),
        name="select_sample",
    )(scores_t, jnp.tri(LANE, dtype=BF16))


def _attn_s_kernel(pt_ref, q_ref, kn_ref, vn_ref, mask_ref, bias_ref, *rest):
    npg = ATTN_ROWS * N_PAGES
    kp = rest[0:npg]
    vp = rest[npg:2 * npg]
    o_ref = rest[2 * npg]
    knp_ref, vnp_ref, logit_ref = rest[2 * npg + 1:]
    del pt_ref
    R = DEC_SEQ
    NT = N_PAGES + 1
    GR = GROUP * R

    knp_ref[...] = jnp.zeros(knp_ref.shape, BF16)
    vnp_ref[...] = jnp.zeros(vnp_ref.shape, BF16)
    for r in range(ATTN_ROWS):
        knp_ref[r, 0:2 * R] = jnp.concatenate([kn_ref[r], jnp.zeros((R, D_KV), F32)], 0).astype(BF16)
        vnp_ref[r, 0:2 * R] = jnp.concatenate([vn_ref[r], jnp.zeros((R, D_KV), F32)], 0).astype(BF16)

    def page_head(refs, pad_ref, r, t, g):
        if t < N_PAGES:
            return refs[r * N_PAGES + t][pl.ds(g, PAGE_SIZE, stride=N_KV_HEADS), :].astype(BF16)
        return pad_ref[r, :, g * HEAD_DIM:(g + 1) * HEAD_DIM]

    for r in range(ATTN_ROWS):
        q = (q_ref[r].astype(F32) * QK_SCALE).astype(BF16)
        for t in range(NT):
            mb = mask_ref[r, :, t * LANE:(t + 1) * LANE]
            for g in range(N_KV_HEADS):
                lg = lax.dot_general(q[g * GR:(g + 1) * GR], page_head(kp, knp_ref, r, t, g),
                                     (((1,), (1,)), ((), ())), preferred_element_type=F32)
                lg = lg.reshape(GROUP, R, LANE) + mb[None]
                if t >= N_PAGES - 1:
                    off = (t - (N_PAGES - 1)) * LANE
                    lg = lg + bias_ref[g * GROUP:(g + 1) * GROUP, :, off:off + LANE]
                logit_ref[r, g * GR:(g + 1) * GR, t * LANE:(t + 1) * LANE] = lg.reshape(GR, LANE)

    for r in range(ATTN_ROWS):
        logits = logit_ref[r]
        m = jnp.max(logits, axis=1, keepdims=True)
        p = jnp.exp(logits - m)
        inv = 1.0 / jnp.sum(p, axis=1, keepdims=True)
        pb = p.astype(BF16)
        outs = [jnp.zeros((GR, HEAD_DIM), F32) for _ in range(N_KV_HEADS)]
        for t in range(NT):
            for g in range(N_KV_HEADS):
                outs[g] = outs[g] + jnp.dot(pb[g * GR:(g + 1) * GR, t * LANE:(t + 1) * LANE],
                                            page_head(vp, vnp_ref, r, t, g),
                                            preferred_element_type=F32)
        for g in range(N_KV_HEADS):
            o = outs[g] * inv[g * GR:(g + 1) * GR]
            for hq in range(GROUP):
                h = g * GROUP + hq
                o_ref[r, :, h * HEAD_DIM:(h + 1) * HEAD_DIM] = o[hq * R:(hq + 1) * R]


def _attn_sample(hb_s, hf_s, page_table, cache_k, cache_v, mask, bias_s):
    R = DEC_SEQ
    G = ATTN_ROWS
    h3 = hf_s.reshape(DEC_BATCH, R, D_HF)
    q_hq = hb_s[:, C_Q:C_Q + D_ATTN].reshape(DEC_BATCH, R, N_HEADS, HEAD_DIM)
    q_hq = q_hq.transpose(0, 2, 1, 3).reshape(DEC_BATCH, N_HEADS * R, HEAD_DIM)
    rows_per_page = PAGE_SIZE * N_KV_HEADS
    ck = cache_k.reshape(-1, HEAD_DIM)
    cv = cache_v.reshape(-1, HEAD_DIM)

    in_specs = [
        pl.BlockSpec((G, N_HEADS * R, HEAD_DIM), lambda b, pt: (b, 0, 0)),
        pl.BlockSpec((G, R, D_KV), lambda b, pt: (b, 0, F_K // D_KV)),
        pl.BlockSpec((G, R, D_KV), lambda b, pt: (b, 0, F_V // D_KV)),
        pl.BlockSpec((G, R, L_SAMPLE), lambda b, pt: (b, 0, 0)),
        pl.BlockSpec((N_HEADS, R, 2 * LANE), lambda b, pt: (0, 0, 0)),
    ]
    pages = [pl.BlockSpec((rows_per_page, HEAD_DIM), lambda b, pt, r=r, p=p: (pt[b * G + r, p], 0))
             for r in range(G) for p in range(N_PAGES)]
    in_specs += pages + pages
    grid_spec = pltpu.PrefetchScalarGridSpec(
        num_scalar_prefetch=1,
        grid=(DEC_BATCH // G,),
        in_specs=in_specs,
        out_specs=pl.BlockSpec((G, R, D_ATTN), lambda b, pt: (b, 0, 0)),
        scratch_shapes=[
            pltpu.VMEM((G, PAGE_SIZE, D_KV), BF16),
            pltpu.VMEM((G, PAGE_SIZE, D_KV), BF16),
            pltpu.VMEM((G, N_HEADS * R, L_SAMPLE), F32),
        ],
    )
    return pl.pallas_call(
        _attn_s_kernel,
        grid_spec=grid_spec,
        out_shape=jax.ShapeDtypeStruct((DEC_BATCH, R, D_ATTN), F32),
        compiler_params=_params(("arbitrary",)),
        name="attn_sample",
    )(page_table, q_hq, h3, h3, mask, bias_s, *([ck] * (G * N_PAGES)), *([cv] * (G * N_PAGES)))


def _conv_p_kernel(val_ref, gate_ref, cw_ref, cb_ref, dw_ref, ut_ref, pad_ref):
    pad_ref[0:PAD_ROWS] = jnp.zeros((PAD_ROWS, LANE), F32)
    pad_ref[PAD_ROWS:] = val_ref[0].astype(F32) * jax.nn.sigmoid(gate_ref[0].astype(F32))
    ut_ref[0] = pad_ref[SEQ:SEQ + PAD_ROWS]
    cw = cw_ref[...]
    cb = cb_ref[...]
    first = PAD_ROWS - (CONV_WIDTH - 1)
    for c in range(SEQ // CONV_CHUNK):
        base = c * CONV_CHUNK
        acc = jnp.broadcast_to(cb, (CONV_CHUNK, LANE))
        for r in range(SUBLANE):
            taps = [w for w in range(CONV_WIDTH) if (first + w) % SUBLANE == r]
            span = max(first + w - r for w in taps) + CONV_CHUNK
            win = pad_ref[base + r:base + r + span]
            for w in taps:
                a = first + w - r
                acc = acc + win[a:a + CONV_CHUNK] * cw[w:w + 1]
        dw_ref[0, base:base + CONV_CHUNK] = acc


def _conv_prompt(hb_p, cw_pad, cb):
    h3 = hb_p.reshape(BATCH, SEQ, D_HB)
    nc = D_CONV // LANE
    return pl.pallas_call(
        _conv_p_kernel,
        grid=(BATCH, nc),
        in_specs=[pl.BlockSpec((1, SEQ, LANE), lambda b, c: (b, 0, C_GV // LANE + c)),
                  pl.BlockSpec((1, SEQ, LANE), lambda b, c: (b, 0, C_GG // LANE + c)),
                  pl.BlockSpec((PAD_ROWS, LANE), lambda b, c: (0, c)),
                  pl.BlockSpec((1, LANE), lambda b, c: (0, c))],
        out_specs=[pl.BlockSpec((1, SEQ, LANE), lambda b, c: (b, 0, c)),
                   pl.BlockSpec((1, PAD_ROWS, LANE), lambda b, c: (b, 0, c))],
        out_shape=[jax.ShapeDtypeStruct((BATCH, SEQ, D_CONV), F32),
                   jax.ShapeDtypeStruct((BATCH, PAD_ROWS, D_CONV), F32)],
        scratch_shapes=[pltpu.VMEM((PAD_ROWS + SEQ, LANE), F32)],
        compiler_params=_params(("parallel", "parallel")),
        name="conv_prompt",
    )(h3, h3, cw_pad, cb)


def _conv_s_kernel(val_ref, gate_ref, st_ref, cw_ref, cb_ref, dw_ref, ns_ref):
    R = DEC_SEQ
    H = CONV_WIDTH - 1
    cw = cw_ref[...]
    cb = jnp.broadcast_to(cb_ref[...], (DEC_BATCH, LANE))
    u = []
    for q in range(R):
        rows = pl.ds(q, DEC_BATCH, stride=R)
        u.append(val_ref[rows, :] * jax.nn.sigmoid(gate_ref[rows, :]))

    def row(r):
        return st_ref[r] if r < H else u[r - H]

    for q in range(R):
        acc = cb
        for w in range(CONV_WIDTH):
            acc = acc + row(q + w) * cw[w:w + 1]
        dw_ref[pl.ds(q, DEC_BATCH, stride=R), :] = acc
    for r in range(H):
        ns_ref[r] = row(r + R)


def _conv_sample(val, gate, state_t, cw_pad, cb):
    n_s = DEC_BATCH * DEC_SEQ
    H = CONV_WIDTH - 1
    nc = D_CONV // LANE
    return pl.pallas_call(
        _conv_s_kernel,
        grid=(nc,),
        in_specs=[pl.BlockSpec((n_s, LANE), lambda c: (0, c)),
                  pl.BlockSpec((n_s, LANE), lambda c: (0, c)),
                  pl.BlockSpec((H, DEC_BATCH, LANE), lambda c: (0, 0, c)),
                  pl.BlockSpec((PAD_ROWS, LANE), lambda c: (0, c)),
                  pl.BlockSpec((1, LANE), lambda c: (0, c))],
        out_specs=[pl.BlockSpec((n_s, LANE), lambda c: (0, c)),
                   pl.BlockSpec((H, DEC_BATCH, LANE), lambda c: (0, 0, c))],
        out_shape=[jax.ShapeDtypeStruct((n_s, D_CONV), F32),
                   jax.ShapeDtypeStruct((H, DEC_BATCH, D_CONV), F32)],
        compiler_params=_params(("parallel",)),
        name="conv_sample",
    )(val, gate, state_t, cw_pad, cb)


def _tail_kernel(attn_ref, za_ref, dw_ref, zc_ref, ga_ref, gc_ref, x_ref,
                 wua_ref, wpw_ref, wuc_ref, wo_ref, ng_ref, nb_ref, bpw_ref, fg_ref, y_ref):
    a = attn_ref[...] * jax.nn.silu(za_ref[...].astype(F32))
    branch_attn = jnp.dot(a.astype(BF16), wua_ref[...], preferred_element_type=F32)

    dw = dw_ref[...]
    mu = jnp.mean(dw, axis=-1, keepdims=True)
    var = jnp.mean(jnp.square(dw - mu), axis=-1, keepdims=True)
    ln = (dw - mu) * lax.rsqrt(var + EPS) * ng_ref[...] + nb_ref[...]
    conv_out = jnp.dot(jax.nn.silu(ln).astype(BF16), wpw_ref[...],
                       preferred_element_type=F32) + bpw_ref[...]
    c = conv_out * jax.nn.silu(zc_ref[...].astype(F32))
    branch_conv = jnp.dot(c.astype(BF16), wuc_ref[...], preferred_element_type=F32)

    merged = (jax.nn.sigmoid(ga_ref[...].astype(F32)) * branch_attn
              + jax.nn.sigmoid(gc_ref[...].astype(F32)) * branch_conv)
    y = x_ref[...] + jnp.dot(merged.astype(BF16), wo_ref[...], preferred_element_type=F32)
    ms = jnp.mean(y * y, axis=-1, keepdims=True)
    y_ref[...] = y * lax.rsqrt(ms + EPS) * fg_ref[...]


def _tail(h, attn, dw, x2d, weights, tm):
    n = x2d.shape[0]

    def const(shape):
        return pl.BlockSpec(shape, lambda i: (0, 0), pipeline_mode=pl.Buffered(1))

    def cols(width, offset):
        return pl.BlockSpec((tm, width), lambda i: (i, offset // width))

    return pl.pallas_call(
        _tail_kernel,
        grid=(n // tm,),
        in_specs=[cols(D_ATTN, 0), cols(D_ATTN, C_ZA), cols(D_CONV, 0), cols(D_CONV, C_ZC),
                  cols(D_MODEL, C_GA), cols(D_MODEL, C_GC), cols(D_MODEL, 0),
                  const((D_ATTN, D_MODEL)), const((D_CONV, D_CONV)),
                  const((D_CONV, D_MODEL)), const((D_MODEL, D_MODEL)),
                  const((1, D_CONV)), const((1, D_CONV)), const((1, D_CONV)),
                  const((1, D_MODEL))],
        out_specs=pl.BlockSpec((tm, D_MODEL), lambda i: (i, 0)),
        out_shape=jax.ShapeDtypeStruct((n, D_MODEL), F32),
        compiler_params=_params(("parallel",)),
        name="tail",
    )(attn, h, dw, h, h, h, x2d, *weights)


def _reorder_kernel(off_ref, w_ref, o_ref):
    del off_ref
    o_ref[...] = w_ref[...].astype(BF16)


def _reorder_w_in(w_in):
    w_t = w_in.T
    src = []
    for first, dest, width in (('gate_attn', C_GA, 2 * D_MODEL), ('q', C_Q, D_ATTN),
                               ('z_attn', C_ZA, D_ATTN), ('glu_val', C_GV, 3 * D_CONV),
                               ('k', C_K, 2 * D_KV), ('q_idx', C_QI, D_H - C_QI)):
        assert dest == len(src) * REORDER_ROWS
        for r in range(0, width, REORDER_ROWS):
            src.append(min(_SRC[first][0] + r, D_IN - REORDER_ROWS) // SUBLANE)
    grid_spec = pltpu.PrefetchScalarGridSpec(
        num_scalar_prefetch=1,
        grid=(len(src),),
        in_specs=[pl.BlockSpec((pl.Element(REORDER_ROWS), pl.Element(D_MODEL)),
                               lambda d, off: (off[d] * SUBLANE, 0))],
        out_specs=pl.BlockSpec((REORDER_ROWS, D_MODEL), lambda d, off: (d, 0)),
    )
    return pl.pallas_call(
        _reorder_kernel,
        grid_spec=grid_spec,
        out_shape=jax.ShapeDtypeStruct((D_H, D_MODEL), BF16),
        compiler_params=_params(("arbitrary",)),
        name="reorder_w_in",
    )(jnp.asarray(src, jnp.int32), w_t)


def kernel(x_prompt, x_sample, cache_k, cache_v, cache_kidx, state_conv, page_table,
           ln_g, w_in, conv_w, conv_b, conv_norm_g, conv_norm_b, w_pw, b_pw,
           w_up_attn, w_up_conv, w_out, rel_bias, final_g):
    w_all = _reorder_w_in(w_in[0])
    g_in = ln_g[0].reshape(1, D_MODEL)
    xp = x_prompt.reshape(BATCH * SEQ, D_MODEL)
    xs = x_sample.reshape(DEC_BATCH * DEC_SEQ, D_MODEL)
    hb_p, hf_p, k_p, v_p = _proj(xp, g_in, w_all)
    hb_s, hf_s, k_s, v_s = _proj(xs, g_in, w_all)

    bias_p, bias_s = _bias_tables(rel_bias)

    attn_p = _attn_prompt(hb_p, hf_p, bias_p).reshape(BATCH * SEQ, D_ATTN)

    n_s = DEC_BATCH * DEC_SEQ
    scores = _score_sample(hf_s, page_table, cache_kidx[0].transpose(0, 2, 1))
    mask_t = _select_sample(scores.reshape(n_s, L_SAMPLE).T)
    mask = mask_t.T.reshape(DEC_BATCH, DEC_SEQ, L_SAMPLE)
    attn_s = _attn_sample(hb_s, hf_s, page_table, cache_k[0], cache_v[0], mask, bias_s)
    attn_s = attn_s.reshape(n_s, D_ATTN)

    cw_pad = jnp.concatenate([conv_w[0], jnp.zeros((PAD_ROWS - CONV_WIDTH, D_CONV), F32)], 0)
    cb = conv_b[0].reshape(1, D_CONV)
    dw_p, u_tail = _conv_prompt(hb_p, cw_pad, cb)
    dw_s, state_new = _conv_sample(hb_s[:, C_GV:C_GV + D_CONV].astype(F32),
                                   hb_s[:, C_GG:C_GG + D_CONV].astype(F32),
                                   state_conv[0].transpose(1, 0, 2), cw_pad, cb)

    wua = w_up_attn[0].astype(BF16)
    wpw = w_pw[0].astype(BF16)
    wuc = w_up_conv[0].astype(BF16)
    wo = w_out[0].astype(BF16)
    ng = conv_norm_g[0].reshape(1, D_CONV)
    nb = conv_norm_b[0].reshape(1, D_CONV)
    bpw = b_pw[0].reshape(1, D_CONV)
    fg = final_g.reshape(1, D_MODEL)
    weights = (wua, wpw, wuc, wo, ng, nb, bpw, fg)
    y_p = _tail(hb_p, attn_p, dw_p.reshape(BATCH * SEQ, D_CONV), xp, weights, TAIL_TM)
    y_s = _tail(hb_s, attn_s, dw_s, xs, weights, TAIL_TM)

    tail_rows = CONV_WIDTH - 1
    return (
        y_p.reshape(BATCH, SEQ, D_MODEL),
        y_s.reshape(DEC_BATCH, DEC_SEQ, D_MODEL),
        k_p.reshape(1, BATCH, SEQ, N_KV_HEADS, HEAD_DIM),
        v_p.reshape(1, BATCH, SEQ, N_KV_HEADS, HEAD_DIM),
        hf_p[:, F_KI:F_KI + IDX_DIM].reshape(1, BATCH, SEQ, IDX_DIM),
        u_tail[:, PAD_ROWS - tail_rows:].reshape(1, BATCH, tail_rows, D_CONV),
        k_s.reshape(1, DEC_BATCH, DEC_SEQ, N_KV_HEADS, HEAD_DIM),
        v_s.reshape(1, DEC_BATCH, DEC_SEQ, N_KV_HEADS, HEAD_DIM),
        hf_s[:, F_KI:F_KI + IDX_DIM].reshape(1, DEC_BATCH, DEC_SEQ, IDX_DIM),
        state_new.transpose(1, 0, 2).reshape(1, DEC_BATCH, tail_rows, D_CONV),
    )
```

```python
import functools
import math

import numpy as np
import jax
import jax.numpy as jnp
from jax import lax
from jax.experimental import pallas as pl
from jax.experimental.pallas import tpu as pltpu

F32 = jnp.float32
BF16 = jnp.bfloat16

D_MODEL = 2048
BATCH = 8
SEQ = 2048
DEC_BATCH = 128
DEC_SEQ = 8
PAST_LEN = 2048
PAGE_SIZE = 128
N_PAGES = PAST_LEN // PAGE_SIZE
N_HEADS = 8
N_KV_HEADS = 2
HEAD_DIM = 128
GROUP = N_HEADS // N_KV_HEADS
D_ATTN = N_HEADS * HEAD_DIM
D_KV = N_KV_HEADS * HEAD_DIM
IDX_HEADS = 8
IDX_DIM = 64
TOPK = 256
Q_BLOCK = 128
N_BUCKETS = 32
MAX_DISTANCE = 128
D_CONV = D_MODEL // 2
CONV_WIDTH = 31
EPS = 1e-6
NEG = -1e30
NEG_INF = float("-inf")

LANE = 128
SUBLANE = 8
VMEM_LIMIT = 56 * 1024 * 1024

C_GA = 0
C_GC = C_GA + D_MODEL
C_Q = C_GC + D_MODEL
C_ZA = C_Q + D_ATTN
C_GV = C_ZA + D_ATTN
C_GG = C_GV + D_CONV
C_ZC = C_GG + D_CONV
C_K = C_ZC + D_CONV
C_V = C_K + D_KV
C_QI = C_V + D_KV
C_KI = C_QI + IDX_HEADS * IDX_DIM
W_OFF = IDX_DIM
MXU_WIDTH = 256
PROJ_TM = 1024
PROJ_TN = 6 * MXU_WIDTH
D_H = -(-(C_KI + LANE) // PROJ_TN) * PROJ_TN
KV_TILE = C_K // PROJ_TN
assert C_K % PROJ_TN == 0 and D_H == C_K + PROJ_TN
D_HB = C_K
D_HF = C_KI + LANE - C_K
F_K = C_K - D_HB
F_V = C_V - D_HB
F_QI = C_QI - D_HB
F_KI = C_KI - D_HB

_SRC = {}
_off = 0
for _name, _w in (('q', D_ATTN), ('k', D_KV), ('v', D_KV), ('z_attn', D_ATTN),
                  ('q_idx', IDX_HEADS * IDX_DIM), ('k_idx', IDX_DIM), ('w_idx', IDX_HEADS),
                  ('glu_val', D_CONV), ('glu_gate', D_CONV), ('z_conv', D_CONV),
                  ('gate_attn', D_MODEL), ('gate_conv', D_MODEL)):
    _SRC[_name] = (_off, _w)
    _off += _w
D_IN = _off
REORDER_ROWS = 512

N_BISECT = 20
PAD_ROWS = 32
CONV_CHUNK = 64
SCORE_SCALE = (IDX_DIM ** -0.5) * (IDX_HEADS ** -0.5)
QK_SCALE = HEAD_DIM ** -0.5
LOG2E = math.log2(math.e)
L_SAMPLE = (N_PAGES + 1) * LANE
SUBTILES = LANE // SUBLANE
CHUNK_BLOCKS = 4
CHUNK = CHUNK_BLOCKS * Q_BLOCK
PAD_KEYS = CHUNK - Q_BLOCK
VT_ROWS = HEAD_DIM + 2 * SUBLANE
SCORE_ROWS = 4
ATTN_ROWS = 4
TAIL_TM = 256


def _t5_bucket_static(dist):
    n = np.maximum(dist, 0)
    max_exact = N_BUCKETS // 2
    ratio = (np.log(np.maximum(n, 1).astype(np.float32) / np.float32(max_exact))
             / np.float32(math.log(MAX_DISTANCE / max_exact)))
    large = np.minimum(max_exact + (ratio * np.float32(N_BUCKETS - max_exact)).astype(np.int32),
                       N_BUCKETS - 1)
    return np.where(n < max_exact, n, large).astype(np.int32)


FAR_BUCKET = int(_t5_bucket_static(np.array([2 * MAX_DISTANCE]))[0])


def _params(sem):
    return pltpu.CompilerParams(dimension_semantics=sem, vmem_limit_bytes=VMEM_LIMIT)


def _proj_kernel(x_ref, g_ref, w_ref, ob_ref, of_ref, k_ref, v_ref, xn_ref):
    tm = x_ref.shape[0]
    j = pl.program_id(1)

    @pl.when(j == 0)
    def _():
        x = x_ref[...]
        ms = jnp.mean(x * x, axis=-1, keepdims=True)
        xn_ref[...] = (x * lax.rsqrt(ms + EPS) * g_ref[...]).astype(BF16)

    def project():
        return lax.dot_general(xn_ref[...], w_ref[...], (((1,), (1,)), ((), ())),
                               preferred_element_type=F32)

    @pl.when(j < KV_TILE)
    def _():
        ob_ref[...] = project().astype(BF16)

    @pl.when(j == KV_TILE)
    def _():
        of_ref[...] = project()[:, :D_HF]
        for g in range(N_KV_HEADS):
            rows = pl.ds(g, tm, stride=N_KV_HEADS)
            k_ref[rows, :] = of_ref[:, F_K + g * HEAD_DIM:F_K + (g + 1) * HEAD_DIM]
            v_ref[rows, :] = of_ref[:, F_V + g * HEAD_DIM:F_V + (g + 1) * HEAD_DIM]


def _proj(x2d, g, w):
    n = x2d.shape[0]
    tm, tn = PROJ_TM, PROJ_TN
    kv_spec = pl.BlockSpec((N_KV_HEADS * tm, HEAD_DIM), lambda i, j: (i, 0))
    kv_shape = jax.ShapeDtypeStruct((N_KV_HEADS * n, HEAD_DIM), F32)
    return pl.pallas_call(
        _proj_kernel,
        grid=(n // tm, D_H // tn),
        in_specs=[pl.BlockSpec((tm, D_MODEL), lambda i, j: (i, 0)),
                  pl.BlockSpec((1, D_MODEL), lambda i, j: (0, 0)),
                  pl.BlockSpec((tn, D_MODEL), lambda i, j: (j, 0))],
        out_specs=[pl.BlockSpec((tm, tn), lambda i, j: (i, jnp.minimum(j, KV_TILE - 1))),
                   pl.BlockSpec((tm, D_HF), lambda i, j: (i, 0)), kv_spec, kv_spec],
        out_shape=[jax.ShapeDtypeStruct((n, D_HB), BF16),
                   jax.ShapeDtypeStruct((n, D_HF), F32), kv_shape, kv_shape],
        scratch_shapes=[pltpu.VMEM((tm, D_MODEL), BF16)],
        compiler_params=_params(("parallel", "arbitrary")),
        name="proj",
    )(x2d, g, w)


def _bias_kernel(rb_ref, bp_ref, bs_ref, op_ref, os_ref):
    bp = bp_ref[...]
    bs = bs_ref[...]
    for h in range(N_HEADS):
        far = rb_ref[FAR_BUCKET, h]
        tp = jnp.zeros(bp.shape, F32)
        ts = jnp.zeros(bs.shape, F32)
        for b in range(N_BUCKETS):
            val = rb_ref[b, h] - far
            tp = jnp.where(bp == b, val * LOG2E, tp)
            ts = jnp.where(bs == b, val, ts)
        op_ref[h] = tp
        os_ref[h] = ts


def _bias_tables(rel_bias):
    key = np.arange(CHUNK)[:, None]
    qry = np.arange(Q_BLOCK)[None, :]
    bucket_p = _t5_bucket_static(CHUNK - Q_BLOCK + qry - key)
    qi = np.arange(DEC_SEQ)[:, None]
    col = np.arange(2 * LANE)[None, :]
    dist_s = np.where(col < LANE, LANE + qi - col, qi - (col - LANE))
    bucket_s = _t5_bucket_static(dist_s)
    return pl.pallas_call(
        _bias_kernel,
        in_specs=[pl.BlockSpec(memory_space=pltpu.SMEM),
                  pl.BlockSpec(memory_space=pltpu.VMEM),
                  pl.BlockSpec(memory_space=pltpu.VMEM)],
        out_specs=[pl.BlockSpec(memory_space=pltpu.VMEM),
                   pl.BlockSpec(memory_space=pltpu.VMEM)],
        out_shape=[jax.ShapeDtypeStruct((N_HEADS, CHUNK, Q_BLOCK), F32),
                   jax.ShapeDtypeStruct((N_HEADS, DEC_SEQ, 2 * LANE), F32)],
        name="bias_tables",
    )(rel_bias, jnp.asarray(bucket_p), jnp.asarray(bucket_s))


def _any(x):
    return jnp.max(jnp.where(x, 1.0, 0.0)) > 0.5


def _rep(x):
    return jnp.broadcast_to(x, (SUBLANE, LANE))


def _fold_rows(x, comb):
    parts = [x[k:k + SUBLANE] for k in range(0, x.shape[0], SUBLANE)]
    while len(parts) > 1:
        parts = [comb(parts[k], parts[k + 1]) for k in range(0, len(parts), 2)]
    return parts[0]


def _select_threshold(tile_fn, ntiles, static):
    def reduce_tiles(fn, init, comb):
        def step(j, acc):
            x = fn(tile_fn(j), j)
            parts = [x[k] for k in range(x.shape[0])]
            while len(parts) > 1:
                parts = [comb(parts[k], parts[k + 1]) for k in range(0, len(parts), 2)]
            return comb(acc, parts[0])
        if static:
            acc = init
            for j in range(ntiles):
                acc = step(j, acc)
            return acc
        return lax.fori_loop(0, ntiles, step, init)

    zeros = jnp.zeros((SUBLANE, LANE), F32)

    def count(pred_fn):
        acc = reduce_tiles(lambda s, j: jnp.where(pred_fn(s, j), 1.0, 0.0), zeros,
                           lambda a, b: a + b)
        return _rep(jnp.sum(acc, axis=0, keepdims=True))

    def masked_max(pred_fn):
        acc = reduce_tiles(lambda s, j: jnp.where(pred_fn(s, j), s, NEG_INF),
                           jnp.full((SUBLANE, LANE), NEG_INF, F32), jnp.maximum)
        return _rep(jnp.max(acc, axis=0, keepdims=True))

    bound = reduce_tiles(lambda s, j: jnp.where(s > NEG_INF, jnp.abs(s), 0.0), zeros,
                         jnp.maximum)
    bound = _rep(jnp.max(bound, axis=0, keepdims=True))

    def bisect(_, carry):
        lo, hi = carry
        mid = 0.5 * lo + 0.5 * hi
        few = count(lambda s, j: s > mid[None]) < TOPK
        return jnp.where(few, lo, mid), jnp.where(few, mid, hi)

    _, hi = lax.fori_loop(0, N_BISECT, bisect, (-bound, bound))

    thr = masked_max(lambda s, j: s <= hi[None])
    n_ge = count(lambda s, j: s >= thr[None])

    def fix_body(carry):
        thr, n_ge, _ = carry
        lower = masked_max(lambda s, j: s < thr[None])
        thr = jnp.where(n_ge < TOPK, lower, thr)
        n_ge = count(lambda s, j: s >= thr[None])
        return thr, n_ge, _any(n_ge < TOPK)

    thr, n_ge, _ = lax.while_loop(lambda c: c[2], fix_body, (thr, n_ge, _any(n_ge < TOPK)))
    n_gt = count(lambda s, j: s > thr[None])
    need = TOPK - n_gt
    return thr[0:1], need[0:1]


def _keep(pred):
    return jnp.where(pred, 0.0, NEG)


def _selection_masks(tiles, thr, need, seen, tri):
    ties = [s == thr for s in tiles]
    ranks = []
    for k in range(0, len(tiles), 2):
        pair = jnp.concatenate([jnp.where(t, 1.0, 0.0).astype(BF16) for t in ties[k:k + 2]], axis=1)
        rank = jnp.dot(tri, pair, preferred_element_type=F32)
        ranks += [rank[:, p * LANE:(p + 1) * LANE] for p in range(len(ties[k:k + 2]))]
    masks = []
    for s, tie, rank in zip(tiles, ties, ranks):
        masks.append(jnp.where(tie, _keep(rank + seen <= need), _keep(s > thr)))
        seen = seen + rank[s.shape[0] - 1:]
    return masks, seen


def _attn_p_kernel(q_ref, qi_ref, wi_ref, k_ref, v_ref, ki_ref, bias_ref, tri_ref, o_ref,
                   kb_ref, vt_ref, kib_ref, qh_ref, qih_ref,
                   score_ref, mask_ref, s_ref):
    i = pl.program_id(1)
    T = Q_BLOCK
    W = GROUP * T
    nch = i // CHUNK_BLOCKS + 1

    @pl.when(i == 0)
    def _():
        kb_ref[0:PAD_KEYS] = jnp.zeros((PAD_KEYS, D_KV), BF16)
        kb_ref[PAD_KEYS:] = k_ref[0].astype(BF16)
        kib_ref[0:PAD_KEYS] = jnp.zeros((PAD_KEYS, IDX_DIM), BF16)
        kib_ref[PAD_KEYS:] = ki_ref[0][:, :IDX_DIM].astype(BF16)
        ones_row = lax.broadcasted_iota(jnp.int32, (VT_ROWS - HEAD_DIM, PAD_KEYS + SEQ), 0) == 0
        for g in range(N_KV_HEADS):
            vt_ref[g, 0:HEAD_DIM, 0:PAD_KEYS] = jnp.zeros((HEAD_DIM, PAD_KEYS), BF16)
            vt_ref[g, HEAD_DIM:VT_ROWS, :] = jnp.where(ones_row, 1.0, 0.0).astype(BF16)
            for c in range(SEQ // LANE):
                blk = v_ref[0, c * LANE:(c + 1) * LANE, g * HEAD_DIM:(g + 1) * HEAD_DIM]
                vt_ref[g, 0:HEAD_DIM,
                       PAD_KEYS + c * LANE:PAD_KEYS + (c + 1) * LANE] = blk.T.astype(BF16)
        score_ref[0:PAD_KEYS] = jnp.full((PAD_KEYS, T), NEG_INF, F32)
        mask_ref[0:PAD_KEYS] = jnp.full((PAD_KEYS, T), NEG, F32)

    q = q_ref[0].astype(F32) * (QK_SCALE * LOG2E)
    for h in range(N_HEADS):
        qh_ref[h] = q[:, h * HEAD_DIM:(h + 1) * HEAD_DIM].astype(BF16)
    qi = qi_ref[0]
    for h in range(IDX_HEADS):
        qih_ref[h] = qi[:, h * IDX_DIM:(h + 1) * IDX_DIM].astype(BF16)
    w_rows = wi_ref[0].T[W_OFF:W_OFF + IDX_HEADS] * SCORE_SCALE

    def span(c):
        return pl.ds(pl.multiple_of((i - CHUNK_BLOCKS * c) * LANE, LANE), CHUNK)

    def first_key(c):
        return (i - CHUNK_BLOCKS * c) * LANE - PAD_KEYS

    key_l = lax.broadcasted_iota(jnp.int32, (CHUNK, T), 0)
    qry = i * T + lax.broadcasted_iota(jnp.int32, (CHUNK, T), 1)

    def score_chunk(c):
        kc = kib_ref[span(c), :]
        d = lax.dot_general(kc, qih_ref[...].reshape(IDX_HEADS * T, IDX_DIM),
                            (((1,), (1,)), ((), ())), preferred_element_type=F32)
        acc = jnp.zeros((CHUNK, T), F32)
        for h in range(IDX_HEADS):
            acc = acc + jnp.maximum(d[:, h * T:(h + 1) * T], 0.0) * w_rows[h:h + 1]
        key = first_key(c) + key_l
        acc = jnp.where(key <= qry, jnp.where(key >= 0, acc, NEG_INF), NEG_INF)
        score_ref[span(c), :] = acc

    nsub = CHUNK // SUBLANE

    def tile_fn(c):
        return score_ref[span(c), :].reshape(nsub, SUBLANE, T)

    def select_all():
        mask_ref[span(0), :] = _keep(score_ref[span(0), :] > NEG_INF)

    def select(n):
        thr, need = _select_threshold(tile_fn, n, static=True)
        seen = jnp.zeros((1, T), F32)
        for c in reversed(range(n)):
            start = pl.multiple_of((i - CHUNK_BLOCKS * c) * LANE, LANE)
            parts = [pl.ds(start + p * MXU_WIDTH, MXU_WIDTH) for p in range(CHUNK // MXU_WIDTH)]
            masks, seen = _selection_masks([score_ref[rows, :] for rows in parts],
                                           thr, need, seen, tri_ref[...])
            for rows, mask in zip(parts, masks):
                mask_ref[rows, :] = mask

    def logits(c, g):
        kc = kb_ref[span(c), g * HEAD_DIM:(g + 1) * HEAD_DIM]
        qg = qh_ref[g * GROUP:(g + 1) * GROUP].reshape(W, HEAD_DIM)
        s = lax.dot_general(kc, qg, (((1,), (1,)), ((), ())), preferred_element_type=F32)
        mb = mask_ref[span(c), :]
        if c == 0:
            add = jnp.concatenate([mb + bias_ref[g * GROUP + hq] for hq in range(GROUP)], axis=1)
        else:
            add = jnp.concatenate([mb] * GROUP, axis=1)
        s = s + add
        s_ref[span(c), g * W:(g + 1) * W] = s
        return jnp.max(_fold_rows(s, jnp.maximum), axis=0, keepdims=True)

    def run(n):
        for c in range(n):
            score_chunk(c)
        if n == 1:
            pl.when(i * T + T <= TOPK)(select_all)
            pl.when(i * T + T > TOPK)(functools.partial(select, n))
        else:
            select(n)
        for g in range(N_KV_HEADS):
            m = functools.reduce(jnp.maximum, [logits(c, g) for c in range(n)])
            acc = None
            for c in range(n):
                p = jnp.exp2((s_ref[span(c), g * W:(g + 1) * W] - m).astype(BF16))
                pv = jnp.dot(vt_ref[g, :, span(c)], p, preferred_element_type=F32)
                acc = pv if acc is None else acc + pv
            o = acc[0:HEAD_DIM] / acc[HEAD_DIM:HEAD_DIM + 1]
            for hq in range(GROUP):
                h = g * GROUP + hq
                o_ref[0, :, h * HEAD_DIM:(h + 1) * HEAD_DIM] = o[:, hq * T:(hq + 1) * T].T

    for n in range(1, SEQ // CHUNK + 1):
        pl.when(nch == n)(functools.partial(run, n))


def _attn_prompt(hb_p, hf_p, bias_p):
    hb3 = hb_p.reshape(BATCH, SEQ, D_HB)
    hf3 = hf_p.reshape(BATCH, SEQ, D_HF)
    nqb = SEQ // Q_BLOCK
    T = Q_BLOCK
    qi_w = IDX_HEADS * IDX_DIM
    return pl.pallas_call(
        _attn_p_kernel,
        grid=(BATCH, nqb),
        in_specs=[
            pl.BlockSpec((1, T, D_ATTN), lambda b, i: (b, i, C_Q // D_ATTN)),
            pl.BlockSpec((1, T, qi_w), lambda b, i: (b, i, F_QI // qi_w)),
            pl.BlockSpec((1, T, LANE), lambda b, i: (b, i, F_KI // LANE)),
            pl.BlockSpec((1, SEQ, D_KV), lambda b, i: (b, 0, F_K // D_KV)),
            pl.BlockSpec((1, SEQ, D_KV), lambda b, i: (b, 0, F_V // D_KV)),
            pl.BlockSpec((1, SEQ, LANE), lambda b, i: (b, 0, F_KI // LANE)),
            pl.BlockSpec((N_HEADS, CHUNK, T), lambda b, i: (0, 0, 0)),
            pl.BlockSpec((MXU_WIDTH, MXU_WIDTH), lambda b, i: (0, 0)),
        ],
        out_specs=pl.BlockSpec((1, T, D_ATTN), lambda b, i: (b, i, 0)),
        out_shape=jax.ShapeDtypeStruct((BATCH, SEQ, D_ATTN), F32),
        scratch_shapes=[
            pltpu.VMEM((PAD_KEYS + SEQ, D_KV), BF16),
            pltpu.VMEM((N_KV_HEADS, VT_ROWS, PAD_KEYS + SEQ), BF16),
            pltpu.VMEM((PAD_KEYS + SEQ, IDX_DIM), BF16),
            pltpu.VMEM((N_HEADS, T, HEAD_DIM), BF16),
            pltpu.VMEM((IDX_HEADS, T, IDX_DIM), BF16),
            pltpu.VMEM((PAD_KEYS + SEQ, T), F32),
            pltpu.VMEM((PAD_KEYS + SEQ, T), F32),
            pltpu.VMEM((PAD_KEYS + SEQ, N_HEADS * T), F32),
        ],
        compiler_params=_params(("parallel", "arbitrary")),
        name="attn_prompt",
    )(hb3, hf3, hf3, hf3, hf3, hf3, bias_p, jnp.tri(MXU_WIDTH, dtype=BF16))


def _score_s_kernel(pt_ref, qi_ref, w_ref, kin_ref, *rest):
    npg = SCORE_ROWS * N_PAGES
    kip = rest[0:npg]
    o_ref = rest[npg]
    kinp_ref = rest[npg + 1]
    del pt_ref
    R = DEC_SEQ
    qrow = lax.broadcasted_iota(jnp.int32, (R, LANE), 0)
    lane = lax.broadcasted_iota(jnp.int32, (R, LANE), 1)
    kinp_ref[...] = jnp.zeros(kinp_ref.shape, BF16)
    for r in range(SCORE_ROWS):
        kinp_ref[r, 0:2 * R] = jnp.concatenate(
            [kin_ref[r][:, :IDX_DIM], jnp.zeros((R, IDX_DIM), F32)], 0).astype(BF16)
    for r in range(SCORE_ROWS):
        qi = qi_ref[r].astype(BF16)
        wb = jnp.broadcast_to(w_ref[r] * SCORE_SCALE, (IDX_HEADS * R, LANE))
        for t in range(N_PAGES + 1):
            if t < N_PAGES:
                d = jnp.dot(qi, kip[r * N_PAGES + t][0].astype(BF16), preferred_element_type=F32)
            else:
                d = lax.dot_general(qi, kinp_ref[r], (((1,), (1,)), ((), ())),
                                    preferred_element_type=F32)
            e = (jnp.maximum(d, 0.0) * wb).reshape(IDX_HEADS, R, LANE)
            s = e[0]
            for h in range(1, IDX_HEADS):
                s = s + e[h]
            if t == N_PAGES:
                s = jnp.where(lane <= qrow, s, NEG_INF)
            o_ref[r, :, t * LANE:(t + 1) * LANE] = s


def _score_sample(hf_s, page_table, cache_kidx_t):
    R = DEC_SEQ
    G = SCORE_ROWS
    h3 = hf_s.reshape(DEC_BATCH, R, D_HF)
    qi_hq = h3[:, :, F_QI:F_QI + IDX_HEADS * IDX_DIM].reshape(DEC_BATCH, R, IDX_HEADS, IDX_DIM)
    qi_hq = qi_hq.transpose(0, 2, 1, 3).reshape(DEC_BATCH, IDX_HEADS * R, IDX_DIM)
    w_hq = h3[:, :, F_KI + W_OFF:F_KI + W_OFF + IDX_HEADS].transpose(0, 2, 1)
    w_hq = w_hq.reshape(DEC_BATCH, IDX_HEADS * R, 1)
    in_specs = [
        pl.BlockSpec((G, IDX_HEADS * R, IDX_DIM), lambda b, pt: (b, 0, 0)),
        pl.BlockSpec((G, IDX_HEADS * R, 1), lambda b, pt: (b, 0, 0)),
        pl.BlockSpec((G, R, LANE), lambda b, pt: (b, 0, F_KI // LANE)),
    ]
    in_specs += [pl.BlockSpec((1, IDX_DIM, PAGE_SIZE),
                              lambda b, pt, r=r, p=p: (pt[b * G + r, p], 0, 0))
                 for r in range(G) for p in range(N_PAGES)]
    grid_spec = pltpu.PrefetchScalarGridSpec(
        num_scalar_prefetch=1,
        grid=(DEC_BATCH // G,),
        in_specs=in_specs,
        out_specs=pl.BlockSpec((G, R, L_SAMPLE), lambda b, pt: (b, 0, 0)),
        scratch_shapes=[pltpu.VMEM((G, PAGE_SIZE, IDX_DIM), BF16)],
    )
    return pl.pallas_call(
        _score_s_kernel,
        grid_spec=grid_spec,
        out_shape=jax.ShapeDtypeStruct((DEC_BATCH, R, L_SAMPLE), F32),
        compiler_params=_params(("arbitrary",)),
        name="score_sample",
    )(page_table, qi_hq, w_hq, h3, *([cache_kidx_t] * (G * N_PAGES)))


def _select_s_kernel(s_ref, tri_ref, o_ref):
    nt = N_PAGES + 1

    def tile_fn(j):
        return s_ref[j * LANE:(j + 1) * LANE, :].reshape(SUBTILES, SUBLANE, LANE)

    thr, need = _select_threshold(tile_fn, nt, static=True)
    parts = [slice(j * LANE, (j + 1) * LANE) for j in range(nt)]
    masks, _ = _selection_masks([s_ref[rows, :] for rows in parts], thr, need,
                                jnp.zeros((1, LANE), F32), tri_ref[...])
    for rows, mask in zip(parts, masks):
        o_ref[rows, :] = mask


def _select_sample(scores_t):
    n = scores_t.shape[1]
    return pl.pallas_call(
        _select_s_kernel,
        grid=(n // LANE,),
        in_specs=[pl.BlockSpec((L_SAMPLE, LANE), lambda c: (0, c)),
                  pl.BlockSpec((LANE, LANE), lambda c: (0, 0))],
        out_specs=pl.BlockSpec((L_SAMPLE, LANE), lambda c: (0, c)),
        out_shape=jax.ShapeDtypeStruct((L_SAMPLE, n), F32),
        compiler_params=_params(("parallel",)),
        name="select_sample",
    )(scores_t, jnp.tri(LANE, dtype=BF16))


def _attn_s_kernel(pt_ref, q_ref, kn_ref, vn_ref, mask_ref, bias_ref, *rest):
    npg = ATTN_ROWS * N_PAGES
    kp = rest[0:npg]
    vp = rest[npg:2 * npg]
    o_ref = rest[2 * npg]
    knp_ref, vnp_ref, logit_ref = rest[2 * npg + 1:]
    del pt_ref
    R = DEC_SEQ
    NT = N_PAGES + 1
    GR = GROUP * R

    knp_ref[...] = jnp.zeros(knp_ref.shape, BF16)
    vnp_ref[...] = jnp.zeros(vnp_ref.shape, BF16)
    for r in range(ATTN_ROWS):
        knp_ref[r, 0:2 * R] = jnp.concatenate([kn_ref[r], jnp.zeros((R, D_KV), F32)], 0).astype(BF16)
        vnp_ref[r, 0:2 * R] = jnp.concatenate([vn_ref[r], jnp.zeros((R, D_KV), F32)], 0).astype(BF16)

    def page_head(refs, pad_ref, r, t, g):
        if t < N_PAGES:
            return refs[r * N_PAGES + t][pl.ds(g, PAGE_SIZE, stride=N_KV_HEADS), :].astype(BF16)
        return pad_ref[r, :, g * HEAD_DIM:(g + 1) * HEAD_DIM]

    for r in range(ATTN_ROWS):
        q = (q_ref[r].astype(F32) * QK_SCALE).astype(BF16)
        for t in range(NT):
            mb = mask_ref[r, :, t * LANE:(t + 1) * LANE]
            for g in range(N_KV_HEADS):
                lg = lax.dot_general(q[g * GR:(g + 1) * GR], page_head(kp, knp_ref, r, t, g),
                                     (((1,), (1,)), ((), ())), preferred_element_type=F32)
                lg = lg.reshape(GROUP, R, LANE) + mb[None]
                if t >= N_PAGES - 1:
                    off = (t - (N_PAGES - 1)) * LANE
                    lg = lg + bias_ref[g * GROUP:(g + 1) * GROUP, :, off:off + LANE]
                logit_ref[r, g * GR:(g + 1) * GR, t * LANE:(t + 1) * LANE] = lg.reshape(GR, LANE)

    for r in range(ATTN_ROWS):
        logits = logit_ref[r]
        m = jnp.max(logits, axis=1, keepdims=True)
        p = jnp.exp(logits - m)
        inv = 1.0 / jnp.sum(p, axis=1, keepdims=True)
        pb = p.astype(BF16)
        outs = [jnp.zeros((GR, HEAD_DIM), F32) for _ in range(N_KV_HEADS)]
        for t in range(NT):
            for g in range(N_KV_HEADS):
                outs[g] = outs[g] + jnp.dot(pb[g * GR:(g + 1) * GR, t * LANE:(t + 1) * LANE],
                                            page_head(vp, vnp_ref, r, t, g),
                                            preferred_element_type=F32)
        for g in range(N_KV_HEADS):
            o = outs[g] * inv[g * GR:(g + 1) * GR]
            for hq in range(GROUP):
                h = g * GROUP + hq
                o_ref[r, :, h * HEAD_DIM:(h + 1) * HEAD_DIM] = o[hq * R:(hq + 1) * R]


def _attn_sample(hb_s, hf_s, page_table, cache_k, cache_v, mask, bias_s):
    R = DEC_SEQ
    G = ATTN_ROWS
    h3 = hf_s.reshape(DEC_BATCH, R, D_HF)
    q_hq = hb_s[:, C_Q:C_Q + D_ATTN].reshape(DEC_BATCH, R, N_HEADS, HEAD_DIM)
    q_hq = q_hq.transpose(0, 2, 1, 3).reshape(DEC_BATCH, N_HEADS * R, HEAD_DIM)
    rows_per_page = PAGE_SIZE * N_KV_HEADS
    ck = cache_k.reshape(-1, HEAD_DIM)
    cv = cache_v.reshape(-1, HEAD_DIM)

    in_specs = [
        pl.BlockSpec((G, N_HEADS * R, HEAD_DIM), lambda b, pt: (b, 0, 0)),
        pl.BlockSpec((G, R, D_KV), lambda b, pt: (b, 0, F_K // D_KV)),
        pl.BlockSpec((G, R, D_KV), lambda b, pt: (b, 0, F_V // D_KV)),
        pl.BlockSpec((G, R, L_SAMPLE), lambda b, pt: (b, 0, 0)),
        pl.BlockSpec((N_HEADS, R, 2 * LANE), lambda b, pt: (0, 0, 0)),
    ]
    pages = [pl.BlockSpec((rows_per_page, HEAD_DIM), lambda b, pt, r=r, p=p: (pt[b * G + r, p], 0))
             for r in range(G) for p in range(N_PAGES)]
    in_specs += pages + pages
    grid_spec = pltpu.PrefetchScalarGridSpec(
        num_scalar_prefetch=1,
        grid=(DEC_BATCH // G,),
        in_specs=in_specs,
        out_specs=pl.BlockSpec((G, R, D_ATTN), lambda b, pt: (b, 0, 0)),
        scratch_shapes=[
            pltpu.VMEM((G, PAGE_SIZE, D_KV), BF16),
            pltpu.VMEM((G, PAGE_SIZE, D_KV), BF16),
            pltpu.VMEM((G, N_HEADS * R, L_SAMPLE), F32),
        ],
    )
    return pl.pallas_call(
        _attn_s_kernel,
        grid_spec=grid_spec,
        out_shape=jax.ShapeDtypeStruct((DEC_BATCH, R, D_ATTN), F32),
        compiler_params=_params(("arbitrary",)),
        name="attn_sample",
    )(page_table, q_hq, h3, h3, mask, bias_s, *([ck] * (G * N_PAGES)), *([cv] * (G * N_PAGES)))


def _conv_p_kernel(val_ref, gate_ref, cw_ref, cb_ref, dw_ref, ut_ref, pad_ref):
    pad_ref[0:PAD_ROWS] = jnp.zeros((PAD_ROWS, LANE), F32)
    pad_ref[PAD_ROWS:] = val_ref[0].astype(F32) * jax.nn.sigmoid(gate_ref[0].astype(F32))
    ut_ref[0] = pad_ref[SEQ:SEQ + PAD_ROWS]
    cw = cw_ref[...]
    cb = cb_ref[...]
    first = PAD_ROWS - (CONV_WIDTH - 1)
    for c in range(SEQ // CONV_CHUNK):
        base = c * CONV_CHUNK
        acc = jnp.broadcast_to(cb, (CONV_CHUNK, LANE))
        for r in range(SUBLANE):
            taps = [w for w in range(CONV_WIDTH) if (first + w) % SUBLANE == r]
            span = max(first + w - r for w in taps) + CONV_CHUNK
            win = pad_ref[base + r:base + r + span]
            for w in taps:
                a = first + w - r
                acc = acc + win[a:a + CONV_CHUNK] * cw[w:w + 1]
        dw_ref[0, base:base + CONV_CHUNK] = acc


def _conv_prompt(hb_p, cw_pad, cb):
    h3 = hb_p.reshape(BATCH, SEQ, D_HB)
    nc = D_CONV // LANE
    return pl.pallas_call(
        _conv_p_kernel,
        grid=(BATCH, nc),
        in_specs=[pl.BlockSpec((1, SEQ, LANE), lambda b, c: (b, 0, C_GV // LANE + c)),
                  pl.BlockSpec((1, SEQ, LANE), lambda b, c: (b, 0, C_GG // LANE + c)),
                  pl.BlockSpec((PAD_ROWS, LANE), lambda b, c: (0, c)),
                  pl.BlockSpec((1, LANE), lambda b, c: (0, c))],
        out_specs=[pl.BlockSpec((1, SEQ, LANE), lambda b, c: (b, 0, c)),
                   pl.BlockSpec((1, PAD_ROWS, LANE), lambda b, c: (b, 0, c))],
        out_shape=[jax.ShapeDtypeStruct((BATCH, SEQ, D_CONV), F32),
                   jax.ShapeDtypeStruct((BATCH, PAD_ROWS, D_CONV), F32)],
        scratch_shapes=[pltpu.VMEM((PAD_ROWS + SEQ, LANE), F32)],
        compiler_params=_params(("parallel", "parallel")),
        name="conv_prompt",
    )(h3, h3, cw_pad, cb)


def _conv_s_kernel(val_ref, gate_ref, st_ref, cw_ref, cb_ref, dw_ref, ns_ref):
    R = DEC_SEQ
    H = CONV_WIDTH - 1
    cw = cw_ref[...]
    cb = jnp.broadcast_to(cb_ref[...], (DEC_BATCH, LANE))
    u = []
    for q in range(R):
        rows = pl.ds(q, DEC_BATCH, stride=R)
        u.append(val_ref[rows, :] * jax.nn.sigmoid(gate_ref[rows, :]))

    def row(r):
        return st_ref[r] if r < H else u[r - H]

    for q in range(R):
        acc = cb
        for w in range(CONV_WIDTH):
            acc = acc + row(q + w) * cw[w:w + 1]
        dw_ref[pl.ds(q, DEC_BATCH, stride=R), :] = acc
    for r in range(H):
        ns_ref[r] = row(r + R)


def _conv_sample(val, gate, state_t, cw_pad, cb):
    n_s = DEC_BATCH * DEC_SEQ
    H = CONV_WIDTH - 1
    nc = D_CONV // LANE
    return pl.pallas_call(
        _conv_s_kernel,
        grid=(nc,),
        in_specs=[pl.BlockSpec((n_s, LANE), lambda c: (0, c)),
                  pl.BlockSpec((n_s, LANE), lambda c: (0, c)),
                  pl.BlockSpec((H, DEC_BATCH, LANE), lambda c: (0, 0, c)),
                  pl.BlockSpec((PAD_ROWS, LANE), lambda c: (0, c)),
                  pl.BlockSpec((1, LANE), lambda c: (0, c))],
        out_specs=[pl.BlockSpec((n_s, LANE), lambda c: (0, c)),
                   pl.BlockSpec((H, DEC_BATCH, LANE), lambda c: (0, 0, c))],
        out_shape=[jax.ShapeDtypeStruct((n_s, D_CONV), F32),
                   jax.ShapeDtypeStruct((H, DEC_BATCH, D_CONV), F32)],
        compiler_params=_params(("parallel",)),
        name="conv_sample",
    )(val, gate, state_t, cw_pad, cb)


def _tail_kernel(attn_ref, za_ref, dw_ref, zc_ref, ga_ref, gc_ref, x_ref,
                 wua_ref, wpw_ref, wuc_ref, wo_ref, ng_ref, nb_ref, bpw_ref, fg_ref, y_ref):
    a = attn_ref[...] * jax.nn.silu(za_ref[...].astype(F32))
    branch_attn = jnp.dot(a.astype(BF16), wua_ref[...], preferred_element_type=F32)

    dw = dw_ref[...]
    mu = jnp.mean(dw, axis=-1, keepdims=True)
    var = jnp.mean(jnp.square(dw - mu), axis=-1, keepdims=True)
    ln = (dw - mu) * lax.rsqrt(var + EPS) * ng_ref[...] + nb_ref[...]
    conv_out = jnp.dot(jax.nn.silu(ln).astype(BF16), wpw_ref[...],
                       preferred_element_type=F32) + bpw_ref[...]
    c = conv_out * jax.nn.silu(zc_ref[...].astype(F32))
    branch_conv = jnp.dot(c.astype(BF16), wuc_ref[...], preferred_element_type=F32)

    merged = (jax.nn.sigmoid(ga_ref[...].astype(F32)) * branch_attn
              + jax.nn.sigmoid(gc_ref[...].astype(F32)) * branch_conv)
    y = x_ref[...] + jnp.dot(merged.astype(BF16), wo_ref[...], preferred_element_type=F32)
    ms = jnp.mean(y * y, axis=-1, keepdims=True)
    y_ref[...] = y * lax.rsqrt(ms + EPS) * fg_ref[...]


def _tail(h, attn, dw, x2d, weights, tm):
    n = x2d.shape[0]

    def const(shape):
        return pl.BlockSpec(shape, lambda i: (0, 0), pipeline_mode=pl.Buffered(1))

    def cols(width, offset):
        return pl.BlockSpec((tm, width), lambda i: (i, offset // width))

    return pl.pallas_call(
        _tail_kernel,
        grid=(n // tm,),
        in_specs=[cols(D_ATTN, 0), cols(D_ATTN, C_ZA), cols(D_CONV, 0), cols(D_CONV, C_ZC),
                  cols(D_MODEL, C_GA), cols(D_MODEL, C_GC), cols(D_MODEL, 0),
                  const((D_ATTN, D_MODEL)), const((D_CONV, D_CONV)),
                  const((D_CONV, D_MODEL)), const((D_MODEL, D_MODEL)),
                  const((1, D_CONV)), const((1, D_CONV)), const((1, D_CONV)),
                  const((1, D_MODEL))],
        out_specs=pl.BlockSpec((tm, D_MODEL), lambda i: (i, 0)),
        out_shape=jax.ShapeDtypeStruct((n, D_MODEL), F32),
        compiler_params=_params(("parallel",)),
        name="tail",
    )(attn, h, dw, h, h, h, x2d, *weights)


def _reorder_kernel(off_ref, w_ref, o_ref):
    del off_ref
    o_ref[...] = w_ref[...].astype(BF16)


def _reorder_w_in(w_in):
    w_t = w_in.T
    src = []
    for first, dest, width in (('gate_attn', C_GA, 2 * D_MODEL), ('q', C_Q, D_ATTN),
                               ('z_attn', C_ZA, D_ATTN), ('glu_val', C_GV, 3 * D_CONV),
                               ('k', C_K, 2 * D_KV), ('q_idx', C_QI, D_H - C_QI)):
        assert dest == len(src) * REORDER_ROWS
        for r in range(0, width, REORDER_ROWS):
            src.append(min(_SRC[first][0] + r, D_IN - REORDER_ROWS) // SUBLANE)
    grid_spec = pltpu.PrefetchScalarGridSpec(
        num_scalar_prefetch=1,
        grid=(len(src),),
        in_specs=[pl.BlockSpec((pl.Element(REORDER_ROWS), pl.Element(D_MODEL)),
                               lambda d, off: (off[d] * SUBLANE, 0))],
        out_specs=pl.BlockSpec((REORDER_ROWS, D_MODEL), lambda d, off: (d, 0)),
    )
    return pl.pallas_call(
        _reorder_kernel,
        grid_spec=grid_spec,
        out_shape=jax.ShapeDtypeStruct((D_H, D_MODEL), BF16),
        compiler_params=_params(("arbitrary",)),
        name="reorder_w_in",
    )(jnp.asarray(src, jnp.int32), w_t)


def kernel(x_prompt, x_sample, cache_k, cache_v, cache_kidx, state_conv, page_table,
           ln_g, w_in, conv_w, conv_b, conv_norm_g, conv_norm_b, w_pw, b_pw,
           w_up_attn, w_up_conv, w_out, rel_bias, final_g):
    w_all = _reorder_w_in(w_in[0])
    g_in = ln_g[0].reshape(1, D_MODEL)
    xp = x_prompt.reshape(BATCH * SEQ, D_MODEL)
    xs = x_sample.reshape(DEC_BATCH * DEC_SEQ, D_MODEL)
    hb_p, hf_p, k_p, v_p = _proj(xp, g_in, w_all)
    hb_s, hf_s, k_s, v_s = _proj(xs, g_in, w_all)

    bias_p, bias_s = _bias_tables(rel_bias)

    attn_p = _attn_prompt(hb_p, hf_p, bias_p).reshape(BATCH * SEQ, D_ATTN)

    n_s = DEC_BATCH * DEC_SEQ
    scores = _score_sample(hf_s, page_table, cache_kidx[0].transpose(0, 2, 1))
    mask_t = _select_sample(scores.reshape(n_s, L_SAMPLE).T)
    mask = mask_t.T.reshape(DEC_BATCH, DEC_SEQ, L_SAMPLE)
    attn_s = _attn_sample(hb_s, hf_s, page_table, cache_k[0], cache_v[0], mask, bias_s)
    attn_s = attn_s.reshape(n_s, D_ATTN)

    cw_pad = jnp.concatenate([conv_w[0], jnp.zeros((PAD_ROWS - CONV_WIDTH, D_CONV), F32)], 0)
    cb = conv_b[0].reshape(1, D_CONV)
    dw_p, u_tail = _conv_prompt(hb_p, cw_pad, cb)
    dw_s, state_new = _conv_sample(hb_s[:, C_GV:C_GV + D_CONV].astype(F32),
                                   hb_s[:, C_GG:C_GG + D_CONV].astype(F32),
                                   state_conv[0].transpose(1, 0, 2), cw_pad, cb)

    wua = w_up_attn[0].astype(BF16)
    wpw = w_pw[0].astype(BF16)
    wuc = w_up_conv[0].astype(BF16)
    wo = w_out[0].astype(BF16)
    ng = conv_norm_g[0].reshape(1, D_CONV)
    nb = conv_norm_b[0].reshape(1, D_CONV)
    bpw = b_pw[0].reshape(1, D_CONV)
    fg = final_g.reshape(1, D_MODEL)
    weights = (wua, wpw, wuc, wo, ng, nb, bpw, fg)
    y_p = _tail(hb_p, attn_p, dw_p.reshape(BATCH * SEQ, D_CONV), xp, weights, TAIL_TM)
    y_s = _tail(hb_s, attn_s, dw_s, xs, weights, TAIL_TM)

    tail_rows = CONV_WIDTH - 1
    return (
        y_p.reshape(BATCH, SEQ, D_MODEL),
        y_s.reshape(DEC_BATCH, DEC_SEQ, D_MODEL),
        k_p.reshape(1, BATCH, SEQ, N_KV_HEADS, HEAD_DIM),
        v_p.reshape(1, BATCH, SEQ, N_KV_HEADS, HEAD_DIM),
        hf_p[:, F_KI:F_KI + IDX_DIM].reshape(1, BATCH, SEQ, IDX_DIM),
        u_tail[:, PAD_ROWS - tail_rows:].reshape(1, BATCH, tail_rows, D_CONV),
        k_s.reshape(1, DEC_BATCH, DEC_SEQ, N_KV_HEADS, HEAD_DIM),
        v_s.reshape(1, DEC_BATCH, DEC_SEQ, N_KV_HEADS, HEAD_DIM),
        hf_s[:, F_KI:F_KI + IDX_DIM].reshape(1, DEC_BATCH, DEC_SEQ, IDX_DIM),
        state_new.transpose(1, 0, 2).reshape(1, DEC_BATCH, tail_rows, D_CONV),
    )
```

```python
import functools
import math

import numpy as np
import jax
import jax.numpy as jnp
from jax import lax
from jax.experimental import pallas as pl
from jax.experimental.pallas import tpu as pltpu

F32 = jnp.float32
BF16 = jnp.bfloat16

D_MODEL = 2048
BATCH = 8
SEQ = 2048
DEC_BATCH = 128
DEC_SEQ = 8
PAST_LEN = 2048
PAGE_SIZE = 128
N_PAGES = PAST_LEN // PAGE_SIZE
N_HEADS = 8
N_KV_HEADS = 2
HEAD_DIM = 128
GROUP = N_HEADS // N_KV_HEADS
D_ATTN = N_HEADS * HEAD_DIM
D_KV = N_KV_HEADS * HEAD_DIM
IDX_HEADS = 8
IDX_DIM = 64
TOPK = 256
Q_BLOCK = 128
N_BUCKETS = 32
MAX_DISTANCE = 128
D_CONV = D_MODEL // 2
CONV_WIDTH = 31
EPS = 1e-6
NEG = -1e30
NEG_INF = float("-inf")

LANE = 128
SUBLANE = 8
VMEM_LIMIT = 56 * 1024 * 1024

C_GA = 0
C_GC = C_GA + D_MODEL
C_Q = C_GC + D_MODEL
C_ZA = C_Q + D_ATTN
C_GV = C_ZA + D_ATTN
C_GG = C_GV + D_CONV
C_ZC = C_GG + D_CONV
C_K = C_ZC + D_CONV
C_V = C_K + D_KV
C_QI = C_V + D_KV
C_KI = C_QI + IDX_HEADS * IDX_DIM
W_OFF = IDX_DIM
MXU_WIDTH = 256
PROJ_TM = 1024
PROJ_TN = 6 * MXU_WIDTH
D_H = -(-(C_KI + LANE) // PROJ_TN) * PROJ_TN
KV_TILE = C_K // PROJ_TN
assert C_K % PROJ_TN == 0 and D_H == C_K + PROJ_TN
D_HB = C_K
D_HF = C_KI + LANE - C_K
F_K = C_K - D_HB
F_V = C_V - D_HB
F_QI = C_QI - D_HB
F_KI = C_KI - D_HB

_SRC = {}
_off = 0
for _name, _w in (('q', D_ATTN), ('k', D_KV), ('v', D_KV), ('z_attn', D_ATTN),
                  ('q_idx', IDX_HEADS * IDX_DIM), ('k_idx', IDX_DIM), ('w_idx', IDX_HEADS),
                  ('glu_val', D_CONV), ('glu_gate', D_CONV), ('z_conv', D_CONV),
                  ('gate_attn', D_MODEL), ('gate_conv', D_MODEL)):
    _SRC[_name] = (_off, _w)
    _off += _w
D_IN = _off
REORDER_ROWS = 512

N_BISECT = 20
PAD_ROWS = 32
CONV_CHUNK = 64
SCORE_SCALE = (IDX_DIM ** -0.5) * (IDX_HEADS ** -0.5)
QK_SCALE = HEAD_DIM ** -0.5
LOG2E = math.log2(math.e)
L_SAMPLE = (N_PAGES + 1) * LANE
SUBTILES = LANE // SUBLANE
CHUNK_BLOCKS = 2
CHUNK = CHUNK_BLOCKS * Q_BLOCK
PAD_KEYS = CHUNK - Q_BLOCK
VT_ROWS = HEAD_DIM + 2 * SUBLANE
SCORE_ROWS = 4
ATTN_ROWS = 4
TAIL_TM = 256


def _t5_bucket_static(dist):
    n = np.maximum(dist, 0)
    max_exact = N_BUCKETS // 2
    ratio = (np.log(np.maximum(n, 1).astype(np.float32) / np.float32(max_exact))
             / np.float32(math.log(MAX_DISTANCE / max_exact)))
    large = np.minimum(max_exact + (ratio * np.float32(N_BUCKETS - max_exact)).astype(np.int32),
                       N_BUCKETS - 1)
    return np.where(n < max_exact, n, large).astype(np.int32)


FAR_BUCKET = int(_t5_bucket_static(np.array([2 * MAX_DISTANCE]))[0])


def _params(sem):
    return pltpu.CompilerParams(dimension_semantics=sem, vmem_limit_bytes=VMEM_LIMIT)


def _proj_kernel(x_ref, g_ref, w_ref, ob_ref, of_ref, k_ref, v_ref, xn_ref):
    tm = x_ref.shape[0]
    j = pl.program_id(1)

    @pl.when(j == 0)
    def _():
        x = x_ref[...]
        ms = jnp.mean(x * x, axis=-1, keepdims=True)
        xn_ref[...] = (x * lax.rsqrt(ms + EPS) * g_ref[...]).astype(BF16)

    def project():
        return lax.dot_general(xn_ref[...], w_ref[...], (((1,), (1,)), ((), ())),
                               preferred_element_type=F32)

    @pl.when(j < KV_TILE)
    def _():
        ob_ref[...] = project().astype(BF16)

    @pl.when(j == KV_TILE)
    def _():
        of_ref[...] = project()[:, :D_HF]
        for g in range(N_KV_HEADS):
            rows = pl.ds(g, tm, stride=N_KV_HEADS)
            k_ref[rows, :] = of_ref[:, F_K + g * HEAD_DIM:F_K + (g + 1) * HEAD_DIM]
            v_ref[rows, :] = of_ref[:, F_V + g * HEAD_DIM:F_V + (g + 1) * HEAD_DIM]


def _proj(x2d, g, w):
    n = x2d.shape[0]
    tm, tn = PROJ_TM, PROJ_TN
    kv_spec = pl.BlockSpec((N_KV_HEADS * tm, HEAD_DIM), lambda i, j: (i, 0))
    kv_shape = jax.ShapeDtypeStruct((N_KV_HEADS * n, HEAD_DIM), F32)
    return pl.pallas_call(
        _proj_kernel,
        grid=(n // tm, D_H // tn),
        in_specs=[pl.BlockSpec((tm, D_MODEL), lambda i, j: (i, 0)),
                  pl.BlockSpec((1, D_MODEL), lambda i, j: (0, 0)),
                  pl.BlockSpec((tn, D_MODEL), lambda i, j: (j, 0))],
        out_specs=[pl.BlockSpec((tm, tn), lambda i, j: (i, jnp.minimum(j, KV_TILE - 1))),
                   pl.BlockSpec((tm, D_HF), lambda i, j: (i, 0)), kv_spec, kv_spec],
        out_shape=[jax.ShapeDtypeStruct((n, D_HB), BF16),
                   jax.ShapeDtypeStruct((n, D_HF), F32), kv_shape, kv_shape],
        scratch_shapes=[pltpu.VMEM((tm, D_MODEL), BF16)],
        compiler_params=_params(("parallel", "arbitrary")),
        name="proj",
    )(x2d, g, w)


def _bias_kernel(rb_ref, bp_ref, bs_ref, op_ref, os_ref):
    bp = bp_ref[...]
    bs = bs_ref[...]
    for h in range(N_HEADS):
        far = rb_ref[FAR_BUCKET, h]
        tp = jnp.zeros(bp.shape, F32)
        ts = jnp.zeros(bs.shape, F32)
        for b in range(N_BUCKETS):
            val = rb_ref[b, h] - far
            tp = jnp.where(bp == b, val * LOG2E, tp)
            ts = jnp.where(bs == b, val, ts)
        op_ref[h] = tp
        os_ref[h] = ts


def _bias_tables(rel_bias):
    key = np.arange(CHUNK)[:, None]
    qry = np.arange(Q_BLOCK)[None, :]
    bucket_p = _t5_bucket_static(CHUNK - Q_BLOCK + qry - key)
    qi = np.arange(DEC_SEQ)[:, None]
    col = np.arange(2 * LANE)[None, :]
    dist_s = np.where(col < LANE, LANE + qi - col, qi - (col - LANE))
    bucket_s = _t5_bucket_static(dist_s)
    return pl.pallas_call(
        _bias_kernel,
        in_specs=[pl.BlockSpec(memory_space=pltpu.SMEM),
                  pl.BlockSpec(memory_space=pltpu.VMEM),
                  pl.BlockSpec(memory_space=pltpu.VMEM)],
        out_specs=[pl.BlockSpec(memory_space=pltpu.VMEM),
                   pl.BlockSpec(memory_space=pltpu.VMEM)],
        out_shape=[jax.ShapeDtypeStruct((N_HEADS, CHUNK, Q_BLOCK), F32),
                   jax.ShapeDtypeStruct((N_HEADS, DEC_SEQ, 2 * LANE), F32)],
        name="bias_tables",
    )(rel_bias, jnp.asarray(bucket_p), jnp.asarray(bucket_s))


def _any(x):
    return jnp.max(jnp.where(x, 1.0, 0.0)) > 0.5


def _rep(x):
    return jnp.broadcast_to(x, (SUBLANE, LANE))


def _fold_rows(x, comb):
    parts = [x[k:k + SUBLANE] for k in range(0, x.shape[0], SUBLANE)]
    while len(parts) > 1:
        parts = [comb(parts[k], parts[k + 1]) for k in range(0, len(parts), 2)]
    return parts[0]


def _select_threshold(tile_fn, ntiles, static):
    def reduce_tiles(fn, init, comb):
        def step(j, acc):
            x = fn(tile_fn(j), j)
            parts = [x[k] for k in range(x.shape[0])]
            while len(parts) > 1:
                parts = [comb(parts[k], parts[k + 1]) for k in range(0, len(parts), 2)]
            return comb(acc, parts[0])
        if static:
            acc = init
            for j in range(ntiles):
                acc = step(j, acc)
            return acc
        return lax.fori_loop(0, ntiles, step, init)

    zeros = jnp.zeros((SUBLANE, LANE), F32)

    def count(pred_fn):
        acc = reduce_tiles(lambda s, j: jnp.where(pred_fn(s, j), 1.0, 0.0), zeros,
                           lambda a, b: a + b)
        return _rep(jnp.sum(acc, axis=0, keepdims=True))

    def masked_max(pred_fn):
        acc = reduce_tiles(lambda s, j: jnp.where(pred_fn(s, j), s, NEG_INF),
                           jnp.full((SUBLANE, LANE), NEG_INF, F32), jnp.maximum)
        return _rep(jnp.max(acc, axis=0, keepdims=True))

    bound = reduce_tiles(lambda s, j: jnp.where(s > NEG_INF, jnp.abs(s), 0.0), zeros,
                         jnp.maximum)
    bound = _rep(jnp.max(bound, axis=0, keepdims=True))

    def bisect(_, carry):
        lo, hi = carry
        mid = 0.5 * lo + 0.5 * hi
        few = count(lambda s, j: s > mid[None]) < TOPK
        return jnp.where(few, lo, mid), jnp.where(few, mid, hi)

    _, hi = lax.fori_loop(0, N_BISECT, bisect, (-bound, bound))

    thr = masked_max(lambda s, j: s <= hi[None])
    n_ge = count(lambda s, j: s >= thr[None])

    def fix_body(carry):
        thr, n_ge, _ = carry
        lower = masked_max(lambda s, j: s < thr[None])
        thr = jnp.where(n_ge < TOPK, lower, thr)
        n_ge = count(lambda s, j: s >= thr[None])
        return thr, n_ge, _any(n_ge < TOPK)

    thr, n_ge, _ = lax.while_loop(lambda c: c[2], fix_body, (thr, n_ge, _any(n_ge < TOPK)))
    n_gt = count(lambda s, j: s > thr[None])
    need = TOPK - n_gt
    return thr[0:1], need[0:1]


def _keep(pred):
    return jnp.where(pred, 0.0, NEG)


def _selection_masks(tiles, thr, need, seen, tri):
    ties = [s == thr for s in tiles]
    ranks = []
    for k in range(0, len(tiles), 2):
        pair = jnp.concatenate([jnp.where(t, 1.0, 0.0).astype(BF16) for t in ties[k:k + 2]], axis=1)
        rank = jnp.dot(tri, pair, preferred_element_type=F32)
        ranks += [rank[:, p * LANE:(p + 1) * LANE] for p in range(len(ties[k:k + 2]))]
    masks = []
    for s, tie, rank in zip(tiles, ties, ranks):
        masks.append(jnp.where(tie, _keep(rank + seen <= need), _keep(s > thr)))
        seen = seen + rank[s.shape[0] - 1:]
    return masks, seen


def _attn_p_kernel(q_ref, qi_ref, wi_ref, k_ref, v_ref, ki_ref, bias_ref, tri_ref, o_ref,
                   kb_ref, vt_ref, kib_ref, qh_ref, qih_ref,
                   score_ref, mask_ref, s_ref):
    i = pl.program_id(1)
    T = Q_BLOCK
    W = GROUP * T
    nch = i // CHUNK_BLOCKS + 1

    @pl.when(i == 0)
    def _():
        kb_ref[0:PAD_KEYS] = jnp.zeros((PAD_KEYS, D_KV), BF16)
        kb_ref[PAD_KEYS:] = k_ref[0].astype(BF16)
        kib_ref[0:PAD_KEYS] = jnp.zeros((PAD_KEYS, IDX_DIM), BF16)
        kib_ref[PAD_KEYS:] = ki_ref[0][:, :IDX_DIM].astype(BF16)
        ones_row = lax.broadcasted_iota(jnp.int32, (VT_ROWS - HEAD_DIM, PAD_KEYS + SEQ), 0) == 0
        for g in range(N_KV_HEADS):
            vt_ref[g, 0:HEAD_DIM, 0:PAD_KEYS] = jnp.zeros((HEAD_DIM, PAD_KEYS), BF16)
            vt_ref[g, HEAD_DIM:VT_ROWS, :] = jnp.where(ones_row, 1.0, 0.0).astype(BF16)
            for c in range(SEQ // LANE):
                blk = v_ref[0, c * LANE:(c + 1) * LANE, g * HEAD_DIM:(g + 1) * HEAD_DIM]
                vt_ref[g, 0:HEAD_DIM,
                       PAD_KEYS + c * LANE:PAD_KEYS + (c + 1) * LANE] = blk.T.astype(BF16)
        score_ref[0:PAD_KEYS] = jnp.full((PAD_KEYS, T), NEG_INF, F32)
        mask_ref[0:PAD_KEYS] = jnp.full((PAD_KEYS, T), NEG, F32)

    q = q_ref[0].astype(F32) * (QK_SCALE * LOG2E)
    for h in range(N_HEADS):
        qh_ref[h] = q[:, h * HEAD_DIM:(h + 1) * HEAD_DIM].astype(BF16)
    qi = qi_ref[0]
    for h in range(IDX_HEADS):
        qih_ref[h] = qi[:, h * IDX_DIM:(h + 1) * IDX_DIM].astype(BF16)
    w_rows = wi_ref[0].T[W_OFF:W_OFF + IDX_HEADS] * SCORE_SCALE

    def span(c):
        return pl.ds(pl.multiple_of((i - CHUNK_BLOCKS * c) * LANE, LANE), CHUNK)

    def first_key(c):
        return (i - CHUNK_BLOCKS * c) * LANE - PAD_KEYS

    key_l = lax.broadcasted_iota(jnp.int32, (CHUNK, T), 0)
    qry = i * T + lax.broadcasted_iota(jnp.int32, (CHUNK, T), 1)

    def score_chunk(c):
        kc = kib_ref[span(c), :]
        d = lax.dot_general(kc, qih_ref[...].reshape(IDX_HEADS * T, IDX_DIM),
                            (((1,), (1,)), ((), ())), preferred_element_type=F32)
        acc = jnp.zeros((CHUNK, T), F32)
        for h in range(IDX_HEADS):
            acc = acc + jnp.maximum(d[:, h * T:(h + 1) * T], 0.0) * w_rows[h:h + 1]
        key = first_key(c) + key_l
        acc = jnp.where(key <= qry, jnp.where(key >= 0, acc, NEG_INF), NEG_INF)
        score_ref[span(c), :] = acc

    nsub = CHUNK // SUBLANE

    def tile_fn(c):
        return score_ref[span(c), :].reshape(nsub, SUBLANE, T)

    def select_all():
        mask_ref[span(0), :] = _keep(score_ref[span(0), :] > NEG_INF)

    def select(n):
        thr, need = _select_threshold(tile_fn, n, static=True)
        seen = jnp.zeros((1, T), F32)
        for c in reversed(range(n)):
            start = pl.multiple_of((i - CHUNK_BLOCKS * c) * LANE, LANE)
            parts = [pl.ds(start + p * MXU_WIDTH, MXU_WIDTH) for p in range(CHUNK // MXU_WIDTH)]
            masks, seen = _selection_masks([score_ref[rows, :] for rows in parts],
                                           thr, need, seen, tri_ref[...])
            for rows, mask in zip(parts, masks):
                mask_ref[rows, :] = mask

    def logits(c, g):
        kc = kb_ref[span(c), g * HEAD_DIM:(g + 1) * HEAD_DIM]
        qg = qh_ref[g * GROUP:(g + 1) * GROUP].reshape(W, HEAD_DIM)
        s = lax.dot_general(kc, qg, (((1,), (1,)), ((), ())), preferred_element_type=F32)
        mb = mask_ref[span(c), :]
        if c == 0:
            add = jnp.concatenate([mb + bias_ref[g * GROUP + hq] for hq in range(GROUP)], axis=1)
        else:
            add = jnp.concatenate([mb] * GROUP, axis=1)
        s = s + add
        s_ref[span(c), g * W:(g + 1) * W] = s
        return jnp.max(_fold_rows(s, jnp.maximum), axis=0, keepdims=True)

    def run(n):
        for c in range(n):
            score_chunk(c)
        if n == 1:
            pl.when(i * T + T <= TOPK)(select_all)
            pl.when(i * T + T > TOPK)(functools.partial(select, n))
        else:
            select(n)
        for g in range(N_KV_HEADS):
            m = functools.reduce(jnp.maximum, [logits(c, g) for c in range(n)])
            acc = None
            for c in range(n):
                p = jnp.exp2((s_ref[span(c), g * W:(g + 1) * W] - m).astype(BF16))
                pv = jnp.dot(vt_ref[g, :, span(c)], p, preferred_element_type=F32)
                acc = pv if acc is None else acc + pv
            o = acc[0:HEAD_DIM] / acc[HEAD_DIM:HEAD_DIM + 1]
            for hq in range(GROUP):
                h = g * GROUP + hq
                o_ref[0, :, h * HEAD_DIM:(h + 1) * HEAD_DIM] = o[:, hq * T:(hq + 1) * T].T

    for n in range(1, SEQ // CHUNK + 1):
        pl.when(nch == n)(functools.partial(run, n))


def _attn_prompt(hb_p, hf_p, bias_p):
    hb3 = hb_p.reshape(BATCH, SEQ, D_HB)
    hf3 = hf_p.reshape(BATCH, SEQ, D_HF)
    nqb = SEQ // Q_BLOCK
    T = Q_BLOCK
    qi_w = IDX_HEADS * IDX_DIM
    return pl.pallas_call(
        _attn_p_kernel,
        grid=(BATCH, nqb),
        in_specs=[
            pl.BlockSpec((1, T, D_ATTN), lambda b, i: (b, i, C_Q // D_ATTN)),
            pl.BlockSpec((1, T, qi_w), lambda b, i: (b, i, F_QI // qi_w)),
            pl.BlockSpec((1, T, LANE), lambda b, i: (b, i, F_KI // LANE)),
            pl.BlockSpec((1, SEQ, D_KV), lambda b, i: (b, 0, F_K // D_KV)),
            pl.BlockSpec((1, SEQ, D_KV), lambda b, i: (b, 0, F_V // D_KV)),
            pl.BlockSpec((1, SEQ, LANE), lambda b, i: (b, 0, F_KI // LANE)),
            pl.BlockSpec((N_HEADS, CHUNK, T), lambda b, i: (0, 0, 0)),
            pl.BlockSpec((MXU_WIDTH, MXU_WIDTH), lambda b, i: (0, 0)),
        ],
        out_specs=pl.BlockSpec((1, T, D_ATTN), lambda b, i: (b, i, 0)),
        out_shape=jax.ShapeDtypeStruct((BATCH, SEQ, D_ATTN), F32),
        scratch_shapes=[
            pltpu.VMEM((PAD_KEYS + SEQ, D_KV), BF16),
            pltpu.VMEM((N_KV_HEADS, VT_ROWS, PAD_KEYS + SEQ), BF16),
            pltpu.VMEM((PAD_KEYS + SEQ, IDX_DIM), BF16),
            pltpu.VMEM((N_HEADS, T, HEAD_DIM), BF16),
            pltpu.VMEM((IDX_HEADS, T, IDX_DIM), BF16),
            pltpu.VMEM((PAD_KEYS + SEQ, T), F32),
            pltpu.VMEM((PAD_KEYS + SEQ, T), F32),
            pltpu.VMEM((PAD_KEYS + SEQ, N_HEADS * T), F32),
        ],
        compiler_params=_params(("parallel", "arbitrary")),
        name="attn_prompt",
    )(hb3, hf3, hf3, hf3, hf3, hf3, bias_p, jnp.tri(MXU_WIDTH, dtype=BF16))


def _score_s_kernel(pt_ref, qi_ref, w_ref, kin_ref, *rest):
    npg = SCORE_ROWS * N_PAGES
    kip = rest[0:npg]
    o_ref = rest[npg]
    kinp_ref = rest[npg + 1]
    del pt_ref
    R = DEC_SEQ
    qrow = lax.broadcasted_iota(jnp.int32, (R, LANE), 0)
    lane = lax.broadcasted_iota(jnp.int32, (R, LANE), 1)
    kinp_ref[...] = jnp.zeros(kinp_ref.shape, BF16)
    for r in range(SCORE_ROWS):
        kinp_ref[r, 0:2 * R] = jnp.concatenate(
            [kin_ref[r][:, :IDX_DIM], jnp.zeros((R, IDX_DIM), F32)], 0).astype(BF16)
    for r in range(SCORE_ROWS):
        qi = qi_ref[r].astype(BF16)
        wb = jnp.broadcast_to(w_ref[r] * SCORE_SCALE, (IDX_HEADS * R, LANE))
        for t in range(N_PAGES + 1):
            if t < N_PAGES:
                d = jnp.dot(qi, kip[r * N_PAGES + t][0].astype(BF16), preferred_element_type=F32)
            else:
                d = lax.dot_general(qi, kinp_ref[r], (((1,), (1,)), ((), ())),
                                    preferred_element_type=F32)
            e = (jnp.maximum(d, 0.0) * wb).reshape(IDX_HEADS, R, LANE)
            s = e[0]
            for h in range(1, IDX_HEADS):
                s = s + e[h]
            if t == N_PAGES:
                s = jnp.where(lane <= qrow, s, NEG_INF)
            o_ref[r, :, t * LANE:(t + 1) * LANE] = s


def _score_sample(hf_s, page_table, cache_kidx_t):
    R = DEC_SEQ
    G = SCORE_ROWS
    h3 = hf_s.reshape(DEC_BATCH, R, D_HF)
    qi_hq = h3[:, :, F_QI:F_QI + IDX_HEADS * IDX_DIM].reshape(DEC_BATCH, R, IDX_HEADS, IDX_DIM)
    qi_hq = qi_hq.transpose(0, 2, 1, 3).reshape(DEC_BATCH, IDX_HEADS * R, IDX_DIM)
    w_hq = h3[:, :, F_KI + W_OFF:F_KI + W_OFF + IDX_HEADS].transpose(0, 2, 1)
    w_hq = w_hq.reshape(DEC_BATCH, IDX_HEADS * R, 1)
    in_specs = [
        pl.BlockSpec((G, IDX_HEADS * R, IDX_DIM), lambda b, pt: (b, 0, 0)),
        pl.BlockSpec((G, IDX_HEADS * R, 1), lambda b, pt: (b, 0, 0)),
        pl.BlockSpec((G, R, LANE), lambda b, pt: (b, 0, F_KI // LANE)),
    ]
    in_specs += [pl.BlockSpec((1, IDX_DIM, PAGE_SIZE),
                              lambda b, pt, r=r, p=p: (pt[b * G + r, p], 0, 0))
                 for r in range(G) for p in range(N_PAGES)]
    grid_spec = pltpu.PrefetchScalarGridSpec(
        num_scalar_prefetch=1,
        grid=(DEC_BATCH // G,),
        in_specs=in_specs,
        out_specs=pl.BlockSpec((G, R, L_SAMPLE), lambda b, pt: (b, 0, 0)),
        scratch_shapes=[pltpu.VMEM((G, PAGE_SIZE, IDX_DIM), BF16)],
    )
    return pl.pallas_call(
        _score_s_kernel,
        grid_spec=grid_spec,
        out_shape=jax.ShapeDtypeStruct((DEC_BATCH, R, L_SAMPLE), F32),
        compiler_params=_params(("arbitrary",)),
        name="score_sample",
    )(page_table, qi_hq, w_hq, h3, *([cache_kidx_t] * (G * N_PAGES)))


def _select_s_kernel(s_ref, tri_ref, o_ref):
    nt = N_PAGES + 1

    def tile_fn(j):
        return s_ref[j * LANE:(j + 1) * LANE, :].reshape(SUBTILES, SUBLANE, LANE)

    thr, need = _select_threshold(tile_fn, nt, static=True)
    parts = [slice(j * LANE, (j + 1) * LANE) for j in range(nt)]
    masks, _ = _selection_masks([s_ref[rows, :] for rows in parts], thr, need,
                                jnp.zeros((1, LANE), F32), tri_ref[...])
    for rows, mask in zip(parts, masks):
        o_ref[rows, :] = mask


def _select_sample(scores_t):
    n = scores_t.shape[1]
    return pl.pallas_call(
        _select_s_kernel,
        grid=(n // LANE,),
        in_specs=[pl.BlockSpec((L_SAMPLE, LANE), lambda c: (0, c)),
                  pl.BlockSpec((LANE, LANE), lambda c: (0, 0))],
        out_specs=pl.BlockSpec((L_SAMPLE, LANE), lambda c: (0, c)),
        out_shape=jax.ShapeDtypeStruct((L_SAMPLE, n), F32),
        compiler_params=_params(("parallel",)),
        name="select_sample",
    )(scores_t, jnp.tri(LANE, dtype=BF16))


def _attn_s_kernel(pt_ref, q_ref, kn_ref, vn_ref, mask_ref, bias_ref, *rest):
    npg = ATTN_ROWS * N_PAGES
    kp = rest[0:npg]
    vp = rest[npg:2 * npg]
    o_ref = rest[2 * npg]
    knp_ref, vnp_ref, logit_ref = rest[2 * npg + 1:]
    del pt_ref
    R = DEC_SEQ
    NT = N_PAGES + 1
    GR = GROUP * R

    knp_ref[...] = jnp.zeros(knp_ref.shape, BF16)
    vnp_ref[...] = jnp.zeros(vnp_ref.shape, BF16)
    for r in range(ATTN_ROWS):
        knp_ref[r, 0:2 * R] = jnp.concatenate([kn_ref[r], jnp.zeros((R, D_KV), F32)], 0).astype(BF16)
        vnp_ref[r, 0:2 * R] = jnp.concatenate([vn_ref[r], jnp.zeros((R, D_KV), F32)], 0).astype(BF16)

    def page_head(refs, pad_ref, r, t, g):
        if t < N_PAGES:
            return refs[r * N_PAGES + t][pl.ds(g, PAGE_SIZE, stride=N_KV_HEADS), :].astype(BF16)
        return pad_ref[r, :, g * HEAD_DIM:(g + 1) * HEAD_DIM]

    for r in range(ATTN_ROWS):
        q = (q_ref[r].astype(F32) * QK_SCALE).astype(BF16)
        for t in range(NT):
            mb = mask_ref[r, :, t * LANE:(t + 1) * LANE]
            for g in range(N_KV_HEADS):
                lg = lax.dot_general(q[g * GR:(g + 1) * GR], page_head(kp, knp_ref, r, t, g),
                                     (((1,), (1,)), ((), ())), preferred_element_type=F32)
                lg = lg.reshape(GROUP, R, LANE) + mb[None]
                if t >= N_PAGES - 1:
                    off = (t - (N_PAGES - 1)) * LANE
                    lg = lg + bias_ref[g * GROUP:(g + 1) * GROUP, :, off:off + LANE]
                logit_ref[r, g * GR:(g + 1) * GR, t * LANE:(t + 1) * LANE] = lg.reshape(GR, LANE)

    for r in range(ATTN_ROWS):
        logits = logit_ref[r]
        m = jnp.max(logits, axis=1, keepdims=True)
        p = jnp.exp(logits - m)
        inv = 1.0 / jnp.sum(p, axis=1, keepdims=True)
        pb = p.astype(BF16)
        outs = [jnp.zeros((GR, HEAD_DIM), F32) for _ in range(N_KV_HEADS)]
        for t in range(NT):
            for g in range(N_KV_HEADS):
                outs[g] = outs[g] + jnp.dot(pb[g * GR:(g + 1) * GR, t * LANE:(t + 1) * LANE],
                                            page_head(vp, vnp_ref, r, t, g),
                                            preferred_element_type=F32)
        for g in range(N_KV_HEADS):
            o = outs[g] * inv[g * GR:(g + 1) * GR]
            for hq in range(GROUP):
                h = g * GROUP + hq
                o_ref[r, :, h * HEAD_DIM:(h + 1) * HEAD_DIM] = o[hq * R:(hq + 1) * R]


def _attn_sample(hb_s, hf_s, page_table, cache_k, cache_v, mask, bias_s):
    R = DEC_SEQ
    G = ATTN_ROWS
    h3 = hf_s.reshape(DEC_BATCH, R, D_HF)
    q_hq = hb_s[:, C_Q:C_Q + D_ATTN].reshape(DEC_BATCH, R, N_HEADS, HEAD_DIM)
    q_hq = q_hq.transpose(0, 2, 1, 3).reshape(DEC_BATCH, N_HEADS * R, HEAD_DIM)
    rows_per_page = PAGE_SIZE * N_KV_HEADS
    ck = cache_k.reshape(-1, HEAD_DIM)
    cv = cache_v.reshape(-1, HEAD_DIM)

    in_specs = [
        pl.BlockSpec((G, N_HEADS * R, HEAD_DIM), lambda b, pt: (b, 0, 0)),
        pl.BlockSpec((G, R, D_KV), lambda b, pt: (b, 0, F_K // D_KV)),
        pl.BlockSpec((G, R, D_KV), lambda b, pt: (b, 0, F_V // D_KV)),
        pl.BlockSpec((G, R, L_SAMPLE), lambda b, pt: (b, 0, 0)),
        pl.BlockSpec((N_HEADS, R, 2 * LANE), lambda b, pt: (0, 0, 0)),
    ]
    pages = [pl.BlockSpec((rows_per_page, HEAD_DIM), lambda b, pt, r=r, p=p: (pt[b * G + r, p], 0))
             for r in range(G) for p in range(N_PAGES)]
    in_specs += pages + pages
    grid_spec = pltpu.PrefetchScalarGridSpec(
        num_scalar_prefetch=1,
        grid=(DEC_BATCH // G,),
        in_specs=in_specs,
        out_specs=pl.BlockSpec((G, R, D_ATTN), lambda b, pt: (b, 0, 0)),
        scratch_shapes=[
            pltpu.VMEM((G, PAGE_SIZE, D_KV), BF16),
            pltpu.VMEM((G, PAGE_SIZE, D_KV), BF16),
            pltpu.VMEM((G, N_HEADS * R, L_SAMPLE), F32),
        ],
    )
    return pl.pallas_call(
        _attn_s_kernel,
        grid_spec=grid_spec,
        out_shape=jax.ShapeDtypeStruct((DEC_BATCH, R, D_ATTN), F32),
        compiler_params=_params(("arbitrary",)),
        name="attn_sample",
    )(page_table, q_hq, h3, h3, mask, bias_s, *([ck] * (G * N_PAGES)), *([cv] * (G * N_PAGES)))


def _conv_p_kernel(val_ref, gate_ref, cw_ref, cb_ref, dw_ref, ut_ref, pad_ref):
    pad_ref[0:PAD_ROWS] = jnp.zeros((PAD_ROWS, LANE), F32)
    pad_ref[PAD_ROWS:] = val_ref[0].astype(F32) * jax.nn.sigmoid(gate_ref[0].astype(F32))
    ut_ref[0] = pad_ref[SEQ:SEQ + PAD_ROWS]
    cw = cw_ref[...]
    cb = cb_ref[...]
    first = PAD_ROWS - (CONV_WIDTH - 1)
    for c in range(SEQ // CONV_CHUNK):
        base = c * CONV_CHUNK
        acc = jnp.broadcast_to(cb, (CONV_CHUNK, LANE))
        for r in range(SUBLANE):
            taps = [w for w in range(CONV_WIDTH) if (first + w) % SUBLANE == r]
            span = max(first + w - r for w in taps) + CONV_CHUNK
            win = pad_ref[base + r:base + r + span]
            for w in taps:
                a = first + w - r
                acc = acc + win[a:a + CONV_CHUNK] * cw[w:w + 1]
        dw_ref[0, base:base + CONV_CHUNK] = acc


def _conv_prompt(hb_p, cw_pad, cb):
    h3 = hb_p.reshape(BATCH, SEQ, D_HB)
    nc = D_CONV // LANE
    return pl.pallas_call(
        _conv_p_kernel,
        grid=(BATCH, nc),
        in_specs=[pl.BlockSpec((1, SEQ, LANE), lambda b, c: (b, 0, C_GV // LANE + c)),
                  pl.BlockSpec((1, SEQ, LANE), lambda b, c: (b, 0, C_GG // LANE + c)),
                  pl.BlockSpec((PAD_ROWS, LANE), lambda b, c: (0, c)),
                  pl.BlockSpec((1, LANE), lambda b, c: (0, c))],
        out_specs=[pl.BlockSpec((1, SEQ, LANE), lambda b, c: (b, 0, c)),
                   pl.BlockSpec((1, PAD_ROWS, LANE), lambda b, c: (b, 0, c))],
        out_shape=[jax.ShapeDtypeStruct((BATCH, SEQ, D_CONV), F32),
                   jax.ShapeDtypeStruct((BATCH, PAD_ROWS, D_CONV), F32)],
        scratch_shapes=[pltpu.VMEM((PAD_ROWS + SEQ, LANE), F32)],
        compiler_params=_params(("parallel", "parallel")),
        name="conv_prompt",
    )(h3, h3, cw_pad, cb)


def _conv_s_kernel(val_ref, gate_ref, st_ref, cw_ref, cb_ref, dw_ref, ns_ref):
    R = DEC_SEQ
    H = CONV_WIDTH - 1
    cw = cw_ref[...]
    cb = jnp.broadcast_to(cb_ref[...], (DEC_BATCH, LANE))
    u = []
    for q in range(R):
        rows = pl.ds(q, DEC_BATCH, stride=R)
        u.append(val_ref[rows, :] * jax.nn.sigmoid(gate_ref[rows, :]))

    def row(r):
        return st_ref[r] if r < H else u[r - H]

    for q in range(R):
        acc = cb
        for w in range(CONV_WIDTH):
            acc = acc + row(q + w) * cw[w:w + 1]
        dw_ref[pl.ds(q, DEC_BATCH, stride=R), :] = acc
    for r in range(H):
        ns_ref[r] = row(r + R)


def _conv_sample(val, gate, state_t, cw_pad, cb):
    n_s = DEC_BATCH * DEC_SEQ
    H = CONV_WIDTH - 1
    nc = D_CONV // LANE
    return pl.pallas_call(
        _conv_s_kernel,
        grid=(nc,),
        in_specs=[pl.BlockSpec((n_s, LANE), lambda c: (0, c)),
                  pl.BlockSpec((n_s, LANE), lambda c: (0, c)),
                  pl.BlockSpec((H, DEC_BATCH, LANE), lambda c: (0, 0, c)),
                  pl.BlockSpec((PAD_ROWS, LANE), lambda c: (0, c)),
                  pl.BlockSpec((1, LANE), lambda c: (0, c))],
        out_specs=[pl.BlockSpec((n_s, LANE), lambda c: (0, c)),
                   pl.BlockSpec((H, DEC_BATCH, LANE), lambda c: (0, 0, c))],
        out_shape=[jax.ShapeDtypeStruct((n_s, D_CONV), F32),
                   jax.ShapeDtypeStruct((H, DEC_BATCH, D_CONV), F32)],
        compiler_params=_params(("parallel",)),
        name="conv_sample",
    )(val, gate, state_t, cw_pad, cb)


def _tail_kernel(attn_ref, za_ref, dw_ref, zc_ref, ga_ref, gc_ref, x_ref,
                 wua_ref, wpw_ref, wuc_ref, wo_ref, ng_ref, nb_ref, bpw_ref, fg_ref, y_ref):
    a = attn_ref[...] * jax.nn.silu(za_ref[...].astype(F32))
    branch_attn = jnp.dot(a.astype(BF16), wua_ref[...], preferred_element_type=F32)

    dw = dw_ref[...]
    mu = jnp.mean(dw, axis=-1, keepdims=True)
    var = jnp.mean(jnp.square(dw - mu), axis=-1, keepdims=True)
    ln = (dw - mu) * lax.rsqrt(var + EPS) * ng_ref[...] + nb_ref[...]
    conv_out = jnp.dot(jax.nn.silu(ln).astype(BF16), wpw_ref[...],
                       preferred_element_type=F32) + bpw_ref[...]
    c = conv_out * jax.nn.silu(zc_ref[...].astype(F32))
    branch_conv = jnp.dot(c.astype(BF16), wuc_ref[...], preferred_element_type=F32)

    merged = (jax.nn.sigmoid(ga_ref[...].astype(F32)) * branch_attn
              + jax.nn.sigmoid(gc_ref[...].astype(F32)) * branch_conv)
    y = x_ref[...] + jnp.dot(merged.astype(BF16), wo_ref[...], preferred_element_type=F32)
    ms = jnp.mean(y * y, axis=-1, keepdims=True)
    y_ref[...] = y * lax.rsqrt(ms + EPS) * fg_ref[...]


def _tail(h, attn, dw, x2d, weights, tm):
    n = x2d.shape[0]

    def const(shape):
        return pl.BlockSpec(shape, lambda i: (0, 0), pipeline_mode=pl.Buffered(1))

    def cols(width, offset):
        return pl.BlockSpec((tm, width), lambda i: (i, offset // width))

    return pl.pallas_call(
        _tail_kernel,
        grid=(n // tm,),
        in_specs=[cols(D_ATTN, 0), cols(D_ATTN, C_ZA), cols(D_CONV, 0), cols(D_CONV, C_ZC),
                  cols(D_MODEL, C_GA), cols(D_MODEL, C_GC), cols(D_MODEL, 0),
                  const((D_ATTN, D_MODEL)), const((D_CONV, D_CONV)),
                  const((D_CONV, D_MODEL)), const((D_MODEL, D_MODEL)),
                  const((1, D_CONV)), const((1, D_CONV)), const((1, D_CONV)),
                  const((1, D_MODEL))],
        out_specs=pl.BlockSpec((tm, D_MODEL), lambda i: (i, 0)),
        out_shape=jax.ShapeDtypeStruct((n, D_MODEL), F32),
        compiler_params=_params(("parallel",)),
        name="tail",
    )(attn, h, dw, h, h, h, x2d, *weights)


def _reorder_kernel(off_ref, w_ref, o_ref):
    del off_ref
    o_ref[...] = w_ref[...].astype(BF16)


def _reorder_w_in(w_in):
    w_t = w_in.T
    src = []
    for first, dest, width in (('gate_attn', C_GA, 2 * D_MODEL), ('q', C_Q, D_ATTN),
                               ('z_attn', C_ZA, D_ATTN), ('glu_val', C_GV, 3 * D_CONV),
                               ('k', C_K, 2 * D_KV), ('q_idx', C_QI, D_H - C_QI)):
        assert dest == len(src) * REORDER_ROWS
        for r in range(0, width, REORDER_ROWS):
            src.append(min(_SRC[first][0] + r, D_IN - REORDER_ROWS) // SUBLANE)
    grid_spec = pltpu.PrefetchScalarGridSpec(
        num_scalar_prefetch=1,
        grid=(len(src),),
        in_specs=[pl.BlockSpec((pl.Element(REORDER_ROWS), pl.Element(D_MODEL)),
                               lambda d, off: (off[d] * SUBLANE, 0))],
        out_specs=pl.BlockSpec((REORDER_ROWS, D_MODEL), lambda d, off: (d, 0)),
    )
    return pl.pallas_call(
        _reorder_kernel,
        grid_spec=grid_spec,
        out_shape=jax.ShapeDtypeStruct((D_H, D_MODEL), BF16),
        compiler_params=_params(("arbitrary",)),
        name="reorder_w_in",
    )(jnp.asarray(src, jnp.int32), w_t)


def kernel(x_prompt, x_sample, cache_k, cache_v, cache_kidx, state_conv, page_table,
           ln_g, w_in, conv_w, conv_b, conv_norm_g, conv_norm_b, w_pw, b_pw,
           w_up_attn, w_up_conv, w_out, rel_bias, final_g):
    w_all = _reorder_w_in(w_in[0])
    g_in = ln_g[0].reshape(1, D_MODEL)
    xp = x_prompt.reshape(BATCH * SEQ, D_MODEL)
    xs = x_sample.reshape(DEC_BATCH * DEC_SEQ, D_MODEL)
    hb_p, hf_p, k_p, v_p = _proj(xp, g_in, w_all)
    hb_s, hf_s, k_s, v_s = _proj(xs, g_in, w_all)

    bias_p, bias_s = _bias_tables(rel_bias)

    attn_p = _attn_prompt(hb_p, hf_p, bias_p).reshape(BATCH * SEQ, D_ATTN)

    n_s = DEC_BATCH * DEC_SEQ
    scores = _score_sample(hf_s, page_table, cache_kidx[0].transpose(0, 2, 1))
    mask_t = _select_sample(scores.reshape(n_s, L_SAMPLE).T)
    mask = mask_t.T.reshape(DEC_BATCH, DEC_SEQ, L_SAMPLE)
    attn_s = _attn_sample(hb_s, hf_s, page_table, cache_k[0], cache_v[0], mask, bias_s)
    attn_s = attn_s.reshape(n_s, D_ATTN)

    cw_pad = jnp.concatenate([conv_w[0], jnp.zeros((PAD_ROWS - CONV_WIDTH, D_CONV), F32)], 0)
    cb = conv_b[0].reshape(1, D_CONV)
    dw_p, u_tail = _conv_prompt(hb_p, cw_pad, cb)
    dw_s, state_new = _conv_sample(hb_s[:, C_GV:C_GV + D_CONV].astype(F32),
                                   hb_s[:, C_GG:C_GG + D_CONV].astype(F32),
                                   state_conv[0].transpose(1, 0, 2), cw_pad, cb)

    wua = w_up_attn[0].astype(BF16)
    wpw = w_pw[0].astype(BF16)
    wuc = w_up_conv[0].astype(BF16)
    wo = w_out[0].astype(BF16)
    ng = conv_norm_g[0].reshape(1, D_CONV)
    nb = conv_norm_b[0].reshape(1, D_CONV)
    bpw = b_pw[0].reshape(1, D_CONV)
    fg = final_g.reshape(1, D_MODEL)
    weights = (wua, wpw, wuc, wo, ng, nb, bpw, fg)
    y_p = _tail(hb_p, attn_p, dw_p.reshape(BATCH * SEQ, D_CONV), xp, weights, TAIL_TM)
    y_s = _tail(hb_s, attn_s, dw_s, xs, weights, TAIL_TM)

    tail_rows = CONV_WIDTH - 1
    return (
        y_p.reshape(BATCH, SEQ, D_MODEL),
        y_s.reshape(DEC_BATCH, DEC_SEQ, D_MODEL),
        k_p.reshape(1, BATCH, SEQ, N_KV_HEADS, HEAD_DIM),
        v_p.reshape(1, BATCH, SEQ, N_KV_HEADS, HEAD_DIM),
        hf_p[:, F_KI:F_KI + IDX_DIM].reshape(1, BATCH, SEQ, IDX_DIM),
        u_tail[:, PAD_ROWS - tail_rows:].reshape(1, BATCH, tail_rows, D_CONV),
        k_s.reshape(1, DEC_BATCH, DEC_SEQ, N_KV_HEADS, HEAD_DIM),
        v_s.reshape(1, DEC_BATCH, DEC_SEQ, N_KV_HEADS, HEAD_DIM),
        hf_s[:, F_KI:F_KI + IDX_DIM].reshape(1, DEC_BATCH, DEC_SEQ, IDX_DIM),
        state_new.transpose(1, 0, 2).reshape(1, DEC_BATCH, tail_rows, D_CONV),
    )
```

```python
import functools
import math

import numpy as np
import jax
import jax.numpy as jnp
from jax import lax
from jax.experimental import pallas as pl
from jax.experimental.pallas import tpu as pltpu

F32 = jnp.float32
BF16 = jnp.bfloat16

D_MODEL = 2048
BATCH = 8
SEQ = 2048
DEC_BATCH = 128
DEC_SEQ = 8
PAST_LEN = 2048
PAGE_SIZE = 128
N_PAGES = PAST_LEN // PAGE_SIZE
N_HEADS = 8
N_KV_HEADS = 2
HEAD_DIM = 128
GROUP = N_HEADS // N_KV_HEADS
D_ATTN = N_HEADS * HEAD_DIM
D_KV = N_KV_HEADS * HEAD_DIM
IDX_HEADS = 8
IDX_DIM = 64
TOPK = 256
Q_BLOCK = 128
N_BUCKETS = 32
MAX_DISTANCE = 128
D_CONV = D_MODEL // 2
CONV_WIDTH = 31
EPS = 1e-6
NEG = -1e30
NEG_INF = float("-inf")

LANE = 128
SUBLANE = 8
VMEM_LIMIT = 56 * 1024 * 1024

C_GA = 0
C_GC = C_GA + D_MODEL
C_Q = C_GC + D_MODEL
C_ZA = C_Q + D_ATTN
C_GV = C_ZA + D_ATTN
C_GG = C_GV + D_CONV
C_ZC = C_GG + D_CONV
C_K = C_ZC + D_CONV
C_V = C_K + D_KV
C_QI = C_V + D_KV
C_KI = C_QI + IDX_HEADS * IDX_DIM
W_OFF = IDX_DIM
MXU_WIDTH = 256
PROJ_TM = 1024
PROJ_TN = 6 * MXU_WIDTH
D_H = -(-(C_KI + LANE) // PROJ_TN) * PROJ_TN
KV_TILE = C_K // PROJ_TN
assert C_K % PROJ_TN == 0 and D_H == C_K + PROJ_TN
D_HB = C_K
D_HF = C_KI + LANE - C_K
F_K = C_K - D_HB
F_V = C_V - D_HB
F_QI = C_QI - D_HB
F_KI = C_KI - D_HB

_SRC = {}
_off = 0
for _name, _w in (('q', D_ATTN), ('k', D_KV), ('v', D_KV), ('z_attn', D_ATTN),
                  ('q_idx', IDX_HEADS * IDX_DIM), ('k_idx', IDX_DIM), ('w_idx', IDX_HEADS),
                  ('glu_val', D_CONV), ('glu_gate', D_CONV), ('z_conv', D_CONV),
                  ('gate_attn', D_MODEL), ('gate_conv', D_MODEL)):
    _SRC[_name] = (_off, _w)
    _off += _w
D_IN = _off
REORDER_ROWS = 512

N_BISECT = 20
PAD_ROWS = 32
CONV_CHUNK = 64
SCORE_SCALE = (IDX_DIM ** -0.5) * (IDX_HEADS ** -0.5)
QK_SCALE = HEAD_DIM ** -0.5
LOG2E = math.log2(math.e)
L_SAMPLE = (N_PAGES + 1) * LANE
SUBTILES = LANE // SUBLANE
CHUNK_BLOCKS = 2
CHUNK = CHUNK_BLOCKS * Q_BLOCK
PAD_KEYS = CHUNK - Q_BLOCK
VT_ROWS = HEAD_DIM + 2 * SUBLANE
SCORE_ROWS = 4
ATTN_ROWS = 4
TAIL_TM = 256


def _t5_bucket_static(dist):
    n = np.maximum(dist, 0)
    max_exact = N_BUCKETS // 2
    ratio = (np.log(np.maximum(n, 1).astype(np.float32) / np.float32(max_exact))
             / np.float32(math.log(MAX_DISTANCE / max_exact)))
    large = np.minimum(max_exact + (ratio * np.float32(N_BUCKETS - max_exact)).astype(np.int32),
                       N_BUCKETS - 1)
    return np.where(n < max_exact, n, large).astype(np.int32)


FAR_BUCKET = int(_t5_bucket_static(np.array([2 * MAX_DISTANCE]))[0])


def _params(sem):
    return pltpu.CompilerParams(dimension_semantics=sem, vmem_limit_bytes=VMEM_LIMIT)


def _proj_kernel(x_ref, g_ref, w_ref, ob_ref, of_ref, k_ref, v_ref, xn_ref):
    tm = x_ref.shape[0]
    j = pl.program_id(1)

    @pl.when(j == 0)
    def _():
        x = x_ref[...]
        ms = jnp.mean(x * x, axis=-1, keepdims=True)
        xn_ref[...] = (x * lax.rsqrt(ms + EPS) * g_ref[...]).astype(BF16)

    def project():
        return lax.dot_general(xn_ref[...], w_ref[...], (((1,), (1,)), ((), ())),
                               preferred_element_type=F32)

    @pl.when(j < KV_TILE)
    def _():
        ob_ref[...] = project().astype(BF16)

    @pl.when(j == KV_TILE)
    def _():
        of_ref[...] = project()[:, :D_HF]
        for g in range(N_KV_HEADS):
            rows = pl.ds(g, tm, stride=N_KV_HEADS)
            k_ref[rows, :] = of_ref[:, F_K + g * HEAD_DIM:F_K + (g + 1) * HEAD_DIM]
            v_ref[rows, :] = of_ref[:, F_V + g * HEAD_DIM:F_V + (g + 1) * HEAD_DIM]


def _proj(x2d, g, w):
    n = x2d.shape[0]
    tm, tn = PROJ_TM, PROJ_TN
    kv_spec = pl.BlockSpec((N_KV_HEADS * tm, HEAD_DIM), lambda i, j: (i, 0))
    kv_shape = jax.ShapeDtypeStruct((N_KV_HEADS * n, HEAD_DIM), F32)
    return pl.pallas_call(
        _proj_kernel,
        grid=(n // tm, D_H // tn),
        in_specs=[pl.BlockSpec((tm, D_MODEL), lambda i, j: (i, 0)),
                  pl.BlockSpec((1, D_MODEL), lambda i, j: (0, 0)),
                  pl.BlockSpec((tn, D_MODEL), lambda i, j: (j, 0))],
        out_specs=[pl.BlockSpec((tm, tn), lambda i, j: (i, jnp.minimum(j, KV_TILE - 1))),
                   pl.BlockSpec((tm, D_HF), lambda i, j: (i, 0)), kv_spec, kv_spec],
        out_shape=[jax.ShapeDtypeStruct((n, D_HB), BF16),
                   jax.ShapeDtypeStruct((n, D_HF), F32), kv_shape, kv_shape],
        scratch_shapes=[pltpu.VMEM((tm, D_MODEL), BF16)],
        compiler_params=_params(("parallel", "arbitrary")),
        name="proj",
    )(x2d, g, w)


def _bias_kernel(rb_ref, bp_ref, bs_ref, op_ref, os_ref):
    bp = bp_ref[...]
    bs = bs_ref[...]
    for h in range(N_HEADS):
        far = rb_ref[FAR_BUCKET, h]
        tp = jnp.zeros(bp.shape, F32)
        ts = jnp.zeros(bs.shape, F32)
        for b in range(N_BUCKETS):
            val = rb_ref[b, h] - far
            tp = jnp.where(bp == b, val * LOG2E, tp)
            ts = jnp.where(bs == b, val, ts)
        op_ref[h] = tp
        os_ref[h] = ts


def _bias_tables(rel_bias):
    key = np.arange(CHUNK)[:, None]
    qry = np.arange(Q_BLOCK)[None, :]
    bucket_p = _t5_bucket_static(CHUNK - Q_BLOCK + qry - key)
    qi = np.arange(DEC_SEQ)[:, None]
    col = np.arange(2 * LANE)[None, :]
    dist_s = np.where(col < LANE, LANE + qi - col, qi - (col - LANE))
    bucket_s = _t5_bucket_static(dist_s)
    return pl.pallas_call(
        _bias_kernel,
        in_specs=[pl.BlockSpec(memory_space=pltpu.SMEM),
                  pl.BlockSpec(memory_space=pltpu.VMEM),
                  pl.BlockSpec(memory_space=pltpu.VMEM)],
        out_specs=[pl.BlockSpec(memory_space=pltpu.VMEM),
                   pl.BlockSpec(memory_space=pltpu.VMEM)],
        out_shape=[jax.ShapeDtypeStruct((N_HEADS, CHUNK, Q_BLOCK), F32),
                   jax.ShapeDtypeStruct((N_HEADS, DEC_SEQ, 2 * LANE), F32)],
        name="bias_tables",
    )(rel_bias, jnp.asarray(bucket_p), jnp.asarray(bucket_s))


def _any(x):
    return jnp.max(jnp.where(x, 1.0, 0.0)) > 0.5


def _rep(x):
    return jnp.broadcast_to(x, (SUBLANE, LANE))


def _fold_rows(x, comb):
    parts = [x[k:k + SUBLANE] for k in range(0, x.shape[0], SUBLANE)]
    while len(parts) > 1:
        parts = [comb(parts[k], parts[k + 1]) for k in range(0, len(parts), 2)]
    return parts[0]


def _select_threshold(tile_fn, ntiles, static):
    def reduce_tiles(fn, init, comb):
        def step(j, acc):
            x = fn(tile_fn(j), j)
            parts = [x[k] for k in range(x.shape[0])]
            while len(parts) > 1:
                parts = [comb(parts[k], parts[k + 1]) for k in range(0, len(parts), 2)]
            return comb(acc, parts[0])
        if static:
            acc = init
            for j in range(ntiles):
                acc = step(j, acc)
            return acc
        return lax.fori_loop(0, ntiles, step, init)

    zeros = jnp.zeros((SUBLANE, LANE), F32)

    def count(pred_fn):
        acc = reduce_tiles(lambda s, j: jnp.where(pred_fn(s, j), 1.0, 0.0), zeros,
                           lambda a, b: a + b)
        return _rep(jnp.sum(acc, axis=0, keepdims=True))

    def masked_max(pred_fn):
        acc = reduce_tiles(lambda s, j: jnp.where(pred_fn(s, j), s, NEG_INF),
                           jnp.full((SUBLANE, LANE), NEG_INF, F32), jnp.maximum)
        return _rep(jnp.max(acc, axis=0, keepdims=True))

    bound = reduce_tiles(lambda s, j: jnp.where(s > NEG_INF, jnp.abs(s), 0.0), zeros,
                         jnp.maximum)
    bound = _rep(jnp.max(bound, axis=0, keepdims=True))

    def bisect(_, carry):
        lo, hi = carry
        mid = 0.5 * lo + 0.5 * hi
        few = count(lambda s, j: s > mid[None]) < TOPK
        return jnp.where(few, lo, mid), jnp.where(few, mid, hi)

    _, hi = lax.fori_loop(0, N_BISECT, bisect, (-bound, bound))

    thr = masked_max(lambda s, j: s <= hi[None])
    n_ge = count(lambda s, j: s >= thr[None])

    def fix_body(carry):
        thr, n_ge, _ = carry
        lower = masked_max(lambda s, j: s < thr[None])
        thr = jnp.where(n_ge < TOPK, lower, thr)
        n_ge = count(lambda s, j: s >= thr[None])
        return thr, n_ge, _any(n_ge < TOPK)

    thr, n_ge, _ = lax.while_loop(lambda c: c[2], fix_body, (thr, n_ge, _any(n_ge < TOPK)))
    n_gt = count(lambda s, j: s > thr[None])
    need = TOPK - n_gt
    return thr[0:1], need[0:1]


def _keep(pred):
    return jnp.where(pred, 0.0, NEG)


def _selection_masks(tiles, thr, need, seen, tri):
    ties = [s == thr for s in tiles]
    ranks = []
    for k in range(0, len(tiles), 2):
        pair = jnp.concatenate([jnp.where(t, 1.0, 0.0).astype(BF16) for t in ties[k:k + 2]], axis=1)
        rank = jnp.dot(tri, pair, preferred_element_type=F32)
        ranks += [rank[:, p * LANE:(p + 1) * LANE] for p in range(len(ties[k:k + 2]))]
    masks = []
    for s, tie, rank in zip(tiles, ties, ranks):
        masks.append(jnp.where(tie, _keep(rank + seen <= need), _keep(s > thr)))
        seen = seen + rank[s.shape[0] - 1:]
    return masks, seen


def _attn_p_kernel(q_ref, qi_ref, wi_ref, k_ref, v_ref, ki_ref, bias_ref, tri_ref, o_ref,
                   kb_ref, vt_ref, kib_ref, qh_ref, qih_ref,
                   score_ref, mask_ref, s_ref):
    i = pl.program_id(1)
    T = Q_BLOCK
    W = GROUP * T
    nch = i // CHUNK_BLOCKS + 1

    @pl.when(i == 0)
    def _():
        kb_ref[0:PAD_KEYS] = jnp.zeros((PAD_KEYS, D_KV), BF16)
        kb_ref[PAD_KEYS:] = k_ref[0].astype(BF16)
        kib_ref[0:PAD_KEYS] = jnp.zeros((PAD_KEYS, IDX_DIM), BF16)
        kib_ref[PAD_KEYS:] = ki_ref[0][:, :IDX_DIM].astype(BF16)
        ones_row = lax.broadcasted_iota(jnp.int32, (VT_ROWS - HEAD_DIM, PAD_KEYS + SEQ), 0) == 0
        for g in range(N_KV_HEADS):
            vt_ref[g, 0:HEAD_DIM, 0:PAD_KEYS] = jnp.zeros((HEAD_DIM, PAD_KEYS), BF16)
            vt_ref[g, HEAD_DIM:VT_ROWS, :] = jnp.where(ones_row, 1.0, 0.0).astype(BF16)
            for c in range(SEQ // LANE):
                blk = v_ref[0, c * LANE:(c + 1) * LANE, g * HEAD_DIM:(g + 1) * HEAD_DIM]
                vt_ref[g, 0:HEAD_DIM,
                       PAD_KEYS + c * LANE:PAD_KEYS + (c + 1) * LANE] = blk.T.astype(BF16)
        score_ref[0:PAD_KEYS] = jnp.full((PAD_KEYS, T), NEG_INF, F32)
        mask_ref[0:PAD_KEYS] = jnp.full((PAD_KEYS, T), NEG, F32)

    q = q_ref[0].astype(F32) * (QK_SCALE * LOG2E)
    for h in range(N_HEADS):
        qh_ref[h] = q[:, h * HEAD_DIM:(h + 1) * HEAD_DIM].astype(BF16)
    qi = qi_ref[0]
    for h in range(IDX_HEADS):
        qih_ref[h] = qi[:, h * IDX_DIM:(h + 1) * IDX_DIM].astype(BF16)
    w_rows = wi_ref[0].T[W_OFF:W_OFF + IDX_HEADS] * SCORE_SCALE

    def span(c):
        return pl.ds(pl.multiple_of((i - CHUNK_BLOCKS * c) * LANE, LANE), CHUNK)

    def first_key(c):
        return (i - CHUNK_BLOCKS * c) * LANE - PAD_KEYS

    key_l = lax.broadcasted_iota(jnp.int32, (CHUNK, T), 0)
    qry = i * T + lax.broadcasted_iota(jnp.int32, (CHUNK, T), 1)

    def score_chunk(c):
        kc = kib_ref[span(c), :]
        d = lax.dot_general(kc, qih_ref[...].reshape(IDX_HEADS * T, IDX_DIM),
                            (((1,), (1,)), ((), ())), preferred_element_type=F32)
        acc = jnp.zeros((CHUNK, T), F32)
        for h in range(IDX_HEADS):
            acc = acc + jnp.maximum(d[:, h * T:(h + 1) * T], 0.0) * w_rows[h:h + 1]
        key = first_key(c) + key_l
        acc = jnp.where(key <= qry, jnp.where(key >= 0, acc, NEG_INF), NEG_INF)
        score_ref[span(c), :] = acc

    nsub = CHUNK // SUBLANE

    def tile_fn(c):
        return score_ref[span(c), :].reshape(nsub, SUBLANE, T)

    def select_all():
        mask_ref[span(0), :] = _keep(score_ref[span(0), :] > NEG_INF)

    def select(n):
        thr, need = _select_threshold(tile_fn, n, static=True)
        parts = []
        for c in reversed(range(n)):
            start = pl.multiple_of((i - CHUNK_BLOCKS * c) * LANE, LANE)
            parts += [pl.ds(start + p * MXU_WIDTH, MXU_WIDTH) for p in range(CHUNK // MXU_WIDTH)]
        masks, _ = _selection_masks([score_ref[rows, :] for rows in parts], thr, need,
                                    jnp.zeros((1, T), F32), tri_ref[...])
        for rows, mask in zip(parts, masks):
            mask_ref[rows, :] = mask

    def logits(c, g):
        kc = kb_ref[span(c), g * HEAD_DIM:(g + 1) * HEAD_DIM]
        qg = qh_ref[g * GROUP:(g + 1) * GROUP].reshape(W, HEAD_DIM)
        s = lax.dot_general(kc, qg, (((1,), (1,)), ((), ())), preferred_element_type=F32)
        mb = mask_ref[span(c), :]
        if c == 0:
            add = jnp.concatenate([mb + bias_ref[g * GROUP + hq] for hq in range(GROUP)], axis=1)
        else:
            add = jnp.concatenate([mb] * GROUP, axis=1)
        s = s + add
        s_ref[span(c), g * W:(g + 1) * W] = s
        return jnp.max(_fold_rows(s, jnp.maximum), axis=0, keepdims=True)

    def run(n):
        for c in range(n):
            score_chunk(c)
        if n == 1:
            pl.when(i * T + T <= TOPK)(select_all)
            pl.when(i * T + T > TOPK)(functools.partial(select, n))
        else:
            select(n)
        for g in range(N_KV_HEADS):
            m = functools.reduce(jnp.maximum, [logits(c, g) for c in range(n)])
            acc = None
            for c in range(n):
                p = jnp.exp2((s_ref[span(c), g * W:(g + 1) * W] - m).astype(BF16))
                pv = jnp.dot(vt_ref[g, :, span(c)], p, preferred_element_type=F32)
                acc = pv if acc is None else acc + pv
            o = acc[0:HEAD_DIM] / acc[HEAD_DIM:HEAD_DIM + 1]
            for hq in range(GROUP):
                h = g * GROUP + hq
                o_ref[0, :, h * HEAD_DIM:(h + 1) * HEAD_DIM] = o[:, hq * T:(hq + 1) * T].T

    for n in range(1, SEQ // CHUNK + 1):
        pl.when(nch == n)(functools.partial(run, n))


def _attn_prompt(hb_p, hf_p, bias_p):
    hb3 = hb_p.reshape(BATCH, SEQ, D_HB)
    hf3 = hf_p.reshape(BATCH, SEQ, D_HF)
    nqb = SEQ // Q_BLOCK
    T = Q_BLOCK
    qi_w = IDX_HEADS * IDX_DIM
    return pl.pallas_call(
        _attn_p_kernel,
        grid=(BATCH, nqb),
        in_specs=[
            pl.BlockSpec((1, T, D_ATTN), lambda b, i: (b, i, C_Q // D_ATTN)),
            pl.BlockSpec((1, T, qi_w), lambda b, i: (b, i, F_QI // qi_w)),
            pl.BlockSpec((1, T, LANE), lambda b, i: (b, i, F_KI // LANE)),
            pl.BlockSpec((1, SEQ, D_KV), lambda b, i: (b, 0, F_K // D_KV)),
            pl.BlockSpec((1, SEQ, D_KV), lambda b, i: (b, 0, F_V // D_KV)),
            pl.BlockSpec((1, SEQ, LANE), lambda b, i: (b, 0, F_KI // LANE)),
            pl.BlockSpec((N_HEADS, CHUNK, T), lambda b, i: (0, 0, 0)),
            pl.BlockSpec((MXU_WIDTH, MXU_WIDTH), lambda b, i: (0, 0)),
        ],
        out_specs=pl.BlockSpec((1, T, D_ATTN), lambda b, i: (b, i, 0)),
        out_shape=jax.ShapeDtypeStruct((BATCH, SEQ, D_ATTN), F32),
        scratch_shapes=[
            pltpu.VMEM((PAD_KEYS + SEQ, D_KV), BF16),
            pltpu.VMEM((N_KV_HEADS, VT_ROWS, PAD_KEYS + SEQ), BF16),
            pltpu.VMEM((PAD_KEYS + SEQ, IDX_DIM), BF16),
            pltpu.VMEM((N_HEADS, T, HEAD_DIM), BF16),
            pltpu.VMEM((IDX_HEADS, T, IDX_DIM), BF16),
            pltpu.VMEM((PAD_KEYS + SEQ, T), F32),
            pltpu.VMEM((PAD_KEYS + SEQ, T), F32),
            pltpu.VMEM((PAD_KEYS + SEQ, N_HEADS * T), F32),
        ],
        compiler_params=_params(("parallel", "arbitrary")),
        name="attn_prompt",
    )(hb3, hf3, hf3, hf3, hf3, hf3, bias_p, jnp.tri(MXU_WIDTH, dtype=BF16))


def _score_s_kernel(pt_ref, qi_ref, w_ref, kin_ref, *rest):
    npg = SCORE_ROWS * N_PAGES
    kip = rest[0:npg]
    o_ref = rest[npg]
    kinp_ref = rest[npg + 1]
    del pt_ref
    R = DEC_SEQ
    qrow = lax.broadcasted_iota(jnp.int32, (R, LANE), 0)
    lane = lax.broadcasted_iota(jnp.int32, (R, LANE), 1)
    kinp_ref[...] = jnp.zeros(kinp_ref.shape, BF16)
    for r in range(SCORE_ROWS):
        kinp_ref[r, 0:2 * R] = jnp.concatenate(
            [kin_ref[r][:, :IDX_DIM], jnp.zeros((R, IDX_DIM), F32)], 0).astype(BF16)
    for r in range(SCORE_ROWS):
        qi = qi_ref[r].astype(BF16)
        wb = jnp.broadcast_to(w_ref[r] * SCORE_SCALE, (IDX_HEADS * R, LANE))
        for t in range(N_PAGES + 1):
            if t < N_PAGES:
                d = jnp.dot(qi, kip[r * N_PAGES + t][0].astype(BF16), preferred_element_type=F32)
            else:
                d = lax.dot_general(qi, kinp_ref[r], (((1,), (1,)), ((), ())),
                                    preferred_element_type=F32)
            e = (jnp.maximum(d, 0.0) * wb).reshape(IDX_HEADS, R, LANE)
            s = e[0]
            for h in range(1, IDX_HEADS):
                s = s + e[h]
            if t == N_PAGES:
                s = jnp.where(lane <= qrow, s, NEG_INF)
            o_ref[r, :, t * LANE:(t + 1) * LANE] = s


def _score_sample(hf_s, page_table, cache_kidx_t):
    R = DEC_SEQ
    G = SCORE_ROWS
    h3 = hf_s.reshape(DEC_BATCH, R, D_HF)
    qi_hq = h3[:, :, F_QI:F_QI + IDX_HEADS * IDX_DIM].reshape(DEC_BATCH, R, IDX_HEADS, IDX_DIM)
    qi_hq = qi_hq.transpose(0, 2, 1, 3).reshape(DEC_BATCH, IDX_HEADS * R, IDX_DIM)
    w_hq = h3[:, :, F_KI + W_OFF:F_KI + W_OFF + IDX_HEADS].transpose(0, 2, 1)
    w_hq = w_hq.reshape(DEC_BATCH, IDX_HEADS * R, 1)
    in_specs = [
        pl.BlockSpec((G, IDX_HEADS * R, IDX_DIM), lambda b, pt: (b, 0, 0)),
        pl.BlockSpec((G, IDX_HEADS * R, 1), lambda b, pt: (b, 0, 0)),
        pl.BlockSpec((G, R, LANE), lambda b, pt: (b, 0, F_KI // LANE)),
    ]
    in_specs += [pl.BlockSpec((1, IDX_DIM, PAGE_SIZE),
                              lambda b, pt, r=r, p=p: (pt[b * G + r, p], 0, 0))
                 for r in range(G) for p in range(N_PAGES)]
    grid_spec = pltpu.PrefetchScalarGridSpec(
        num_scalar_prefetch=1,
        grid=(DEC_BATCH // G,),
        in_specs=in_specs,
        out_specs=pl.BlockSpec((G, R, L_SAMPLE), lambda b, pt: (b, 0, 0)),
        scratch_shapes=[pltpu.VMEM((G, PAGE_SIZE, IDX_DIM), BF16)],
    )
    return pl.pallas_call(
        _score_s_kernel,
        grid_spec=grid_spec,
        out_shape=jax.ShapeDtypeStruct((DEC_BATCH, R, L_SAMPLE), F32),
        compiler_params=_params(("arbitrary",)),
        name="score_sample",
    )(page_table, qi_hq, w_hq, h3, *([cache_kidx_t] * (G * N_PAGES)))


def _select_s_kernel(s_ref, tri_ref, o_ref):
    nt = N_PAGES + 1

    def tile_fn(j):
        return s_ref[j * LANE:(j + 1) * LANE, :].reshape(SUBTILES, SUBLANE, LANE)

    thr, need = _select_threshold(tile_fn, nt, static=True)
    parts = [slice(j * LANE, (j + 1) * LANE) for j in range(nt)]
    masks, _ = _selection_masks([s_ref[rows, :] for rows in parts], thr, need,
                                jnp.zeros((1, LANE), F32), tri_ref[...])
    for rows, mask in zip(parts, masks):
        o_ref[rows, :] = mask


def _select_sample(scores_t):
    n = scores_t.shape[1]
    return pl.pallas_call(
        _select_s_kernel,
        grid=(n // LANE,),
        in_specs=[pl.BlockSpec((L_SAMPLE, LANE), lambda c: (0, c)),
                  pl.BlockSpec((LANE, LANE), lambda c: (0, 0))],
        out_specs=pl.BlockSpec((L_SAMPLE, LANE), lambda c: (0, c)),
        out_shape=jax.ShapeDtypeStruct((L_SAMPLE, n), F32),
        compiler_params=_params(("parallel",)),
        name="select_sample",
    )(scores_t, jnp.tri(LANE, dtype=BF16))


def _attn_s_kernel(pt_ref, q_ref, kn_ref, vn_ref, mask_ref, bias_ref, *rest):
    npg = ATTN_ROWS * N_PAGES
    kp = rest[0:npg]
    vp = rest[npg:2 * npg]
    o_ref = rest[2 * npg]
    knp_ref, vnp_ref, logit_ref = rest[2 * npg + 1:]
    del pt_ref
    R = DEC_SEQ
    NT = N_PAGES + 1
    GR = GROUP * R

    knp_ref[...] = jnp.zeros(knp_ref.shape, BF16)
    vnp_ref[...] = jnp.zeros(vnp_ref.shape, BF16)
    for r in range(ATTN_ROWS):
        knp_ref[r, 0:2 * R] = jnp.concatenate([kn_ref[r], jnp.zeros((R, D_KV), F32)], 0).astype(BF16)
        vnp_ref[r, 0:2 * R] = jnp.concatenate([vn_ref[r], jnp.zeros((R, D_KV), F32)], 0).astype(BF16)

    def page_head(refs, pad_ref, r, t, g):
        if t < N_PAGES:
            return refs[r * N_PAGES + t][pl.ds(g, PAGE_SIZE, stride=N_KV_HEADS), :].astype(BF16)
        return pad_ref[r, :, g * HEAD_DIM:(g + 1) * HEAD_DIM]

    for r in range(ATTN_ROWS):
        q = (q_ref[r].astype(F32) * QK_SCALE).astype(BF16)
        for t in range(NT):
            mb = mask_ref[r, :, t * LANE:(t + 1) * LANE]
            for g in range(N_KV_HEADS):
                lg = lax.dot_general(q[g * GR:(g + 1) * GR], page_head(kp, knp_ref, r, t, g),
                                     (((1,), (1,)), ((), ())), preferred_element_type=F32)
                lg = lg.reshape(GROUP, R, LANE) + mb[None]
                if t >= N_PAGES - 1:
                    off = (t - (N_PAGES - 1)) * LANE
                    lg = lg + bias_ref[g * GROUP:(g + 1) * GROUP, :, off:off + LANE]
                logit_ref[r, g * GR:(g + 1) * GR, t * LANE:(t + 1) * LANE] = lg.reshape(GR, LANE)

    for r in range(ATTN_ROWS):
        logits = logit_ref[r]
        m = jnp.max(logits, axis=1, keepdims=True)
        p = jnp.exp(logits - m)
        inv = 1.0 / jnp.sum(p, axis=1, keepdims=True)
        pb = p.astype(BF16)
        outs = [jnp.zeros((GR, HEAD_DIM), F32) for _ in range(N_KV_HEADS)]
        for t in range(NT):
            for g in range(N_KV_HEADS):
                outs[g] = outs[g] + jnp.dot(pb[g * GR:(g + 1) * GR, t * LANE:(t + 1) * LANE],
                                            page_head(vp, vnp_ref, r, t, g),
                                            preferred_element_type=F32)
        for g in range(N_KV_HEADS):
            o = outs[g] * inv[g * GR:(g + 1) * GR]
            for hq in range(GROUP):
                h = g * GROUP + hq
                o_ref[r, :, h * HEAD_DIM:(h + 1) * HEAD_DIM] = o[hq * R:(hq + 1) * R]


def _attn_sample(hb_s, hf_s, page_table, cache_k, cache_v, mask, bias_s):
    R = DEC_SEQ
    G = ATTN_ROWS
    h3 = hf_s.reshape(DEC_BATCH, R, D_HF)
    q_hq = hb_s[:, C_Q:C_Q + D_ATTN].reshape(DEC_BATCH, R, N_HEADS, HEAD_DIM)
    q_hq = q_hq.transpose(0, 2, 1, 3).reshape(DEC_BATCH, N_HEADS * R, HEAD_DIM)
    rows_per_page = PAGE_SIZE * N_KV_HEADS
    ck = cache_k.reshape(-1, HEAD_DIM)
    cv = cache_v.reshape(-1, HEAD_DIM)

    in_specs = [
        pl.BlockSpec((G, N_HEADS * R, HEAD_DIM), lambda b, pt: (b, 0, 0)),
        pl.BlockSpec((G, R, D_KV), lambda b, pt: (b, 0, F_K // D_KV)),
        pl.BlockSpec((G, R, D_KV), lambda b, pt: (b, 0, F_V // D_KV)),
        pl.BlockSpec((G, R, L_SAMPLE), lambda b, pt: (b, 0, 0)),
        pl.BlockSpec((N_HEADS, R, 2 * LANE), lambda b, pt: (0, 0, 0)),
    ]
    pages = [pl.BlockSpec((rows_per_page, HEAD_DIM), lambda b, pt, r=r, p=p: (pt[b * G + r, p], 0))
             for r in range(G) for p in range(N_PAGES)]
    in_specs += pages + pages
    grid_spec = pltpu.PrefetchScalarGridSpec(
        num_scalar_prefetch=1,
        grid=(DEC_BATCH // G,),
        in_specs=in_specs,
        out_specs=pl.BlockSpec((G, R, D_ATTN), lambda b, pt: (b, 0, 0)),
        scratch_shapes=[
            pltpu.VMEM((G, PAGE_SIZE, D_KV), BF16),
            pltpu.VMEM((G, PAGE_SIZE, D_KV), BF16),
            pltpu.VMEM((G, N_HEADS * R, L_SAMPLE), F32),
        ],
    )
    return pl.pallas_call(
        _attn_s_kernel,
        grid_spec=grid_spec,
        out_shape=jax.ShapeDtypeStruct((DEC_BATCH, R, D_ATTN), F32),
        compiler_params=_params(("arbitrary",)),
        name="attn_sample",
    )(page_table, q_hq, h3, h3, mask, bias_s, *([ck] * (G * N_PAGES)), *([cv] * (G * N_PAGES)))


def _conv_p_kernel(val_ref, gate_ref, cw_ref, cb_ref, dw_ref, ut_ref, pad_ref):
    pad_ref[0:PAD_ROWS] = jnp.zeros((PAD_ROWS, LANE), F32)
    pad_ref[PAD_ROWS:] = val_ref[0].astype(F32) * jax.nn.sigmoid(gate_ref[0].astype(F32))
    ut_ref[0] = pad_ref[SEQ:SEQ + PAD_ROWS]
    cw = cw_ref[...]
    cb = cb_ref[...]
    first = PAD_ROWS - (CONV_WIDTH - 1)
    for c in range(SEQ // CONV_CHUNK):
        base = c * CONV_CHUNK
        acc = jnp.broadcast_to(cb, (CONV_CHUNK, LANE))
        for r in range(SUBLANE):
            taps = [w for w in range(CONV_WIDTH) if (first + w) % SUBLANE == r]
            span = max(first + w - r for w in taps) + CONV_CHUNK
            win = pad_ref[base + r:base + r + span]
            for w in taps:
                a = first + w - r
                acc = acc + win[a:a + CONV_CHUNK] * cw[w:w + 1]
        dw_ref[0, base:base + CONV_CHUNK] = acc


def _conv_prompt(hb_p, cw_pad, cb):
    h3 = hb_p.reshape(BATCH, SEQ, D_HB)
    nc = D_CONV // LANE
    return pl.pallas_call(
        _conv_p_kernel,
        grid=(BATCH, nc),
        in_specs=[pl.BlockSpec((1, SEQ, LANE), lambda b, c: (b, 0, C_GV // LANE + c)),
                  pl.BlockSpec((1, SEQ, LANE), lambda b, c: (b, 0, C_GG // LANE + c)),
                  pl.BlockSpec((PAD_ROWS, LANE), lambda b, c: (0, c)),
                  pl.BlockSpec((1, LANE), lambda b, c: (0, c))],
        out_specs=[pl.BlockSpec((1, SEQ, LANE), lambda b, c: (b, 0, c)),
                   pl.BlockSpec((1, PAD_ROWS, LANE), lambda b, c: (b, 0, c))],
        out_shape=[jax.ShapeDtypeStruct((BATCH, SEQ, D_CONV), F32),
                   jax.ShapeDtypeStruct((BATCH, PAD_ROWS, D_CONV), F32)],
        scratch_shapes=[pltpu.VMEM((PAD_ROWS + SEQ, LANE), F32)],
        compiler_params=_params(("parallel", "parallel")),
        name="conv_prompt",
    )(h3, h3, cw_pad, cb)


def _conv_s_kernel(val_ref, gate_ref, st_ref, cw_ref, cb_ref, dw_ref, ns_ref):
    R = DEC_SEQ
    H = CONV_WIDTH - 1
    cw = cw_ref[...]
    cb = jnp.broadcast_to(cb_ref[...], (DEC_BATCH, LANE))
    u = []
    for q in range(R):
        rows = pl.ds(q, DEC_BATCH, stride=R)
        u.append(val_ref[rows, :] * jax.nn.sigmoid(gate_ref[rows, :]))

    def row(r):
        return st_ref[r] if r < H else u[r - H]

    for q in range(R):
        acc = cb
        for w in range(CONV_WIDTH):
            acc = acc + row(q + w) * cw[w:w + 1]
        dw_ref[pl.ds(q, DEC_BATCH, stride=R), :] = acc
    for r in range(H):
        ns_ref[r] = row(r + R)


def _conv_sample(val, gate, state_t, cw_pad, cb):
    n_s = DEC_BATCH * DEC_SEQ
    H = CONV_WIDTH - 1
    nc = D_CONV // LANE
    return pl.pallas_call(
        _conv_s_kernel,
        grid=(nc,),
        in_specs=[pl.BlockSpec((n_s, LANE), lambda c: (0, c)),
                  pl.BlockSpec((n_s, LANE), lambda c: (0, c)),
                  pl.BlockSpec((H, DEC_BATCH, LANE), lambda c: (0, 0, c)),
                  pl.BlockSpec((PAD_ROWS, LANE), lambda c: (0, c)),
                  pl.BlockSpec((1, LANE), lambda c: (0, c))],
        out_specs=[pl.BlockSpec((n_s, LANE), lambda c: (0, c)),
                   pl.BlockSpec((H, DEC_BATCH, LANE), lambda c: (0, 0, c))],
        out_shape=[jax.ShapeDtypeStruct((n_s, D_CONV), F32),
                   jax.ShapeDtypeStruct((H, DEC_BATCH, D_CONV), F32)],
        compiler_params=_params(("parallel",)),
        name="conv_sample",
    )(val, gate, state_t, cw_pad, cb)


def _tail_kernel(attn_ref, za_ref, dw_ref, zc_ref, ga_ref, gc_ref, x_ref,
                 wua_ref, wpw_ref, wuc_ref, wo_ref, ng_ref, nb_ref, bpw_ref, fg_ref, y_ref):
    a = attn_ref[...] * jax.nn.silu(za_ref[...].astype(F32))
    branch_attn = jnp.dot(a.astype(BF16), wua_ref[...], preferred_element_type=F32)

    dw = dw_ref[...]
    mu = jnp.mean(dw, axis=-1, keepdims=True)
    var = jnp.mean(jnp.square(dw - mu), axis=-1, keepdims=True)
    ln = (dw - mu) * lax.rsqrt(var + EPS) * ng_ref[...] + nb_ref[...]
    conv_out = jnp.dot(jax.nn.silu(ln).astype(BF16), wpw_ref[...],
                       preferred_element_type=F32) + bpw_ref[...]
    c = conv_out * jax.nn.silu(zc_ref[...].astype(F32))
    branch_conv = jnp.dot(c.astype(BF16), wuc_ref[...], preferred_element_type=F32)

    merged = (jax.nn.sigmoid(ga_ref[...].astype(F32)) * branch_attn
              + jax.nn.sigmoid(gc_ref[...].astype(F32)) * branch_conv)
    y = x_ref[...] + jnp.dot(merged.astype(BF16), wo_ref[...], preferred_element_type=F32)
    ms = jnp.mean(y * y, axis=-1, keepdims=True)
    y_ref[...] = y * lax.rsqrt(ms + EPS) * fg_ref[...]


def _tail(h, attn, dw, x2d, weights, tm):
    n = x2d.shape[0]

    def const(shape):
        return pl.BlockSpec(shape, lambda i: (0, 0), pipeline_mode=pl.Buffered(1))

    def cols(width, offset):
        return pl.BlockSpec((tm, width), lambda i: (i, offset // width))

    return pl.pallas_call(
        _tail_kernel,
        grid=(n // tm,),
        in_specs=[cols(D_ATTN, 0), cols(D_ATTN, C_ZA), cols(D_CONV, 0), cols(D_CONV, C_ZC),
                  cols(D_MODEL, C_GA), cols(D_MODEL, C_GC), cols(D_MODEL, 0),
                  const((D_ATTN, D_MODEL)), const((D_CONV, D_CONV)),
                  const((D_CONV, D_MODEL)), const((D_MODEL, D_MODEL)),
                  const((1, D_CONV)), const((1, D_CONV)), const((1, D_CONV)),
                  const((1, D_MODEL))],
        out_specs=pl.BlockSpec((tm, D_MODEL), lambda i: (i, 0)),
        out_shape=jax.ShapeDtypeStruct((n, D_MODEL), F32),
        compiler_params=_params(("parallel",)),
        name="tail",
    )(attn, h, dw, h, h, h, x2d, *weights)


def _reorder_kernel(off_ref, w_ref, o_ref):
    del off_ref
    o_ref[...] = w_ref[...].astype(BF16)


def _reorder_w_in(w_in):
    w_t = w_in.T
    src = []
    for first, dest, width in (('gate_attn', C_GA, 2 * D_MODEL), ('q', C_Q, D_ATTN),
                               ('z_attn', C_ZA, D_ATTN), ('glu_val', C_GV, 3 * D_CONV),
                               ('k', C_K, 2 * D_KV), ('q_idx', C_QI, D_H - C_QI)):
        assert dest == len(src) * REORDER_ROWS
        for r in range(0, width, REORDER_ROWS):
            src.append(min(_SRC[first][0] + r, D_IN - REORDER_ROWS) // SUBLANE)
    grid_spec = pltpu.PrefetchScalarGridSpec(
        num_scalar_prefetch=1,
        grid=(len(src),),
        in_specs=[pl.BlockSpec((pl.Element(REORDER_ROWS), pl.Element(D_MODEL)),
                               lambda d, off: (off[d] * SUBLANE, 0))],
        out_specs=pl.BlockSpec((REORDER_ROWS, D_MODEL), lambda d, off: (d, 0)),
    )
    return pl.pallas_call(
        _reorder_kernel,
        grid_spec=grid_spec,
        out_shape=jax.ShapeDtypeStruct((D_H, D_MODEL), BF16),
        compiler_params=_params(("arbitrary",)),
        name="reorder_w_in",
    )(jnp.asarray(src, jnp.int32), w_t)


def kernel(x_prompt, x_sample, cache_k, cache_v, cache_kidx, state_conv, page_table,
           ln_g, w_in, conv_w, conv_b, conv_norm_g, conv_norm_b, w_pw, b_pw,
           w_up_attn, w_up_conv, w_out, rel_bias, final_g):
    w_all = _reorder_w_in(w_in[0])
    g_in = ln_g[0].reshape(1, D_MODEL)
    xp = x_prompt.reshape(BATCH * SEQ, D_MODEL)
    xs = x_sample.reshape(DEC_BATCH * DEC_SEQ, D_MODEL)
    hb_p, hf_p, k_p, v_p = _proj(xp, g_in, w_all)
    hb_s, hf_s, k_s, v_s = _proj(xs, g_in, w_all)

    bias_p, bias_s = _bias_tables(rel_bias)

    attn_p = _attn_prompt(hb_p, hf_p, bias_p).reshape(BATCH * SEQ, D_ATTN)

    n_s = DEC_BATCH * DEC_SEQ
    scores = _score_sample(hf_s, page_table, cache_kidx[0].transpose(0, 2, 1))
    mask_t = _select_sample(scores.reshape(n_s, L_SAMPLE).T)
    mask = mask_t.T.reshape(DEC_BATCH, DEC_SEQ, L_SAMPLE)
    attn_s = _attn_sample(hb_s, hf_s, page_table, cache_k[0], cache_v[0], mask, bias_s)
    attn_s = attn_s.reshape(n_s, D_ATTN)

    cw_pad = jnp.concatenate([conv_w[0], jnp.zeros((PAD_ROWS - CONV_WIDTH, D_CONV), F32)], 0)
    cb = conv_b[0].reshape(1, D_CONV)
    dw_p, u_tail = _conv_prompt(hb_p, cw_pad, cb)
    dw_s, state_new = _conv_sample(hb_s[:, C_GV:C_GV + D_CONV].astype(F32),
                                   hb_s[:, C_GG:C_GG + D_CONV].astype(F32),
                                   state_conv[0].transpose(1, 0, 2), cw_pad, cb)

    wua = w_up_attn[0].astype(BF16)
    wpw = w_pw[0].astype(BF16)
    wuc = w_up_conv[0].astype(BF16)
    wo = w_out[0].astype(BF16)
    ng = conv_norm_g[0].reshape(1, D_CONV)
    nb = conv_norm_b[0].reshape(1, D_CONV)
    bpw = b_pw[0].reshape(1, D_CONV)
    fg = final_g.reshape(1, D_MODEL)
    weights = (wua, wpw, wuc, wo, ng, nb, bpw, fg)
    y_p = _tail(hb_p, attn_p, dw_p.reshape(BATCH * SEQ, D_CONV), xp, weights, TAIL_TM)
    y_s = _tail(hb_s, attn_s, dw_s, xs, weights, TAIL_TM)

    tail_rows = CONV_WIDTH - 1
    return (
        y_p.reshape(BATCH, SEQ, D_MODEL),
        y_s.reshape(DEC_BATCH, DEC_SEQ, D_MODEL),
        k_p.reshape(1, BATCH, SEQ, N_KV_HEADS, HEAD_DIM),
        v_p.reshape(1, BATCH, SEQ, N_KV_HEADS, HEAD_DIM),
        hf_p[:, F_KI:F_KI + IDX_DIM].reshape(1, BATCH, SEQ, IDX_DIM),
        u_tail[:, PAD_ROWS - tail_rows:].reshape(1, BATCH, tail_rows, D_CONV),
        k_s.reshape(1, DEC_BATCH, DEC_SEQ, N_KV_HEADS, HEAD_DIM),
        v_s.reshape(1, DEC_BATCH, DEC_SEQ, N_KV_HEADS, HEAD_DIM),
        hf_s[:, F_KI:F_KI + IDX_DIM].reshape(1, DEC_BATCH, DEC_SEQ, IDX_DIM),
        state_new.transpose(1, 0, 2).reshape(1, DEC_BATCH, tail_rows, D_CONV),
    )
```

```python
import functools
import math

import numpy as np
import jax
import jax.numpy as jnp
from jax import lax
from jax.experimental import pallas as pl
from jax.experimental.pallas import tpu as pltpu

F32 = jnp.float32
BF16 = jnp.bfloat16

D_MODEL = 2048
BATCH = 8
SEQ = 2048
DEC_BATCH = 128
DEC_SEQ = 8
PAST_LEN = 2048
PAGE_SIZE = 128
N_PAGES = PAST_LEN // PAGE_SIZE
N_HEADS = 8
N_KV_HEADS = 2
HEAD_DIM = 128
GROUP = N_HEADS // N_KV_HEADS
D_ATTN = N_HEADS * HEAD_DIM
D_KV = N_KV_HEADS * HEAD_DIM
IDX_HEADS = 8
IDX_DIM = 64
TOPK = 256
Q_BLOCK = 128
N_BUCKETS = 32
MAX_DISTANCE = 128
D_CONV = D_MODEL // 2
CONV_WIDTH = 31
EPS = 1e-6
NEG = -1e30
NEG_INF = float("-inf")

LANE = 128
SUBLANE = 8
VMEM_LIMIT = 56 * 1024 * 1024

C_GA = 0
C_GC = C_GA + D_MODEL
C_Q = C_GC + D_MODEL
C_ZA = C_Q + D_ATTN
C_GV = C_ZA + D_ATTN
C_GG = C_GV + D_CONV
C_ZC = C_GG + D_CONV
C_K = C_ZC + D_CONV
C_V = C_K + D_KV
C_QI = C_V + D_KV
C_KI = C_QI + IDX_HEADS * IDX_DIM
W_OFF = IDX_DIM
MXU_WIDTH = 256
PROJ_TM = 1024
PROJ_TN = 6 * MXU_WIDTH
D_H = -(-(C_KI + LANE) // PROJ_TN) * PROJ_TN
KV_TILE = C_K // PROJ_TN
assert C_K % PROJ_TN == 0 and D_H == C_K + PROJ_TN
D_HB = C_K
D_HF = C_KI + LANE - C_K
F_K = C_K - D_HB
F_V = C_V - D_HB
F_QI = C_QI - D_HB
F_KI = C_KI - D_HB

_SRC = {}
_off = 0
for _name, _w in (('q', D_ATTN), ('k', D_KV), ('v', D_KV), ('z_attn', D_ATTN),
                  ('q_idx', IDX_HEADS * IDX_DIM), ('k_idx', IDX_DIM), ('w_idx', IDX_HEADS),
                  ('glu_val', D_CONV), ('glu_gate', D_CONV), ('z_conv', D_CONV),
                  ('gate_attn', D_MODEL), ('gate_conv', D_MODEL)):
    _SRC[_name] = (_off, _w)
    _off += _w
D_IN = _off
REORDER_ROWS = 512

N_BISECT = 20
PAD_ROWS = 32
CONV_CHUNK = 64
SCORE_SCALE = (IDX_DIM ** -0.5) * (IDX_HEADS ** -0.5)
QK_SCALE = HEAD_DIM ** -0.5
LOG2E = math.log2(math.e)
L_SAMPLE = (N_PAGES + 1) * LANE
SUBTILES = LANE // SUBLANE
CHUNK_BLOCKS = 2
CHUNK = CHUNK_BLOCKS * Q_BLOCK
PAD_KEYS = CHUNK - Q_BLOCK
VT_ROWS = HEAD_DIM + 2 * SUBLANE
SCORE_ROWS = 4
ATTN_ROWS = 4
TAIL_TM = 256


def _t5_bucket_static(dist):
    n = np.maximum(dist, 0)
    max_exact = N_BUCKETS // 2
    ratio = (np.log(np.maximum(n, 1).astype(np.float32) / np.float32(max_exact))
             / np.float32(math.log(MAX_DISTANCE / max_exact)))
    large = np.minimum(max_exact + (ratio * np.float32(N_BUCKETS - max_exact)).astype(np.int32),
                       N_BUCKETS - 1)
    return np.where(n < max_exact, n, large).astype(np.int32)


FAR_BUCKET = int(_t5_bucket_static(np.array([2 * MAX_DISTANCE]))[0])


def _params(sem):
    return pltpu.CompilerParams(dimension_semantics=sem, vmem_limit_bytes=VMEM_LIMIT)


def _proj_kernel(x_ref, g_ref, w_ref, ob_ref, of_ref, k_ref, v_ref, xn_ref):
    tm = x_ref.shape[0]
    j = pl.program_id(1)

    @pl.when(j == 0)
    def _():
        x = x_ref[...]
        ms = jnp.mean(x * x, axis=-1, keepdims=True)
        xn_ref[...] = (x * lax.rsqrt(ms + EPS) * g_ref[...]).astype(BF16)

    def project():
        return lax.dot_general(xn_ref[...], w_ref[...], (((1,), (1,)), ((), ())),
                               preferred_element_type=F32)

    @pl.when(j < KV_TILE)
    def _():
        ob_ref[...] = project().astype(BF16)

    @pl.when(j == KV_TILE)
    def _():
        of_ref[...] = project()[:, :D_HF]
        for g in range(N_KV_HEADS):
            rows = pl.ds(g, tm, stride=N_KV_HEADS)
            k_ref[rows, :] = of_ref[:, F_K + g * HEAD_DIM:F_K + (g + 1) * HEAD_DIM]
            v_ref[rows, :] = of_ref[:, F_V + g * HEAD_DIM:F_V + (g + 1) * HEAD_DIM]


def _proj(x2d, g, w):
    n = x2d.shape[0]
    tm, tn = PROJ_TM, PROJ_TN
    kv_spec = pl.BlockSpec((N_KV_HEADS * tm, HEAD_DIM), lambda i, j: (i, 0))
    kv_shape = jax.ShapeDtypeStruct((N_KV_HEADS * n, HEAD_DIM), F32)
    return pl.pallas_call(
        _proj_kernel,
        grid=(n // tm, D_H // tn),
        in_specs=[pl.BlockSpec((tm, D_MODEL), lambda i, j: (i, 0)),
                  pl.BlockSpec((1, D_MODEL), lambda i, j: (0, 0)),
                  pl.BlockSpec((tn, D_MODEL), lambda i, j: (j, 0))],
        out_specs=[pl.BlockSpec((tm, tn), lambda i, j: (i, jnp.minimum(j, KV_TILE - 1))),
                   pl.BlockSpec((tm, D_HF), lambda i, j: (i, 0)), kv_spec, kv_spec],
        out_shape=[jax.ShapeDtypeStruct((n, D_HB), BF16),
                   jax.ShapeDtypeStruct((n, D_HF), F32), kv_shape, kv_shape],
        scratch_shapes=[pltpu.VMEM((tm, D_MODEL), BF16)],
        compiler_params=_params(("parallel", "arbitrary")),
        name="proj",
    )(x2d, g, w)


def _bias_kernel(rb_ref, bp_ref, bs_ref, op_ref, os_ref):
    bp = bp_ref[...]
    bs = bs_ref[...]
    for h in range(N_HEADS):
        far = rb_ref[FAR_BUCKET, h]
        tp = jnp.zeros(bp.shape, F32)
        ts = jnp.zeros(bs.shape, F32)
        for b in range(N_BUCKETS):
            val = rb_ref[b, h] - far
            tp = jnp.where(bp == b, val * LOG2E, tp)
            ts = jnp.where(bs == b, val, ts)
        op_ref[h] = tp
        os_ref[h] = ts


def _bias_tables(rel_bias):
    key = np.arange(CHUNK)[:, None]
    qry = np.arange(Q_BLOCK)[None, :]
    bucket_p = _t5_bucket_static(CHUNK - Q_BLOCK + qry - key)
    qi = np.arange(DEC_SEQ)[:, None]
    col = np.arange(2 * LANE)[None, :]
    dist_s = np.where(col < LANE, LANE + qi - col, qi - (col - LANE))
    bucket_s = _t5_bucket_static(dist_s)
    return pl.pallas_call(
        _bias_kernel,
        in_specs=[pl.BlockSpec(memory_space=pltpu.SMEM),
                  pl.BlockSpec(memory_space=pltpu.VMEM),
                  pl.BlockSpec(memory_space=pltpu.VMEM)],
        out_specs=[pl.BlockSpec(memory_space=pltpu.VMEM),
                   pl.BlockSpec(memory_space=pltpu.VMEM)],
        out_shape=[jax.ShapeDtypeStruct((N_HEADS, CHUNK, Q_BLOCK), F32),
                   jax.ShapeDtypeStruct((N_HEADS, DEC_SEQ, 2 * LANE), F32)],
        name="bias_tables",
    )(rel_bias, jnp.asarray(bucket_p), jnp.asarray(bucket_s))


def _any(x):
    return jnp.max(jnp.where(x, 1.0, 0.0)) > 0.5


def _rep(x):
    return jnp.broadcast_to(x, (SUBLANE, LANE))


def _fold_rows(x, comb):
    parts = [x[k:k + SUBLANE] for k in range(0, x.shape[0], SUBLANE)]
    while len(parts) > 1:
        parts = [comb(parts[k], parts[k + 1]) for k in range(0, len(parts), 2)]
    return parts[0]


def _select_threshold(tile_fn, ntiles):
    def reduce_tiles(fn, init, comb):
        acc = init
        for j in range(ntiles):
            x = fn(tile_fn(j), j)
            parts = [x[k] for k in range(x.shape[0])]
            while len(parts) > 1:
                parts = [comb(parts[k], parts[k + 1]) for k in range(0, len(parts), 2)]
            acc = comb(acc, parts[0])
        return acc

    zeros = jnp.zeros((SUBLANE, LANE), F32)

    def count(pred_fn):
        acc = reduce_tiles(lambda s, j: jnp.where(pred_fn(s, j), 1.0, 0.0), zeros,
                           lambda a, b: a + b)
        return _rep(jnp.sum(acc, axis=0, keepdims=True))

    def masked_max(pred_fn):
        acc = reduce_tiles(lambda s, j: jnp.where(pred_fn(s, j), s, NEG_INF),
                           jnp.full((SUBLANE, LANE), NEG_INF, F32), jnp.maximum)
        return _rep(jnp.max(acc, axis=0, keepdims=True))

    bound = reduce_tiles(lambda s, j: jnp.where(s > NEG_INF, jnp.abs(s), 0.0), zeros,
                         jnp.maximum)
    bound = _rep(jnp.max(bound, axis=0, keepdims=True))

    def bisect(_, carry):
        lo, hi = carry
        mid = 0.5 * lo + 0.5 * hi
        few = count(lambda s, j: s > mid[None]) < TOPK
        return jnp.where(few, lo, mid), jnp.where(few, mid, hi)

    _, hi = lax.fori_loop(0, N_BISECT, bisect, (-bound, bound))

    thr = masked_max(lambda s, j: s <= hi[None])
    n_ge = count(lambda s, j: s >= thr[None])

    def fix_body(carry):
        thr, n_ge, _ = carry
        lower = masked_max(lambda s, j: s < thr[None])
        thr = jnp.where(n_ge < TOPK, lower, thr)
        n_ge = count(lambda s, j: s >= thr[None])
        return thr, n_ge, _any(n_ge < TOPK)

    thr, n_ge, _ = lax.while_loop(lambda c: c[2], fix_body, (thr, n_ge, _any(n_ge < TOPK)))
    n_gt = count(lambda s, j: s > thr[None])
    need = TOPK - n_gt
    return thr[0:1], need[0:1]


def _keep(pred):
    return jnp.where(pred, 0.0, NEG)


def _selection_masks(tiles, thr, need, seen, tri):
    ties = [s == thr for s in tiles]
    ranks = []
    for k in range(0, len(tiles), 2):
        pair = jnp.concatenate([jnp.where(t, 1.0, 0.0).astype(BF16) for t in ties[k:k + 2]], axis=1)
        rank = jnp.dot(tri, pair, preferred_element_type=F32)
        ranks += [rank[:, p * LANE:(p + 1) * LANE] for p in range(len(ties[k:k + 2]))]
    masks = []
    for s, tie, rank in zip(tiles, ties, ranks):
        masks.append(jnp.where(tie, _keep(rank + seen <= need), _keep(s > thr)))
        seen = seen + rank[s.shape[0] - 1:]
    return masks, seen


def _attn_p_kernel(q_ref, qi_ref, wi_ref, k_ref, v_ref, ki_ref, bias_ref, tri_ref, o_ref,
                   kb_ref, vt_ref, kib_ref, qh_ref, qih_ref,
                   score_ref, mask_ref, s_ref):
    i = pl.program_id(1)
    T = Q_BLOCK
    W = GROUP * T
    nch = i // CHUNK_BLOCKS + 1

    @pl.when(i == 0)
    def _():
        kb_ref[0:PAD_KEYS] = jnp.zeros((PAD_KEYS, D_KV), BF16)
        kb_ref[PAD_KEYS:] = k_ref[0].astype(BF16)
        kib_ref[0:PAD_KEYS] = jnp.zeros((PAD_KEYS, IDX_DIM), BF16)
        kib_ref[PAD_KEYS:] = ki_ref[0][:, :IDX_DIM].astype(BF16)
        ones_row = lax.broadcasted_iota(jnp.int32, (VT_ROWS - HEAD_DIM, PAD_KEYS + SEQ), 0) == 0
        for g in range(N_KV_HEADS):
            vt_ref[g, 0:HEAD_DIM, 0:PAD_KEYS] = jnp.zeros((HEAD_DIM, PAD_KEYS), BF16)
            vt_ref[g, HEAD_DIM:VT_ROWS, :] = jnp.where(ones_row, 1.0, 0.0).astype(BF16)
            for c in range(SEQ // LANE):
                blk = v_ref[0, c * LANE:(c + 1) * LANE, g * HEAD_DIM:(g + 1) * HEAD_DIM]
                vt_ref[g, 0:HEAD_DIM,
                       PAD_KEYS + c * LANE:PAD_KEYS + (c + 1) * LANE] = blk.T.astype(BF16)
        score_ref[0:PAD_KEYS] = jnp.full((PAD_KEYS, T), NEG_INF, F32)
        mask_ref[0:PAD_KEYS] = jnp.full((PAD_KEYS, T), NEG, F32)

    q = q_ref[0].astype(F32) * (QK_SCALE * LOG2E)
    for h in range(N_HEADS):
        qh_ref[h] = q[:, h * HEAD_DIM:(h + 1) * HEAD_DIM].astype(BF16)
    qi = qi_ref[0]
    for h in range(IDX_HEADS):
        qih_ref[h] = qi[:, h * IDX_DIM:(h + 1) * IDX_DIM].astype(BF16)
    w_rows = wi_ref[0].T[W_OFF:W_OFF + IDX_HEADS] * SCORE_SCALE

    def span(c):
        return pl.ds(pl.multiple_of((i - CHUNK_BLOCKS * c) * LANE, LANE), CHUNK)

    def first_key(c):
        return (i - CHUNK_BLOCKS * c) * LANE - PAD_KEYS

    key_l = lax.broadcasted_iota(jnp.int32, (CHUNK, T), 0)
    qry = i * T + lax.broadcasted_iota(jnp.int32, (CHUNK, T), 1)

    def score_chunk(c):
        kc = kib_ref[span(c), :]
        d = lax.dot_general(kc, qih_ref[...].reshape(IDX_HEADS * T, IDX_DIM),
                            (((1,), (1,)), ((), ())), preferred_element_type=F32)
        acc = jnp.zeros((CHUNK, T), F32)
        for h in range(IDX_HEADS):
            acc = acc + jnp.maximum(d[:, h * T:(h + 1) * T], 0.0) * w_rows[h:h + 1]
        key = first_key(c) + key_l
        acc = jnp.where(key <= qry, jnp.where(key >= 0, acc, NEG_INF), NEG_INF)
        score_ref[span(c), :] = acc

    nsub = CHUNK // SUBLANE

    def tile_fn(c):
        return score_ref[span(c), :].reshape(nsub, SUBLANE, T)

    def select_all():
        mask_ref[span(0), :] = _keep(score_ref[span(0), :] > NEG_INF)

    def select(n):
        thr, need = _select_threshold(tile_fn, n)
        parts = []
        for c in reversed(range(n)):
            start = pl.multiple_of((i - CHUNK_BLOCKS * c) * LANE, LANE)
            parts += [pl.ds(start + p * MXU_WIDTH, MXU_WIDTH) for p in range(CHUNK // MXU_WIDTH)]
        masks, _ = _selection_masks([score_ref[rows, :] for rows in parts], thr, need,
                                    jnp.zeros((1, T), F32), tri_ref[...])
        for rows, mask in zip(parts, masks):
            mask_ref[rows, :] = mask

    def logits(c, g):
        kc = kb_ref[span(c), g * HEAD_DIM:(g + 1) * HEAD_DIM]
        qg = qh_ref[g * GROUP:(g + 1) * GROUP].reshape(W, HEAD_DIM)
        s = lax.dot_general(kc, qg, (((1,), (1,)), ((), ())), preferred_element_type=F32)
        mb = mask_ref[span(c), :]
        if c == 0:
            add = jnp.concatenate([mb + bias_ref[g * GROUP + hq] for hq in range(GROUP)], axis=1)
        else:
            add = jnp.concatenate([mb] * GROUP, axis=1)
        s = s + add
        s_ref[span(c), g * W:(g + 1) * W] = s
        return jnp.max(_fold_rows(s, jnp.maximum), axis=0, keepdims=True)

    def run(n):
        for c in range(n):
            score_chunk(c)
        if n == 1:
            pl.when(i * T + T <= TOPK)(select_all)
            pl.when(i * T + T > TOPK)(functools.partial(select, n))
        else:
            select(n)
        for g in range(N_KV_HEADS):
            m = functools.reduce(jnp.maximum, [logits(c, g) for c in range(n)])
            acc = None
            for c in range(n):
                p = jnp.exp2((s_ref[span(c), g * W:(g + 1) * W] - m).astype(BF16))
                pv = jnp.dot(vt_ref[g, :, span(c)], p, preferred_element_type=F32)
                acc = pv if acc is None else acc + pv
            o = acc[0:HEAD_DIM] / acc[HEAD_DIM:HEAD_DIM + 1]
            for hq in range(GROUP):
                h = g * GROUP + hq
                o_ref[0, :, h * HEAD_DIM:(h + 1) * HEAD_DIM] = o[:, hq * T:(hq + 1) * T].T

    for n in range(1, SEQ // CHUNK + 1):
        pl.when(nch == n)(functools.partial(run, n))


def _attn_prompt(hb_p, hf_p, bias_p):
    hb3 = hb_p.reshape(BATCH, SEQ, D_HB)
    hf3 = hf_p.reshape(BATCH, SEQ, D_HF)
    nqb = SEQ // Q_BLOCK
    T = Q_BLOCK
    qi_w = IDX_HEADS * IDX_DIM
    return pl.pallas_call(
        _attn_p_kernel,
        grid=(BATCH, nqb),
        in_specs=[
            pl.BlockSpec((1, T, D_ATTN), lambda b, i: (b, i, C_Q // D_ATTN)),
            pl.BlockSpec((1, T, qi_w), lambda b, i: (b, i, F_QI // qi_w)),
            pl.BlockSpec((1, T, LANE), lambda b, i: (b, i, F_KI // LANE)),
            pl.BlockSpec((1, SEQ, D_KV), lambda b, i: (b, 0, F_K // D_KV)),
            pl.BlockSpec((1, SEQ, D_KV), lambda b, i: (b, 0, F_V // D_KV)),
            pl.BlockSpec((1, SEQ, LANE), lambda b, i: (b, 0, F_KI // LANE)),
            pl.BlockSpec((N_HEADS, CHUNK, T), lambda b, i: (0, 0, 0)),
            pl.BlockSpec((MXU_WIDTH, MXU_WIDTH), lambda b, i: (0, 0)),
        ],
        out_specs=pl.BlockSpec((1, T, D_ATTN), lambda b, i: (b, i, 0)),
        out_shape=jax.ShapeDtypeStruct((BATCH, SEQ, D_ATTN), F32),
        scratch_shapes=[
            pltpu.VMEM((PAD_KEYS + SEQ, D_KV), BF16),
            pltpu.VMEM((N_KV_HEADS, VT_ROWS, PAD_KEYS + SEQ), BF16),
            pltpu.VMEM((PAD_KEYS + SEQ, IDX_DIM), BF16),
            pltpu.VMEM((N_HEADS, T, HEAD_DIM), BF16),
            pltpu.VMEM((IDX_HEADS, T, IDX_DIM), BF16),
            pltpu.VMEM((PAD_KEYS + SEQ, T), F32),
            pltpu.VMEM((PAD_KEYS + SEQ, T), F32),
            pltpu.VMEM((PAD_KEYS + SEQ, N_HEADS * T), F32),
        ],
        compiler_params=_params(("parallel", "arbitrary")),
        name="attn_prompt",
    )(hb3, hf3, hf3, hf3, hf3, hf3, bias_p, jnp.tri(MXU_WIDTH, dtype=BF16))


def _score_s_kernel(pt_ref, qi_ref, w_ref, kin_ref, *rest):
    npg = SCORE_ROWS * N_PAGES
    kip = rest[0:npg]
    o_ref = rest[npg]
    kinp_ref = rest[npg + 1]
    del pt_ref
    R = DEC_SEQ
    qrow = lax.broadcasted_iota(jnp.int32, (R, LANE), 0)
    lane = lax.broadcasted_iota(jnp.int32, (R, LANE), 1)
    kinp_ref[...] = jnp.zeros(kinp_ref.shape, BF16)
    for r in range(SCORE_ROWS):
        kinp_ref[r, 0:2 * R] = jnp.concatenate(
            [kin_ref[r][:, :IDX_DIM], jnp.zeros((R, IDX_DIM), F32)], 0).astype(BF16)
    for r in range(SCORE_ROWS):
        qi = qi_ref[r].astype(BF16)
        wb = jnp.broadcast_to(w_ref[r] * SCORE_SCALE, (IDX_HEADS * R, LANE))
        for t in range(N_PAGES + 1):
            if t < N_PAGES:
                d = jnp.dot(qi, kip[r * N_PAGES + t][0].astype(BF16), preferred_element_type=F32)
            else:
                d = lax.dot_general(qi, kinp_ref[r], (((1,), (1,)), ((), ())),
                                    preferred_element_type=F32)
            e = (jnp.maximum(d, 0.0) * wb).reshape(IDX_HEADS, R, LANE)
            s = e[0]
            for h in range(1, IDX_HEADS):
                s = s + e[h]
            if t == N_PAGES:
                s = jnp.where(lane <= qrow, s, NEG_INF)
            o_ref[r, :, t * LANE:(t + 1) * LANE] = s


def _score_sample(hf_s, page_table, cache_kidx_t):
    R = DEC_SEQ
    G = SCORE_ROWS
    h3 = hf_s.reshape(DEC_BATCH, R, D_HF)
    qi_hq = h3[:, :, F_QI:F_QI + IDX_HEADS * IDX_DIM].reshape(DEC_BATCH, R, IDX_HEADS, IDX_DIM)
    qi_hq = qi_hq.transpose(0, 2, 1, 3).reshape(DEC_BATCH, IDX_HEADS * R, IDX_DIM)
    w_hq = h3[:, :, F_KI + W_OFF:F_KI + W_OFF + IDX_HEADS].transpose(0, 2, 1)
    w_hq = w_hq.reshape(DEC_BATCH, IDX_HEADS * R, 1)
    in_specs = [
        pl.BlockSpec((G, IDX_HEADS * R, IDX_DIM), lambda b, pt: (b, 0, 0)),
        pl.BlockSpec((G, IDX_HEADS * R, 1), lambda b, pt: (b, 0, 0)),
        pl.BlockSpec((G, R, LANE), lambda b, pt: (b, 0, F_KI // LANE)),
    ]
    in_specs += [pl.BlockSpec((1, IDX_DIM, PAGE_SIZE),
                              lambda b, pt, r=r, p=p: (pt[b * G + r, p], 0, 0))
                 for r in range(G) for p in range(N_PAGES)]
    grid_spec = pltpu.PrefetchScalarGridSpec(
        num_scalar_prefetch=1,
        grid=(DEC_BATCH // G,),
        in_specs=in_specs,
        out_specs=pl.BlockSpec((G, R, L_SAMPLE), lambda b, pt: (b, 0, 0)),
        scratch_shapes=[pltpu.VMEM((G, PAGE_SIZE, IDX_DIM), BF16)],
    )
    return pl.pallas_call(
        _score_s_kernel,
        grid_spec=grid_spec,
        out_shape=jax.ShapeDtypeStruct((DEC_BATCH, R, L_SAMPLE), F32),
        compiler_params=_params(("arbitrary",)),
        name="score_sample",
    )(page_table, qi_hq, w_hq, h3, *([cache_kidx_t] * (G * N_PAGES)))


def _select_s_kernel(s_ref, tri_ref, o_ref):
    nt = N_PAGES + 1

    def tile_fn(j):
        return s_ref[j * LANE:(j + 1) * LANE, :].reshape(SUBTILES, SUBLANE, LANE)

    thr, need = _select_threshold(tile_fn, nt)
    parts = [slice(j * LANE, (j + 1) * LANE) for j in range(nt)]
    masks, _ = _selection_masks([s_ref[rows, :] for rows in parts], thr, need,
                                jnp.zeros((1, LANE), F32), tri_ref[...])
    for rows, mask in zip(parts, masks):
        o_ref[rows, :] = mask


def _select_sample(scores_t):
    n = scores_t.shape[1]
    return pl.pallas_call(
        _select_s_kernel,
        grid=(n // LANE,),
        in_specs=[pl.BlockSpec((L_SAMPLE, LANE), lambda c: (0, c)),
                  pl.BlockSpec((LANE, LANE), lambda c: (0, 0))],
        out_specs=pl.BlockSpec((L_SAMPLE, LANE), lambda c: (0, c)),
        out_shape=jax.ShapeDtypeStruct((L_SAMPLE, n), F32),
        compiler_params=_params(("parallel",)),
        name="select_sample",
    )(scores_t, jnp.tri(LANE, dtype=BF16))


def _attn_s_kernel(pt_ref, q_ref, kn_ref, vn_ref, mask_ref, bias_ref, *rest):
    npg = ATTN_ROWS * N_PAGES
    kp = rest[0:npg]
    vp = rest[npg:2 * npg]
    o_ref = rest[2 * npg]
    knp_ref, vnp_ref, logit_ref = rest[2 * npg + 1:]
    del pt_ref
    R = DEC_SEQ
    NT = N_PAGES + 1
    GR = GROUP * R

    knp_ref[...] = jnp.zeros(knp_ref.shape, BF16)
    vnp_ref[...] = jnp.zeros(vnp_ref.shape, BF16)
    for r in range(ATTN_ROWS):
        knp_ref[r, 0:2 * R] = jnp.concatenate([kn_ref[r], jnp.zeros((R, D_KV), F32)], 0).astype(BF16)
        vnp_ref[r, 0:2 * R] = jnp.concatenate([vn_ref[r], jnp.zeros((R, D_KV), F32)], 0).astype(BF16)

    def page_head(refs, pad_ref, r, t, g):
        if t < N_PAGES:
            return refs[r * N_PAGES + t][pl.ds(g, PAGE_SIZE, stride=N_KV_HEADS), :].astype(BF16)
        return pad_ref[r, :, g * HEAD_DIM:(g + 1) * HEAD_DIM]

    for r in range(ATTN_ROWS):
        q = (q_ref[r].astype(F32) * QK_SCALE).astype(BF16)
        for t in range(NT):
            mb = mask_ref[r, :, t * LANE:(t + 1) * LANE]
            for g in range(N_KV_HEADS):
                lg = lax.dot_general(q[g * GR:(g + 1) * GR], page_head(kp, knp_ref, r, t, g),
                                     (((1,), (1,)), ((), ())), preferred_element_type=F32)
                lg = lg.reshape(GROUP, R, LANE) + mb[None]
                if t >= N_PAGES - 1:
                    off = (t - (N_PAGES - 1)) * LANE
                    lg = lg + bias_ref[g * GROUP:(g + 1) * GROUP, :, off:off + LANE]
                logit_ref[r, g * GR:(g + 1) * GR, t * LANE:(t + 1) * LANE] = lg.reshape(GR, LANE)

    for r in range(ATTN_ROWS):
        logits = logit_ref[r]
        m = jnp.max(logits, axis=1, keepdims=True)
        p = jnp.exp(logits - m)
        inv = 1.0 / jnp.sum(p, axis=1, keepdims=True)
        pb = p.astype(BF16)
        outs = [jnp.zeros((GR, HEAD_DIM), F32) for _ in range(N_KV_HEADS)]
        for t in range(NT):
            for g in range(N_KV_HEADS):
                outs[g] = outs[g] + jnp.dot(pb[g * GR:(g + 1) * GR, t * LANE:(t + 1) * LANE],
                                            page_head(vp, vnp_ref, r, t, g),
                                            preferred_element_type=F32)
        for g in range(N_KV_HEADS):
            o = outs[g] * inv[g * GR:(g + 1) * GR]
            for hq in range(GROUP):
                h = g * GROUP + hq
                o_ref[r, :, h * HEAD_DIM:(h + 1) * HEAD_DIM] = o[hq * R:(hq + 1) * R]


def _attn_sample(hb_s, hf_s, page_table, cache_k, cache_v, mask, bias_s):
    R = DEC_SEQ
    G = ATTN_ROWS
    h3 = hf_s.reshape(DEC_BATCH, R, D_HF)
    q_hq = hb_s[:, C_Q:C_Q + D_ATTN].reshape(DEC_BATCH, R, N_HEADS, HEAD_DIM)
    q_hq = q_hq.transpose(0, 2, 1, 3).reshape(DEC_BATCH, N_HEADS * R, HEAD_DIM)
    rows_per_page = PAGE_SIZE * N_KV_HEADS
    ck = cache_k.reshape(-1, HEAD_DIM)
    cv = cache_v.reshape(-1, HEAD_DIM)

    in_specs = [
        pl.BlockSpec((G, N_HEADS * R, HEAD_DIM), lambda b, pt: (b, 0, 0)),
        pl.BlockSpec((G, R, D_KV), lambda b, pt: (b, 0, F_K // D_KV)),
        pl.BlockSpec((G, R, D_KV), lambda b, pt: (b, 0, F_V // D_KV)),
        pl.BlockSpec((G, R, L_SAMPLE), lambda b, pt: (b, 0, 0)),
        pl.BlockSpec((N_HEADS, R, 2 * LANE), lambda b, pt: (0, 0, 0)),
    ]
    pages = [pl.BlockSpec((rows_per_page, HEAD_DIM), lambda b, pt, r=r, p=p: (pt[b * G + r, p], 0))
             for r in range(G) for p in range(N_PAGES)]
    in_specs += pages + pages
    grid_spec = pltpu.PrefetchScalarGridSpec(
        num_scalar_prefetch=1,
        grid=(DEC_BATCH // G,),
        in_specs=in_specs,
        out_specs=pl.BlockSpec((G, R, D_ATTN), lambda b, pt: (b, 0, 0)),
        scratch_shapes=[
            pltpu.VMEM((G, PAGE_SIZE, D_KV), BF16),
            pltpu.VMEM((G, PAGE_SIZE, D_KV), BF16),
            pltpu.VMEM((G, N_HEADS * R, L_SAMPLE), F32),
        ],
    )
    return pl.pallas_call(
        _attn_s_kernel,
        grid_spec=grid_spec,
        out_shape=jax.ShapeDtypeStruct((DEC_BATCH, R, D_ATTN), F32),
        compiler_params=_params(("arbitrary",)),
        name="attn_sample",
    )(page_table, q_hq, h3, h3, mask, bias_s, *([ck] * (G * N_PAGES)), *([cv] * (G * N_PAGES)))


def _conv_p_kernel(val_ref, gate_ref, cw_ref, cb_ref, dw_ref, ut_ref, pad_ref):
    pad_ref[0:PAD_ROWS] = jnp.zeros((PAD_ROWS, LANE), F32)
    pad_ref[PAD_ROWS:] = val_ref[0].astype(F32) * jax.nn.sigmoid(gate_ref[0].astype(F32))
    ut_ref[0] = pad_ref[SEQ:SEQ + PAD_ROWS]
    cw = cw_ref[...]
    cb = cb_ref[...]
    first = PAD_ROWS - (CONV_WIDTH - 1)
    for c in range(SEQ // CONV_CHUNK):
        base = c * CONV_CHUNK
        acc = jnp.broadcast_to(cb, (CONV_CHUNK, LANE))
        for r in range(SUBLANE):
            taps = [w for w in range(CONV_WIDTH) if (first + w) % SUBLANE == r]
            span = max(first + w - r for w in taps) + CONV_CHUNK
            win = pad_ref[base + r:base + r + span]
            for w in taps:
                a = first + w - r
                acc = acc + win[a:a + CONV_CHUNK] * cw[w:w + 1]
        dw_ref[0, base:base + CONV_CHUNK] = acc


def _conv_prompt(hb_p, cw_pad, cb):
    h3 = hb_p.reshape(BATCH, SEQ, D_HB)
    nc = D_CONV // LANE
    return pl.pallas_call(
        _conv_p_kernel,
        grid=(BATCH, nc),
        in_specs=[pl.BlockSpec((1, SEQ, LANE), lambda b, c: (b, 0, C_GV // LANE + c)),
                  pl.BlockSpec((1, SEQ, LANE), lambda b, c: (b, 0, C_GG // LANE + c)),
                  pl.BlockSpec((PAD_ROWS, LANE), lambda b, c: (0, c)),
                  pl.BlockSpec((1, LANE), lambda b, c: (0, c))],
        out_specs=[pl.BlockSpec((1, SEQ, LANE), lambda b, c: (b, 0, c)),
                   pl.BlockSpec((1, PAD_ROWS, LANE), lambda b, c: (b, 0, c))],
        out_shape=[jax.ShapeDtypeStruct((BATCH, SEQ, D_CONV), F32),
                   jax.ShapeDtypeStruct((BATCH, PAD_ROWS, D_CONV), F32)],
        scratch_shapes=[pltpu.VMEM((PAD_ROWS + SEQ, LANE), F32)],
        compiler_params=_params(("parallel", "parallel")),
        name="conv_prompt",
    )(h3, h3, cw_pad, cb)


def _conv_s_kernel(val_ref, gate_ref, st_ref, cw_ref, cb_ref, dw_ref, ns_ref):
    R = DEC_SEQ
    H = CONV_WIDTH - 1
    cw = cw_ref[...]
    cb = jnp.broadcast_to(cb_ref[...], (DEC_BATCH, LANE))
    u = []
    for q in range(R):
        rows = pl.ds(q, DEC_BATCH, stride=R)
        u.append(val_ref[rows, :] * jax.nn.sigmoid(gate_ref[rows, :]))

    def row(r):
        return st_ref[r] if r < H else u[r - H]

    for q in range(R):
        acc = cb
        for w in range(CONV_WIDTH):
            acc = acc + row(q + w) * cw[w:w + 1]
        dw_ref[pl.ds(q, DEC_BATCH, stride=R), :] = acc
    for r in range(H):
        ns_ref[r] = row(r + R)


def _conv_sample(val, gate, state_t, cw_pad, cb):
    n_s = DEC_BATCH * DEC_SEQ
    H = CONV_WIDTH - 1
    nc = D_CONV // LANE
    return pl.pallas_call(
        _conv_s_kernel,
        grid=(nc,),
        in_specs=[pl.BlockSpec((n_s, LANE), lambda c: (0, c)),
                  pl.BlockSpec((n_s, LANE), lambda c: (0, c)),
                  pl.BlockSpec((H, DEC_BATCH, LANE), lambda c: (0, 0, c)),
                  pl.BlockSpec((PAD_ROWS, LANE), lambda c: (0, c)),
                  pl.BlockSpec((1, LANE), lambda c: (0, c))],
        out_specs=[pl.BlockSpec((n_s, LANE), lambda c: (0, c)),
                   pl.BlockSpec((H, DEC_BATCH, LANE), lambda c: (0, 0, c))],
        out_shape=[jax.ShapeDtypeStruct((n_s, D_CONV), F32),
                   jax.ShapeDtypeStruct((H, DEC_BATCH, D_CONV), F32)],
        compiler_params=_params(("parallel",)),
        name="conv_sample",
    )(val, gate, state_t, cw_pad, cb)


def _tail_kernel(attn_ref, za_ref, dw_ref, zc_ref, ga_ref, gc_ref, x_ref,
                 wua_ref, wpw_ref, wuc_ref, wo_ref, ng_ref, nb_ref, bpw_ref, fg_ref, y_ref):
    a = attn_ref[...] * jax.nn.silu(za_ref[...].astype(F32))
    branch_attn = jnp.dot(a.astype(BF16), wua_ref[...], preferred_element_type=F32)

    dw = dw_ref[...]
    mu = jnp.mean(dw, axis=-1, keepdims=True)
    var = jnp.mean(jnp.square(dw - mu), axis=-1, keepdims=True)
    ln = (dw - mu) * lax.rsqrt(var + EPS) * ng_ref[...] + nb_ref[...]
    conv_out = jnp.dot(jax.nn.silu(ln).astype(BF16), wpw_ref[...],
                       preferred_element_type=F32) + bpw_ref[...]
    c = conv_out * jax.nn.silu(zc_ref[...].astype(F32))
    branch_conv = jnp.dot(c.astype(BF16), wuc_ref[...], preferred_element_type=F32)

    merged = (jax.nn.sigmoid(ga_ref[...].astype(F32)) * branch_attn
              + jax.nn.sigmoid(gc_ref[...].astype(F32)) * branch_conv)
    y = x_ref[...] + jnp.dot(merged.astype(BF16), wo_ref[...], preferred_element_type=F32)
    ms = jnp.mean(y * y, axis=-1, keepdims=True)
    y_ref[...] = y * lax.rsqrt(ms + EPS) * fg_ref[...]


def _tail(h, attn, dw, x2d, weights, tm):
    n = x2d.shape[0]

    def const(shape):
        return pl.BlockSpec(shape, lambda i: (0, 0), pipeline_mode=pl.Buffered(1))

    def cols(width, offset):
        return pl.BlockSpec((tm, width), lambda i: (i, offset // width))

    return pl.pallas_call(
        _tail_kernel,
        grid=(n // tm,),
        in_specs=[cols(D_ATTN, 0), cols(D_ATTN, C_ZA), cols(D_CONV, 0), cols(D_CONV, C_ZC),
                  cols(D_MODEL, C_GA), cols(D_MODEL, C_GC), cols(D_MODEL, 0),
                  const((D_ATTN, D_MODEL)), const((D_CONV, D_CONV)),
                  const((D_CONV, D_MODEL)), const((D_MODEL, D_MODEL)),
                  const((1, D_CONV)), const((1, D_CONV)), const((1, D_CONV)),
                  const((1, D_MODEL))],
        out_specs=pl.BlockSpec((tm, D_MODEL), lambda i: (i, 0)),
        out_shape=jax.ShapeDtypeStruct((n, D_MODEL), F32),
        compiler_params=_params(("parallel",)),
        name="tail",
    )(attn, h, dw, h, h, h, x2d, *weights)


def _reorder_kernel(off_ref, w_ref, o_ref):
    del off_ref
    o_ref[...] = w_ref[...].astype(BF16)


def _reorder_w_in(w_in):
    w_t = w_in.T
    src = []
    for first, dest, width in (('gate_attn', C_GA, 2 * D_MODEL), ('q', C_Q, D_ATTN),
                               ('z_attn', C_ZA, D_ATTN), ('glu_val', C_GV, 3 * D_CONV),
                               ('k', C_K, 2 * D_KV), ('q_idx', C_QI, D_H - C_QI)):
        assert dest == len(src) * REORDER_ROWS
        for r in range(0, width, REORDER_ROWS):
            src.append(min(_SRC[first][0] + r, D_IN - REORDER_ROWS) // SUBLANE)
    grid_spec = pltpu.PrefetchScalarGridSpec(
        num_scalar_prefetch=1,
        grid=(len(src),),
        in_specs=[pl.BlockSpec((pl.Element(REORDER_ROWS), pl.Element(D_MODEL)),
                               lambda d, off: (off[d] * SUBLANE, 0))],
        out_specs=pl.BlockSpec((REORDER_ROWS, D_MODEL), lambda d, off: (d, 0)),
    )
    return pl.pallas_call(
        _reorder_kernel,
        grid_spec=grid_spec,
        out_shape=jax.ShapeDtypeStruct((D_H, D_MODEL), BF16),
        compiler_params=_params(("arbitrary",)),
        name="reorder_w_in",
    )(jnp.asarray(src, jnp.int32), w_t)


def kernel(x_prompt, x_sample, cache_k, cache_v, cache_kidx, state_conv, page_table,
           ln_g, w_in, conv_w, conv_b, conv_norm_g, conv_norm_b, w_pw, b_pw,
           w_up_attn, w_up_conv, w_out, rel_bias, final_g):
    w_all = _reorder_w_in(w_in[0])
    g_in = ln_g[0].reshape(1, D_MODEL)
    xp = x_prompt.reshape(BATCH * SEQ, D_MODEL)
    xs = x_sample.reshape(DEC_BATCH * DEC_SEQ, D_MODEL)
    hb_p, hf_p, k_p, v_p = _proj(xp, g_in, w_all)
    hb_s, hf_s, k_s, v_s = _proj(xs, g_in, w_all)

    bias_p, bias_s = _bias_tables(rel_bias)

    attn_p = _attn_prompt(hb_p, hf_p, bias_p).reshape(BATCH * SEQ, D_ATTN)

    n_s = DEC_BATCH * DEC_SEQ
    scores = _score_sample(hf_s, page_table, cache_kidx[0].transpose(0, 2, 1))
    mask_t = _select_sample(scores.reshape(n_s, L_SAMPLE).T)
    mask = mask_t.T.reshape(DEC_BATCH, DEC_SEQ, L_SAMPLE)
    attn_s = _attn_sample(hb_s, hf_s, page_table, cache_k[0], cache_v[0], mask, bias_s)
    attn_s = attn_s.reshape(n_s, D_ATTN)

    cw_pad = jnp.concatenate([conv_w[0], jnp.zeros((PAD_ROWS - CONV_WIDTH, D_CONV), F32)], 0)
    cb = conv_b[0].reshape(1, D_CONV)
    dw_p, u_tail = _conv_prompt(hb_p, cw_pad, cb)
    dw_s, state_new = _conv_sample(hb_s[:, C_GV:C_GV + D_CONV].astype(F32),
                                   hb_s[:, C_GG:C_GG + D_CONV].astype(F32),
                                   state_conv[0].transpose(1, 0, 2), cw_pad, cb)

    wua = w_up_attn[0].astype(BF16)
    wpw = w_pw[0].astype(BF16)
    wuc = w_up_conv[0].astype(BF16)
    wo = w_out[0].astype(BF16)
    ng = conv_norm_g[0].reshape(1, D_CONV)
    nb = conv_norm_b[0].reshape(1, D_CONV)
    bpw = b_pw[0].reshape(1, D_CONV)
    fg = final_g.reshape(1, D_MODEL)
    weights = (wua, wpw, wuc, wo, ng, nb, bpw, fg)
    y_p = _tail(hb_p, attn_p, dw_p.reshape(BATCH * SEQ, D_CONV), xp, weights, TAIL_TM)
    y_s = _tail(hb_s, attn_s, dw_s, xs, weights, TAIL_TM)

    tail_rows = CONV_WIDTH - 1
    return (
        y_p.reshape(BATCH, SEQ, D_MODEL),
        y_s.reshape(DEC_BATCH, DEC_SEQ, D_MODEL),
        k_p.reshape(1, BATCH, SEQ, N_KV_HEADS, HEAD_DIM),
        v_p.reshape(1, BATCH, SEQ, N_KV_HEADS, HEAD_DIM),
        hf_p[:, F_KI:F_KI + IDX_DIM].reshape(1, BATCH, SEQ, IDX_DIM),
        u_tail[:, PAD_ROWS - tail_rows:].reshape(1, BATCH, tail_rows, D_CONV),
        k_s.reshape(1, DEC_BATCH, DEC_SEQ, N_KV_HEADS, HEAD_DIM),
        v_s.reshape(1, DEC_BATCH, DEC_SEQ, N_KV_HEADS, HEAD_DIM),
        hf_s[:, F_KI:F_KI + IDX_DIM].reshape(1, DEC_BATCH, DEC_SEQ, IDX_DIM),
        state_new.transpose(1, 0, 2).reshape(1, DEC_BATCH, tail_rows, D_CONV),
    )
```

```python
import functools
import math

import numpy as np
import jax
import jax.numpy as jnp
from jax import lax
from jax.experimental import pallas as pl
from jax.experimental.pallas import tpu as pltpu

F32 = jnp.float32
BF16 = jnp.bfloat16

D_MODEL = 2048
BATCH = 8
SEQ = 2048
DEC_BATCH = 128
DEC_SEQ = 8
PAST_LEN = 2048
PAGE_SIZE = 128
N_PAGES = PAST_LEN // PAGE_SIZE
N_HEADS = 8
N_KV_HEADS = 2
HEAD_DIM = 128
GROUP = N_HEADS // N_KV_HEADS
D_ATTN = N_HEADS * HEAD_DIM
D_KV = N_KV_HEADS * HEAD_DIM
IDX_HEADS = 8
IDX_DIM = 64
TOPK = 256
Q_BLOCK = 128
N_BUCKETS = 32
MAX_DISTANCE = 128
D_CONV = D_MODEL // 2
CONV_WIDTH = 31
EPS = 1e-6
NEG = -1e30
NEG_INF = float("-inf")

LANE = 128
SUBLANE = 8
VMEM_LIMIT = 56 * 1024 * 1024

C_GA = 0
C_GC = C_GA + D_MODEL
C_Q = C_GC + D_MODEL
C_ZA = C_Q + D_ATTN
C_GV = C_ZA + D_ATTN
C_GG = C_GV + D_CONV
C_ZC = C_GG + D_CONV
C_K = C_ZC + D_CONV
C_V = C_K + D_KV
C_QI = C_V + D_KV
C_KI = C_QI + IDX_HEADS * IDX_DIM
W_OFF = IDX_DIM
MXU_WIDTH = 256
PROJ_TM = 1024
PROJ_TN = 6 * MXU_WIDTH
D_H = -(-(C_KI + LANE) // PROJ_TN) * PROJ_TN
KV_TILE = C_K // PROJ_TN
assert C_K % PROJ_TN == 0 and D_H == C_K + PROJ_TN
D_HB = C_K
D_HF = C_KI + LANE - C_K
F_K = C_K - D_HB
F_V = C_V - D_HB
F_QI = C_QI - D_HB
F_KI = C_KI - D_HB

_SRC = {}
_off = 0
for _name, _w in (('q', D_ATTN), ('k', D_KV), ('v', D_KV), ('z_attn', D_ATTN),
                  ('q_idx', IDX_HEADS * IDX_DIM), ('k_idx', IDX_DIM), ('w_idx', IDX_HEADS),
                  ('glu_val', D_CONV), ('glu_gate', D_CONV), ('z_conv', D_CONV),
                  ('gate_attn', D_MODEL), ('gate_conv', D_MODEL)):
    _SRC[_name] = (_off, _w)
    _off += _w
D_IN = _off
REORDER_ROWS = 512

N_BISECT = 20
PAD_ROWS = 32
CONV_CHUNK = 64
CONV_LANES = 512
SCORE_SCALE = (IDX_DIM ** -0.5) * (IDX_HEADS ** -0.5)
QK_SCALE = HEAD_DIM ** -0.5
LOG2E = math.log2(math.e)
L_SAMPLE = (N_PAGES + 1) * LANE
SUBTILES = LANE // SUBLANE
CHUNK_BLOCKS = 2
CHUNK = CHUNK_BLOCKS * Q_BLOCK
PAD_KEYS = CHUNK - Q_BLOCK
VT_ROWS = HEAD_DIM + 2 * SUBLANE
SCORE_ROWS = 4
ATTN_ROWS = 4
TAIL_TM = 256


def _t5_bucket_static(dist):
    n = np.maximum(dist, 0)
    max_exact = N_BUCKETS // 2
    ratio = (np.log(np.maximum(n, 1).astype(np.float32) / np.float32(max_exact))
             / np.float32(math.log(MAX_DISTANCE / max_exact)))
    large = np.minimum(max_exact + (ratio * np.float32(N_BUCKETS - max_exact)).astype(np.int32),
                       N_BUCKETS - 1)
    return np.where(n < max_exact, n, large).astype(np.int32)


FAR_BUCKET = int(_t5_bucket_static(np.array([2 * MAX_DISTANCE]))[0])


def _params(sem):
    return pltpu.CompilerParams(dimension_semantics=sem, vmem_limit_bytes=VMEM_LIMIT)


def _proj_kernel(x_ref, g_ref, w_ref, ob_ref, of_ref, k_ref, v_ref, xn_ref):
    tm = x_ref.shape[0]
    j = pl.program_id(1)

    @pl.when(j == 0)
    def _():
        x = x_ref[...]
        ms = jnp.mean(x * x, axis=-1, keepdims=True)
        xn_ref[...] = (x * lax.rsqrt(ms + EPS) * g_ref[...]).astype(BF16)

    def project():
        return lax.dot_general(xn_ref[...], w_ref[...], (((1,), (1,)), ((), ())),
                               preferred_element_type=F32)

    @pl.when(j < KV_TILE)
    def _():
        ob_ref[...] = project().astype(BF16)

    @pl.when(j == KV_TILE)
    def _():
        of_ref[...] = project()[:, :D_HF]
        for g in range(N_KV_HEADS):
            rows = pl.ds(g, tm, stride=N_KV_HEADS)
            k_ref[rows, :] = of_ref[:, F_K + g * HEAD_DIM:F_K + (g + 1) * HEAD_DIM]
            v_ref[rows, :] = of_ref[:, F_V + g * HEAD_DIM:F_V + (g + 1) * HEAD_DIM]


def _proj(x2d, g, w):
    n = x2d.shape[0]
    tm, tn = PROJ_TM, PROJ_TN
    kv_spec = pl.BlockSpec((N_KV_HEADS * tm, HEAD_DIM), lambda i, j: (i, 0))
    kv_shape = jax.ShapeDtypeStruct((N_KV_HEADS * n, HEAD_DIM), F32)
    return pl.pallas_call(
        _proj_kernel,
        grid=(n // tm, D_H // tn),
        in_specs=[pl.BlockSpec((tm, D_MODEL), lambda i, j: (i, 0)),
                  pl.BlockSpec((1, D_MODEL), lambda i, j: (0, 0)),
                  pl.BlockSpec((tn, D_MODEL), lambda i, j: (j, 0))],
        out_specs=[pl.BlockSpec((tm, tn), lambda i, j: (i, jnp.minimum(j, KV_TILE - 1))),
                   pl.BlockSpec((tm, D_HF), lambda i, j: (i, 0)), kv_spec, kv_spec],
        out_shape=[jax.ShapeDtypeStruct((n, D_HB), BF16),
                   jax.ShapeDtypeStruct((n, D_HF), F32), kv_shape, kv_shape],
        scratch_shapes=[pltpu.VMEM((tm, D_MODEL), BF16)],
        compiler_params=_params(("parallel", "arbitrary")),
        name="proj",
    )(x2d, g, w)


def _bias_kernel(rb_ref, bp_ref, bs_ref, op_ref, os_ref):
    bp = bp_ref[...]
    bs = bs_ref[...]
    for h in range(N_HEADS):
        far = rb_ref[FAR_BUCKET, h]
        tp = jnp.zeros(bp.shape, F32)
        ts = jnp.zeros(bs.shape, F32)
        for b in range(N_BUCKETS):
            val = rb_ref[b, h] - far
            tp = jnp.where(bp == b, val * LOG2E, tp)
            ts = jnp.where(bs == b, val, ts)
        op_ref[h] = tp
        os_ref[h] = ts


def _bias_tables(rel_bias):
    key = np.arange(CHUNK)[:, None]
    qry = np.arange(Q_BLOCK)[None, :]
    bucket_p = _t5_bucket_static(CHUNK - Q_BLOCK + qry - key)
    qi = np.arange(DEC_SEQ)[:, None]
    col = np.arange(2 * LANE)[None, :]
    dist_s = np.where(col < LANE, LANE + qi - col, qi - (col - LANE))
    bucket_s = _t5_bucket_static(dist_s)
    return pl.pallas_call(
        _bias_kernel,
        in_specs=[pl.BlockSpec(memory_space=pltpu.SMEM),
                  pl.BlockSpec(memory_space=pltpu.VMEM),
                  pl.BlockSpec(memory_space=pltpu.VMEM)],
        out_specs=[pl.BlockSpec(memory_space=pltpu.VMEM),
                   pl.BlockSpec(memory_space=pltpu.VMEM)],
        out_shape=[jax.ShapeDtypeStruct((N_HEADS, CHUNK, Q_BLOCK), F32),
                   jax.ShapeDtypeStruct((N_HEADS, DEC_SEQ, 2 * LANE), F32)],
        name="bias_tables",
    )(rel_bias, jnp.asarray(bucket_p), jnp.asarray(bucket_s))


def _any(x):
    return jnp.max(jnp.where(x, 1.0, 0.0)) > 0.5


def _rep(x):
    return jnp.broadcast_to(x, (SUBLANE, LANE))


def _fold_rows(x, comb):
    parts = [x[k:k + SUBLANE] for k in range(0, x.shape[0], SUBLANE)]
    while len(parts) > 1:
        parts = [comb(parts[k], parts[k + 1]) for k in range(0, len(parts), 2)]
    return parts[0]


def _select_threshold(tile_fn, ntiles):
    def reduce_tiles(fn, init, comb):
        acc = init
        for j in range(ntiles):
            x = fn(tile_fn(j), j)
            parts = [x[k] for k in range(x.shape[0])]
            while len(parts) > 1:
                parts = [comb(parts[k], parts[k + 1]) for k in range(0, len(parts), 2)]
            acc = comb(acc, parts[0])
        return acc

    zeros = jnp.zeros((SUBLANE, LANE), F32)

    def count(pred_fn):
        acc = reduce_tiles(lambda s, j: jnp.where(pred_fn(s, j), 1.0, 0.0), zeros,
                           lambda a, b: a + b)
        return _rep(jnp.sum(acc, axis=0, keepdims=True))

    def masked_max(pred_fn):
        acc = reduce_tiles(lambda s, j: jnp.where(pred_fn(s, j), s, NEG_INF),
                           jnp.full((SUBLANE, LANE), NEG_INF, F32), jnp.maximum)
        return _rep(jnp.max(acc, axis=0, keepdims=True))

    bound = reduce_tiles(lambda s, j: jnp.where(s > NEG_INF, jnp.abs(s), 0.0), zeros,
                         jnp.maximum)
    bound = _rep(jnp.max(bound, axis=0, keepdims=True))

    def bisect(_, carry):
        lo, hi = carry
        mid = 0.5 * lo + 0.5 * hi
        few = count(lambda s, j: s > mid[None]) < TOPK
        return jnp.where(few, lo, mid), jnp.where(few, mid, hi)

    _, hi = lax.fori_loop(0, N_BISECT, bisect, (-bound, bound))

    thr = masked_max(lambda s, j: s <= hi[None])
    n_ge = count(lambda s, j: s >= thr[None])

    def fix_body(carry):
        thr, n_ge, _ = carry
        lower = masked_max(lambda s, j: s < thr[None])
        thr = jnp.where(n_ge < TOPK, lower, thr)
        n_ge = count(lambda s, j: s >= thr[None])
        return thr, n_ge, _any(n_ge < TOPK)

    thr, n_ge, _ = lax.while_loop(lambda c: c[2], fix_body, (thr, n_ge, _any(n_ge < TOPK)))
    n_gt = count(lambda s, j: s > thr[None])
    need = TOPK - n_gt
    return thr[0:1], need[0:1]


def _keep(pred):
    return jnp.where(pred, 0.0, NEG)


def _selection_masks(tiles, thr, need, seen, tri):
    ties = [s == thr for s in tiles]
    ranks = []
    for k in range(0, len(tiles), 2):
        pair = jnp.concatenate([jnp.where(t, 1.0, 0.0).astype(BF16) for t in ties[k:k + 2]], axis=1)
        rank = jnp.dot(tri, pair, preferred_element_type=F32)
        ranks += [rank[:, p * LANE:(p + 1) * LANE] for p in range(len(ties[k:k + 2]))]
    masks = []
    for s, tie, rank in zip(tiles, ties, ranks):
        masks.append(jnp.where(tie, _keep(rank + seen <= need), _keep(s > thr)))
        seen = seen + rank[s.shape[0] - 1:]
    return masks, seen


def _attn_p_kernel(q_ref, qi_ref, wi_ref, k_ref, v_ref, ki_ref, bias_ref, tri_ref, o_ref,
                   kb_ref, vt_ref, kib_ref, qh_ref, qih_ref,
                   score_ref, mask_ref, s_ref):
    i = pl.program_id(1)
    T = Q_BLOCK
    W = GROUP * T
    nch = i // CHUNK_BLOCKS + 1

    @pl.when(i == 0)
    def _():
        kb_ref[0:PAD_KEYS] = jnp.zeros((PAD_KEYS, D_KV), BF16)
        kb_ref[PAD_KEYS:] = k_ref[0].astype(BF16)
        kib_ref[0:PAD_KEYS] = jnp.zeros((PAD_KEYS, IDX_DIM), BF16)
        kib_ref[PAD_KEYS:] = ki_ref[0][:, :IDX_DIM].astype(BF16)
        ones_row = lax.broadcasted_iota(jnp.int32, (VT_ROWS - HEAD_DIM, PAD_KEYS + SEQ), 0) == 0
        for g in range(N_KV_HEADS):
            vt_ref[g, 0:HEAD_DIM, 0:PAD_KEYS] = jnp.zeros((HEAD_DIM, PAD_KEYS), BF16)
            vt_ref[g, HEAD_DIM:VT_ROWS, :] = jnp.where(ones_row, 1.0, 0.0).astype(BF16)
            for c in range(SEQ // LANE):
                blk = v_ref[0, c * LANE:(c + 1) * LANE, g * HEAD_DIM:(g + 1) * HEAD_DIM]
                vt_ref[g, 0:HEAD_DIM,
                       PAD_KEYS + c * LANE:PAD_KEYS + (c + 1) * LANE] = blk.T.astype(BF16)
        score_ref[0:PAD_KEYS] = jnp.full((PAD_KEYS, T), NEG_INF, F32)
        mask_ref[0:PAD_KEYS] = jnp.full((PAD_KEYS, T), NEG, F32)

    q = q_ref[0].astype(F32) * (QK_SCALE * LOG2E)
    for h in range(N_HEADS):
        qh_ref[h] = q[:, h * HEAD_DIM:(h + 1) * HEAD_DIM].astype(BF16)
    qi = qi_ref[0]
    for h in range(IDX_HEADS):
        qih_ref[h] = qi[:, h * IDX_DIM:(h + 1) * IDX_DIM].astype(BF16)
    w_rows = wi_ref[0].T[W_OFF:W_OFF + IDX_HEADS] * SCORE_SCALE

    def span(c):
        return pl.ds(pl.multiple_of((i - CHUNK_BLOCKS * c) * LANE, LANE), CHUNK)

    def first_key(c):
        return (i - CHUNK_BLOCKS * c) * LANE - PAD_KEYS

    key_l = lax.broadcasted_iota(jnp.int32, (CHUNK, T), 0)
    qry = i * T + lax.broadcasted_iota(jnp.int32, (CHUNK, T), 1)

    def score_chunk(c):
        kc = kib_ref[span(c), :]
        d = lax.dot_general(kc, qih_ref[...].reshape(IDX_HEADS * T, IDX_DIM),
                            (((1,), (1,)), ((), ())), preferred_element_type=F32)
        acc = jnp.zeros((CHUNK, T), F32)
        for h in range(IDX_HEADS):
            acc = acc + jnp.maximum(d[:, h * T:(h + 1) * T], 0.0) * w_rows[h:h + 1]
        key = first_key(c) + key_l
        acc = jnp.where(key <= qry, jnp.where(key >= 0, acc, NEG_INF), NEG_INF)
        score_ref[span(c), :] = acc

    nsub = CHUNK // SUBLANE

    def tile_fn(c):
        return score_ref[span(c), :].reshape(nsub, SUBLANE, T)

    def select_all():
        mask_ref[span(0), :] = _keep(score_ref[span(0), :] > NEG_INF)

    def select(n):
        thr, need = _select_threshold(tile_fn, n)
        parts = []
        for c in reversed(range(n)):
            start = pl.multiple_of((i - CHUNK_BLOCKS * c) * LANE, LANE)
            parts += [pl.ds(start + p * MXU_WIDTH, MXU_WIDTH) for p in range(CHUNK // MXU_WIDTH)]
        masks, _ = _selection_masks([score_ref[rows, :] for rows in parts], thr, need,
                                    jnp.zeros((1, T), F32), tri_ref[...])
        for rows, mask in zip(parts, masks):
            mask_ref[rows, :] = mask

    def logits(c, g):
        kc = kb_ref[span(c), g * HEAD_DIM:(g + 1) * HEAD_DIM]
        qg = qh_ref[g * GROUP:(g + 1) * GROUP].reshape(W, HEAD_DIM)
        s = lax.dot_general(kc, qg, (((1,), (1,)), ((), ())), preferred_element_type=F32)
        mb = mask_ref[span(c), :]
        if c == 0:
            add = jnp.concatenate([mb + bias_ref[g * GROUP + hq] for hq in range(GROUP)], axis=1)
        else:
            add = jnp.concatenate([mb] * GROUP, axis=1)
        s = s + add
        s_ref[span(c), g * W:(g + 1) * W] = s
        return jnp.max(_fold_rows(s, jnp.maximum), axis=0, keepdims=True)

    def run(n):
        for c in range(n):
            score_chunk(c)
        if n == 1:
            pl.when(i * T + T <= TOPK)(select_all)
            pl.when(i * T + T > TOPK)(functools.partial(select, n))
        else:
            select(n)
        for g in range(N_KV_HEADS):
            m = functools.reduce(jnp.maximum, [logits(c, g) for c in range(n)])
            acc = None
            for c in range(n):
                p = jnp.exp2((s_ref[span(c), g * W:(g + 1) * W] - m).astype(BF16))
                pv = jnp.dot(vt_ref[g, :, span(c)], p, preferred_element_type=F32)
                acc = pv if acc is None else acc + pv
            o = acc[0:HEAD_DIM] / acc[HEAD_DIM:HEAD_DIM + 1]
            for hq in range(GROUP):
                h = g * GROUP + hq
                o_ref[0, :, h * HEAD_DIM:(h + 1) * HEAD_DIM] = o[:, hq * T:(hq + 1) * T].T

    for n in range(1, SEQ // CHUNK + 1):
        pl.when(nch == n)(functools.partial(run, n))


def _attn_prompt(hb_p, hf_p, bias_p):
    hb3 = hb_p.reshape(BATCH, SEQ, D_HB)
    hf3 = hf_p.reshape(BATCH, SEQ, D_HF)
    nqb = SEQ // Q_BLOCK
    T = Q_BLOCK
    qi_w = IDX_HEADS * IDX_DIM
    return pl.pallas_call(
        _attn_p_kernel,
        grid=(BATCH, nqb),
        in_specs=[
            pl.BlockSpec((1, T, D_ATTN), lambda b, i: (b, i, C_Q // D_ATTN)),
            pl.BlockSpec((1, T, qi_w), lambda b, i: (b, i, F_QI // qi_w)),
            pl.BlockSpec((1, T, LANE), lambda b, i: (b, i, F_KI // LANE)),
            pl.BlockSpec((1, SEQ, D_KV), lambda b, i: (b, 0, F_K // D_KV)),
            pl.BlockSpec((1, SEQ, D_KV), lambda b, i: (b, 0, F_V // D_KV)),
            pl.BlockSpec((1, SEQ, LANE), lambda b, i: (b, 0, F_KI // LANE)),
            pl.BlockSpec((N_HEADS, CHUNK, T), lambda b, i: (0, 0, 0)),
            pl.BlockSpec((MXU_WIDTH, MXU_WIDTH), lambda b, i: (0, 0)),
        ],
        out_specs=pl.BlockSpec((1, T, D_ATTN), lambda b, i: (b, i, 0)),
        out_shape=jax.ShapeDtypeStruct((BATCH, SEQ, D_ATTN), F32),
        scratch_shapes=[
            pltpu.VMEM((PAD_KEYS + SEQ, D_KV), BF16),
            pltpu.VMEM((N_KV_HEADS, VT_ROWS, PAD_KEYS + SEQ), BF16),
            pltpu.VMEM((PAD_KEYS + SEQ, IDX_DIM), BF16),
            pltpu.VMEM((N_HEADS, T, HEAD_DIM), BF16),
            pltpu.VMEM((IDX_HEADS, T, IDX_DIM), BF16),
            pltpu.VMEM((PAD_KEYS + SEQ, T), F32),
            pltpu.VMEM((PAD_KEYS + SEQ, T), F32),
            pltpu.VMEM((PAD_KEYS + SEQ, N_HEADS * T), F32),
        ],
        compiler_params=_params(("parallel", "arbitrary")),
        name="attn_prompt",
    )(hb3, hf3, hf3, hf3, hf3, hf3, bias_p, jnp.tri(MXU_WIDTH, dtype=BF16))


def _score_s_kernel(pt_ref, qi_ref, w_ref, kin_ref, *rest):
    npg = SCORE_ROWS * N_PAGES
    kip = rest[0:npg]
    o_ref = rest[npg]
    kinp_ref = rest[npg + 1]
    del pt_ref
    R = DEC_SEQ
    qrow = lax.broadcasted_iota(jnp.int32, (R, LANE), 0)
    lane = lax.broadcasted_iota(jnp.int32, (R, LANE), 1)
    kinp_ref[...] = jnp.zeros(kinp_ref.shape, BF16)
    for r in range(SCORE_ROWS):
        kinp_ref[r, 0:2 * R] = jnp.concatenate(
            [kin_ref[r][:, :IDX_DIM], jnp.zeros((R, IDX_DIM), F32)], 0).astype(BF16)
    for r in range(SCORE_ROWS):
        qi = qi_ref[r].astype(BF16)
        wb = jnp.broadcast_to(w_ref[r] * SCORE_SCALE, (IDX_HEADS * R, LANE))
        for t in range(N_PAGES + 1):
            if t < N_PAGES:
                d = jnp.dot(qi, kip[r * N_PAGES + t][0].astype(BF16), preferred_element_type=F32)
            else:
                d = lax.dot_general(qi, kinp_ref[r], (((1,), (1,)), ((), ())),
                                    preferred_element_type=F32)
            e = (jnp.maximum(d, 0.0) * wb).reshape(IDX_HEADS, R, LANE)
            s = e[0]
            for h in range(1, IDX_HEADS):
                s = s + e[h]
            if t == N_PAGES:
                s = jnp.where(lane <= qrow, s, NEG_INF)
            o_ref[r, :, t * LANE:(t + 1) * LANE] = s


def _score_sample(hf_s, page_table, cache_kidx_t):
    R = DEC_SEQ
    G = SCORE_ROWS
    h3 = hf_s.reshape(DEC_BATCH, R, D_HF)
    qi_hq = h3[:, :, F_QI:F_QI + IDX_HEADS * IDX_DIM].reshape(DEC_BATCH, R, IDX_HEADS, IDX_DIM)
    qi_hq = qi_hq.transpose(0, 2, 1, 3).reshape(DEC_BATCH, IDX_HEADS * R, IDX_DIM)
    w_hq = h3[:, :, F_KI + W_OFF:F_KI + W_OFF + IDX_HEADS].transpose(0, 2, 1)
    w_hq = w_hq.reshape(DEC_BATCH, IDX_HEADS * R, 1)
    in_specs = [
        pl.BlockSpec((G, IDX_HEADS * R, IDX_DIM), lambda b, pt: (b, 0, 0)),
        pl.BlockSpec((G, IDX_HEADS * R, 1), lambda b, pt: (b, 0, 0)),
        pl.BlockSpec((G, R, LANE), lambda b, pt: (b, 0, F_KI // LANE)),
    ]
    in_specs += [pl.BlockSpec((1, IDX_DIM, PAGE_SIZE),
                              lambda b, pt, r=r, p=p: (pt[b * G + r, p], 0, 0))
                 for r in range(G) for p in range(N_PAGES)]
    grid_spec = pltpu.PrefetchScalarGridSpec(
        num_scalar_prefetch=1,
        grid=(DEC_BATCH // G,),
        in_specs=in_specs,
        out_specs=pl.BlockSpec((G, R, L_SAMPLE), lambda b, pt: (b, 0, 0)),
        scratch_shapes=[pltpu.VMEM((G, PAGE_SIZE, IDX_DIM), BF16)],
    )
    return pl.pallas_call(
        _score_s_kernel,
        grid_spec=grid_spec,
        out_shape=jax.ShapeDtypeStruct((DEC_BATCH, R, L_SAMPLE), F32),
        compiler_params=_params(("arbitrary",)),
        name="score_sample",
    )(page_table, qi_hq, w_hq, h3, *([cache_kidx_t] * (G * N_PAGES)))


def _select_s_kernel(s_ref, tri_ref, o_ref):
    nt = N_PAGES + 1

    def tile_fn(j):
        return s_ref[j * LANE:(j + 1) * LANE, :].reshape(SUBTILES, SUBLANE, LANE)

    thr, need = _select_threshold(tile_fn, nt)
    parts = [slice(j * LANE, (j + 1) * LANE) for j in range(nt)]
    masks, _ = _selection_masks([s_ref[rows, :] for rows in parts], thr, need,
                                jnp.zeros((1, LANE), F32), tri_ref[...])
    for rows, mask in zip(parts, masks):
        o_ref[rows, :] = mask


def _select_sample(scores_t):
    n = scores_t.shape[1]
    return pl.pallas_call(
        _select_s_kernel,
        grid=(n // LANE,),
        in_specs=[pl.BlockSpec((L_SAMPLE, LANE), lambda c: (0, c)),
                  pl.BlockSpec((LANE, LANE), lambda c: (0, 0))],
        out_specs=pl.BlockSpec((L_SAMPLE, LANE), lambda c: (0, c)),
        out_shape=jax.ShapeDtypeStruct((L_SAMPLE, n), F32),
        compiler_params=_params(("parallel",)),
        name="select_sample",
    )(scores_t, jnp.tri(LANE, dtype=BF16))


def _attn_s_kernel(pt_ref, q_ref, kn_ref, vn_ref, mask_ref, bias_ref, *rest):
    npg = ATTN_ROWS * N_PAGES
    kp = rest[0:npg]
    vp = rest[npg:2 * npg]
    o_ref = rest[2 * npg]
    knp_ref, vnp_ref, logit_ref = rest[2 * npg + 1:]
    del pt_ref
    R = DEC_SEQ
    NT = N_PAGES + 1
    GR = GROUP * R

    knp_ref[...] = jnp.zeros(knp_ref.shape, BF16)
    vnp_ref[...] = jnp.zeros(vnp_ref.shape, BF16)
    for r in range(ATTN_ROWS):
        knp_ref[r, 0:2 * R] = jnp.concatenate([kn_ref[r], jnp.zeros((R, D_KV), F32)], 0).astype(BF16)
        vnp_ref[r, 0:2 * R] = jnp.concatenate([vn_ref[r], jnp.zeros((R, D_KV), F32)], 0).astype(BF16)

    def page_head(refs, pad_ref, r, t, g):
        if t < N_PAGES:
            return refs[r * N_PAGES + t][pl.ds(g, PAGE_SIZE, stride=N_KV_HEADS), :].astype(BF16)
        return pad_ref[r, :, g * HEAD_DIM:(g + 1) * HEAD_DIM]

    for r in range(ATTN_ROWS):
        q = (q_ref[r].astype(F32) * QK_SCALE).astype(BF16)
        for t in range(NT):
            mb = mask_ref[r, :, t * LANE:(t + 1) * LANE]
            for g in range(N_KV_HEADS):
                lg = lax.dot_general(q[g * GR:(g + 1) * GR], page_head(kp, knp_ref, r, t, g),
                                     (((1,), (1,)), ((), ())), preferred_element_type=F32)
                lg = lg.reshape(GROUP, R, LANE) + mb[None]
                if t >= N_PAGES - 1:
                    off = (t - (N_PAGES - 1)) * LANE
                    lg = lg + bias_ref[g * GROUP:(g + 1) * GROUP, :, off:off + LANE]
                logit_ref[r, g * GR:(g + 1) * GR, t * LANE:(t + 1) * LANE] = lg.reshape(GR, LANE)

    for r in range(ATTN_ROWS):
        logits = logit_ref[r]
        m = jnp.max(logits, axis=1, keepdims=True)
        p = jnp.exp(logits - m)
        inv = 1.0 / jnp.sum(p, axis=1, keepdims=True)
        pb = p.astype(BF16)
        outs = [jnp.zeros((GR, HEAD_DIM), F32) for _ in range(N_KV_HEADS)]
        for t in range(NT):
            for g in range(N_KV_HEADS):
                outs[g] = outs[g] + jnp.dot(pb[g * GR:(g + 1) * GR, t * LANE:(t + 1) * LANE],
                                            page_head(vp, vnp_ref, r, t, g),
                                            preferred_element_type=F32)
        for g in range(N_KV_HEADS):
            o = outs[g] * inv[g * GR:(g + 1) * GR]
            for hq in range(GROUP):
                h = g * GROUP + hq
                o_ref[r, :, h * HEAD_DIM:(h + 1) * HEAD_DIM] = o[hq * R:(hq + 1) * R]


def _attn_sample(hb_s, hf_s, page_table, cache_k, cache_v, mask, bias_s):
    R = DEC_SEQ
    G = ATTN_ROWS
    h3 = hf_s.reshape(DEC_BATCH, R, D_HF)
    q_hq = hb_s[:, C_Q:C_Q + D_ATTN].reshape(DEC_BATCH, R, N_HEADS, HEAD_DIM)
    q_hq = q_hq.transpose(0, 2, 1, 3).reshape(DEC_BATCH, N_HEADS * R, HEAD_DIM)
    rows_per_page = PAGE_SIZE * N_KV_HEADS
    ck = cache_k.reshape(-1, HEAD_DIM)
    cv = cache_v.reshape(-1, HEAD_DIM)

    in_specs = [
        pl.BlockSpec((G, N_HEADS * R, HEAD_DIM), lambda b, pt: (b, 0, 0)),
        pl.BlockSpec((G, R, D_KV), lambda b, pt: (b, 0, F_K // D_KV)),
        pl.BlockSpec((G, R, D_KV), lambda b, pt: (b, 0, F_V // D_KV)),
        pl.BlockSpec((G, R, L_SAMPLE), lambda b, pt: (b, 0, 0)),
        pl.BlockSpec((N_HEADS, R, 2 * LANE), lambda b, pt: (0, 0, 0)),
    ]
    pages = [pl.BlockSpec((rows_per_page, HEAD_DIM), lambda b, pt, r=r, p=p: (pt[b * G + r, p], 0))
             for r in range(G) for p in range(N_PAGES)]
    in_specs += pages + pages
    grid_spec = pltpu.PrefetchScalarGridSpec(
        num_scalar_prefetch=1,
        grid=(DEC_BATCH // G,),
        in_specs=in_specs,
        out_specs=pl.BlockSpec((G, R, D_ATTN), lambda b, pt: (b, 0, 0)),
        scratch_shapes=[
            pltpu.VMEM((G, PAGE_SIZE, D_KV), BF16),
            pltpu.VMEM((G, PAGE_SIZE, D_KV), BF16),
            pltpu.VMEM((G, N_HEADS * R, L_SAMPLE), F32),
        ],
    )
    return pl.pallas_call(
        _attn_s_kernel,
        grid_spec=grid_spec,
        out_shape=jax.ShapeDtypeStruct((DEC_BATCH, R, D_ATTN), F32),
        compiler_params=_params(("arbitrary",)),
        name="attn_sample",
    )(page_table, q_hq, h3, h3, mask, bias_s, *([ck] * (G * N_PAGES)), *([cv] * (G * N_PAGES)))


def _conv_p_kernel(val_ref, gate_ref, cw_ref, cb_ref, dw_ref, ut_ref, pad_ref):
    first = PAD_ROWS - (CONV_WIDTH - 1)
    for t in range(CONV_LANES // LANE):
        lanes = slice(t * LANE, (t + 1) * LANE)
        pad = pad_ref.at[t]
        pad[0:PAD_ROWS] = jnp.zeros((PAD_ROWS, LANE), F32)
        pad[PAD_ROWS:] = (val_ref[0, :, lanes].astype(F32)
                          * jax.nn.sigmoid(gate_ref[0, :, lanes].astype(F32)))
        ut_ref[0, :, lanes] = pad[SEQ:SEQ + PAD_ROWS]
        cw = cw_ref[:, lanes]
        cb = cb_ref[:, lanes]
        for c in range(SEQ // CONV_CHUNK):
            base = c * CONV_CHUNK
            acc = jnp.broadcast_to(cb, (CONV_CHUNK, LANE))
            for r in range(SUBLANE):
                taps = [w for w in range(CONV_WIDTH) if (first + w) % SUBLANE == r]
                span = max(first + w - r for w in taps) + CONV_CHUNK
                win = pad[base + r:base + r + span]
                for w in taps:
                    a = first + w - r
                    acc = acc + win[a:a + CONV_CHUNK] * cw[w:w + 1]
            dw_ref[0, base:base + CONV_CHUNK, lanes] = acc


def _conv_prompt(hb_p, cw_pad, cb):
    h3 = hb_p.reshape(BATCH, SEQ, D_HB)
    nc = D_CONV // CONV_LANES
    return pl.pallas_call(
        _conv_p_kernel,
        grid=(BATCH, nc),
        in_specs=[pl.BlockSpec((1, SEQ, CONV_LANES), lambda b, c: (b, 0, C_GV // CONV_LANES + c)),
                  pl.BlockSpec((1, SEQ, CONV_LANES), lambda b, c: (b, 0, C_GG // CONV_LANES + c)),
                  pl.BlockSpec((PAD_ROWS, CONV_LANES), lambda b, c: (0, c)),
                  pl.BlockSpec((1, CONV_LANES), lambda b, c: (0, c))],
        out_specs=[pl.BlockSpec((1, SEQ, CONV_LANES), lambda b, c: (b, 0, c)),
                   pl.BlockSpec((1, PAD_ROWS, CONV_LANES), lambda b, c: (b, 0, c))],
        out_shape=[jax.ShapeDtypeStruct((BATCH, SEQ, D_CONV), F32),
                   jax.ShapeDtypeStruct((BATCH, PAD_ROWS, D_CONV), F32)],
        scratch_shapes=[pltpu.VMEM((CONV_LANES // LANE, PAD_ROWS + SEQ, LANE), F32)],
        compiler_params=_params(("parallel", "parallel")),
        name="conv_prompt",
    )(h3, h3, cw_pad, cb)


def _conv_s_kernel(val_ref, gate_ref, st_ref, cw_ref, cb_ref, dw_ref, ns_ref):
    R = DEC_SEQ
    H = CONV_WIDTH - 1
    cw = cw_ref[...]
    cb = jnp.broadcast_to(cb_ref[...], (DEC_BATCH, LANE))
    u = []
    for q in range(R):
        rows = pl.ds(q, DEC_BATCH, stride=R)
        u.append(val_ref[rows, :] * jax.nn.sigmoid(gate_ref[rows, :]))

    def row(r):
        return st_ref[r] if r < H else u[r - H]

    for q in range(R):
        acc = cb
        for w in range(CONV_WIDTH):
            acc = acc + row(q + w) * cw[w:w + 1]
        dw_ref[pl.ds(q, DEC_BATCH, stride=R), :] = acc
    for r in range(H):
        ns_ref[r] = row(r + R)


def _conv_sample(val, gate, state_t, cw_pad, cb):
    n_s = DEC_BATCH * DEC_SEQ
    H = CONV_WIDTH - 1
    nc = D_CONV // LANE
    return pl.pallas_call(
        _conv_s_kernel,
        grid=(nc,),
        in_specs=[pl.BlockSpec((n_s, LANE), lambda c: (0, c)),
                  pl.BlockSpec((n_s, LANE), lambda c: (0, c)),
                  pl.BlockSpec((H, DEC_BATCH, LANE), lambda c: (0, 0, c)),
                  pl.BlockSpec((PAD_ROWS, LANE), lambda c: (0, c)),
                  pl.BlockSpec((1, LANE), lambda c: (0, c))],
        out_specs=[pl.BlockSpec((n_s, LANE), lambda c: (0, c)),
                   pl.BlockSpec((H, DEC_BATCH, LANE), lambda c: (0, 0, c))],
        out_shape=[jax.ShapeDtypeStruct((n_s, D_CONV), F32),
                   jax.ShapeDtypeStruct((H, DEC_BATCH, D_CONV), F32)],
        compiler_params=_params(("parallel",)),
        name="conv_sample",
    )(val, gate, state_t, cw_pad, cb)


def _tail_kernel(attn_ref, za_ref, dw_ref, zc_ref, ga_ref, gc_ref, x_ref,
                 wua_ref, wpw_ref, wuc_ref, wo_ref, ng_ref, nb_ref, bpw_ref, fg_ref, y_ref):
    a = attn_ref[...] * jax.nn.silu(za_ref[...].astype(F32))
    branch_attn = jnp.dot(a.astype(BF16), wua_ref[...], preferred_element_type=F32)

    dw = dw_ref[...]
    mu = jnp.mean(dw, axis=-1, keepdims=True)
    var = jnp.mean(jnp.square(dw - mu), axis=-1, keepdims=True)
    ln = (dw - mu) * lax.rsqrt(var + EPS) * ng_ref[...] + nb_ref[...]
    conv_out = jnp.dot(jax.nn.silu(ln).astype(BF16), wpw_ref[...],
                       preferred_element_type=F32) + bpw_ref[...]
    c = conv_out * jax.nn.silu(zc_ref[...].astype(F32))
    branch_conv = jnp.dot(c.astype(BF16), wuc_ref[...], preferred_element_type=F32)

    merged = (jax.nn.sigmoid(ga_ref[...].astype(F32)) * branch_attn
              + jax.nn.sigmoid(gc_ref[...].astype(F32)) * branch_conv)
    y = x_ref[...] + jnp.dot(merged.astype(BF16), wo_ref[...], preferred_element_type=F32)
    ms = jnp.mean(y * y, axis=-1, keepdims=True)
    y_ref[...] = y * lax.rsqrt(ms + EPS) * fg_ref[...]


def _tail(h, attn, dw, x2d, weights, tm):
    n = x2d.shape[0]

    def const(shape):
        return pl.BlockSpec(shape, lambda i: (0, 0), pipeline_mode=pl.Buffered(1))

    def cols(width, offset):
        return pl.BlockSpec((tm, width), lambda i: (i, offset // width))

    return pl.pallas_call(
        _tail_kernel,
        grid=(n // tm,),
        in_specs=[cols(D_ATTN, 0), cols(D_ATTN, C_ZA), cols(D_CONV, 0), cols(D_CONV, C_ZC),
                  cols(D_MODEL, C_GA), cols(D_MODEL, C_GC), cols(D_MODEL, 0),
                  const((D_ATTN, D_MODEL)), const((D_CONV, D_CONV)),
                  const((D_CONV, D_MODEL)), const((D_MODEL, D_MODEL)),
                  const((1, D_CONV)), const((1, D_CONV)), const((1, D_CONV)),
                  const((1, D_MODEL))],
        out_specs=pl.BlockSpec((tm, D_MODEL), lambda i: (i, 0)),
        out_shape=jax.ShapeDtypeStruct((n, D_MODEL), F32),
        compiler_params=_params(("parallel",)),
        name="tail",
    )(attn, h, dw, h, h, h, x2d, *weights)


def _reorder_kernel(off_ref, w_ref, o_ref):
    del off_ref
    o_ref[...] = w_ref[...].astype(BF16)


def _reorder_w_in(w_in):
    w_t = w_in.T
    src = []
    for first, dest, width in (('gate_attn', C_GA, 2 * D_MODEL), ('q', C_Q, D_ATTN),
                               ('z_attn', C_ZA, D_ATTN), ('glu_val', C_GV, 3 * D_CONV),
                               ('k', C_K, 2 * D_KV), ('q_idx', C_QI, D_H - C_QI)):
        assert dest == len(src) * REORDER_ROWS
        for r in range(0, width, REORDER_ROWS):
            src.append(min(_SRC[first][0] + r, D_IN - REORDER_ROWS) // SUBLANE)
    grid_spec = pltpu.PrefetchScalarGridSpec(
        num_scalar_prefetch=1,
        grid=(len(src),),
        in_specs=[pl.BlockSpec((pl.Element(REORDER_ROWS), pl.Element(D_MODEL)),
                               lambda d, off: (off[d] * SUBLANE, 0))],
        out_specs=pl.BlockSpec((REORDER_ROWS, D_MODEL), lambda d, off: (d, 0)),
    )
    return pl.pallas_call(
        _reorder_kernel,
        grid_spec=grid_spec,
        out_shape=jax.ShapeDtypeStruct((D_H, D_MODEL), BF16),
        compiler_params=_params(("arbitrary",)),
        name="reorder_w_in",
    )(jnp.asarray(src, jnp.int32), w_t)


def kernel(x_prompt, x_sample, cache_k, cache_v, cache_kidx, state_conv, page_table,
           ln_g, w_in, conv_w, conv_b, conv_norm_g, conv_norm_b, w_pw, b_pw,
           w_up_attn, w_up_conv, w_out, rel_bias, final_g):
    w_all = _reorder_w_in(w_in[0])
    g_in = ln_g[0].reshape(1, D_MODEL)
    xp = x_prompt.reshape(BATCH * SEQ, D_MODEL)
    xs = x_sample.reshape(DEC_BATCH * DEC_SEQ, D_MODEL)
    hb_p, hf_p, k_p, v_p = _proj(xp, g_in, w_all)
    hb_s, hf_s, k_s, v_s = _proj(xs, g_in, w_all)

    bias_p, bias_s = _bias_tables(rel_bias)

    attn_p = _attn_prompt(hb_p, hf_p, bias_p).reshape(BATCH * SEQ, D_ATTN)

    n_s = DEC_BATCH * DEC_SEQ
    scores = _score_sample(hf_s, page_table, cache_kidx[0].transpose(0, 2, 1))
    mask_t = _select_sample(scores.reshape(n_s, L_SAMPLE).T)
    mask = mask_t.T.reshape(DEC_BATCH, DEC_SEQ, L_SAMPLE)
    attn_s = _attn_sample(hb_s, hf_s, page_table, cache_k[0], cache_v[0], mask, bias_s)
    attn_s = attn_s.reshape(n_s, D_ATTN)

    cw_pad = jnp.concatenate([conv_w[0], jnp.zeros((PAD_ROWS - CONV_WIDTH, D_CONV), F32)], 0)
    cb = conv_b[0].reshape(1, D_CONV)
    dw_p, u_tail = _conv_prompt(hb_p, cw_pad, cb)
    dw_s, state_new = _conv_sample(hb_s[:, C_GV:C_GV + D_CONV].astype(F32),
                                   hb_s[:, C_GG:C_GG + D_CONV].astype(F32),
                                   state_conv[0].transpose(1, 0, 2), cw_pad, cb)

    wua = w_up_attn[0].astype(BF16)
    wpw = w_pw[0].astype(BF16)
    wuc = w_up_conv[0].astype(BF16)
    wo = w_out[0].astype(BF16)
    ng = conv_norm_g[0].reshape(1, D_CONV)
    nb = conv_norm_b[0].reshape(1, D_CONV)
    bpw = b_pw[0].reshape(1, D_CONV)
    fg = final_g.reshape(1, D_MODEL)
    weights = (wua, wpw, wuc, wo, ng, nb, bpw, fg)
    y_p = _tail(hb_p, attn_p, dw_p.reshape(BATCH * SEQ, D_CONV), xp, weights, TAIL_TM)
    y_s = _tail(hb_s, attn_s, dw_s, xs, weights, TAIL_TM)

    tail_rows = CONV_WIDTH - 1
    return (
        y_p.reshape(BATCH, SEQ, D_MODEL),
        y_s.reshape(DEC_BATCH, DEC_SEQ, D_MODEL),
        k_p.reshape(1, BATCH, SEQ, N_KV_HEADS, HEAD_DIM),
        v_p.reshape(1, BATCH, SEQ, N_KV_HEADS, HEAD_DIM),
        hf_p[:, F_KI:F_KI + IDX_DIM].reshape(1, BATCH, SEQ, IDX_DIM),
        u_tail[:, PAD_ROWS - tail_rows:].reshape(1, BATCH, tail_rows, D_CONV),
        k_s.reshape(1, DEC_BATCH, DEC_SEQ, N_KV_HEADS, HEAD_DIM),
        v_s.reshape(1, DEC_BATCH, DEC_SEQ, N_KV_HEADS, HEAD_DIM),
        hf_s[:, F_KI:F_KI + IDX_DIM].reshape(1, DEC_BATCH, DEC_SEQ, IDX_DIM),
        state_new.transpose(1, 0, 2).reshape(1, DEC_BATCH, tail_rows, D_CONV),
    )
```

```python
import functools
import math

import numpy as np
import jax
import jax.numpy as jnp
from jax import lax
from jax.experimental import pallas as pl
from jax.experimental.pallas import tpu as pltpu

F32 = jnp.float32
BF16 = jnp.bfloat16

D_MODEL = 2048
BATCH = 8
SEQ = 2048
DEC_BATCH = 128
DEC_SEQ = 8
PAST_LEN = 2048
PAGE_SIZE = 128
N_PAGES = PAST_LEN // PAGE_SIZE
N_HEADS = 8
N_KV_HEADS = 2
HEAD_DIM = 128
GROUP = N_HEADS // N_KV_HEADS
D_ATTN = N_HEADS * HEAD_DIM
D_KV = N_KV_HEADS * HEAD_DIM
IDX_HEADS = 8
IDX_DIM = 64
TOPK = 256
Q_BLOCK = 128
N_BUCKETS = 32
MAX_DISTANCE = 128
D_CONV = D_MODEL // 2
CONV_WIDTH = 31
EPS = 1e-6
NEG = -1e30
NEG_INF = float("-inf")

LANE = 128
SUBLANE = 8
VMEM_LIMIT = 56 * 1024 * 1024

C_GA = 0
C_GC = C_GA + D_MODEL
C_Q = C_GC + D_MODEL
C_ZA = C_Q + D_ATTN
C_GV = C_ZA + D_ATTN
C_GG = C_GV + D_CONV
C_ZC = C_GG + D_CONV
C_K = C_ZC + D_CONV
C_V = C_K + D_KV
C_QI = C_V + D_KV
C_KI = C_QI + IDX_HEADS * IDX_DIM
W_OFF = IDX_DIM
MXU_WIDTH = 256
PROJ_TM = 1024
PROJ_TN = 6 * MXU_WIDTH
D_H = -(-(C_KI + LANE) // PROJ_TN) * PROJ_TN
KV_TILE = C_K // PROJ_TN
assert C_K % PROJ_TN == 0 and D_H == C_K + PROJ_TN
D_HB = C_K
D_HF = C_KI + LANE - C_K
F_K = C_K - D_HB
F_V = C_V - D_HB
F_QI = C_QI - D_HB
F_KI = C_KI - D_HB

_SRC = {}
_off = 0
for _name, _w in (('q', D_ATTN), ('k', D_KV), ('v', D_KV), ('z_attn', D_ATTN),
                  ('q_idx', IDX_HEADS * IDX_DIM), ('k_idx', IDX_DIM), ('w_idx', IDX_HEADS),
                  ('glu_val', D_CONV), ('glu_gate', D_CONV), ('z_conv', D_CONV),
                  ('gate_attn', D_MODEL), ('gate_conv', D_MODEL)):
    _SRC[_name] = (_off, _w)
    _off += _w
D_IN = _off
REORDER_ROWS = 512

N_BISECT = 20
PAD_ROWS = 32
CONV_CHUNK = 64
SCORE_SCALE = (IDX_DIM ** -0.5) * (IDX_HEADS ** -0.5)
QK_SCALE = HEAD_DIM ** -0.5
LOG2E = math.log2(math.e)
L_SAMPLE = (N_PAGES + 1) * LANE
SUBTILES = LANE // SUBLANE
CHUNK_BLOCKS = 2
CHUNK = CHUNK_BLOCKS * Q_BLOCK
PAD_KEYS = CHUNK - Q_BLOCK
VT_ROWS = HEAD_DIM + 2 * SUBLANE
SCORE_ROWS = 4
ATTN_ROWS = 4
TAIL_TM = 256


def _t5_bucket_static(dist):
    n = np.maximum(dist, 0)
    max_exact = N_BUCKETS // 2
    ratio = (np.log(np.maximum(n, 1).astype(np.float32) / np.float32(max_exact))
             / np.float32(math.log(MAX_DISTANCE / max_exact)))
    large = np.minimum(max_exact + (ratio * np.float32(N_BUCKETS - max_exact)).astype(np.int32),
                       N_BUCKETS - 1)
    return np.where(n < max_exact, n, large).astype(np.int32)


FAR_BUCKET = int(_t5_bucket_static(np.array([2 * MAX_DISTANCE]))[0])


def _params(sem):
    return pltpu.CompilerParams(dimension_semantics=sem, vmem_limit_bytes=VMEM_LIMIT)


def _proj_kernel(x_ref, g_ref, w_ref, ob_ref, of_ref, k_ref, v_ref, xn_ref):
    tm = x_ref.shape[0]
    j = pl.program_id(1)

    @pl.when(j == 0)
    def _():
        x = x_ref[...]
        ms = jnp.mean(x * x, axis=-1, keepdims=True)
        xn_ref[...] = (x * lax.rsqrt(ms + EPS) * g_ref[...]).astype(BF16)

    def project():
        return lax.dot_general(xn_ref[...], w_ref[...], (((1,), (1,)), ((), ())),
                               preferred_element_type=F32)

    @pl.when(j < KV_TILE)
    def _():
        ob_ref[...] = project().astype(BF16)

    @pl.when(j == KV_TILE)
    def _():
        of_ref[...] = project()[:, :D_HF]
        for g in range(N_KV_HEADS):
            rows = pl.ds(g, tm, stride=N_KV_HEADS)
            k_ref[rows, :] = of_ref[:, F_K + g * HEAD_DIM:F_K + (g + 1) * HEAD_DIM]
            v_ref[rows, :] = of_ref[:, F_V + g * HEAD_DIM:F_V + (g + 1) * HEAD_DIM]


def _proj(x2d, g, w):
    n = x2d.shape[0]
    tm, tn = PROJ_TM, PROJ_TN
    kv_spec = pl.BlockSpec((N_KV_HEADS * tm, HEAD_DIM), lambda i, j: (i, 0))
    kv_shape = jax.ShapeDtypeStruct((N_KV_HEADS * n, HEAD_DIM), F32)
    return pl.pallas_call(
        _proj_kernel,
        grid=(n // tm, D_H // tn),
        in_specs=[pl.BlockSpec((tm, D_MODEL), lambda i, j: (i, 0)),
                  pl.BlockSpec((1, D_MODEL), lambda i, j: (0, 0)),
                  pl.BlockSpec((tn, D_MODEL), lambda i, j: (j, 0))],
        out_specs=[pl.BlockSpec((tm, tn), lambda i, j: (i, jnp.minimum(j, KV_TILE - 1))),
                   pl.BlockSpec((tm, D_HF), lambda i, j: (i, 0)), kv_spec, kv_spec],
        out_shape=[jax.ShapeDtypeStruct((n, D_HB), BF16),
                   jax.ShapeDtypeStruct((n, D_HF), F32), kv_shape, kv_shape],
        scratch_shapes=[pltpu.VMEM((tm, D_MODEL), BF16)],
        compiler_params=_params(("parallel", "arbitrary")),
        name="proj",
    )(x2d, g, w)


def _bias_kernel(rb_ref, bp_ref, bs_ref, op_ref, os_ref):
    bp = bp_ref[...]
    bs = bs_ref[...]
    for h in range(N_HEADS):
        far = rb_ref[FAR_BUCKET, h]
        tp = jnp.zeros(bp.shape, F32)
        ts = jnp.zeros(bs.shape, F32)
        for b in range(N_BUCKETS):
            val = rb_ref[b, h] - far
            tp = jnp.where(bp == b, val * LOG2E, tp)
            ts = jnp.where(bs == b, val, ts)
        op_ref[h] = tp
        os_ref[h] = ts


def _bias_tables(rel_bias):
    key = np.arange(CHUNK)[:, None]
    qry = np.arange(Q_BLOCK)[None, :]
    bucket_p = _t5_bucket_static(CHUNK - Q_BLOCK + qry - key)
    qi = np.arange(DEC_SEQ)[:, None]
    col = np.arange(2 * LANE)[None, :]
    dist_s = np.where(col < LANE, LANE + qi - col, qi - (col - LANE))
    bucket_s = _t5_bucket_static(dist_s)
    return pl.pallas_call(
        _bias_kernel,
        in_specs=[pl.BlockSpec(memory_space=pltpu.SMEM),
                  pl.BlockSpec(memory_space=pltpu.VMEM),
                  pl.BlockSpec(memory_space=pltpu.VMEM)],
        out_specs=[pl.BlockSpec(memory_space=pltpu.VMEM),
                   pl.BlockSpec(memory_space=pltpu.VMEM)],
        out_shape=[jax.ShapeDtypeStruct((N_HEADS, CHUNK, Q_BLOCK), F32),
                   jax.ShapeDtypeStruct((N_HEADS, DEC_SEQ, 2 * LANE), F32)],
        name="bias_tables",
    )(rel_bias, jnp.asarray(bucket_p), jnp.asarray(bucket_s))


def _any(x):
    return jnp.max(jnp.where(x, 1.0, 0.0)) > 0.5


def _rep(x):
    return jnp.broadcast_to(x, (SUBLANE, LANE))


def _fold_rows(x, comb):
    parts = [x[k:k + SUBLANE] for k in range(0, x.shape[0], SUBLANE)]
    while len(parts) > 1:
        parts = [comb(parts[k], parts[k + 1]) for k in range(0, len(parts), 2)]
    return parts[0]


def _select_threshold(tile_fn, ntiles):
    def reduce_tiles(fn, init, comb):
        acc = init
        for j in range(ntiles):
            x = fn(tile_fn(j), j)
            parts = [x[k] for k in range(x.shape[0])]
            while len(parts) > 1:
                parts = [comb(parts[k], parts[k + 1]) for k in range(0, len(parts), 2)]
            acc = comb(acc, parts[0])
        return acc

    zeros = jnp.zeros((SUBLANE, LANE), F32)

    def count(pred_fn):
        acc = reduce_tiles(lambda s, j: jnp.where(pred_fn(s, j), 1.0, 0.0), zeros,
                           lambda a, b: a + b)
        return _rep(jnp.sum(acc, axis=0, keepdims=True))

    def masked_max(pred_fn):
        acc = reduce_tiles(lambda s, j: jnp.where(pred_fn(s, j), s, NEG_INF),
                           jnp.full((SUBLANE, LANE), NEG_INF, F32), jnp.maximum)
        return _rep(jnp.max(acc, axis=0, keepdims=True))

    bound = reduce_tiles(lambda s, j: jnp.where(s > NEG_INF, jnp.abs(s), 0.0), zeros,
                         jnp.maximum)
    bound = _rep(jnp.max(bound, axis=0, keepdims=True))

    def bisect(_, carry):
        lo, hi = carry
        mid = 0.5 * lo + 0.5 * hi
        few = count(lambda s, j: s > mid[None]) < TOPK
        return jnp.where(few, lo, mid), jnp.where(few, mid, hi)

    _, hi = lax.fori_loop(0, N_BISECT, bisect, (-bound, bound))

    thr = masked_max(lambda s, j: s <= hi[None])
    n_ge = count(lambda s, j: s >= thr[None])

    def fix_body(carry):
        thr, n_ge, _ = carry
        lower = masked_max(lambda s, j: s < thr[None])
        thr = jnp.where(n_ge < TOPK, lower, thr)
        n_ge = count(lambda s, j: s >= thr[None])
        return thr, n_ge, _any(n_ge < TOPK)

    thr, n_ge, _ = lax.while_loop(lambda c: c[2], fix_body, (thr, n_ge, _any(n_ge < TOPK)))
    n_gt = count(lambda s, j: s > thr[None])
    need = TOPK - n_gt
    return thr[0:1], need[0:1]


def _keep(pred):
    return jnp.where(pred, 0.0, NEG)


def _selection_masks(tiles, thr, need, seen, tri):
    ties = [s == thr for s in tiles]
    ranks = []
    for k in range(0, len(tiles), 2):
        pair = jnp.concatenate([jnp.where(t, 1.0, 0.0).astype(BF16) for t in ties[k:k + 2]], axis=1)
        rank = jnp.dot(tri, pair, preferred_element_type=F32)
        ranks += [rank[:, p * LANE:(p + 1) * LANE] for p in range(len(ties[k:k + 2]))]
    masks = []
    for s, tie, rank in zip(tiles, ties, ranks):
        masks.append(jnp.where(tie, _keep(rank + seen <= need), _keep(s > thr)))
        seen = seen + rank[s.shape[0] - 1:]
    return masks, seen


def _attn_p_kernel(q_ref, qi_ref, wi_ref, k_ref, v_ref, ki_ref, bias_ref, tri_ref, o_ref,
                   kb_ref, vt_ref, kib_ref, qh_ref, qih_ref,
                   score_ref, mask_ref, s_ref):
    i = pl.program_id(1)
    T = Q_BLOCK
    W = GROUP * T
    nch = i // CHUNK_BLOCKS + 1

    @pl.when(i == 0)
    def _():
        kb_ref[0:PAD_KEYS] = jnp.zeros((PAD_KEYS, D_KV), BF16)
        kb_ref[PAD_KEYS:] = k_ref[0].astype(BF16)
        kib_ref[0:PAD_KEYS] = jnp.zeros((PAD_KEYS, IDX_DIM), BF16)
        kib_ref[PAD_KEYS:] = ki_ref[0][:, :IDX_DIM].astype(BF16)
        ones_row = lax.broadcasted_iota(jnp.int32, (VT_ROWS - HEAD_DIM, PAD_KEYS + SEQ), 0) == 0
        for g in range(N_KV_HEADS):
            vt_ref[g, 0:HEAD_DIM, 0:PAD_KEYS] = jnp.zeros((HEAD_DIM, PAD_KEYS), BF16)
            vt_ref[g, HEAD_DIM:VT_ROWS, :] = jnp.where(ones_row, 1.0, 0.0).astype(BF16)
            for c in range(SEQ // LANE):
                blk = v_ref[0, c * LANE:(c + 1) * LANE, g * HEAD_DIM:(g + 1) * HEAD_DIM]
                vt_ref[g, 0:HEAD_DIM,
                       PAD_KEYS + c * LANE:PAD_KEYS + (c + 1) * LANE] = blk.T.astype(BF16)
        score_ref[0:PAD_KEYS] = jnp.full((PAD_KEYS, T), NEG_INF, F32)
        mask_ref[0:PAD_KEYS] = jnp.full((PAD_KEYS, T), NEG, F32)

    q = q_ref[0].astype(F32) * (QK_SCALE * LOG2E)
    for h in range(N_HEADS):
        qh_ref[h] = q[:, h * HEAD_DIM:(h + 1) * HEAD_DIM].astype(BF16)
    qi = qi_ref[0]
    for h in range(IDX_HEADS):
        qih_ref[h] = qi[:, h * IDX_DIM:(h + 1) * IDX_DIM].astype(BF16)
    w_rows = wi_ref[0].T[W_OFF:W_OFF + IDX_HEADS] * SCORE_SCALE

    def span(c):
        return pl.ds(pl.multiple_of((i - CHUNK_BLOCKS * c) * LANE, LANE), CHUNK)

    def first_key(c):
        return (i - CHUNK_BLOCKS * c) * LANE - PAD_KEYS

    key_l = lax.broadcasted_iota(jnp.int32, (CHUNK, T), 0)
    qry = i * T + lax.broadcasted_iota(jnp.int32, (CHUNK, T), 1)

    def score_chunk(c):
        kc = kib_ref[span(c), :]
        d = lax.dot_general(kc, qih_ref[...].reshape(IDX_HEADS * T, IDX_DIM),
                            (((1,), (1,)), ((), ())), preferred_element_type=F32)
        acc = jnp.zeros((CHUNK, T), F32)
        for h in range(IDX_HEADS):
            acc = acc + jnp.maximum(d[:, h * T:(h + 1) * T], 0.0) * w_rows[h:h + 1]
        key = first_key(c) + key_l
        acc = jnp.where(key <= qry, jnp.where(key >= 0, acc, NEG_INF), NEG_INF)
        score_ref[span(c), :] = acc

    nsub = CHUNK // SUBLANE

    def tile_fn(c):
        return score_ref[span(c), :].reshape(nsub, SUBLANE, T)

    def select_all():
        mask_ref[span(0), :] = _keep(score_ref[span(0), :] > NEG_INF)

    def select(n):
        thr, need = _select_threshold(tile_fn, n)
        parts = []
        for c in reversed(range(n)):
            start = pl.multiple_of((i - CHUNK_BLOCKS * c) * LANE, LANE)
            parts += [pl.ds(start + p * MXU_WIDTH, MXU_WIDTH) for p in range(CHUNK // MXU_WIDTH)]
        masks, _ = _selection_masks([score_ref[rows, :] for rows in parts], thr, need,
                                    jnp.zeros((1, T), F32), tri_ref[...])
        for rows, mask in zip(parts, masks):
            mask_ref[rows, :] = mask

    def logits(c, g):
        kc = kb_ref[span(c), g * HEAD_DIM:(g + 1) * HEAD_DIM]
        qg = qh_ref[g * GROUP:(g + 1) * GROUP].reshape(W, HEAD_DIM)
        s = lax.dot_general(kc, qg, (((1,), (1,)), ((), ())), preferred_element_type=F32)
        mb = mask_ref[span(c), :]
        if c == 0:
            add = jnp.concatenate([mb + bias_ref[g * GROUP + hq] for hq in range(GROUP)], axis=1)
        else:
            add = jnp.concatenate([mb] * GROUP, axis=1)
        s = s + add
        s_ref[span(c), g * W:(g + 1) * W] = s
        return jnp.max(_fold_rows(s, jnp.maximum), axis=0, keepdims=True)

    def run(n):
        for c in range(n):
            score_chunk(c)
        if n == 1:
            pl.when(i * T + T <= TOPK)(select_all)
            pl.when(i * T + T > TOPK)(functools.partial(select, n))
        else:
            select(n)
        for g in range(N_KV_HEADS):
            m = functools.reduce(jnp.maximum, [logits(c, g) for c in range(n)])
            acc = None
            for c in range(n):
                p = jnp.exp2((s_ref[span(c), g * W:(g + 1) * W] - m).astype(BF16))
                pv = jnp.dot(vt_ref[g, :, span(c)], p, preferred_element_type=F32)
                acc = pv if acc is None else acc + pv
            o = acc[0:HEAD_DIM] / acc[HEAD_DIM:HEAD_DIM + 1]
            for hq in range(GROUP):
                h = g * GROUP + hq
                o_ref[0, :, h * HEAD_DIM:(h + 1) * HEAD_DIM] = o[:, hq * T:(hq + 1) * T].T

    for n in range(1, SEQ // CHUNK + 1):
        pl.when(nch == n)(functools.partial(run, n))


def _attn_prompt(hb_p, hf_p, bias_p):
    hb3 = hb_p.reshape(BATCH, SEQ, D_HB)
    hf3 = hf_p.reshape(BATCH, SEQ, D_HF)
    nqb = SEQ // Q_BLOCK
    T = Q_BLOCK
    qi_w = IDX_HEADS * IDX_DIM
    return pl.pallas_call(
        _attn_p_kernel,
        grid=(BATCH, nqb),
        in_specs=[
            pl.BlockSpec((1, T, D_ATTN), lambda b, i: (b, i, C_Q // D_ATTN)),
            pl.BlockSpec((1, T, qi_w), lambda b, i: (b, i, F_QI // qi_w)),
            pl.BlockSpec((1, T, LANE), lambda b, i: (b, i, F_KI // LANE)),
            pl.BlockSpec((1, SEQ, D_KV), lambda b, i: (b, 0, F_K // D_KV)),
            pl.BlockSpec((1, SEQ, D_KV), lambda b, i: (b, 0, F_V // D_KV)),
            pl.BlockSpec((1, SEQ, LANE), lambda b, i: (b, 0, F_KI // LANE)),
            pl.BlockSpec((N_HEADS, CHUNK, T), lambda b, i: (0, 0, 0)),
            pl.BlockSpec((MXU_WIDTH, MXU_WIDTH), lambda b, i: (0, 0)),
        ],
        out_specs=pl.BlockSpec((1, T, D_ATTN), lambda b, i: (b, i, 0)),
        out_shape=jax.ShapeDtypeStruct((BATCH, SEQ, D_ATTN), F32),
        scratch_shapes=[
            pltpu.VMEM((PAD_KEYS + SEQ, D_KV), BF16),
            pltpu.VMEM((N_KV_HEADS, VT_ROWS, PAD_KEYS + SEQ), BF16),
            pltpu.VMEM((PAD_KEYS + SEQ, IDX_DIM), BF16),
            pltpu.VMEM((N_HEADS, T, HEAD_DIM), BF16),
            pltpu.VMEM((IDX_HEADS, T, IDX_DIM), BF16),
            pltpu.VMEM((PAD_KEYS + SEQ, T), F32),
            pltpu.VMEM((PAD_KEYS + SEQ, T), F32),
            pltpu.VMEM((PAD_KEYS + SEQ, N_HEADS * T), F32),
        ],
        compiler_params=_params(("parallel", "arbitrary")),
        name="attn_prompt",
    )(hb3, hf3, hf3, hf3, hf3, hf3, bias_p, jnp.tri(MXU_WIDTH, dtype=BF16))


def _score_s_kernel(pt_ref, qi_ref, kin_ref, *rest):
    npg = SCORE_ROWS * N_PAGES
    kip = rest[0:npg]
    o_ref = rest[npg]
    kinp_ref = rest[npg + 1]
    del pt_ref
    R = DEC_SEQ
    qrow = lax.broadcasted_iota(jnp.int32, (R, LANE), 0)
    lane = lax.broadcasted_iota(jnp.int32, (R, LANE), 1)
    kinp_ref[...] = jnp.zeros(kinp_ref.shape, BF16)
    for r in range(SCORE_ROWS):
        kinp_ref[r, 0:2 * R] = jnp.concatenate(
            [kin_ref[r][:, :IDX_DIM], jnp.zeros((R, IDX_DIM), F32)], 0).astype(BF16)
    for r in range(SCORE_ROWS):
        qi = jnp.concatenate([qi_ref[r][:, h * IDX_DIM:(h + 1) * IDX_DIM]
                              for h in range(IDX_HEADS)], axis=0).astype(BF16)
        w = kin_ref[r][:, W_OFF:W_OFF + IDX_HEADS] * SCORE_SCALE
        wb = jnp.broadcast_to(jnp.concatenate([w[:, h:h + 1] for h in range(IDX_HEADS)], axis=0),
                              (IDX_HEADS * R, LANE))
        for t in range(N_PAGES + 1):
            if t < N_PAGES:
                d = jnp.dot(qi, kip[r * N_PAGES + t][0].astype(BF16), preferred_element_type=F32)
            else:
                d = lax.dot_general(qi, kinp_ref[r], (((1,), (1,)), ((), ())),
                                    preferred_element_type=F32)
            e = (jnp.maximum(d, 0.0) * wb).reshape(IDX_HEADS, R, LANE)
            s = e[0]
            for h in range(1, IDX_HEADS):
                s = s + e[h]
            if t == N_PAGES:
                s = jnp.where(lane <= qrow, s, NEG_INF)
            o_ref[r, :, t * LANE:(t + 1) * LANE] = s


def _score_sample(hf_s, page_table, cache_kidx_t):
    R = DEC_SEQ
    G = SCORE_ROWS
    h3 = hf_s.reshape(DEC_BATCH, R, D_HF)
    qi_w = IDX_HEADS * IDX_DIM
    in_specs = [
        pl.BlockSpec((G, R, qi_w), lambda b, pt: (b, 0, F_QI // qi_w)),
        pl.BlockSpec((G, R, LANE), lambda b, pt: (b, 0, F_KI // LANE)),
    ]
    in_specs += [pl.BlockSpec((1, IDX_DIM, PAGE_SIZE),
                              lambda b, pt, r=r, p=p: (pt[b * G + r, p], 0, 0))
                 for r in range(G) for p in range(N_PAGES)]
    grid_spec = pltpu.PrefetchScalarGridSpec(
        num_scalar_prefetch=1,
        grid=(DEC_BATCH // G,),
        in_specs=in_specs,
        out_specs=pl.BlockSpec((G, R, L_SAMPLE), lambda b, pt: (b, 0, 0)),
        scratch_shapes=[pltpu.VMEM((G, PAGE_SIZE, IDX_DIM), BF16)],
    )
    return pl.pallas_call(
        _score_s_kernel,
        grid_spec=grid_spec,
        out_shape=jax.ShapeDtypeStruct((DEC_BATCH, R, L_SAMPLE), F32),
        compiler_params=_params(("arbitrary",)),
        name="score_sample",
    )(page_table, h3, h3, *([cache_kidx_t] * (G * N_PAGES)))


def _select_s_kernel(s_ref, tri_ref, o_ref):
    nt = N_PAGES + 1

    def tile_fn(j):
        return s_ref[j * LANE:(j + 1) * LANE, :].reshape(SUBTILES, SUBLANE, LANE)

    thr, need = _select_threshold(tile_fn, nt)
    parts = [slice(j * LANE, (j + 1) * LANE) for j in range(nt)]
    masks, _ = _selection_masks([s_ref[rows, :] for rows in parts], thr, need,
                                jnp.zeros((1, LANE), F32), tri_ref[...])
    for rows, mask in zip(parts, masks):
        o_ref[rows, :] = mask


def _select_sample(scores_t):
    n = scores_t.shape[1]
    return pl.pallas_call(
        _select_s_kernel,
        grid=(n // LANE,),
        in_specs=[pl.BlockSpec((L_SAMPLE, LANE), lambda c: (0, c)),
                  pl.BlockSpec((LANE, LANE), lambda c: (0, 0))],
        out_specs=pl.BlockSpec((L_SAMPLE, LANE), lambda c: (0, c)),
        out_shape=jax.ShapeDtypeStruct((L_SAMPLE, n), F32),
        compiler_params=_params(("parallel",)),
        name="select_sample",
    )(scores_t, jnp.tri(LANE, dtype=BF16))


def _attn_s_kernel(pt_ref, q_ref, kn_ref, vn_ref, mask_ref, bias_ref, *rest):
    npg = ATTN_ROWS * N_PAGES
    kp = rest[0:npg]
    vp = rest[npg:2 * npg]
    o_ref = rest[2 * npg]
    knp_ref, vnp_ref, logit_ref = rest[2 * npg + 1:]
    del pt_ref
    R = DEC_SEQ
    NT = N_PAGES + 1
    GR = GROUP * R

    knp_ref[...] = jnp.zeros(knp_ref.shape, BF16)
    vnp_ref[...] = jnp.zeros(vnp_ref.shape, BF16)
    for r in range(ATTN_ROWS):
        knp_ref[r, 0:2 * R] = jnp.concatenate([kn_ref[r], jnp.zeros((R, D_KV), F32)], 0).astype(BF16)
        vnp_ref[r, 0:2 * R] = jnp.concatenate([vn_ref[r], jnp.zeros((R, D_KV), F32)], 0).astype(BF16)

    def page_head(refs, pad_ref, r, t, g):
        if t < N_PAGES:
            return refs[r * N_PAGES + t][pl.ds(g, PAGE_SIZE, stride=N_KV_HEADS), :].astype(BF16)
        return pad_ref[r, :, g * HEAD_DIM:(g + 1) * HEAD_DIM]

    for r in range(ATTN_ROWS):
        qf = q_ref[r].astype(F32) * QK_SCALE
        q = jnp.concatenate([qf[:, h * HEAD_DIM:(h + 1) * HEAD_DIM] for h in range(N_HEADS)],
                            axis=0).astype(BF16)
        for t in range(NT):
            mb = mask_ref[r, :, t * LANE:(t + 1) * LANE]
            for g in range(N_KV_HEADS):
                lg = lax.dot_general(q[g * GR:(g + 1) * GR], page_head(kp, knp_ref, r, t, g),
                                     (((1,), (1,)), ((), ())), preferred_element_type=F32)
                lg = lg.reshape(GROUP, R, LANE) + mb[None]
                if t >= N_PAGES - 1:
                    off = (t - (N_PAGES - 1)) * LANE
                    lg = lg + bias_ref[g * GROUP:(g + 1) * GROUP, :, off:off + LANE]
                logit_ref[r, g * GR:(g + 1) * GR, t * LANE:(t + 1) * LANE] = lg.reshape(GR, LANE)

    for r in range(ATTN_ROWS):
        logits = logit_ref[r]
        m = jnp.max(logits, axis=1, keepdims=True)
        p = jnp.exp(logits - m)
        inv = 1.0 / jnp.sum(p, axis=1, keepdims=True)
        pb = p.astype(BF16)
        outs = [jnp.zeros((GR, HEAD_DIM), F32) for _ in range(N_KV_HEADS)]
        for t in range(NT):
            for g in range(N_KV_HEADS):
                outs[g] = outs[g] + jnp.dot(pb[g * GR:(g + 1) * GR, t * LANE:(t + 1) * LANE],
                                            page_head(vp, vnp_ref, r, t, g),
                                            preferred_element_type=F32)
        for g in range(N_KV_HEADS):
            o = outs[g] * inv[g * GR:(g + 1) * GR]
            for hq in range(GROUP):
                h = g * GROUP + hq
                o_ref[r, :, h * HEAD_DIM:(h + 1) * HEAD_DIM] = o[hq * R:(hq + 1) * R]


def _attn_sample(hb_s, hf_s, page_table, cache_k, cache_v, mask, bias_s):
    R = DEC_SEQ
    G = ATTN_ROWS
    h3 = hf_s.reshape(DEC_BATCH, R, D_HF)
    hb3 = hb_s.reshape(DEC_BATCH, R, D_HB)
    rows_per_page = PAGE_SIZE * N_KV_HEADS
    ck = cache_k.reshape(-1, HEAD_DIM)
    cv = cache_v.reshape(-1, HEAD_DIM)

    in_specs = [
        pl.BlockSpec((G, R, D_ATTN), lambda b, pt: (b, 0, C_Q // D_ATTN)),
        pl.BlockSpec((G, R, D_KV), lambda b, pt: (b, 0, F_K // D_KV)),
        pl.BlockSpec((G, R, D_KV), lambda b, pt: (b, 0, F_V // D_KV)),
        pl.BlockSpec((G, R, L_SAMPLE), lambda b, pt: (b, 0, 0)),
        pl.BlockSpec((N_HEADS, R, 2 * LANE), lambda b, pt: (0, 0, 0)),
    ]
    pages = [pl.BlockSpec((rows_per_page, HEAD_DIM), lambda b, pt, r=r, p=p: (pt[b * G + r, p], 0))
             for r in range(G) for p in range(N_PAGES)]
    in_specs += pages + pages
    grid_spec = pltpu.PrefetchScalarGridSpec(
        num_scalar_prefetch=1,
        grid=(DEC_BATCH // G,),
        in_specs=in_specs,
        out_specs=pl.BlockSpec((G, R, D_ATTN), lambda b, pt: (b, 0, 0)),
        scratch_shapes=[
            pltpu.VMEM((G, PAGE_SIZE, D_KV), BF16),
            pltpu.VMEM((G, PAGE_SIZE, D_KV), BF16),
            pltpu.VMEM((G, N_HEADS * R, L_SAMPLE), F32),
        ],
    )
    return pl.pallas_call(
        _attn_s_kernel,
        grid_spec=grid_spec,
        out_shape=jax.ShapeDtypeStruct((DEC_BATCH, R, D_ATTN), F32),
        compiler_params=_params(("arbitrary",)),
        name="attn_sample",
    )(page_table, hb3, h3, h3, mask, bias_s, *([ck] * (G * N_PAGES)), *([cv] * (G * N_PAGES)))


def _conv_p_kernel(val_ref, gate_ref, cw_ref, cb_ref, dw_ref, ut_ref, pad_ref):
    pad_ref[0:PAD_ROWS] = jnp.zeros((PAD_ROWS, LANE), F32)
    pad_ref[PAD_ROWS:] = val_ref[0].astype(F32) * jax.nn.sigmoid(gate_ref[0].astype(F32))
    ut_ref[0] = pad_ref[SEQ:SEQ + PAD_ROWS]
    cw = cw_ref[...]
    cb = cb_ref[...]
    first = PAD_ROWS - (CONV_WIDTH - 1)
    for c in range(SEQ // CONV_CHUNK):
        base = c * CONV_CHUNK
        acc = jnp.broadcast_to(cb, (CONV_CHUNK, LANE))
        for r in range(SUBLANE):
            taps = [w for w in range(CONV_WIDTH) if (first + w) % SUBLANE == r]
            span = max(first + w - r for w in taps) + CONV_CHUNK
            win = pad_ref[base + r:base + r + span]
            for w in taps:
                a = first + w - r
                acc = acc + win[a:a + CONV_CHUNK] * cw[w:w + 1]
        dw_ref[0, base:base + CONV_CHUNK] = acc


def _conv_prompt(hb_p, cw_pad, cb):
    h3 = hb_p.reshape(BATCH, SEQ, D_HB)
    nc = D_CONV // LANE
    return pl.pallas_call(
        _conv_p_kernel,
        grid=(BATCH, nc),
        in_specs=[pl.BlockSpec((1, SEQ, LANE), lambda b, c: (b, 0, C_GV // LANE + c)),
                  pl.BlockSpec((1, SEQ, LANE), lambda b, c: (b, 0, C_GG // LANE + c)),
                  pl.BlockSpec((PAD_ROWS, LANE), lambda b, c: (0, c)),
                  pl.BlockSpec((1, LANE), lambda b, c: (0, c))],
        out_specs=[pl.BlockSpec((1, SEQ, LANE), lambda b, c: (b, 0, c)),
                   pl.BlockSpec((1, PAD_ROWS, LANE), lambda b, c: (b, 0, c))],
        out_shape=[jax.ShapeDtypeStruct((BATCH, SEQ, D_CONV), F32),
                   jax.ShapeDtypeStruct((BATCH, PAD_ROWS, D_CONV), F32)],
        scratch_shapes=[pltpu.VMEM((PAD_ROWS + SEQ, LANE), F32)],
        compiler_params=_params(("parallel", "parallel")),
        name="conv_prompt",
    )(h3, h3, cw_pad, cb)


def _conv_s_kernel(val_ref, gate_ref, st_ref, cw_ref, cb_ref, dw_ref, ns_ref):
    R = DEC_SEQ
    H = CONV_WIDTH - 1
    cw = cw_ref[...]
    cb = jnp.broadcast_to(cb_ref[...], (DEC_BATCH, LANE))
    u = []
    for q in range(R):
        rows = pl.ds(q, DEC_BATCH, stride=R)
        u.append(val_ref[rows, :] * jax.nn.sigmoid(gate_ref[rows, :]))

    def row(r):
        return st_ref[r] if r < H else u[r - H]

    for q in range(R):
        acc = cb
        for w in range(CONV_WIDTH):
            acc = acc + row(q + w) * cw[w:w + 1]
        dw_ref[pl.ds(q, DEC_BATCH, stride=R), :] = acc
    for r in range(H):
        ns_ref[r] = row(r + R)


def _conv_sample(val, gate, state_t, cw_pad, cb):
    n_s = DEC_BATCH * DEC_SEQ
    H = CONV_WIDTH - 1
    nc = D_CONV // LANE
    return pl.pallas_call(
        _conv_s_kernel,
        grid=(nc,),
        in_specs=[pl.BlockSpec((n_s, LANE), lambda c: (0, c)),
                  pl.BlockSpec((n_s, LANE), lambda c: (0, c)),
                  pl.BlockSpec((H, DEC_BATCH, LANE), lambda c: (0, 0, c)),
                  pl.BlockSpec((PAD_ROWS, LANE), lambda c: (0, c)),
                  pl.BlockSpec((1, LANE), lambda c: (0, c))],
        out_specs=[pl.BlockSpec((n_s, LANE), lambda c: (0, c)),
                   pl.BlockSpec((H, DEC_BATCH, LANE), lambda c: (0, 0, c))],
        out_shape=[jax.ShapeDtypeStruct((n_s, D_CONV), F32),
                   jax.ShapeDtypeStruct((H, DEC_BATCH, D_CONV), F32)],
        compiler_params=_params(("parallel",)),
        name="conv_sample",
    )(val, gate, state_t, cw_pad, cb)


def _tail_kernel(attn_ref, za_ref, dw_ref, zc_ref, ga_ref, gc_ref, x_ref,
                 wua_ref, wpw_ref, wuc_ref, wo_ref, ng_ref, nb_ref, bpw_ref, fg_ref, y_ref):
    a = attn_ref[...] * jax.nn.silu(za_ref[...].astype(F32))
    branch_attn = jnp.dot(a.astype(BF16), wua_ref[...], preferred_element_type=F32)

    dw = dw_ref[...]
    mu = jnp.mean(dw, axis=-1, keepdims=True)
    var = jnp.mean(jnp.square(dw - mu), axis=-1, keepdims=True)
    ln = (dw - mu) * lax.rsqrt(var + EPS) * ng_ref[...] + nb_ref[...]
    conv_out = jnp.dot(jax.nn.silu(ln).astype(BF16), wpw_ref[...],
                       preferred_element_type=F32) + bpw_ref[...]
    c = conv_out * jax.nn.silu(zc_ref[...].astype(F32))
    branch_conv = jnp.dot(c.astype(BF16), wuc_ref[...], preferred_element_type=F32)

    merged = (jax.nn.sigmoid(ga_ref[...].astype(F32)) * branch_attn
              + jax.nn.sigmoid(gc_ref[...].astype(F32)) * branch_conv)
    y = x_ref[...] + jnp.dot(merged.astype(BF16), wo_ref[...], preferred_element_type=F32)
    ms = jnp.mean(y * y, axis=-1, keepdims=True)
    y_ref[...] = y * lax.rsqrt(ms + EPS) * fg_ref[...]


def _tail(h, attn, dw, x2d, weights, tm):
    n = x2d.shape[0]

    def const(shape):
        return pl.BlockSpec(shape, lambda i: (0, 0), pipeline_mode=pl.Buffered(1))

    def cols(width, offset):
        return pl.BlockSpec((tm, width), lambda i: (i, offset // width))

    return pl.pallas_call(
        _tail_kernel,
        grid=(n // tm,),
        in_specs=[cols(D_ATTN, 0), cols(D_ATTN, C_ZA), cols(D_CONV, 0), cols(D_CONV, C_ZC),
                  cols(D_MODEL, C_GA), cols(D_MODEL, C_GC), cols(D_MODEL, 0),
                  const((D_ATTN, D_MODEL)), const((D_CONV, D_CONV)),
                  const((D_CONV, D_MODEL)), const((D_MODEL, D_MODEL)),
                  const((1, D_CONV)), const((1, D_CONV)), const((1, D_CONV)),
                  const((1, D_MODEL))],
        out_specs=pl.BlockSpec((tm, D_MODEL), lambda i: (i, 0)),
        out_shape=jax.ShapeDtypeStruct((n, D_MODEL), F32),
        compiler_params=_params(("parallel",)),
        name="tail",
    )(attn, h, dw, h, h, h, x2d, *weights)


def _reorder_kernel(off_ref, w_ref, o_ref):
    del off_ref
    o_ref[...] = w_ref[...].astype(BF16)


def _reorder_w_in(w_in):
    w_t = w_in.T
    src = []
    for first, dest, width in (('gate_attn', C_GA, 2 * D_MODEL), ('q', C_Q, D_ATTN),
                               ('z_attn', C_ZA, D_ATTN), ('glu_val', C_GV, 3 * D_CONV),
                               ('k', C_K, 2 * D_KV), ('q_idx', C_QI, D_H - C_QI)):
        assert dest == len(src) * REORDER_ROWS
        for r in range(0, width, REORDER_ROWS):
            src.append(min(_SRC[first][0] + r, D_IN - REORDER_ROWS) // SUBLANE)
    grid_spec = pltpu.PrefetchScalarGridSpec(
        num_scalar_prefetch=1,
        grid=(len(src),),
        in_specs=[pl.BlockSpec((pl.Element(REORDER_ROWS), pl.Element(D_MODEL)),
                               lambda d, off: (off[d] * SUBLANE, 0))],
        out_specs=pl.BlockSpec((REORDER_ROWS, D_MODEL), lambda d, off: (d, 0)),
    )
    return pl.pallas_call(
        _reorder_kernel,
        grid_spec=grid_spec,
        out_shape=jax.ShapeDtypeStruct((D_H, D_MODEL), BF16),
        compiler_params=_params(("arbitrary",)),
        name="reorder_w_in",
    )(jnp.asarray(src, jnp.int32), w_t)


def kernel(x_prompt, x_sample, cache_k, cache_v, cache_kidx, state_conv, page_table,
           ln_g, w_in, conv_w, conv_b, conv_norm_g, conv_norm_b, w_pw, b_pw,
           w_up_attn, w_up_conv, w_out, rel_bias, final_g):
    w_all = _reorder_w_in(w_in[0])
    g_in = ln_g[0].reshape(1, D_MODEL)
    xp = x_prompt.reshape(BATCH * SEQ, D_MODEL)
    xs = x_sample.reshape(DEC_BATCH * DEC_SEQ, D_MODEL)
    hb_p, hf_p, k_p, v_p = _proj(xp, g_in, w_all)
    hb_s, hf_s, k_s, v_s = _proj(xs, g_in, w_all)

    bias_p, bias_s = _bias_tables(rel_bias)

    attn_p = _attn_prompt(hb_p, hf_p, bias_p).reshape(BATCH * SEQ, D_ATTN)

    n_s = DEC_BATCH * DEC_SEQ
    scores = _score_sample(hf_s, page_table, cache_kidx[0].transpose(0, 2, 1))
    mask_t = _select_sample(scores.reshape(n_s, L_SAMPLE).T)
    mask = mask_t.T.reshape(DEC_BATCH, DEC_SEQ, L_SAMPLE)
    attn_s = _attn_sample(hb_s, hf_s, page_table, cache_k[0], cache_v[0], mask, bias_s)
    attn_s = attn_s.reshape(n_s, D_ATTN)

    cw_pad = jnp.concatenate([conv_w[0], jnp.zeros((PAD_ROWS - CONV_WIDTH, D_CONV), F32)], 0)
    cb = conv_b[0].reshape(1, D_CONV)
    dw_p, u_tail = _conv_prompt(hb_p, cw_pad, cb)
    dw_s, state_new = _conv_sample(hb_s[:, C_GV:C_GV + D_CONV].astype(F32),
                                   hb_s[:, C_GG:C_GG + D_CONV].astype(F32),
                                   state_conv[0].transpose(1, 0, 2), cw_pad, cb)

    wua = w_up_attn[0].astype(BF16)
    wpw = w_pw[0].astype(BF16)
    wuc = w_up_conv[0].astype(BF16)
    wo = w_out[0].astype(BF16)
    ng = conv_norm_g[0].reshape(1, D_CONV)
    nb = conv_norm_b[0].reshape(1, D_CONV)
    bpw = b_pw[0].reshape(1, D_CONV)
    fg = final_g.reshape(1, D_MODEL)
    weights = (wua, wpw, wuc, wo, ng, nb, bpw, fg)
    y_p = _tail(hb_p, attn_p, dw_p.reshape(BATCH * SEQ, D_CONV), xp, weights, TAIL_TM)
    y_s = _tail(hb_s, attn_s, dw_s, xs, weights, TAIL_TM)

    tail_rows = CONV_WIDTH - 1
    return (
        y_p.reshape(BATCH, SEQ, D_MODEL),
        y_s.reshape(DEC_BATCH, DEC_SEQ, D_MODEL),
        k_p.reshape(1, BATCH, SEQ, N_KV_HEADS, HEAD_DIM),
        v_p.reshape(1, BATCH, SEQ, N_KV_HEADS, HEAD_DIM),
        hf_p[:, F_KI:F_KI + IDX_DIM].reshape(1, BATCH, SEQ, IDX_DIM),
        u_tail[:, PAD_ROWS - tail_rows:].reshape(1, BATCH, tail_rows, D_CONV),
        k_s.reshape(1, DEC_BATCH, DEC_SEQ, N_KV_HEADS, HEAD_DIM),
        v_s.reshape(1, DEC_BATCH, DEC_SEQ, N_KV_HEADS, HEAD_DIM),
        hf_s[:, F_KI:F_KI + IDX_DIM].reshape(1, DEC_BATCH, DEC_SEQ, IDX_DIM),
        state_new.transpose(1, 0, 2).reshape(1, DEC_BATCH, tail_rows, D_CONV),
    )
```

```python
import functools
import math

import numpy as np
import jax
import jax.numpy as jnp
from jax import lax
from jax.experimental import pallas as pl
from jax.experimental.pallas import tpu as pltpu

F32 = jnp.float32
BF16 = jnp.bfloat16

D_MODEL = 2048
BATCH = 8
SEQ = 2048
DEC_BATCH = 128
DEC_SEQ = 8
PAST_LEN = 2048
PAGE_SIZE = 128
N_PAGES = PAST_LEN // PAGE_SIZE
N_HEADS = 8
N_KV_HEADS = 2
HEAD_DIM = 128
GROUP = N_HEADS // N_KV_HEADS
D_ATTN = N_HEADS * HEAD_DIM
D_KV = N_KV_HEADS * HEAD_DIM
IDX_HEADS = 8
IDX_DIM = 64
TOPK = 256
Q_BLOCK = 128
N_BUCKETS = 32
MAX_DISTANCE = 128
D_CONV = D_MODEL // 2
CONV_WIDTH = 31
EPS = 1e-6
NEG = -1e30
NEG_INF = float("-inf")

LANE = 128
SUBLANE = 8
VMEM_LIMIT = 56 * 1024 * 1024

C_GA = 0
C_GC = C_GA + D_MODEL
C_Q = C_GC + D_MODEL
C_ZA = C_Q + D_ATTN
C_GV = C_ZA + D_ATTN
C_GG = C_GV + D_CONV
C_ZC = C_GG + D_CONV
C_K = C_ZC + D_CONV
C_V = C_K + D_KV
C_QI = C_V + D_KV
C_KI = C_QI + IDX_HEADS * IDX_DIM
W_OFF = IDX_DIM
MXU_WIDTH = 256
PROJ_TM = 1024
PROJ_TN = 6 * MXU_WIDTH
D_H = -(-(C_KI + LANE) // PROJ_TN) * PROJ_TN
KV_TILE = C_K // PROJ_TN
assert C_K % PROJ_TN == 0 and D_H == C_K + PROJ_TN
D_HB = C_K
D_HF = C_KI + LANE - C_K
F_K = C_K - D_HB
F_V = C_V - D_HB
F_QI = C_QI - D_HB
F_KI = C_KI - D_HB

_SRC = {}
_off = 0
for _name, _w in (('q', D_ATTN), ('k', D_KV), ('v', D_KV), ('z_attn', D_ATTN),
                  ('q_idx', IDX_HEADS * IDX_DIM), ('k_idx', IDX_DIM), ('w_idx', IDX_HEADS),
                  ('glu_val', D_CONV), ('glu_gate', D_CONV), ('z_conv', D_CONV),
                  ('gate_attn', D_MODEL), ('gate_conv', D_MODEL)):
    _SRC[_name] = (_off, _w)
    _off += _w
D_IN = _off
REORDER_ROWS = 512

N_BISECT = 20
PAD_ROWS = 32
CONV_CHUNK = 64
SCORE_SCALE = (IDX_DIM ** -0.5) * (IDX_HEADS ** -0.5)
QK_SCALE = HEAD_DIM ** -0.5
LOG2E = math.log2(math.e)
L_SAMPLE = (N_PAGES + 1) * LANE
SUBTILES = LANE // SUBLANE
CHUNK_BLOCKS = 2
CHUNK = CHUNK_BLOCKS * Q_BLOCK
PAD_KEYS = CHUNK - Q_BLOCK
VT_ROWS = HEAD_DIM + 2 * SUBLANE
SCORE_ROWS = 4
ATTN_ROWS = 4
TAIL_TM = 256


def _t5_bucket_static(dist):
    n = np.maximum(dist, 0)
    max_exact = N_BUCKETS // 2
    ratio = (np.log(np.maximum(n, 1).astype(np.float32) / np.float32(max_exact))
             / np.float32(math.log(MAX_DISTANCE / max_exact)))
    large = np.minimum(max_exact + (ratio * np.float32(N_BUCKETS - max_exact)).astype(np.int32),
                       N_BUCKETS - 1)
    return np.where(n < max_exact, n, large).astype(np.int32)


FAR_BUCKET = int(_t5_bucket_static(np.array([2 * MAX_DISTANCE]))[0])


def _params(sem):
    return pltpu.CompilerParams(dimension_semantics=sem, vmem_limit_bytes=VMEM_LIMIT)


def _proj_kernel(x_ref, g_ref, w_ref, ob_ref, of_ref, k_ref, v_ref, xn_ref):
    tm = x_ref.shape[0]
    j = pl.program_id(1)

    @pl.when(j == 0)
    def _():
        x = x_ref[...]
        ms = jnp.mean(x * x, axis=-1, keepdims=True)
        xn_ref[...] = (x * lax.rsqrt(ms + EPS) * g_ref[...]).astype(BF16)

    def project():
        return lax.dot_general(xn_ref[...], w_ref[...], (((1,), (1,)), ((), ())),
                               preferred_element_type=F32)

    @pl.when(j < KV_TILE)
    def _():
        ob_ref[...] = project().astype(BF16)

    @pl.when(j == KV_TILE)
    def _():
        of_ref[...] = project()[:, :D_HF]
        for g in range(N_KV_HEADS):
            rows = pl.ds(g, tm, stride=N_KV_HEADS)
            k_ref[rows, :] = of_ref[:, F_K + g * HEAD_DIM:F_K + (g + 1) * HEAD_DIM]
            v_ref[rows, :] = of_ref[:, F_V + g * HEAD_DIM:F_V + (g + 1) * HEAD_DIM]


def _proj(x2d, g, w):
    n = x2d.shape[0]
    tm, tn = PROJ_TM, PROJ_TN
    kv_spec = pl.BlockSpec((N_KV_HEADS * tm, HEAD_DIM), lambda i, j: (i, 0))
    kv_shape = jax.ShapeDtypeStruct((N_KV_HEADS * n, HEAD_DIM), F32)
    return pl.pallas_call(
        _proj_kernel,
        grid=(n // tm, D_H // tn),
        in_specs=[pl.BlockSpec((tm, D_MODEL), lambda i, j: (i, 0)),
                  pl.BlockSpec((1, D_MODEL), lambda i, j: (0, 0)),
                  pl.BlockSpec((tn, D_MODEL), lambda i, j: (j, 0))],
        out_specs=[pl.BlockSpec((tm, tn), lambda i, j: (i, jnp.minimum(j, KV_TILE - 1))),
                   pl.BlockSpec((tm, D_HF), lambda i, j: (i, 0)), kv_spec, kv_spec],
        out_shape=[jax.ShapeDtypeStruct((n, D_HB), BF16),
                   jax.ShapeDtypeStruct((n, D_HF), F32), kv_shape, kv_shape],
        scratch_shapes=[pltpu.VMEM((tm, D_MODEL), BF16)],
        compiler_params=_params(("parallel", "arbitrary")),
        name="proj",
    )(x2d, g, w)


def _bias_kernel(rb_ref, bp_ref, bs_ref, op_ref, os_ref):
    bp = bp_ref[...]
    bs = bs_ref[...]
    for h in range(N_HEADS):
        far = rb_ref[FAR_BUCKET, h]
        tp = jnp.zeros(bp.shape, F32)
        ts = jnp.zeros(bs.shape, F32)
        for b in range(N_BUCKETS):
            val = rb_ref[b, h] - far
            tp = jnp.where(bp == b, val * LOG2E, tp)
            ts = jnp.where(bs == b, val, ts)
        op_ref[h] = tp
        os_ref[h] = ts


def _bias_tables(rel_bias):
    key = np.arange(CHUNK)[:, None]
    qry = np.arange(Q_BLOCK)[None, :]
    bucket_p = _t5_bucket_static(CHUNK - Q_BLOCK + qry - key)
    qi = np.arange(DEC_SEQ)[:, None]
    col = np.arange(2 * LANE)[None, :]
    dist_s = np.where(col < LANE, LANE + qi - col, qi - (col - LANE))
    bucket_s = _t5_bucket_static(dist_s)
    return pl.pallas_call(
        _bias_kernel,
        in_specs=[pl.BlockSpec(memory_space=pltpu.SMEM),
                  pl.BlockSpec(memory_space=pltpu.VMEM),
                  pl.BlockSpec(memory_space=pltpu.VMEM)],
        out_specs=[pl.BlockSpec(memory_space=pltpu.VMEM),
                   pl.BlockSpec(memory_space=pltpu.VMEM)],
        out_shape=[jax.ShapeDtypeStruct((N_HEADS, CHUNK, Q_BLOCK), F32),
                   jax.ShapeDtypeStruct((N_HEADS, DEC_SEQ, 2 * LANE), F32)],
        name="bias_tables",
    )(rel_bias, jnp.asarray(bucket_p), jnp.asarray(bucket_s))


def _any(x):
    return jnp.max(jnp.where(x, 1.0, 0.0)) > 0.5


def _rep(x):
    return jnp.broadcast_to(x, (SUBLANE, LANE))


def _fold_rows(x, comb):
    parts = [x[k:k + SUBLANE] for k in range(0, x.shape[0], SUBLANE)]
    while len(parts) > 1:
        parts = [comb(parts[k], parts[k + 1]) for k in range(0, len(parts), 2)]
    return parts[0]


def _select_threshold(tile_fn, ntiles):
    def reduce_tiles(fn, init, comb):
        acc = init
        for j in range(ntiles):
            x = fn(tile_fn(j), j)
            parts = [x[k] for k in range(x.shape[0])]
            while len(parts) > 1:
                parts = [comb(parts[k], parts[k + 1]) for k in range(0, len(parts), 2)]
            acc = comb(acc, parts[0])
        return acc

    zeros = jnp.zeros((SUBLANE, LANE), F32)

    def count(pred_fn):
        acc = reduce_tiles(lambda s, j: jnp.where(pred_fn(s, j), 1.0, 0.0), zeros,
                           lambda a, b: a + b)
        return _rep(jnp.sum(acc, axis=0, keepdims=True))

    def masked_max(pred_fn):
        acc = reduce_tiles(lambda s, j: jnp.where(pred_fn(s, j), s, NEG_INF),
                           jnp.full((SUBLANE, LANE), NEG_INF, F32), jnp.maximum)
        return _rep(jnp.max(acc, axis=0, keepdims=True))

    bound = reduce_tiles(lambda s, j: jnp.where(s > NEG_INF, jnp.abs(s), 0.0), zeros,
                         jnp.maximum)
    bound = _rep(jnp.max(bound, axis=0, keepdims=True))

    def bisect(_, carry):
        lo, hi = carry
        mid = 0.5 * lo + 0.5 * hi
        few = count(lambda s, j: s > mid[None]) < TOPK
        return jnp.where(few, lo, mid), jnp.where(few, mid, hi)

    _, hi = lax.fori_loop(0, N_BISECT, bisect, (-bound, bound))

    thr = masked_max(lambda s, j: s <= hi[None])
    n_ge = count(lambda s, j: s >= thr[None])

    def fix_body(carry):
        thr, n_ge, _ = carry
        lower = masked_max(lambda s, j: s < thr[None])
        thr = jnp.where(n_ge < TOPK, lower, thr)
        n_ge = count(lambda s, j: s >= thr[None])
        return thr, n_ge, _any(n_ge < TOPK)

    thr, n_ge, _ = lax.while_loop(lambda c: c[2], fix_body, (thr, n_ge, _any(n_ge < TOPK)))
    n_gt = count(lambda s, j: s > thr[None])
    need = TOPK - n_gt
    return thr[0:1], need[0:1]


def _keep(pred):
    return jnp.where(pred, 0.0, NEG)


def _selection_masks(tiles, thr, need, seen, tri):
    ties = [s == thr for s in tiles]
    ranks = []
    for k in range(0, len(tiles), 2):
        pair = jnp.concatenate([jnp.where(t, 1.0, 0.0).astype(BF16) for t in ties[k:k + 2]], axis=1)
        rank = jnp.dot(tri, pair, preferred_element_type=F32)
        ranks += [rank[:, p * LANE:(p + 1) * LANE] for p in range(len(ties[k:k + 2]))]
    masks = []
    for s, tie, rank in zip(tiles, ties, ranks):
        masks.append(jnp.where(tie, _keep(rank + seen <= need), _keep(s > thr)))
        seen = seen + rank[s.shape[0] - 1:]
    return masks, seen


def _attn_p_kernel(q_ref, qi_ref, wi_ref, k_ref, v_ref, ki_ref, bias_ref, tri_ref, o_ref,
                   kb_ref, vt_ref, kib_ref, qh_ref, qih_ref,
                   score_ref, mask_ref, s_ref):
    i = pl.program_id(1)
    T = Q_BLOCK
    W = GROUP * T
    nch = i // CHUNK_BLOCKS + 1

    @pl.when(i == 0)
    def _():
        kb_ref[0:PAD_KEYS] = jnp.zeros((PAD_KEYS, D_KV), BF16)
        kb_ref[PAD_KEYS:] = k_ref[0].astype(BF16)
        kib_ref[0:PAD_KEYS] = jnp.zeros((PAD_KEYS, IDX_DIM), BF16)
        kib_ref[PAD_KEYS:] = ki_ref[0][:, :IDX_DIM].astype(BF16)
        ones_row = lax.broadcasted_iota(jnp.int32, (VT_ROWS - HEAD_DIM, PAD_KEYS + SEQ), 0) == 0
        for g in range(N_KV_HEADS):
            vt_ref[g, 0:HEAD_DIM, 0:PAD_KEYS] = jnp.zeros((HEAD_DIM, PAD_KEYS), BF16)
            vt_ref[g, HEAD_DIM:VT_ROWS, :] = jnp.where(ones_row, 1.0, 0.0).astype(BF16)
            for c in range(SEQ // LANE):
                blk = v_ref[0, c * LANE:(c + 1) * LANE, g * HEAD_DIM:(g + 1) * HEAD_DIM]
                vt_ref[g, 0:HEAD_DIM,
                       PAD_KEYS + c * LANE:PAD_KEYS + (c + 1) * LANE] = blk.T.astype(BF16)
        score_ref[0:PAD_KEYS] = jnp.full((PAD_KEYS, T), NEG_INF, F32)
        mask_ref[0:PAD_KEYS] = jnp.full((PAD_KEYS, T), NEG, F32)

    q = q_ref[0].astype(F32) * (QK_SCALE * LOG2E)
    for h in range(N_HEADS):
        qh_ref[h] = q[:, h * HEAD_DIM:(h + 1) * HEAD_DIM].astype(BF16)
    qi = qi_ref[0]
    for h in range(IDX_HEADS):
        qih_ref[h] = qi[:, h * IDX_DIM:(h + 1) * IDX_DIM].astype(BF16)
    w_rows = wi_ref[0].T[W_OFF:W_OFF + IDX_HEADS] * SCORE_SCALE

    def span(c):
        return pl.ds(pl.multiple_of((i - CHUNK_BLOCKS * c) * LANE, LANE), CHUNK)

    def first_key(c):
        return (i - CHUNK_BLOCKS * c) * LANE - PAD_KEYS

    key_l = lax.broadcasted_iota(jnp.int32, (CHUNK, T), 0)
    qry = i * T + lax.broadcasted_iota(jnp.int32, (CHUNK, T), 1)

    def score_chunk(c):
        kc = kib_ref[span(c), :]
        d = lax.dot_general(kc, qih_ref[...].reshape(IDX_HEADS * T, IDX_DIM),
                            (((1,), (1,)), ((), ())), preferred_element_type=F32)
        acc = jnp.zeros((CHUNK, T), F32)
        for h in range(IDX_HEADS):
            acc = acc + jnp.maximum(d[:, h * T:(h + 1) * T], 0.0) * w_rows[h:h + 1]
        key = first_key(c) + key_l
        acc = jnp.where(key <= qry, jnp.where(key >= 0, acc, NEG_INF), NEG_INF)
        score_ref[span(c), :] = acc

    nsub = CHUNK // SUBLANE

    def tile_fn(c):
        return score_ref[span(c), :].reshape(nsub, SUBLANE, T)

    def select_all():
        mask_ref[span(0), :] = _keep(score_ref[span(0), :] > NEG_INF)

    def select(n):
        thr, need = _select_threshold(tile_fn, n)
        parts = []
        for c in reversed(range(n)):
            start = pl.multiple_of((i - CHUNK_BLOCKS * c) * LANE, LANE)
            parts += [pl.ds(start + p * MXU_WIDTH, MXU_WIDTH) for p in range(CHUNK // MXU_WIDTH)]
        masks, _ = _selection_masks([score_ref[rows, :] for rows in parts], thr, need,
                                    jnp.zeros((1, T), F32), tri_ref[...])
        for rows, mask in zip(parts, masks):
            mask_ref[rows, :] = mask

    def logits(c, g):
        kc = kb_ref[span(c), g * HEAD_DIM:(g + 1) * HEAD_DIM]
        qg = qh_ref[g * GROUP:(g + 1) * GROUP].reshape(W, HEAD_DIM)
        s = lax.dot_general(kc, qg, (((1,), (1,)), ((), ())), preferred_element_type=F32)
        mb = mask_ref[span(c), :]
        if c == 0:
            add = jnp.concatenate([mb + bias_ref[g * GROUP + hq] for hq in range(GROUP)], axis=1)
        else:
            add = jnp.concatenate([mb] * GROUP, axis=1)
        s = s + add
        s_ref[span(c), g * W:(g + 1) * W] = s
        return jnp.max(_fold_rows(s, jnp.maximum), axis=0, keepdims=True)

    def run(n):
        for c in range(n):
            score_chunk(c)
        if n == 1:
            pl.when(i * T + T <= TOPK)(select_all)
            pl.when(i * T + T > TOPK)(functools.partial(select, n))
        else:
            select(n)
        for g in range(N_KV_HEADS):
            m = functools.reduce(jnp.maximum, [logits(c, g) for c in range(n)])
            acc = None
            for c in range(n):
                p = jnp.exp2((s_ref[span(c), g * W:(g + 1) * W] - m).astype(BF16))
                pv = jnp.dot(vt_ref[g, :, span(c)], p, preferred_element_type=F32)
                acc = pv if acc is None else acc + pv
            o = acc[0:HEAD_DIM] / acc[HEAD_DIM:HEAD_DIM + 1]
            for hq in range(GROUP):
                h = g * GROUP + hq
                o_ref[0, :, h * HEAD_DIM:(h + 1) * HEAD_DIM] = o[:, hq * T:(hq + 1) * T].T

    for n in range(1, SEQ // CHUNK + 1):
        pl.when(nch == n)(functools.partial(run, n))


def _attn_prompt(hb_p, hf_p, bias_p):
    hb3 = hb_p.reshape(BATCH, SEQ, D_HB)
    hf3 = hf_p.reshape(BATCH, SEQ, D_HF)
    nqb = SEQ // Q_BLOCK
    T = Q_BLOCK
    qi_w = IDX_HEADS * IDX_DIM
    return pl.pallas_call(
        _attn_p_kernel,
        grid=(BATCH, nqb),
        in_specs=[
            pl.BlockSpec((1, T, D_ATTN), lambda b, i: (b, i, C_Q // D_ATTN)),
            pl.BlockSpec((1, T, qi_w), lambda b, i: (b, i, F_QI // qi_w)),
            pl.BlockSpec((1, T, LANE), lambda b, i: (b, i, F_KI // LANE)),
            pl.BlockSpec((1, SEQ, D_KV), lambda b, i: (b, 0, F_K // D_KV)),
            pl.BlockSpec((1, SEQ, D_KV), lambda b, i: (b, 0, F_V // D_KV)),
            pl.BlockSpec((1, SEQ, LANE), lambda b, i: (b, 0, F_KI // LANE)),
            pl.BlockSpec((N_HEADS, CHUNK, T), lambda b, i: (0, 0, 0)),
            pl.BlockSpec((MXU_WIDTH, MXU_WIDTH), lambda b, i: (0, 0)),
        ],
        out_specs=pl.BlockSpec((1, T, D_ATTN), lambda b, i: (b, i, 0)),
        out_shape=jax.ShapeDtypeStruct((BATCH, SEQ, D_ATTN), F32),
        scratch_shapes=[
            pltpu.VMEM((PAD_KEYS + SEQ, D_KV), BF16),
            pltpu.VMEM((N_KV_HEADS, VT_ROWS, PAD_KEYS + SEQ), BF16),
            pltpu.VMEM((PAD_KEYS + SEQ, IDX_DIM), BF16),
            pltpu.VMEM((N_HEADS, T, HEAD_DIM), BF16),
            pltpu.VMEM((IDX_HEADS, T, IDX_DIM), BF16),
            pltpu.VMEM((PAD_KEYS + SEQ, T), F32),
            pltpu.VMEM((PAD_KEYS + SEQ, T), F32),
            pltpu.VMEM((PAD_KEYS + SEQ, N_HEADS * T), F32),
        ],
        compiler_params=_params(("parallel", "arbitrary")),
        name="attn_prompt",
    )(hb3, hf3, hf3, hf3, hf3, hf3, bias_p, jnp.tri(MXU_WIDTH, dtype=BF16))


def _score_s_kernel(pt_ref, qi_ref, kin_ref, *rest):
    npg = SCORE_ROWS * N_PAGES
    kip = rest[0:npg]
    o_ref = rest[npg]
    kinp_ref = rest[npg + 1]
    del pt_ref
    R = DEC_SEQ
    qrow = lax.broadcasted_iota(jnp.int32, (R, LANE), 0)
    lane = lax.broadcasted_iota(jnp.int32, (R, LANE), 1)
    kinp_ref[...] = jnp.zeros(kinp_ref.shape, BF16)
    for r in range(SCORE_ROWS):
        kinp_ref[r, 0:2 * R] = jnp.concatenate(
            [kin_ref[r][:, :IDX_DIM], jnp.zeros((R, IDX_DIM), F32)], 0).astype(BF16)
    for r in range(SCORE_ROWS):
        qi = jnp.concatenate([qi_ref[r][:, h * IDX_DIM:(h + 1) * IDX_DIM]
                              for h in range(IDX_HEADS)], axis=0).astype(BF16)
        w = kin_ref[r][:, W_OFF:W_OFF + IDX_HEADS] * SCORE_SCALE
        wb = jnp.broadcast_to(jnp.concatenate([w[:, h:h + 1] for h in range(IDX_HEADS)], axis=0),
                              (IDX_HEADS * R, LANE))
        for t in range(N_PAGES + 1):
            if t < N_PAGES:
                d = jnp.dot(qi, kip[r * N_PAGES + t][0].astype(BF16), preferred_element_type=F32)
            else:
                d = lax.dot_general(qi, kinp_ref[r], (((1,), (1,)), ((), ())),
                                    preferred_element_type=F32)
            e = (jnp.maximum(d, 0.0) * wb).reshape(IDX_HEADS, R, LANE)
            s = e[0]
            for h in range(1, IDX_HEADS):
                s = s + e[h]
            if t == N_PAGES:
                s = jnp.where(lane <= qrow, s, NEG_INF)
            o_ref[r, :, t * LANE:(t + 1) * LANE] = s


def _score_sample(hf_s, page_table, cache_kidx_t):
    R = DEC_SEQ
    G = SCORE_ROWS
    h3 = hf_s.reshape(DEC_BATCH, R, D_HF)
    qi_w = IDX_HEADS * IDX_DIM
    in_specs = [
        pl.BlockSpec((G, R, qi_w), lambda b, pt: (b, 0, F_QI // qi_w)),
        pl.BlockSpec((G, R, LANE), lambda b, pt: (b, 0, F_KI // LANE)),
    ]
    in_specs += [pl.BlockSpec((1, IDX_DIM, PAGE_SIZE),
                              lambda b, pt, r=r, p=p: (pt[b * G + r, p], 0, 0))
                 for r in range(G) for p in range(N_PAGES)]
    grid_spec = pltpu.PrefetchScalarGridSpec(
        num_scalar_prefetch=1,
        grid=(DEC_BATCH // G,),
        in_specs=in_specs,
        out_specs=pl.BlockSpec((G, R, L_SAMPLE), lambda b, pt: (b, 0, 0)),
        scratch_shapes=[pltpu.VMEM((G, PAGE_SIZE, IDX_DIM), BF16)],
    )
    return pl.pallas_call(
        _score_s_kernel,
        grid_spec=grid_spec,
        out_shape=jax.ShapeDtypeStruct((DEC_BATCH, R, L_SAMPLE), F32),
        compiler_params=_params(("arbitrary",)),
        name="score_sample",
    )(page_table, h3, h3, *([cache_kidx_t] * (G * N_PAGES)))


def _select_s_kernel(s_ref, tri_ref, o_ref, st_ref):
    nt = N_PAGES + 1
    parts = [slice(j * LANE, (j + 1) * LANE) for j in range(nt)]
    for cols in parts:
        st_ref[cols, :] = s_ref[:, cols].T

    def tile_fn(j):
        return st_ref[parts[j], :].reshape(SUBTILES, SUBLANE, LANE)

    thr, need = _select_threshold(tile_fn, nt)
    masks, _ = _selection_masks([st_ref[rows, :] for rows in parts], thr, need,
                                jnp.zeros((1, LANE), F32), tri_ref[...])
    for cols, mask in zip(parts, masks):
        o_ref[:, cols] = mask.T


def _select_sample(scores):
    n = scores.shape[0]
    return pl.pallas_call(
        _select_s_kernel,
        grid=(n // LANE,),
        in_specs=[pl.BlockSpec((LANE, L_SAMPLE), lambda c: (c, 0)),
                  pl.BlockSpec((LANE, LANE), lambda c: (0, 0))],
        out_specs=pl.BlockSpec((LANE, L_SAMPLE), lambda c: (c, 0)),
        out_shape=jax.ShapeDtypeStruct((n, L_SAMPLE), F32),
        scratch_shapes=[pltpu.VMEM((L_SAMPLE, LANE), F32)],
        compiler_params=_params(("parallel",)),
        name="select_sample",
    )(scores, jnp.tri(LANE, dtype=BF16))


def _attn_s_kernel(pt_ref, q_ref, kn_ref, vn_ref, mask_ref, bias_ref, *rest):
    npg = ATTN_ROWS * N_PAGES
    kp = rest[0:npg]
    vp = rest[npg:2 * npg]
    o_ref = rest[2 * npg]
    knp_ref, vnp_ref, logit_ref = rest[2 * npg + 1:]
    del pt_ref
    R = DEC_SEQ
    NT = N_PAGES + 1
    GR = GROUP * R

    knp_ref[...] = jnp.zeros(knp_ref.shape, BF16)
    vnp_ref[...] = jnp.zeros(vnp_ref.shape, BF16)
    for r in range(ATTN_ROWS):
        knp_ref[r, 0:2 * R] = jnp.concatenate([kn_ref[r], jnp.zeros((R, D_KV), F32)], 0).astype(BF16)
        vnp_ref[r, 0:2 * R] = jnp.concatenate([vn_ref[r], jnp.zeros((R, D_KV), F32)], 0).astype(BF16)

    def page_head(refs, pad_ref, r, t, g):
        if t < N_PAGES:
            return refs[r * N_PAGES + t][pl.ds(g, PAGE_SIZE, stride=N_KV_HEADS), :].astype(BF16)
        return pad_ref[r, :, g * HEAD_DIM:(g + 1) * HEAD_DIM]

    for r in range(ATTN_ROWS):
        qf = q_ref[r].astype(F32) * QK_SCALE
        q = jnp.concatenate([qf[:, h * HEAD_DIM:(h + 1) * HEAD_DIM] for h in range(N_HEADS)],
                            axis=0).astype(BF16)
        for t in range(NT):
            mb = mask_ref[r, :, t * LANE:(t + 1) * LANE]
            for g in range(N_KV_HEADS):
                lg = lax.dot_general(q[g * GR:(g + 1) * GR], page_head(kp, knp_ref, r, t, g),
                                     (((1,), (1,)), ((), ())), preferred_element_type=F32)
                lg = lg.reshape(GROUP, R, LANE) + mb[None]
                if t >= N_PAGES - 1:
                    off = (t - (N_PAGES - 1)) * LANE
                    lg = lg + bias_ref[g * GROUP:(g + 1) * GROUP, :, off:off + LANE]
                logit_ref[r, g * GR:(g + 1) * GR, t * LANE:(t + 1) * LANE] = lg.reshape(GR, LANE)

    for r in range(ATTN_ROWS):
        logits = logit_ref[r]
        m = jnp.max(logits, axis=1, keepdims=True)
        p = jnp.exp(logits - m)
        inv = 1.0 / jnp.sum(p, axis=1, keepdims=True)
        pb = p.astype(BF16)
        outs = [jnp.zeros((GR, HEAD_DIM), F32) for _ in range(N_KV_HEADS)]
        for t in range(NT):
            for g in range(N_KV_HEADS):
                outs[g] = outs[g] + jnp.dot(pb[g * GR:(g + 1) * GR, t * LANE:(t + 1) * LANE],
                                            page_head(vp, vnp_ref, r, t, g),
                                            preferred_element_type=F32)
        for g in range(N_KV_HEADS):
            o = outs[g] * inv[g * GR:(g + 1) * GR]
            for hq in range(GROUP):
                h = g * GROUP + hq
                o_ref[r, :, h * HEAD_DIM:(h + 1) * HEAD_DIM] = o[hq * R:(hq + 1) * R]


def _attn_sample(hb_s, hf_s, page_table, cache_k, cache_v, mask, bias_s):
    R = DEC_SEQ
    G = ATTN_ROWS
    h3 = hf_s.reshape(DEC_BATCH, R, D_HF)
    hb3 = hb_s.reshape(DEC_BATCH, R, D_HB)
    rows_per_page = PAGE_SIZE * N_KV_HEADS
    ck = cache_k.reshape(-1, HEAD_DIM)
    cv = cache_v.reshape(-1, HEAD_DIM)

    in_specs = [
        pl.BlockSpec((G, R, D_ATTN), lambda b, pt: (b, 0, C_Q // D_ATTN)),
        pl.BlockSpec((G, R, D_KV), lambda b, pt: (b, 0, F_K // D_KV)),
        pl.BlockSpec((G, R, D_KV), lambda b, pt: (b, 0, F_V // D_KV)),
        pl.BlockSpec((G, R, L_SAMPLE), lambda b, pt: (b, 0, 0)),
        pl.BlockSpec((N_HEADS, R, 2 * LANE), lambda b, pt: (0, 0, 0)),
    ]
    pages = [pl.BlockSpec((rows_per_page, HEAD_DIM), lambda b, pt, r=r, p=p: (pt[b * G + r, p], 0))
             for r in range(G) for p in range(N_PAGES)]
    in_specs += pages + pages
    grid_spec = pltpu.PrefetchScalarGridSpec(
        num_scalar_prefetch=1,
        grid=(DEC_BATCH // G,),
        in_specs=in_specs,
        out_specs=pl.BlockSpec((G, R, D_ATTN), lambda b, pt: (b, 0, 0)),
        scratch_shapes=[
            pltpu.VMEM((G, PAGE_SIZE, D_KV), BF16),
            pltpu.VMEM((G, PAGE_SIZE, D_KV), BF16),
            pltpu.VMEM((G, N_HEADS * R, L_SAMPLE), F32),
        ],
    )
    return pl.pallas_call(
        _attn_s_kernel,
        grid_spec=grid_spec,
        out_shape=jax.ShapeDtypeStruct((DEC_BATCH, R, D_ATTN), F32),
        compiler_params=_params(("arbitrary",)),
        name="attn_sample",
    )(page_table, hb3, h3, h3, mask, bias_s, *([ck] * (G * N_PAGES)), *([cv] * (G * N_PAGES)))


def _conv_p_kernel(val_ref, gate_ref, cw_ref, cb_ref, dw_ref, ut_ref, pad_ref):
    pad_ref[0:PAD_ROWS] = jnp.zeros((PAD_ROWS, LANE), F32)
    pad_ref[PAD_ROWS:] = val_ref[0].astype(F32) * jax.nn.sigmoid(gate_ref[0].astype(F32))
    ut_ref[0] = pad_ref[SEQ:SEQ + PAD_ROWS]
    cw = cw_ref[...]
    cb = cb_ref[...]
    first = PAD_ROWS - (CONV_WIDTH - 1)
    for c in range(SEQ // CONV_CHUNK):
        base = c * CONV_CHUNK
        acc = jnp.broadcast_to(cb, (CONV_CHUNK, LANE))
        for r in range(SUBLANE):
            taps = [w for w in range(CONV_WIDTH) if (first + w) % SUBLANE == r]
            span = max(first + w - r for w in taps) + CONV_CHUNK
            win = pad_ref[base + r:base + r + span]
            for w in taps:
                a = first + w - r
                acc = acc + win[a:a + CONV_CHUNK] * cw[w:w + 1]
        dw_ref[0, base:base + CONV_CHUNK] = acc


def _conv_prompt(hb_p, cw_pad, cb):
    h3 = hb_p.reshape(BATCH, SEQ, D_HB)
    nc = D_CONV // LANE
    return pl.pallas_call(
        _conv_p_kernel,
        grid=(BATCH, nc),
        in_specs=[pl.BlockSpec((1, SEQ, LANE), lambda b, c: (b, 0, C_GV // LANE + c)),
                  pl.BlockSpec((1, SEQ, LANE), lambda b, c: (b, 0, C_GG // LANE + c)),
                  pl.BlockSpec((PAD_ROWS, LANE), lambda b, c: (0, c)),
                  pl.BlockSpec((1, LANE), lambda b, c: (0, c))],
        out_specs=[pl.BlockSpec((1, SEQ, LANE), lambda b, c: (b, 0, c)),
                   pl.BlockSpec((1, PAD_ROWS, LANE), lambda b, c: (b, 0, c))],
        out_shape=[jax.ShapeDtypeStruct((BATCH, SEQ, D_CONV), F32),
                   jax.ShapeDtypeStruct((BATCH, PAD_ROWS, D_CONV), F32)],
        scratch_shapes=[pltpu.VMEM((PAD_ROWS + SEQ, LANE), F32)],
        compiler_params=_params(("parallel", "parallel")),
        name="conv_prompt",
    )(h3, h3, cw_pad, cb)


def _conv_s_kernel(val_ref, gate_ref, st_ref, cw_ref, cb_ref, dw_ref, ns_ref, glu_ref):
    R = DEC_SEQ
    H = CONV_WIDTH - 1
    cw = cw_ref[...]
    cb = jnp.broadcast_to(cb_ref[...], (DEC_BATCH, LANE))
    glu_ref[...] = val_ref[...].astype(F32) * jax.nn.sigmoid(gate_ref[...].astype(F32))
    u = []
    for q in range(R):
        u.append(glu_ref[pl.ds(q, DEC_BATCH, stride=R), :])

    def row(r):
        return st_ref[r] if r < H else u[r - H]

    for q in range(R):
        acc = cb
        for w in range(CONV_WIDTH):
            acc = acc + row(q + w) * cw[w:w + 1]
        dw_ref[pl.ds(q, DEC_BATCH, stride=R), :] = acc
    for r in range(H):
        ns_ref[r] = row(r + R)


def _conv_sample(hb_s, state_t, cw_pad, cb):
    n_s = DEC_BATCH * DEC_SEQ
    H = CONV_WIDTH - 1
    nc = D_CONV // LANE
    return pl.pallas_call(
        _conv_s_kernel,
        grid=(nc,),
        in_specs=[pl.BlockSpec((n_s, LANE), lambda c: (0, C_GV // LANE + c)),
                  pl.BlockSpec((n_s, LANE), lambda c: (0, C_GG // LANE + c)),
                  pl.BlockSpec((H, DEC_BATCH, LANE), lambda c: (0, 0, c)),
                  pl.BlockSpec((PAD_ROWS, LANE), lambda c: (0, c)),
                  pl.BlockSpec((1, LANE), lambda c: (0, c))],
        out_specs=[pl.BlockSpec((n_s, LANE), lambda c: (0, c)),
                   pl.BlockSpec((H, DEC_BATCH, LANE), lambda c: (0, 0, c))],
        out_shape=[jax.ShapeDtypeStruct((n_s, D_CONV), F32),
                   jax.ShapeDtypeStruct((H, DEC_BATCH, D_CONV), F32)],
        scratch_shapes=[pltpu.VMEM((n_s, LANE), F32)],
        compiler_params=_params(("parallel",)),
        name="conv_sample",
    )(hb_s, hb_s, state_t, cw_pad, cb)


def _tail_kernel(attn_ref, za_ref, dw_ref, zc_ref, ga_ref, gc_ref, x_ref,
                 wua_ref, wpw_ref, wuc_ref, wo_ref, ng_ref, nb_ref, bpw_ref, fg_ref, y_ref):
    a = attn_ref[...] * jax.nn.silu(za_ref[...].astype(F32))
    branch_attn = jnp.dot(a.astype(BF16), wua_ref[...], preferred_element_type=F32)

    dw = dw_ref[...]
    mu = jnp.mean(dw, axis=-1, keepdims=True)
    var = jnp.mean(jnp.square(dw - mu), axis=-1, keepdims=True)
    ln = (dw - mu) * lax.rsqrt(var + EPS) * ng_ref[...] + nb_ref[...]
    conv_out = jnp.dot(jax.nn.silu(ln).astype(BF16), wpw_ref[...],
                       preferred_element_type=F32) + bpw_ref[...]
    c = conv_out * jax.nn.silu(zc_ref[...].astype(F32))
    branch_conv = jnp.dot(c.astype(BF16), wuc_ref[...], preferred_element_type=F32)

    merged = (jax.nn.sigmoid(ga_ref[...].astype(F32)) * branch_attn
              + jax.nn.sigmoid(gc_ref[...].astype(F32)) * branch_conv)
    y = x_ref[...] + jnp.dot(merged.astype(BF16), wo_ref[...], preferred_element_type=F32)
    ms = jnp.mean(y * y, axis=-1, keepdims=True)
    y_ref[...] = y * lax.rsqrt(ms + EPS) * fg_ref[...]


def _tail(h, attn, dw, x2d, weights, tm):
    n = x2d.shape[0]

    def const(shape):
        return pl.BlockSpec(shape, lambda i: (0, 0), pipeline_mode=pl.Buffered(1))

    def cols(width, offset):
        return pl.BlockSpec((tm, width), lambda i: (i, offset // width))

    return pl.pallas_call(
        _tail_kernel,
        grid=(n // tm,),
        in_specs=[cols(D_ATTN, 0), cols(D_ATTN, C_ZA), cols(D_CONV, 0), cols(D_CONV, C_ZC),
                  cols(D_MODEL, C_GA), cols(D_MODEL, C_GC), cols(D_MODEL, 0),
                  const((D_ATTN, D_MODEL)), const((D_CONV, D_CONV)),
                  const((D_CONV, D_MODEL)), const((D_MODEL, D_MODEL)),
                  const((1, D_CONV)), const((1, D_CONV)), const((1, D_CONV)),
                  const((1, D_MODEL))],
        out_specs=pl.BlockSpec((tm, D_MODEL), lambda i: (i, 0)),
        out_shape=jax.ShapeDtypeStruct((n, D_MODEL), F32),
        compiler_params=_params(("parallel",)),
        name="tail",
    )(attn, h, dw, h, h, h, x2d, *weights)


def _reorder_kernel(off_ref, w_ref, o_ref):
    del off_ref
    o_ref[...] = w_ref[...].astype(BF16)


def _reorder_w_in(w_in):
    w_t = w_in.T
    src = []
    for first, dest, width in (('gate_attn', C_GA, 2 * D_MODEL), ('q', C_Q, D_ATTN),
                               ('z_attn', C_ZA, D_ATTN), ('glu_val', C_GV, 3 * D_CONV),
                               ('k', C_K, 2 * D_KV), ('q_idx', C_QI, D_H - C_QI)):
        assert dest == len(src) * REORDER_ROWS
        for r in range(0, width, REORDER_ROWS):
            src.append(min(_SRC[first][0] + r, D_IN - REORDER_ROWS) // SUBLANE)
    grid_spec = pltpu.PrefetchScalarGridSpec(
        num_scalar_prefetch=1,
        grid=(len(src),),
        in_specs=[pl.BlockSpec((pl.Element(REORDER_ROWS), pl.Element(D_MODEL)),
                               lambda d, off: (off[d] * SUBLANE, 0))],
        out_specs=pl.BlockSpec((REORDER_ROWS, D_MODEL), lambda d, off: (d, 0)),
    )
    return pl.pallas_call(
        _reorder_kernel,
        grid_spec=grid_spec,
        out_shape=jax.ShapeDtypeStruct((D_H, D_MODEL), BF16),
        compiler_params=_params(("arbitrary",)),
        name="reorder_w_in",
    )(jnp.asarray(src, jnp.int32), w_t)


def kernel(x_prompt, x_sample, cache_k, cache_v, cache_kidx, state_conv, page_table,
           ln_g, w_in, conv_w, conv_b, conv_norm_g, conv_norm_b, w_pw, b_pw,
           w_up_attn, w_up_conv, w_out, rel_bias, final_g):
    w_all = _reorder_w_in(w_in[0])
    g_in = ln_g[0].reshape(1, D_MODEL)
    xp = x_prompt.reshape(BATCH * SEQ, D_MODEL)
    xs = x_sample.reshape(DEC_BATCH * DEC_SEQ, D_MODEL)
    hb_p, hf_p, k_p, v_p = _proj(xp, g_in, w_all)
    hb_s, hf_s, k_s, v_s = _proj(xs, g_in, w_all)

    bias_p, bias_s = _bias_tables(rel_bias)

    attn_p = _attn_prompt(hb_p, hf_p, bias_p).reshape(BATCH * SEQ, D_ATTN)

    n_s = DEC_BATCH * DEC_SEQ
    scores = _score_sample(hf_s, page_table, cache_kidx[0].transpose(0, 2, 1))
    mask = _select_sample(scores.reshape(n_s, L_SAMPLE)).reshape(DEC_BATCH, DEC_SEQ, L_SAMPLE)
    attn_s = _attn_sample(hb_s, hf_s, page_table, cache_k[0], cache_v[0], mask, bias_s)
    attn_s = attn_s.reshape(n_s, D_ATTN)

    cw_pad = jnp.concatenate([conv_w[0], jnp.zeros((PAD_ROWS - CONV_WIDTH, D_CONV), F32)], 0)
    cb = conv_b[0].reshape(1, D_CONV)
    dw_p, u_tail = _conv_prompt(hb_p, cw_pad, cb)
    dw_s, state_new = _conv_sample(hb_s, state_conv[0].transpose(1, 0, 2), cw_pad, cb)

    wua = w_up_attn[0].astype(BF16)
    wpw = w_pw[0].astype(BF16)
    wuc = w_up_conv[0].astype(BF16)
    wo = w_out[0].astype(BF16)
    ng = conv_norm_g[0].reshape(1, D_CONV)
    nb = conv_norm_b[0].reshape(1, D_CONV)
    bpw = b_pw[0].reshape(1, D_CONV)
    fg = final_g.reshape(1, D_MODEL)
    weights = (wua, wpw, wuc, wo, ng, nb, bpw, fg)
    y_p = _tail(hb_p, attn_p, dw_p.reshape(BATCH * SEQ, D_CONV), xp, weights, TAIL_TM)
    y_s = _tail(hb_s, attn_s, dw_s, xs, weights, TAIL_TM)

    tail_rows = CONV_WIDTH - 1
    return (
        y_p.reshape(BATCH, SEQ, D_MODEL),
        y_s.reshape(DEC_BATCH, DEC_SEQ, D_MODEL),
        k_p.reshape(1, BATCH, SEQ, N_KV_HEADS, HEAD_DIM),
        v_p.reshape(1, BATCH, SEQ, N_KV_HEADS, HEAD_DIM),
        hf_p[:, F_KI:F_KI + IDX_DIM].reshape(1, BATCH, SEQ, IDX_DIM),
        u_tail[:, PAD_ROWS - tail_rows:].reshape(1, BATCH, tail_rows, D_CONV),
        k_s.reshape(1, DEC_BATCH, DEC_SEQ, N_KV_HEADS, HEAD_DIM),
        v_s.reshape(1, DEC_BATCH, DEC_SEQ, N_KV_HEADS, HEAD_DIM),
        hf_s[:, F_KI:F_KI + IDX_DIM].reshape(1, DEC_BATCH, DEC_SEQ, IDX_DIM),
        state_new.transpose(1, 0, 2).reshape(1, DEC_BATCH, tail_rows, D_CONV),
    )
```

```python
import functools
import math

import numpy as np
import jax
import jax.numpy as jnp
from jax import lax
from jax.experimental import pallas as pl
from jax.experimental.pallas import tpu as pltpu

F32 = jnp.float32
BF16 = jnp.bfloat16

D_MODEL = 2048
BATCH = 8
SEQ = 2048
DEC_BATCH = 128
DEC_SEQ = 8
PAST_LEN = 2048
PAGE_SIZE = 128
N_PAGES = PAST_LEN // PAGE_SIZE
N_HEADS = 8
N_KV_HEADS = 2
HEAD_DIM = 128
GROUP = N_HEADS // N_KV_HEADS
D_ATTN = N_HEADS * HEAD_DIM
D_KV = N_KV_HEADS * HEAD_DIM
IDX_HEADS = 8
IDX_DIM = 64
TOPK = 256
Q_BLOCK = 128
N_BUCKETS = 32
MAX_DISTANCE = 128
D_CONV = D_MODEL // 2
CONV_WIDTH = 31
EPS = 1e-6
NEG = -1e30
NEG_INF = float("-inf")

LANE = 128
SUBLANE = 8
VMEM_LIMIT = 56 * 1024 * 1024

C_GA = 0
C_GC = C_GA + D_MODEL
C_Q = C_GC + D_MODEL
C_ZA = C_Q + D_ATTN
C_GV = C_ZA + D_ATTN
C_GG = C_GV + D_CONV
C_ZC = C_GG + D_CONV
C_K = C_ZC + D_CONV
C_V = C_K + D_KV
C_QI = C_V + D_KV
C_KI = C_QI + IDX_HEADS * IDX_DIM
W_OFF = IDX_DIM
MXU_WIDTH = 256
PROJ_TM = 1024
PROJ_TN = 6 * MXU_WIDTH
D_H = -(-(C_KI + LANE) // PROJ_TN) * PROJ_TN
KV_TILE = C_K // PROJ_TN
assert C_K % PROJ_TN == 0 and D_H == C_K + PROJ_TN
D_HB = C_K
D_HF = C_KI + LANE - C_K
F_K = C_K - D_HB
F_V = C_V - D_HB
F_QI = C_QI - D_HB
F_KI = C_KI - D_HB

_SRC = {}
_off = 0
for _name, _w in (('q', D_ATTN), ('k', D_KV), ('v', D_KV), ('z_attn', D_ATTN),
                  ('q_idx', IDX_HEADS * IDX_DIM), ('k_idx', IDX_DIM), ('w_idx', IDX_HEADS),
                  ('glu_val', D_CONV), ('glu_gate', D_CONV), ('z_conv', D_CONV),
                  ('gate_attn', D_MODEL), ('gate_conv', D_MODEL)):
    _SRC[_name] = (_off, _w)
    _off += _w
D_IN = _off
REORDER_ROWS = 512

N_BISECT = 20
PAD_ROWS = 32
CONV_CHUNK = 64
SCORE_SCALE = (IDX_DIM ** -0.5) * (IDX_HEADS ** -0.5)
QK_SCALE = HEAD_DIM ** -0.5
LOG2E = math.log2(math.e)
L_SAMPLE = (N_PAGES + 1) * LANE
SUBTILES = LANE // SUBLANE
CHUNK_BLOCKS = 2
CHUNK = CHUNK_BLOCKS * Q_BLOCK
PAD_KEYS = CHUNK - Q_BLOCK
VT_ROWS = HEAD_DIM + 2 * SUBLANE
SCORE_ROWS = 4
ATTN_ROWS = 4
TAIL_TM = 256


def _t5_bucket_static(dist):
    n = np.maximum(dist, 0)
    max_exact = N_BUCKETS // 2
    ratio = (np.log(np.maximum(n, 1).astype(np.float32) / np.float32(max_exact))
             / np.float32(math.log(MAX_DISTANCE / max_exact)))
    large = np.minimum(max_exact + (ratio * np.float32(N_BUCKETS - max_exact)).astype(np.int32),
                       N_BUCKETS - 1)
    return np.where(n < max_exact, n, large).astype(np.int32)


FAR_BUCKET = int(_t5_bucket_static(np.array([2 * MAX_DISTANCE]))[0])


def _params(sem):
    return pltpu.CompilerParams(dimension_semantics=sem, vmem_limit_bytes=VMEM_LIMIT)


def _proj_kernel(x_ref, g_ref, w_ref, ob_ref, of_ref, k_ref, v_ref, kit_ref, xn_ref):
    tm = x_ref.shape[0]
    j = pl.program_id(1)

    @pl.when(j == 0)
    def _():
        x = x_ref[...]
        ms = jnp.mean(x * x, axis=-1, keepdims=True)
        xn_ref[...] = (x * lax.rsqrt(ms + EPS) * g_ref[...]).astype(BF16)

    def project():
        return lax.dot_general(xn_ref[...], w_ref[...], (((1,), (1,)), ((), ())),
                               preferred_element_type=F32)

    @pl.when(j < KV_TILE)
    def _():
        ob_ref[...] = project().astype(BF16)

    @pl.when(j == KV_TILE)
    def _():
        of_ref[...] = project()[:, :D_HF]
        for g in range(N_KV_HEADS):
            rows = pl.ds(g, tm, stride=N_KV_HEADS)
            k_ref[rows, :] = of_ref[:, F_K + g * HEAD_DIM:F_K + (g + 1) * HEAD_DIM]
            v_ref[rows, :] = of_ref[:, F_V + g * HEAD_DIM:F_V + (g + 1) * HEAD_DIM]
        for c in range(tm // LANE):
            tile = of_ref[c * LANE:(c + 1) * LANE, F_KI:F_KI + LANE].T
            kit_ref[0, :, c * LANE:(c + 1) * LANE] = tile[0:IDX_DIM]


def _proj(x2d, g, w, seq):
    n = x2d.shape[0]
    tm, tn = PROJ_TM, PROJ_TN
    tiles = seq // tm
    kv_spec = pl.BlockSpec((N_KV_HEADS * tm, HEAD_DIM), lambda i, j: (i, 0))
    kv_shape = jax.ShapeDtypeStruct((N_KV_HEADS * n, HEAD_DIM), F32)
    return pl.pallas_call(
        _proj_kernel,
        grid=(n // tm, D_H // tn),
        in_specs=[pl.BlockSpec((tm, D_MODEL), lambda i, j: (i, 0)),
                  pl.BlockSpec((1, D_MODEL), lambda i, j: (0, 0)),
                  pl.BlockSpec((tn, D_MODEL), lambda i, j: (j, 0))],
        out_specs=[pl.BlockSpec((tm, tn), lambda i, j: (i, jnp.minimum(j, KV_TILE - 1))),
                   pl.BlockSpec((tm, D_HF), lambda i, j: (i, 0)), kv_spec, kv_spec,
                   pl.BlockSpec((1, IDX_DIM, tm), lambda i, j: (i // tiles, 0, i % tiles))],
        out_shape=[jax.ShapeDtypeStruct((n, D_HB), BF16),
                   jax.ShapeDtypeStruct((n, D_HF), F32), kv_shape, kv_shape,
                   jax.ShapeDtypeStruct((n // seq, IDX_DIM, seq), F32)],
        scratch_shapes=[pltpu.VMEM((tm, D_MODEL), BF16)],
        compiler_params=_params(("parallel", "arbitrary")),
        name="proj",
    )(x2d, g, w)


def _bias_kernel(rb_ref, bp_ref, bs_ref, op_ref, os_ref):
    bp = bp_ref[...]
    bs = bs_ref[...]
    for h in range(N_HEADS):
        far = rb_ref[FAR_BUCKET, h]
        tp = jnp.zeros(bp.shape, F32)
        ts = jnp.zeros(bs.shape, F32)
        for b in range(N_BUCKETS):
            val = rb_ref[b, h] - far
            tp = jnp.where(bp == b, val * LOG2E, tp)
            ts = jnp.where(bs == b, val, ts)
        op_ref[h] = tp
        os_ref[h] = ts


def _bias_tables(rel_bias):
    key = np.arange(CHUNK)[:, None]
    qry = np.arange(Q_BLOCK)[None, :]
    bucket_p = _t5_bucket_static(CHUNK - Q_BLOCK + qry - key)
    qi = np.arange(DEC_SEQ)[:, None]
    col = np.arange(2 * LANE)[None, :]
    dist_s = np.where(col < LANE, LANE + qi - col, qi - (col - LANE))
    bucket_s = _t5_bucket_static(dist_s)
    return pl.pallas_call(
        _bias_kernel,
        in_specs=[pl.BlockSpec(memory_space=pltpu.SMEM),
                  pl.BlockSpec(memory_space=pltpu.VMEM),
                  pl.BlockSpec(memory_space=pltpu.VMEM)],
        out_specs=[pl.BlockSpec(memory_space=pltpu.VMEM),
                   pl.BlockSpec(memory_space=pltpu.VMEM)],
        out_shape=[jax.ShapeDtypeStruct((N_HEADS, CHUNK, Q_BLOCK), F32),
                   jax.ShapeDtypeStruct((N_HEADS, DEC_SEQ, 2 * LANE), F32)],
        name="bias_tables",
    )(rel_bias, jnp.asarray(bucket_p), jnp.asarray(bucket_s))


def _any(x):
    return jnp.max(jnp.where(x, 1.0, 0.0)) > 0.5


def _rep(x):
    return jnp.broadcast_to(x, (SUBLANE, LANE))


def _fold_rows(x, comb):
    parts = [x[k:k + SUBLANE] for k in range(0, x.shape[0], SUBLANE)]
    while len(parts) > 1:
        parts = [comb(parts[k], parts[k + 1]) for k in range(0, len(parts), 2)]
    return parts[0]


def _select_threshold(tile_fn, ntiles):
    def reduce_tiles(fn, init, comb):
        acc = init
        for j in range(ntiles):
            x = fn(tile_fn(j), j)
            parts = [x[k] for k in range(x.shape[0])]
            while len(parts) > 1:
                parts = [comb(parts[k], parts[k + 1]) for k in range(0, len(parts), 2)]
            acc = comb(acc, parts[0])
        return acc

    zeros = jnp.zeros((SUBLANE, LANE), F32)

    def count(pred_fn):
        acc = reduce_tiles(lambda s, j: jnp.where(pred_fn(s, j), 1.0, 0.0), zeros,
                           lambda a, b: a + b)
        return _rep(jnp.sum(acc, axis=0, keepdims=True))

    def masked_max(pred_fn):
        acc = reduce_tiles(lambda s, j: jnp.where(pred_fn(s, j), s, NEG_INF),
                           jnp.full((SUBLANE, LANE), NEG_INF, F32), jnp.maximum)
        return _rep(jnp.max(acc, axis=0, keepdims=True))

    bound = reduce_tiles(lambda s, j: jnp.where(s > NEG_INF, jnp.abs(s), 0.0), zeros,
                         jnp.maximum)
    bound = _rep(jnp.max(bound, axis=0, keepdims=True))

    def bisect(_, carry):
        lo, hi = carry
        mid = 0.5 * lo + 0.5 * hi
        few = count(lambda s, j: s > mid[None]) < TOPK
        return jnp.where(few, lo, mid), jnp.where(few, mid, hi)

    _, hi = lax.fori_loop(0, N_BISECT, bisect, (-bound, bound))

    thr = masked_max(lambda s, j: s <= hi[None])
    n_ge = count(lambda s, j: s >= thr[None])

    def fix_body(carry):
        thr, n_ge, _ = carry
        lower = masked_max(lambda s, j: s < thr[None])
        thr = jnp.where(n_ge < TOPK, lower, thr)
        n_ge = count(lambda s, j: s >= thr[None])
        return thr, n_ge, _any(n_ge < TOPK)

    thr, n_ge, _ = lax.while_loop(lambda c: c[2], fix_body, (thr, n_ge, _any(n_ge < TOPK)))
    n_gt = count(lambda s, j: s > thr[None])
    need = TOPK - n_gt
    return thr[0:1], need[0:1]


def _keep(pred):
    return jnp.where(pred, 0.0, NEG)


def _selection_masks(tiles, thr, need, seen, tri):
    ties = [s == thr for s in tiles]
    ranks = []
    for k in range(0, len(tiles), 2):
        pair = jnp.concatenate([jnp.where(t, 1.0, 0.0).astype(BF16) for t in ties[k:k + 2]], axis=1)
        rank = jnp.dot(tri, pair, preferred_element_type=F32)
        ranks += [rank[:, p * LANE:(p + 1) * LANE] for p in range(len(ties[k:k + 2]))]
    masks = []
    for s, tie, rank in zip(tiles, ties, ranks):
        masks.append(jnp.where(tie, _keep(rank + seen <= need), _keep(s > thr)))
        seen = seen + rank[s.shape[0] - 1:]
    return masks, seen


def _attn_p_kernel(q_ref, qi_ref, wi_ref, k_ref, v_ref, ki_ref, bias_ref, tri_ref, o_ref,
                   kb_ref, vt_ref, kib_ref, qh_ref, qih_ref,
                   score_ref, mask_ref, s_ref):
    i = pl.program_id(1)
    T = Q_BLOCK
    W = GROUP * T
    nch = i // CHUNK_BLOCKS + 1

    @pl.when(i == 0)
    def _():
        kb_ref[0:PAD_KEYS] = jnp.zeros((PAD_KEYS, D_KV), BF16)
        kb_ref[PAD_KEYS:] = k_ref[0].astype(BF16)
        kib_ref[0:PAD_KEYS] = jnp.zeros((PAD_KEYS, IDX_DIM), BF16)
        kib_ref[PAD_KEYS:] = ki_ref[0][:, :IDX_DIM].astype(BF16)
        ones_row = lax.broadcasted_iota(jnp.int32, (VT_ROWS - HEAD_DIM, PAD_KEYS + SEQ), 0) == 0
        for g in range(N_KV_HEADS):
            vt_ref[g, 0:HEAD_DIM, 0:PAD_KEYS] = jnp.zeros((HEAD_DIM, PAD_KEYS), BF16)
            vt_ref[g, HEAD_DIM:VT_ROWS, :] = jnp.where(ones_row, 1.0, 0.0).astype(BF16)
            for c in range(SEQ // LANE):
                blk = v_ref[0, c * LANE:(c + 1) * LANE, g * HEAD_DIM:(g + 1) * HEAD_DIM]
                vt_ref[g, 0:HEAD_DIM,
                       PAD_KEYS + c * LANE:PAD_KEYS + (c + 1) * LANE] = blk.T.astype(BF16)
        score_ref[0:PAD_KEYS] = jnp.full((PAD_KEYS, T), NEG_INF, F32)
        mask_ref[0:PAD_KEYS] = jnp.full((PAD_KEYS, T), NEG, F32)

    q = q_ref[0].astype(F32) * (QK_SCALE * LOG2E)
    for h in range(N_HEADS):
        qh_ref[h] = q[:, h * HEAD_DIM:(h + 1) * HEAD_DIM].astype(BF16)
    qi = qi_ref[0]
    for h in range(IDX_HEADS):
        qih_ref[h] = qi[:, h * IDX_DIM:(h + 1) * IDX_DIM].astype(BF16)
    w_rows = wi_ref[0].T[W_OFF:W_OFF + IDX_HEADS] * SCORE_SCALE

    def span(c):
        return pl.ds(pl.multiple_of((i - CHUNK_BLOCKS * c) * LANE, LANE), CHUNK)

    def first_key(c):
        return (i - CHUNK_BLOCKS * c) * LANE - PAD_KEYS

    key_l = lax.broadcasted_iota(jnp.int32, (CHUNK, T), 0)
    qry = i * T + lax.broadcasted_iota(jnp.int32, (CHUNK, T), 1)

    def score_chunk(c):
        kc = kib_ref[span(c), :]
        d = lax.dot_general(kc, qih_ref[...].reshape(IDX_HEADS * T, IDX_DIM),
                            (((1,), (1,)), ((), ())), preferred_element_type=F32)
        acc = jnp.zeros((CHUNK, T), F32)
        for h in range(IDX_HEADS):
            acc = acc + jnp.maximum(d[:, h * T:(h + 1) * T], 0.0) * w_rows[h:h + 1]
        key = first_key(c) + key_l
        acc = jnp.where(key <= qry, jnp.where(key >= 0, acc, NEG_INF), NEG_INF)
        score_ref[span(c), :] = acc

    nsub = CHUNK // SUBLANE

    def tile_fn(c):
        return score_ref[span(c), :].reshape(nsub, SUBLANE, T)

    def select_all():
        mask_ref[span(0), :] = _keep(score_ref[span(0), :] > NEG_INF)

    def select(n):
        thr, need = _select_threshold(tile_fn, n)
        parts = []
        for c in reversed(range(n)):
            start = pl.multiple_of((i - CHUNK_BLOCKS * c) * LANE, LANE)
            parts += [pl.ds(start + p * MXU_WIDTH, MXU_WIDTH) for p in range(CHUNK // MXU_WIDTH)]
        masks, _ = _selection_masks([score_ref[rows, :] for rows in parts], thr, need,
                                    jnp.zeros((1, T), F32), tri_ref[...])
        for rows, mask in zip(parts, masks):
            mask_ref[rows, :] = mask

    def logits(c, g):
        kc = kb_ref[span(c), g * HEAD_DIM:(g + 1) * HEAD_DIM]
        qg = qh_ref[g * GROUP:(g + 1) * GROUP].reshape(W, HEAD_DIM)
        s = lax.dot_general(kc, qg, (((1,), (1,)), ((), ())), preferred_element_type=F32)
        mb = mask_ref[span(c), :]
        if c == 0:
            add = jnp.concatenate([mb + bias_ref[g * GROUP + hq] for hq in range(GROUP)], axis=1)
        else:
            add = jnp.concatenate([mb] * GROUP, axis=1)
        s = s + add
        s_ref[span(c), g * W:(g + 1) * W] = s
        return jnp.max(_fold_rows(s, jnp.maximum), axis=0, keepdims=True)

    def run(n):
        for c in range(n):
            score_chunk(c)
        if n == 1:
            pl.when(i * T + T <= TOPK)(select_all)
            pl.when(i * T + T > TOPK)(functools.partial(select, n))
        else:
            select(n)
        for g in range(N_KV_HEADS):
            m = functools.reduce(jnp.maximum, [logits(c, g) for c in range(n)])
            acc = None
            for c in range(n):
                p = jnp.exp2((s_ref[span(c), g * W:(g + 1) * W] - m).astype(BF16))
                pv = jnp.dot(vt_ref[g, :, span(c)], p, preferred_element_type=F32)
                acc = pv if acc is None else acc + pv
            o = acc[0:HEAD_DIM] / acc[HEAD_DIM:HEAD_DIM + 1]
            for hq in range(GROUP):
                h = g * GROUP + hq
                o_ref[0, :, h * HEAD_DIM:(h + 1) * HEAD_DIM] = o[:, hq * T:(hq + 1) * T].T

    for n in range(1, SEQ // CHUNK + 1):
        pl.when(nch == n)(functools.partial(run, n))


def _attn_prompt(hb_p, hf_p, bias_p):
    hb3 = hb_p.reshape(BATCH, SEQ, D_HB)
    hf3 = hf_p.reshape(BATCH, SEQ, D_HF)
    nqb = SEQ // Q_BLOCK
    T = Q_BLOCK
    qi_w = IDX_HEADS * IDX_DIM
    return pl.pallas_call(
        _attn_p_kernel,
        grid=(BATCH, nqb),
        in_specs=[
            pl.BlockSpec((1, T, D_ATTN), lambda b, i: (b, i, C_Q // D_ATTN)),
            pl.BlockSpec((1, T, qi_w), lambda b, i: (b, i, F_QI // qi_w)),
            pl.BlockSpec((1, T, LANE), lambda b, i: (b, i, F_KI // LANE)),
            pl.BlockSpec((1, SEQ, D_KV), lambda b, i: (b, 0, F_K // D_KV)),
            pl.BlockSpec((1, SEQ, D_KV), lambda b, i: (b, 0, F_V // D_KV)),
            pl.BlockSpec((1, SEQ, LANE), lambda b, i: (b, 0, F_KI // LANE)),
            pl.BlockSpec((N_HEADS, CHUNK, T), lambda b, i: (0, 0, 0)),
            pl.BlockSpec((MXU_WIDTH, MXU_WIDTH), lambda b, i: (0, 0)),
        ],
        out_specs=pl.BlockSpec((1, T, D_ATTN), lambda b, i: (b, i, 0)),
        out_shape=jax.ShapeDtypeStruct((BATCH, SEQ, D_ATTN), F32),
        scratch_shapes=[
            pltpu.VMEM((PAD_KEYS + SEQ, D_KV), BF16),
            pltpu.VMEM((N_KV_HEADS, VT_ROWS, PAD_KEYS + SEQ), BF16),
            pltpu.VMEM((PAD_KEYS + SEQ, IDX_DIM), BF16),
            pltpu.VMEM((N_HEADS, T, HEAD_DIM), BF16),
            pltpu.VMEM((IDX_HEADS, T, IDX_DIM), BF16),
            pltpu.VMEM((PAD_KEYS + SEQ, T), F32),
            pltpu.VMEM((PAD_KEYS + SEQ, T), F32),
            pltpu.VMEM((PAD_KEYS + SEQ, N_HEADS * T), F32),
        ],
        compiler_params=_params(("parallel", "arbitrary")),
        name="attn_prompt",
    )(hb3, hf3, hf3, hf3, hf3, hf3, bias_p, jnp.tri(MXU_WIDTH, dtype=BF16))


def _score_s_kernel(pt_ref, qi_ref, kin_ref, *rest):
    npg = SCORE_ROWS * N_PAGES
    kip = rest[0:npg]
    o_ref = rest[npg]
    kinp_ref = rest[npg + 1]
    del pt_ref
    R = DEC_SEQ
    qrow = lax.broadcasted_iota(jnp.int32, (R, LANE), 0)
    lane = lax.broadcasted_iota(jnp.int32, (R, LANE), 1)
    kinp_ref[...] = jnp.zeros(kinp_ref.shape, BF16)
    for r in range(SCORE_ROWS):
        kinp_ref[r, 0:2 * R] = jnp.concatenate(
            [kin_ref[r][:, :IDX_DIM], jnp.zeros((R, IDX_DIM), F32)], 0).astype(BF16)
    for r in range(SCORE_ROWS):
        qi = jnp.concatenate([qi_ref[r][:, h * IDX_DIM:(h + 1) * IDX_DIM]
                              for h in range(IDX_HEADS)], axis=0).astype(BF16)
        w = kin_ref[r][:, W_OFF:W_OFF + IDX_HEADS] * SCORE_SCALE
        wb = jnp.broadcast_to(jnp.concatenate([w[:, h:h + 1] for h in range(IDX_HEADS)], axis=0),
                              (IDX_HEADS * R, LANE))
        for t in range(N_PAGES + 1):
            if t < N_PAGES:
                d = jnp.dot(qi, kip[r * N_PAGES + t][0].astype(BF16), preferred_element_type=F32)
            else:
                d = lax.dot_general(qi, kinp_ref[r], (((1,), (1,)), ((), ())),
                                    preferred_element_type=F32)
            e = (jnp.maximum(d, 0.0) * wb).reshape(IDX_HEADS, R, LANE)
            s = e[0]
            for h in range(1, IDX_HEADS):
                s = s + e[h]
            if t == N_PAGES:
                s = jnp.where(lane <= qrow, s, NEG_INF)
            o_ref[r, :, t * LANE:(t + 1) * LANE] = s


def _score_sample(hf_s, page_table, cache_kidx_t):
    R = DEC_SEQ
    G = SCORE_ROWS
    h3 = hf_s.reshape(DEC_BATCH, R, D_HF)
    qi_w = IDX_HEADS * IDX_DIM
    in_specs = [
        pl.BlockSpec((G, R, qi_w), lambda b, pt: (b, 0, F_QI // qi_w)),
        pl.BlockSpec((G, R, LANE), lambda b, pt: (b, 0, F_KI // LANE)),
    ]
    in_specs += [pl.BlockSpec((1, IDX_DIM, PAGE_SIZE),
                              lambda b, pt, r=r, p=p: (pt[b * G + r, p], 0, 0))
                 for r in range(G) for p in range(N_PAGES)]
    grid_spec = pltpu.PrefetchScalarGridSpec(
        num_scalar_prefetch=1,
        grid=(DEC_BATCH // G,),
        in_specs=in_specs,
        out_specs=pl.BlockSpec((G, R, L_SAMPLE), lambda b, pt: (b, 0, 0)),
        scratch_shapes=[pltpu.VMEM((G, PAGE_SIZE, IDX_DIM), BF16)],
    )
    return pl.pallas_call(
        _score_s_kernel,
        grid_spec=grid_spec,
        out_shape=jax.ShapeDtypeStruct((DEC_BATCH, R, L_SAMPLE), F32),
        compiler_params=_params(("arbitrary",)),
        name="score_sample",
    )(page_table, h3, h3, *([cache_kidx_t] * (G * N_PAGES)))


def _select_s_kernel(s_ref, tri_ref, o_ref, st_ref):
    nt = N_PAGES + 1
    parts = [slice(j * LANE, (j + 1) * LANE) for j in range(nt)]
    for cols in parts:
        st_ref[cols, :] = s_ref[:, cols].T

    def tile_fn(j):
        return st_ref[parts[j], :].reshape(SUBTILES, SUBLANE, LANE)

    thr, need = _select_threshold(tile_fn, nt)
    masks, _ = _selection_masks([st_ref[rows, :] for rows in parts], thr, need,
                                jnp.zeros((1, LANE), F32), tri_ref[...])
    for cols, mask in zip(parts, masks):
        o_ref[:, cols] = mask.T


def _select_sample(scores):
    n = scores.shape[0]
    return pl.pallas_call(
        _select_s_kernel,
        grid=(n // LANE,),
        in_specs=[pl.BlockSpec((LANE, L_SAMPLE), lambda c: (c, 0)),
                  pl.BlockSpec((LANE, LANE), lambda c: (0, 0))],
        out_specs=pl.BlockSpec((LANE, L_SAMPLE), lambda c: (c, 0)),
        out_shape=jax.ShapeDtypeStruct((n, L_SAMPLE), F32),
        scratch_shapes=[pltpu.VMEM((L_SAMPLE, LANE), F32)],
        compiler_params=_params(("parallel",)),
        name="select_sample",
    )(scores, jnp.tri(LANE, dtype=BF16))


def _attn_s_kernel(pt_ref, q_ref, kn_ref, vn_ref, mask_ref, bias_ref, *rest):
    npg = ATTN_ROWS * N_PAGES
    kp = rest[0:npg]
    vp = rest[npg:2 * npg]
    o_ref = rest[2 * npg]
    knp_ref, vnp_ref, logit_ref = rest[2 * npg + 1:]
    del pt_ref
    R = DEC_SEQ
    NT = N_PAGES + 1
    GR = GROUP * R

    knp_ref[...] = jnp.zeros(knp_ref.shape, BF16)
    vnp_ref[...] = jnp.zeros(vnp_ref.shape, BF16)
    for r in range(ATTN_ROWS):
        knp_ref[r, 0:2 * R] = jnp.concatenate([kn_ref[r], jnp.zeros((R, D_KV), F32)], 0).astype(BF16)
        vnp_ref[r, 0:2 * R] = jnp.concatenate([vn_ref[r], jnp.zeros((R, D_KV), F32)], 0).astype(BF16)

    def page_head(refs, pad_ref, r, t, g):
        if t < N_PAGES:
            return refs[r * N_PAGES + t][pl.ds(g, PAGE_SIZE, stride=N_KV_HEADS), :].astype(BF16)
        return pad_ref[r, :, g * HEAD_DIM:(g + 1) * HEAD_DIM]

    for r in range(ATTN_ROWS):
        qf = q_ref[r].astype(F32) * QK_SCALE
        q = jnp.concatenate([qf[:, h * HEAD_DIM:(h + 1) * HEAD_DIM] for h in range(N_HEADS)],
                            axis=0).astype(BF16)
        for t in range(NT):
            mb = mask_ref[r, :, t * LANE:(t + 1) * LANE]
            for g in range(N_KV_HEADS):
                lg = lax.dot_general(q[g * GR:(g + 1) * GR], page_head(kp, knp_ref, r, t, g),
                                     (((1,), (1,)), ((), ())), preferred_element_type=F32)
                lg = lg.reshape(GROUP, R, LANE) + mb[None]
                if t >= N_PAGES - 1:
                    off = (t - (N_PAGES - 1)) * LANE
                    lg = lg + bias_ref[g * GROUP:(g + 1) * GROUP, :, off:off + LANE]
                logit_ref[r, g * GR:(g + 1) * GR, t * LANE:(t + 1) * LANE] = lg.reshape(GR, LANE)

    for r in range(ATTN_ROWS):
        logits = logit_ref[r]
        m = jnp.max(logits, axis=1, keepdims=True)
        p = jnp.exp(logits - m)
        inv = 1.0 / jnp.sum(p, axis=1, keepdims=True)
        pb = p.astype(BF16)
        outs = [jnp.zeros((GR, HEAD_DIM), F32) for _ in range(N_KV_HEADS)]
        for t in range(NT):
            for g in range(N_KV_HEADS):
                outs[g] = outs[g] + jnp.dot(pb[g * GR:(g + 1) * GR, t * LANE:(t + 1) * LANE],
                                            page_head(vp, vnp_ref, r, t, g),
                                            preferred_element_type=F32)
        for g in range(N_KV_HEADS):
            o = outs[g] * inv[g * GR:(g + 1) * GR]
            for hq in range(GROUP):
                h = g * GROUP + hq
                o_ref[r, :, h * HEAD_DIM:(h + 1) * HEAD_DIM] = o[hq * R:(hq + 1) * R]


def _attn_sample(hb_s, hf_s, page_table, cache_k, cache_v, mask, bias_s):
    R = DEC_SEQ
    G = ATTN_ROWS
    h3 = hf_s.reshape(DEC_BATCH, R, D_HF)
    hb3 = hb_s.reshape(DEC_BATCH, R, D_HB)
    rows_per_page = PAGE_SIZE * N_KV_HEADS
    ck = cache_k.reshape(-1, HEAD_DIM)
    cv = cache_v.reshape(-1, HEAD_DIM)

    in_specs = [
        pl.BlockSpec((G, R, D_ATTN), lambda b, pt: (b, 0, C_Q // D_ATTN)),
        pl.BlockSpec((G, R, D_KV), lambda b, pt: (b, 0, F_K // D_KV)),
        pl.BlockSpec((G, R, D_KV), lambda b, pt: (b, 0, F_V // D_KV)),
        pl.BlockSpec((G, R, L_SAMPLE), lambda b, pt: (b, 0, 0)),
        pl.BlockSpec((N_HEADS, R, 2 * LANE), lambda b, pt: (0, 0, 0)),
    ]
    pages = [pl.BlockSpec((rows_per_page, HEAD_DIM), lambda b, pt, r=r, p=p: (pt[b * G + r, p], 0))
             for r in range(G) for p in range(N_PAGES)]
    in_specs += pages + pages
    grid_spec = pltpu.PrefetchScalarGridSpec(
        num_scalar_prefetch=1,
        grid=(DEC_BATCH // G,),
        in_specs=in_specs,
        out_specs=pl.BlockSpec((G, R, D_ATTN), lambda b, pt: (b, 0, 0)),
        scratch_shapes=[
            pltpu.VMEM((G, PAGE_SIZE, D_KV), BF16),
            pltpu.VMEM((G, PAGE_SIZE, D_KV), BF16),
            pltpu.VMEM((G, N_HEADS * R, L_SAMPLE), F32),
        ],
    )
    return pl.pallas_call(
        _attn_s_kernel,
        grid_spec=grid_spec,
        out_shape=jax.ShapeDtypeStruct((DEC_BATCH, R, D_ATTN), F32),
        compiler_params=_params(("arbitrary",)),
        name="attn_sample",
    )(page_table, hb3, h3, h3, mask, bias_s, *([ck] * (G * N_PAGES)), *([cv] * (G * N_PAGES)))


def _conv_p_kernel(val_ref, gate_ref, cw_ref, cb_ref, dw_ref, ut_ref, pad_ref):
    pad_ref[0:PAD_ROWS] = jnp.zeros((PAD_ROWS, LANE), F32)
    pad_ref[PAD_ROWS:] = val_ref[0].astype(F32) * jax.nn.sigmoid(gate_ref[0].astype(F32))
    ut_ref[0] = pad_ref[SEQ:SEQ + PAD_ROWS]
    cw = cw_ref[...]
    cb = cb_ref[...]
    first = PAD_ROWS - (CONV_WIDTH - 1)
    for c in range(SEQ // CONV_CHUNK):
        base = c * CONV_CHUNK
        acc = jnp.broadcast_to(cb, (CONV_CHUNK, LANE))
        for r in range(SUBLANE):
            taps = [w for w in range(CONV_WIDTH) if (first + w) % SUBLANE == r]
            span = max(first + w - r for w in taps) + CONV_CHUNK
            win = pad_ref[base + r:base + r + span]
            for w in taps:
                a = first + w - r
                acc = acc + win[a:a + CONV_CHUNK] * cw[w:w + 1]
        dw_ref[0, base:base + CONV_CHUNK] = acc


def _conv_prompt(hb_p, cw_pad, cb):
    h3 = hb_p.reshape(BATCH, SEQ, D_HB)
    nc = D_CONV // LANE
    return pl.pallas_call(
        _conv_p_kernel,
        grid=(BATCH, nc),
        in_specs=[pl.BlockSpec((1, SEQ, LANE), lambda b, c: (b, 0, C_GV // LANE + c)),
                  pl.BlockSpec((1, SEQ, LANE), lambda b, c: (b, 0, C_GG // LANE + c)),
                  pl.BlockSpec((PAD_ROWS, LANE), lambda b, c: (0, c)),
                  pl.BlockSpec((1, LANE), lambda b, c: (0, c))],
        out_specs=[pl.BlockSpec((1, SEQ, LANE), lambda b, c: (b, 0, c)),
                   pl.BlockSpec((1, PAD_ROWS, LANE), lambda b, c: (b, 0, c))],
        out_shape=[jax.ShapeDtypeStruct((BATCH, SEQ, D_CONV), F32),
                   jax.ShapeDtypeStruct((BATCH, PAD_ROWS, D_CONV), F32)],
        scratch_shapes=[pltpu.VMEM((PAD_ROWS + SEQ, LANE), F32)],
        compiler_params=_params(("parallel", "parallel")),
        name="conv_prompt",
    )(h3, h3, cw_pad, cb)


def _conv_s_kernel(val_ref, gate_ref, st_ref, cw_ref, cb_ref, dw_ref, ns_ref, glu_ref):
    R = DEC_SEQ
    H = CONV_WIDTH - 1
    cw = cw_ref[...]
    cb = jnp.broadcast_to(cb_ref[...], (DEC_BATCH, LANE))
    glu_ref[...] = val_ref[...].astype(F32) * jax.nn.sigmoid(gate_ref[...].astype(F32))
    u = []
    for q in range(R):
        u.append(glu_ref[pl.ds(q, DEC_BATCH, stride=R), :])

    def row(r):
        return st_ref[r] if r < H else u[r - H]

    for q in range(R):
        acc = cb
        for w in range(CONV_WIDTH):
            acc = acc + row(q + w) * cw[w:w + 1]
        dw_ref[pl.ds(q, DEC_BATCH, stride=R), :] = acc
    for r in range(H):
        ns_ref[r] = row(r + R)


def _conv_sample(hb_s, state_t, cw_pad, cb):
    n_s = DEC_BATCH * DEC_SEQ
    H = CONV_WIDTH - 1
    nc = D_CONV // LANE
    return pl.pallas_call(
        _conv_s_kernel,
        grid=(nc,),
        in_specs=[pl.BlockSpec((n_s, LANE), lambda c: (0, C_GV // LANE + c)),
                  pl.BlockSpec((n_s, LANE), lambda c: (0, C_GG // LANE + c)),
                  pl.BlockSpec((H, DEC_BATCH, LANE), lambda c: (0, 0, c)),
                  pl.BlockSpec((PAD_ROWS, LANE), lambda c: (0, c)),
                  pl.BlockSpec((1, LANE), lambda c: (0, c))],
        out_specs=[pl.BlockSpec((n_s, LANE), lambda c: (0, c)),
                   pl.BlockSpec((H, DEC_BATCH, LANE), lambda c: (0, 0, c))],
        out_shape=[jax.ShapeDtypeStruct((n_s, D_CONV), F32),
                   jax.ShapeDtypeStruct((H, DEC_BATCH, D_CONV), F32)],
        scratch_shapes=[pltpu.VMEM((n_s, LANE), F32)],
        compiler_params=_params(("parallel",)),
        name="conv_sample",
    )(hb_s, hb_s, state_t, cw_pad, cb)


def _tail_kernel(attn_ref, za_ref, dw_ref, zc_ref, ga_ref, gc_ref, x_ref,
                 wua_ref, wpw_ref, wuc_ref, wo_ref, ng_ref, nb_ref, bpw_ref, fg_ref, y_ref):
    a = attn_ref[...] * jax.nn.silu(za_ref[...].astype(F32))
    branch_attn = jnp.dot(a.astype(BF16), wua_ref[...], preferred_element_type=F32)

    dw = dw_ref[...]
    mu = jnp.mean(dw, axis=-1, keepdims=True)
    var = jnp.mean(jnp.square(dw - mu), axis=-1, keepdims=True)
    ln = (dw - mu) * lax.rsqrt(var + EPS) * ng_ref[...] + nb_ref[...]
    conv_out = jnp.dot(jax.nn.silu(ln).astype(BF16), wpw_ref[...],
                       preferred_element_type=F32) + bpw_ref[...]
    c = conv_out * jax.nn.silu(zc_ref[...].astype(F32))
    branch_conv = jnp.dot(c.astype(BF16), wuc_ref[...], preferred_element_type=F32)

    merged = (jax.nn.sigmoid(ga_ref[...].astype(F32)) * branch_attn
              + jax.nn.sigmoid(gc_ref[...].astype(F32)) * branch_conv)
    y = x_ref[...] + jnp.dot(merged.astype(BF16), wo_ref[...], preferred_element_type=F32)
    ms = jnp.mean(y * y, axis=-1, keepdims=True)
    y_ref[...] = y * lax.rsqrt(ms + EPS) * fg_ref[...]


def _tail(h, attn, dw, x2d, weights, tm):
    n = x2d.shape[0]

    def const(shape):
        return pl.BlockSpec(shape, lambda i: (0, 0), pipeline_mode=pl.Buffered(1))

    def cols(width, offset):
        return pl.BlockSpec((tm, width), lambda i: (i, offset // width))

    return pl.pallas_call(
        _tail_kernel,
        grid=(n // tm,),
        in_specs=[cols(D_ATTN, 0), cols(D_ATTN, C_ZA), cols(D_CONV, 0), cols(D_CONV, C_ZC),
                  cols(D_MODEL, C_GA), cols(D_MODEL, C_GC), cols(D_MODEL, 0),
                  const((D_ATTN, D_MODEL)), const((D_CONV, D_CONV)),
                  const((D_CONV, D_MODEL)), const((D_MODEL, D_MODEL)),
                  const((1, D_CONV)), const((1, D_CONV)), const((1, D_CONV)),
                  const((1, D_MODEL))],
        out_specs=pl.BlockSpec((tm, D_MODEL), lambda i: (i, 0)),
        out_shape=jax.ShapeDtypeStruct((n, D_MODEL), F32),
        compiler_params=_params(("parallel",)),
        name="tail",
    )(attn, h, dw, h, h, h, x2d, *weights)


def _reorder_kernel(off_ref, w_ref, o_ref):
    del off_ref
    o_ref[...] = w_ref[...].astype(BF16)


def _reorder_w_in(w_in):
    w_t = w_in.T
    src = []
    for first, dest, width in (('gate_attn', C_GA, 2 * D_MODEL), ('q', C_Q, D_ATTN),
                               ('z_attn', C_ZA, D_ATTN), ('glu_val', C_GV, 3 * D_CONV),
                               ('k', C_K, 2 * D_KV), ('q_idx', C_QI, D_H - C_QI)):
        assert dest == len(src) * REORDER_ROWS
        for r in range(0, width, REORDER_ROWS):
            src.append(min(_SRC[first][0] + r, D_IN - REORDER_ROWS) // SUBLANE)
    grid_spec = pltpu.PrefetchScalarGridSpec(
        num_scalar_prefetch=1,
        grid=(len(src),),
        in_specs=[pl.BlockSpec((pl.Element(REORDER_ROWS), pl.Element(D_MODEL)),
                               lambda d, off: (off[d] * SUBLANE, 0))],
        out_specs=pl.BlockSpec((REORDER_ROWS, D_MODEL), lambda d, off: (d, 0)),
    )
    return pl.pallas_call(
        _reorder_kernel,
        grid_spec=grid_spec,
        out_shape=jax.ShapeDtypeStruct((D_H, D_MODEL), BF16),
        compiler_params=_params(("arbitrary",)),
        name="reorder_w_in",
    )(jnp.asarray(src, jnp.int32), w_t)


def kernel(x_prompt, x_sample, cache_k, cache_v, cache_kidx, state_conv, page_table,
           ln_g, w_in, conv_w, conv_b, conv_norm_g, conv_norm_b, w_pw, b_pw,
           w_up_attn, w_up_conv, w_out, rel_bias, final_g):
    w_all = _reorder_w_in(w_in[0])
    g_in = ln_g[0].reshape(1, D_MODEL)
    xp = x_prompt.reshape(BATCH * SEQ, D_MODEL)
    xs = x_sample.reshape(DEC_BATCH * DEC_SEQ, D_MODEL)
    hb_p, hf_p, k_p, v_p, kit_p = _proj(xp, g_in, w_all, SEQ)
    hb_s, hf_s, k_s, v_s, _ = _proj(xs, g_in, w_all, DEC_BATCH * DEC_SEQ)

    bias_p, bias_s = _bias_tables(rel_bias)

    attn_p = _attn_prompt(hb_p, hf_p, bias_p).reshape(BATCH * SEQ, D_ATTN)

    n_s = DEC_BATCH * DEC_SEQ
    scores = _score_sample(hf_s, page_table, cache_kidx[0].transpose(0, 2, 1))
    mask = _select_sample(scores.reshape(n_s, L_SAMPLE)).reshape(DEC_BATCH, DEC_SEQ, L_SAMPLE)
    attn_s = _attn_sample(hb_s, hf_s, page_table, cache_k[0], cache_v[0], mask, bias_s)
    attn_s = attn_s.reshape(n_s, D_ATTN)

    cw_pad = jnp.concatenate([conv_w[0], jnp.zeros((PAD_ROWS - CONV_WIDTH, D_CONV), F32)], 0)
    cb = conv_b[0].reshape(1, D_CONV)
    dw_p, u_tail = _conv_prompt(hb_p, cw_pad, cb)
    dw_s, state_new = _conv_sample(hb_s, state_conv[0].transpose(1, 0, 2), cw_pad, cb)

    wua = w_up_attn[0].astype(BF16)
    wpw = w_pw[0].astype(BF16)
    wuc = w_up_conv[0].astype(BF16)
    wo = w_out[0].astype(BF16)
    ng = conv_norm_g[0].reshape(1, D_CONV)
    nb = conv_norm_b[0].reshape(1, D_CONV)
    bpw = b_pw[0].reshape(1, D_CONV)
    fg = final_g.reshape(1, D_MODEL)
    weights = (wua, wpw, wuc, wo, ng, nb, bpw, fg)
    y_p = _tail(hb_p, attn_p, dw_p.reshape(BATCH * SEQ, D_CONV), xp, weights, TAIL_TM)
    y_s = _tail(hb_s, attn_s, dw_s, xs, weights, TAIL_TM)

    tail_rows = CONV_WIDTH - 1
    return (
        y_p.reshape(BATCH, SEQ, D_MODEL),
        y_s.reshape(DEC_BATCH, DEC_SEQ, D_MODEL),
        k_p.reshape(1, BATCH, SEQ, N_KV_HEADS, HEAD_DIM),
        v_p.reshape(1, BATCH, SEQ, N_KV_HEADS, HEAD_DIM),
        kit_p.transpose(0, 2, 1).reshape(1, BATCH, SEQ, IDX_DIM),
        u_tail[:, PAD_ROWS - tail_rows:].reshape(1, BATCH, tail_rows, D_CONV),
        k_s.reshape(1, DEC_BATCH, DEC_SEQ, N_KV_HEADS, HEAD_DIM),
        v_s.reshape(1, DEC_BATCH, DEC_SEQ, N_KV_HEADS, HEAD_DIM),
        hf_s[:, F_KI:F_KI + IDX_DIM].reshape(1, DEC_BATCH, DEC_SEQ, IDX_DIM),
        state_new.transpose(1, 0, 2).reshape(1, DEC_BATCH, tail_rows, D_CONV),
    )
```

```python
import functools
import math

import numpy as np
import jax
import jax.numpy as jnp
from jax import lax
from jax.experimental import pallas as pl
from jax.experimental.pallas import tpu as pltpu

F32 = jnp.float32
BF16 = jnp.bfloat16

D_MODEL = 2048
BATCH = 8
SEQ = 2048
DEC_BATCH = 128
DEC_SEQ = 8
PAST_LEN = 2048
PAGE_SIZE = 128
N_PAGES = PAST_LEN // PAGE_SIZE
N_HEADS = 8
N_KV_HEADS = 2
HEAD_DIM = 128
GROUP = N_HEADS // N_KV_HEADS
D_ATTN = N_HEADS * HEAD_DIM
D_KV = N_KV_HEADS * HEAD_DIM
IDX_HEADS = 8
IDX_DIM = 64
TOPK = 256
Q_BLOCK = 128
N_BUCKETS = 32
MAX_DISTANCE = 128
D_CONV = D_MODEL // 2
CONV_WIDTH = 31
EPS = 1e-6
NEG = -1e30
NEG_INF = float("-inf")

LANE = 128
SUBLANE = 8
VMEM_LIMIT = 56 * 1024 * 1024

C_GA = 0
C_GC = C_GA + D_MODEL
C_Q = C_GC + D_MODEL
C_ZA = C_Q + D_ATTN
C_GV = C_ZA + D_ATTN
C_GG = C_GV + D_CONV
C_ZC = C_GG + D_CONV
C_K = C_ZC + D_CONV
C_V = C_K + D_KV
C_QI = C_V + D_KV
C_KI = C_QI + IDX_HEADS * IDX_DIM
W_OFF = IDX_DIM
MXU_WIDTH = 256
PROJ_TM = 1024
PROJ_TN = 6 * MXU_WIDTH
D_H = -(-(C_KI + LANE) // PROJ_TN) * PROJ_TN
KV_TILE = C_K // PROJ_TN
assert C_K % PROJ_TN == 0 and D_H == C_K + PROJ_TN
D_HB = C_K
D_HF = C_KI + LANE - C_K
F_K = C_K - D_HB
F_V = C_V - D_HB
F_QI = C_QI - D_HB
F_KI = C_KI - D_HB

_SRC = {}
_off = 0
for _name, _w in (('q', D_ATTN), ('k', D_KV), ('v', D_KV), ('z_attn', D_ATTN),
                  ('q_idx', IDX_HEADS * IDX_DIM), ('k_idx', IDX_DIM), ('w_idx', IDX_HEADS),
                  ('glu_val', D_CONV), ('glu_gate', D_CONV), ('z_conv', D_CONV),
                  ('gate_attn', D_MODEL), ('gate_conv', D_MODEL)):
    _SRC[_name] = (_off, _w)
    _off += _w
D_IN = _off
REORDER_ROWS = 512

N_BISECT = 20
PAD_ROWS = 32
CONV_CHUNK = 64
SCORE_SCALE = (IDX_DIM ** -0.5) * (IDX_HEADS ** -0.5)
QK_SCALE = HEAD_DIM ** -0.5
LOG2E = math.log2(math.e)
L_SAMPLE = (N_PAGES + 1) * LANE
SUBTILES = LANE // SUBLANE
CHUNK_BLOCKS = 2
CHUNK = CHUNK_BLOCKS * Q_BLOCK
PAD_KEYS = CHUNK - Q_BLOCK
VT_ROWS = HEAD_DIM + 2 * SUBLANE
SCORE_ROWS = 8
ATTN_ROWS = 4
TAIL_TM = 256


def _t5_bucket_static(dist):
    n = np.maximum(dist, 0)
    max_exact = N_BUCKETS // 2
    ratio = (np.log(np.maximum(n, 1).astype(np.float32) / np.float32(max_exact))
             / np.float32(math.log(MAX_DISTANCE / max_exact)))
    large = np.minimum(max_exact + (ratio * np.float32(N_BUCKETS - max_exact)).astype(np.int32),
                       N_BUCKETS - 1)
    return np.where(n < max_exact, n, large).astype(np.int32)


FAR_BUCKET = int(_t5_bucket_static(np.array([2 * MAX_DISTANCE]))[0])


def _params(sem):
    return pltpu.CompilerParams(dimension_semantics=sem, vmem_limit_bytes=VMEM_LIMIT)


def _proj_kernel(x_ref, g_ref, w_ref, ob_ref, of_ref, k_ref, v_ref, kit_ref, xn_ref):
    tm = x_ref.shape[0]
    j = pl.program_id(1)

    @pl.when(j == 0)
    def _():
        x = x_ref[...]
        ms = jnp.mean(x * x, axis=-1, keepdims=True)
        xn_ref[...] = (x * lax.rsqrt(ms + EPS) * g_ref[...]).astype(BF16)

    def project():
        return lax.dot_general(xn_ref[...], w_ref[...], (((1,), (1,)), ((), ())),
                               preferred_element_type=F32)

    @pl.when(j < KV_TILE)
    def _():
        ob_ref[...] = project().astype(BF16)

    @pl.when(j == KV_TILE)
    def _():
        of_ref[...] = project()[:, :D_HF]
        for g in range(N_KV_HEADS):
            rows = pl.ds(g, tm, stride=N_KV_HEADS)
            k_ref[rows, :] = of_ref[:, F_K + g * HEAD_DIM:F_K + (g + 1) * HEAD_DIM]
            v_ref[rows, :] = of_ref[:, F_V + g * HEAD_DIM:F_V + (g + 1) * HEAD_DIM]
        for c in range(tm // LANE):
            tile = of_ref[c * LANE:(c + 1) * LANE, F_KI:F_KI + LANE].T
            kit_ref[0, :, c * LANE:(c + 1) * LANE] = tile[0:IDX_DIM]


def _proj(x2d, g, w, seq):
    n = x2d.shape[0]
    tm, tn = PROJ_TM, PROJ_TN
    tiles = seq // tm
    kv_spec = pl.BlockSpec((N_KV_HEADS * tm, HEAD_DIM), lambda i, j: (i, 0))
    kv_shape = jax.ShapeDtypeStruct((N_KV_HEADS * n, HEAD_DIM), F32)
    return pl.pallas_call(
        _proj_kernel,
        grid=(n // tm, D_H // tn),
        in_specs=[pl.BlockSpec((tm, D_MODEL), lambda i, j: (i, 0)),
                  pl.BlockSpec((1, D_MODEL), lambda i, j: (0, 0)),
                  pl.BlockSpec((tn, D_MODEL), lambda i, j: (j, 0))],
        out_specs=[pl.BlockSpec((tm, tn), lambda i, j: (i, jnp.minimum(j, KV_TILE - 1))),
                   pl.BlockSpec((tm, D_HF), lambda i, j: (i, 0)), kv_spec, kv_spec,
                   pl.BlockSpec((1, IDX_DIM, tm), lambda i, j: (i // tiles, 0, i % tiles))],
        out_shape=[jax.ShapeDtypeStruct((n, D_HB), BF16),
                   jax.ShapeDtypeStruct((n, D_HF), F32), kv_shape, kv_shape,
                   jax.ShapeDtypeStruct((n // seq, IDX_DIM, seq), F32)],
        scratch_shapes=[pltpu.VMEM((tm, D_MODEL), BF16)],
        compiler_params=_params(("parallel", "arbitrary")),
        name="proj",
    )(x2d, g, w)


def _bias_kernel(rb_ref, bp_ref, bs_ref, op_ref, os_ref):
    bp = bp_ref[...]
    bs = bs_ref[...]
    for h in range(N_HEADS):
        far = rb_ref[FAR_BUCKET, h]
        tp = jnp.zeros(bp.shape, F32)
        ts = jnp.zeros(bs.shape, F32)
        for b in range(N_BUCKETS):
            val = rb_ref[b, h] - far
            tp = jnp.where(bp == b, val * LOG2E, tp)
            ts = jnp.where(bs == b, val, ts)
        op_ref[h] = tp
        os_ref[h] = ts


def _bias_tables(rel_bias):
    key = np.arange(CHUNK)[:, None]
    qry = np.arange(Q_BLOCK)[None, :]
    bucket_p = _t5_bucket_static(CHUNK - Q_BLOCK + qry - key)
    qi = np.arange(DEC_SEQ)[:, None]
    col = np.arange(2 * LANE)[None, :]
    dist_s = np.where(col < LANE, LANE + qi - col, qi - (col - LANE))
    bucket_s = _t5_bucket_static(dist_s)
    return pl.pallas_call(
        _bias_kernel,
        in_specs=[pl.BlockSpec(memory_space=pltpu.SMEM),
                  pl.BlockSpec(memory_space=pltpu.VMEM),
                  pl.BlockSpec(memory_space=pltpu.VMEM)],
        out_specs=[pl.BlockSpec(memory_space=pltpu.VMEM),
                   pl.BlockSpec(memory_space=pltpu.VMEM)],
        out_shape=[jax.ShapeDtypeStruct((N_HEADS, CHUNK, Q_BLOCK), F32),
                   jax.ShapeDtypeStruct((N_HEADS, DEC_SEQ, 2 * LANE), F32)],
        name="bias_tables",
    )(rel_bias, jnp.asarray(bucket_p), jnp.asarray(bucket_s))


def _any(x):
    return jnp.max(jnp.where(x, 1.0, 0.0)) > 0.5


def _rep(x):
    return jnp.broadcast_to(x, (SUBLANE, LANE))


def _fold_rows(x, comb):
    parts = [x[k:k + SUBLANE] for k in range(0, x.shape[0], SUBLANE)]
    while len(parts) > 1:
        parts = [comb(parts[k], parts[k + 1]) for k in range(0, len(parts), 2)]
    return parts[0]


def _select_threshold(tile_fn, ntiles):
    def reduce_tiles(fn, init, comb):
        acc = init
        for j in range(ntiles):
            x = fn(tile_fn(j), j)
            parts = [x[k] for k in range(x.shape[0])]
            while len(parts) > 1:
                parts = [comb(parts[k], parts[k + 1]) for k in range(0, len(parts), 2)]
            acc = comb(acc, parts[0])
        return acc

    zeros = jnp.zeros((SUBLANE, LANE), F32)

    def count(pred_fn):
        acc = reduce_tiles(lambda s, j: jnp.where(pred_fn(s, j), 1.0, 0.0), zeros,
                           lambda a, b: a + b)
        return _rep(jnp.sum(acc, axis=0, keepdims=True))

    def masked_max(pred_fn):
        acc = reduce_tiles(lambda s, j: jnp.where(pred_fn(s, j), s, NEG_INF),
                           jnp.full((SUBLANE, LANE), NEG_INF, F32), jnp.maximum)
        return _rep(jnp.max(acc, axis=0, keepdims=True))

    bound = reduce_tiles(lambda s, j: jnp.where(s > NEG_INF, jnp.abs(s), 0.0), zeros,
                         jnp.maximum)
    bound = _rep(jnp.max(bound, axis=0, keepdims=True))

    def bisect(_, carry):
        lo, hi = carry
        mid = 0.5 * lo + 0.5 * hi
        few = count(lambda s, j: s > mid[None]) < TOPK
        return jnp.where(few, lo, mid), jnp.where(few, mid, hi)

    _, hi = lax.fori_loop(0, N_BISECT, bisect, (-bound, bound))

    thr = masked_max(lambda s, j: s <= hi[None])
    n_ge = count(lambda s, j: s >= thr[None])

    def fix_body(carry):
        thr, n_ge, _ = carry
        lower = masked_max(lambda s, j: s < thr[None])
        thr = jnp.where(n_ge < TOPK, lower, thr)
        n_ge = count(lambda s, j: s >= thr[None])
        return thr, n_ge, _any(n_ge < TOPK)

    thr, n_ge, _ = lax.while_loop(lambda c: c[2], fix_body, (thr, n_ge, _any(n_ge < TOPK)))
    n_gt = count(lambda s, j: s > thr[None])
    need = TOPK - n_gt
    return thr[0:1], need[0:1]


def _keep(pred):
    return jnp.where(pred, 0.0, NEG)


def _selection_masks(tiles, thr, need, seen, tri):
    ties = [s == thr for s in tiles]
    ranks = []
    for k in range(0, len(tiles), 2):
        pair = jnp.concatenate([jnp.where(t, 1.0, 0.0).astype(BF16) for t in ties[k:k + 2]], axis=1)
        rank = jnp.dot(tri, pair, preferred_element_type=F32)
        ranks += [rank[:, p * LANE:(p + 1) * LANE] for p in range(len(ties[k:k + 2]))]
    masks = []
    for s, tie, rank in zip(tiles, ties, ranks):
        masks.append(jnp.where(tie, _keep(rank + seen <= need), _keep(s > thr)))
        seen = seen + rank[s.shape[0] - 1:]
    return masks, seen


def _attn_p_kernel(q_ref, qi_ref, wi_ref, k_ref, v_ref, ki_ref, bias_ref, tri_ref, o_ref,
                   kb_ref, vt_ref, kib_ref, qh_ref, qih_ref,
                   score_ref, mask_ref, s_ref):
    i = pl.program_id(1)
    T = Q_BLOCK
    W = GROUP * T
    nch = i // CHUNK_BLOCKS + 1

    @pl.when(i == 0)
    def _():
        kb_ref[0:PAD_KEYS] = jnp.zeros((PAD_KEYS, D_KV), BF16)
        kb_ref[PAD_KEYS:] = k_ref[0].astype(BF16)
        kib_ref[0:PAD_KEYS] = jnp.zeros((PAD_KEYS, IDX_DIM), BF16)
        kib_ref[PAD_KEYS:] = ki_ref[0][:, :IDX_DIM].astype(BF16)
        ones_row = lax.broadcasted_iota(jnp.int32, (VT_ROWS - HEAD_DIM, PAD_KEYS + SEQ), 0) == 0
        for g in range(N_KV_HEADS):
            vt_ref[g, 0:HEAD_DIM, 0:PAD_KEYS] = jnp.zeros((HEAD_DIM, PAD_KEYS), BF16)
            vt_ref[g, HEAD_DIM:VT_ROWS, :] = jnp.where(ones_row, 1.0, 0.0).astype(BF16)
            for c in range(SEQ // LANE):
                blk = v_ref[0, c * LANE:(c + 1) * LANE, g * HEAD_DIM:(g + 1) * HEAD_DIM]
                vt_ref[g, 0:HEAD_DIM,
                       PAD_KEYS + c * LANE:PAD_KEYS + (c + 1) * LANE] = blk.T.astype(BF16)
        score_ref[0:PAD_KEYS] = jnp.full((PAD_KEYS, T), NEG_INF, F32)
        mask_ref[0:PAD_KEYS] = jnp.full((PAD_KEYS, T), NEG, F32)

    q = q_ref[0].astype(F32) * (QK_SCALE * LOG2E)
    for h in range(N_HEADS):
        qh_ref[h] = q[:, h * HEAD_DIM:(h + 1) * HEAD_DIM].astype(BF16)
    qi = qi_ref[0]
    for h in range(IDX_HEADS):
        qih_ref[h] = qi[:, h * IDX_DIM:(h + 1) * IDX_DIM].astype(BF16)
    w_rows = wi_ref[0].T[W_OFF:W_OFF + IDX_HEADS] * SCORE_SCALE

    def span(c):
        return pl.ds(pl.multiple_of((i - CHUNK_BLOCKS * c) * LANE, LANE), CHUNK)

    def first_key(c):
        return (i - CHUNK_BLOCKS * c) * LANE - PAD_KEYS

    key_l = lax.broadcasted_iota(jnp.int32, (CHUNK, T), 0)
    qry = i * T + lax.broadcasted_iota(jnp.int32, (CHUNK, T), 1)

    def score_chunk(c):
        kc = kib_ref[span(c), :]
        d = lax.dot_general(kc, qih_ref[...].reshape(IDX_HEADS * T, IDX_DIM),
                            (((1,), (1,)), ((), ())), preferred_element_type=F32)
        acc = jnp.zeros((CHUNK, T), F32)
        for h in range(IDX_HEADS):
            acc = acc + jnp.maximum(d[:, h * T:(h + 1) * T], 0.0) * w_rows[h:h + 1]
        key = first_key(c) + key_l
        acc = jnp.where(key <= qry, jnp.where(key >= 0, acc, NEG_INF), NEG_INF)
        score_ref[span(c), :] = acc

    nsub = CHUNK // SUBLANE

    def tile_fn(c):
        return score_ref[span(c), :].reshape(nsub, SUBLANE, T)

    def select_all():
        mask_ref[span(0), :] = _keep(score_ref[span(0), :] > NEG_INF)

    def select(n):
        thr, need = _select_threshold(tile_fn, n)
        parts = []
        for c in reversed(range(n)):
            start = pl.multiple_of((i - CHUNK_BLOCKS * c) * LANE, LANE)
            parts += [pl.ds(start + p * MXU_WIDTH, MXU_WIDTH) for p in range(CHUNK // MXU_WIDTH)]
        masks, _ = _selection_masks([score_ref[rows, :] for rows in parts], thr, need,
                                    jnp.zeros((1, T), F32), tri_ref[...])
        for rows, mask in zip(parts, masks):
            mask_ref[rows, :] = mask

    def logits(c, g):
        kc = kb_ref[span(c), g * HEAD_DIM:(g + 1) * HEAD_DIM]
        qg = qh_ref[g * GROUP:(g + 1) * GROUP].reshape(W, HEAD_DIM)
        s = lax.dot_general(kc, qg, (((1,), (1,)), ((), ())), preferred_element_type=F32)
        mb = mask_ref[span(c), :]
        if c == 0:
            add = jnp.concatenate([mb + bias_ref[g * GROUP + hq] for hq in range(GROUP)], axis=1)
        else:
            add = jnp.concatenate([mb] * GROUP, axis=1)
        s = s + add
        s_ref[span(c), g * W:(g + 1) * W] = s
        return jnp.max(_fold_rows(s, jnp.maximum), axis=0, keepdims=True)

    def run(n):
        for c in range(n):
            score_chunk(c)
        if n == 1:
            pl.when(i * T + T <= TOPK)(select_all)
            pl.when(i * T + T > TOPK)(functools.partial(select, n))
        else:
            select(n)
        for g in range(N_KV_HEADS):
            m = functools.reduce(jnp.maximum, [logits(c, g) for c in range(n)])
            acc = None
            for c in range(n):
                p = jnp.exp2((s_ref[span(c), g * W:(g + 1) * W] - m).astype(BF16))
                pv = jnp.dot(vt_ref[g, :, span(c)], p, preferred_element_type=F32)
                acc = pv if acc is None else acc + pv
            o = acc[0:HEAD_DIM] / acc[HEAD_DIM:HEAD_DIM + 1]
            for hq in range(GROUP):
                h = g * GROUP + hq
                o_ref[0, :, h * HEAD_DIM:(h + 1) * HEAD_DIM] = o[:, hq * T:(hq + 1) * T].T

    for n in range(1, SEQ // CHUNK + 1):
        pl.when(nch == n)(functools.partial(run, n))


def _attn_prompt(hb_p, hf_p, bias_p):
    hb3 = hb_p.reshape(BATCH, SEQ, D_HB)
    hf3 = hf_p.reshape(BATCH, SEQ, D_HF)
    nqb = SEQ // Q_BLOCK
    T = Q_BLOCK
    qi_w = IDX_HEADS * IDX_DIM
    return pl.pallas_call(
        _attn_p_kernel,
        grid=(BATCH, nqb),
        in_specs=[
            pl.BlockSpec((1, T, D_ATTN), lambda b, i: (b, i, C_Q // D_ATTN)),
            pl.BlockSpec((1, T, qi_w), lambda b, i: (b, i, F_QI // qi_w)),
            pl.BlockSpec((1, T, LANE), lambda b, i: (b, i, F_KI // LANE)),
            pl.BlockSpec((1, SEQ, D_KV), lambda b, i: (b, 0, F_K // D_KV)),
            pl.BlockSpec((1, SEQ, D_KV), lambda b, i: (b, 0, F_V // D_KV)),
            pl.BlockSpec((1, SEQ, LANE), lambda b, i: (b, 0, F_KI // LANE)),
            pl.BlockSpec((N_HEADS, CHUNK, T), lambda b, i: (0, 0, 0)),
            pl.BlockSpec((MXU_WIDTH, MXU_WIDTH), lambda b, i: (0, 0)),
        ],
        out_specs=pl.BlockSpec((1, T, D_ATTN), lambda b, i: (b, i, 0)),
        out_shape=jax.ShapeDtypeStruct((BATCH, SEQ, D_ATTN), F32),
        scratch_shapes=[
            pltpu.VMEM((PAD_KEYS + SEQ, D_KV), BF16),
            pltpu.VMEM((N_KV_HEADS, VT_ROWS, PAD_KEYS + SEQ), BF16),
            pltpu.VMEM((PAD_KEYS + SEQ, IDX_DIM), BF16),
            pltpu.VMEM((N_HEADS, T, HEAD_DIM), BF16),
            pltpu.VMEM((IDX_HEADS, T, IDX_DIM), BF16),
            pltpu.VMEM((PAD_KEYS + SEQ, T), F32),
            pltpu.VMEM((PAD_KEYS + SEQ, T), F32),
            pltpu.VMEM((PAD_KEYS + SEQ, N_HEADS * T), F32),
        ],
        compiler_params=_params(("parallel", "arbitrary")),
        name="attn_prompt",
    )(hb3, hf3, hf3, hf3, hf3, hf3, bias_p, jnp.tri(MXU_WIDTH, dtype=BF16))


def _score_s_kernel(pt_ref, qi_ref, kin_ref, *rest):
    npg = SCORE_ROWS * N_PAGES
    kip = rest[0:npg]
    o_ref = rest[npg]
    kinp_ref = rest[npg + 1]
    del pt_ref
    R = DEC_SEQ
    qrow = lax.broadcasted_iota(jnp.int32, (R, LANE), 0)
    lane = lax.broadcasted_iota(jnp.int32, (R, LANE), 1)
    kinp_ref[...] = jnp.zeros(kinp_ref.shape, BF16)
    for r in range(SCORE_ROWS):
        kinp_ref[r, 0:2 * R] = jnp.concatenate(
            [kin_ref[r][:, :IDX_DIM], jnp.zeros((R, IDX_DIM), F32)], 0).astype(BF16)
    for r in range(SCORE_ROWS):
        qi = jnp.concatenate([qi_ref[r][:, h * IDX_DIM:(h + 1) * IDX_DIM]
                              for h in range(IDX_HEADS)], axis=0).astype(BF16)
        w = kin_ref[r][:, W_OFF:W_OFF + IDX_HEADS] * SCORE_SCALE
        wb = jnp.broadcast_to(jnp.concatenate([w[:, h:h + 1] for h in range(IDX_HEADS)], axis=0),
                              (IDX_HEADS * R, LANE))
        for t in range(N_PAGES + 1):
            if t < N_PAGES:
                d = jnp.dot(qi, kip[r * N_PAGES + t][0].astype(BF16), preferred_element_type=F32)
            else:
                d = lax.dot_general(qi, kinp_ref[r], (((1,), (1,)), ((), ())),
                                    preferred_element_type=F32)
            e = (jnp.maximum(d, 0.0) * wb).reshape(IDX_HEADS, R, LANE)
            s = e[0]
            for h in range(1, IDX_HEADS):
                s = s + e[h]
            if t == N_PAGES:
                s = jnp.where(lane <= qrow, s, NEG_INF)
            o_ref[r, :, t * LANE:(t + 1) * LANE] = s


def _score_sample(hf_s, page_table, cache_kidx_t):
    R = DEC_SEQ
    G = SCORE_ROWS
    h3 = hf_s.reshape(DEC_BATCH, R, D_HF)
    qi_w = IDX_HEADS * IDX_DIM
    in_specs = [
        pl.BlockSpec((G, R, qi_w), lambda b, pt: (b, 0, F_QI // qi_w)),
        pl.BlockSpec((G, R, LANE), lambda b, pt: (b, 0, F_KI // LANE)),
    ]
    in_specs += [pl.BlockSpec((1, IDX_DIM, PAGE_SIZE),
                              lambda b, pt, r=r, p=p: (pt[b * G + r, p], 0, 0))
                 for r in range(G) for p in range(N_PAGES)]
    grid_spec = pltpu.PrefetchScalarGridSpec(
        num_scalar_prefetch=1,
        grid=(DEC_BATCH // G,),
        in_specs=in_specs,
        out_specs=pl.BlockSpec((G, R, L_SAMPLE), lambda b, pt: (b, 0, 0)),
        scratch_shapes=[pltpu.VMEM((G, PAGE_SIZE, IDX_DIM), BF16)],
    )
    return pl.pallas_call(
        _score_s_kernel,
        grid_spec=grid_spec,
        out_shape=jax.ShapeDtypeStruct((DEC_BATCH, R, L_SAMPLE), F32),
        compiler_params=_params(("arbitrary",)),
        name="score_sample",
    )(page_table, h3, h3, *([cache_kidx_t] * (G * N_PAGES)))


def _select_s_kernel(s_ref, tri_ref, o_ref, st_ref):
    nt = N_PAGES + 1
    parts = [slice(j * LANE, (j + 1) * LANE) for j in range(nt)]
    for cols in parts:
        st_ref[cols, :] = s_ref[:, cols].T

    def tile_fn(j):
        return st_ref[parts[j], :].reshape(SUBTILES, SUBLANE, LANE)

    thr, need = _select_threshold(tile_fn, nt)
    masks, _ = _selection_masks([st_ref[rows, :] for rows in parts], thr, need,
                                jnp.zeros((1, LANE), F32), tri_ref[...])
    for cols, mask in zip(parts, masks):
        o_ref[:, cols] = mask.T


def _select_sample(scores):
    n = scores.shape[0]
    return pl.pallas_call(
        _select_s_kernel,
        grid=(n // LANE,),
        in_specs=[pl.BlockSpec((LANE, L_SAMPLE), lambda c: (c, 0)),
                  pl.BlockSpec((LANE, LANE), lambda c: (0, 0))],
        out_specs=pl.BlockSpec((LANE, L_SAMPLE), lambda c: (c, 0)),
        out_shape=jax.ShapeDtypeStruct((n, L_SAMPLE), F32),
        scratch_shapes=[pltpu.VMEM((L_SAMPLE, LANE), F32)],
        compiler_params=_params(("parallel",)),
        name="select_sample",
    )(scores, jnp.tri(LANE, dtype=BF16))


def _attn_s_kernel(pt_ref, q_ref, kn_ref, vn_ref, mask_ref, bias_ref, *rest):
    npg = ATTN_ROWS * N_PAGES
    kp = rest[0:npg]
    vp = rest[npg:2 * npg]
    o_ref = rest[2 * npg]
    knp_ref, vnp_ref, logit_ref = rest[2 * npg + 1:]
    del pt_ref
    R = DEC_SEQ
    NT = N_PAGES + 1
    GR = GROUP * R

    knp_ref[...] = jnp.zeros(knp_ref.shape, BF16)
    vnp_ref[...] = jnp.zeros(vnp_ref.shape, BF16)
    for r in range(ATTN_ROWS):
        knp_ref[r, 0:2 * R] = jnp.concatenate([kn_ref[r], jnp.zeros((R, D_KV), F32)], 0).astype(BF16)
        vnp_ref[r, 0:2 * R] = jnp.concatenate([vn_ref[r], jnp.zeros((R, D_KV), F32)], 0).astype(BF16)

    def page_head(refs, pad_ref, r, t, g):
        if t < N_PAGES:
            return refs[r * N_PAGES + t][pl.ds(g, PAGE_SIZE, stride=N_KV_HEADS), :].astype(BF16)
        return pad_ref[r, :, g * HEAD_DIM:(g + 1) * HEAD_DIM]

    for r in range(ATTN_ROWS):
        qf = q_ref[r].astype(F32) * QK_SCALE
        q = jnp.concatenate([qf[:, h * HEAD_DIM:(h + 1) * HEAD_DIM] for h in range(N_HEADS)],
                            axis=0).astype(BF16)
        for t in range(NT):
            mb = mask_ref[r, :, t * LANE:(t + 1) * LANE]
            for g in range(N_KV_HEADS):
                lg = lax.dot_general(q[g * GR:(g + 1) * GR], page_head(kp, knp_ref, r, t, g),
                                     (((1,), (1,)), ((), ())), preferred_element_type=F32)
                lg = lg.reshape(GROUP, R, LANE) + mb[None]
                if t >= N_PAGES - 1:
                    off = (t - (N_PAGES - 1)) * LANE
                    lg = lg + bias_ref[g * GROUP:(g + 1) * GROUP, :, off:off + LANE]
                logit_ref[r, g * GR:(g + 1) * GR, t * LANE:(t + 1) * LANE] = lg.reshape(GR, LANE)

    for r in range(ATTN_ROWS):
        logits = logit_ref[r]
        m = jnp.max(logits, axis=1, keepdims=True)
        p = jnp.exp(logits - m)
        inv = 1.0 / jnp.sum(p, axis=1, keepdims=True)
        pb = p.astype(BF16)
        outs = [jnp.zeros((GR, HEAD_DIM), F32) for _ in range(N_KV_HEADS)]
        for t in range(NT):
            for g in range(N_KV_HEADS):
                outs[g] = outs[g] + jnp.dot(pb[g * GR:(g + 1) * GR, t * LANE:(t + 1) * LANE],
                                            page_head(vp, vnp_ref, r, t, g),
                                            preferred_element_type=F32)
        for g in range(N_KV_HEADS):
            o = outs[g] * inv[g * GR:(g + 1) * GR]
            for hq in range(GROUP):
                h = g * GROUP + hq
                o_ref[r, :, h * HEAD_DIM:(h + 1) * HEAD_DIM] = o[hq * R:(hq + 1) * R]


def _attn_sample(hb_s, hf_s, page_table, cache_k, cache_v, mask, bias_s):
    R = DEC_SEQ
    G = ATTN_ROWS
    h3 = hf_s.reshape(DEC_BATCH, R, D_HF)
    hb3 = hb_s.reshape(DEC_BATCH, R, D_HB)
    rows_per_page = PAGE_SIZE * N_KV_HEADS
    ck = cache_k.reshape(-1, HEAD_DIM)
    cv = cache_v.reshape(-1, HEAD_DIM)

    in_specs = [
        pl.BlockSpec((G, R, D_ATTN), lambda b, pt: (b, 0, C_Q // D_ATTN)),
        pl.BlockSpec((G, R, D_KV), lambda b, pt: (b, 0, F_K // D_KV)),
        pl.BlockSpec((G, R, D_KV), lambda b, pt: (b, 0, F_V // D_KV)),
        pl.BlockSpec((G, R, L_SAMPLE), lambda b, pt: (b, 0, 0)),
        pl.BlockSpec((N_HEADS, R, 2 * LANE), lambda b, pt: (0, 0, 0)),
    ]
    pages = [pl.BlockSpec((rows_per_page, HEAD_DIM), lambda b, pt, r=r, p=p: (pt[b * G + r, p], 0))
             for r in range(G) for p in range(N_PAGES)]
    in_specs += pages + pages
    grid_spec = pltpu.PrefetchScalarGridSpec(
        num_scalar_prefetch=1,
        grid=(DEC_BATCH // G,),
        in_specs=in_specs,
        out_specs=pl.BlockSpec((G, R, D_ATTN), lambda b, pt: (b, 0, 0)),
        scratch_shapes=[
            pltpu.VMEM((G, PAGE_SIZE, D_KV), BF16),
            pltpu.VMEM((G, PAGE_SIZE, D_KV), BF16),
            pltpu.VMEM((G, N_HEADS * R, L_SAMPLE), F32),
        ],
    )
    return pl.pallas_call(
        _attn_s_kernel,
        grid_spec=grid_spec,
        out_shape=jax.ShapeDtypeStruct((DEC_BATCH, R, D_ATTN), F32),
        compiler_params=_params(("arbitrary",)),
        name="attn_sample",
    )(page_table, hb3, h3, h3, mask, bias_s, *([ck] * (G * N_PAGES)), *([cv] * (G * N_PAGES)))


def _conv_p_kernel(val_ref, gate_ref, cw_ref, cb_ref, dw_ref, ut_ref, pad_ref):
    pad_ref[0:PAD_ROWS] = jnp.zeros((PAD_ROWS, LANE), F32)
    pad_ref[PAD_ROWS:] = val_ref[0].astype(F32) * jax.nn.sigmoid(gate_ref[0].astype(F32))
    ut_ref[0] = pad_ref[SEQ:SEQ + PAD_ROWS]
    cw = cw_ref[...]
    cb = cb_ref[...]
    first = PAD_ROWS - (CONV_WIDTH - 1)
    for c in range(SEQ // CONV_CHUNK):
        base = c * CONV_CHUNK
        acc = jnp.broadcast_to(cb, (CONV_CHUNK, LANE))
        for r in range(SUBLANE):
            taps = [w for w in range(CONV_WIDTH) if (first + w) % SUBLANE == r]
            span = max(first + w - r for w in taps) + CONV_CHUNK
            win = pad_ref[base + r:base + r + span]
            for w in taps:
                a = first + w - r
                acc = acc + win[a:a + CONV_CHUNK] * cw[w:w + 1]
        dw_ref[0, base:base + CONV_CHUNK] = acc


def _conv_prompt(hb_p, cw_pad, cb):
    h3 = hb_p.reshape(BATCH, SEQ, D_HB)
    nc = D_CONV // LANE
    return pl.pallas_call(
        _conv_p_kernel,
        grid=(BATCH, nc),
        in_specs=[pl.BlockSpec((1, SEQ, LANE), lambda b, c: (b, 0, C_GV // LANE + c)),
                  pl.BlockSpec((1, SEQ, LANE), lambda b, c: (b, 0, C_GG // LANE + c)),
                  pl.BlockSpec((PAD_ROWS, LANE), lambda b, c: (0, c)),
                  pl.BlockSpec((1, LANE), lambda b, c: (0, c))],
        out_specs=[pl.BlockSpec((1, SEQ, LANE), lambda b, c: (b, 0, c)),
                   pl.BlockSpec((1, PAD_ROWS, LANE), lambda b, c: (b, 0, c))],
        out_shape=[jax.ShapeDtypeStruct((BATCH, SEQ, D_CONV), F32),
                   jax.ShapeDtypeStruct((BATCH, PAD_ROWS, D_CONV), F32)],
        scratch_shapes=[pltpu.VMEM((PAD_ROWS + SEQ, LANE), F32)],
        compiler_params=_params(("parallel", "parallel")),
        name="conv_prompt",
    )(h3, h3, cw_pad, cb)


def _conv_s_kernel(val_ref, gate_ref, st_ref, cw_ref, cb_ref, dw_ref, ns_ref, glu_ref):
    R = DEC_SEQ
    H = CONV_WIDTH - 1
    cw = cw_ref[...]
    cb = jnp.broadcast_to(cb_ref[...], (DEC_BATCH, LANE))
    glu_ref[...] = val_ref[...].astype(F32) * jax.nn.sigmoid(gate_ref[...].astype(F32))
    u = []
    for q in range(R):
        u.append(glu_ref[pl.ds(q, DEC_BATCH, stride=R), :])

    def row(r):
        return st_ref[r] if r < H else u[r - H]

    for q in range(R):
        acc = cb
        for w in range(CONV_WIDTH):
            acc = acc + row(q + w) * cw[w:w + 1]
        dw_ref[pl.ds(q, DEC_BATCH, stride=R), :] = acc
    for r in range(H):
        ns_ref[r] = row(r + R)


def _conv_sample(hb_s, state_t, cw_pad, cb):
    n_s = DEC_BATCH * DEC_SEQ
    H = CONV_WIDTH - 1
    nc = D_CONV // LANE
    return pl.pallas_call(
        _conv_s_kernel,
        grid=(nc,),
        in_specs=[pl.BlockSpec((n_s, LANE), lambda c: (0, C_GV // LANE + c)),
                  pl.BlockSpec((n_s, LANE), lambda c: (0, C_GG // LANE + c)),
                  pl.BlockSpec((H, DEC_BATCH, LANE), lambda c: (0, 0, c)),
                  pl.BlockSpec((PAD_ROWS, LANE), lambda c: (0, c)),
                  pl.BlockSpec((1, LANE), lambda c: (0, c))],
        out_specs=[pl.BlockSpec((n_s, LANE), lambda c: (0, c)),
                   pl.BlockSpec((H, DEC_BATCH, LANE), lambda c: (0, 0, c))],
        out_shape=[jax.ShapeDtypeStruct((n_s, D_CONV), F32),
                   jax.ShapeDtypeStruct((H, DEC_BATCH, D_CONV), F32)],
        scratch_shapes=[pltpu.VMEM((n_s, LANE), F32)],
        compiler_params=_params(("parallel",)),
        name="conv_sample",
    )(hb_s, hb_s, state_t, cw_pad, cb)


def _tail_kernel(attn_ref, za_ref, dw_ref, zc_ref, ga_ref, gc_ref, x_ref,
                 wua_ref, wpw_ref, wuc_ref, wo_ref, ng_ref, nb_ref, bpw_ref, fg_ref, y_ref):
    a = attn_ref[...] * jax.nn.silu(za_ref[...].astype(F32))
    branch_attn = jnp.dot(a.astype(BF16), wua_ref[...], preferred_element_type=F32)

    dw = dw_ref[...]
    mu = jnp.mean(dw, axis=-1, keepdims=True)
    var = jnp.mean(jnp.square(dw - mu), axis=-1, keepdims=True)
    ln = (dw - mu) * lax.rsqrt(var + EPS) * ng_ref[...] + nb_ref[...]
    conv_out = jnp.dot(jax.nn.silu(ln).astype(BF16), wpw_ref[...],
                       preferred_element_type=F32) + bpw_ref[...]
    c = conv_out * jax.nn.silu(zc_ref[...].astype(F32))
    branch_conv = jnp.dot(c.astype(BF16), wuc_ref[...], preferred_element_type=F32)

    merged = (jax.nn.sigmoid(ga_ref[...].astype(F32)) * branch_attn
              + jax.nn.sigmoid(gc_ref[...].astype(F32)) * branch_conv)
    y = x_ref[...] + jnp.dot(merged.astype(BF16), wo_ref[...], preferred_element_type=F32)
    ms = jnp.mean(y * y, axis=-1, keepdims=True)
    y_ref[...] = y * lax.rsqrt(ms + EPS) * fg_ref[...]


def _tail(h, attn, dw, x2d, weights, tm):
    n = x2d.shape[0]

    def const(shape):
        return pl.BlockSpec(shape, lambda i: (0, 0), pipeline_mode=pl.Buffered(1))

    def cols(width, offset):
        return pl.BlockSpec((tm, width), lambda i: (i, offset // width))

    return pl.pallas_call(
        _tail_kernel,
        grid=(n // tm,),
        in_specs=[cols(D_ATTN, 0), cols(D_ATTN, C_ZA), cols(D_CONV, 0), cols(D_CONV, C_ZC),
                  cols(D_MODEL, C_GA), cols(D_MODEL, C_GC), cols(D_MODEL, 0),
                  const((D_ATTN, D_MODEL)), const((D_CONV, D_CONV)),
                  const((D_CONV, D_MODEL)), const((D_MODEL, D_MODEL)),
                  const((1, D_CONV)), const((1, D_CONV)), const((1, D_CONV)),
                  const((1, D_MODEL))],
        out_specs=pl.BlockSpec((tm, D_MODEL), lambda i: (i, 0)),
        out_shape=jax.ShapeDtypeStruct((n, D_MODEL), F32),
        compiler_params=_params(("parallel",)),
        name="tail",
    )(attn, h, dw, h, h, h, x2d, *weights)


def _reorder_kernel(off_ref, w_ref, o_ref):
    del off_ref
    o_ref[...] = w_ref[...].astype(BF16)


def _reorder_w_in(w_in):
    w_t = w_in.T
    src = []
    for first, dest, width in (('gate_attn', C_GA, 2 * D_MODEL), ('q', C_Q, D_ATTN),
                               ('z_attn', C_ZA, D_ATTN), ('glu_val', C_GV, 3 * D_CONV),
                               ('k', C_K, 2 * D_KV), ('q_idx', C_QI, D_H - C_QI)):
        assert dest == len(src) * REORDER_ROWS
        for r in range(0, width, REORDER_ROWS):
            src.append(min(_SRC[first][0] + r, D_IN - REORDER_ROWS) // SUBLANE)
    grid_spec = pltpu.PrefetchScalarGridSpec(
        num_scalar_prefetch=1,
        grid=(len(src),),
        in_specs=[pl.BlockSpec((pl.Element(REORDER_ROWS), pl.Element(D_MODEL)),
                               lambda d, off: (off[d] * SUBLANE, 0))],
        out_specs=pl.BlockSpec((REORDER_ROWS, D_MODEL), lambda d, off: (d, 0)),
    )
    return pl.pallas_call(
        _reorder_kernel,
        grid_spec=grid_spec,
        out_shape=jax.ShapeDtypeStruct((D_H, D_MODEL), BF16),
        compiler_params=_params(("arbitrary",)),
        name="reorder_w_in",
    )(jnp.asarray(src, jnp.int32), w_t)


def kernel(x_prompt, x_sample, cache_k, cache_v, cache_kidx, state_conv, page_table,
           ln_g, w_in, conv_w, conv_b, conv_norm_g, conv_norm_b, w_pw, b_pw,
           w_up_attn, w_up_conv, w_out, rel_bias, final_g):
    w_all = _reorder_w_in(w_in[0])
    g_in = ln_g[0].reshape(1, D_MODEL)
    xp = x_prompt.reshape(BATCH * SEQ, D_MODEL)
    xs = x_sample.reshape(DEC_BATCH * DEC_SEQ, D_MODEL)
    hb_p, hf_p, k_p, v_p, kit_p = _proj(xp, g_in, w_all, SEQ)
    hb_s, hf_s, k_s, v_s, _ = _proj(xs, g_in, w_all, DEC_BATCH * DEC_SEQ)

    bias_p, bias_s = _bias_tables(rel_bias)

    attn_p = _attn_prompt(hb_p, hf_p, bias_p).reshape(BATCH * SEQ, D_ATTN)

    n_s = DEC_BATCH * DEC_SEQ
    scores = _score_sample(hf_s, page_table, cache_kidx[0].transpose(0, 2, 1))
    mask = _select_sample(scores.reshape(n_s, L_SAMPLE)).reshape(DEC_BATCH, DEC_SEQ, L_SAMPLE)
    attn_s = _attn_sample(hb_s, hf_s, page_table, cache_k[0], cache_v[0], mask, bias_s)
    attn_s = attn_s.reshape(n_s, D_ATTN)

    cw_pad = jnp.concatenate([conv_w[0], jnp.zeros((PAD_ROWS - CONV_WIDTH, D_CONV), F32)], 0)
    cb = conv_b[0].reshape(1, D_CONV)
    dw_p, u_tail = _conv_prompt(hb_p, cw_pad, cb)
    dw_s, state_new = _conv_sample(hb_s, state_conv[0].transpose(1, 0, 2), cw_pad, cb)

    wua = w_up_attn[0].astype(BF16)
    wpw = w_pw[0].astype(BF16)
    wuc = w_up_conv[0].astype(BF16)
    wo = w_out[0].astype(BF16)
    ng = conv_norm_g[0].reshape(1, D_CONV)
    nb = conv_norm_b[0].reshape(1, D_CONV)
    bpw = b_pw[0].reshape(1, D_CONV)
    fg = final_g.reshape(1, D_MODEL)
    weights = (wua, wpw, wuc, wo, ng, nb, bpw, fg)
    y_p = _tail(hb_p, attn_p, dw_p.reshape(BATCH * SEQ, D_CONV), xp, weights, TAIL_TM)
    y_s = _tail(hb_s, attn_s, dw_s, xs, weights, TAIL_TM)

    tail_rows = CONV_WIDTH - 1
    return (
        y_p.reshape(BATCH, SEQ, D_MODEL),
        y_s.reshape(DEC_BATCH, DEC_SEQ, D_MODEL),
        k_p.reshape(1, BATCH, SEQ, N_KV_HEADS, HEAD_DIM),
        v_p.reshape(1, BATCH, SEQ, N_KV_HEADS, HEAD_DIM),
        kit_p.transpose(0, 2, 1).reshape(1, BATCH, SEQ, IDX_DIM),
        u_tail[:, PAD_ROWS - tail_rows:].reshape(1, BATCH, tail_rows, D_CONV),
        k_s.reshape(1, DEC_BATCH, DEC_SEQ, N_KV_HEADS, HEAD_DIM),
        v_s.reshape(1, DEC_BATCH, DEC_SEQ, N_KV_HEADS, HEAD_DIM),
        hf_s[:, F_KI:F_KI + IDX_DIM].reshape(1, DEC_BATCH, DEC_SEQ, IDX_DIM),
        state_new.transpose(1, 0, 2).reshape(1, DEC_BATCH, tail_rows, D_CONV),
    )
```

```python
import functools
import math

import numpy as np
import jax
import jax.numpy as jnp
from jax import lax
from jax.experimental import pallas as pl
from jax.experimental.pallas import tpu as pltpu

F32 = jnp.float32
BF16 = jnp.bfloat16

D_MODEL = 2048
BATCH = 8
SEQ = 2048
DEC_BATCH = 128
DEC_SEQ = 8
PAST_LEN = 2048
PAGE_SIZE = 128
N_PAGES = PAST_LEN // PAGE_SIZE
N_HEADS = 8
N_KV_HEADS = 2
HEAD_DIM = 128
GROUP = N_HEADS // N_KV_HEADS
D_ATTN = N_HEADS * HEAD_DIM
D_KV = N_KV_HEADS * HEAD_DIM
IDX_HEADS = 8
IDX_DIM = 64
TOPK = 256
Q_BLOCK = 128
N_BUCKETS = 32
MAX_DISTANCE = 128
D_CONV = D_MODEL // 2
CONV_WIDTH = 31
EPS = 1e-6
NEG = -1e30
NEG_INF = float("-inf")

LANE = 128
SUBLANE = 8
VMEM_LIMIT = 56 * 1024 * 1024

C_GA = 0
C_GC = C_GA + D_MODEL
C_Q = C_GC + D_MODEL
C_ZA = C_Q + D_ATTN
C_GV = C_ZA + D_ATTN
C_GG = C_GV + D_CONV
C_ZC = C_GG + D_CONV
C_K = C_ZC + D_CONV
C_V = C_K + D_KV
C_QI = C_V + D_KV
C_KI = C_QI + IDX_HEADS * IDX_DIM
W_OFF = IDX_DIM
MXU_WIDTH = 256
PROJ_TM = 1024
PROJ_TN = 6 * MXU_WIDTH
D_H = -(-(C_KI + LANE) // PROJ_TN) * PROJ_TN
KV_TILE = C_K // PROJ_TN
assert C_K % PROJ_TN == 0 and D_H == C_K + PROJ_TN
D_HB = C_K
D_HF = C_KI + LANE - C_K
F_K = C_K - D_HB
F_V = C_V - D_HB
F_QI = C_QI - D_HB
F_KI = C_KI - D_HB

_SRC = {}
_off = 0
for _name, _w in (('q', D_ATTN), ('k', D_KV), ('v', D_KV), ('z_attn', D_ATTN),
                  ('q_idx', IDX_HEADS * IDX_DIM), ('k_idx', IDX_DIM), ('w_idx', IDX_HEADS),
                  ('glu_val', D_CONV), ('glu_gate', D_CONV), ('z_conv', D_CONV),
                  ('gate_attn', D_MODEL), ('gate_conv', D_MODEL)):
    _SRC[_name] = (_off, _w)
    _off += _w
D_IN = _off
REORDER_ROWS = 512

N_BISECT = 20
PAD_ROWS = 32
CONV_CHUNK = 64
SCORE_SCALE = (IDX_DIM ** -0.5) * (IDX_HEADS ** -0.5)
QK_SCALE = HEAD_DIM ** -0.5
LOG2E = math.log2(math.e)
L_SAMPLE = (N_PAGES + 1) * LANE
SUBTILES = LANE // SUBLANE
CHUNK_BLOCKS = 2
CHUNK = CHUNK_BLOCKS * Q_BLOCK
PAD_KEYS = CHUNK - Q_BLOCK
VT_ROWS = HEAD_DIM + 2 * SUBLANE
SCORE_ROWS = 8
ATTN_ROWS = 4
TAIL_TM = 512


def _t5_bucket_static(dist):
    n = np.maximum(dist, 0)
    max_exact = N_BUCKETS // 2
    ratio = (np.log(np.maximum(n, 1).astype(np.float32) / np.float32(max_exact))
             / np.float32(math.log(MAX_DISTANCE / max_exact)))
    large = np.minimum(max_exact + (ratio * np.float32(N_BUCKETS - max_exact)).astype(np.int32),
                       N_BUCKETS - 1)
    return np.where(n < max_exact, n, large).astype(np.int32)


FAR_BUCKET = int(_t5_bucket_static(np.array([2 * MAX_DISTANCE]))[0])


def _params(sem):
    return pltpu.CompilerParams(dimension_semantics=sem, vmem_limit_bytes=VMEM_LIMIT)


def _proj_kernel(x_ref, g_ref, w_ref, ob_ref, of_ref, k_ref, v_ref, kit_ref, xn_ref):
    tm = x_ref.shape[0]
    j = pl.program_id(1)

    @pl.when(j == 0)
    def _():
        x = x_ref[...]
        ms = jnp.mean(x * x, axis=-1, keepdims=True)
        xn_ref[...] = (x * lax.rsqrt(ms + EPS) * g_ref[...]).astype(BF16)

    def project():
        return lax.dot_general(xn_ref[...], w_ref[...], (((1,), (1,)), ((), ())),
                               preferred_element_type=F32)

    @pl.when(j < KV_TILE)
    def _():
        ob_ref[...] = project().astype(BF16)

    @pl.when(j == KV_TILE)
    def _():
        of_ref[...] = project()[:, :D_HF]
        for g in range(N_KV_HEADS):
            rows = pl.ds(g, tm, stride=N_KV_HEADS)
            k_ref[rows, :] = of_ref[:, F_K + g * HEAD_DIM:F_K + (g + 1) * HEAD_DIM]
            v_ref[rows, :] = of_ref[:, F_V + g * HEAD_DIM:F_V + (g + 1) * HEAD_DIM]
        for c in range(tm // LANE):
            tile = of_ref[c * LANE:(c + 1) * LANE, F_KI:F_KI + LANE].T
            kit_ref[0, :, c * LANE:(c + 1) * LANE] = tile[0:IDX_DIM]


def _proj(x2d, g, w, seq):
    n = x2d.shape[0]
    tm, tn = PROJ_TM, PROJ_TN
    tiles = seq // tm
    kv_spec = pl.BlockSpec((N_KV_HEADS * tm, HEAD_DIM), lambda i, j: (i, 0))
    kv_shape = jax.ShapeDtypeStruct((N_KV_HEADS * n, HEAD_DIM), F32)
    return pl.pallas_call(
        _proj_kernel,
        grid=(n // tm, D_H // tn),
        in_specs=[pl.BlockSpec((tm, D_MODEL), lambda i, j: (i, 0)),
                  pl.BlockSpec((1, D_MODEL), lambda i, j: (0, 0)),
                  pl.BlockSpec((tn, D_MODEL), lambda i, j: (j, 0))],
        out_specs=[pl.BlockSpec((tm, tn), lambda i, j: (i, jnp.minimum(j, KV_TILE - 1))),
                   pl.BlockSpec((tm, D_HF), lambda i, j: (i, 0)), kv_spec, kv_spec,
                   pl.BlockSpec((1, IDX_DIM, tm), lambda i, j: (i // tiles, 0, i % tiles))],
        out_shape=[jax.ShapeDtypeStruct((n, D_HB), BF16),
                   jax.ShapeDtypeStruct((n, D_HF), F32), kv_shape, kv_shape,
                   jax.ShapeDtypeStruct((n // seq, IDX_DIM, seq), F32)],
        scratch_shapes=[pltpu.VMEM((tm, D_MODEL), BF16)],
        compiler_params=_params(("parallel", "arbitrary")),
        name="proj",
    )(x2d, g, w)


def _bias_kernel(rb_ref, bp_ref, bs_ref, op_ref, os_ref):
    bp = bp_ref[...]
    bs = bs_ref[...]
    for h in range(N_HEADS):
        far = rb_ref[FAR_BUCKET, h]
        tp = jnp.zeros(bp.shape, F32)
        ts = jnp.zeros(bs.shape, F32)
        for b in range(N_BUCKETS):
            val = rb_ref[b, h] - far
            tp = jnp.where(bp == b, val * LOG2E, tp)
            ts = jnp.where(bs == b, val, ts)
        op_ref[h] = tp
        os_ref[h] = ts


def _bias_tables(rel_bias):
    key = np.arange(CHUNK)[:, None]
    qry = np.arange(Q_BLOCK)[None, :]
    bucket_p = _t5_bucket_static(CHUNK - Q_BLOCK + qry - key)
    qi = np.arange(DEC_SEQ)[:, None]
    col = np.arange(2 * LANE)[None, :]
    dist_s = np.where(col < LANE, LANE + qi - col, qi - (col - LANE))
    bucket_s = _t5_bucket_static(dist_s)
    return pl.pallas_call(
        _bias_kernel,
        in_specs=[pl.BlockSpec(memory_space=pltpu.SMEM),
                  pl.BlockSpec(memory_space=pltpu.VMEM),
                  pl.BlockSpec(memory_space=pltpu.VMEM)],
        out_specs=[pl.BlockSpec(memory_space=pltpu.VMEM),
                   pl.BlockSpec(memory_space=pltpu.VMEM)],
        out_shape=[jax.ShapeDtypeStruct((N_HEADS, CHUNK, Q_BLOCK), F32),
                   jax.ShapeDtypeStruct((N_HEADS, DEC_SEQ, 2 * LANE), F32)],
        name="bias_tables",
    )(rel_bias, jnp.asarray(bucket_p), jnp.asarray(bucket_s))


def _any(x):
    return jnp.max(jnp.where(x, 1.0, 0.0)) > 0.5


def _rep(x):
    return jnp.broadcast_to(x, (SUBLANE, LANE))


def _fold_rows(x, comb):
    parts = [x[k:k + SUBLANE] for k in range(0, x.shape[0], SUBLANE)]
    while len(parts) > 1:
        parts = [comb(parts[k], parts[k + 1]) for k in range(0, len(parts), 2)]
    return parts[0]


def _select_threshold(tile_fn, ntiles):
    def reduce_tiles(fn, init, comb):
        acc = init
        for j in range(ntiles):
            x = fn(tile_fn(j), j)
            parts = [x[k] for k in range(x.shape[0])]
            while len(parts) > 1:
                parts = [comb(parts[k], parts[k + 1]) for k in range(0, len(parts), 2)]
            acc = comb(acc, parts[0])
        return acc

    zeros = jnp.zeros((SUBLANE, LANE), F32)

    def count(pred_fn):
        acc = reduce_tiles(lambda s, j: jnp.where(pred_fn(s, j), 1.0, 0.0), zeros,
                           lambda a, b: a + b)
        return _rep(jnp.sum(acc, axis=0, keepdims=True))

    def masked_max(pred_fn):
        acc = reduce_tiles(lambda s, j: jnp.where(pred_fn(s, j), s, NEG_INF),
                           jnp.full((SUBLANE, LANE), NEG_INF, F32), jnp.maximum)
        return _rep(jnp.max(acc, axis=0, keepdims=True))

    bound = reduce_tiles(lambda s, j: jnp.where(s > NEG_INF, jnp.abs(s), 0.0), zeros,
                         jnp.maximum)
    bound = _rep(jnp.max(bound, axis=0, keepdims=True))

    def bisect(_, carry):
        lo, hi = carry
        mid = 0.5 * lo + 0.5 * hi
        few = count(lambda s, j: s > mid[None]) < TOPK
        return jnp.where(few, lo, mid), jnp.where(few, mid, hi)

    _, hi = lax.fori_loop(0, N_BISECT, bisect, (-bound, bound))

    thr = masked_max(lambda s, j: s <= hi[None])
    n_ge = count(lambda s, j: s >= thr[None])

    def fix_body(carry):
        thr, n_ge, _ = carry
        lower = masked_max(lambda s, j: s < thr[None])
        thr = jnp.where(n_ge < TOPK, lower, thr)
        n_ge = count(lambda s, j: s >= thr[None])
        return thr, n_ge, _any(n_ge < TOPK)

    thr, n_ge, _ = lax.while_loop(lambda c: c[2], fix_body, (thr, n_ge, _any(n_ge < TOPK)))
    n_gt = count(lambda s, j: s > thr[None])
    need = TOPK - n_gt
    return thr[0:1], need[0:1]


def _keep(pred):
    return jnp.where(pred, 0.0, NEG)


def _selection_masks(tiles, thr, need, seen, tri):
    ties = [s == thr for s in tiles]
    ranks = []
    for k in range(0, len(tiles), 2):
        pair = jnp.concatenate([jnp.where(t, 1.0, 0.0).astype(BF16) for t in ties[k:k + 2]], axis=1)
        rank = jnp.dot(tri, pair, preferred_element_type=F32)
        ranks += [rank[:, p * LANE:(p + 1) * LANE] for p in range(len(ties[k:k + 2]))]
    masks = []
    for s, tie, rank in zip(tiles, ties, ranks):
        masks.append(jnp.where(tie, _keep(rank + seen <= need), _keep(s > thr)))
        seen = seen + rank[s.shape[0] - 1:]
    return masks, seen


def _attn_p_kernel(q_ref, qi_ref, wi_ref, k_ref, v_ref, ki_ref, bias_ref, tri_ref, o_ref,
                   kb_ref, vt_ref, kib_ref, qh_ref, qih_ref,
                   score_ref, mask_ref, s_ref):
    i = pl.program_id(1)
    T = Q_BLOCK
    W = GROUP * T
    nch = i // CHUNK_BLOCKS + 1

    @pl.when(i == 0)
    def _():
        kb_ref[0:PAD_KEYS] = jnp.zeros((PAD_KEYS, D_KV), BF16)
        kb_ref[PAD_KEYS:] = k_ref[0].astype(BF16)
        kib_ref[0:PAD_KEYS] = jnp.zeros((PAD_KEYS, IDX_DIM), BF16)
        kib_ref[PAD_KEYS:] = ki_ref[0][:, :IDX_DIM].astype(BF16)
        ones_row = lax.broadcasted_iota(jnp.int32, (VT_ROWS - HEAD_DIM, PAD_KEYS + SEQ), 0) == 0
        for g in range(N_KV_HEADS):
            vt_ref[g, 0:HEAD_DIM, 0:PAD_KEYS] = jnp.zeros((HEAD_DIM, PAD_KEYS), BF16)
            vt_ref[g, HEAD_DIM:VT_ROWS, :] = jnp.where(ones_row, 1.0, 0.0).astype(BF16)
            for c in range(SEQ // LANE):
                blk = v_ref[0, c * LANE:(c + 1) * LANE, g * HEAD_DIM:(g + 1) * HEAD_DIM]
                vt_ref[g, 0:HEAD_DIM,
                       PAD_KEYS + c * LANE:PAD_KEYS + (c + 1) * LANE] = blk.T.astype(BF16)
        score_ref[0:PAD_KEYS] = jnp.full((PAD_KEYS, T), NEG_INF, F32)
        mask_ref[0:PAD_KEYS] = jnp.full((PAD_KEYS, T), NEG, F32)

    q = q_ref[0].astype(F32) * (QK_SCALE * LOG2E)
    for h in range(N_HEADS):
        qh_ref[h] = q[:, h * HEAD_DIM:(h + 1) * HEAD_DIM].astype(BF16)
    qi = qi_ref[0]
    for h in range(IDX_HEADS):
        qih_ref[h] = qi[:, h * IDX_DIM:(h + 1) * IDX_DIM].astype(BF16)
    w_rows = wi_ref[0].T[W_OFF:W_OFF + IDX_HEADS] * SCORE_SCALE

    def span(c):
        return pl.ds(pl.multiple_of((i - CHUNK_BLOCKS * c) * LANE, LANE), CHUNK)

    def first_key(c):
        return (i - CHUNK_BLOCKS * c) * LANE - PAD_KEYS

    key_l = lax.broadcasted_iota(jnp.int32, (CHUNK, T), 0)
    qry = i * T + lax.broadcasted_iota(jnp.int32, (CHUNK, T), 1)

    def score_chunk(c):
        kc = kib_ref[span(c), :]
        d = lax.dot_general(kc, qih_ref[...].reshape(IDX_HEADS * T, IDX_DIM),
                            (((1,), (1,)), ((), ())), preferred_element_type=F32)
        acc = jnp.zeros((CHUNK, T), F32)
        for h in range(IDX_HEADS):
            acc = acc + jnp.maximum(d[:, h * T:(h + 1) * T], 0.0) * w_rows[h:h + 1]
        key = first_key(c) + key_l
        acc = jnp.where(key <= qry, jnp.where(key >= 0, acc, NEG_INF), NEG_INF)
        score_ref[span(c), :] = acc

    nsub = CHUNK // SUBLANE

    def tile_fn(c):
        return score_ref[span(c), :].reshape(nsub, SUBLANE, T)

    def select_all():
        mask_ref[span(0), :] = _keep(score_ref[span(0), :] > NEG_INF)

    def select(n):
        thr, need = _select_threshold(tile_fn, n)
        parts = []
        for c in reversed(range(n)):
            start = pl.multiple_of((i - CHUNK_BLOCKS * c) * LANE, LANE)
            parts += [pl.ds(start + p * MXU_WIDTH, MXU_WIDTH) for p in range(CHUNK // MXU_WIDTH)]
        masks, _ = _selection_masks([score_ref[rows, :] for rows in parts], thr, need,
                                    jnp.zeros((1, T), F32), tri_ref[...])
        for rows, mask in zip(parts, masks):
            mask_ref[rows, :] = mask

    def logits(c, g):
        kc = kb_ref[span(c), g * HEAD_DIM:(g + 1) * HEAD_DIM]
        qg = qh_ref[g * GROUP:(g + 1) * GROUP].reshape(W, HEAD_DIM)
        s = lax.dot_general(kc, qg, (((1,), (1,)), ((), ())), preferred_element_type=F32)
        mb = mask_ref[span(c), :]
        if c == 0:
            add = jnp.concatenate([mb + bias_ref[g * GROUP + hq] for hq in range(GROUP)], axis=1)
        else:
            add = jnp.concatenate([mb] * GROUP, axis=1)
        s = s + add
        s_ref[span(c), g * W:(g + 1) * W] = s
        return jnp.max(_fold_rows(s, jnp.maximum), axis=0, keepdims=True)

    def run(n):
        for c in range(n):
            score_chunk(c)
        if n == 1:
            pl.when(i * T + T <= TOPK)(select_all)
            pl.when(i * T + T > TOPK)(functools.partial(select, n))
        else:
            select(n)
        for g in range(N_KV_HEADS):
            m = functools.reduce(jnp.maximum, [logits(c, g) for c in range(n)])
            acc = None
            for c in range(n):
                p = jnp.exp2((s_ref[span(c), g * W:(g + 1) * W] - m).astype(BF16))
                pv = jnp.dot(vt_ref[g, :, span(c)], p, preferred_element_type=F32)
                acc = pv if acc is None else acc + pv
            o = acc[0:HEAD_DIM] / acc[HEAD_DIM:HEAD_DIM + 1]
            for hq in range(GROUP):
                h = g * GROUP + hq
                o_ref[0, :, h * HEAD_DIM:(h + 1) * HEAD_DIM] = o[:, hq * T:(hq + 1) * T].T

    for n in range(1, SEQ // CHUNK + 1):
        pl.when(nch == n)(functools.partial(run, n))


def _attn_prompt(hb_p, hf_p, bias_p):
    hb3 = hb_p.reshape(BATCH, SEQ, D_HB)
    hf3 = hf_p.reshape(BATCH, SEQ, D_HF)
    nqb = SEQ // Q_BLOCK
    T = Q_BLOCK
    qi_w = IDX_HEADS * IDX_DIM
    return pl.pallas_call(
        _attn_p_kernel,
        grid=(BATCH, nqb),
        in_specs=[
            pl.BlockSpec((1, T, D_ATTN), lambda b, i: (b, i, C_Q // D_ATTN)),
            pl.BlockSpec((1, T, qi_w), lambda b, i: (b, i, F_QI // qi_w)),
            pl.BlockSpec((1, T, LANE), lambda b, i: (b, i, F_KI // LANE)),
            pl.BlockSpec((1, SEQ, D_KV), lambda b, i: (b, 0, F_K // D_KV)),
            pl.BlockSpec((1, SEQ, D_KV), lambda b, i: (b, 0, F_V // D_KV)),
            pl.BlockSpec((1, SEQ, LANE), lambda b, i: (b, 0, F_KI // LANE)),
            pl.BlockSpec((N_HEADS, CHUNK, T), lambda b, i: (0, 0, 0)),
            pl.BlockSpec((MXU_WIDTH, MXU_WIDTH), lambda b, i: (0, 0)),
        ],
        out_specs=pl.BlockSpec((1, T, D_ATTN), lambda b, i: (b, i, 0)),
        out_shape=jax.ShapeDtypeStruct((BATCH, SEQ, D_ATTN), F32),
        scratch_shapes=[
            pltpu.VMEM((PAD_KEYS + SEQ, D_KV), BF16),
            pltpu.VMEM((N_KV_HEADS, VT_ROWS, PAD_KEYS + SEQ), BF16),
            pltpu.VMEM((PAD_KEYS + SEQ, IDX_DIM), BF16),
            pltpu.VMEM((N_HEADS, T, HEAD_DIM), BF16),
            pltpu.VMEM((IDX_HEADS, T, IDX_DIM), BF16),
            pltpu.VMEM((PAD_KEYS + SEQ, T), F32),
            pltpu.VMEM((PAD_KEYS + SEQ, T), F32),
            pltpu.VMEM((PAD_KEYS + SEQ, N_HEADS * T), F32),
        ],
        compiler_params=_params(("parallel", "arbitrary")),
        name="attn_prompt",
    )(hb3, hf3, hf3, hf3, hf3, hf3, bias_p, jnp.tri(MXU_WIDTH, dtype=BF16))


def _score_s_kernel(pt_ref, qi_ref, kin_ref, *rest):
    npg = SCORE_ROWS * N_PAGES
    kip = rest[0:npg]
    o_ref = rest[npg]
    kinp_ref = rest[npg + 1]
    del pt_ref
    R = DEC_SEQ
    qrow = lax.broadcasted_iota(jnp.int32, (R, LANE), 0)
    lane = lax.broadcasted_iota(jnp.int32, (R, LANE), 1)
    kinp_ref[...] = jnp.zeros(kinp_ref.shape, BF16)
    for r in range(SCORE_ROWS):
        kinp_ref[r, 0:2 * R] = jnp.concatenate(
            [kin_ref[r][:, :IDX_DIM], jnp.zeros((R, IDX_DIM), F32)], 0).astype(BF16)
    for r in range(SCORE_ROWS):
        qi = jnp.concatenate([qi_ref[r][:, h * IDX_DIM:(h + 1) * IDX_DIM]
                              for h in range(IDX_HEADS)], axis=0).astype(BF16)
        w = kin_ref[r][:, W_OFF:W_OFF + IDX_HEADS] * SCORE_SCALE
        wb = jnp.broadcast_to(jnp.concatenate([w[:, h:h + 1] for h in range(IDX_HEADS)], axis=0),
                              (IDX_HEADS * R, LANE))
        for t in range(N_PAGES + 1):
            if t < N_PAGES:
                d = jnp.dot(qi, kip[r * N_PAGES + t][0].astype(BF16), preferred_element_type=F32)
            else:
                d = lax.dot_general(qi, kinp_ref[r], (((1,), (1,)), ((), ())),
                                    preferred_element_type=F32)
            e = (jnp.maximum(d, 0.0) * wb).reshape(IDX_HEADS, R, LANE)
            s = e[0]
            for h in range(1, IDX_HEADS):
                s = s + e[h]
            if t == N_PAGES:
                s = jnp.where(lane <= qrow, s, NEG_INF)
            o_ref[r, :, t * LANE:(t + 1) * LANE] = s


def _score_sample(hf_s, page_table, cache_kidx_t):
    R = DEC_SEQ
    G = SCORE_ROWS
    h3 = hf_s.reshape(DEC_BATCH, R, D_HF)
    qi_w = IDX_HEADS * IDX_DIM
    in_specs = [
        pl.BlockSpec((G, R, qi_w), lambda b, pt: (b, 0, F_QI // qi_w)),
        pl.BlockSpec((G, R, LANE), lambda b, pt: (b, 0, F_KI // LANE)),
    ]
    in_specs += [pl.BlockSpec((1, IDX_DIM, PAGE_SIZE),
                              lambda b, pt, r=r, p=p: (pt[b * G + r, p], 0, 0))
                 for r in range(G) for p in range(N_PAGES)]
    grid_spec = pltpu.PrefetchScalarGridSpec(
        num_scalar_prefetch=1,
        grid=(DEC_BATCH // G,),
        in_specs=in_specs,
        out_specs=pl.BlockSpec((G, R, L_SAMPLE), lambda b, pt: (b, 0, 0)),
        scratch_shapes=[pltpu.VMEM((G, PAGE_SIZE, IDX_DIM), BF16)],
    )
    return pl.pallas_call(
        _score_s_kernel,
        grid_spec=grid_spec,
        out_shape=jax.ShapeDtypeStruct((DEC_BATCH, R, L_SAMPLE), F32),
        compiler_params=_params(("arbitrary",)),
        name="score_sample",
    )(page_table, h3, h3, *([cache_kidx_t] * (G * N_PAGES)))


def _select_s_kernel(s_ref, tri_ref, o_ref, st_ref):
    nt = N_PAGES + 1
    parts = [slice(j * LANE, (j + 1) * LANE) for j in range(nt)]
    for cols in parts:
        st_ref[cols, :] = s_ref[:, cols].T

    def tile_fn(j):
        return st_ref[parts[j], :].reshape(SUBTILES, SUBLANE, LANE)

    thr, need = _select_threshold(tile_fn, nt)
    masks, _ = _selection_masks([st_ref[rows, :] for rows in parts], thr, need,
                                jnp.zeros((1, LANE), F32), tri_ref[...])
    for cols, mask in zip(parts, masks):
        o_ref[:, cols] = mask.T


def _select_sample(scores):
    n = scores.shape[0]
    return pl.pallas_call(
        _select_s_kernel,
        grid=(n // LANE,),
        in_specs=[pl.BlockSpec((LANE, L_SAMPLE), lambda c: (c, 0)),
                  pl.BlockSpec((LANE, LANE), lambda c: (0, 0))],
        out_specs=pl.BlockSpec((LANE, L_SAMPLE), lambda c: (c, 0)),
        out_shape=jax.ShapeDtypeStruct((n, L_SAMPLE), F32),
        scratch_shapes=[pltpu.VMEM((L_SAMPLE, LANE), F32)],
        compiler_params=_params(("parallel",)),
        name="select_sample",
    )(scores, jnp.tri(LANE, dtype=BF16))


def _attn_s_kernel(pt_ref, q_ref, kn_ref, vn_ref, mask_ref, bias_ref, *rest):
    npg = ATTN_ROWS * N_PAGES
    kp = rest[0:npg]
    vp = rest[npg:2 * npg]
    o_ref = rest[2 * npg]
    knp_ref, vnp_ref, logit_ref = rest[2 * npg + 1:]
    del pt_ref
    R = DEC_SEQ
    NT = N_PAGES + 1
    GR = GROUP * R

    knp_ref[...] = jnp.zeros(knp_ref.shape, BF16)
    vnp_ref[...] = jnp.zeros(vnp_ref.shape, BF16)
    for r in range(ATTN_ROWS):
        knp_ref[r, 0:2 * R] = jnp.concatenate([kn_ref[r], jnp.zeros((R, D_KV), F32)], 0).astype(BF16)
        vnp_ref[r, 0:2 * R] = jnp.concatenate([vn_ref[r], jnp.zeros((R, D_KV), F32)], 0).astype(BF16)

    def page_head(refs, pad_ref, r, t, g):
        if t < N_PAGES:
            return refs[r * N_PAGES + t][pl.ds(g, PAGE_SIZE, stride=N_KV_HEADS), :].astype(BF16)
        return pad_ref[r, :, g * HEAD_DIM:(g + 1) * HEAD_DIM]

    for r in range(ATTN_ROWS):
        qf = q_ref[r].astype(F32) * QK_SCALE
        q = jnp.concatenate([qf[:, h * HEAD_DIM:(h + 1) * HEAD_DIM] for h in range(N_HEADS)],
                            axis=0).astype(BF16)
        for t in range(NT):
            mb = mask_ref[r, :, t * LANE:(t + 1) * LANE]
            for g in range(N_KV_HEADS):
                lg = lax.dot_general(q[g * GR:(g + 1) * GR], page_head(kp, knp_ref, r, t, g),
                                     (((1,), (1,)), ((), ())), preferred_element_type=F32)
                lg = lg.reshape(GROUP, R, LANE) + mb[None]
                if t >= N_PAGES - 1:
                    off = (t - (N_PAGES - 1)) * LANE
                    lg = lg + bias_ref[g * GROUP:(g + 1) * GROUP, :, off:off + LANE]
                logit_ref[r, g * GR:(g + 1) * GR, t * LANE:(t + 1) * LANE] = lg.reshape(GR, LANE)

    for r in range(ATTN_ROWS):
        logits = logit_ref[r]
        m = jnp.max(logits, axis=1, keepdims=True)
        p = jnp.exp(logits - m)
        inv = 1.0 / jnp.sum(p, axis=1, keepdims=True)
        pb = p.astype(BF16)
        outs = [jnp.zeros((GR, HEAD_DIM), F32) for _ in range(N_KV_HEADS)]
        for t in range(NT):
            for g in range(N_KV_HEADS):
                outs[g] = outs[g] + jnp.dot(pb[g * GR:(g + 1) * GR, t * LANE:(t + 1) * LANE],
                                            page_head(vp, vnp_ref, r, t, g),
                                            preferred_element_type=F32)
        for g in range(N_KV_HEADS):
            o = outs[g] * inv[g * GR:(g + 1) * GR]
            for hq in range(GROUP):
                h = g * GROUP + hq
                o_ref[r, :, h * HEAD_DIM:(h + 1) * HEAD_DIM] = o[hq * R:(hq + 1) * R]


def _attn_sample(hb_s, hf_s, page_table, cache_k, cache_v, mask, bias_s):
    R = DEC_SEQ
    G = ATTN_ROWS
    h3 = hf_s.reshape(DEC_BATCH, R, D_HF)
    hb3 = hb_s.reshape(DEC_BATCH, R, D_HB)
    rows_per_page = PAGE_SIZE * N_KV_HEADS
    ck = cache_k.reshape(-1, HEAD_DIM)
    cv = cache_v.reshape(-1, HEAD_DIM)

    in_specs = [
        pl.BlockSpec((G, R, D_ATTN), lambda b, pt: (b, 0, C_Q // D_ATTN)),
        pl.BlockSpec((G, R, D_KV), lambda b, pt: (b, 0, F_K // D_KV)),
        pl.BlockSpec((G, R, D_KV), lambda b, pt: (b, 0, F_V // D_KV)),
        pl.BlockSpec((G, R, L_SAMPLE), lambda b, pt: (b, 0, 0)),
        pl.BlockSpec((N_HEADS, R, 2 * LANE), lambda b, pt: (0, 0, 0)),
    ]
    pages = [pl.BlockSpec((rows_per_page, HEAD_DIM), lambda b, pt, r=r, p=p: (pt[b * G + r, p], 0))
             for r in range(G) for p in range(N_PAGES)]
    in_specs += pages + pages
    grid_spec = pltpu.PrefetchScalarGridSpec(
        num_scalar_prefetch=1,
        grid=(DEC_BATCH // G,),
        in_specs=in_specs,
        out_specs=pl.BlockSpec((G, R, D_ATTN), lambda b, pt: (b, 0, 0)),
        scratch_shapes=[
            pltpu.VMEM((G, PAGE_SIZE, D_KV), BF16),
            pltpu.VMEM((G, PAGE_SIZE, D_KV), BF16),
            pltpu.VMEM((G, N_HEADS * R, L_SAMPLE), F32),
        ],
    )
    return pl.pallas_call(
        _attn_s_kernel,
        grid_spec=grid_spec,
        out_shape=jax.ShapeDtypeStruct((DEC_BATCH, R, D_ATTN), F32),
        compiler_params=_params(("arbitrary",)),
        name="attn_sample",
    )(page_table, hb3, h3, h3, mask, bias_s, *([ck] * (G * N_PAGES)), *([cv] * (G * N_PAGES)))


def _conv_p_kernel(val_ref, gate_ref, cw_ref, cb_ref, dw_ref, ut_ref, pad_ref):
    pad_ref[0:PAD_ROWS] = jnp.zeros((PAD_ROWS, LANE), F32)
    pad_ref[PAD_ROWS:] = val_ref[0].astype(F32) * jax.nn.sigmoid(gate_ref[0].astype(F32))
    ut_ref[0] = pad_ref[SEQ:SEQ + PAD_ROWS]
    cw = cw_ref[...]
    cb = cb_ref[...]
    first = PAD_ROWS - (CONV_WIDTH - 1)
    for c in range(SEQ // CONV_CHUNK):
        base = c * CONV_CHUNK
        acc = jnp.broadcast_to(cb, (CONV_CHUNK, LANE))
        for r in range(SUBLANE):
            taps = [w for w in range(CONV_WIDTH) if (first + w) % SUBLANE == r]
            span = max(first + w - r for w in taps) + CONV_CHUNK
            win = pad_ref[base + r:base + r + span]
            for w in taps:
                a = first + w - r
                acc = acc + win[a:a + CONV_CHUNK] * cw[w:w + 1]
        dw_ref[0, base:base + CONV_CHUNK] = acc


def _conv_prompt(hb_p, cw_pad, cb):
    h3 = hb_p.reshape(BATCH, SEQ, D_HB)
    nc = D_CONV // LANE
    return pl.pallas_call(
        _conv_p_kernel,
        grid=(BATCH, nc),
        in_specs=[pl.BlockSpec((1, SEQ, LANE), lambda b, c: (b, 0, C_GV // LANE + c)),
                  pl.BlockSpec((1, SEQ, LANE), lambda b, c: (b, 0, C_GG // LANE + c)),
                  pl.BlockSpec((PAD_ROWS, LANE), lambda b, c: (0, c)),
                  pl.BlockSpec((1, LANE), lambda b, c: (0, c))],
        out_specs=[pl.BlockSpec((1, SEQ, LANE), lambda b, c: (b, 0, c)),
                   pl.BlockSpec((1, PAD_ROWS, LANE), lambda b, c: (b, 0, c))],
        out_shape=[jax.ShapeDtypeStruct((BATCH, SEQ, D_CONV), F32),
                   jax.ShapeDtypeStruct((BATCH, PAD_ROWS, D_CONV), F32)],
        scratch_shapes=[pltpu.VMEM((PAD_ROWS + SEQ, LANE), F32)],
        compiler_params=_params(("parallel", "parallel")),
        name="conv_prompt",
    )(h3, h3, cw_pad, cb)


def _conv_s_kernel(val_ref, gate_ref, st_ref, cw_ref, cb_ref, dw_ref, ns_ref, glu_ref):
    R = DEC_SEQ
    H = CONV_WIDTH - 1
    cw = cw_ref[...]
    cb = jnp.broadcast_to(cb_ref[...], (DEC_BATCH, LANE))
    glu_ref[...] = val_ref[...].astype(F32) * jax.nn.sigmoid(gate_ref[...].astype(F32))
    u = []
    for q in range(R):
        u.append(glu_ref[pl.ds(q, DEC_BATCH, stride=R), :])

    def row(r):
        return st_ref[r] if r < H else u[r - H]

    for q in range(R):
        acc = cb
        for w in range(CONV_WIDTH):
            acc = acc + row(q + w) * cw[w:w + 1]
        dw_ref[pl.ds(q, DEC_BATCH, stride=R), :] = acc
    for r in range(H):
        ns_ref[r] = row(r + R)


def _conv_sample(hb_s, state_t, cw_pad, cb):
    n_s = DEC_BATCH * DEC_SEQ
    H = CONV_WIDTH - 1
    nc = D_CONV // LANE
    return pl.pallas_call(
        _conv_s_kernel,
        grid=(nc,),
        in_specs=[pl.BlockSpec((n_s, LANE), lambda c: (0, C_GV // LANE + c)),
                  pl.BlockSpec((n_s, LANE), lambda c: (0, C_GG // LANE + c)),
                  pl.BlockSpec((H, DEC_BATCH, LANE), lambda c: (0, 0, c)),
                  pl.BlockSpec((PAD_ROWS, LANE), lambda c: (0, c)),
                  pl.BlockSpec((1, LANE), lambda c: (0, c))],
        out_specs=[pl.BlockSpec((n_s, LANE), lambda c: (0, c)),
                   pl.BlockSpec((H, DEC_BATCH, LANE), lambda c: (0, 0, c))],
        out_shape=[jax.ShapeDtypeStruct((n_s, D_CONV), F32),
                   jax.ShapeDtypeStruct((H, DEC_BATCH, D_CONV), F32)],
        scratch_shapes=[pltpu.VMEM((n_s, LANE), F32)],
        compiler_params=_params(("parallel",)),
        name="conv_sample",
    )(hb_s, hb_s, state_t, cw_pad, cb)


def _tail_kernel(attn_ref, za_ref, dw_ref, zc_ref, ga_ref, gc_ref,
                 wua_ref, wpw_ref, wuc_ref, ng_ref, nb_ref, bpw_ref, m_ref):
    a = attn_ref[...] * jax.nn.silu(za_ref[...].astype(F32))
    branch_attn = jnp.dot(a.astype(BF16), wua_ref[...], preferred_element_type=F32)

    dw = dw_ref[...]
    mu = jnp.mean(dw, axis=-1, keepdims=True)
    var = jnp.mean(jnp.square(dw - mu), axis=-1, keepdims=True)
    ln = (dw - mu) * lax.rsqrt(var + EPS) * ng_ref[...] + nb_ref[...]
    conv_out = jnp.dot(jax.nn.silu(ln).astype(BF16), wpw_ref[...],
                       preferred_element_type=F32) + bpw_ref[...]
    c = conv_out * jax.nn.silu(zc_ref[...].astype(F32))
    branch_conv = jnp.dot(c.astype(BF16), wuc_ref[...], preferred_element_type=F32)

    merged = (jax.nn.sigmoid(ga_ref[...].astype(F32)) * branch_attn
              + jax.nn.sigmoid(gc_ref[...].astype(F32)) * branch_conv)
    m_ref[...] = merged.astype(BF16)


def _out_kernel(m_ref, x_ref, wo_ref, fg_ref, y_ref):
    y = x_ref[...] + jnp.dot(m_ref[...], wo_ref[...], preferred_element_type=F32)
    ms = jnp.mean(y * y, axis=-1, keepdims=True)
    y_ref[...] = y * lax.rsqrt(ms + EPS) * fg_ref[...]


def _tail(h, attn, dw, x2d, weights, tm):
    n = x2d.shape[0]
    wua, wpw, wuc, wo, ng, nb, bpw, fg = weights

    def const(shape):
        return pl.BlockSpec(shape, lambda i: (0, 0), pipeline_mode=pl.Buffered(1))

    def cols(width, offset):
        return pl.BlockSpec((tm, width), lambda i: (i, offset // width))

    merged = pl.pallas_call(
        _tail_kernel,
        grid=(n // tm,),
        in_specs=[cols(D_ATTN, 0), cols(D_ATTN, C_ZA), cols(D_CONV, 0), cols(D_CONV, C_ZC),
                  cols(D_MODEL, C_GA), cols(D_MODEL, C_GC),
                  const((D_ATTN, D_MODEL)), const((D_CONV, D_CONV)),
                  const((D_CONV, D_MODEL)),
                  const((1, D_CONV)), const((1, D_CONV)), const((1, D_CONV))],
        out_specs=pl.BlockSpec((tm, D_MODEL), lambda i: (i, 0)),
        out_shape=jax.ShapeDtypeStruct((n, D_MODEL), BF16),
        compiler_params=_params(("parallel",)),
        name="tail_merge",
    )(attn, h, dw, h, h, h, wua, wpw, wuc, ng, nb, bpw)
    return pl.pallas_call(
        _out_kernel,
        grid=(n // tm,),
        in_specs=[cols(D_MODEL, 0), cols(D_MODEL, 0), const((D_MODEL, D_MODEL)),
                  const((1, D_MODEL))],
        out_specs=pl.BlockSpec((tm, D_MODEL), lambda i: (i, 0)),
        out_shape=jax.ShapeDtypeStruct((n, D_MODEL), F32),
        compiler_params=_params(("parallel",)),
        name="tail_out",
    )(merged, x2d, wo, fg)


def _reorder_kernel(off_ref, w_ref, o_ref):
    del off_ref
    o_ref[...] = w_ref[...].astype(BF16)


def _reorder_w_in(w_in):
    w_t = w_in.T
    src = []
    for first, dest, width in (('gate_attn', C_GA, 2 * D_MODEL), ('q', C_Q, D_ATTN),
                               ('z_attn', C_ZA, D_ATTN), ('glu_val', C_GV, 3 * D_CONV),
                               ('k', C_K, 2 * D_KV), ('q_idx', C_QI, D_H - C_QI)):
        assert dest == len(src) * REORDER_ROWS
        for r in range(0, width, REORDER_ROWS):
            src.append(min(_SRC[first][0] + r, D_IN - REORDER_ROWS) // SUBLANE)
    grid_spec = pltpu.PrefetchScalarGridSpec(
        num_scalar_prefetch=1,
        grid=(len(src),),
        in_specs=[pl.BlockSpec((pl.Element(REORDER_ROWS), pl.Element(D_MODEL)),
                               lambda d, off: (off[d] * SUBLANE, 0))],
        out_specs=pl.BlockSpec((REORDER_ROWS, D_MODEL), lambda d, off: (d, 0)),
    )
    return pl.pallas_call(
        _reorder_kernel,
        grid_spec=grid_spec,
        out_shape=jax.ShapeDtypeStruct((D_H, D_MODEL), BF16),
        compiler_params=_params(("arbitrary",)),
        name="reorder_w_in",
    )(jnp.asarray(src, jnp.int32), w_t)


def kernel(x_prompt, x_sample, cache_k, cache_v, cache_kidx, state_conv, page_table,
           ln_g, w_in, conv_w, conv_b, conv_norm_g, conv_norm_b, w_pw, b_pw,
           w_up_attn, w_up_conv, w_out, rel_bias, final_g):
    w_all = _reorder_w_in(w_in[0])
    g_in = ln_g[0].reshape(1, D_MODEL)
    xp = x_prompt.reshape(BATCH * SEQ, D_MODEL)
    xs = x_sample.reshape(DEC_BATCH * DEC_SEQ, D_MODEL)
    hb_p, hf_p, k_p, v_p, kit_p = _proj(xp, g_in, w_all, SEQ)
    hb_s, hf_s, k_s, v_s, _ = _proj(xs, g_in, w_all, DEC_BATCH * DEC_SEQ)

    bias_p, bias_s = _bias_tables(rel_bias)

    attn_p = _attn_prompt(hb_p, hf_p, bias_p).reshape(BATCH * SEQ, D_ATTN)

    n_s = DEC_BATCH * DEC_SEQ
    scores = _score_sample(hf_s, page_table, cache_kidx[0].transpose(0, 2, 1))
    mask = _select_sample(scores.reshape(n_s, L_SAMPLE)).reshape(DEC_BATCH, DEC_SEQ, L_SAMPLE)
    attn_s = _attn_sample(hb_s, hf_s, page_table, cache_k[0], cache_v[0], mask, bias_s)
    attn_s = attn_s.reshape(n_s, D_ATTN)

    cw_pad = jnp.concatenate([conv_w[0], jnp.zeros((PAD_ROWS - CONV_WIDTH, D_CONV), F32)], 0)
    cb = conv_b[0].reshape(1, D_CONV)
    dw_p, u_tail = _conv_prompt(hb_p, cw_pad, cb)
    dw_s, state_new = _conv_sample(hb_s, state_conv[0].transpose(1, 0, 2), cw_pad, cb)

    wua = w_up_attn[0].astype(BF16)
    wpw = w_pw[0].astype(BF16)
    wuc = w_up_conv[0].astype(BF16)
    wo = w_out[0].astype(BF16)
    ng = conv_norm_g[0].reshape(1, D_CONV)
    nb = conv_norm_b[0].reshape(1, D_CONV)
    bpw = b_pw[0].reshape(1, D_CONV)
    fg = final_g.reshape(1, D_MODEL)
    weights = (wua, wpw, wuc, wo, ng, nb, bpw, fg)
    y_p = _tail(hb_p, attn_p, dw_p.reshape(BATCH * SEQ, D_CONV), xp, weights, TAIL_TM)
    y_s = _tail(hb_s, attn_s, dw_s, xs, weights, TAIL_TM)

    tail_rows = CONV_WIDTH - 1
    return (
        y_p.reshape(BATCH, SEQ, D_MODEL),
        y_s.reshape(DEC_BATCH, DEC_SEQ, D_MODEL),
        k_p.reshape(1, BATCH, SEQ, N_KV_HEADS, HEAD_DIM),
        v_p.reshape(1, BATCH, SEQ, N_KV_HEADS, HEAD_DIM),
        kit_p.transpose(0, 2, 1).reshape(1, BATCH, SEQ, IDX_DIM),
        u_tail[:, PAD_ROWS - tail_rows:].reshape(1, BATCH, tail_rows, D_CONV),
        k_s.reshape(1, DEC_BATCH, DEC_SEQ, N_KV_HEADS, HEAD_DIM),
        v_s.reshape(1, DEC_BATCH, DEC_SEQ, N_KV_HEADS, HEAD_DIM),
        hf_s[:, F_KI:F_KI + IDX_DIM].reshape(1, DEC_BATCH, DEC_SEQ, IDX_DIM),
        state_new.transpose(1, 0, 2).reshape(1, DEC_BATCH, tail_rows, D_CONV),
    )
```
